```python
import jax
import jax.numpy as jnp
from jax import lax
import numpy as np

D_MODEL = 1024
BATCH = 16
SEQ = 256
DEPTH = 2
DEC_BATCH = 2
DEC_SEQ = 1024
PAST_LEN = 256

GRID_W = 64
N_EVEN = (DEPTH + 1) // 2
N_ODD = DEPTH // 2
HEAD_DIM = 64
NA_WIDTH = D_MODEL // 2
NA_HEADS = NA_WIDTH // HEAD_DIM
WIN_H = 8
WIN_W = 16
LRU_WIDTH = D_MODEL // 2
LRU_BLOCKS = 8
LRU_BLOCK = LRU_WIDTH // LRU_BLOCKS
LRU_C = 8.0
CONV_W = 4
FOURIER_GROUPS = 4
D_FF = 2816
N_MOD = 9
IN_WIDTH = 3 * NA_WIDTH + 2 * LRU_WIDTH
CTX_Q_BLOCK = 128
EPS = 1e-6

kernel_name = 'hybrid_natten_rglru_fnet_diffusion_step'


def rms_norm(x, g):
    xf = x.astype(jnp.float32)
    y = xf * lax.rsqrt(jnp.mean(xf * xf, axis=-1, keepdims=True) + EPS)
    return (y * g.astype(jnp.float32)).astype(x.dtype)


def modulate(h, shift, scale):
    return h * (1 + scale[:, None, :]) + shift[:, None, :]


def swiglu(h, w_gate, w_up, w_down):
    return (jax.nn.silu(h @ w_gate) * (h @ w_up)) @ w_down


def depthwise_conv_centred(x, w, b):
    t = x.shape[1]
    left = (CONV_W - 1) // 2
    xp = jnp.pad(x, ((0, 0), (left, CONV_W - 1 - left), (0, 0)))
    out = b
    for j in range(CONV_W):
        out = out + xp[:, j:j + t] * w[j]
    return out


def block_diag_linear(x, w, b):
    bsz, t, _ = x.shape
    y = jnp.einsum('btni,nij->btnj', x.reshape(bsz, t, LRU_BLOCKS, LRU_BLOCK), w)
    return y.reshape(bsz, t, LRU_WIDTH) + b


def rglru_coeffs(xc, w_r, b_r, w_i, b_i, lam):
    r = jax.nn.sigmoid(block_diag_linear(xc, w_r, b_r).astype(jnp.float32))
    i = jax.nn.sigmoid(block_diag_linear(xc, w_i, b_i).astype(jnp.float32))
    log_a = LRU_C * r * jax.nn.log_sigmoid(lam.astype(jnp.float32))
    a = jnp.exp(log_a)
    u = jnp.sqrt(-jnp.expm1(2.0 * log_a)) * (i * xc.astype(jnp.float32))
    return a, u


def linear_scan(a, u, h0, reverse):
    def step(h, au):
        h = au[0] * h + au[1]
        return h, h
    h_last, hs = lax.scan(step, h0, (jnp.swapaxes(a, 0, 1), jnp.swapaxes(u, 0, 1)), reverse=reverse)
    return jnp.swapaxes(hs, 0, 1), h_last


def rglru_bidir(xb, gb, conv_w, conv_b, w_r, b_r, w_i, b_i, lam, h0_fwd, h0_bwd):
    xc = depthwise_conv_centred(xb, conv_w, conv_b)
    a_f, u_f = rglru_coeffs(xc, w_r[0], b_r[0], w_i[0], b_i[0], lam[0])
    a_b, u_b = rglru_coeffs(xc, w_r[1], b_r[1], w_i[1], b_i[1], lam[1])
    h_f, hl_f = linear_scan(a_f, u_f, h0_fwd.astype(jnp.float32), False)
    h_b, hl_b = linear_scan(a_b, u_b, h0_bwd.astype(jnp.float32), True)
    y = (h_f + h_b).astype(xb.dtype) * jax.nn.gelu(gb)
    return y, hl_f, hl_b


def context_attention(q, k, v):
    bsz, s, h, dh = q.shape
    nq = s // CTX_Q_BLOCK
    qb = jnp.moveaxis(q.reshape(bsz, nq, CTX_Q_BLOCK, h, dh), 1, 0)

    def one_block(qi):
        sc = jnp.einsum('bqhd,bkhd->bhqk', qi, k).astype(jnp.float32)
        p = jax.nn.softmax(sc, axis=-1).astype(v.dtype)
        return jnp.einsum('bhqk,bkhd->bqhd', p, v)

    o = lax.map(one_block, qb)
    return jnp.moveaxis(o, 0, 1).reshape(bsz, s, h, dh)


def neighbourhood_attention(q, k, v, k_ctx, v_ctx, rpb):
    bsz, t, h, dh = q.shape
    rows = t // GRID_W
    kh = min(WIN_H, rows)
    r = jnp.arange(rows)
    row_start = jnp.clip(r - kh // 2, 0, rows - kh)
    key_rows = row_start[:, None] + jnp.arange(kh)[None, :]
    cq = jnp.arange(GRID_W)
    col_start = jnp.clip(cq - WIN_W // 2, 0, GRID_W - WIN_W)
    col_in = (cq[None, :] >= col_start[:, None]) & (cq[None, :] < col_start[:, None] + WIN_W)
    dr_idx = key_rows - r[:, None] + (WIN_H - 1)
    dc_idx = jnp.clip(cq[None, :] - cq[:, None] + (WIN_W - 1), 0, 2 * WIN_W - 2)
    bias = rpb[:, dr_idx[:, None, :, None], dc_idx[None, :, None, :]].astype(jnp.float32)
    bias = jnp.where(col_in[None, None, :, None, :], bias, -jnp.inf)

    qg = q.reshape(bsz, rows, GRID_W, h, dh)
    kg = k.reshape(bsz, rows, GRID_W, h, dh)[:, key_rows]
    vg = v.reshape(bsz, rows, GRID_W, h, dh)[:, key_rows]
    s_loc = jnp.einsum('brqhd,brkchd->bhrqkc', qg, kg).astype(jnp.float32) + bias[None]
    s_ctx = jnp.einsum('brqhd,bphd->bhrqp', qg, k_ctx).astype(jnp.float32)
    n_loc = kh * GRID_W
    s_all = jnp.concatenate([s_loc.reshape(bsz, h, rows, GRID_W, n_loc), s_ctx], axis=-1)
    p = jax.nn.softmax(s_all, axis=-1).astype(v.dtype)
    p_loc = p[..., :n_loc].reshape(bsz, h, rows, GRID_W, kh, GRID_W)
    p_ctx = p[..., n_loc:]
    o = (jnp.einsum('bhrqkc,brkchd->brqhd', p_loc, vg)
         + jnp.einsum('bhrqp,bphd->brqhd', p_ctx, v_ctx))
    return o.reshape(bsz, t, h, dh)


def mix_ab(h, w_in, q_g, k_g, rpb, conv_w, conv_b, w_r, b_r, w_i, b_i, lam, w_out, past):
    bsz, t, _ = h.shape
    proj = h @ w_in
    q, k, v, xb, gb = jnp.split(
        proj, [NA_WIDTH, 2 * NA_WIDTH, 3 * NA_WIDTH, 3 * NA_WIDTH + LRU_WIDTH], axis=-1)
    q = rms_norm(q.reshape(bsz, t, NA_HEADS, HEAD_DIM), q_g) * (HEAD_DIM ** -0.5)
    k = rms_norm(k.reshape(bsz, t, NA_HEADS, HEAD_DIM), k_g)
    v = v.reshape(bsz, t, NA_HEADS, HEAD_DIM)
    if past is None:
        o = context_attention(q, k, v)
        zeros = jnp.zeros((bsz, LRU_WIDTH), jnp.float32)
        y_b, hl_f, hl_b = rglru_bidir(xb, gb, conv_w, conv_b, w_r, b_r, w_i, b_i, lam, zeros, zeros)
        new = (k, v, hl_f.astype(h.dtype), hl_b.astype(h.dtype))
    else:
        k_ctx, v_ctx, h0_f, h0_b = past
        o = neighbourhood_attention(q, k, v, k_ctx, v_ctx, rpb)
        y_b, _, _ = rglru_bidir(xb, gb, conv_w, conv_b, w_r, b_r, w_i, b_i, lam, h0_f, h0_b)
        new = None
    y = jnp.concatenate([o.reshape(bsz, t, NA_WIDTH), y_b], axis=-1) @ w_out
    return y, new


def fourier_mix(h, w_out):
    bsz, t, _ = h.shape
    hg = h.astype(jnp.float32).reshape(bsz, t, FOURIER_GROUPS, D_MODEL // FOURIER_GROUPS)
    f = jnp.fft.fft2(hg, axes=(1, 3), norm='ortho').real
    return f.reshape(bsz, t, D_MODEL).astype(h.dtype) @ w_out


def trunk(x, cond, prm, past):
    s_cond = jax.nn.silu(cond)
    new_k, new_v, new_hf, new_hb = [], [], [], []
    for l in range(DEPTH):
        mod = (s_cond @ prm['w_ada'][l] + prm['b_ada'][l]).reshape(-1, N_MOD, D_MODEL)
        g = prm['norm_g'][l]
        hh = modulate(rms_norm(x, g[0]), mod[:, 0], mod[:, 1])
        x = x + 0.5 * mod[:, 2][:, None, :] * swiglu(hh, prm['ffn1_gate'][l], prm['ffn1_up'][l], prm['ffn1_down'][l])
        hh = modulate(rms_norm(x, g[1]), mod[:, 3], mod[:, 4])
        if l % 2 == 0:
            e = l // 2
            layer_past = None if past is None else (past[0][:, e], past[1][:, e], past[2][:, e], past[3][:, e])
            y, new = mix_ab(hh, prm['w_in'][e], prm['q_norm_g'][e], prm['k_norm_g'][e], prm['rpb'][e],
                            prm['conv_w'][e], prm['conv_b'][e], prm['lru_w_r'][e], prm['lru_b_r'][e],
                            prm['lru_w_i'][e], prm['lru_b_i'][e], prm['lru_lambda'][e], prm['w_out_ab'][e],
                            layer_past)
            if new is not None:
                new_k.append(new[0])
                new_v.append(new[1])
                new_hf.append(new[2])
                new_hb.append(new[3])
        else:
            y = fourier_mix(hh, prm['w_out_c'][l // 2])
        x = x + mod[:, 5][:, None, :] * y
        hh = modulate(rms_norm(x, g[2]), mod[:, 6], mod[:, 7])
        x = x + 0.5 * mod[:, 8][:, None, :] * swiglu(hh, prm['ffn2_gate'][l], prm['ffn2_up'][l], prm['ffn2_down'][l])
    if past is None:
        states = (jnp.stack(new_k, axis=1), jnp.stack(new_v, axis=1),
                  jnp.stack(new_hf, axis=1), jnp.stack(new_hb, axis=1))
    else:
        states = None
    return x, states


def setup_inputs(seed: int = 0) -> dict:
    key = jax.random.key(seed)
    ks = jax.random.split(key, 32)

    def nrm(k, shape, s):
        return jax.random.normal(k, shape, jnp.float32) * s

    a0 = jax.random.uniform(ks[28], (N_EVEN, 2, LRU_WIDTH), jnp.float32, 0.9, 0.999)
    sig = a0 ** (1.0 / LRU_C)
    lru_lambda = jnp.log(sig) - jnp.log1p(-sig)
    return {
        'x_prompt': nrm(ks[0], (BATCH, SEQ, D_MODEL), 1.0),
        'x_sample': nrm(ks[1], (DEC_BATCH, DEC_SEQ, D_MODEL), 1.0),
        'cache_k': nrm(ks[2], (DEC_BATCH, N_EVEN, PAST_LEN, NA_HEADS, HEAD_DIM), 1.0),
        'cache_v': nrm(ks[3], (DEC_BATCH, N_EVEN, PAST_LEN, NA_HEADS, HEAD_DIM), 1.0),
        'state_lru_fwd': nrm(ks[4], (DEC_BATCH, N_EVEN, LRU_WIDTH), 0.5),
        'state_lru_bwd': nrm(ks[5], (DEC_BATCH, N_EVEN, LRU_WIDTH), 0.5),
        'c': nrm(ks[6], (DEC_BATCH, D_MODEL), 1.0),
        'c_ctx': nrm(ks[7], (D_MODEL,), 1.0),
        'w_ada': nrm(ks[8], (DEPTH, D_MODEL, N_MOD * D_MODEL), 0.5 * D_MODEL ** -0.5),
        'b_ada': nrm(ks[9], (DEPTH, N_MOD * D_MODEL), 0.01),
        'norm_g': 1.0 + nrm(ks[10], (DEPTH, 3, D_MODEL), 0.02),
        'ffn1_gate': nrm(ks[11], (DEPTH, D_MODEL, D_FF), D_MODEL ** -0.5),
        'ffn1_up': nrm(ks[12], (DEPTH, D_MODEL, D_FF), D_MODEL ** -0.5),
        'ffn1_down': nrm(ks[13], (DEPTH, D_FF, D_MODEL), D_FF ** -0.5),
        'ffn2_gate': nrm(ks[14], (DEPTH, D_MODEL, D_FF), D_MODEL ** -0.5),
        'ffn2_up': nrm(ks[15], (DEPTH, D_MODEL, D_FF), D_MODEL ** -0.5),
        'ffn2_down': nrm(ks[16], (DEPTH, D_FF, D_MODEL), D_FF ** -0.5),
        'w_in': nrm(ks[17], (N_EVEN, D_MODEL, IN_WIDTH), D_MODEL ** -0.5),
        'q_norm_g': 1.0 + nrm(ks[18], (N_EVEN, HEAD_DIM), 0.02),
        'k_norm_g': 1.0 + nrm(ks[19], (N_EVEN, HEAD_DIM), 0.02),
        'rpb': nrm(ks[20], (N_EVEN, NA_HEADS, 2 * WIN_H - 1, 2 * WIN_W - 1), 0.1),
        'conv_w': nrm(ks[21], (N_EVEN, CONV_W, LRU_WIDTH), CONV_W ** -0.5),
        'conv_b': nrm(ks[22], (N_EVEN, LRU_WIDTH), 0.01),
        'lru_w_r': nrm(ks[23], (N_EVEN, 2, LRU_BLOCKS, LRU_BLOCK, LRU_BLOCK), LRU_BLOCK ** -0.5),
        'lru_b_r': nrm(ks[24], (N_EVEN, 2, LRU_WIDTH), 0.01),
        'lru_w_i': nrm(ks[25], (N_EVEN, 2, LRU_BLOCKS, LRU_BLOCK, LRU_BLOCK), LRU_BLOCK ** -0.5),
        'lru_b_i': nrm(ks[26], (N_EVEN, 2, LRU_WIDTH), 0.01),
        'lru_lambda': lru_lambda,
        'w_out_ab': nrm(ks[29], (N_EVEN, NA_WIDTH + LRU_WIDTH, D_MODEL), (NA_WIDTH + LRU_WIDTH) ** -0.5),
        'w_out_c': nrm(ks[30], (N_ODD, D_MODEL, D_MODEL), D_MODEL ** -0.5),
    }


def reference(x_prompt, x_sample, cache_k, cache_v, state_lru_fwd, state_lru_bwd, c, c_ctx,
              w_ada, b_ada, norm_g, ffn1_gate, ffn1_up, ffn1_down, ffn2_gate, ffn2_up, ffn2_down,
              w_in, q_norm_g, k_norm_g, rpb, conv_w, conv_b, lru_w_r, lru_b_r, lru_w_i, lru_b_i,
              lru_lambda, w_out_ab, w_out_c):
    prm = {
        'w_ada': w_ada, 'b_ada': b_ada, 'norm_g': norm_g,
        'ffn1_gate': ffn1_gate, 'ffn1_up': ffn1_up, 'ffn1_down': ffn1_down,
        'ffn2_gate': ffn2_gate, 'ffn2_up': ffn2_up, 'ffn2_down': ffn2_down,
        'w_in': w_in, 'q_norm_g': q_norm_g, 'k_norm_g': k_norm_g, 'rpb': rpb,
        'conv_w': conv_w, 'conv_b': conv_b, 'lru_w_r': lru_w_r, 'lru_b_r': lru_b_r,
        'lru_w_i': lru_w_i, 'lru_b_i': lru_b_i, 'lru_lambda': lru_lambda,
        'w_out_ab': w_out_ab, 'w_out_c': w_out_c,
    }
    y_prompt, ctx_states = trunk(x_prompt, c_ctx[None, :], prm, None)
    new_cache_k, new_cache_v, new_state_lru_fwd, new_state_lru_bwd = ctx_states
    y_sample, _ = trunk(x_sample, c, prm, (cache_k, cache_v, state_lru_fwd, state_lru_bwd))
    return (y_prompt, y_sample, new_cache_k, new_cache_v, new_state_lru_fwd, new_state_lru_bwd)
```

```python
import functools

import numpy as np
import jax
import jax.numpy as jnp
from jax import lax
from jax.experimental import pallas as pl
from jax.experimental.pallas import tpu as pltpu

F32 = jnp.float32
BF16 = jnp.bfloat16

D_MODEL = 1024
BATCH = 16
SEQ = 256
DEPTH = 2
DEC_BATCH = 2
DEC_SEQ = 1024
PAST_LEN = 256
GRID_W = 64
HEAD_DIM = 64
NA_WIDTH = 512
NA_HEADS = 8
WIN_H = 8
WIN_W = 16
LRU_WIDTH = 512
LRU_BLOCKS = 8
LRU_BLOCK = 64
LRU_C = 8.0
CONV_W = 4
FOURIER_GROUPS = 4
GROUP_W = D_MODEL // FOURIER_GROUPS
D_FF = 2816
N_MOD = 9
IN_WIDTH = 3 * NA_WIDTH + 2 * LRU_WIDTH
EPS = 1e-6

N_CTX_TOK = BATCH * SEQ
N_SMP_TOK = DEC_BATCH * DEC_SEQ
N_TOK = N_CTX_TOK + N_SMP_TOK
MOD_ROWS = 8
ROWS = DEC_SEQ // GRID_W
KH = min(WIN_H, ROWS)
N_LOC = KH * GRID_W

TOKEN_TILE = 512
FF_CHUNKS = 2
SUBLANES = 8
VMEM_LIMIT = 56 * 1024 * 1024


def _cparams(n_axes):
    return pltpu.CompilerParams(
        dimension_semantics=("arbitrary",) * n_axes, vmem_limit_bytes=VMEM_LIMIT)


def _resident(block_shape, index_map):
    return pl.BlockSpec(block_shape, index_map, pipeline_mode=pl.Buffered(1))


def _mod_row_of_tile(i):
    n_ctx_tiles = N_CTX_TOK // TOKEN_TILE
    tiles_per_seq = DEC_SEQ // TOKEN_TILE
    return jnp.where(i < n_ctx_tiles, 0, 1 + (i - n_ctx_tiles) // tiles_per_seq)


def _norm_mod(x, g, shift, scale):
    ms = jnp.mean(x * x, axis=-1, keepdims=True)
    y = x * lax.rsqrt(ms + EPS) * g
    return y * (1.0 + scale) + shift


def _adaln_kernel(c_ref, w_ref, b_ref, o_ref):
    c = c_ref[...]
    s = (c * jax.nn.sigmoid(c)).astype(BF16)
    w = w_ref[0].astype(BF16)
    o_ref[0] = jnp.dot(s, w, preferred_element_type=F32) + b_ref[0]


def _adaln(cond, w_ada, b_ada):
    n = N_MOD * D_MODEL
    tn = n // 4
    return pl.pallas_call(
        _adaln_kernel,
        grid=(DEPTH, n // tn),
        in_specs=[
            pl.BlockSpec((MOD_ROWS, D_MODEL), lambda l, j: (0, 0)),
            pl.BlockSpec((1, D_MODEL, tn), lambda l, j: (l, 0, j)),
            pl.BlockSpec((1, 1, tn), lambda l, j: (l, 0, j)),
        ],
        out_specs=pl.BlockSpec((1, MOD_ROWS, tn), lambda l, j: (l, 0, j)),
        out_shape=jax.ShapeDtypeStruct((DEPTH, MOD_ROWS, n), F32),
        compiler_params=_cparams(2),
        name="adaln",
    )(cond, w_ada, b_ada.reshape(DEPTH, 1, n))


def _ffn_kernel(x_ref, mod_ref, g_ref, wg_ref, wu_ref, wd_ref, o_ref, *, mod_base):
    x = x_ref[...]
    shift = mod_ref[0, mod_base:mod_base + 1, :]
    scale = mod_ref[0, mod_base + 1:mod_base + 2, :]
    gate = mod_ref[0, mod_base + 2:mod_base + 3, :]
    h = _norm_mod(x, g_ref[...], shift, scale).astype(BF16)
    tf = D_FF // FF_CHUNKS
    acc = None
    for j in range(FF_CHUNKS):
        sl = slice(j * tf, (j + 1) * tf)
        a = jnp.dot(h, wg_ref[:, sl], preferred_element_type=F32)
        b = jnp.dot(h, wu_ref[:, sl], preferred_element_type=F32)
        act = (a * jax.nn.sigmoid(a) * b).astype(BF16)
        y = jnp.dot(act, wd_ref[sl, :], preferred_element_type=F32)
        acc = y if acc is None else acc + y
    o_ref[...] = x + 0.5 * gate * acc


def _ffn(x, mod_l, g, wg, wu, wd, layer, mod_base):
    tm = TOKEN_TILE
    return pl.pallas_call(
        functools.partial(_ffn_kernel, mod_base=mod_base),
        grid=(N_TOK // tm,),
        in_specs=[
            pl.BlockSpec((tm, D_MODEL), lambda i: (i, 0)),
            pl.BlockSpec((1, N_MOD, D_MODEL), lambda i: (_mod_row_of_tile(i), 0, 0)),
            pl.BlockSpec((1, D_MODEL), lambda i: (0, 0)),
            _resident((None, D_MODEL, D_FF), lambda i: (layer, 0, 0)),
            _resident((None, D_MODEL, D_FF), lambda i: (layer, 0, 0)),
            _resident((None, D_FF, D_MODEL), lambda i: (layer, 0, 0)),
        ],
        out_specs=pl.BlockSpec((tm, D_MODEL), lambda i: (i, 0)),
        out_shape=jax.ShapeDtypeStruct((N_TOK, D_MODEL), F32),
        compiler_params=_cparams(1),
        name="ffn",
    )(x, mod_l, g.reshape(1, D_MODEL), wg, wu, wd)


def _head_rms_norm(z, g, ones_bd):
    z2 = z * z
    hi = z2.astype(BF16)
    lo = (z2 - hi.astype(F32)).astype(BF16)
    ss = (jnp.dot(hi, ones_bd, preferred_element_type=F32)
          + jnp.dot(lo, ones_bd, preferred_element_type=F32))
    return z * lax.rsqrt(ss * (1.0 / HEAD_DIM) + EPS) * g


def _proj_kernel(x_ref, mod_ref, g_ref, w_ref, qg_ref, kg_ref, ones_ref,
                 q_ref, k_ref, v_ref, xb_ref, gb_ref):
    x = x_ref[...]
    h = _norm_mod(x, g_ref[...], mod_ref[0, 3:4, :], mod_ref[0, 4:5, :]).astype(BF16)
    proj = jnp.dot(h, w_ref[...], preferred_element_type=F32)
    ones_bd = ones_ref[...]
    q = _head_rms_norm(proj[:, :NA_WIDTH], qg_ref[...], ones_bd) * (HEAD_DIM ** -0.5)
    q_ref[...] = q.astype(BF16)
    k_ref[...] = _head_rms_norm(proj[:, NA_WIDTH:2 * NA_WIDTH], kg_ref[...], ones_bd)
    v_ref[...] = proj[:, 2 * NA_WIDTH:3 * NA_WIDTH]
    xb_ref[...] = proj[:, 3 * NA_WIDTH:3 * NA_WIDTH + LRU_WIDTH]
    gb_ref[...] = proj[:, 3 * NA_WIDTH + LRU_WIDTH:]


def _proj(x, mod_l, g, w_in, q_g, k_g):
    tm = TOKEN_TILE
    head = np.arange(NA_WIDTH) // HEAD_DIM
    ones_bd = jnp.asarray((head[:, None] == head[None, :]).astype(np.float32), dtype=BF16)
    tok = lambda i: (i, 0)
    const = lambda i: (0, 0)
    half = jax.ShapeDtypeStruct((N_TOK, NA_WIDTH), F32)
    return pl.pallas_call(
        _proj_kernel,
        grid=(N_TOK // tm,),
        in_specs=[
            pl.BlockSpec((tm, D_MODEL), tok),
            pl.BlockSpec((1, N_MOD, D_MODEL), lambda i: (_mod_row_of_tile(i), 0, 0)),
            pl.BlockSpec((1, D_MODEL), const),
            _resident((D_MODEL, IN_WIDTH), const),
            pl.BlockSpec((1, NA_WIDTH), const),
            pl.BlockSpec((1, NA_WIDTH), const),
            _resident((NA_WIDTH, NA_WIDTH), const),
        ],
        out_specs=[pl.BlockSpec((tm, NA_WIDTH), tok)] * 5,
        out_shape=[jax.ShapeDtypeStruct((N_TOK, NA_WIDTH), BF16), half, half, half, half],
        compiler_params=_cparams(1),
        name="mixer_in_proj",
    )(x, mod_l, g.reshape(1, D_MODEL), w_in,
      jnp.tile(q_g, NA_HEADS).reshape(1, NA_WIDTH), jnp.tile(k_g, NA_HEADS).reshape(1, NA_WIDTH),
      ones_bd)


def _head_masks():
    lane = lax.broadcasted_iota(jnp.int32, (1, 2 * HEAD_DIM), 1)
    return [lane < HEAD_DIM, lane >= HEAD_DIM]


def _qk(q, k):
    return lax.dot_general(q, k, (((1,), (1,)), ((), ())), preferred_element_type=F32)


def _ctx_attn_kernel(q_ref, k_ref, v_ref, o_ref):
    masks = _head_masks()
    for p in range(NA_HEADS // 2):
        sl = slice(2 * HEAD_DIM * p, 2 * HEAD_DIM * (p + 1))
        q2 = q_ref[:, sl]
        k2 = k_ref[:, sl].astype(BF16)
        v2 = v_ref[:, sl].astype(BF16)
        out = None
        for e in range(2):
            qm = jnp.where(masks[e], q2, jnp.zeros_like(q2))
            s = _qk(qm, k2)
            pe = jnp.exp(s - jnp.max(s, axis=-1, keepdims=True))
            den = jnp.sum(pe, axis=-1, keepdims=True)
            o = jnp.dot(pe.astype(BF16), v2, preferred_element_type=F32) / den
            out = o if out is None else jnp.where(masks[e], o, out)
        o_ref[:, sl] = out.astype(BF16)


def _ctx_attn(q, k, v):
    blk = pl.BlockSpec((SEQ, NA_WIDTH), lambda b: (b, 0))
    return pl.pallas_call(
        _ctx_attn_kernel,
        grid=(BATCH,),
        in_specs=[blk, blk, blk],
        out_specs=blk,
        out_shape=jax.ShapeDtypeStruct((N_CTX_TOK, NA_WIDTH), BF16),
        compiler_params=_cparams(1),
        name="ctx_attention",
    )(q, k, v)


def _na_kernel(q_ref, k_ref, v_ref, kc_ref, vc_ref, bias_ref, o_ref):
    r = pl.program_id(1)
    row_start = jnp.clip(r - KH // 2, 0, ROWS - KH)
    start = pl.multiple_of(row_start * GRID_W, GRID_W)
    masks = _head_masks()
    for p in range(NA_HEADS // 2):
        sl = slice(2 * HEAD_DIM * p, 2 * HEAD_DIM * (p + 1))
        q2 = q_ref[:, sl]
        kl = k_ref[pl.ds(start, N_LOC), sl].astype(BF16)
        vl = v_ref[pl.ds(start, N_LOC), sl].astype(BF16)
        kc = kc_ref[0, :, sl].astype(BF16)
        vc = vc_ref[0, :, sl].astype(BF16)
        out = None
        for e in range(2):
            qm = jnp.where(masks[e], q2, jnp.zeros_like(q2))
            s_loc = _qk(qm, kl) + bias_ref[2 * p + e, 0]
            s_ctx = _qk(qm, kc)
            m = jnp.maximum(jnp.max(s_loc, axis=-1, keepdims=True),
                            jnp.max(s_ctx, axis=-1, keepdims=True))
            p_loc = jnp.exp(s_loc - m)
            p_ctx = jnp.exp(s_ctx - m)
            den = jnp.sum(p_loc, axis=-1, keepdims=True) + jnp.sum(p_ctx, axis=-1, keepdims=True)
            o = (jnp.dot(p_loc.astype(BF16), vl, preferred_element_type=F32)
                 + jnp.dot(p_ctx.astype(BF16), vc, preferred_element_type=F32)) / den
            out = o if out is None else jnp.where(masks[e], o, out)
        o_ref[:, sl] = out.astype(BF16)


def _na_bias(rpb_e):
    r = np.arange(ROWS)
    row_start = np.clip(r - KH // 2, 0, ROWS - KH)
    key_rows = row_start[:, None] + np.arange(KH)[None, :]
    cq = np.arange(GRID_W)
    col_start = np.clip(cq - WIN_W // 2, 0, GRID_W - WIN_W)
    col_in = (cq[None, :] >= col_start[:, None]) & (cq[None, :] < col_start[:, None] + WIN_W)
    dr_idx = key_rows - r[:, None] + (WIN_H - 1)
    dc_idx = np.clip(cq[None, :] - cq[:, None] + (WIN_W - 1), 0, 2 * WIN_W - 2)
    bias = rpb_e[:, dr_idx[:, None, :, None], dc_idx[None, :, None, :]].astype(F32)
    bias = jnp.where(col_in[None, None, :, None, :], bias, -jnp.inf)
    return bias.reshape(NA_HEADS, ROWS, GRID_W, N_LOC)


def _na_attn(q, k, v, k_ctx, v_ctx, bias):
    smp_blk0 = N_CTX_TOK // DEC_SEQ
    q_blk0 = N_CTX_TOK // GRID_W
    kv = pl.BlockSpec((DEC_SEQ, NA_WIDTH), lambda b, r: (smp_blk0 + b, 0))
    ctx = pl.BlockSpec((1, PAST_LEN, NA_WIDTH), lambda b, r: (b, 0, 0))
    return pl.pallas_call(
        _na_kernel,
        grid=(DEC_BATCH, ROWS),
        in_specs=[
            pl.BlockSpec((GRID_W, NA_WIDTH), lambda b, r: (q_blk0 + b * ROWS + r, 0)),
            kv, kv, ctx, ctx,
            pl.BlockSpec((NA_HEADS, 1, GRID_W, N_LOC), lambda b, r: (0, r, 0, 0)),
        ],
        out_specs=pl.BlockSpec((GRID_W, NA_WIDTH), lambda b, r: (b * ROWS + r, 0)),
        out_shape=jax.ShapeDtypeStruct((N_SMP_TOK, NA_WIDTH), BF16),
        compiler_params=_cparams(2),
        name="neighbourhood_attention",
    )(q, k, v, k_ctx, v_ctx, bias)


def _expm1(y):
    e = jnp.exp(y)
    return jnp.where(jnp.abs(y) < 0.5, jnp.tanh(0.5 * y) * (e + 1.0), e - 1.0)


def _lru_kernel(xb_ref, gb_ref, cw_ref, cb_ref, w_ref, b_ref, lam_ref, h0_ref,
                y_ref, hl_ref, af_ref, uf_ref, ab_ref, ub_ref):
    x = xb_ref[...]
    t_len, width = x.shape
    row = lax.broadcasted_iota(jnp.int32, (t_len, 1), 0)

    def shifted(z, s, fill):
        rolled = pltpu.roll(z, (-s) % t_len, axis=0)
        ok = (row + s >= 0) & (row + s < t_len)
        return jnp.where(ok, rolled, fill)

    left = (CONV_W - 1) // 2
    xc = cb_ref[...]
    for j in range(CONV_W):
        tap = x if j == left else shifted(x, j - left, 0.0)
        xc = xc + tap * cw_ref[j:j + 1, :]

    gates = jnp.dot(xc.astype(BF16), w_ref[0], preferred_element_type=F32) + b_ref[0]
    in_block = row % SUBLANES
    h = []
    for d, (a_ref, u_ref) in enumerate(((af_ref, uf_ref), (ab_ref, ub_ref))):
        r_gate = jax.nn.sigmoid(gates[:, (2 * d) * width:(2 * d + 1) * width])
        i_gate = jax.nn.sigmoid(gates[:, (2 * d + 1) * width:(2 * d + 2) * width])
        lam = lam_ref[0, d:d + 1, :]
        log_sig = jnp.minimum(lam, 0.0) - jnp.log1p(jnp.exp(-jnp.abs(lam)))
        log_a = LRU_C * r_gate * log_sig
        a = jnp.exp(log_a)
        u = jnp.sqrt(-_expm1(2.0 * log_a)) * (i_gate * xc)
        sign = -1 if d == 0 else 1
        step = 1
        while step < SUBLANES:
            if d == 0:
                ok = in_block >= step
            else:
                ok = in_block < SUBLANES - step
            a_prev = jnp.where(ok, pltpu.roll(a, (-sign * step) % t_len, axis=0), 1.0)
            u_prev = jnp.where(ok, pltpu.roll(u, (-sign * step) % t_len, axis=0), 0.0)
            u = u + a * u_prev
            a = a * a_prev
            step *= 2
        a_ref[...] = a
        u_ref[...] = u

    n_blk = t_len // SUBLANES

    def body(i, carry):
        cf, cb = carry
        f0 = pl.multiple_of(i * SUBLANES, SUBLANES)
        b0 = pl.multiple_of((n_blk - 1 - i) * SUBLANES, SUBLANES)
        hf = uf_ref[pl.ds(f0, SUBLANES), :] + af_ref[pl.ds(f0, SUBLANES), :] * cf
        hb = ub_ref[pl.ds(b0, SUBLANES), :] + ab_ref[pl.ds(b0, SUBLANES), :] * cb
        uf_ref[pl.ds(f0, SUBLANES), :] = hf
        ub_ref[pl.ds(b0, SUBLANES), :] = hb
        cf = jnp.broadcast_to(hf[SUBLANES - 1:SUBLANES, :], (SUBLANES, width))
        cb = jnp.broadcast_to(hb[0:1, :], (SUBLANES, width))
        return cf, cb

    c0f = jnp.broadcast_to(h0_ref[0, 0:1, :], (SUBLANES, width))
    c0b = jnp.broadcast_to(h0_ref[0, 1:2, :], (SUBLANES, width))
    cf, cb = lax.fori_loop(0, n_blk, body, (c0f, c0b))
    hl_ref[0, 0:1, :] = cf[0:1, :]
    hl_ref[0, 1:2, :] = cb[0:1, :]
    y_ref[...] = ((uf_ref[...] + ub_ref[...]) * jax.nn.gelu(gb_ref[...])).astype(BF16)


def _lru(xb, gb, conv_w, conv_b, w_bd, b_bd, lam, h0, n_seq, t_len, tok_blk0):
    n_half = 2
    cw = LRU_WIDTH // n_half
    seq = lambda s, c: (tok_blk0 + s, c)
    return pl.pallas_call(
        _lru_kernel,
        grid=(n_seq, n_half),
        in_specs=[
            pl.BlockSpec((t_len, cw), seq),
            pl.BlockSpec((t_len, cw), seq),
            pl.BlockSpec((CONV_W, cw), lambda s, c: (0, c)),
            pl.BlockSpec((1, cw), lambda s, c: (0, c)),
            pl.BlockSpec((1, cw, 4 * cw), lambda s, c: (c, 0, 0)),
            pl.BlockSpec((1, 1, 4 * cw), lambda s, c: (c, 0, 0)),
            pl.BlockSpec((1, 2, cw), lambda s, c: (0, 0, c)),
            pl.BlockSpec((1, 2, cw), lambda s, c: (s, 0, c)),
        ],
        out_specs=[
            pl.BlockSpec((t_len, cw), lambda s, c: (s, c)),
            pl.BlockSpec((1, 2, cw), lambda s, c: (s, 0, c)),
        ],
        out_shape=[
            jax.ShapeDtypeStruct((n_seq * t_len, LRU_WIDTH), BF16),
            jax.ShapeDtypeStruct((n_seq, 2, LRU_WIDTH), F32),
        ],
        scratch_shapes=[pltpu.VMEM((t_len, cw), F32)] * 4,
        compiler_params=_cparams(2),
        name="rglru",
    )(xb, gb, conv_w, conv_b.reshape(1, LRU_WIDTH), w_bd, b_bd, lam.reshape(1, 2, LRU_WIDTH), h0)


def _lru_gate_weights(w_r, b_r, w_i, b_i):
    cw = LRU_WIDTH // 2
    bpc = cw // LRU_BLOCK

    def dense_half(w, c):
        blocks = w[c * bpc:(c + 1) * bpc]
        eye = jnp.eye(bpc, dtype=w.dtype)
        return jnp.einsum('nij,nm->nimj', blocks, eye).reshape(cw, cw)

    w_halves, b_halves = [], []
    for c in range(2):
        w_halves.append(jnp.concatenate(
            [dense_half(w_r[0], c), dense_half(w_i[0], c), dense_half(w_r[1], c), dense_half(w_i[1], c)],
            axis=1))
        sl = slice(c * cw, (c + 1) * cw)
        b_halves.append(jnp.concatenate([b_r[0, sl], b_i[0, sl], b_r[1, sl], b_i[1, sl]]))
    return jnp.stack(w_halves).astype(BF16), jnp.stack(b_halves).reshape(2, 1, 4 * cw)


def _out_proj_kernel(x_ref, mod_ref, o_ref, yb_ref, w_ref, out_ref):
    cat = jnp.concatenate([o_ref[...], yb_ref[...]], axis=1)
    y = jnp.dot(cat, w_ref[...], preferred_element_type=F32)
    out_ref[...] = x_ref[...] + mod_ref[0, 5:6, :] * y


def _out_proj(x, mod_l, o, yb, w_out):
    tm = TOKEN_TILE
    tok = lambda i: (i, 0)
    return pl.pallas_call(
        _out_proj_kernel,
        grid=(N_TOK // tm,),
        in_specs=[
            pl.BlockSpec((tm, D_MODEL), tok),
            pl.BlockSpec((1, N_MOD, D_MODEL), lambda i: (_mod_row_of_tile(i), 0, 0)),
            pl.BlockSpec((tm, NA_WIDTH), tok),
            pl.BlockSpec((tm, LRU_WIDTH), tok),
            _resident((D_MODEL, D_MODEL), lambda i: (0, 0)),
        ],
        out_specs=pl.BlockSpec((tm, D_MODEL), tok),
        out_shape=jax.ShapeDtypeStruct((N_TOK, D_MODEL), F32),
        compiler_params=_cparams(1),
        name="mixer_out_proj",
    )(x, mod_l, o, yb, w_out)


def _fourier_kernel(x_ref, mod_ref, g_ref, cs_ref, ct_ref, w_ref, o_ref):
    x = x_ref[...]
    t_len = x.shape[0]
    h = _norm_mod(x, g_ref[...], mod_ref[0, 3:4, :], mod_ref[0, 4:5, :]).astype(BF16)
    cos_parts, sin_parts = [], []
    for g in range(FOURIER_GROUPS):
        ab = jnp.dot(h[:, g * GROUP_W:(g + 1) * GROUP_W], cs_ref[...], preferred_element_type=F32)
        cos_parts.append(ab[:, :GROUP_W])
        sin_parts.append(ab[:, GROUP_W:])
    stacked = jnp.concatenate(
        [jnp.concatenate(cos_parts, axis=1), jnp.concatenate(sin_parts, axis=1)], axis=0).astype(BF16)
    f = jnp.dot(ct_ref[...], stacked, preferred_element_type=F32) * ((t_len * GROUP_W) ** -0.5)
    y = jnp.dot(f.astype(BF16), w_ref[...], preferred_element_type=F32)
    o_ref[...] = x + mod_ref[0, 5:6, :] * y


def _dft_tables(t_len):
    def cos_sin(n):
        jk = np.outer(np.arange(n), np.arange(n)) % n
        ang = 2.0 * np.pi * jk.astype(np.float64) / n
        return np.cos(ang), np.sin(ang)

    cc, sc = cos_sin(GROUP_W)
    ct, st = cos_sin(t_len)
    chan = jnp.asarray(np.concatenate([cc, sc], axis=1).astype(np.float32)).astype(BF16)
    time = jnp.asarray(np.concatenate([ct, -st], axis=1).astype(np.float32)).astype(BF16)
    return chan, time


def _fourier(x, mod_l, g, w_out, n_seq, t_len, tok_blk0, mod_row0):
    chan, time = _dft_tables(t_len)
    seq = lambda s: (tok_blk0 + s, 0)
    const = lambda s: (0, 0)
    return pl.pallas_call(
        _fourier_kernel,
        grid=(n_seq,),
        in_specs=[
            pl.BlockSpec((t_len, D_MODEL), seq),
            pl.BlockSpec((1, N_MOD, D_MODEL), lambda s: (mod_row0 + (s if mod_row0 else 0), 0, 0)),
            pl.BlockSpec((1, D_MODEL), const),
            _resident((GROUP_W, 2 * GROUP_W), const),
            _resident((t_len, 2 * t_len), const),
            _resident((D_MODEL, D_MODEL), const),
        ],
        out_specs=pl.BlockSpec((t_len, D_MODEL), seq),
        out_shape=jax.ShapeDtypeStruct((N_TOK, D_MODEL), F32),
        input_output_aliases={0: 0},
        compiler_params=_cparams(1),
        name="fourier_mixer",
    )(x, mod_l, g.reshape(1, D_MODEL), chan, time, w_out)


def kernel(x_prompt, x_sample, cache_k, cache_v, state_lru_fwd, state_lru_bwd, c, c_ctx, w_ada, b_ada, norm_g, ffn1_gate, ffn1_up, ffn1_down, ffn2_gate, ffn2_up, ffn2_down, w_in, q_norm_g, k_norm_g, rpb, conv_w, conv_b, lru_w_r, lru_b_r, lru_w_i, lru_b_i, lru_lambda, w_out_ab, w_out_c):
    cond = jnp.concatenate(
        [c_ctx[None, :], c, jnp.zeros((MOD_ROWS - 1 - DEC_BATCH, D_MODEL), F32)], axis=0)
    mod = _adaln(cond, w_ada, b_ada).reshape(DEPTH, MOD_ROWS, N_MOD, D_MODEL)

    ffn_w = [[w.astype(BF16) for w in (ffn1_gate, ffn1_up, ffn1_down)],
             [w.astype(BF16) for w in (ffn2_gate, ffn2_up, ffn2_down)]]
    x = jnp.concatenate(
        [x_prompt.reshape(N_CTX_TOK, D_MODEL), x_sample.reshape(N_SMP_TOK, D_MODEL)], axis=0)

    new_k = new_v = new_hf = new_hb = None
    for l in range(DEPTH):
        x = _ffn(x, mod[l], norm_g[l, 0], *ffn_w[0], l, 0)
        if l % 2 == 0:
            e = l // 2
            q, k, v, xb, gb = _proj(x, mod[l], norm_g[l, 1], w_in[e].astype(BF16),
                                    q_norm_g[e], k_norm_g[e])
            o_ctx = _ctx_attn(q, k, v)
            o_smp = _na_attn(q, k, v,
                             cache_k[:, e].reshape(DEC_BATCH, PAST_LEN, NA_WIDTH),
                             cache_v[:, e].reshape(DEC_BATCH, PAST_LEN, NA_WIDTH),
                             _na_bias(rpb[e]))
            w_bd, b_bd = _lru_gate_weights(lru_w_r[e], lru_b_r[e], lru_w_i[e], lru_b_i[e])
            h0_ctx = jnp.zeros((BATCH, 2, LRU_WIDTH), F32)
            h0_smp = jnp.stack([state_lru_fwd[:, e], state_lru_bwd[:, e]], axis=1).astype(F32)
            yb_ctx, hl_ctx = _lru(xb, gb, conv_w[e], conv_b[e], w_bd, b_bd, lru_lambda[e],
                                  h0_ctx, BATCH, SEQ, 0)
            yb_smp, _ = _lru(xb, gb, conv_w[e], conv_b[e], w_bd, b_bd, lru_lambda[e],
                             h0_smp, DEC_BATCH, DEC_SEQ, N_CTX_TOK // DEC_SEQ)
            x = _out_proj(x, mod[l], jnp.concatenate([o_ctx, o_smp], axis=0),
                          jnp.concatenate([yb_ctx, yb_smp], axis=0), w_out_ab[e].astype(BF16))
            new_k = k[:N_CTX_TOK].reshape(BATCH, 1, SEQ, NA_HEADS, HEAD_DIM)
            new_v = v[:N_CTX_TOK].reshape(BATCH, 1, SEQ, NA_HEADS, HEAD_DIM)
            new_hf = hl_ctx[:, 0].reshape(BATCH, 1, LRU_WIDTH)
            new_hb = hl_ctx[:, 1].reshape(BATCH, 1, LRU_WIDTH)
        else:
            w_c = w_out_c[l // 2].astype(BF16)
            x = _fourier(x, mod[l], norm_g[l, 1], w_c, BATCH, SEQ, 0, 0)
            x = _fourier(x, mod[l], norm_g[l, 1], w_c, DEC_BATCH, DEC_SEQ, N_CTX_TOK // DEC_SEQ, 1)
        x = _ffn(x, mod[l], norm_g[l, 2], *ffn_w[1], l, 6)

    y_prompt = x[:N_CTX_TOK].reshape(BATCH, SEQ, D_MODEL)
    y_sample = x[N_CTX_TOK:].reshape(DEC_BATCH, DEC_SEQ, D_MODEL)
    return (y_prompt, y_sample, new_k, new_v, new_hf, new_hb)
```

```python
import functools

import numpy as np
import jax
import jax.numpy as jnp
from jax import lax
from jax.experimental import pallas as pl
from jax.experimental.pallas import tpu as pltpu

F32 = jnp.float32
BF16 = jnp.bfloat16

D_MODEL = 1024
BATCH = 16
SEQ = 256
DEPTH = 2
DEC_BATCH = 2
DEC_SEQ = 1024
PAST_LEN = 256
GRID_W = 64
HEAD_DIM = 64
NA_WIDTH = 512
NA_HEADS = 8
WIN_H = 8
WIN_W = 16
LRU_WIDTH = 512
LRU_BLOCKS = 8
LRU_BLOCK = 64
LRU_C = 8.0
CONV_W = 4
FOURIER_GROUPS = 4
GROUP_W = D_MODEL // FOURIER_GROUPS
D_FF = 2816
N_MOD = 9
IN_WIDTH = 3 * NA_WIDTH + 2 * LRU_WIDTH
EPS = 1e-6

N_CTX_TOK = BATCH * SEQ
N_SMP_TOK = DEC_BATCH * DEC_SEQ
N_TOK = N_CTX_TOK + N_SMP_TOK
MOD_ROWS = 8
MOD_WIDTH = N_MOD * D_MODEL
ROWS = DEC_SEQ // GRID_W
KH = min(WIN_H, ROWS)

TOKEN_TILE = 512
N_CTX_TILES = N_CTX_TOK // TOKEN_TILE
FF_CHUNKS = 2
SUBLANES = 8
LANES = 128
VMEM_LIMIT = 56 * 1024 * 1024

NA_Q_ROWS = 4
NA_GROUPS = ROWS // NA_Q_ROWS
NA_K_ROWS = 12
NA_Q = NA_Q_ROWS * GRID_W
NA_K = NA_K_ROWS * GRID_W
N_DR = 2 * WIN_H - 1
N_DC = 2 * WIN_W - 1
N_DR_PAIRS = N_DR + 1


def _cparams(n_axes):
    return pltpu.CompilerParams(
        dimension_semantics=("arbitrary",) * n_axes, vmem_limit_bytes=VMEM_LIMIT)


def _resident(block_shape, index_map):
    return pl.BlockSpec(block_shape, index_map, pipeline_mode=pl.Buffered(1))


def _mod_spec(layer, n_axes):
    if n_axes == 1:
        return _resident((None, MOD_ROWS, MOD_WIDTH), lambda i: (layer, 0, 0))
    return _resident((None, MOD_ROWS, MOD_WIDTH), lambda i, j: (layer, 0, 0))


def _mod_row_of_tile(i):
    tiles_per_seq = DEC_SEQ // TOKEN_TILE
    return jnp.where(i < N_CTX_TILES, 0, 1 + (i - N_CTX_TILES) // tiles_per_seq)


def _mod_vec(mod_ref, row, k):
    return mod_ref[pl.ds(row, 1), k * D_MODEL:(k + 1) * D_MODEL]


def _norm_mod(x, g, shift, scale):
    ms = jnp.mean(x * x, axis=-1, keepdims=True)
    y = x * lax.rsqrt(ms + EPS) * g
    return y * (1.0 + scale) + shift


def _adaln_kernel(c_ref, w_ref, b_ref, o_ref):
    c = c_ref[...]
    s = (c * jax.nn.sigmoid(c)).astype(BF16)
    w = w_ref[0].astype(BF16)
    o_ref[0] = jnp.dot(s, w, preferred_element_type=F32) + b_ref[0]


def _adaln(cond, w_ada, b_ada):
    tn = MOD_WIDTH // 4
    return pl.pallas_call(
        _adaln_kernel,
        grid=(DEPTH, MOD_WIDTH // tn),
        in_specs=[
            pl.BlockSpec((MOD_ROWS, D_MODEL), lambda l, j: (0, 0)),
            pl.BlockSpec((1, D_MODEL, tn), lambda l, j: (l, 0, j)),
            pl.BlockSpec((1, 1, tn), lambda l, j: (l, 0, j)),
        ],
        out_specs=pl.BlockSpec((1, MOD_ROWS, tn), lambda l, j: (l, 0, j)),
        out_shape=jax.ShapeDtypeStruct((DEPTH, MOD_ROWS, MOD_WIDTH), F32),
        compiler_params=_cparams(2),
        name="adaln",
    )(cond, w_ada, b_ada.reshape(DEPTH, 1, MOD_WIDTH))


def _ffn_kernel(*refs, mod_base, split_in, split_out):
    n_x = 2 if split_in else 1
    x_refs, (mod_ref, g_ref, wg_ref, wu_ref, wd_ref), o_refs = refs[:n_x], refs[n_x:n_x + 5], refs[n_x + 5:]
    i = pl.program_id(0)
    is_ctx = i < N_CTX_TILES
    if split_in:
        x = jnp.where(is_ctx, x_refs[0][...], x_refs[1][...])
    else:
        x = x_refs[0][...]
    row = _mod_row_of_tile(i)
    h = _norm_mod(x, g_ref[...], _mod_vec(mod_ref, row, mod_base),
                  _mod_vec(mod_ref, row, mod_base + 1)).astype(BF16)
    tf = D_FF // FF_CHUNKS
    acc = None
    for j in range(FF_CHUNKS):
        sl = slice(j * tf, (j + 1) * tf)
        a = jnp.dot(h, wg_ref[:, sl], preferred_element_type=F32)
        b = jnp.dot(h, wu_ref[:, sl], preferred_element_type=F32)
        act = (a * jax.nn.sigmoid(a) * b).astype(BF16)
        y = jnp.dot(act, wd_ref[sl, :], preferred_element_type=F32)
        acc = y if acc is None else acc + y
    res = x + 0.5 * _mod_vec(mod_ref, row, mod_base + 2) * acc
    if split_out:
        @pl.when(is_ctx)
        def _():
            o_refs[0][...] = res

        @pl.when(jnp.logical_not(is_ctx))
        def _():
            o_refs[1][...] = res
    else:
        o_refs[0][...] = res


def _ffn(xs, mod, g, wg, wu, wd, layer, mod_base, split_out=False):
    tm = TOKEN_TILE
    split_in = len(xs) == 2
    tok = pl.BlockSpec((tm, D_MODEL), lambda i: (i, 0))
    ctx_tok = pl.BlockSpec((tm, D_MODEL), lambda i: (jnp.minimum(i, N_CTX_TILES - 1), 0))
    smp_tok = pl.BlockSpec((tm, D_MODEL), lambda i: (jnp.maximum(i - N_CTX_TILES, 0), 0))
    full = jax.ShapeDtypeStruct((N_TOK, D_MODEL), F32)
    pair = [jax.ShapeDtypeStruct((N_CTX_TOK, D_MODEL), F32), jax.ShapeDtypeStruct((N_SMP_TOK, D_MODEL), F32)]
    return pl.pallas_call(
        functools.partial(_ffn_kernel, mod_base=mod_base, split_in=split_in, split_out=split_out),
        grid=(N_TOK // tm,),
        in_specs=([ctx_tok, smp_tok] if split_in else [tok]) + [
            _mod_spec(layer, 1),
            pl.BlockSpec((1, D_MODEL), lambda i: (0, 0)),
            _resident((None, D_MODEL, D_FF), lambda i: (layer, 0, 0)),
            _resident((None, D_MODEL, D_FF), lambda i: (layer, 0, 0)),
            _resident((None, D_FF, D_MODEL), lambda i: (layer, 0, 0)),
        ],
        out_specs=[ctx_tok, smp_tok] if split_out else tok,
        out_shape=pair if split_out else full,
        compiler_params=_cparams(1),
        name="ffn",
    )(*xs, mod, g.reshape(1, D_MODEL), wg, wu, wd)


def _head_rms_norm(z, g, ones_bd):
    z2 = z * z
    hi = z2.astype(BF16)
    lo = (z2 - hi.astype(F32)).astype(BF16)
    ss = (jnp.dot(hi, ones_bd, preferred_element_type=F32)
          + jnp.dot(lo, ones_bd, preferred_element_type=F32))
    return z * lax.rsqrt(ss * (1.0 / HEAD_DIM) + EPS) * g


def _proj_kernel(x_ref, mod_ref, g_ref, w_ref, qg_ref, kg_ref, ones_ref,
                 q_ref, k_ref, v_ref, xb_ref, gb_ref):
    x = x_ref[...]
    row = _mod_row_of_tile(pl.program_id(0))
    h = _norm_mod(x, g_ref[...], _mod_vec(mod_ref, row, 3), _mod_vec(mod_ref, row, 4)).astype(BF16)
    proj = jnp.dot(h, w_ref[...], preferred_element_type=F32)
    ones_bd = ones_ref[...]
    q = _head_rms_norm(proj[:, :NA_WIDTH], qg_ref[...], ones_bd) * (HEAD_DIM ** -0.5)
    q_ref[...] = q.astype(BF16)
    k_ref[...] = _head_rms_norm(proj[:, NA_WIDTH:2 * NA_WIDTH], kg_ref[...], ones_bd)
    v_ref[...] = proj[:, 2 * NA_WIDTH:3 * NA_WIDTH]
    xb_ref[...] = proj[:, 3 * NA_WIDTH:3 * NA_WIDTH + LRU_WIDTH]
    gb_ref[...] = proj[:, 3 * NA_WIDTH + LRU_WIDTH:]


def _proj(x, mod, layer, g, w_in, q_g, k_g):
    tm = TOKEN_TILE
    head = np.arange(NA_WIDTH) // HEAD_DIM
    ones_bd = jnp.asarray((head[:, None] == head[None, :]).astype(np.float32), dtype=BF16)
    tok = lambda i: (i, 0)
    const = lambda i: (0, 0)
    half = jax.ShapeDtypeStruct((N_TOK, NA_WIDTH), F32)
    return pl.pallas_call(
        _proj_kernel,
        grid=(N_TOK // tm,),
        in_specs=[
            pl.BlockSpec((tm, D_MODEL), tok),
            _mod_spec(layer, 1),
            pl.BlockSpec((1, D_MODEL), const),
            _resident((D_MODEL, IN_WIDTH), const),
            pl.BlockSpec((1, NA_WIDTH), const),
            pl.BlockSpec((1, NA_WIDTH), const),
            _resident((NA_WIDTH, NA_WIDTH), const),
        ],
        out_specs=[pl.BlockSpec((tm, NA_WIDTH), tok)] * 5,
        out_shape=[jax.ShapeDtypeStruct((N_TOK, NA_WIDTH), BF16), half, half, half, half],
        compiler_params=_cparams(1),
        name="mixer_in_proj",
    )(x, mod, g.reshape(1, D_MODEL), w_in,
      jnp.tile(q_g, NA_HEADS).reshape(1, NA_WIDTH), jnp.tile(k_g, NA_HEADS).reshape(1, NA_WIDTH),
      ones_bd)


def _head_masks():
    lane = lax.broadcasted_iota(jnp.int32, (1, 2 * HEAD_DIM), 1)
    return [lane < HEAD_DIM, lane >= HEAD_DIM]


def _qk(q, k):
    return lax.dot_general(q, k, (((1,), (1,)), ((), ())), preferred_element_type=F32)


def _ctx_attn_kernel(q_ref, k_ref, v_ref, o_ref):
    masks = _head_masks()
    for p in range(NA_HEADS // 2):
        sl = slice(2 * HEAD_DIM * p, 2 * HEAD_DIM * (p + 1))
        q2 = q_ref[:, sl]
        k2 = k_ref[:, sl].astype(BF16)
        v2 = v_ref[:, sl].astype(BF16)
        out = None
        for e in range(2):
            qm = jnp.where(masks[e], q2, jnp.zeros_like(q2))
            s = _qk(qm, k2)
            pe = jnp.exp(s - jnp.max(s, axis=-1, keepdims=True))
            den = jnp.sum(pe, axis=-1, keepdims=True)
            o = jnp.dot(pe.astype(BF16), v2, preferred_element_type=F32) / den
            out = o if out is None else jnp.where(masks[e], o, out)
        o_ref[:, sl] = out.astype(BF16)


def _ctx_attn(q, k, v):
    blk = pl.BlockSpec((SEQ, NA_WIDTH), lambda b: (b, 0))
    return pl.pallas_call(
        _ctx_attn_kernel,
        grid=(BATCH,),
        in_specs=[blk, blk, blk],
        out_specs=blk,
        out_shape=jax.ShapeDtypeStruct((N_CTX_TOK, NA_WIDTH), BF16),
        compiler_params=_cparams(1),
        name="ctx_attention",
    )(q, k, v)


def _na_build_bias_table(rpb_ref, table_ref):
    qc = lax.broadcasted_iota(jnp.int32, (GRID_W, LANES), 0)
    lane = lax.broadcasted_iota(jnp.int32, (GRID_W, LANES), 1)
    kc = lane % GRID_W
    col_start = jnp.clip(qc - WIN_W // 2, 0, GRID_W - WIN_W)
    col_in = (kc >= col_start) & (kc < col_start + WIN_W)
    neg = jnp.full((GRID_W, LANES), -jnp.inf, F32)

    def toeplitz(h, dr, lane0):
        if dr < 0 or dr >= N_DR:
            return neg
        w = jnp.broadcast_to(rpb_ref[h, dr:dr + 1, :], (GRID_W, LANES))
        return pltpu.roll(w, (lane0 - (WIN_W - 1)) % LANES, 1, stride=1, stride_axis=0)

    for h in range(NA_HEADS):
        for i in range(N_DR_PAIRS):
            t = jnp.where(lane < GRID_W, toeplitz(h, i - 1, 0), toeplitz(h, i, GRID_W))
            table_ref[h, i] = jnp.where(col_in, t, neg)


def _na_kernel(q_ref, k_ref, v_ref, kc_ref, vc_ref, rpb_ref, o_ref, table_ref):
    b = pl.program_id(0)
    g = pl.program_id(1)

    @pl.when((b == 0) & (g == 0))
    def _():
        _na_build_bias_table(rpb_ref, table_ref)

    win_row0 = jnp.where(g < NA_GROUPS // 2, 0, ROWS - NA_K_ROWS)
    start = pl.multiple_of(win_row0 * GRID_W, GRID_W)
    q_row = g * NA_Q_ROWS + lax.broadcasted_iota(jnp.int32, (NA_Q, 1), 0) // GRID_W
    k_row = win_row0 + lax.broadcasted_iota(jnp.int32, (1, NA_K), 1) // GRID_W
    row_start = jnp.clip(q_row - KH // 2, 0, ROWS - KH)
    row_in = (k_row >= row_start) & (k_row < row_start + KH)
    masks = _head_masks()
    for p in range(NA_HEADS // 2):
        sl = slice(2 * HEAD_DIM * p, 2 * HEAD_DIM * (p + 1))
        q2 = q_ref[:, sl]
        kl = k_ref[pl.ds(start, NA_K), sl].astype(BF16)
        vl = v_ref[pl.ds(start, NA_K), sl].astype(BF16)
        kc = kc_ref[0, :, sl].astype(BF16)
        vc = vc_ref[0, :, sl].astype(BF16)
        out = None
        for e in range(2):
            head = 2 * p + e
            bias_rows = []
            for a in range(NA_Q_ROWS):
                tiles = []
                for m in range(NA_K_ROWS // 2):
                    dr = win_row0 + 2 * m - (g * NA_Q_ROWS + a) + (WIN_H - 1)
                    tiles.append(table_ref[head, jnp.clip(dr + 1, 0, N_DR_PAIRS - 1)])
                bias_rows.append(jnp.concatenate(tiles, axis=1))
            bias = jnp.concatenate(bias_rows, axis=0)
            qm = jnp.where(masks[e], q2, jnp.zeros_like(q2))
            s_loc = jnp.where(row_in, _qk(qm, kl) + bias, -jnp.inf)
            s_ctx = _qk(qm, kc)
            m_max = jnp.maximum(jnp.max(s_loc, axis=-1, keepdims=True),
                                jnp.max(s_ctx, axis=-1, keepdims=True))
            p_loc = jnp.exp(s_loc - m_max)
            p_ctx = jnp.exp(s_ctx - m_max)
            den = jnp.sum(p_loc, axis=-1, keepdims=True) + jnp.sum(p_ctx, axis=-1, keepdims=True)
            o = (jnp.dot(p_loc.astype(BF16), vl, preferred_element_type=F32)
                 + jnp.dot(p_ctx.astype(BF16), vc, preferred_element_type=F32)) / den
            out = o if out is None else jnp.where(masks[e], o, out)
        o_ref[:, sl] = out.astype(BF16)


def _na_attn(q, k, v, k_ctx, v_ctx, rpb_e):
    smp_blk0 = N_CTX_TOK // DEC_SEQ
    q_blk0 = N_CTX_TOK // NA_Q
    kv = pl.BlockSpec((DEC_SEQ, NA_WIDTH), lambda b, g: (smp_blk0 + b, 0))
    ctx = pl.BlockSpec((1, PAST_LEN, NA_WIDTH), lambda b, g: (b, 0, 0))
    rpb_pad = jnp.pad(rpb_e.astype(F32), ((0, 0), (0, 0), (0, LANES - N_DC)))
    return pl.pallas_call(
        _na_kernel,
        grid=(DEC_BATCH, NA_GROUPS),
        in_specs=[
            pl.BlockSpec((NA_Q, NA_WIDTH), lambda b, g: (q_blk0 + b * NA_GROUPS + g, 0)),
            kv, kv, ctx, ctx,
            pl.BlockSpec((NA_HEADS, N_DR, LANES), lambda b, g: (0, 0, 0)),
        ],
        out_specs=pl.BlockSpec((NA_Q, NA_WIDTH), lambda b, g: (b * NA_GROUPS + g, 0)),
        out_shape=jax.ShapeDtypeStruct((N_SMP_TOK, NA_WIDTH), BF16),
        scratch_shapes=[pltpu.VMEM((NA_HEADS, N_DR_PAIRS, GRID_W, LANES), F32)],
        compiler_params=_cparams(2),
        name="neighbourhood_attention",
    )(q, k, v, k_ctx, v_ctx, rpb_pad)


def _expm1(y):
    e = jnp.exp(y)
    return jnp.where(jnp.abs(y) < 0.5, jnp.tanh(0.5 * y) * (e + 1.0), e - 1.0)


def _lru_kernel(xb_ref, gb_ref, cw_ref, cb_ref, w_ref, b_ref, lam_ref, h0_ref,
                y_ref, hl_ref, af_ref, uf_ref, ab_ref, ub_ref):
    x = xb_ref[...]
    t_len, width = x.shape
    row = lax.broadcasted_iota(jnp.int32, (t_len, 1), 0)

    def shifted(z, s, fill):
        rolled = pltpu.roll(z, (-s) % t_len, axis=0)
        ok = (row + s >= 0) & (row + s < t_len)
        return jnp.where(ok, rolled, fill)

    left = (CONV_W - 1) // 2
    xc = cb_ref[...]
    for j in range(CONV_W):
        tap = x if j == left else shifted(x, j - left, 0.0)
        xc = xc + tap * cw_ref[j:j + 1, :]

    gates = jnp.dot(xc.astype(BF16), w_ref[0], preferred_element_type=F32) + b_ref[0]
    in_block = row % SUBLANES
    for d, (a_ref, u_ref) in enumerate(((af_ref, uf_ref), (ab_ref, ub_ref))):
        r_gate = jax.nn.sigmoid(gates[:, (2 * d) * width:(2 * d + 1) * width])
        i_gate = jax.nn.sigmoid(gates[:, (2 * d + 1) * width:(2 * d + 2) * width])
        lam = lam_ref[0, d:d + 1, :]
        log_sig = jnp.minimum(lam, 0.0) - jnp.log1p(jnp.exp(-jnp.abs(lam)))
        log_a = LRU_C * r_gate * log_sig
        a = jnp.exp(log_a)
        u = jnp.sqrt(-_expm1(2.0 * log_a)) * (i_gate * xc)
        sign = -1 if d == 0 else 1
        step = 1
        while step < SUBLANES:
            if d == 0:
                ok = in_block >= step
            else:
                ok = in_block < SUBLANES - step
            a_prev = jnp.where(ok, pltpu.roll(a, (-sign * step) % t_len, axis=0), 1.0)
            u_prev = jnp.where(ok, pltpu.roll(u, (-sign * step) % t_len, axis=0), 0.0)
            u = u + a * u_prev
            a = a * a_prev
            step *= 2
        a_ref[...] = a
        u_ref[...] = u

    n_blk = t_len // SUBLANES

    def body(i, carry):
        cf, cb = carry
        f0 = pl.multiple_of(i * SUBLANES, SUBLANES)
        b0 = pl.multiple_of((n_blk - 1 - i) * SUBLANES, SUBLANES)
        hf = uf_ref[pl.ds(f0, SUBLANES), :] + af_ref[pl.ds(f0, SUBLANES), :] * cf
        hb = ub_ref[pl.ds(b0, SUBLANES), :] + ab_ref[pl.ds(b0, SUBLANES), :] * cb
        uf_ref[pl.ds(f0, SUBLANES), :] = hf
        ub_ref[pl.ds(b0, SUBLANES), :] = hb
        cf = jnp.broadcast_to(hf[SUBLANES - 1:SUBLANES, :], (SUBLANES, width))
        cb = jnp.broadcast_to(hb[0:1, :], (SUBLANES, width))
        return cf, cb

    c0f = jnp.broadcast_to(h0_ref[0, 0:1, :], (SUBLANES, width))
    c0b = jnp.broadcast_to(h0_ref[0, 1:2, :], (SUBLANES, width))
    cf, cb = lax.fori_loop(0, n_blk, body, (c0f, c0b))
    hl_ref[0, 0:1, :] = cf[0:1, :]
    hl_ref[0, 1:2, :] = cb[0:1, :]
    y_ref[...] = ((uf_ref[...] + ub_ref[...]) * jax.nn.gelu(gb_ref[...])).astype(BF16)


def _lru(xb, gb, conv_w, conv_b, w_bd, b_bd, lam, h0, n_seq, t_len, tok_blk0):
    n_half = 2
    cw = LRU_WIDTH // n_half
    seq = lambda s, c: (tok_blk0 + s, c)
    return pl.pallas_call(
        _lru_kernel,
        grid=(n_seq, n_half),
        in_specs=[
            pl.BlockSpec((t_len, cw), seq),
            pl.BlockSpec((t_len, cw), seq),
            pl.BlockSpec((CONV_W, cw), lambda s, c: (0, c)),
            pl.BlockSpec((1, cw), lambda s, c: (0, c)),
            pl.BlockSpec((1, cw, 4 * cw), lambda s, c: (c, 0, 0)),
            pl.BlockSpec((1, 1, 4 * cw), lambda s, c: (c, 0, 0)),
            pl.BlockSpec((1, 2, cw), lambda s, c: (0, 0, c)),
            pl.BlockSpec((1, 2, cw), lambda s, c: (s, 0, c)),
        ],
        out_specs=[
            pl.BlockSpec((t_len, cw), lambda s, c: (s, c)),
            pl.BlockSpec((1, 2, cw), lambda s, c: (s, 0, c)),
        ],
        out_shape=[
            jax.ShapeDtypeStruct((n_seq * t_len, LRU_WIDTH), BF16),
            jax.ShapeDtypeStruct((n_seq, 2, LRU_WIDTH), F32),
        ],
        scratch_shapes=[pltpu.VMEM((t_len, cw), F32)] * 4,
        compiler_params=_cparams(2),
        name="rglru",
    )(xb, gb, conv_w, conv_b.reshape(1, LRU_WIDTH), w_bd, b_bd, lam.reshape(1, 2, LRU_WIDTH), h0)


def _lru_gate_weights(w_r, b_r, w_i, b_i):
    cw = LRU_WIDTH // 2
    bpc = cw // LRU_BLOCK

    def dense_half(w, c):
        blocks = w[c * bpc:(c + 1) * bpc]
        eye = jnp.eye(bpc, dtype=w.dtype)
        return jnp.einsum('nij,nm->nimj', blocks, eye).reshape(cw, cw)

    w_halves, b_halves = [], []
    for c in range(2):
        w_halves.append(jnp.concatenate(
            [dense_half(w_r[0], c), dense_half(w_i[0], c), dense_half(w_r[1], c), dense_half(w_i[1], c)],
            axis=1))
        sl = slice(c * cw, (c + 1) * cw)
        b_halves.append(jnp.concatenate([b_r[0, sl], b_i[0, sl], b_r[1, sl], b_i[1, sl]]))
    return jnp.stack(w_halves).astype(BF16), jnp.stack(b_halves).reshape(2, 1, 4 * cw)


def _out_proj_kernel(x_ref, mod_ref, oc_ref, os_ref, yc_ref, ys_ref, w_ref, out_ref):
    i = pl.program_id(0)
    is_ctx = i < N_CTX_TILES
    o = jnp.where(is_ctx, oc_ref[...], os_ref[...])
    yb = jnp.where(is_ctx, yc_ref[...], ys_ref[...])
    cat = jnp.concatenate([o, yb], axis=1)
    y = jnp.dot(cat, w_ref[...], preferred_element_type=F32)
    out_ref[...] = x_ref[...] + _mod_vec(mod_ref, _mod_row_of_tile(i), 5) * y


def _out_proj(x, mod, layer, o_ctx, o_smp, yb_ctx, yb_smp, w_out):
    tm = TOKEN_TILE
    tok = lambda i: (i, 0)
    ctx_tok = pl.BlockSpec((tm, NA_WIDTH), lambda i: (jnp.minimum(i, N_CTX_TILES - 1), 0))
    smp_tok = pl.BlockSpec((tm, NA_WIDTH), lambda i: (jnp.maximum(i - N_CTX_TILES, 0), 0))
    return pl.pallas_call(
        _out_proj_kernel,
        grid=(N_TOK // tm,),
        in_specs=[
            pl.BlockSpec((tm, D_MODEL), tok),
            _mod_spec(layer, 1),
            ctx_tok, smp_tok, ctx_tok, smp_tok,
            _resident((D_MODEL, D_MODEL), lambda i: (0, 0)),
        ],
        out_specs=pl.BlockSpec((tm, D_MODEL), tok),
        out_shape=jax.ShapeDtypeStruct((N_TOK, D_MODEL), F32),
        compiler_params=_cparams(1),
        name="mixer_out_proj",
    )(x, mod, o_ctx, o_smp, yb_ctx, yb_smp, w_out)


def _fourier_kernel(x_ref, mod_ref, g_ref, cs_ref, ct_ref, w_ref, o_ref, *, mod_row0):
    x = x_ref[...]
    t_len = x.shape[0]
    row = mod_row0 + pl.program_id(0) if mod_row0 else 0
    h = _norm_mod(x, g_ref[...], _mod_vec(mod_ref, row, 3), _mod_vec(mod_ref, row, 4)).astype(BF16)
    cos_parts, sin_parts = [], []
    for g in range(FOURIER_GROUPS):
        ab = jnp.dot(h[:, g * GROUP_W:(g + 1) * GROUP_W], cs_ref[...], preferred_element_type=F32)
        cos_parts.append(ab[:, :GROUP_W])
        sin_parts.append(ab[:, GROUP_W:])
    stacked = jnp.concatenate(
        [jnp.concatenate(cos_parts, axis=1), jnp.concatenate(sin_parts, axis=1)], axis=0).astype(BF16)
    f = jnp.dot(ct_ref[...], stacked, preferred_element_type=F32) * ((t_len * GROUP_W) ** -0.5)
    y = jnp.dot(f.astype(BF16), w_ref[...], preferred_element_type=F32)
    o_ref[...] = x + _mod_vec(mod_ref, row, 5) * y


def _dft_tables(t_len):
    def cos_sin(n):
        jk = np.outer(np.arange(n), np.arange(n)) % n
        ang = 2.0 * np.pi * jk.astype(np.float64) / n
        return np.cos(ang), np.sin(ang)

    cc, sc = cos_sin(GROUP_W)
    ct, st = cos_sin(t_len)
    chan = jnp.asarray(np.concatenate([cc, sc], axis=1).astype(np.float32)).astype(BF16)
    time = jnp.asarray(np.concatenate([ct, -st], axis=1).astype(np.float32)).astype(BF16)
    return chan, time


def _fourier(x, mod, layer, g, w_out, n_seq, t_len, tok_blk0, mod_row0):
    chan, time = _dft_tables(t_len)
    seq = lambda s: (tok_blk0 + s, 0)
    const = lambda s: (0, 0)
    return pl.pallas_call(
        functools.partial(_fourier_kernel, mod_row0=mod_row0),
        grid=(n_seq,),
        in_specs=[
            pl.BlockSpec((t_len, D_MODEL), seq),
            _mod_spec(layer, 1),
            pl.BlockSpec((1, D_MODEL), const),
            _resident((GROUP_W, 2 * GROUP_W), const),
            _resident((t_len, 2 * t_len), const),
            _resident((D_MODEL, D_MODEL), const),
        ],
        out_specs=pl.BlockSpec((t_len, D_MODEL), seq),
        out_shape=jax.ShapeDtypeStruct((N_TOK, D_MODEL), F32),
        input_output_aliases={0: 0},
        compiler_params=_cparams(1),
        name="fourier_mixer",
    )(x, mod, g.reshape(1, D_MODEL), chan, time, w_out)


def kernel(x_prompt, x_sample, cache_k, cache_v, state_lru_fwd, state_lru_bwd, c, c_ctx, w_ada, b_ada, norm_g, ffn1_gate, ffn1_up, ffn1_down, ffn2_gate, ffn2_up, ffn2_down, w_in, q_norm_g, k_norm_g, rpb, conv_w, conv_b, lru_w_r, lru_b_r, lru_w_i, lru_b_i, lru_lambda, w_out_ab, w_out_c):
    assert DEPTH == 2, "one neighbourhood/RG-LRU layer followed by one Fourier layer"
    cond = jnp.concatenate(
        [c_ctx[None, :], c, jnp.zeros((MOD_ROWS - 1 - DEC_BATCH, D_MODEL), F32)], axis=0)
    mod = _adaln(cond, w_ada, b_ada)

    ffn1 = [w.astype(BF16) for w in (ffn1_gate, ffn1_up, ffn1_down)]
    ffn2 = [w.astype(BF16) for w in (ffn2_gate, ffn2_up, ffn2_down)]

    x = _ffn((x_prompt.reshape(N_CTX_TOK, D_MODEL), x_sample.reshape(N_SMP_TOK, D_MODEL)),
             mod, norm_g[0, 0], *ffn1, 0, 0)
    q, k, v, xb, gb = _proj(x, mod, 0, norm_g[0, 1], w_in[0].astype(BF16), q_norm_g[0], k_norm_g[0])
    o_ctx = _ctx_attn(q, k, v)
    o_smp = _na_attn(q, k, v,
                     cache_k[:, 0].reshape(DEC_BATCH, PAST_LEN, NA_WIDTH),
                     cache_v[:, 0].reshape(DEC_BATCH, PAST_LEN, NA_WIDTH), rpb[0])
    w_bd, b_bd = _lru_gate_weights(lru_w_r[0], lru_b_r[0], lru_w_i[0], lru_b_i[0])
    h0_ctx = jnp.zeros((BATCH, 2, LRU_WIDTH), F32)
    h0_smp = jnp.stack([state_lru_fwd[:, 0], state_lru_bwd[:, 0]], axis=1).astype(F32)
    yb_ctx, hl_ctx = _lru(xb, gb, conv_w[0], conv_b[0], w_bd, b_bd, lru_lambda[0],
                          h0_ctx, BATCH, SEQ, 0)
    yb_smp, _ = _lru(xb, gb, conv_w[0], conv_b[0], w_bd, b_bd, lru_lambda[0],
                     h0_smp, DEC_BATCH, DEC_SEQ, N_CTX_TOK // DEC_SEQ)
    x = _out_proj(x, mod, 0, o_ctx, o_smp, yb_ctx, yb_smp, w_out_ab[0].astype(BF16))
    x = _ffn((x,), mod, norm_g[0, 2], *ffn2, 0, 6)

    x = _ffn((x,), mod, norm_g[1, 0], *ffn1, 1, 0)
    w_c = w_out_c[0].astype(BF16)
    x = _fourier(x, mod, 1, norm_g[1, 1], w_c, BATCH, SEQ, 0, 0)
    x = _fourier(x, mod, 1, norm_g[1, 1], w_c, DEC_BATCH, DEC_SEQ, N_CTX_TOK // DEC_SEQ, 1)
    y_prompt, y_sample = _ffn((x,), mod, norm_g[1, 2], *ffn2, 1, 6, split_out=True)

    new_k = k[:N_CTX_TOK].reshape(BATCH, 1, SEQ, NA_HEADS, HEAD_DIM)
    new_v = v[:N_CTX_TOK].reshape(BATCH, 1, SEQ, NA_HEADS, HEAD_DIM)
    return (y_prompt.reshape(BATCH, SEQ, D_MODEL), y_sample.reshape(DEC_BATCH, DEC_SEQ, D_MODEL),
            new_k, new_v,
            hl_ctx[:, 0].reshape(BATCH, 1, LRU_WIDTH), hl_ctx[:, 1].reshape(BATCH, 1, LRU_WIDTH))
```

```python
import functools

import numpy as np
import jax
import jax.numpy as jnp
from jax import lax
from jax.experimental import pallas as pl
from jax.experimental.pallas import tpu as pltpu

F32 = jnp.float32
BF16 = jnp.bfloat16

D_MODEL = 1024
BATCH = 16
SEQ = 256
DEPTH = 2
DEC_BATCH = 2
DEC_SEQ = 1024
PAST_LEN = 256
GRID_W = 64
HEAD_DIM = 64
NA_WIDTH = 512
NA_HEADS = 8
WIN_H = 8
WIN_W = 16
LRU_WIDTH = 512
LRU_BLOCKS = 8
LRU_BLOCK = 64
LRU_C = 8.0
CONV_W = 4
FOURIER_GROUPS = 4
GROUP_W = D_MODEL // FOURIER_GROUPS
D_FF = 2816
N_MOD = 9
IN_WIDTH = 3 * NA_WIDTH + 2 * LRU_WIDTH
EPS = 1e-6

N_CTX_TOK = BATCH * SEQ
N_SMP_TOK = DEC_BATCH * DEC_SEQ
N_TOK = N_CTX_TOK + N_SMP_TOK
MOD_ROWS = 8
MOD_WIDTH = N_MOD * D_MODEL
ROWS = DEC_SEQ // GRID_W
KH = min(WIN_H, ROWS)

TOKEN_TILE = 512
N_CTX_TILES = N_CTX_TOK // TOKEN_TILE
FF_CHUNKS = 1
SUBLANES = 8
LANES = 128
VMEM_LIMIT = 56 * 1024 * 1024

NA_Q_ROWS = 4
NA_GROUPS = ROWS // NA_Q_ROWS
NA_K_ROWS = 12
NA_Q = NA_Q_ROWS * GRID_W
NA_K = NA_K_ROWS * GRID_W
N_DR = 2 * WIN_H - 1
N_DC = 2 * WIN_W - 1
N_DR_PAIRS = N_DR + 1


def _cparams(n_axes):
    return pltpu.CompilerParams(
        dimension_semantics=("arbitrary",) * n_axes, vmem_limit_bytes=VMEM_LIMIT)


def _resident(block_shape, index_map):
    return pl.BlockSpec(block_shape, index_map, pipeline_mode=pl.Buffered(1))


def _mod_spec(layer, n_axes):
    if n_axes == 1:
        return _resident((None, MOD_ROWS, MOD_WIDTH), lambda i: (layer, 0, 0))
    return _resident((None, MOD_ROWS, MOD_WIDTH), lambda i, j: (layer, 0, 0))


def _mod_row_of_tile(i):
    tiles_per_seq = DEC_SEQ // TOKEN_TILE
    return jnp.where(i < N_CTX_TILES, 0, 1 + (i - N_CTX_TILES) // tiles_per_seq)


def _mod_vec(mod_ref, row, k):
    return mod_ref[pl.ds(row, 1), k * D_MODEL:(k + 1) * D_MODEL]


def _norm_mod(x, g, shift, scale):
    ms = jnp.mean(x * x, axis=-1, keepdims=True)
    y = x * lax.rsqrt(ms + EPS) * g
    return y * (1.0 + scale) + shift


def _adaln_kernel(c_ref, w_ref, b_ref, o_ref):
    c = c_ref[...]
    s = (c * jax.nn.sigmoid(c)).astype(BF16)
    w = w_ref[0].astype(BF16)
    o_ref[0] = jnp.dot(s, w, preferred_element_type=F32) + b_ref[0]


def _adaln(cond, w_ada, b_ada):
    tn = MOD_WIDTH // 4
    return pl.pallas_call(
        _adaln_kernel,
        grid=(DEPTH, MOD_WIDTH // tn),
        in_specs=[
            pl.BlockSpec((MOD_ROWS, D_MODEL), lambda l, j: (0, 0)),
            pl.BlockSpec((1, D_MODEL, tn), lambda l, j: (l, 0, j)),
            pl.BlockSpec((1, 1, tn), lambda l, j: (l, 0, j)),
        ],
        out_specs=pl.BlockSpec((1, MOD_ROWS, tn), lambda l, j: (l, 0, j)),
        out_shape=jax.ShapeDtypeStruct((DEPTH, MOD_ROWS, MOD_WIDTH), F32),
        compiler_params=_cparams(2),
        name="adaln",
    )(cond, w_ada, b_ada.reshape(DEPTH, 1, MOD_WIDTH))


def _ffn_kernel(*refs, mod_base, split_in, split_out):
    n_x = 2 if split_in else 1
    x_refs, (mod_ref, g_ref, wg_ref, wu_ref, wd_ref), o_refs = refs[:n_x], refs[n_x:n_x + 5], refs[n_x + 5:]
    i = pl.program_id(0)
    is_ctx = i < N_CTX_TILES
    if split_in:
        x = jnp.where(is_ctx, x_refs[0][...], x_refs[1][...])
    else:
        x = x_refs[0][...]
    row = _mod_row_of_tile(i)
    h = _norm_mod(x, g_ref[...], _mod_vec(mod_ref, row, mod_base),
                  _mod_vec(mod_ref, row, mod_base + 1)).astype(BF16)
    tf = D_FF // FF_CHUNKS
    acc = None
    for j in range(FF_CHUNKS):
        sl = slice(j * tf, (j + 1) * tf)
        a = jnp.dot(h, wg_ref[:, sl], preferred_element_type=F32)
        b = jnp.dot(h, wu_ref[:, sl], preferred_element_type=F32)
        act = (a * jax.nn.sigmoid(a) * b).astype(BF16)
        y = jnp.dot(act, wd_ref[sl, :], preferred_element_type=F32)
        acc = y if acc is None else acc + y
    res = x + 0.5 * _mod_vec(mod_ref, row, mod_base + 2) * acc
    if split_out:
        @pl.when(is_ctx)
        def _():
            o_refs[0][...] = res

        @pl.when(jnp.logical_not(is_ctx))
        def _():
            o_refs[1][...] = res
    else:
        o_refs[0][...] = res


def _ffn(xs, mod, g, wg, wu, wd, layer, mod_base, split_out=False):
    tm = TOKEN_TILE
    split_in = len(xs) == 2
    tok = pl.BlockSpec((tm, D_MODEL), lambda i: (i, 0))
    ctx_tok = pl.BlockSpec((tm, D_MODEL), lambda i: (jnp.minimum(i, N_CTX_TILES - 1), 0))
    smp_tok = pl.BlockSpec((tm, D_MODEL), lambda i: (jnp.maximum(i - N_CTX_TILES, 0), 0))
    full = jax.ShapeDtypeStruct((N_TOK, D_MODEL), F32)
    pair = [jax.ShapeDtypeStruct((N_CTX_TOK, D_MODEL), F32), jax.ShapeDtypeStruct((N_SMP_TOK, D_MODEL), F32)]
    return pl.pallas_call(
        functools.partial(_ffn_kernel, mod_base=mod_base, split_in=split_in, split_out=split_out),
        grid=(N_TOK // tm,),
        in_specs=([ctx_tok, smp_tok] if split_in else [tok]) + [
            _mod_spec(layer, 1),
            pl.BlockSpec((1, D_MODEL), lambda i: (0, 0)),
            _resident((None, D_MODEL, D_FF), lambda i: (layer, 0, 0)),
            _resident((None, D_MODEL, D_FF), lambda i: (layer, 0, 0)),
            _resident((None, D_FF, D_MODEL), lambda i: (layer, 0, 0)),
        ],
        out_specs=[ctx_tok, smp_tok] if split_out else tok,
        out_shape=pair if split_out else full,
        compiler_params=_cparams(1),
        name="ffn",
    )(*xs, mod, g.reshape(1, D_MODEL), wg, wu, wd)


def _head_rms_norm(z, g, ones_bd):
    z2 = z * z
    hi = z2.astype(BF16)
    lo = (z2 - hi.astype(F32)).astype(BF16)
    ss = (jnp.dot(hi, ones_bd, preferred_element_type=F32)
          + jnp.dot(lo, ones_bd, preferred_element_type=F32))
    return z * lax.rsqrt(ss * (1.0 / HEAD_DIM) + EPS) * g


def _proj_kernel(x_ref, mod_ref, g_ref, w_ref, qg_ref, kg_ref, ones_ref,
                 q_ref, k_ref, v_ref, xb_ref, gb_ref, kout_ref, vout_ref, w_bf_ref):
    i = pl.program_id(0)

    @pl.when(i == 0)
    def _():
        w_bf_ref[...] = w_ref[...].astype(BF16)

    x = x_ref[...]
    row = _mod_row_of_tile(i)
    h = _norm_mod(x, g_ref[...], _mod_vec(mod_ref, row, 3), _mod_vec(mod_ref, row, 4)).astype(BF16)
    proj = jnp.dot(h, w_bf_ref[...], preferred_element_type=F32)
    ones_bd = ones_ref[...]
    q = _head_rms_norm(proj[:, :NA_WIDTH], qg_ref[...], ones_bd) * (HEAD_DIM ** -0.5)
    k = _head_rms_norm(proj[:, NA_WIDTH:2 * NA_WIDTH], kg_ref[...], ones_bd)
    v = proj[:, 2 * NA_WIDTH:3 * NA_WIDTH]
    q_ref[...] = q.astype(BF16)
    k_ref[...] = k.astype(BF16)
    v_ref[...] = v.astype(BF16)
    xb_ref[...] = proj[:, 3 * NA_WIDTH:3 * NA_WIDTH + LRU_WIDTH]
    gb_ref[...] = proj[:, 3 * NA_WIDTH + LRU_WIDTH:]

    @pl.when(i < N_CTX_TILES)
    def _():
        kout_ref[...] = k.reshape(TOKEN_TILE, NA_HEADS, HEAD_DIM)
        vout_ref[...] = v.reshape(TOKEN_TILE, NA_HEADS, HEAD_DIM)


def _proj(x, mod, layer, g, w_in, q_g, k_g):
    tm = TOKEN_TILE
    head = np.arange(NA_WIDTH) // HEAD_DIM
    ones_bd = jnp.asarray((head[:, None] == head[None, :]).astype(np.float32), dtype=BF16)
    tok = lambda i: (i, 0)
    const = lambda i: (0, 0)
    act_f32 = jax.ShapeDtypeStruct((N_TOK, NA_WIDTH), F32)
    act_bf16 = jax.ShapeDtypeStruct((N_TOK, NA_WIDTH), BF16)
    cache = jax.ShapeDtypeStruct((N_CTX_TOK, NA_HEADS, HEAD_DIM), F32)
    cache_spec = pl.BlockSpec((tm, NA_HEADS, HEAD_DIM), lambda i: (jnp.minimum(i, N_CTX_TILES - 1), 0, 0))
    return pl.pallas_call(
        _proj_kernel,
        grid=(N_TOK // tm,),
        in_specs=[
            pl.BlockSpec((tm, D_MODEL), tok),
            _mod_spec(layer, 1),
            pl.BlockSpec((1, D_MODEL), const),
            _resident((D_MODEL, IN_WIDTH), const),
            pl.BlockSpec((1, NA_WIDTH), const),
            pl.BlockSpec((1, NA_WIDTH), const),
            _resident((NA_WIDTH, NA_WIDTH), const),
        ],
        out_specs=[pl.BlockSpec((tm, NA_WIDTH), tok)] * 5 + [cache_spec, cache_spec],
        out_shape=[act_bf16, act_bf16, act_bf16, act_f32, act_f32, cache, cache],
        scratch_shapes=[pltpu.VMEM((D_MODEL, IN_WIDTH), BF16)],
        compiler_params=_cparams(1),
        name="mixer_in_proj",
    )(x, mod, g.reshape(1, D_MODEL), w_in,
      jnp.tile(q_g, NA_HEADS).reshape(1, NA_WIDTH), jnp.tile(k_g, NA_HEADS).reshape(1, NA_WIDTH),
      ones_bd)


def _head_masks():
    lane = lax.broadcasted_iota(jnp.int32, (1, 2 * HEAD_DIM), 1)
    return [lane < HEAD_DIM, lane >= HEAD_DIM]


def _qk(q, k):
    return lax.dot_general(q, k, (((1,), (1,)), ((), ())), preferred_element_type=F32)


def _ctx_attn_kernel(q_ref, k_ref, v_ref, o_ref):
    masks = _head_masks()
    for p in range(NA_HEADS // 2):
        sl = slice(2 * HEAD_DIM * p, 2 * HEAD_DIM * (p + 1))
        q2 = q_ref[:, sl]
        k2 = k_ref[:, sl]
        v2 = v_ref[:, sl]
        out = None
        for e in range(2):
            qm = jnp.where(masks[e], q2, jnp.zeros_like(q2))
            s = _qk(qm, k2)
            pe = jnp.exp(s - jnp.max(s, axis=-1, keepdims=True))
            den = jnp.sum(pe, axis=-1, keepdims=True)
            o = jnp.dot(pe.astype(BF16), v2, preferred_element_type=F32) / den
            out = o if out is None else jnp.where(masks[e], o, out)
        o_ref[:, sl] = out.astype(BF16)


def _ctx_attn(q, k, v):
    blk = pl.BlockSpec((SEQ, NA_WIDTH), lambda b: (b, 0))
    return pl.pallas_call(
        _ctx_attn_kernel,
        grid=(BATCH,),
        in_specs=[blk, blk, blk],
        out_specs=blk,
        out_shape=jax.ShapeDtypeStruct((N_CTX_TOK, NA_WIDTH), BF16),
        compiler_params=_cparams(1),
        name="ctx_attention",
    )(q, k, v)


def _na_build_bias_table(rpb_ref, table_ref):
    qc = lax.broadcasted_iota(jnp.int32, (GRID_W, LANES), 0)
    lane = lax.broadcasted_iota(jnp.int32, (GRID_W, LANES), 1)
    kc = lane % GRID_W
    col_start = jnp.clip(qc - WIN_W // 2, 0, GRID_W - WIN_W)
    col_in = (kc >= col_start) & (kc < col_start + WIN_W)
    neg = jnp.full((GRID_W, LANES), -jnp.inf, F32)

    def toeplitz(h, dr, lane0):
        if dr < 0 or dr >= N_DR:
            return neg
        w = jnp.broadcast_to(rpb_ref[h, dr:dr + 1, :], (GRID_W, LANES))
        return pltpu.roll(w, (lane0 - (WIN_W - 1)) % LANES, 1, stride=1, stride_axis=0)

    for h in range(NA_HEADS):
        for i in range(N_DR_PAIRS):
            t = jnp.where(lane < GRID_W, toeplitz(h, i - 1, 0), toeplitz(h, i, GRID_W))
            table_ref[h, i] = jnp.where(col_in, t, neg)


def _na_kernel(q_ref, k_ref, v_ref, kc_ref, vc_ref, rpb_ref, o_ref, table_ref):
    b = pl.program_id(0)
    g = pl.program_id(1)

    @pl.when((b == 0) & (g == 0))
    def _():
        _na_build_bias_table(rpb_ref, table_ref)

    win_row0 = jnp.where(g < NA_GROUPS // 2, 0, ROWS - NA_K_ROWS)
    start = pl.multiple_of(win_row0 * GRID_W, GRID_W)
    q_row = g * NA_Q_ROWS + lax.broadcasted_iota(jnp.int32, (NA_Q, 1), 0) // GRID_W
    k_row = win_row0 + lax.broadcasted_iota(jnp.int32, (1, NA_K), 1) // GRID_W
    row_start = jnp.clip(q_row - KH // 2, 0, ROWS - KH)
    row_in = (k_row >= row_start) & (k_row < row_start + KH)
    masks = _head_masks()
    for p in range(NA_HEADS // 2):
        sl = slice(2 * HEAD_DIM * p, 2 * HEAD_DIM * (p + 1))
        q2 = q_ref[:, sl]
        kl = k_ref[pl.ds(start, NA_K), sl]
        vl = v_ref[pl.ds(start, NA_K), sl]
        kc = kc_ref[0, :, sl].astype(BF16)
        vc = vc_ref[0, :, sl].astype(BF16)
        out = None
        for e in range(2):
            head = 2 * p + e
            bias_rows = []
            for a in range(NA_Q_ROWS):
                tiles = []
                for m in range(NA_K_ROWS // 2):
                    dr = win_row0 + 2 * m - (g * NA_Q_ROWS + a) + (WIN_H - 1)
                    tiles.append(table_ref[head, jnp.clip(dr + 1, 0, N_DR_PAIRS - 1)])
                bias_rows.append(jnp.concatenate(tiles, axis=1))
            bias = jnp.concatenate(bias_rows, axis=0)
            qm = jnp.where(masks[e], q2, jnp.zeros_like(q2))
            s_loc = jnp.where(row_in, _qk(qm, kl) + bias, -jnp.inf)
            s_ctx = _qk(qm, kc)
            m_max = jnp.maximum(jnp.max(s_loc, axis=-1, keepdims=True),
                                jnp.max(s_ctx, axis=-1, keepdims=True))
            p_loc = jnp.exp(s_loc - m_max)
            p_ctx = jnp.exp(s_ctx - m_max)
            den = jnp.sum(p_loc, axis=-1, keepdims=True) + jnp.sum(p_ctx, axis=-1, keepdims=True)
            o = (jnp.dot(p_loc.astype(BF16), vl, preferred_element_type=F32)
                 + jnp.dot(p_ctx.astype(BF16), vc, preferred_element_type=F32)) / den
            out = o if out is None else jnp.where(masks[e], o, out)
        o_ref[:, sl] = out.astype(BF16)


def _na_attn(q, k, v, k_ctx, v_ctx, rpb_e):
    smp_blk0 = N_CTX_TOK // DEC_SEQ
    q_blk0 = N_CTX_TOK // NA_Q
    kv = pl.BlockSpec((DEC_SEQ, NA_WIDTH), lambda b, g: (smp_blk0 + b, 0))
    ctx = pl.BlockSpec((1, PAST_LEN, NA_WIDTH), lambda b, g: (b, 0, 0))
    rpb_pad = jnp.pad(rpb_e.astype(F32), ((0, 0), (0, 0), (0, LANES - N_DC)))
    return pl.pallas_call(
        _na_kernel,
        grid=(DEC_BATCH, NA_GROUPS),
        in_specs=[
            pl.BlockSpec((NA_Q, NA_WIDTH), lambda b, g: (q_blk0 + b * NA_GROUPS + g, 0)),
            kv, kv, ctx, ctx,
            pl.BlockSpec((NA_HEADS, N_DR, LANES), lambda b, g: (0, 0, 0)),
        ],
        out_specs=pl.BlockSpec((NA_Q, NA_WIDTH), lambda b, g: (b * NA_GROUPS + g, 0)),
        out_shape=jax.ShapeDtypeStruct((N_SMP_TOK, NA_WIDTH), BF16),
        scratch_shapes=[pltpu.VMEM((NA_HEADS, N_DR_PAIRS, GRID_W, LANES), F32)],
        compiler_params=_cparams(2),
        name="neighbourhood_attention",
    )(q, k, v, k_ctx, v_ctx, rpb_pad)


def _expm1(y):
    e = jnp.exp(y)
    return jnp.where(jnp.abs(y) < 0.5, jnp.tanh(0.5 * y) * (e + 1.0), e - 1.0)


def _lru_kernel(xb_ref, gb_ref, cw_ref, cb_ref, w_ref, b_ref, lam_ref, h0_ref,
                y_ref, hl_ref, af_ref, uf_ref, ab_ref, ub_ref):
    x = xb_ref[...]
    t_len, width = x.shape
    row = lax.broadcasted_iota(jnp.int32, (t_len, 1), 0)

    def shifted(z, s, fill):
        rolled = pltpu.roll(z, (-s) % t_len, axis=0)
        ok = (row + s >= 0) & (row + s < t_len)
        return jnp.where(ok, rolled, fill)

    left = (CONV_W - 1) // 2
    xc = cb_ref[...]
    for j in range(CONV_W):
        tap = x if j == left else shifted(x, j - left, 0.0)
        xc = xc + tap * cw_ref[j:j + 1, :]

    gates = jnp.dot(xc.astype(BF16), w_ref[0], preferred_element_type=F32) + b_ref[0]
    in_block = row % SUBLANES
    for d, (a_ref, u_ref) in enumerate(((af_ref, uf_ref), (ab_ref, ub_ref))):
        r_gate = jax.nn.sigmoid(gates[:, (2 * d) * width:(2 * d + 1) * width])
        i_gate = jax.nn.sigmoid(gates[:, (2 * d + 1) * width:(2 * d + 2) * width])
        lam = lam_ref[0, d:d + 1, :]
        log_sig = jnp.minimum(lam, 0.0) - jnp.log1p(jnp.exp(-jnp.abs(lam)))
        log_a = LRU_C * r_gate * log_sig
        a = jnp.exp(log_a)
        u = jnp.sqrt(-_expm1(2.0 * log_a)) * (i_gate * xc)
        sign = -1 if d == 0 else 1
        step = 1
        while step < SUBLANES:
            if d == 0:
                ok = in_block >= step
            else:
                ok = in_block < SUBLANES - step
            a_prev = jnp.where(ok, pltpu.roll(a, (-sign * step) % t_len, axis=0), 1.0)
            u_prev = jnp.where(ok, pltpu.roll(u, (-sign * step) % t_len, axis=0), 0.0)
            u = u + a * u_prev
            a = a * a_prev
            step *= 2
        a_ref[...] = a
        u_ref[...] = u

    n_blk = t_len // SUBLANES

    def body(i, carry):
        cf, cb = carry
        f0 = pl.multiple_of(i * SUBLANES, SUBLANES)
        b0 = pl.multiple_of((n_blk - 1 - i) * SUBLANES, SUBLANES)
        hf = uf_ref[pl.ds(f0, SUBLANES), :] + af_ref[pl.ds(f0, SUBLANES), :] * cf
        hb = ub_ref[pl.ds(b0, SUBLANES), :] + ab_ref[pl.ds(b0, SUBLANES), :] * cb
        uf_ref[pl.ds(f0, SUBLANES), :] = hf
        ub_ref[pl.ds(b0, SUBLANES), :] = hb
        cf = jnp.broadcast_to(hf[SUBLANES - 1:SUBLANES, :], (SUBLANES, width))
        cb = jnp.broadcast_to(hb[0:1, :], (SUBLANES, width))
        return cf, cb

    c0f = jnp.broadcast_to(h0_ref[0, 0:1, :], (SUBLANES, width))
    c0b = jnp.broadcast_to(h0_ref[0, 1:2, :], (SUBLANES, width))
    cf, cb = lax.fori_loop(0, n_blk, body, (c0f, c0b))
    hl_ref[0, 0:1, :] = cf[0:1, :]
    hl_ref[0, 1:2, :] = cb[0:1, :]
    y_ref[...] = ((uf_ref[...] + ub_ref[...]) * jax.nn.gelu(gb_ref[...])).astype(BF16)


def _lru(xb, gb, conv_w, conv_b, w_bd, b_bd, lam, h0, n_seq, t_len, tok_blk0):
    n_half = 2
    cw = LRU_WIDTH // n_half
    seq = lambda s, c: (tok_blk0 + s, c)
    return pl.pallas_call(
        _lru_kernel,
        grid=(n_seq, n_half),
        in_specs=[
            pl.BlockSpec((t_len, cw), seq),
            pl.BlockSpec((t_len, cw), seq),
            pl.BlockSpec((CONV_W, cw), lambda s, c: (0, c)),
            pl.BlockSpec((1, cw), lambda s, c: (0, c)),
            pl.BlockSpec((1, cw, 4 * cw), lambda s, c: (c, 0, 0)),
            pl.BlockSpec((1, 1, 4 * cw), lambda s, c: (c, 0, 0)),
            pl.BlockSpec((1, 2, cw), lambda s, c: (0, 0, c)),
            pl.BlockSpec((1, 2, cw), lambda s, c: (s, 0, c)),
        ],
        out_specs=[
            pl.BlockSpec((t_len, cw), lambda s, c: (s, c)),
            pl.BlockSpec((1, 2, cw), lambda s, c: (s, 0, c)),
        ],
        out_shape=[
            jax.ShapeDtypeStruct((n_seq * t_len, LRU_WIDTH), BF16),
            jax.ShapeDtypeStruct((n_seq, 2, LRU_WIDTH), F32),
        ],
        scratch_shapes=[pltpu.VMEM((t_len, cw), F32)] * 4,
        compiler_params=_cparams(2),
        name="rglru",
    )(xb, gb, conv_w, conv_b.reshape(1, LRU_WIDTH), w_bd, b_bd, lam.reshape(1, 2, LRU_WIDTH), h0)


def _lru_gate_weights(w_r, b_r, w_i, b_i):
    cw = LRU_WIDTH // 2
    bpc = cw // LRU_BLOCK

    def dense_half(w, c):
        blocks = w[c * bpc:(c + 1) * bpc]
        eye = jnp.eye(bpc, dtype=w.dtype)
        return jnp.einsum('nij,nm->nimj', blocks, eye).reshape(cw, cw)

    w_halves, b_halves = [], []
    for c in range(2):
        w_halves.append(jnp.concatenate(
            [dense_half(w_r[0], c), dense_half(w_i[0], c), dense_half(w_r[1], c), dense_half(w_i[1], c)],
            axis=1))
        sl = slice(c * cw, (c + 1) * cw)
        b_halves.append(jnp.concatenate([b_r[0, sl], b_i[0, sl], b_r[1, sl], b_i[1, sl]]))
    return jnp.stack(w_halves).astype(BF16), jnp.stack(b_halves).reshape(2, 1, 4 * cw)


def _out_proj_kernel(x_ref, mod_ref, oc_ref, os_ref, yc_ref, ys_ref, w_ref, out_ref, w_bf_ref):
    i = pl.program_id(0)

    @pl.when(i == 0)
    def _():
        w_bf_ref[...] = w_ref[...].astype(BF16)

    is_ctx = i < N_CTX_TILES
    o = jnp.where(is_ctx, oc_ref[...], os_ref[...])
    yb = jnp.where(is_ctx, yc_ref[...], ys_ref[...])
    cat = jnp.concatenate([o, yb], axis=1)
    y = jnp.dot(cat, w_bf_ref[...], preferred_element_type=F32)
    out_ref[...] = x_ref[...] + _mod_vec(mod_ref, _mod_row_of_tile(i), 5) * y


def _out_proj(x, mod, layer, o_ctx, o_smp, yb_ctx, yb_smp, w_out):
    tm = TOKEN_TILE
    tok = lambda i: (i, 0)
    ctx_tok = pl.BlockSpec((tm, NA_WIDTH), lambda i: (jnp.minimum(i, N_CTX_TILES - 1), 0))
    smp_tok = pl.BlockSpec((tm, NA_WIDTH), lambda i: (jnp.maximum(i - N_CTX_TILES, 0), 0))
    return pl.pallas_call(
        _out_proj_kernel,
        grid=(N_TOK // tm,),
        in_specs=[
            pl.BlockSpec((tm, D_MODEL), tok),
            _mod_spec(layer, 1),
            ctx_tok, smp_tok, ctx_tok, smp_tok,
            _resident((D_MODEL, D_MODEL), lambda i: (0, 0)),
        ],
        out_specs=pl.BlockSpec((tm, D_MODEL), tok),
        out_shape=jax.ShapeDtypeStruct((N_TOK, D_MODEL), F32),
        scratch_shapes=[pltpu.VMEM((D_MODEL, D_MODEL), BF16)],
        compiler_params=_cparams(1),
        name="mixer_out_proj",
    )(x, mod, o_ctx, o_smp, yb_ctx, yb_smp, w_out)


def _fourier_kernel(x_ref, mod_ref, g_ref, cs_ref, ct_ref, w_ref, o_ref, w_bf_ref, *, mod_row0):
    @pl.when(pl.program_id(0) == 0)
    def _():
        w_bf_ref[...] = w_ref[...].astype(BF16)

    x = x_ref[...]
    t_len = x.shape[0]
    row = mod_row0 + pl.program_id(0) if mod_row0 else 0
    h = _norm_mod(x, g_ref[...], _mod_vec(mod_ref, row, 3), _mod_vec(mod_ref, row, 4)).astype(BF16)
    cos_parts, sin_parts = [], []
    for g in range(FOURIER_GROUPS):
        ab = jnp.dot(h[:, g * GROUP_W:(g + 1) * GROUP_W], cs_ref[...], preferred_element_type=F32)
        cos_parts.append(ab[:, :GROUP_W])
        sin_parts.append(ab[:, GROUP_W:])
    stacked = jnp.concatenate(
        [jnp.concatenate(cos_parts, axis=1), jnp.concatenate(sin_parts, axis=1)], axis=0).astype(BF16)
    f = jnp.dot(ct_ref[...], stacked, preferred_element_type=F32) * ((t_len * GROUP_W) ** -0.5)
    y = jnp.dot(f.astype(BF16), w_bf_ref[...], preferred_element_type=F32)
    o_ref[...] = x + _mod_vec(mod_ref, row, 5) * y


def _dft_tables(t_len):
    def cos_sin(n):
        jk = np.outer(np.arange(n), np.arange(n)) % n
        ang = 2.0 * np.pi * jk.astype(np.float64) / n
        return np.cos(ang), np.sin(ang)

    cc, sc = cos_sin(GROUP_W)
    ct, st = cos_sin(t_len)
    chan = jnp.asarray(np.concatenate([cc, sc], axis=1).astype(np.float32)).astype(BF16)
    time = jnp.asarray(np.concatenate([ct, -st], axis=1).astype(np.float32)).astype(BF16)
    return chan, time


def _fourier(x, mod, layer, g, w_out, n_seq, t_len, tok_blk0, mod_row0):
    chan, time = _dft_tables(t_len)
    seq = lambda s: (tok_blk0 + s, 0)
    const = lambda s: (0, 0)
    return pl.pallas_call(
        functools.partial(_fourier_kernel, mod_row0=mod_row0),
        grid=(n_seq,),
        in_specs=[
            pl.BlockSpec((t_len, D_MODEL), seq),
            _mod_spec(layer, 1),
            pl.BlockSpec((1, D_MODEL), const),
            _resident((GROUP_W, 2 * GROUP_W), const),
            _resident((t_len, 2 * t_len), const),
            _resident((D_MODEL, D_MODEL), const),
        ],
        out_specs=pl.BlockSpec((t_len, D_MODEL), seq),
        out_shape=jax.ShapeDtypeStruct((N_TOK, D_MODEL), F32),
        input_output_aliases={0: 0},
        scratch_shapes=[pltpu.VMEM((D_MODEL, D_MODEL), BF16)],
        compiler_params=_cparams(1),
        name="fourier_mixer",
    )(x, mod, g.reshape(1, D_MODEL), chan, time, w_out)


def kernel(x_prompt, x_sample, cache_k, cache_v, state_lru_fwd, state_lru_bwd, c, c_ctx, w_ada, b_ada, norm_g, ffn1_gate, ffn1_up, ffn1_down, ffn2_gate, ffn2_up, ffn2_down, w_in, q_norm_g, k_norm_g, rpb, conv_w, conv_b, lru_w_r, lru_b_r, lru_w_i, lru_b_i, lru_lambda, w_out_ab, w_out_c):
    assert DEPTH == 2, "one neighbourhood/RG-LRU layer followed by one Fourier layer"
    cond = jnp.concatenate(
        [c_ctx[None, :], c, jnp.zeros((MOD_ROWS - 1 - DEC_BATCH, D_MODEL), F32)], axis=0)
    mod = _adaln(cond, w_ada, b_ada)

    ffn1 = [w.astype(BF16) for w in (ffn1_gate, ffn1_up, ffn1_down)]
    ffn2 = [w.astype(BF16) for w in (ffn2_gate, ffn2_up, ffn2_down)]

    x = _ffn((x_prompt.reshape(N_CTX_TOK, D_MODEL), x_sample.reshape(N_SMP_TOK, D_MODEL)),
             mod, norm_g[0, 0], *ffn1, 0, 0)
    q, k, v, xb, gb, new_k, new_v = _proj(x, mod, 0, norm_g[0, 1], w_in[0], q_norm_g[0], k_norm_g[0])
    o_ctx = _ctx_attn(q, k, v)
    o_smp = _na_attn(q, k, v,
                     cache_k[:, 0].reshape(DEC_BATCH, PAST_LEN, NA_WIDTH),
                     cache_v[:, 0].reshape(DEC_BATCH, PAST_LEN, NA_WIDTH), rpb[0])
    w_bd, b_bd = _lru_gate_weights(lru_w_r[0], lru_b_r[0], lru_w_i[0], lru_b_i[0])
    h0_ctx = jnp.zeros((BATCH, 2, LRU_WIDTH), F32)
    h0_smp = jnp.stack([state_lru_fwd[:, 0], state_lru_bwd[:, 0]], axis=1).astype(F32)
    yb_ctx, hl_ctx = _lru(xb, gb, conv_w[0], conv_b[0], w_bd, b_bd, lru_lambda[0],
                          h0_ctx, BATCH, SEQ, 0)
    yb_smp, _ = _lru(xb, gb, conv_w[0], conv_b[0], w_bd, b_bd, lru_lambda[0],
                     h0_smp, DEC_BATCH, DEC_SEQ, N_CTX_TOK // DEC_SEQ)
    x = _out_proj(x, mod, 0, o_ctx, o_smp, yb_ctx, yb_smp, w_out_ab[0])
    x = _ffn((x,), mod, norm_g[0, 2], *ffn2, 0, 6)

    x = _ffn((x,), mod, norm_g[1, 0], *ffn1, 1, 0)
    w_c = w_out_c[0]
    x = _fourier(x, mod, 1, norm_g[1, 1], w_c, BATCH, SEQ, 0, 0)
    x = _fourier(x, mod, 1, norm_g[1, 1], w_c, DEC_BATCH, DEC_SEQ, N_CTX_TOK // DEC_SEQ, 1)
    y_prompt, y_sample = _ffn((x,), mod, norm_g[1, 2], *ffn2, 1, 6, split_out=True)

    return (y_prompt.reshape(BATCH, SEQ, D_MODEL), y_sample.reshape(DEC_BATCH, DEC_SEQ, D_MODEL),
            new_k.reshape(BATCH, 1, SEQ, NA_HEADS, HEAD_DIM), new_v.reshape(BATCH, 1, SEQ, NA_HEADS, HEAD_DIM),
            hl_ctx[:, 0].reshape(BATCH, 1, LRU_WIDTH), hl_ctx[:, 1].reshape(BATCH, 1, LRU_WIDTH))
```

```python
import functools

import numpy as np
import jax
import jax.numpy as jnp
from jax import lax
from jax.experimental import pallas as pl
from jax.experimental.pallas import tpu as pltpu

F32 = jnp.float32
BF16 = jnp.bfloat16

D_MODEL = 1024
BATCH = 16
SEQ = 256
DEPTH = 2
DEC_BATCH = 2
DEC_SEQ = 1024
PAST_LEN = 256
GRID_W = 64
HEAD_DIM = 64
NA_WIDTH = 512
NA_HEADS = 8
WIN_H = 8
WIN_W = 16
LRU_WIDTH = 512
LRU_BLOCKS = 8
LRU_BLOCK = 64
LRU_C = 8.0
CONV_W = 4
FOURIER_GROUPS = 4
GROUP_W = D_MODEL // FOURIER_GROUPS
D_FF = 2816
N_MOD = 9
IN_WIDTH = 3 * NA_WIDTH + 2 * LRU_WIDTH
EPS = 1e-6

N_CTX_TOK = BATCH * SEQ
N_SMP_TOK = DEC_BATCH * DEC_SEQ
N_TOK = N_CTX_TOK + N_SMP_TOK
MOD_ROWS = 8
MOD_WIDTH = N_MOD * D_MODEL
ROWS = DEC_SEQ // GRID_W
KH = min(WIN_H, ROWS)

TOKEN_TILE = 512
N_CTX_TILES = N_CTX_TOK // TOKEN_TILE
FFN_TILE = 512
FF_TILE = 256
FF_CHUNKS = D_FF // FF_TILE
FF_STAGE_SLOTS = 2
SUBLANES = 8
LANES = 128
VMEM_LIMIT = 56 * 1024 * 1024

NA_Q_ROWS = 4
NA_GROUPS = ROWS // NA_Q_ROWS
NA_K_ROWS = 12
NA_Q = NA_Q_ROWS * GRID_W
NA_K = NA_K_ROWS * GRID_W
N_DR = 2 * WIN_H - 1
N_DC = 2 * WIN_W - 1
N_DR_PAIRS = N_DR + 1


def _cparams(n_axes):
    return pltpu.CompilerParams(
        dimension_semantics=("arbitrary",) * n_axes, vmem_limit_bytes=VMEM_LIMIT)


def _resident(block_shape, index_map):
    return pl.BlockSpec(block_shape, index_map, pipeline_mode=pl.Buffered(1))


def _mod_spec(layer, n_axes):
    if n_axes == 1:
        return _resident((None, MOD_ROWS, MOD_WIDTH), lambda i: (layer, 0, 0))
    return _resident((None, MOD_ROWS, MOD_WIDTH), lambda i, j: (layer, 0, 0))


def _mod_row_of_tile(i, tile=TOKEN_TILE):
    n_ctx_tiles = N_CTX_TOK // tile
    tiles_per_seq = DEC_SEQ // tile
    return jnp.where(i < n_ctx_tiles, 0, 1 + (i - n_ctx_tiles) // tiles_per_seq)


def _mod_vec(mod_ref, row, k):
    return mod_ref[pl.ds(row, 1), k * D_MODEL:(k + 1) * D_MODEL]


def _norm_mod(x, g, shift, scale):
    ms = jnp.mean(x * x, axis=-1, keepdims=True)
    y = x * lax.rsqrt(ms + EPS) * g
    return y * (1.0 + scale) + shift


def _adaln_kernel(c_ref, w_ref, b_ref, o_ref):
    c = c_ref[...]
    s = (c * jax.nn.sigmoid(c)).astype(BF16)
    w = w_ref[0].astype(BF16)
    o_ref[0] = jnp.dot(s, w, preferred_element_type=F32) + b_ref[0]


def _adaln(cond, w_ada, b_ada):
    tn = MOD_WIDTH // 4
    return pl.pallas_call(
        _adaln_kernel,
        grid=(DEPTH, MOD_WIDTH // tn),
        in_specs=[
            pl.BlockSpec((MOD_ROWS, D_MODEL), lambda l, j: (0, 0)),
            pl.BlockSpec((1, D_MODEL, tn), lambda l, j: (l, 0, j)),
            pl.BlockSpec((1, 1, tn), lambda l, j: (l, 0, j)),
        ],
        out_specs=pl.BlockSpec((1, MOD_ROWS, tn), lambda l, j: (l, 0, j)),
        out_shape=jax.ShapeDtypeStruct((DEPTH, MOD_ROWS, MOD_WIDTH), F32),
        compiler_params=_cparams(2),
        name="adaln",
    )(cond, w_ada, b_ada.reshape(DEPTH, 1, MOD_WIDTH))


def _ffn_weight_copy(w_hbm, stage_ref, sem_ref, layer, j, ff_axis):
    ff = pl.ds(j * FF_TILE, FF_TILE)
    src = w_hbm.at[layer, :, ff] if ff_axis == 1 else w_hbm.at[layer, ff, :]
    slot = j % FF_STAGE_SLOTS
    return pltpu.make_async_copy(src, stage_ref.at[slot], sem_ref.at[slot])


def _ffn_kernel(*refs, layer, mod_base, split_in, split_out):
    n_x = 2 if split_in else 1
    n_o = 2 if split_out else 1
    x_refs = refs[:n_x]
    mod_ref, g_ref, wg_hbm, wu_hbm, wd_hbm = refs[n_x:n_x + 5]
    o_refs = refs[n_x + 5:n_x + 5 + n_o]
    wg_bf, wu_bf, wd_bf, stg_g, stg_u, stg_d, sem_g, sem_u, sem_d = refs[n_x + 5 + n_o:]
    streams = ((wg_hbm, stg_g, sem_g, wg_bf, 1), (wu_hbm, stg_u, sem_u, wu_bf, 1),
               (wd_hbm, stg_d, sem_d, wd_bf, 0))

    i = pl.program_id(0)
    is_ctx = i < N_CTX_TOK // FFN_TILE
    if split_in:
        x = jnp.where(is_ctx, x_refs[0][...], x_refs[1][...])
    else:
        x = x_refs[0][...]
    row = _mod_row_of_tile(i, FFN_TILE)
    h = _norm_mod(x, g_ref[...], _mod_vec(mod_ref, row, mod_base),
                  _mod_vec(mod_ref, row, mod_base + 1)).astype(BF16)

    def start_chunk(j):
        for w_hbm, stg, sem, _, ff_axis in streams:
            _ffn_weight_copy(w_hbm, stg, sem, layer, j, ff_axis).start()

    def finish_chunk(j):
        for w_hbm, stg, sem, w_bf, ff_axis in streams:
            _ffn_weight_copy(w_hbm, stg, sem, layer, j, ff_axis).wait()
            w_bf[j] = stg[j % FF_STAGE_SLOTS].astype(BF16)

    def run(stream_weights):
        if stream_weights:
            for j in range(FF_STAGE_SLOTS):
                start_chunk(j)
        acc = None
        for j in range(FF_CHUNKS):
            if stream_weights:
                finish_chunk(j)
                if j + FF_STAGE_SLOTS < FF_CHUNKS:
                    start_chunk(j + FF_STAGE_SLOTS)
            a = jnp.dot(h, wg_bf[j], preferred_element_type=F32)
            b = jnp.dot(h, wu_bf[j], preferred_element_type=F32)
            act = (a * jax.nn.sigmoid(a) * b).astype(BF16)
            y = jnp.dot(act, wd_bf[j], preferred_element_type=F32)
            acc = y if acc is None else acc + y
        res = x + 0.5 * _mod_vec(mod_ref, row, mod_base + 2) * acc
        if split_out:
            @pl.when(is_ctx)
            def _():
                o_refs[0][...] = res

            @pl.when(jnp.logical_not(is_ctx))
            def _():
                o_refs[1][...] = res
        else:
            o_refs[0][...] = res

    @pl.when(i == 0)
    def _():
        run(True)

    @pl.when(i > 0)
    def _():
        run(False)


def _ffn(xs, mod, g, wg, wu, wd, layer, mod_base, split_out=False):
    tm = FFN_TILE
    n_ctx_tiles = N_CTX_TOK // tm
    split_in = len(xs) == 2
    tok = pl.BlockSpec((tm, D_MODEL), lambda i: (i, 0))
    ctx_tok = pl.BlockSpec((tm, D_MODEL), lambda i: (jnp.minimum(i, n_ctx_tiles - 1), 0))
    smp_tok = pl.BlockSpec((tm, D_MODEL), lambda i: (jnp.maximum(i - n_ctx_tiles, 0), 0))
    full = jax.ShapeDtypeStruct((N_TOK, D_MODEL), F32)
    pair = [jax.ShapeDtypeStruct((N_CTX_TOK, D_MODEL), F32), jax.ShapeDtypeStruct((N_SMP_TOK, D_MODEL), F32)]
    hbm = pl.BlockSpec(memory_space=pl.ANY)
    return pl.pallas_call(
        functools.partial(_ffn_kernel, layer=layer, mod_base=mod_base, split_in=split_in,
                          split_out=split_out),
        grid=(N_TOK // tm,),
        in_specs=([ctx_tok, smp_tok] if split_in else [tok]) + [
            _mod_spec(layer, 1),
            pl.BlockSpec((1, D_MODEL), lambda i: (0, 0)),
            hbm, hbm, hbm,
        ],
        out_specs=[ctx_tok, smp_tok] if split_out else tok,
        out_shape=pair if split_out else full,
        scratch_shapes=[
            pltpu.VMEM((FF_CHUNKS, D_MODEL, FF_TILE), BF16),
            pltpu.VMEM((FF_CHUNKS, D_MODEL, FF_TILE), BF16),
            pltpu.VMEM((FF_CHUNKS, FF_TILE, D_MODEL), BF16),
            pltpu.VMEM((FF_STAGE_SLOTS, D_MODEL, FF_TILE), F32),
            pltpu.VMEM((FF_STAGE_SLOTS, D_MODEL, FF_TILE), F32),
            pltpu.VMEM((FF_STAGE_SLOTS, FF_TILE, D_MODEL), F32),
            pltpu.SemaphoreType.DMA((FF_STAGE_SLOTS,)),
            pltpu.SemaphoreType.DMA((FF_STAGE_SLOTS,)),
            pltpu.SemaphoreType.DMA((FF_STAGE_SLOTS,)),
        ],
        compiler_params=_cparams(1),
        name="ffn",
    )(*xs, mod, g.reshape(1, D_MODEL), wg, wu, wd)


def _head_rms_norm(z, g, ones_bd):
    z2 = z * z
    hi = z2.astype(BF16)
    lo = (z2 - hi.astype(F32)).astype(BF16)
    ss = (jnp.dot(hi, ones_bd, preferred_element_type=F32)
          + jnp.dot(lo, ones_bd, preferred_element_type=F32))
    return z * lax.rsqrt(ss * (1.0 / HEAD_DIM) + EPS) * g


def _proj_kernel(x_ref, mod_ref, g_ref, w_ref, qg_ref, kg_ref, ones_ref,
                 q_ref, k_ref, v_ref, xb_ref, gb_ref, kout_ref, vout_ref, w_bf_ref):
    i = pl.program_id(0)

    @pl.when(i == 0)
    def _():
        w_bf_ref[...] = w_ref[...].astype(BF16)

    x = x_ref[...]
    row = _mod_row_of_tile(i)
    h = _norm_mod(x, g_ref[...], _mod_vec(mod_ref, row, 3), _mod_vec(mod_ref, row, 4)).astype(BF16)
    proj = jnp.dot(h, w_bf_ref[...], preferred_element_type=F32)
    ones_bd = ones_ref[...]
    q = _head_rms_norm(proj[:, :NA_WIDTH], qg_ref[...], ones_bd) * (HEAD_DIM ** -0.5)
    k = _head_rms_norm(proj[:, NA_WIDTH:2 * NA_WIDTH], kg_ref[...], ones_bd)
    v = proj[:, 2 * NA_WIDTH:3 * NA_WIDTH]
    q_ref[...] = q.astype(BF16)
    k_ref[...] = k.astype(BF16)
    v_ref[...] = v.astype(BF16)
    xb_ref[...] = proj[:, 3 * NA_WIDTH:3 * NA_WIDTH + LRU_WIDTH]
    gb_ref[...] = proj[:, 3 * NA_WIDTH + LRU_WIDTH:]

    @pl.when(i < N_CTX_TILES)
    def _():
        kout_ref[...] = k.reshape(TOKEN_TILE, NA_HEADS, HEAD_DIM)
        vout_ref[...] = v.reshape(TOKEN_TILE, NA_HEADS, HEAD_DIM)


def _proj(x, mod, layer, g, w_in, q_g, k_g):
    tm = TOKEN_TILE
    head = np.arange(NA_WIDTH) // HEAD_DIM
    ones_bd = jnp.asarray((head[:, None] == head[None, :]).astype(np.float32), dtype=BF16)
    tok = lambda i: (i, 0)
    const = lambda i: (0, 0)
    act_f32 = jax.ShapeDtypeStruct((N_TOK, NA_WIDTH), F32)
    act_bf16 = jax.ShapeDtypeStruct((N_TOK, NA_WIDTH), BF16)
    cache = jax.ShapeDtypeStruct((N_CTX_TOK, NA_HEADS, HEAD_DIM), F32)
    cache_spec = pl.BlockSpec((tm, NA_HEADS, HEAD_DIM), lambda i: (jnp.minimum(i, N_CTX_TILES - 1), 0, 0))
    return pl.pallas_call(
        _proj_kernel,
        grid=(N_TOK // tm,),
        in_specs=[
            pl.BlockSpec((tm, D_MODEL), tok),
            _mod_spec(layer, 1),
            pl.BlockSpec((1, D_MODEL), const),
            _resident((D_MODEL, IN_WIDTH), const),
            pl.BlockSpec((1, NA_WIDTH), const),
            pl.BlockSpec((1, NA_WIDTH), const),
            _resident((NA_WIDTH, NA_WIDTH), const),
        ],
        out_specs=[pl.BlockSpec((tm, NA_WIDTH), tok)] * 5 + [cache_spec, cache_spec],
        out_shape=[act_bf16, act_bf16, act_bf16, act_f32, act_f32, cache, cache],
        scratch_shapes=[pltpu.VMEM((D_MODEL, IN_WIDTH), BF16)],
        compiler_params=_cparams(1),
        name="mixer_in_proj",
    )(x, mod, g.reshape(1, D_MODEL), w_in,
      jnp.tile(q_g, NA_HEADS).reshape(1, NA_WIDTH), jnp.tile(k_g, NA_HEADS).reshape(1, NA_WIDTH),
      ones_bd)


def _head_masks():
    lane = lax.broadcasted_iota(jnp.int32, (1, 2 * HEAD_DIM), 1)
    return [lane < HEAD_DIM, lane >= HEAD_DIM]


def _qk(q, k):
    return lax.dot_general(q, k, (((1,), (1,)), ((), ())), preferred_element_type=F32)


def _ctx_attn_kernel(q_ref, k_ref, v_ref, o_ref):
    masks = _head_masks()
    for p in range(NA_HEADS // 2):
        sl = slice(2 * HEAD_DIM * p, 2 * HEAD_DIM * (p + 1))
        q2 = q_ref[:, sl]
        k2 = k_ref[:, sl]
        v2 = v_ref[:, sl]
        out = None
        for e in range(2):
            qm = jnp.where(masks[e], q2, jnp.zeros_like(q2))
            s = _qk(qm, k2)
            pe = jnp.exp(s - jnp.max(s, axis=-1, keepdims=True))
            den = jnp.sum(pe, axis=-1, keepdims=True)
            o = jnp.dot(pe.astype(BF16), v2, preferred_element_type=F32) / den
            out = o if out is None else jnp.where(masks[e], o, out)
        o_ref[:, sl] = out.astype(BF16)


def _ctx_attn(q, k, v):
    blk = pl.BlockSpec((SEQ, NA_WIDTH), lambda b: (b, 0))
    return pl.pallas_call(
        _ctx_attn_kernel,
        grid=(BATCH,),
        in_specs=[blk, blk, blk],
        out_specs=blk,
        out_shape=jax.ShapeDtypeStruct((N_CTX_TOK, NA_WIDTH), BF16),
        compiler_params=_cparams(1),
        name="ctx_attention",
    )(q, k, v)


def _na_build_bias_table(rpb_ref, table_ref):
    qc = lax.broadcasted_iota(jnp.int32, (GRID_W, LANES), 0)
    lane = lax.broadcasted_iota(jnp.int32, (GRID_W, LANES), 1)
    kc = lane % GRID_W
    col_start = jnp.clip(qc - WIN_W // 2, 0, GRID_W - WIN_W)
    col_in = (kc >= col_start) & (kc < col_start + WIN_W)
    neg = jnp.full((GRID_W, LANES), -jnp.inf, F32)

    def toeplitz(h, dr, lane0):
        if dr < 0 or dr >= N_DR:
            return neg
        w = jnp.broadcast_to(rpb_ref[h, dr:dr + 1, :], (GRID_W, LANES))
        return pltpu.roll(w, (lane0 - (WIN_W - 1)) % LANES, 1, stride=1, stride_axis=0)

    for h in range(NA_HEADS):
        for i in range(N_DR_PAIRS):
            t = jnp.where(lane < GRID_W, toeplitz(h, i - 1, 0), toeplitz(h, i, GRID_W))
            table_ref[h, i] = jnp.where(col_in, t, neg)


def _na_kernel(q_ref, k_ref, v_ref, kc_ref, vc_ref, rpb_ref, o_ref, table_ref):
    b = pl.program_id(0)
    g = pl.program_id(1)

    @pl.when((b == 0) & (g == 0))
    def _():
        _na_build_bias_table(rpb_ref, table_ref)

    win_row0 = jnp.where(g < NA_GROUPS // 2, 0, ROWS - NA_K_ROWS)
    start = pl.multiple_of(win_row0 * GRID_W, GRID_W)
    q_row = g * NA_Q_ROWS + lax.broadcasted_iota(jnp.int32, (NA_Q, 1), 0) // GRID_W
    k_row = win_row0 + lax.broadcasted_iota(jnp.int32, (1, NA_K), 1) // GRID_W
    row_start = jnp.clip(q_row - KH // 2, 0, ROWS - KH)
    row_in = (k_row >= row_start) & (k_row < row_start + KH)
    masks = _head_masks()
    for p in range(NA_HEADS // 2):
        sl = slice(2 * HEAD_DIM * p, 2 * HEAD_DIM * (p + 1))
        q2 = q_ref[:, sl]
        kl = k_ref[pl.ds(start, NA_K), sl]
        vl = v_ref[pl.ds(start, NA_K), sl]
        kc = kc_ref[0, :, sl].astype(BF16)
        vc = vc_ref[0, :, sl].astype(BF16)
        out = None
        for e in range(2):
            head = 2 * p + e
            bias_rows = []
            for a in range(NA_Q_ROWS):
                tiles = []
                for m in range(NA_K_ROWS // 2):
                    dr = win_row0 + 2 * m - (g * NA_Q_ROWS + a) + (WIN_H - 1)
                    tiles.append(table_ref[head, jnp.clip(dr + 1, 0, N_DR_PAIRS - 1)])
                bias_rows.append(jnp.concatenate(tiles, axis=1))
            bias = jnp.concatenate(bias_rows, axis=0)
            qm = jnp.where(masks[e], q2, jnp.zeros_like(q2))
            s_loc = jnp.where(row_in, _qk(qm, kl) + bias, -jnp.inf)
            s_ctx = _qk(qm, kc)
            m_max = jnp.maximum(jnp.max(s_loc, axis=-1, keepdims=True),
                                jnp.max(s_ctx, axis=-1, keepdims=True))
            p_loc = jnp.exp(s_loc - m_max)
            p_ctx = jnp.exp(s_ctx - m_max)
            den = jnp.sum(p_loc, axis=-1, keepdims=True) + jnp.sum(p_ctx, axis=-1, keepdims=True)
            o = (jnp.dot(p_loc.astype(BF16), vl, preferred_element_type=F32)
                 + jnp.dot(p_ctx.astype(BF16), vc, preferred_element_type=F32)) / den
            out = o if out is None else jnp.where(masks[e], o, out)
        o_ref[:, sl] = out.astype(BF16)


def _na_attn(q, k, v, k_ctx, v_ctx, rpb_e):
    smp_blk0 = N_CTX_TOK // DEC_SEQ
    q_blk0 = N_CTX_TOK // NA_Q
    kv = pl.BlockSpec((DEC_SEQ, NA_WIDTH), lambda b, g: (smp_blk0 + b, 0))
    ctx = pl.BlockSpec((1, PAST_LEN, NA_WIDTH), lambda b, g: (b, 0, 0))
    rpb_pad = jnp.pad(rpb_e.astype(F32), ((0, 0), (0, 0), (0, LANES - N_DC)))
    return pl.pallas_call(
        _na_kernel,
        grid=(DEC_BATCH, NA_GROUPS),
        in_specs=[
            pl.BlockSpec((NA_Q, NA_WIDTH), lambda b, g: (q_blk0 + b * NA_GROUPS + g, 0)),
            kv, kv, ctx, ctx,
            pl.BlockSpec((NA_HEADS, N_DR, LANES), lambda b, g: (0, 0, 0)),
        ],
        out_specs=pl.BlockSpec((NA_Q, NA_WIDTH), lambda b, g: (b * NA_GROUPS + g, 0)),
        out_shape=jax.ShapeDtypeStruct((N_SMP_TOK, NA_WIDTH), BF16),
        scratch_shapes=[pltpu.VMEM((NA_HEADS, N_DR_PAIRS, GRID_W, LANES), F32)],
        compiler_params=_cparams(2),
        name="neighbourhood_attention",
    )(q, k, v, k_ctx, v_ctx, rpb_pad)


def _expm1(y):
    e = jnp.exp(y)
    return jnp.where(jnp.abs(y) < 0.5, jnp.tanh(0.5 * y) * (e + 1.0), e - 1.0)


def _lru_kernel(xb_ref, gb_ref, cw_ref, cb_ref, w_ref, b_ref, lam_ref, h0_ref,
                y_ref, hl_ref, af_ref, uf_ref, ab_ref, ub_ref):
    x = xb_ref[...]
    t_len, width = x.shape
    row = lax.broadcasted_iota(jnp.int32, (t_len, 1), 0)

    def shifted(z, s, fill):
        rolled = pltpu.roll(z, (-s) % t_len, axis=0)
        ok = (row + s >= 0) & (row + s < t_len)
        return jnp.where(ok, rolled, fill)

    left = (CONV_W - 1) // 2
    xc = cb_ref[...]
    for j in range(CONV_W):
        tap = x if j == left else shifted(x, j - left, 0.0)
        xc = xc + tap * cw_ref[j:j + 1, :]

    gates = jnp.dot(xc.astype(BF16), w_ref[0], preferred_element_type=F32) + b_ref[0]
    in_block = row % SUBLANES
    for d, (a_ref, u_ref) in enumerate(((af_ref, uf_ref), (ab_ref, ub_ref))):
        r_gate = jax.nn.sigmoid(gates[:, (2 * d) * width:(2 * d + 1) * width])
        i_gate = jax.nn.sigmoid(gates[:, (2 * d + 1) * width:(2 * d + 2) * width])
        lam = lam_ref[0, d:d + 1, :]
        log_sig = jnp.minimum(lam, 0.0) - jnp.log1p(jnp.exp(-jnp.abs(lam)))
        log_a = LRU_C * r_gate * log_sig
        a = jnp.exp(log_a)
        u = jnp.sqrt(-_expm1(2.0 * log_a)) * (i_gate * xc)
        sign = -1 if d == 0 else 1
        step = 1
        while step < SUBLANES:
            if d == 0:
                ok = in_block >= step
            else:
                ok = in_block < SUBLANES - step
            a_prev = jnp.where(ok, pltpu.roll(a, (-sign * step) % t_len, axis=0), 1.0)
            u_prev = jnp.where(ok, pltpu.roll(u, (-sign * step) % t_len, axis=0), 0.0)
            u = u + a * u_prev
            a = a * a_prev
            step *= 2
        a_ref[...] = a
        u_ref[...] = u

    n_blk = t_len // SUBLANES

    def body(i, carry):
        cf, cb = carry
        f0 = pl.multiple_of(i * SUBLANES, SUBLANES)
        b0 = pl.multiple_of((n_blk - 1 - i) * SUBLANES, SUBLANES)
        hf = uf_ref[pl.ds(f0, SUBLANES), :] + af_ref[pl.ds(f0, SUBLANES), :] * cf
        hb = ub_ref[pl.ds(b0, SUBLANES), :] + ab_ref[pl.ds(b0, SUBLANES), :] * cb
        uf_ref[pl.ds(f0, SUBLANES), :] = hf
        ub_ref[pl.ds(b0, SUBLANES), :] = hb
        cf = jnp.broadcast_to(hf[SUBLANES - 1:SUBLANES, :], (SUBLANES, width))
        cb = jnp.broadcast_to(hb[0:1, :], (SUBLANES, width))
        return cf, cb

    c0f = jnp.broadcast_to(h0_ref[0, 0:1, :], (SUBLANES, width))
    c0b = jnp.broadcast_to(h0_ref[0, 1:2, :], (SUBLANES, width))
    cf, cb = lax.fori_loop(0, n_blk, body, (c0f, c0b))
    hl_ref[0, 0:1, :] = cf[0:1, :]
    hl_ref[0, 1:2, :] = cb[0:1, :]
    y_ref[...] = ((uf_ref[...] + ub_ref[...]) * jax.nn.gelu(gb_ref[...])).astype(BF16)


def _lru(xb, gb, conv_w, conv_b, w_bd, b_bd, lam, h0, n_seq, t_len, tok_blk0):
    n_half = 2
    cw = LRU_WIDTH // n_half
    seq = lambda s, c: (tok_blk0 + s, c)
    return pl.pallas_call(
        _lru_kernel,
        grid=(n_seq, n_half),
        in_specs=[
            pl.BlockSpec((t_len, cw), seq),
            pl.BlockSpec((t_len, cw), seq),
            pl.BlockSpec((CONV_W, cw), lambda s, c: (0, c)),
            pl.BlockSpec((1, cw), lambda s, c: (0, c)),
            pl.BlockSpec((1, cw, 4 * cw), lambda s, c: (c, 0, 0)),
            pl.BlockSpec((1, 1, 4 * cw), lambda s, c: (c, 0, 0)),
            pl.BlockSpec((1, 2, cw), lambda s, c: (0, 0, c)),
            pl.BlockSpec((1, 2, cw), lambda s, c: (s, 0, c)),
        ],
        out_specs=[
            pl.BlockSpec((t_len, cw), lambda s, c: (s, c)),
            pl.BlockSpec((1, 2, cw), lambda s, c: (s, 0, c)),
        ],
        out_shape=[
            jax.ShapeDtypeStruct((n_seq * t_len, LRU_WIDTH), BF16),
            jax.ShapeDtypeStruct((n_seq, 2, LRU_WIDTH), F32),
        ],
        scratch_shapes=[pltpu.VMEM((t_len, cw), F32)] * 4,
        compiler_params=_cparams(2),
        name="rglru",
    )(xb, gb, conv_w, conv_b.reshape(1, LRU_WIDTH), w_bd, b_bd, lam.reshape(1, 2, LRU_WIDTH), h0)


def _lru_gate_weights(w_r, b_r, w_i, b_i):
    cw = LRU_WIDTH // 2
    bpc = cw // LRU_BLOCK

    def dense_half(w, c):
        blocks = w[c * bpc:(c + 1) * bpc]
        eye = jnp.eye(bpc, dtype=w.dtype)
        return jnp.einsum('nij,nm->nimj', blocks, eye).reshape(cw, cw)

    w_halves, b_halves = [], []
    for c in range(2):
        w_halves.append(jnp.concatenate(
            [dense_half(w_r[0], c), dense_half(w_i[0], c), dense_half(w_r[1], c), dense_half(w_i[1], c)],
            axis=1))
        sl = slice(c * cw, (c + 1) * cw)
        b_halves.append(jnp.concatenate([b_r[0, sl], b_i[0, sl], b_r[1, sl], b_i[1, sl]]))
    return jnp.stack(w_halves).astype(BF16), jnp.stack(b_halves).reshape(2, 1, 4 * cw)


def _out_proj_kernel(x_ref, mod_ref, oc_ref, os_ref, yc_ref, ys_ref, w_ref, out_ref, w_bf_ref):
    i = pl.program_id(0)

    @pl.when(i == 0)
    def _():
        w_bf_ref[...] = w_ref[...].astype(BF16)

    is_ctx = i < N_CTX_TILES
    o = jnp.where(is_ctx, oc_ref[...], os_ref[...])
    yb = jnp.where(is_ctx, yc_ref[...], ys_ref[...])
    cat = jnp.concatenate([o, yb], axis=1)
    y = jnp.dot(cat, w_bf_ref[...], preferred_element_type=F32)
    out_ref[...] = x_ref[...] + _mod_vec(mod_ref, _mod_row_of_tile(i), 5) * y


def _out_proj(x, mod, layer, o_ctx, o_smp, yb_ctx, yb_smp, w_out):
    tm = TOKEN_TILE
    tok = lambda i: (i, 0)
    ctx_tok = pl.BlockSpec((tm, NA_WIDTH), lambda i: (jnp.minimum(i, N_CTX_TILES - 1), 0))
    smp_tok = pl.BlockSpec((tm, NA_WIDTH), lambda i: (jnp.maximum(i - N_CTX_TILES, 0), 0))
    return pl.pallas_call(
        _out_proj_kernel,
        grid=(N_TOK // tm,),
        in_specs=[
            pl.BlockSpec((tm, D_MODEL), tok),
            _mod_spec(layer, 1),
            ctx_tok, smp_tok, ctx_tok, smp_tok,
            _resident((D_MODEL, D_MODEL), lambda i: (0, 0)),
        ],
        out_specs=pl.BlockSpec((tm, D_MODEL), tok),
        out_shape=jax.ShapeDtypeStruct((N_TOK, D_MODEL), F32),
        scratch_shapes=[pltpu.VMEM((D_MODEL, D_MODEL), BF16)],
        compiler_params=_cparams(1),
        name="mixer_out_proj",
    )(x, mod, o_ctx, o_smp, yb_ctx, yb_smp, w_out)


def _fourier_kernel(x_ref, mod_ref, g_ref, cs_ref, ct_ref, w_ref, o_ref, w_bf_ref, *, mod_row0):
    @pl.when(pl.program_id(0) == 0)
    def _():
        w_bf_ref[...] = w_ref[...].astype(BF16)

    x = x_ref[...]
    t_len = x.shape[0]
    row = mod_row0 + pl.program_id(0) if mod_row0 else 0
    h = _norm_mod(x, g_ref[...], _mod_vec(mod_ref, row, 3), _mod_vec(mod_ref, row, 4)).astype(BF16)
    cos_parts, sin_parts = [], []
    for g in range(FOURIER_GROUPS):
        ab = jnp.dot(h[:, g * GROUP_W:(g + 1) * GROUP_W], cs_ref[...], preferred_element_type=F32)
        cos_parts.append(ab[:, :GROUP_W])
        sin_parts.append(ab[:, GROUP_W:])
    stacked = jnp.concatenate(
        [jnp.concatenate(cos_parts, axis=1), jnp.concatenate(sin_parts, axis=1)], axis=0).astype(BF16)
    f = jnp.dot(ct_ref[...], stacked, preferred_element_type=F32) * ((t_len * GROUP_W) ** -0.5)
    y = jnp.dot(f.astype(BF16), w_bf_ref[...], preferred_element_type=F32)
    o_ref[...] = x + _mod_vec(mod_ref, row, 5) * y


def _dft_tables(t_len):
    def cos_sin(n):
        jk = np.outer(np.arange(n), np.arange(n)) % n
        ang = 2.0 * np.pi * jk.astype(np.float64) / n
        return np.cos(ang), np.sin(ang)

    cc, sc = cos_sin(GROUP_W)
    ct, st = cos_sin(t_len)
    chan = jnp.asarray(np.concatenate([cc, sc], axis=1).astype(np.float32)).astype(BF16)
    time = jnp.asarray(np.concatenate([ct, -st], axis=1).astype(np.float32)).astype(BF16)
    return chan, time


def _fourier(x, mod, layer, g, w_out, n_seq, t_len, tok_blk0, mod_row0):
    chan, time = _dft_tables(t_len)
    seq = lambda s: (tok_blk0 + s, 0)
    const = lambda s: (0, 0)
    return pl.pallas_call(
        functools.partial(_fourier_kernel, mod_row0=mod_row0),
        grid=(n_seq,),
        in_specs=[
            pl.BlockSpec((t_len, D_MODEL), seq),
            _mod_spec(layer, 1),
            pl.BlockSpec((1, D_MODEL), const),
            _resident((GROUP_W, 2 * GROUP_W), const),
            _resident((t_len, 2 * t_len), const),
            _resident((D_MODEL, D_MODEL), const),
        ],
        out_specs=pl.BlockSpec((t_len, D_MODEL), seq),
        out_shape=jax.ShapeDtypeStruct((N_TOK, D_MODEL), F32),
        input_output_aliases={0: 0},
        scratch_shapes=[pltpu.VMEM((D_MODEL, D_MODEL), BF16)],
        compiler_params=_cparams(1),
        name="fourier_mixer",
    )(x, mod, g.reshape(1, D_MODEL), chan, time, w_out)


def kernel(x_prompt, x_sample, cache_k, cache_v, state_lru_fwd, state_lru_bwd, c, c_ctx, w_ada, b_ada, norm_g, ffn1_gate, ffn1_up, ffn1_down, ffn2_gate, ffn2_up, ffn2_down, w_in, q_norm_g, k_norm_g, rpb, conv_w, conv_b, lru_w_r, lru_b_r, lru_w_i, lru_b_i, lru_lambda, w_out_ab, w_out_c):
    assert DEPTH == 2, "one neighbourhood/RG-LRU layer followed by one Fourier layer"
    cond = jnp.concatenate(
        [c_ctx[None, :], c, jnp.zeros((MOD_ROWS - 1 - DEC_BATCH, D_MODEL), F32)], axis=0)
    mod = _adaln(cond, w_ada, b_ada)

    ffn1 = (ffn1_gate, ffn1_up, ffn1_down)
    ffn2 = (ffn2_gate, ffn2_up, ffn2_down)

    x = _ffn((x_prompt.reshape(N_CTX_TOK, D_MODEL), x_sample.reshape(N_SMP_TOK, D_MODEL)),
             mod, norm_g[0, 0], *ffn1, 0, 0)
    q, k, v, xb, gb, new_k, new_v = _proj(x, mod, 0, norm_g[0, 1], w_in[0], q_norm_g[0], k_norm_g[0])
    o_ctx = _ctx_attn(q, k, v)
    o_smp = _na_attn(q, k, v,
                     cache_k[:, 0].reshape(DEC_BATCH, PAST_LEN, NA_WIDTH),
                     cache_v[:, 0].reshape(DEC_BATCH, PAST_LEN, NA_WIDTH), rpb[0])
    w_bd, b_bd = _lru_gate_weights(lru_w_r[0], lru_b_r[0], lru_w_i[0], lru_b_i[0])
    h0_ctx = jnp.zeros((BATCH, 2, LRU_WIDTH), F32)
    h0_smp = jnp.stack([state_lru_fwd[:, 0], state_lru_bwd[:, 0]], axis=1).astype(F32)
    yb_ctx, hl_ctx = _lru(xb, gb, conv_w[0], conv_b[0], w_bd, b_bd, lru_lambda[0],
                          h0_ctx, BATCH, SEQ, 0)
    yb_smp, _ = _lru(xb, gb, conv_w[0], conv_b[0], w_bd, b_bd, lru_lambda[0],
                     h0_smp, DEC_BATCH, DEC_SEQ, N_CTX_TOK // DEC_SEQ)
    x = _out_proj(x, mod, 0, o_ctx, o_smp, yb_ctx, yb_smp, w_out_ab[0])
    x = _ffn((x,), mod, norm_g[0, 2], *ffn2, 0, 6)

    x = _ffn((x,), mod, norm_g[1, 0], *ffn1, 1, 0)
    w_c = w_out_c[0]
    x = _fourier(x, mod, 1, norm_g[1, 1], w_c, BATCH, SEQ, 0, 0)
    x = _fourier(x, mod, 1, norm_g[1, 1], w_c, DEC_BATCH, DEC_SEQ, N_CTX_TOK // DEC_SEQ, 1)
    y_prompt, y_sample = _ffn((x,), mod, norm_g[1, 2], *ffn2, 1, 6, split_out=True)

    return (y_prompt.reshape(BATCH, SEQ, D_MODEL), y_sample.reshape(DEC_BATCH, DEC_SEQ, D_MODEL),
            new_k.reshape(BATCH, 1, SEQ, NA_HEADS, HEAD_DIM), new_v.reshape(BATCH, 1, SEQ, NA_HEADS, HEAD_DIM),
            hl_ctx[:, 0].reshape(BATCH, 1, LRU_WIDTH), hl_ctx[:, 1].reshape(BATCH, 1, LRU_WIDTH))
```

```python
import functools

import numpy as np
import jax
import jax.numpy as jnp
from jax import lax
from jax.experimental import pallas as pl
from jax.experimental.pallas import tpu as pltpu

F32 = jnp.float32
BF16 = jnp.bfloat16

D_MODEL = 1024
BATCH = 16
SEQ = 256
DEPTH = 2
DEC_BATCH = 2
DEC_SEQ = 1024
PAST_LEN = 256
GRID_W = 64
HEAD_DIM = 64
NA_WIDTH = 512
NA_HEADS = 8
WIN_H = 8
WIN_W = 16
LRU_WIDTH = 512
LRU_BLOCKS = 8
LRU_BLOCK = 64
LRU_C = 8.0
LRU_SUB = 256
CONV_W = 4
FOURIER_GROUPS = 4
GROUP_W = D_MODEL // FOURIER_GROUPS
D_FF = 2816
N_MOD = 9
IN_WIDTH = 3 * NA_WIDTH + 2 * LRU_WIDTH
EPS = 1e-6

N_CTX_TOK = BATCH * SEQ
N_SMP_TOK = DEC_BATCH * DEC_SEQ
N_TOK = N_CTX_TOK + N_SMP_TOK
MOD_ROWS = 8
MOD_WIDTH = N_MOD * D_MODEL
ROWS = DEC_SEQ // GRID_W
KH = min(WIN_H, ROWS)

TOKEN_TILE = 512
N_CTX_TILES = N_CTX_TOK // TOKEN_TILE
FFN_TILE = 512
FF_TILE = 256
FF_CHUNKS = D_FF // FF_TILE
FF_STAGE_SLOTS = 2
SUBLANES = 8
LANES = 128
VMEM_LIMIT = 56 * 1024 * 1024

NA_Q_ROWS = 4
NA_GROUPS = ROWS // NA_Q_ROWS
NA_K_ROWS = 12
NA_Q = NA_Q_ROWS * GRID_W
NA_K = NA_K_ROWS * GRID_W
N_DR = 2 * WIN_H - 1
N_DC = 2 * WIN_W - 1
N_DR_PAIRS = N_DR + 1


def _cparams(n_axes):
    return pltpu.CompilerParams(
        dimension_semantics=("arbitrary",) * n_axes, vmem_limit_bytes=VMEM_LIMIT)


def _resident(block_shape, index_map):
    return pl.BlockSpec(block_shape, index_map, pipeline_mode=pl.Buffered(1))


def _mod_spec(layer, n_axes):
    if n_axes == 1:
        return _resident((None, MOD_ROWS, MOD_WIDTH), lambda i: (layer, 0, 0))
    return _resident((None, MOD_ROWS, MOD_WIDTH), lambda i, j: (layer, 0, 0))


def _mod_row_of_tile(i, tile=TOKEN_TILE):
    n_ctx_tiles = N_CTX_TOK // tile
    tiles_per_seq = DEC_SEQ // tile
    return jnp.where(i < n_ctx_tiles, 0, 1 + (i - n_ctx_tiles) // tiles_per_seq)


def _mod_vec(mod_ref, row, k):
    return mod_ref[pl.ds(row, 1), k * D_MODEL:(k + 1) * D_MODEL]


def _norm_mod(x, g, shift, scale):
    ms = jnp.mean(x * x, axis=-1, keepdims=True)
    y = x * lax.rsqrt(ms + EPS) * g
    return y * (1.0 + scale) + shift


def _adaln_kernel(c_ref, w_ref, b_ref, o_ref):
    c = c_ref[...]
    s = (c * jax.nn.sigmoid(c)).astype(BF16)
    w = w_ref[0].astype(BF16)
    o_ref[0] = jnp.dot(s, w, preferred_element_type=F32) + b_ref[0]


def _adaln(cond, w_ada, b_ada):
    tn = MOD_WIDTH // 4
    return pl.pallas_call(
        _adaln_kernel,
        grid=(DEPTH, MOD_WIDTH // tn),
        in_specs=[
            pl.BlockSpec((MOD_ROWS, D_MODEL), lambda l, j: (0, 0)),
            pl.BlockSpec((1, D_MODEL, tn), lambda l, j: (l, 0, j)),
            pl.BlockSpec((1, 1, tn), lambda l, j: (l, 0, j)),
        ],
        out_specs=pl.BlockSpec((1, MOD_ROWS, tn), lambda l, j: (l, 0, j)),
        out_shape=jax.ShapeDtypeStruct((DEPTH, MOD_ROWS, MOD_WIDTH), F32),
        compiler_params=_cparams(2),
        name="adaln",
    )(cond, w_ada, b_ada.reshape(DEPTH, 1, MOD_WIDTH))


def _ffn_weight_copy(w_hbm, stage_ref, sem_ref, layer, j, ff_axis):
    ff = pl.ds(j * FF_TILE, FF_TILE)
    src = w_hbm.at[layer, :, ff] if ff_axis == 1 else w_hbm.at[layer, ff, :]
    slot = j % FF_STAGE_SLOTS
    return pltpu.make_async_copy(src, stage_ref.at[slot], sem_ref.at[slot])


def _ffn_kernel(*refs, layer, mod_base, split_in, split_out):
    n_x = 2 if split_in else 1
    n_o = 2 if split_out else 1
    x_refs = refs[:n_x]
    mod_ref, g_ref, wg_hbm, wu_hbm, wd_hbm = refs[n_x:n_x + 5]
    o_refs = refs[n_x + 5:n_x + 5 + n_o]
    wg_bf, wu_bf, wd_bf, stg_g, stg_u, stg_d, sem_g, sem_u, sem_d = refs[n_x + 5 + n_o:]
    streams = ((wg_hbm, stg_g, sem_g, wg_bf, 1), (wu_hbm, stg_u, sem_u, wu_bf, 1),
               (wd_hbm, stg_d, sem_d, wd_bf, 0))

    i = pl.program_id(0)
    is_ctx = i < N_CTX_TOK // FFN_TILE
    if split_in:
        x = jnp.where(is_ctx, x_refs[0][...], x_refs[1][...])
    else:
        x = x_refs[0][...]
    row = _mod_row_of_tile(i, FFN_TILE)
    h = _norm_mod(x, g_ref[...], _mod_vec(mod_ref, row, mod_base),
                  _mod_vec(mod_ref, row, mod_base + 1)).astype(BF16)

    def start_chunk(j):
        for w_hbm, stg, sem, _, ff_axis in streams:
            _ffn_weight_copy(w_hbm, stg, sem, layer, j, ff_axis).start()

    def finish_chunk(j):
        for w_hbm, stg, sem, w_bf, ff_axis in streams:
            _ffn_weight_copy(w_hbm, stg, sem, layer, j, ff_axis).wait()
            w_bf[j] = stg[j % FF_STAGE_SLOTS].astype(BF16)

    def run(stream_weights):
        if stream_weights:
            for j in range(FF_STAGE_SLOTS):
                start_chunk(j)
        acc = None
        for j in range(FF_CHUNKS):
            if stream_weights:
                finish_chunk(j)
                if j + FF_STAGE_SLOTS < FF_CHUNKS:
                    start_chunk(j + FF_STAGE_SLOTS)
            a = jnp.dot(h, wg_bf[j], preferred_element_type=F32)
            b = jnp.dot(h, wu_bf[j], preferred_element_type=F32)
            act = (a * jax.nn.sigmoid(a) * b).astype(BF16)
            y = jnp.dot(act, wd_bf[j], preferred_element_type=F32)
            acc = y if acc is None else acc + y
        res = x + 0.5 * _mod_vec(mod_ref, row, mod_base + 2) * acc
        if split_out:
            @pl.when(is_ctx)
            def _():
                o_refs[0][...] = res

            @pl.when(jnp.logical_not(is_ctx))
            def _():
                o_refs[1][...] = res
        else:
            o_refs[0][...] = res

    @pl.when(i == 0)
    def _():
        run(True)

    @pl.when(i > 0)
    def _():
        run(False)


def _ffn(xs, mod, g, wg, wu, wd, layer, mod_base, split_out=False):
    tm = FFN_TILE
    n_ctx_tiles = N_CTX_TOK // tm
    split_in = len(xs) == 2
    tok = pl.BlockSpec((tm, D_MODEL), lambda i: (i, 0))
    ctx_tok = pl.BlockSpec((tm, D_MODEL), lambda i: (jnp.minimum(i, n_ctx_tiles - 1), 0))
    smp_tok = pl.BlockSpec((tm, D_MODEL), lambda i: (jnp.maximum(i - n_ctx_tiles, 0), 0))
    full = jax.ShapeDtypeStruct((N_TOK, D_MODEL), F32)
    pair = [jax.ShapeDtypeStruct((N_CTX_TOK, D_MODEL), F32), jax.ShapeDtypeStruct((N_SMP_TOK, D_MODEL), F32)]
    hbm = pl.BlockSpec(memory_space=pl.ANY)
    return pl.pallas_call(
        functools.partial(_ffn_kernel, layer=layer, mod_base=mod_base, split_in=split_in,
                          split_out=split_out),
        grid=(N_TOK // tm,),
        in_specs=([ctx_tok, smp_tok] if split_in else [tok]) + [
            _mod_spec(layer, 1),
            pl.BlockSpec((1, D_MODEL), lambda i: (0, 0)),
            hbm, hbm, hbm,
        ],
        out_specs=[ctx_tok, smp_tok] if split_out else tok,
        out_shape=pair if split_out else full,
        scratch_shapes=[
            pltpu.VMEM((FF_CHUNKS, D_MODEL, FF_TILE), BF16),
            pltpu.VMEM((FF_CHUNKS, D_MODEL, FF_TILE), BF16),
            pltpu.VMEM((FF_CHUNKS, FF_TILE, D_MODEL), BF16),
            pltpu.VMEM((FF_STAGE_SLOTS, D_MODEL, FF_TILE), F32),
            pltpu.VMEM((FF_STAGE_SLOTS, D_MODEL, FF_TILE), F32),
            pltpu.VMEM((FF_STAGE_SLOTS, FF_TILE, D_MODEL), F32),
            pltpu.SemaphoreType.DMA((FF_STAGE_SLOTS,)),
            pltpu.SemaphoreType.DMA((FF_STAGE_SLOTS,)),
            pltpu.SemaphoreType.DMA((FF_STAGE_SLOTS,)),
        ],
        compiler_params=_cparams(1),
        name="ffn",
    )(*xs, mod, g.reshape(1, D_MODEL), wg, wu, wd)


def _head_rms_norm(z, g, ones_bd):
    z2 = z * z
    hi = z2.astype(BF16)
    lo = (z2 - hi.astype(F32)).astype(BF16)
    n = ones_bd.shape[0]
    parts = []
    for c in range(z.shape[1] // n):
        sl = slice(c * n, (c + 1) * n)
        parts.append(jnp.dot(hi[:, sl], ones_bd, preferred_element_type=F32)
                     + jnp.dot(lo[:, sl], ones_bd, preferred_element_type=F32))
    ss = jnp.concatenate(parts, axis=1)
    return z * lax.rsqrt(ss * (1.0 / HEAD_DIM) + EPS) * g


def _proj_kernel(x_ref, mod_ref, g_ref, w_ref, qg_ref, kg_ref, ones_ref,
                 q_ref, k_ref, v_ref, xb_ref, gb_ref, kout_ref, vout_ref, w_bf_ref):
    i = pl.program_id(0)

    @pl.when(i == 0)
    def _():
        w_bf_ref[...] = w_ref[...].astype(BF16)

    x = x_ref[...]
    row = _mod_row_of_tile(i)
    h = _norm_mod(x, g_ref[...], _mod_vec(mod_ref, row, 3), _mod_vec(mod_ref, row, 4)).astype(BF16)

    def proj(part):
        return jnp.dot(h, w_bf_ref[:, part * NA_WIDTH:(part + 1) * NA_WIDTH], preferred_element_type=F32)

    ones_bd = ones_ref[...]
    q = _head_rms_norm(proj(0), qg_ref[...], ones_bd) * (HEAD_DIM ** -0.5)
    q_ref[...] = q.astype(BF16)
    k = _head_rms_norm(proj(1), kg_ref[...], ones_bd)
    k_ref[...] = k.astype(BF16)
    v = proj(2)
    v_ref[...] = v.astype(BF16)
    xb_ref[...] = proj(3)
    gb_ref[...] = proj(4)

    @pl.when(i < N_CTX_TILES)
    def _():
        kout_ref[...] = k.reshape(TOKEN_TILE, NA_HEADS, HEAD_DIM)
        vout_ref[...] = v.reshape(TOKEN_TILE, NA_HEADS, HEAD_DIM)


def _proj(x, mod, layer, g, w_in, q_g, k_g):
    tm = TOKEN_TILE
    head = np.arange(2 * LANES) // HEAD_DIM
    ones_bd = jnp.asarray((head[:, None] == head[None, :]).astype(np.float32), dtype=BF16)
    tok = lambda i: (i, 0)
    const = lambda i: (0, 0)
    act_f32 = jax.ShapeDtypeStruct((N_TOK, NA_WIDTH), F32)
    act_bf16 = jax.ShapeDtypeStruct((N_TOK, NA_WIDTH), BF16)
    cache = jax.ShapeDtypeStruct((N_CTX_TOK, NA_HEADS, HEAD_DIM), F32)
    cache_spec = pl.BlockSpec((tm, NA_HEADS, HEAD_DIM), lambda i: (jnp.minimum(i, N_CTX_TILES - 1), 0, 0))
    return pl.pallas_call(
        _proj_kernel,
        grid=(N_TOK // tm,),
        in_specs=[
            pl.BlockSpec((tm, D_MODEL), tok),
            _mod_spec(layer, 1),
            pl.BlockSpec((1, D_MODEL), const),
            _resident((D_MODEL, IN_WIDTH), const),
            pl.BlockSpec((1, NA_WIDTH), const),
            pl.BlockSpec((1, NA_WIDTH), const),
            _resident((2 * LANES, 2 * LANES), const),
        ],
        out_specs=[pl.BlockSpec((tm, NA_WIDTH), tok)] * 5 + [cache_spec, cache_spec],
        out_shape=[act_bf16, act_bf16, act_bf16, act_f32, act_f32, cache, cache],
        scratch_shapes=[pltpu.VMEM((D_MODEL, IN_WIDTH), BF16)],
        compiler_params=_cparams(1),
        name="mixer_in_proj",
    )(x, mod, g.reshape(1, D_MODEL), w_in,
      jnp.tile(q_g, NA_HEADS).reshape(1, NA_WIDTH), jnp.tile(k_g, NA_HEADS).reshape(1, NA_WIDTH),
      ones_bd)


def _head_masks():
    lane = lax.broadcasted_iota(jnp.int32, (1, 2 * HEAD_DIM), 1)
    return [lane < HEAD_DIM, lane >= HEAD_DIM]


def _ctx_attn_kernel(q_ref, k_ref, v_ref, o_ref):
    masks = _head_masks()
    for p in range(NA_HEADS // 2):
        sl = slice(2 * HEAD_DIM * p, 2 * HEAD_DIM * (p + 1))
        q2 = q_ref[:, sl]
        k2t = k_ref[:, sl].T
        v2 = v_ref[:, sl]
        out = None
        for e in range(2):
            qm = jnp.where(masks[e], q2, jnp.zeros_like(q2))
            s = jnp.dot(qm, k2t, preferred_element_type=F32)
            pe = jnp.exp(s - jnp.max(s, axis=-1, keepdims=True))
            den = jnp.sum(pe, axis=-1, keepdims=True)
            o = jnp.dot(pe.astype(BF16), v2, preferred_element_type=F32) / den
            out = o if out is None else jnp.where(masks[e], o, out)
        o_ref[:, sl] = out.astype(BF16)


def _ctx_attn(q, k, v):
    blk = pl.BlockSpec((SEQ, NA_WIDTH), lambda b: (b, 0))
    return pl.pallas_call(
        _ctx_attn_kernel,
        grid=(BATCH,),
        in_specs=[blk, blk, blk],
        out_specs=blk,
        out_shape=jax.ShapeDtypeStruct((N_CTX_TOK, NA_WIDTH), BF16),
        compiler_params=_cparams(1),
        name="ctx_attention",
    )(q, k, v)


def _na_build_bias_table(rpb_ref, table_ref):
    qc = lax.broadcasted_iota(jnp.int32, (GRID_W, LANES), 0)
    lane = lax.broadcasted_iota(jnp.int32, (GRID_W, LANES), 1)
    kc = lane % GRID_W
    col_start = jnp.clip(qc - WIN_W // 2, 0, GRID_W - WIN_W)
    col_in = (kc >= col_start) & (kc < col_start + WIN_W)
    neg = jnp.full((GRID_W, LANES), -jnp.inf, F32)

    def toeplitz(h, dr, lane0):
        if dr < 0 or dr >= N_DR:
            return neg
        w = jnp.broadcast_to(rpb_ref[h, dr:dr + 1, :], (GRID_W, LANES))
        return pltpu.roll(w, (lane0 - (WIN_W - 1)) % LANES, 1, stride=1, stride_axis=0)

    for h in range(NA_HEADS):
        for i in range(N_DR_PAIRS):
            t = jnp.where(lane < GRID_W, toeplitz(h, i - 1, 0), toeplitz(h, i, GRID_W))
            table_ref[h, i] = jnp.where(col_in, t, neg)


def _na_kernel(q_ref, k_ref, v_ref, kc_ref, vc_ref, rpb_ref, o_ref, table_ref):
    b = pl.program_id(0)
    g = pl.program_id(1)

    @pl.when((b == 0) & (g == 0))
    def _():
        _na_build_bias_table(rpb_ref, table_ref)

    win_row0 = jnp.where(g < NA_GROUPS // 2, 0, ROWS - NA_K_ROWS)
    start = pl.multiple_of(win_row0 * GRID_W, GRID_W)
    q_row = g * NA_Q_ROWS + lax.broadcasted_iota(jnp.int32, (NA_Q, 1), 0) // GRID_W
    k_row = win_row0 + lax.broadcasted_iota(jnp.int32, (1, NA_K), 1) // GRID_W
    row_start = jnp.clip(q_row - KH // 2, 0, ROWS - KH)
    row_in = (k_row >= row_start) & (k_row < row_start + KH)
    masks = _head_masks()
    for p in range(NA_HEADS // 2):
        sl = slice(2 * HEAD_DIM * p, 2 * HEAD_DIM * (p + 1))
        q2 = q_ref[:, sl]
        klt = k_ref[pl.ds(start, NA_K), sl].T
        vl = v_ref[pl.ds(start, NA_K), sl]
        kct = kc_ref[0, :, sl].astype(BF16).T
        vc = vc_ref[0, :, sl].astype(BF16)
        out = None
        for e in range(2):
            head = 2 * p + e
            bias_rows = []
            for a in range(NA_Q_ROWS):
                tiles = []
                for m in range(NA_K_ROWS // 2):
                    dr = win_row0 + 2 * m - (g * NA_Q_ROWS + a) + (WIN_H - 1)
                    tiles.append(table_ref[head, jnp.clip(dr + 1, 0, N_DR_PAIRS - 1)])
                bias_rows.append(jnp.concatenate(tiles, axis=1))
            bias = jnp.concatenate(bias_rows, axis=0)
            qm = jnp.where(masks[e], q2, jnp.zeros_like(q2))
            s_loc = jnp.where(row_in, jnp.dot(qm, klt, preferred_element_type=F32) + bias, -jnp.inf)
            s_ctx = jnp.dot(qm, kct, preferred_element_type=F32)
            m_max = jnp.maximum(jnp.max(s_loc, axis=-1, keepdims=True),
                                jnp.max(s_ctx, axis=-1, keepdims=True))
            p_loc = jnp.exp(s_loc - m_max)
            p_ctx = jnp.exp(s_ctx - m_max)
            den = jnp.sum(p_loc, axis=-1, keepdims=True) + jnp.sum(p_ctx, axis=-1, keepdims=True)
            o = (jnp.dot(p_loc.astype(BF16), vl, preferred_element_type=F32)
                 + jnp.dot(p_ctx.astype(BF16), vc, preferred_element_type=F32)) / den
            out = o if out is None else jnp.where(masks[e], o, out)
        o_ref[:, sl] = out.astype(BF16)


def _na_attn(q, k, v, k_ctx, v_ctx, rpb_e):
    smp_blk0 = N_CTX_TOK // DEC_SEQ
    q_blk0 = N_CTX_TOK // NA_Q
    kv = pl.BlockSpec((DEC_SEQ, NA_WIDTH), lambda b, g: (smp_blk0 + b, 0))
    ctx = pl.BlockSpec((1, PAST_LEN, NA_WIDTH), lambda b, g: (b, 0, 0))
    rpb_pad = jnp.pad(rpb_e.astype(F32), ((0, 0), (0, 0), (0, LANES - N_DC)))
    return pl.pallas_call(
        _na_kernel,
        grid=(DEC_BATCH, NA_GROUPS),
        in_specs=[
            pl.BlockSpec((NA_Q, NA_WIDTH), lambda b, g: (q_blk0 + b * NA_GROUPS + g, 0)),
            kv, kv, ctx, ctx,
            pl.BlockSpec((NA_HEADS, N_DR, LANES), lambda b, g: (0, 0, 0)),
        ],
        out_specs=pl.BlockSpec((NA_Q, NA_WIDTH), lambda b, g: (b * NA_GROUPS + g, 0)),
        out_shape=jax.ShapeDtypeStruct((N_SMP_TOK, NA_WIDTH), BF16),
        scratch_shapes=[pltpu.VMEM((NA_HEADS, N_DR_PAIRS, GRID_W, LANES), F32)],
        compiler_params=_cparams(2),
        name="neighbourhood_attention",
    )(q, k, v, k_ctx, v_ctx, rpb_pad)


def _sigmoid(x):
    return 0.5 * jnp.tanh(0.5 * x) + 0.5


def _lru_kernel(xb_ref, gb_ref, cw_ref, cb_ref, w_ref, b_ref, lam_ref, h0_ref,
                y_ref, hl_ref, af_ref, uf_ref, ab_ref, ub_ref):
    t_len, width = xb_ref.shape
    n_blk = t_len // SUBLANES
    row = lax.broadcasted_iota(jnp.int32, (t_len, 1), 0)
    in_block = lax.broadcasted_iota(jnp.int32, (1, SUBLANES, 1), 1)

    def shifted(z, s):
        rolled = pltpu.roll(z, (-s) % t_len, axis=0)
        ok = (row + s >= 0) & (row + s < t_len)
        return jnp.where(ok, rolled, 0.0)

    left = (CONV_W - 1) // 2
    for c in range(width // LRU_SUB):
        cs = slice(c * LRU_SUB, (c + 1) * LRU_SUB)
        x = xb_ref[:, cs]
        xc = cb_ref[:, cs]
        for j in range(CONV_W):
            tap = x if j == left else shifted(x, j - left)
            xc = xc + tap * cw_ref[j:j + 1, cs]
        gates = jnp.dot(xc.astype(BF16), w_ref[c], preferred_element_type=F32) + b_ref[c]
        for d, (a_ref, u_ref) in enumerate(((af_ref, uf_ref), (ab_ref, ub_ref))):
            r_gate = _sigmoid(gates[:, (2 * d) * LRU_SUB:(2 * d + 1) * LRU_SUB])
            i_gate = _sigmoid(gates[:, (2 * d + 1) * LRU_SUB:(2 * d + 2) * LRU_SUB])
            lam = lam_ref[0, d:d + 1, cs]
            log_sig = jnp.minimum(lam, 0.0) - jnp.log1p(jnp.exp(-jnp.abs(lam)))
            log_a = LRU_C * r_gate * log_sig
            a = jnp.exp(log_a)
            var = -jnp.tanh(log_a) * (a * a + 1.0)
            u = jnp.where(var > 0.0, var * lax.rsqrt(var), 0.0) * (i_gate * xc)
            a = a.reshape(n_blk, SUBLANES, LRU_SUB)
            u = u.reshape(n_blk, SUBLANES, LRU_SUB)
            step = 1
            while step < SUBLANES:
                if d == 0:
                    ok, shift = in_block >= step, step
                else:
                    ok, shift = in_block < SUBLANES - step, SUBLANES - step
                a_prev = jnp.where(ok, pltpu.roll(a, shift, axis=1), 1.0)
                u_prev = jnp.where(ok, pltpu.roll(u, shift, axis=1), 0.0)
                u = u + a * u_prev
                a = a * a_prev
                step *= 2
            a_ref[:, cs] = a.reshape(t_len, LRU_SUB)
            u_ref[:, cs] = u.reshape(t_len, LRU_SUB)

    def body(i, carry):
        cf, cb = carry
        f0 = pl.multiple_of(i * SUBLANES, SUBLANES)
        b0 = pl.multiple_of((n_blk - 1 - i) * SUBLANES, SUBLANES)
        hf = uf_ref[pl.ds(f0, SUBLANES), :] + af_ref[pl.ds(f0, SUBLANES), :] * cf
        hb = ub_ref[pl.ds(b0, SUBLANES), :] + ab_ref[pl.ds(b0, SUBLANES), :] * cb
        uf_ref[pl.ds(f0, SUBLANES), :] = hf
        ub_ref[pl.ds(b0, SUBLANES), :] = hb
        cf = jnp.broadcast_to(hf[SUBLANES - 1:SUBLANES, :], (SUBLANES, width))
        cb = jnp.broadcast_to(hb[0:1, :], (SUBLANES, width))
        return cf, cb

    c0f = jnp.broadcast_to(h0_ref[0, 0:1, :], (SUBLANES, width))
    c0b = jnp.broadcast_to(h0_ref[0, 1:2, :], (SUBLANES, width))
    cf, cb = lax.fori_loop(0, n_blk, body, (c0f, c0b))
    hl_ref[0, 0:1, :] = cf[0:1, :]
    hl_ref[0, 1:2, :] = cb[0:1, :]
    y_ref[...] = ((uf_ref[...] + ub_ref[...]) * jax.nn.gelu(gb_ref[...])).astype(BF16)


def _lru(xb, gb, conv_w, conv_b, w_bd, b_bd, lam, h0, n_seq, t_len, tok_blk0, width):
    n_sub = width // LRU_SUB
    seq = lambda s, c: (tok_blk0 + s, c)
    return pl.pallas_call(
        _lru_kernel,
        grid=(n_seq, LRU_WIDTH // width),
        in_specs=[
            pl.BlockSpec((t_len, width), seq),
            pl.BlockSpec((t_len, width), seq),
            pl.BlockSpec((CONV_W, width), lambda s, c: (0, c)),
            pl.BlockSpec((1, width), lambda s, c: (0, c)),
            pl.BlockSpec((n_sub, LRU_SUB, 4 * LRU_SUB), lambda s, c: (c, 0, 0)),
            pl.BlockSpec((n_sub, 1, 4 * LRU_SUB), lambda s, c: (c, 0, 0)),
            pl.BlockSpec((1, 2, width), lambda s, c: (0, 0, c)),
            pl.BlockSpec((1, 2, width), lambda s, c: (s, 0, c)),
        ],
        out_specs=[
            pl.BlockSpec((t_len, width), lambda s, c: (s, c)),
            pl.BlockSpec((1, 2, width), lambda s, c: (s, 0, c)),
        ],
        out_shape=[
            jax.ShapeDtypeStruct((n_seq * t_len, LRU_WIDTH), BF16),
            jax.ShapeDtypeStruct((n_seq, 2, LRU_WIDTH), F32),
        ],
        scratch_shapes=[pltpu.VMEM((t_len, width), F32)] * 4,
        compiler_params=_cparams(2),
        name="rglru",
    )(xb, gb, conv_w, conv_b.reshape(1, LRU_WIDTH), w_bd, b_bd, lam.reshape(1, 2, LRU_WIDTH), h0)


def _lru_gate_weights(w_r, b_r, w_i, b_i):
    cw = LRU_SUB
    bpc = cw // LRU_BLOCK

    def dense_half(w, c):
        blocks = w[c * bpc:(c + 1) * bpc]
        eye = jnp.eye(bpc, dtype=w.dtype)
        return jnp.einsum('nij,nm->nimj', blocks, eye).reshape(cw, cw)

    w_halves, b_halves = [], []
    for c in range(LRU_WIDTH // cw):
        w_halves.append(jnp.concatenate(
            [dense_half(w_r[0], c), dense_half(w_i[0], c), dense_half(w_r[1], c), dense_half(w_i[1], c)],
            axis=1))
        sl = slice(c * cw, (c + 1) * cw)
        b_halves.append(jnp.concatenate([b_r[0, sl], b_i[0, sl], b_r[1, sl], b_i[1, sl]]))
    return jnp.stack(w_halves).astype(BF16), jnp.stack(b_halves).reshape(LRU_WIDTH // cw, 1, 4 * cw)


def _out_proj_kernel(x_ref, mod_ref, oc_ref, os_ref, yc_ref, ys_ref, w_ref, out_ref, w_bf_ref):
    i = pl.program_id(0)

    @pl.when(i == 0)
    def _():
        w_bf_ref[...] = w_ref[...].astype(BF16)

    is_ctx = i < N_CTX_TILES
    o = jnp.where(is_ctx, oc_ref[...], os_ref[...])
    yb = jnp.where(is_ctx, yc_ref[...], ys_ref[...])
    cat = jnp.concatenate([o, yb], axis=1)
    y = jnp.dot(cat, w_bf_ref[...], preferred_element_type=F32)
    out_ref[...] = x_ref[...] + _mod_vec(mod_ref, _mod_row_of_tile(i), 5) * y


def _out_proj(x, mod, layer, o_ctx, o_smp, yb_ctx, yb_smp, w_out):
    tm = TOKEN_TILE
    tok = lambda i: (i, 0)
    ctx_tok = pl.BlockSpec((tm, NA_WIDTH), lambda i: (jnp.minimum(i, N_CTX_TILES - 1), 0))
    smp_tok = pl.BlockSpec((tm, NA_WIDTH), lambda i: (jnp.maximum(i - N_CTX_TILES, 0), 0))
    return pl.pallas_call(
        _out_proj_kernel,
        grid=(N_TOK // tm,),
        in_specs=[
            pl.BlockSpec((tm, D_MODEL), tok),
            _mod_spec(layer, 1),
            ctx_tok, smp_tok, ctx_tok, smp_tok,
            _resident((D_MODEL, D_MODEL), lambda i: (0, 0)),
        ],
        out_specs=pl.BlockSpec((tm, D_MODEL), tok),
        out_shape=jax.ShapeDtypeStruct((N_TOK, D_MODEL), F32),
        scratch_shapes=[pltpu.VMEM((D_MODEL, D_MODEL), BF16)],
        compiler_params=_cparams(1),
        name="mixer_out_proj",
    )(x, mod, o_ctx, o_smp, yb_ctx, yb_smp, w_out)


def _fourier_kernel(x_ref, mod_ref, g_ref, cs_ref, ct_ref, w_ref, o_ref, w_bf_ref, *, mod_row0):
    @pl.when(pl.program_id(0) == 0)
    def _():
        w_bf_ref[...] = w_ref[...].astype(BF16)

    x = x_ref[...]
    t_len = x.shape[0]
    row = mod_row0 + pl.program_id(0) if mod_row0 else 0
    h = _norm_mod(x, g_ref[...], _mod_vec(mod_ref, row, 3), _mod_vec(mod_ref, row, 4)).astype(BF16)
    cos_parts, sin_parts = [], []
    for g in range(FOURIER_GROUPS):
        ab = jnp.dot(h[:, g * GROUP_W:(g + 1) * GROUP_W], cs_ref[...], preferred_element_type=F32)
        cos_parts.append(ab[:, :GROUP_W])
        sin_parts.append(ab[:, GROUP_W:])
    stacked = jnp.concatenate(
        [jnp.concatenate(cos_parts, axis=1), jnp.concatenate(sin_parts, axis=1)], axis=0).astype(BF16)
    f = jnp.dot(ct_ref[...], stacked, preferred_element_type=F32) * ((t_len * GROUP_W) ** -0.5)
    y = jnp.dot(f.astype(BF16), w_bf_ref[...], preferred_element_type=F32)
    o_ref[...] = x + _mod_vec(mod_ref, row, 5) * y


def _dft_tables(t_len):
    def cos_sin(n):
        jk = np.outer(np.arange(n), np.arange(n)) % n
        ang = 2.0 * np.pi * jk.astype(np.float64) / n
        return np.cos(ang), np.sin(ang)

    cc, sc = cos_sin(GROUP_W)
    ct, st = cos_sin(t_len)
    chan = jnp.asarray(np.concatenate([cc, sc], axis=1).astype(np.float32)).astype(BF16)
    time = jnp.asarray(np.concatenate([ct, -st], axis=1).astype(np.float32)).astype(BF16)
    return chan, time


def _fourier(x, mod, layer, g, w_out, n_seq, t_len, tok_blk0, mod_row0):
    chan, time = _dft_tables(t_len)
    seq = lambda s: (tok_blk0 + s, 0)
    const = lambda s: (0, 0)
    return pl.pallas_call(
        functools.partial(_fourier_kernel, mod_row0=mod_row0),
        grid=(n_seq,),
        in_specs=[
            pl.BlockSpec((t_len, D_MODEL), seq),
            _mod_spec(layer, 1),
            pl.BlockSpec((1, D_MODEL), const),
            _resident((GROUP_W, 2 * GROUP_W), const),
            _resident((t_len, 2 * t_len), const),
            _resident((D_MODEL, D_MODEL), const),
        ],
        out_specs=pl.BlockSpec((t_len, D_MODEL), seq),
        out_shape=jax.ShapeDtypeStruct((N_TOK, D_MODEL), F32),
        input_output_aliases={0: 0},
        scratch_shapes=[pltpu.VMEM((D_MODEL, D_MODEL), BF16)],
        compiler_params=_cparams(1),
        name="fourier_mixer",
    )(x, mod, g.reshape(1, D_MODEL), chan, time, w_out)


def kernel(x_prompt, x_sample, cache_k, cache_v, state_lru_fwd, state_lru_bwd, c, c_ctx, w_ada, b_ada, norm_g, ffn1_gate, ffn1_up, ffn1_down, ffn2_gate, ffn2_up, ffn2_down, w_in, q_norm_g, k_norm_g, rpb, conv_w, conv_b, lru_w_r, lru_b_r, lru_w_i, lru_b_i, lru_lambda, w_out_ab, w_out_c):
    assert DEPTH == 2, "one neighbourhood/RG-LRU layer followed by one Fourier layer"
    cond = jnp.concatenate(
        [c_ctx[None, :], c, jnp.zeros((MOD_ROWS - 1 - DEC_BATCH, D_MODEL), F32)], axis=0)
    mod = _adaln(cond, w_ada, b_ada)

    ffn1 = (ffn1_gate, ffn1_up, ffn1_down)
    ffn2 = (ffn2_gate, ffn2_up, ffn2_down)

    x = _ffn((x_prompt.reshape(N_CTX_TOK, D_MODEL), x_sample.reshape(N_SMP_TOK, D_MODEL)),
             mod, norm_g[0, 0], *ffn1, 0, 0)
    q, k, v, xb, gb, new_k, new_v = _proj(x, mod, 0, norm_g[0, 1], w_in[0], q_norm_g[0], k_norm_g[0])
    o_ctx = _ctx_attn(q, k, v)
    o_smp = _na_attn(q, k, v,
                     cache_k[:, 0].reshape(DEC_BATCH, PAST_LEN, NA_WIDTH),
                     cache_v[:, 0].reshape(DEC_BATCH, PAST_LEN, NA_WIDTH), rpb[0])
    w_bd, b_bd = _lru_gate_weights(lru_w_r[0], lru_b_r[0], lru_w_i[0], lru_b_i[0])
    h0_ctx = jnp.zeros((BATCH, 2, LRU_WIDTH), F32)
    h0_smp = jnp.stack([state_lru_fwd[:, 0], state_lru_bwd[:, 0]], axis=1).astype(F32)
    yb_ctx, hl_ctx = _lru(xb, gb, conv_w[0], conv_b[0], w_bd, b_bd, lru_lambda[0],
                          h0_ctx, BATCH, SEQ, 0, LRU_WIDTH)
    yb_smp, _ = _lru(xb, gb, conv_w[0], conv_b[0], w_bd, b_bd, lru_lambda[0],
                     h0_smp, DEC_BATCH, DEC_SEQ, N_CTX_TOK // DEC_SEQ, LRU_SUB)
    x = _out_proj(x, mod, 0, o_ctx, o_smp, yb_ctx, yb_smp, w_out_ab[0])
    x = _ffn((x,), mod, norm_g[0, 2], *ffn2, 0, 6)

    x = _ffn((x,), mod, norm_g[1, 0], *ffn1, 1, 0)
    w_c = w_out_c[0]
    x = _fourier(x, mod, 1, norm_g[1, 1], w_c, BATCH, SEQ, 0, 0)
    x = _fourier(x, mod, 1, norm_g[1, 1], w_c, DEC_BATCH, DEC_SEQ, N_CTX_TOK // DEC_SEQ, 1)
    y_prompt, y_sample = _ffn((x,), mod, norm_g[1, 2], *ffn2, 1, 6, split_out=True)

    return (y_prompt.reshape(BATCH, SEQ, D_MODEL), y_sample.reshape(DEC_BATCH, DEC_SEQ, D_MODEL),
            new_k.reshape(BATCH, 1, SEQ, NA_HEADS, HEAD_DIM), new_v.reshape(BATCH, 1, SEQ, NA_HEADS, HEAD_DIM),
            hl_ctx[:, 0].reshape(BATCH, 1, LRU_WIDTH), hl_ctx[:, 1].reshape(BATCH, 1, LRU_WIDTH))
```

```python
import functools

import numpy as np
import jax
import jax.numpy as jnp
from jax import lax
from jax.experimental import pallas as pl
from jax.experimental.pallas import tpu as pltpu

F32 = jnp.float32
BF16 = jnp.bfloat16

D_MODEL = 1024
BATCH = 16
SEQ = 256
DEPTH = 2
DEC_BATCH = 2
DEC_SEQ = 1024
PAST_LEN = 256
GRID_W = 64
HEAD_DIM = 64
NA_WIDTH = 512
NA_HEADS = 8
WIN_H = 8
WIN_W = 16
LRU_WIDTH = 512
LRU_BLOCKS = 8
LRU_BLOCK = 64
LRU_C = 8.0
LRU_SUB = 256
CONV_W = 4
FOURIER_GROUPS = 4
GROUP_W = D_MODEL // FOURIER_GROUPS
D_FF = 2816
N_MOD = 9
IN_WIDTH = 3 * NA_WIDTH + 2 * LRU_WIDTH
EPS = 1e-6

N_CTX_TOK = BATCH * SEQ
N_SMP_TOK = DEC_BATCH * DEC_SEQ
N_TOK = N_CTX_TOK + N_SMP_TOK
MOD_ROWS = 8
MOD_WIDTH = N_MOD * D_MODEL
ROWS = DEC_SEQ // GRID_W
KH = min(WIN_H, ROWS)

TOKEN_TILE = 512
N_CTX_TILES = N_CTX_TOK // TOKEN_TILE
FFN_TILE = 512
FF_TILE = 256
FF_CHUNKS = D_FF // FF_TILE
FF_STAGE_SLOTS = 2
SUBLANES = 8
LANES = 128
VMEM_LIMIT = 56 * 1024 * 1024

NA_Q_ROWS = 4
NA_GROUPS = ROWS // NA_Q_ROWS
NA_K_ROWS = 12
NA_Q = NA_Q_ROWS * GRID_W
NA_K = NA_K_ROWS * GRID_W
N_DR = 2 * WIN_H - 1
N_DC = 2 * WIN_W - 1
N_DR_PAIRS = N_DR + 1


def _cparams(n_axes):
    return pltpu.CompilerParams(
        dimension_semantics=("arbitrary",) * n_axes, vmem_limit_bytes=VMEM_LIMIT)


def _resident(block_shape, index_map):
    return pl.BlockSpec(block_shape, index_map, pipeline_mode=pl.Buffered(1))


def _mod_spec(layer, n_axes):
    if n_axes == 1:
        return _resident((None, MOD_ROWS, MOD_WIDTH), lambda i: (layer, 0, 0))
    return _resident((None, MOD_ROWS, MOD_WIDTH), lambda i, j: (layer, 0, 0))


def _mod_row_of_tile(i, tile=TOKEN_TILE):
    n_ctx_tiles = N_CTX_TOK // tile
    tiles_per_seq = DEC_SEQ // tile
    return jnp.where(i < n_ctx_tiles, 0, 1 + (i - n_ctx_tiles) // tiles_per_seq)


def _mod_vec(mod_ref, row, k):
    return mod_ref[pl.ds(row, 1), k * D_MODEL:(k + 1) * D_MODEL]


def _norm_mod(x, g, shift, scale):
    ms = jnp.mean(x * x, axis=-1, keepdims=True)
    y = x * lax.rsqrt(ms + EPS) * g
    return y * (1.0 + scale) + shift


def _adaln_kernel(c_ref, w_ref, b_ref, o_ref):
    c = c_ref[...]
    s = (c * jax.nn.sigmoid(c)).astype(BF16)
    w = w_ref[0].astype(BF16)
    o_ref[0] = jnp.dot(s, w, preferred_element_type=F32) + b_ref[0]


def _adaln(cond, w_ada, b_ada):
    tn = MOD_WIDTH // 4
    return pl.pallas_call(
        _adaln_kernel,
        grid=(DEPTH, MOD_WIDTH // tn),
        in_specs=[
            pl.BlockSpec((MOD_ROWS, D_MODEL), lambda l, j: (0, 0)),
            pl.BlockSpec((1, D_MODEL, tn), lambda l, j: (l, 0, j)),
            pl.BlockSpec((1, 1, tn), lambda l, j: (l, 0, j)),
        ],
        out_specs=pl.BlockSpec((1, MOD_ROWS, tn), lambda l, j: (l, 0, j)),
        out_shape=jax.ShapeDtypeStruct((DEPTH, MOD_ROWS, MOD_WIDTH), F32),
        compiler_params=_cparams(2),
        name="adaln",
    )(cond, w_ada, b_ada.reshape(DEPTH, 1, MOD_WIDTH))


def _ffn_weight_copy(w_hbm, stage_ref, sem_ref, layer, j, ff_axis):
    ff = pl.ds(j * FF_TILE, FF_TILE)
    src = w_hbm.at[layer, :, ff] if ff_axis == 1 else w_hbm.at[layer, ff, :]
    slot = j % FF_STAGE_SLOTS
    return pltpu.make_async_copy(src, stage_ref.at[slot], sem_ref.at[slot])


def _mixer_out_copy(w_hbm, stage_ref, sem_ref, j):
    rows = stage_ref.shape[1]
    slot = j % stage_ref.shape[0]
    return pltpu.make_async_copy(w_hbm.at[0, pl.ds(j * rows, rows), :], stage_ref.at[slot], sem_ref.at[slot])


def _ffn_kernel(*refs, layer, mod_base, split_in, split_out, mixer_out):
    refs = list(refs)
    take = lambda n: [refs.pop(0) for _ in range(n)]
    x_refs = take(2 if split_in else 1)
    mix_refs = take(4) if mixer_out else None
    mod_ref, g_ref = take(2)
    wo_hbm = take(1)[0] if mixer_out else None
    wg_hbm, wu_hbm, wd_hbm = take(3)
    o_refs = take(2 if split_out else 1)
    wg_bf, wu_bf, wd_bf, stg_g, stg_u, stg_d, sem_g, sem_u, sem_d = take(9)
    streams = ((wg_hbm, stg_g, sem_g, wg_bf, 1), (wu_hbm, stg_u, sem_u, wu_bf, 1),
               (wd_hbm, stg_d, sem_d, wd_bf, 0))

    i = pl.program_id(0)
    is_ctx = i < N_CTX_TOK // FFN_TILE
    if split_in:
        x = jnp.where(is_ctx, x_refs[0][...], x_refs[1][...])
    else:
        x = x_refs[0][...]
    row = _mod_row_of_tile(i, FFN_TILE)

    if mixer_out:
        wo_bf, stg_o, sem_o = take(3)
        rows = stg_o.shape[1]
        n_chunks = D_MODEL // rows

        @pl.when(i == 0)
        def _():
            for j in range(stg_o.shape[0]):
                _mixer_out_copy(wo_hbm, stg_o, sem_o, j).start()
            for j in range(n_chunks):
                _mixer_out_copy(wo_hbm, stg_o, sem_o, j).wait()
                wo_bf[j * rows:(j + 1) * rows, :] = stg_o[j % stg_o.shape[0]].astype(BF16)
                if j + stg_o.shape[0] < n_chunks:
                    _mixer_out_copy(wo_hbm, stg_o, sem_o, j + stg_o.shape[0]).start()

        oc_ref, os_ref, yc_ref, ys_ref = mix_refs
        cat = jnp.concatenate([jnp.where(is_ctx, oc_ref[...], os_ref[...]),
                               jnp.where(is_ctx, yc_ref[...], ys_ref[...])], axis=1)
        x = x + _mod_vec(mod_ref, row, mod_base - 1) * jnp.dot(cat, wo_bf[...], preferred_element_type=F32)

    h = _norm_mod(x, g_ref[...], _mod_vec(mod_ref, row, mod_base),
                  _mod_vec(mod_ref, row, mod_base + 1)).astype(BF16)

    def start_chunk(j):
        for w_hbm, stg, sem, _, ff_axis in streams:
            _ffn_weight_copy(w_hbm, stg, sem, layer, j, ff_axis).start()

    def finish_chunk(j):
        for w_hbm, stg, sem, w_bf, ff_axis in streams:
            _ffn_weight_copy(w_hbm, stg, sem, layer, j, ff_axis).wait()
            w_bf[j] = stg[j % FF_STAGE_SLOTS].astype(BF16)

    def run(stream_weights):
        if stream_weights:
            for j in range(FF_STAGE_SLOTS):
                start_chunk(j)
        acc = None
        for j in range(FF_CHUNKS):
            if stream_weights:
                finish_chunk(j)
                if j + FF_STAGE_SLOTS < FF_CHUNKS:
                    start_chunk(j + FF_STAGE_SLOTS)
            a = jnp.dot(h, wg_bf[j], preferred_element_type=F32)
            b = jnp.dot(h, wu_bf[j], preferred_element_type=F32)
            act = (a * jax.nn.sigmoid(a) * b).astype(BF16)
            y = jnp.dot(act, wd_bf[j], preferred_element_type=F32)
            acc = y if acc is None else acc + y
        res = x + 0.5 * _mod_vec(mod_ref, row, mod_base + 2) * acc
        if split_out:
            @pl.when(is_ctx)
            def _():
                o_refs[0][...] = res

            @pl.when(jnp.logical_not(is_ctx))
            def _():
                o_refs[1][...] = res
        else:
            o_refs[0][...] = res

    @pl.when(i == 0)
    def _():
        run(True)

    @pl.when(i > 0)
    def _():
        run(False)


def _ffn(xs, mod, g, wg, wu, wd, layer, mod_base, split_out=False, mixer_out=None):
    tm = FFN_TILE
    n_ctx_tiles = N_CTX_TOK // tm
    split_in = len(xs) == 2

    def tiles(width):
        return (pl.BlockSpec((tm, width), lambda i: (i, 0)),
                pl.BlockSpec((tm, width), lambda i: (jnp.minimum(i, n_ctx_tiles - 1), 0)),
                pl.BlockSpec((tm, width), lambda i: (jnp.maximum(i - n_ctx_tiles, 0), 0)))

    tok, ctx_tok, smp_tok = tiles(D_MODEL)
    full = jax.ShapeDtypeStruct((N_TOK, D_MODEL), F32)
    pair = [jax.ShapeDtypeStruct((N_CTX_TOK, D_MODEL), F32), jax.ShapeDtypeStruct((N_SMP_TOK, D_MODEL), F32)]
    hbm = pl.BlockSpec(memory_space=pl.ANY)
    in_specs = [ctx_tok, smp_tok] if split_in else [tok]
    operands = list(xs)
    scratch = [
        pltpu.VMEM((FF_CHUNKS, D_MODEL, FF_TILE), BF16),
        pltpu.VMEM((FF_CHUNKS, D_MODEL, FF_TILE), BF16),
        pltpu.VMEM((FF_CHUNKS, FF_TILE, D_MODEL), BF16),
        pltpu.VMEM((FF_STAGE_SLOTS, D_MODEL, FF_TILE), F32),
        pltpu.VMEM((FF_STAGE_SLOTS, D_MODEL, FF_TILE), F32),
        pltpu.VMEM((FF_STAGE_SLOTS, FF_TILE, D_MODEL), F32),
        pltpu.SemaphoreType.DMA((FF_STAGE_SLOTS,)),
        pltpu.SemaphoreType.DMA((FF_STAGE_SLOTS,)),
        pltpu.SemaphoreType.DMA((FF_STAGE_SLOTS,)),
    ]
    if mixer_out is not None:
        _, ctx_half, smp_half = tiles(NA_WIDTH)
        in_specs += [ctx_half, smp_half, ctx_half, smp_half]
        operands += list(mixer_out[:4])
    in_specs += [_mod_spec(layer, 1), pl.BlockSpec((1, D_MODEL), lambda i: (0, 0))]
    operands += [mod, g.reshape(1, D_MODEL)]
    if mixer_out is not None:
        in_specs.append(hbm)
        operands.append(mixer_out[4])
        scratch += [
            pltpu.VMEM((D_MODEL, D_MODEL), BF16),
            pltpu.VMEM((FF_STAGE_SLOTS, FF_TILE, D_MODEL), F32),
            pltpu.SemaphoreType.DMA((FF_STAGE_SLOTS,)),
        ]
    return pl.pallas_call(
        functools.partial(_ffn_kernel, layer=layer, mod_base=mod_base, split_in=split_in,
                          split_out=split_out, mixer_out=mixer_out is not None),
        grid=(N_TOK // tm,),
        in_specs=in_specs + [hbm, hbm, hbm],
        out_specs=[ctx_tok, smp_tok] if split_out else tok,
        out_shape=pair if split_out else full,
        scratch_shapes=scratch,
        compiler_params=_cparams(1),
        name="ffn",
    )(*operands, wg, wu, wd)


def _head_rms_norm(z, g, ones_bd):
    z2 = z * z
    hi = z2.astype(BF16)
    lo = (z2 - hi.astype(F32)).astype(BF16)
    n = ones_bd.shape[0]
    parts = []
    for c in range(z.shape[1] // n):
        sl = slice(c * n, (c + 1) * n)
        parts.append(jnp.dot(hi[:, sl], ones_bd, preferred_element_type=F32)
                     + jnp.dot(lo[:, sl], ones_bd, preferred_element_type=F32))
    ss = jnp.concatenate(parts, axis=1)
    return z * lax.rsqrt(ss * (1.0 / HEAD_DIM) + EPS) * g


def _proj_kernel(x_ref, mod_ref, g_ref, w_ref, qg_ref, kg_ref, ones_ref,
                 q_ref, k_ref, v_ref, xb_ref, gb_ref, kout_ref, vout_ref, w_bf_ref):
    i = pl.program_id(0)

    @pl.when(i == 0)
    def _():
        w_bf_ref[...] = w_ref[...].astype(BF16)

    x = x_ref[...]
    row = _mod_row_of_tile(i)
    h = _norm_mod(x, g_ref[...], _mod_vec(mod_ref, row, 3), _mod_vec(mod_ref, row, 4)).astype(BF16)

    def proj(part):
        return jnp.dot(h, w_bf_ref[:, part * NA_WIDTH:(part + 1) * NA_WIDTH], preferred_element_type=F32)

    ones_bd = ones_ref[...]
    q = _head_rms_norm(proj(0), qg_ref[...], ones_bd) * (HEAD_DIM ** -0.5)
    q_ref[...] = q.astype(BF16)
    k = _head_rms_norm(proj(1), kg_ref[...], ones_bd)
    k_ref[...] = k.astype(BF16)
    v = proj(2)
    v_ref[...] = v.astype(BF16)
    xb_ref[...] = proj(3)
    gb_ref[...] = proj(4)

    @pl.when(i < N_CTX_TILES)
    def _():
        kout_ref[...] = k.reshape(TOKEN_TILE, NA_HEADS, HEAD_DIM)
        vout_ref[...] = v.reshape(TOKEN_TILE, NA_HEADS, HEAD_DIM)


def _proj(x, mod, layer, g, w_in, q_g, k_g):
    tm = TOKEN_TILE
    head = np.arange(2 * LANES) // HEAD_DIM
    ones_bd = jnp.asarray((head[:, None] == head[None, :]).astype(np.float32), dtype=BF16)
    tok = lambda i: (i, 0)
    const = lambda i: (0, 0)
    act_f32 = jax.ShapeDtypeStruct((N_TOK, NA_WIDTH), F32)
    act_bf16 = jax.ShapeDtypeStruct((N_TOK, NA_WIDTH), BF16)
    cache = jax.ShapeDtypeStruct((N_CTX_TOK, NA_HEADS, HEAD_DIM), F32)
    cache_spec = pl.BlockSpec((tm, NA_HEADS, HEAD_DIM), lambda i: (jnp.minimum(i, N_CTX_TILES - 1), 0, 0))
    return pl.pallas_call(
        _proj_kernel,
        grid=(N_TOK // tm,),
        in_specs=[
            pl.BlockSpec((tm, D_MODEL), tok),
            _mod_spec(layer, 1),
            pl.BlockSpec((1, D_MODEL), const),
            _resident((D_MODEL, IN_WIDTH), const),
            pl.BlockSpec((1, NA_WIDTH), const),
            pl.BlockSpec((1, NA_WIDTH), const),
            _resident((2 * LANES, 2 * LANES), const),
        ],
        out_specs=[pl.BlockSpec((tm, NA_WIDTH), tok)] * 5 + [cache_spec, cache_spec],
        out_shape=[act_bf16, act_bf16, act_bf16, act_f32, act_f32, cache, cache],
        scratch_shapes=[pltpu.VMEM((D_MODEL, IN_WIDTH), BF16)],
        compiler_params=_cparams(1),
        name="mixer_in_proj",
    )(x, mod, g.reshape(1, D_MODEL), w_in,
      jnp.tile(q_g, NA_HEADS).reshape(1, NA_WIDTH), jnp.tile(k_g, NA_HEADS).reshape(1, NA_WIDTH),
      ones_bd)


def _head_masks():
    lane = lax.broadcasted_iota(jnp.int32, (1, 2 * HEAD_DIM), 1)
    return [lane < HEAD_DIM, lane >= HEAD_DIM]


def _ctx_attn_kernel(q_ref, k_ref, v_ref, o_ref):
    masks = _head_masks()
    for p in range(NA_HEADS // 2):
        sl = slice(2 * HEAD_DIM * p, 2 * HEAD_DIM * (p + 1))
        q2 = q_ref[:, sl]
        k2t = k_ref[:, sl].T
        v2 = v_ref[:, sl]
        out = None
        for e in range(2):
            qm = jnp.where(masks[e], q2, jnp.zeros_like(q2))
            s = jnp.dot(qm, k2t, preferred_element_type=F32)
            pe = jnp.exp(s - jnp.max(s, axis=-1, keepdims=True))
            den = jnp.sum(pe, axis=-1, keepdims=True)
            o = jnp.dot(pe.astype(BF16), v2, preferred_element_type=F32) / den
            out = o if out is None else jnp.where(masks[e], o, out)
        o_ref[:, sl] = out.astype(BF16)


def _ctx_attn(q, k, v):
    blk = pl.BlockSpec((SEQ, NA_WIDTH), lambda b: (b, 0))
    return pl.pallas_call(
        _ctx_attn_kernel,
        grid=(BATCH,),
        in_specs=[blk, blk, blk],
        out_specs=blk,
        out_shape=jax.ShapeDtypeStruct((N_CTX_TOK, NA_WIDTH), BF16),
        compiler_params=_cparams(1),
        name="ctx_attention",
    )(q, k, v)


def _na_build_bias_table(rpb_ref, table_ref):
    qc = lax.broadcasted_iota(jnp.int32, (GRID_W, LANES), 0)
    lane = lax.broadcasted_iota(jnp.int32, (GRID_W, LANES), 1)
    kc = lane % GRID_W
    col_start = jnp.clip(qc - WIN_W // 2, 0, GRID_W - WIN_W)
    col_in = (kc >= col_start) & (kc < col_start + WIN_W)
    neg = jnp.full((GRID_W, LANES), -jnp.inf, F32)

    def toeplitz(h, dr, lane0):
        if dr < 0 or dr >= N_DR:
            return neg
        w = jnp.broadcast_to(rpb_ref[h, dr:dr + 1, :], (GRID_W, LANES))
        return pltpu.roll(w, (lane0 - (WIN_W - 1)) % LANES, 1, stride=1, stride_axis=0)

    for h in range(NA_HEADS):
        for i in range(N_DR_PAIRS):
            t = jnp.where(lane < GRID_W, toeplitz(h, i - 1, 0), toeplitz(h, i, GRID_W))
            table_ref[h, i] = jnp.where(col_in, t, neg)


def _na_kernel(q_ref, k_ref, v_ref, kc_ref, vc_ref, rpb_ref, o_ref, table_ref):
    b = pl.program_id(0)
    g = pl.program_id(1)

    @pl.when((b == 0) & (g == 0))
    def _():
        _na_build_bias_table(rpb_ref, table_ref)

    win_row0 = jnp.where(g < NA_GROUPS // 2, 0, ROWS - NA_K_ROWS)
    start = pl.multiple_of(win_row0 * GRID_W, GRID_W)
    q_row = g * NA_Q_ROWS + lax.broadcasted_iota(jnp.int32, (NA_Q, 1), 0) // GRID_W
    k_row = win_row0 + lax.broadcasted_iota(jnp.int32, (1, NA_K), 1) // GRID_W
    row_start = jnp.clip(q_row - KH // 2, 0, ROWS - KH)
    row_in = (k_row >= row_start) & (k_row < row_start + KH)
    masks = _head_masks()
    for p in range(NA_HEADS // 2):
        sl = slice(2 * HEAD_DIM * p, 2 * HEAD_DIM * (p + 1))
        q2 = q_ref[:, sl]
        klt = k_ref[pl.ds(start, NA_K), sl].T
        vl = v_ref[pl.ds(start, NA_K), sl]
        kct = kc_ref[0, :, sl].astype(BF16).T
        vc = vc_ref[0, :, sl].astype(BF16)
        out = None
        for e in range(2):
            head = 2 * p + e
            bias_rows = []
            for a in range(NA_Q_ROWS):
                tiles = []
                for m in range(NA_K_ROWS // 2):
                    dr = win_row0 + 2 * m - (g * NA_Q_ROWS + a) + (WIN_H - 1)
                    tiles.append(table_ref[head, jnp.clip(dr + 1, 0, N_DR_PAIRS - 1)])
                bias_rows.append(jnp.concatenate(tiles, axis=1))
            bias = jnp.concatenate(bias_rows, axis=0)
            qm = jnp.where(masks[e], q2, jnp.zeros_like(q2))
            s_loc = jnp.where(row_in, jnp.dot(qm, klt, preferred_element_type=F32) + bias, -jnp.inf)
            s_ctx = jnp.dot(qm, kct, preferred_element_type=F32)
            m_max = jnp.maximum(jnp.max(s_loc, axis=-1, keepdims=True),
                                jnp.max(s_ctx, axis=-1, keepdims=True))
            p_loc = jnp.exp(s_loc - m_max)
            p_ctx = jnp.exp(s_ctx - m_max)
            den = jnp.sum(p_loc, axis=-1, keepdims=True) + jnp.sum(p_ctx, axis=-1, keepdims=True)
            o = (jnp.dot(p_loc.astype(BF16), vl, preferred_element_type=F32)
                 + jnp.dot(p_ctx.astype(BF16), vc, preferred_element_type=F32)) / den
            out = o if out is None else jnp.where(masks[e], o, out)
        o_ref[:, sl] = out.astype(BF16)


def _na_attn(q, k, v, k_ctx, v_ctx, rpb_e):
    smp_blk0 = N_CTX_TOK // DEC_SEQ
    q_blk0 = N_CTX_TOK // NA_Q
    kv = pl.BlockSpec((DEC_SEQ, NA_WIDTH), lambda b, g: (smp_blk0 + b, 0))
    ctx = pl.BlockSpec((1, PAST_LEN, NA_WIDTH), lambda b, g: (b, 0, 0))
    rpb_pad = jnp.pad(rpb_e.astype(F32), ((0, 0), (0, 0), (0, LANES - N_DC)))
    return pl.pallas_call(
        _na_kernel,
        grid=(DEC_BATCH, NA_GROUPS),
        in_specs=[
            pl.BlockSpec((NA_Q, NA_WIDTH), lambda b, g: (q_blk0 + b * NA_GROUPS + g, 0)),
            kv, kv, ctx, ctx,
            pl.BlockSpec((NA_HEADS, N_DR, LANES), lambda b, g: (0, 0, 0)),
        ],
        out_specs=pl.BlockSpec((NA_Q, NA_WIDTH), lambda b, g: (b * NA_GROUPS + g, 0)),
        out_shape=jax.ShapeDtypeStruct((N_SMP_TOK, NA_WIDTH), BF16),
        scratch_shapes=[pltpu.VMEM((NA_HEADS, N_DR_PAIRS, GRID_W, LANES), F32)],
        compiler_params=_cparams(2),
        name="neighbourhood_attention",
    )(q, k, v, k_ctx, v_ctx, rpb_pad)


def _sigmoid(x):
    return 0.5 * jnp.tanh(0.5 * x) + 0.5


def _lru_kernel(xb_ref, gb_ref, cw_ref, cb_ref, w_ref, b_ref, lam_ref, h0_ref,
                y_ref, hl_ref, af_ref, uf_ref, ab_ref, ub_ref):
    t_len, width = xb_ref.shape
    n_blk = t_len // SUBLANES
    row = lax.broadcasted_iota(jnp.int32, (t_len, 1), 0)
    in_block = lax.broadcasted_iota(jnp.int32, (1, SUBLANES, 1), 1)

    def shifted(z, s):
        rolled = pltpu.roll(z, (-s) % t_len, axis=0)
        ok = (row + s >= 0) & (row + s < t_len)
        return jnp.where(ok, rolled, 0.0)

    left = (CONV_W - 1) // 2
    for c in range(width // LRU_SUB):
        cs = slice(c * LRU_SUB, (c + 1) * LRU_SUB)
        x = xb_ref[:, cs]
        xc = cb_ref[:, cs]
        for j in range(CONV_W):
            tap = x if j == left else shifted(x, j - left)
            xc = xc + tap * cw_ref[j:j + 1, cs]
        gates = jnp.dot(xc.astype(BF16), w_ref[c], preferred_element_type=F32) + b_ref[c]
        for d, (a_ref, u_ref) in enumerate(((af_ref, uf_ref), (ab_ref, ub_ref))):
            r_gate = _sigmoid(gates[:, (2 * d) * LRU_SUB:(2 * d + 1) * LRU_SUB])
            i_gate = _sigmoid(gates[:, (2 * d + 1) * LRU_SUB:(2 * d + 2) * LRU_SUB])
            lam = lam_ref[0, d:d + 1, cs]
            log_sig = jnp.minimum(lam, 0.0) - jnp.log1p(jnp.exp(-jnp.abs(lam)))
            log_a = LRU_C * r_gate * log_sig
            a = jnp.exp(log_a)
            var = -jnp.tanh(log_a) * (a * a + 1.0)
            u = jnp.where(var > 0.0, var * lax.rsqrt(var), 0.0) * (i_gate * xc)
            a = a.reshape(n_blk, SUBLANES, LRU_SUB)
            u = u.reshape(n_blk, SUBLANES, LRU_SUB)
            step = 1
            while step < SUBLANES:
                if d == 0:
                    ok, shift = in_block >= step, step
                else:
                    ok, shift = in_block < SUBLANES - step, SUBLANES - step
                a_prev = jnp.where(ok, pltpu.roll(a, shift, axis=1), 1.0)
                u_prev = jnp.where(ok, pltpu.roll(u, shift, axis=1), 0.0)
                u = u + a * u_prev
                a = a * a_prev
                step *= 2
            a_ref[:, cs] = a.reshape(t_len, LRU_SUB)
            u_ref[:, cs] = u.reshape(t_len, LRU_SUB)

    def body(i, carry):
        cf, cb = carry
        f0 = pl.multiple_of(i * SUBLANES, SUBLANES)
        b0 = pl.multiple_of((n_blk - 1 - i) * SUBLANES, SUBLANES)
        hf = uf_ref[pl.ds(f0, SUBLANES), :] + af_ref[pl.ds(f0, SUBLANES), :] * cf
        hb = ub_ref[pl.ds(b0, SUBLANES), :] + ab_ref[pl.ds(b0, SUBLANES), :] * cb
        uf_ref[pl.ds(f0, SUBLANES), :] = hf
        ub_ref[pl.ds(b0, SUBLANES), :] = hb
        cf = jnp.broadcast_to(hf[SUBLANES - 1:SUBLANES, :], (SUBLANES, width))
        cb = jnp.broadcast_to(hb[0:1, :], (SUBLANES, width))
        return cf, cb

    c0f = jnp.broadcast_to(h0_ref[0, 0:1, :], (SUBLANES, width))
    c0b = jnp.broadcast_to(h0_ref[0, 1:2, :], (SUBLANES, width))
    cf, cb = lax.fori_loop(0, n_blk, body, (c0f, c0b))
    hl_ref[0, 0:1, :] = cf[0:1, :]
    hl_ref[0, 1:2, :] = cb[0:1, :]
    y_ref[...] = ((uf_ref[...] + ub_ref[...]) * jax.nn.gelu(gb_ref[...])).astype(BF16)


def _lru(xb, gb, conv_w, conv_b, w_bd, b_bd, lam, h0, n_seq, t_len, tok_blk0, width):
    n_sub = width // LRU_SUB
    seq = lambda s, c: (tok_blk0 + s, c)
    return pl.pallas_call(
        _lru_kernel,
        grid=(n_seq, LRU_WIDTH // width),
        in_specs=[
            pl.BlockSpec((t_len, width), seq),
            pl.BlockSpec((t_len, width), seq),
            pl.BlockSpec((CONV_W, width), lambda s, c: (0, c)),
            pl.BlockSpec((1, width), lambda s, c: (0, c)),
            pl.BlockSpec((n_sub, LRU_SUB, 4 * LRU_SUB), lambda s, c: (c, 0, 0)),
            pl.BlockSpec((n_sub, 1, 4 * LRU_SUB), lambda s, c: (c, 0, 0)),
            pl.BlockSpec((1, 2, width), lambda s, c: (0, 0, c)),
            pl.BlockSpec((1, 2, width), lambda s, c: (s, 0, c)),
        ],
        out_specs=[
            pl.BlockSpec((t_len, width), lambda s, c: (s, c)),
            pl.BlockSpec((1, 2, width), lambda s, c: (s, 0, c)),
        ],
        out_shape=[
            jax.ShapeDtypeStruct((n_seq * t_len, LRU_WIDTH), BF16),
            jax.ShapeDtypeStruct((n_seq, 2, LRU_WIDTH), F32),
        ],
        scratch_shapes=[pltpu.VMEM((t_len, width), F32)] * 4,
        compiler_params=_cparams(2),
        name="rglru",
    )(xb, gb, conv_w, conv_b.reshape(1, LRU_WIDTH), w_bd, b_bd, lam.reshape(1, 2, LRU_WIDTH), h0)


def _lru_gate_weights(w_r, b_r, w_i, b_i):
    cw = LRU_SUB
    bpc = cw // LRU_BLOCK

    def dense_half(w, c):
        blocks = w[c * bpc:(c + 1) * bpc]
        eye = jnp.eye(bpc, dtype=w.dtype)
        return jnp.einsum('nij,nm->nimj', blocks, eye).reshape(cw, cw)

    w_halves, b_halves = [], []
    for c in range(LRU_WIDTH // cw):
        w_halves.append(jnp.concatenate(
            [dense_half(w_r[0], c), dense_half(w_i[0], c), dense_half(w_r[1], c), dense_half(w_i[1], c)],
            axis=1))
        sl = slice(c * cw, (c + 1) * cw)
        b_halves.append(jnp.concatenate([b_r[0, sl], b_i[0, sl], b_r[1, sl], b_i[1, sl]]))
    return jnp.stack(w_halves).astype(BF16), jnp.stack(b_halves).reshape(LRU_WIDTH // cw, 1, 4 * cw)


def _fourier_kernel(x_ref, mod_ref, g_ref, cs_ref, ct_ref, w_ref, o_ref, w_bf_ref, *, mod_row0):
    @pl.when(pl.program_id(0) == 0)
    def _():
        w_bf_ref[...] = w_ref[...].astype(BF16)

    x = x_ref[...]
    t_len = x.shape[0]
    row = mod_row0 + pl.program_id(0) if mod_row0 else 0
    h = _norm_mod(x, g_ref[...], _mod_vec(mod_ref, row, 3), _mod_vec(mod_ref, row, 4)).astype(BF16)
    cos_parts, sin_parts = [], []
    for g in range(FOURIER_GROUPS):
        ab = jnp.dot(h[:, g * GROUP_W:(g + 1) * GROUP_W], cs_ref[...], preferred_element_type=F32)
        cos_parts.append(ab[:, :GROUP_W])
        sin_parts.append(ab[:, GROUP_W:])
    stacked = jnp.concatenate(
        [jnp.concatenate(cos_parts, axis=1), jnp.concatenate(sin_parts, axis=1)], axis=0).astype(BF16)
    f = jnp.dot(ct_ref[...], stacked, preferred_element_type=F32) * ((t_len * GROUP_W) ** -0.5)
    y = jnp.dot(f.astype(BF16), w_bf_ref[...], preferred_element_type=F32)
    o_ref[...] = x + _mod_vec(mod_ref, row, 5) * y


def _dft_tables(t_len):
    def cos_sin(n):
        jk = np.outer(np.arange(n), np.arange(n)) % n
        ang = 2.0 * np.pi * jk.astype(np.float64) / n
        return np.cos(ang), np.sin(ang)

    cc, sc = cos_sin(GROUP_W)
    ct, st = cos_sin(t_len)
    chan = jnp.asarray(np.concatenate([cc, sc], axis=1).astype(np.float32)).astype(BF16)
    time = jnp.asarray(np.concatenate([ct, -st], axis=1).astype(np.float32)).astype(BF16)
    return chan, time


def _fourier(x, mod, layer, g, w_out, n_seq, t_len, tok_blk0, mod_row0):
    chan, time = _dft_tables(t_len)
    seq = lambda s: (tok_blk0 + s, 0)
    const = lambda s: (0, 0)
    return pl.pallas_call(
        functools.partial(_fourier_kernel, mod_row0=mod_row0),
        grid=(n_seq,),
        in_specs=[
            pl.BlockSpec((t_len, D_MODEL), seq),
            _mod_spec(layer, 1),
            pl.BlockSpec((1, D_MODEL), const),
            _resident((GROUP_W, 2 * GROUP_W), const),
            _resident((t_len, 2 * t_len), const),
            _resident((D_MODEL, D_MODEL), const),
        ],
        out_specs=pl.BlockSpec((t_len, D_MODEL), seq),
        out_shape=jax.ShapeDtypeStruct((N_TOK, D_MODEL), F32),
        input_output_aliases={0: 0},
        scratch_shapes=[pltpu.VMEM((D_MODEL, D_MODEL), BF16)],
        compiler_params=_cparams(1),
        name="fourier_mixer",
    )(x, mod, g.reshape(1, D_MODEL), chan, time, w_out)


def kernel(x_prompt, x_sample, cache_k, cache_v, state_lru_fwd, state_lru_bwd, c, c_ctx, w_ada, b_ada, norm_g, ffn1_gate, ffn1_up, ffn1_down, ffn2_gate, ffn2_up, ffn2_down, w_in, q_norm_g, k_norm_g, rpb, conv_w, conv_b, lru_w_r, lru_b_r, lru_w_i, lru_b_i, lru_lambda, w_out_ab, w_out_c):
    assert DEPTH == 2, "one neighbourhood/RG-LRU layer followed by one Fourier layer"
    cond = jnp.concatenate(
        [c_ctx[None, :], c, jnp.zeros((MOD_ROWS - 1 - DEC_BATCH, D_MODEL), F32)], axis=0)
    mod = _adaln(cond, w_ada, b_ada)

    ffn1 = (ffn1_gate, ffn1_up, ffn1_down)
    ffn2 = (ffn2_gate, ffn2_up, ffn2_down)

    x = _ffn((x_prompt.reshape(N_CTX_TOK, D_MODEL), x_sample.reshape(N_SMP_TOK, D_MODEL)),
             mod, norm_g[0, 0], *ffn1, 0, 0)
    q, k, v, xb, gb, new_k, new_v = _proj(x, mod, 0, norm_g[0, 1], w_in[0], q_norm_g[0], k_norm_g[0])
    o_ctx = _ctx_attn(q, k, v)
    o_smp = _na_attn(q, k, v,
                     cache_k[:, 0].reshape(DEC_BATCH, PAST_LEN, NA_WIDTH),
                     cache_v[:, 0].reshape(DEC_BATCH, PAST_LEN, NA_WIDTH), rpb[0])
    w_bd, b_bd = _lru_gate_weights(lru_w_r[0], lru_b_r[0], lru_w_i[0], lru_b_i[0])
    h0_ctx = jnp.zeros((BATCH, 2, LRU_WIDTH), F32)
    h0_smp = jnp.stack([state_lru_fwd[:, 0], state_lru_bwd[:, 0]], axis=1).astype(F32)
    yb_ctx, hl_ctx = _lru(xb, gb, conv_w[0], conv_b[0], w_bd, b_bd, lru_lambda[0],
                          h0_ctx, BATCH, SEQ, 0, LRU_WIDTH)
    yb_smp, _ = _lru(xb, gb, conv_w[0], conv_b[0], w_bd, b_bd, lru_lambda[0],
                     h0_smp, DEC_BATCH, DEC_SEQ, N_CTX_TOK // DEC_SEQ, LRU_SUB)
    x = _ffn((x,), mod, norm_g[0, 2], *ffn2, 0, 6, mixer_out=(o_ctx, o_smp, yb_ctx, yb_smp, w_out_ab))

    x = _ffn((x,), mod, norm_g[1, 0], *ffn1, 1, 0)
    w_c = w_out_c[0]
    x = _fourier(x, mod, 1, norm_g[1, 1], w_c, BATCH, SEQ, 0, 0)
    x = _fourier(x, mod, 1, norm_g[1, 1], w_c, DEC_BATCH, DEC_SEQ, N_CTX_TOK // DEC_SEQ, 1)
    y_prompt, y_sample = _ffn((x,), mod, norm_g[1, 2], *ffn2, 1, 6, split_out=True)

    return (y_prompt.reshape(BATCH, SEQ, D_MODEL), y_sample.reshape(DEC_BATCH, DEC_SEQ, D_MODEL),
            new_k.reshape(BATCH, 1, SEQ, NA_HEADS, HEAD_DIM), new_v.reshape(BATCH, 1, SEQ, NA_HEADS, HEAD_DIM),
            hl_ctx[:, 0].reshape(BATCH, 1, LRU_WIDTH), hl_ctx[:, 1].reshape(BATCH, 1, LRU_WIDTH))
```

```python
import functools

import numpy as np
import jax
import jax.numpy as jnp
from jax import lax
from jax.experimental import pallas as pl
from jax.experimental.pallas import tpu as pltpu

F32 = jnp.float32
BF16 = jnp.bfloat16

D_MODEL = 1024
BATCH = 16
SEQ = 256
DEPTH = 2
DEC_BATCH = 2
DEC_SEQ = 1024
PAST_LEN = 256
GRID_W = 64
HEAD_DIM = 64
NA_WIDTH = 512
NA_HEADS = 8
WIN_H = 8
WIN_W = 16
LRU_WIDTH = 512
LRU_BLOCKS = 8
LRU_BLOCK = 64
LRU_C = 8.0
LRU_SUB = 256
CONV_W = 4
FOURIER_GROUPS = 4
GROUP_W = D_MODEL // FOURIER_GROUPS
D_FF = 2816
N_MOD = 9
IN_WIDTH = 3 * NA_WIDTH + 2 * LRU_WIDTH
EPS = 1e-6

N_CTX_TOK = BATCH * SEQ
N_SMP_TOK = DEC_BATCH * DEC_SEQ
N_TOK = N_CTX_TOK + N_SMP_TOK
MOD_ROWS = 8
MOD_WIDTH = N_MOD * D_MODEL
ROWS = DEC_SEQ // GRID_W
KH = min(WIN_H, ROWS)

TOKEN_TILE = 512
N_CTX_TILES = N_CTX_TOK // TOKEN_TILE
FFN_TILE = 512
FF_TILE = 256
FF_CHUNKS = D_FF // FF_TILE
FF_STAGE_SLOTS = 2
SUBLANES = 8
LANES = 128
VMEM_LIMIT = 56 * 1024 * 1024

NA_Q_ROWS = 4
NA_GROUPS = ROWS // NA_Q_ROWS
NA_K_ROWS = 12
NA_Q = NA_Q_ROWS * GRID_W
NA_K = NA_K_ROWS * GRID_W
N_DR = 2 * WIN_H - 1
N_DC = 2 * WIN_W - 1
N_DR_PAIRS = N_DR + 1


def _cparams(n_axes):
    return pltpu.CompilerParams(
        dimension_semantics=("arbitrary",) * n_axes, vmem_limit_bytes=VMEM_LIMIT)


def _resident(block_shape, index_map):
    return pl.BlockSpec(block_shape, index_map, pipeline_mode=pl.Buffered(1))


def _mod_spec():
    return _resident((MOD_ROWS, MOD_WIDTH), lambda i: (0, 0))


def _mod_row_of_tile(i, tile=TOKEN_TILE):
    n_ctx_tiles = N_CTX_TOK // tile
    tiles_per_seq = DEC_SEQ // tile
    return jnp.where(i < n_ctx_tiles, 0, 1 + (i - n_ctx_tiles) // tiles_per_seq)


def _mod_vec(mod_ref, row, k):
    return mod_ref[pl.ds(row, 1), k * D_MODEL:(k + 1) * D_MODEL]


def _norm_mod(x, g, shift, scale):
    ms = jnp.mean(x * x, axis=-1, keepdims=True)
    y = x * lax.rsqrt(ms + EPS) * g
    return y * (1.0 + scale) + shift


def _adaln_slab(cctx_ref, c_ref, w_ref, b_ref, cond_ref):
    cond_ref[...] = jnp.zeros_like(cond_ref)
    cond_ref[0:1, :] = cctx_ref[...]
    cond_ref[1:1 + DEC_BATCH, :] = c_ref[...]
    cond = cond_ref[...]
    s = (cond * jax.nn.sigmoid(cond)).astype(BF16)
    return jnp.dot(s, w_ref[...].astype(BF16), preferred_element_type=F32) + b_ref[...]


def _adaln_specs(layer, slab):
    return ([pl.BlockSpec((1, D_MODEL), lambda i: (0, 0)),
             pl.BlockSpec((DEC_BATCH, D_MODEL), lambda i: (0, 0)),
             pl.BlockSpec((None, D_MODEL, slab), lambda i: (layer, 0, i)),
             pl.BlockSpec((None, 1, slab), lambda i: (layer, 0, i))],
            pl.BlockSpec((MOD_ROWS, slab), lambda i: (0, i)))


def _adaln_kernel(cctx_ref, c_ref, w_ref, b_ref, o_ref, cond_ref):
    o_ref[...] = _adaln_slab(cctx_ref, c_ref, w_ref, b_ref, cond_ref)


def _adaln(c_ctx, c, w_ada, b_ada, layer):
    slab = MOD_WIDTH // 4
    in_specs, out_spec = _adaln_specs(layer, slab)
    return pl.pallas_call(
        _adaln_kernel,
        grid=(MOD_WIDTH // slab,),
        in_specs=in_specs,
        out_specs=out_spec,
        out_shape=jax.ShapeDtypeStruct((MOD_ROWS, MOD_WIDTH), F32),
        scratch_shapes=[pltpu.VMEM((MOD_ROWS, D_MODEL), F32)],
        compiler_params=_cparams(1),
        name="adaln",
    )(c_ctx, c, w_ada, b_ada)


def _ffn_weight_copy(w_hbm, stage_ref, sem_ref, layer, j, ff_axis):
    ff = pl.ds(j * FF_TILE, FF_TILE)
    src = w_hbm.at[layer, :, ff] if ff_axis == 1 else w_hbm.at[layer, ff, :]
    slot = j % FF_STAGE_SLOTS
    return pltpu.make_async_copy(src, stage_ref.at[slot], sem_ref.at[slot])


def _mixer_out_copy(w_hbm, stage_ref, sem_ref, j):
    rows = stage_ref.shape[1]
    slot = j % stage_ref.shape[0]
    return pltpu.make_async_copy(w_hbm.at[0, pl.ds(j * rows, rows), :], stage_ref.at[slot], sem_ref.at[slot])


def _ffn_kernel(*refs, layer, mod_base, split_in, split_out, mixer_out, adaln_next):
    refs = list(refs)
    take = lambda n: [refs.pop(0) for _ in range(n)]
    x_refs = take(2 if split_in else 1)
    mix_refs = take(4) if mixer_out else None
    mod_ref, g_ref = take(2)
    wo_hbm = take(1)[0] if mixer_out else None
    ada_refs = take(4) if adaln_next else None
    wg_hbm, wu_hbm, wd_hbm = take(3)
    o_refs = take(2 if split_out else 1)
    modn_ref = take(1)[0] if adaln_next else None
    wg_bf, wu_bf, wd_bf, stg_g, stg_u, stg_d, sem_g, sem_u, sem_d = take(9)
    if adaln_next:
        modn_ref[...] = _adaln_slab(*ada_refs, take(1)[0])
    streams = ((wg_hbm, stg_g, sem_g, wg_bf, 1), (wu_hbm, stg_u, sem_u, wu_bf, 1),
               (wd_hbm, stg_d, sem_d, wd_bf, 0))

    i = pl.program_id(0)
    is_ctx = i < N_CTX_TOK // FFN_TILE
    if split_in:
        x = jnp.where(is_ctx, x_refs[0][...], x_refs[1][...])
    else:
        x = x_refs[0][...]
    row = _mod_row_of_tile(i, FFN_TILE)

    if mixer_out:
        wo_bf, stg_o, sem_o = take(3)
        rows = stg_o.shape[1]
        n_chunks = D_MODEL // rows

        @pl.when(i == 0)
        def _():
            for j in range(stg_o.shape[0]):
                _mixer_out_copy(wo_hbm, stg_o, sem_o, j).start()
            for j in range(n_chunks):
                _mixer_out_copy(wo_hbm, stg_o, sem_o, j).wait()
                wo_bf[j * rows:(j + 1) * rows, :] = stg_o[j % stg_o.shape[0]].astype(BF16)
                if j + stg_o.shape[0] < n_chunks:
                    _mixer_out_copy(wo_hbm, stg_o, sem_o, j + stg_o.shape[0]).start()

        oc_ref, os_ref, yc_ref, ys_ref = mix_refs
        cat = jnp.concatenate([jnp.where(is_ctx, oc_ref[...], os_ref[...]),
                               jnp.where(is_ctx, yc_ref[...], ys_ref[...])], axis=1)
        x = x + _mod_vec(mod_ref, row, mod_base - 1) * jnp.dot(cat, wo_bf[...], preferred_element_type=F32)

    h = _norm_mod(x, g_ref[...], _mod_vec(mod_ref, row, mod_base),
                  _mod_vec(mod_ref, row, mod_base + 1)).astype(BF16)

    def start_chunk(j):
        for w_hbm, stg, sem, _, ff_axis in streams:
            _ffn_weight_copy(w_hbm, stg, sem, layer, j, ff_axis).start()

    def finish_chunk(j):
        for w_hbm, stg, sem, w_bf, ff_axis in streams:
            _ffn_weight_copy(w_hbm, stg, sem, layer, j, ff_axis).wait()
            w_bf[j] = stg[j % FF_STAGE_SLOTS].astype(BF16)

    def run(stream_weights):
        if stream_weights:
            for j in range(FF_STAGE_SLOTS):
                start_chunk(j)
        acc = None
        for j in range(FF_CHUNKS):
            if stream_weights:
                finish_chunk(j)
                if j + FF_STAGE_SLOTS < FF_CHUNKS:
                    start_chunk(j + FF_STAGE_SLOTS)
            a = jnp.dot(h, wg_bf[j], preferred_element_type=F32)
            b = jnp.dot(h, wu_bf[j], preferred_element_type=F32)
            act = (a * jax.nn.sigmoid(a) * b).astype(BF16)
            y = jnp.dot(act, wd_bf[j], preferred_element_type=F32)
            acc = y if acc is None else acc + y
        res = x + 0.5 * _mod_vec(mod_ref, row, mod_base + 2) * acc
        if split_out:
            @pl.when(is_ctx)
            def _():
                o_refs[0][...] = res

            @pl.when(jnp.logical_not(is_ctx))
            def _():
                o_refs[1][...] = res
        else:
            o_refs[0][...] = res

    @pl.when(i == 0)
    def _():
        run(True)

    @pl.when(i > 0)
    def _():
        run(False)


def _ffn(xs, mod, g, wg, wu, wd, layer, mod_base, split_out=False, mixer_out=None, adaln_next=None):
    tm = FFN_TILE
    n_ctx_tiles = N_CTX_TOK // tm
    split_in = len(xs) == 2

    def tiles(width):
        return (pl.BlockSpec((tm, width), lambda i: (i, 0)),
                pl.BlockSpec((tm, width), lambda i: (jnp.minimum(i, n_ctx_tiles - 1), 0)),
                pl.BlockSpec((tm, width), lambda i: (jnp.maximum(i - n_ctx_tiles, 0), 0)))

    tok, ctx_tok, smp_tok = tiles(D_MODEL)
    full = jax.ShapeDtypeStruct((N_TOK, D_MODEL), F32)
    pair = [jax.ShapeDtypeStruct((N_CTX_TOK, D_MODEL), F32), jax.ShapeDtypeStruct((N_SMP_TOK, D_MODEL), F32)]
    hbm = pl.BlockSpec(memory_space=pl.ANY)
    in_specs = [ctx_tok, smp_tok] if split_in else [tok]
    operands = list(xs)
    scratch = [
        pltpu.VMEM((FF_CHUNKS, D_MODEL, FF_TILE), BF16),
        pltpu.VMEM((FF_CHUNKS, D_MODEL, FF_TILE), BF16),
        pltpu.VMEM((FF_CHUNKS, FF_TILE, D_MODEL), BF16),
        pltpu.VMEM((FF_STAGE_SLOTS, D_MODEL, FF_TILE), F32),
        pltpu.VMEM((FF_STAGE_SLOTS, D_MODEL, FF_TILE), F32),
        pltpu.VMEM((FF_STAGE_SLOTS, FF_TILE, D_MODEL), F32),
        pltpu.SemaphoreType.DMA((FF_STAGE_SLOTS,)),
        pltpu.SemaphoreType.DMA((FF_STAGE_SLOTS,)),
        pltpu.SemaphoreType.DMA((FF_STAGE_SLOTS,)),
    ]
    if mixer_out is not None:
        _, ctx_half, smp_half = tiles(NA_WIDTH)
        in_specs += [ctx_half, smp_half, ctx_half, smp_half]
        operands += list(mixer_out[:4])
    in_specs += [_mod_spec(), pl.BlockSpec((1, D_MODEL), lambda i: (0, 0))]
    operands += [mod, g.reshape(1, D_MODEL)]
    out_specs = [ctx_tok, smp_tok] if split_out else [tok]
    out_shape = pair if split_out else [full]
    if adaln_next is not None:
        ada_in, ada_out = _adaln_specs(adaln_next[4], MOD_WIDTH // (N_TOK // tm))
        in_specs += ada_in
        operands += list(adaln_next[:4])
        out_specs.append(ada_out)
        out_shape.append(jax.ShapeDtypeStruct((MOD_ROWS, MOD_WIDTH), F32))
        scratch.append(pltpu.VMEM((MOD_ROWS, D_MODEL), F32))
    if mixer_out is not None:
        in_specs.insert(len(in_specs) - (4 if adaln_next is not None else 0), hbm)
        operands.insert(len(operands) - (4 if adaln_next is not None else 0), mixer_out[4])
        scratch += [
            pltpu.VMEM((D_MODEL, D_MODEL), BF16),
            pltpu.VMEM((FF_STAGE_SLOTS, FF_TILE, D_MODEL), F32),
            pltpu.SemaphoreType.DMA((FF_STAGE_SLOTS,)),
        ]
    return pl.pallas_call(
        functools.partial(_ffn_kernel, layer=layer, mod_base=mod_base, split_in=split_in,
                          split_out=split_out, mixer_out=mixer_out is not None,
                          adaln_next=adaln_next is not None),
        grid=(N_TOK // tm,),
        in_specs=in_specs + [hbm, hbm, hbm],
        out_specs=out_specs,
        out_shape=out_shape,
        scratch_shapes=scratch,
        compiler_params=_cparams(1),
        name="ffn",
    )(*operands, wg, wu, wd)


def _head_rms_norm(z, g, ones_bd):
    z2 = z * z
    hi = z2.astype(BF16)
    lo = (z2 - hi.astype(F32)).astype(BF16)
    n = ones_bd.shape[0]
    parts = []
    for c in range(z.shape[1] // n):
        sl = slice(c * n, (c + 1) * n)
        parts.append(jnp.dot(hi[:, sl], ones_bd, preferred_element_type=F32)
                     + jnp.dot(lo[:, sl], ones_bd, preferred_element_type=F32))
    ss = jnp.concatenate(parts, axis=1)
    return z * lax.rsqrt(ss * (1.0 / HEAD_DIM) + EPS) * g


def _proj_kernel(x_ref, mod_ref, g_ref, w_ref, qkg_ref, ones_ref,
                 q_ref, k_ref, v_ref, xb_ref, gb_ref, kout_ref, vout_ref, w_bf_ref):
    i = pl.program_id(0)

    @pl.when(i == 0)
    def _():
        w_bf_ref[...] = w_ref[...].astype(BF16)

    x = x_ref[...]
    row = _mod_row_of_tile(i)
    h = _norm_mod(x, g_ref[...], _mod_vec(mod_ref, row, 3), _mod_vec(mod_ref, row, 4)).astype(BF16)

    def proj(part):
        return jnp.dot(h, w_bf_ref[:, part * NA_WIDTH:(part + 1) * NA_WIDTH], preferred_element_type=F32)

    ones_bd = ones_ref[...]
    q = _head_rms_norm(proj(0), qkg_ref[0:1, :], ones_bd) * (HEAD_DIM ** -0.5)
    q_ref[...] = q.astype(BF16)
    k = _head_rms_norm(proj(1), qkg_ref[1:2, :], ones_bd)
    k_ref[...] = k.astype(BF16)
    v = proj(2)
    v_ref[...] = v.astype(BF16)
    xb_ref[...] = proj(3)
    gb_ref[...] = proj(4)

    @pl.when(i < N_CTX_TILES)
    def _():
        kout_ref[...] = k.reshape(TOKEN_TILE, NA_HEADS, HEAD_DIM)
        vout_ref[...] = v.reshape(TOKEN_TILE, NA_HEADS, HEAD_DIM)


def _proj(x, mod, g, w_in, q_g, k_g):
    tm = TOKEN_TILE
    head = np.arange(2 * LANES) // HEAD_DIM
    ones_bd = jnp.asarray((head[:, None] == head[None, :]).astype(np.float32), dtype=BF16)
    tok = lambda i: (i, 0)
    const = lambda i: (0, 0)
    act_f32 = jax.ShapeDtypeStruct((N_TOK, NA_WIDTH), F32)
    act_bf16 = jax.ShapeDtypeStruct((N_TOK, NA_WIDTH), BF16)
    cache = jax.ShapeDtypeStruct((N_CTX_TOK, NA_HEADS, HEAD_DIM), F32)
    cache_spec = pl.BlockSpec((tm, NA_HEADS, HEAD_DIM), lambda i: (jnp.minimum(i, N_CTX_TILES - 1), 0, 0))
    return pl.pallas_call(
        _proj_kernel,
        grid=(N_TOK // tm,),
        in_specs=[
            pl.BlockSpec((tm, D_MODEL), tok),
            _mod_spec(),
            pl.BlockSpec((1, D_MODEL), const),
            _resident((D_MODEL, IN_WIDTH), const),
            pl.BlockSpec((2, NA_WIDTH), const),
            _resident((2 * LANES, 2 * LANES), const),
        ],
        out_specs=[pl.BlockSpec((tm, NA_WIDTH), tok)] * 5 + [cache_spec, cache_spec],
        out_shape=[act_bf16, act_bf16, act_bf16, act_f32, act_f32, cache, cache],
        scratch_shapes=[pltpu.VMEM((D_MODEL, IN_WIDTH), BF16)],
        compiler_params=_cparams(1),
        name="mixer_in_proj",
    )(x, mod, g.reshape(1, D_MODEL), w_in, jnp.tile(jnp.stack([q_g, k_g]), (1, NA_HEADS)), ones_bd)


def _head_masks():
    lane = lax.broadcasted_iota(jnp.int32, (1, 2 * HEAD_DIM), 1)
    return [lane < HEAD_DIM, lane >= HEAD_DIM]


def _ctx_attn_kernel(q_ref, k_ref, v_ref, o_ref):
    masks = _head_masks()
    for p in range(NA_HEADS // 2):
        sl = slice(2 * HEAD_DIM * p, 2 * HEAD_DIM * (p + 1))
        q2 = q_ref[:, sl]
        k2t = k_ref[:, sl].T
        v2 = v_ref[:, sl]
        out = None
        for e in range(2):
            qm = jnp.where(masks[e], q2, jnp.zeros_like(q2))
            s = jnp.dot(qm, k2t, preferred_element_type=F32)
            pe = jnp.exp(s - jnp.max(s, axis=-1, keepdims=True))
            den = jnp.sum(pe, axis=-1, keepdims=True)
            o = jnp.dot(pe.astype(BF16), v2, preferred_element_type=F32) / den
            out = o if out is None else jnp.where(masks[e], o, out)
        o_ref[:, sl] = out.astype(BF16)


def _ctx_attn(q, k, v):
    blk = pl.BlockSpec((SEQ, NA_WIDTH), lambda b: (b, 0))
    return pl.pallas_call(
        _ctx_attn_kernel,
        grid=(BATCH,),
        in_specs=[blk, blk, blk],
        out_specs=blk,
        out_shape=jax.ShapeDtypeStruct((N_CTX_TOK, NA_WIDTH), BF16),
        compiler_params=_cparams(1),
        name="ctx_attention",
    )(q, k, v)


def _na_build_bias_table(rpb_ref, table_ref):
    qc = lax.broadcasted_iota(jnp.int32, (GRID_W, LANES), 0)
    lane = lax.broadcasted_iota(jnp.int32, (GRID_W, LANES), 1)
    kc = lane % GRID_W
    col_start = jnp.clip(qc - WIN_W // 2, 0, GRID_W - WIN_W)
    col_in = (kc >= col_start) & (kc < col_start + WIN_W)
    neg = jnp.full((GRID_W, LANES), -jnp.inf, F32)

    def toeplitz(h, dr, lane0):
        if dr < 0 or dr >= N_DR:
            return neg
        w = jnp.broadcast_to(rpb_ref[h, dr:dr + 1, :], (GRID_W, LANES))
        return pltpu.roll(w, (lane0 - (WIN_W - 1)) % LANES, 1, stride=1, stride_axis=0)

    for h in range(NA_HEADS):
        for i in range(N_DR_PAIRS):
            t = jnp.where(lane < GRID_W, toeplitz(h, i - 1, 0), toeplitz(h, i, GRID_W))
            table_ref[h, i] = jnp.where(col_in, t, neg)


def _na_kernel(q_ref, k_ref, v_ref, kc_ref, vc_ref, rpb_ref, o_ref, table_ref):
    b = pl.program_id(0)
    g = pl.program_id(1)

    @pl.when((b == 0) & (g == 0))
    def _():
        _na_build_bias_table(rpb_ref, table_ref)

    win_row0 = jnp.where(g < NA_GROUPS // 2, 0, ROWS - NA_K_ROWS)
    start = pl.multiple_of(win_row0 * GRID_W, GRID_W)
    q_row = g * NA_Q_ROWS + lax.broadcasted_iota(jnp.int32, (NA_Q, 1), 0) // GRID_W
    k_row = win_row0 + lax.broadcasted_iota(jnp.int32, (1, NA_K), 1) // GRID_W
    row_start = jnp.clip(q_row - KH // 2, 0, ROWS - KH)
    row_in = (k_row >= row_start) & (k_row < row_start + KH)
    masks = _head_masks()
    for p in range(NA_HEADS // 2):
        sl = slice(2 * HEAD_DIM * p, 2 * HEAD_DIM * (p + 1))
        q2 = q_ref[:, sl]
        klt = k_ref[pl.ds(start, NA_K), sl].T
        vl = v_ref[pl.ds(start, NA_K), sl]
        kct = kc_ref[0, :, sl].astype(BF16).T
        vc = vc_ref[0, :, sl].astype(BF16)
        out = None
        for e in range(2):
            head = 2 * p + e
            bias_rows = []
            for a in range(NA_Q_ROWS):
                tiles = []
                for m in range(NA_K_ROWS // 2):
                    dr = win_row0 + 2 * m - (g * NA_Q_ROWS + a) + (WIN_H - 1)
                    tiles.append(table_ref[head, jnp.clip(dr + 1, 0, N_DR_PAIRS - 1)])
                bias_rows.append(jnp.concatenate(tiles, axis=1))
            bias = jnp.concatenate(bias_rows, axis=0)
            qm = jnp.where(masks[e], q2, jnp.zeros_like(q2))
            s_loc = jnp.where(row_in, jnp.dot(qm, klt, preferred_element_type=F32) + bias, -jnp.inf)
            s_ctx = jnp.dot(qm, kct, preferred_element_type=F32)
            m_max = jnp.maximum(jnp.max(s_loc, axis=-1, keepdims=True),
                                jnp.max(s_ctx, axis=-1, keepdims=True))
            p_loc = jnp.exp(s_loc - m_max)
            p_ctx = jnp.exp(s_ctx - m_max)
            den = jnp.sum(p_loc, axis=-1, keepdims=True) + jnp.sum(p_ctx, axis=-1, keepdims=True)
            o = (jnp.dot(p_loc.astype(BF16), vl, preferred_element_type=F32)
                 + jnp.dot(p_ctx.astype(BF16), vc, preferred_element_type=F32)) / den
            out = o if out is None else jnp.where(masks[e], o, out)
        o_ref[:, sl] = out.astype(BF16)


def _na_attn(q, k, v, k_ctx, v_ctx, rpb_e):
    smp_blk0 = N_CTX_TOK // DEC_SEQ
    q_blk0 = N_CTX_TOK // NA_Q
    kv = pl.BlockSpec((DEC_SEQ, NA_WIDTH), lambda b, g: (smp_blk0 + b, 0))
    ctx = pl.BlockSpec((1, PAST_LEN, NA_WIDTH), lambda b, g: (b, 0, 0))
    rpb_pad = jnp.pad(rpb_e.astype(F32), ((0, 0), (0, 0), (0, LANES - N_DC)))
    return pl.pallas_call(
        _na_kernel,
        grid=(DEC_BATCH, NA_GROUPS),
        in_specs=[
            pl.BlockSpec((NA_Q, NA_WIDTH), lambda b, g: (q_blk0 + b * NA_GROUPS + g, 0)),
            kv, kv, ctx, ctx,
            pl.BlockSpec((NA_HEADS, N_DR, LANES), lambda b, g: (0, 0, 0)),
        ],
        out_specs=pl.BlockSpec((NA_Q, NA_WIDTH), lambda b, g: (b * NA_GROUPS + g, 0)),
        out_shape=jax.ShapeDtypeStruct((N_SMP_TOK, NA_WIDTH), BF16),
        scratch_shapes=[pltpu.VMEM((NA_HEADS, N_DR_PAIRS, GRID_W, LANES), F32)],
        compiler_params=_cparams(2),
        name="neighbourhood_attention",
    )(q, k, v, k_ctx, v_ctx, rpb_pad)


def _sigmoid(x):
    return 0.5 * jnp.tanh(0.5 * x) + 0.5


def _lru_kernel(*refs, zero_state):
    refs = list(refs)
    xb_ref, gb_ref, cw_ref, cb_ref, w_ref, b_ref, lam_ref = refs[:7]
    h0f_ref, h0b_ref = (None, None) if zero_state else refs[7:9]
    y_ref, hlf_ref, hlb_ref, af_ref, uf_ref, ab_ref, ub_ref = refs[-7:]
    t_len, width = xb_ref.shape
    n_blk = t_len // SUBLANES
    row = lax.broadcasted_iota(jnp.int32, (t_len, 1), 0)
    in_block = lax.broadcasted_iota(jnp.int32, (1, SUBLANES, 1), 1)

    def shifted(z, s):
        rolled = pltpu.roll(z, (-s) % t_len, axis=0)
        ok = (row + s >= 0) & (row + s < t_len)
        return jnp.where(ok, rolled, 0.0)

    left = (CONV_W - 1) // 2
    for c in range(width // LRU_SUB):
        cs = slice(c * LRU_SUB, (c + 1) * LRU_SUB)
        x = xb_ref[:, cs]
        xc = cb_ref[:, cs]
        for j in range(CONV_W):
            tap = x if j == left else shifted(x, j - left)
            xc = xc + tap * cw_ref[j:j + 1, cs]
        gates = jnp.dot(xc.astype(BF16), w_ref[c], preferred_element_type=F32) + b_ref[c]
        for d, (a_ref, u_ref) in enumerate(((af_ref, uf_ref), (ab_ref, ub_ref))):
            r_gate = _sigmoid(gates[:, (2 * d) * LRU_SUB:(2 * d + 1) * LRU_SUB])
            i_gate = _sigmoid(gates[:, (2 * d + 1) * LRU_SUB:(2 * d + 2) * LRU_SUB])
            lam = lam_ref[0, d:d + 1, cs]
            log_sig = jnp.minimum(lam, 0.0) - jnp.log1p(jnp.exp(-jnp.abs(lam)))
            log_a = LRU_C * r_gate * log_sig
            a = jnp.exp(log_a)
            var = -jnp.tanh(log_a) * (a * a + 1.0)
            u = jnp.where(var > 0.0, var * lax.rsqrt(var), 0.0) * (i_gate * xc)
            a = a.reshape(n_blk, SUBLANES, LRU_SUB)
            u = u.reshape(n_blk, SUBLANES, LRU_SUB)
            step = 1
            while step < SUBLANES:
                if d == 0:
                    ok, shift = in_block >= step, step
                else:
                    ok, shift = in_block < SUBLANES - step, SUBLANES - step
                a_prev = jnp.where(ok, pltpu.roll(a, shift, axis=1), 1.0)
                u_prev = jnp.where(ok, pltpu.roll(u, shift, axis=1), 0.0)
                u = u + a * u_prev
                a = a * a_prev
                step *= 2
            a_ref[:, cs] = a.reshape(t_len, LRU_SUB)
            u_ref[:, cs] = u.reshape(t_len, LRU_SUB)

    def body(i, carry):
        cf, cb = carry
        f0 = pl.multiple_of(i * SUBLANES, SUBLANES)
        b0 = pl.multiple_of((n_blk - 1 - i) * SUBLANES, SUBLANES)
        hf = uf_ref[pl.ds(f0, SUBLANES), :] + af_ref[pl.ds(f0, SUBLANES), :] * cf
        hb = ub_ref[pl.ds(b0, SUBLANES), :] + ab_ref[pl.ds(b0, SUBLANES), :] * cb
        uf_ref[pl.ds(f0, SUBLANES), :] = hf
        ub_ref[pl.ds(b0, SUBLANES), :] = hb
        cf = jnp.broadcast_to(hf[SUBLANES - 1:SUBLANES, :], (SUBLANES, width))
        cb = jnp.broadcast_to(hb[0:1, :], (SUBLANES, width))
        return cf, cb

    if zero_state:
        c0f = c0b = jnp.zeros((SUBLANES, width), F32)
    else:
        c0f = jnp.broadcast_to(h0f_ref[0], (SUBLANES, width))
        c0b = jnp.broadcast_to(h0b_ref[0], (SUBLANES, width))
    cf, cb = lax.fori_loop(0, n_blk, body, (c0f, c0b))
    hlf_ref[0] = cf[0:1, :]
    hlb_ref[0] = cb[0:1, :]
    y_ref[...] = ((uf_ref[...] + ub_ref[...]) * jax.nn.gelu(gb_ref[...])).astype(BF16)


def _lru(xb, gb, conv_w, conv_b, w_bd, b_bd, lam, h0, n_seq, t_len, tok_blk0, width):
    n_sub = width // LRU_SUB
    seq = lambda s, c: (tok_blk0 + s, c)
    state = pl.BlockSpec((1, 1, width), lambda s, c: (s, 0, c))
    state_shape = jax.ShapeDtypeStruct((n_seq, 1, LRU_WIDTH), F32)
    return pl.pallas_call(
        functools.partial(_lru_kernel, zero_state=h0 is None),
        grid=(n_seq, LRU_WIDTH // width),
        in_specs=[
            pl.BlockSpec((t_len, width), seq),
            pl.BlockSpec((t_len, width), seq),
            pl.BlockSpec((CONV_W, width), lambda s, c: (0, c)),
            pl.BlockSpec((1, width), lambda s, c: (0, c)),
            pl.BlockSpec((n_sub, LRU_SUB, 4 * LRU_SUB), lambda s, c: (c, 0, 0)),
            pl.BlockSpec((n_sub, 1, 4 * LRU_SUB), lambda s, c: (c, 0, 0)),
            pl.BlockSpec((1, 2, width), lambda s, c: (0, 0, c)),
        ] + ([] if h0 is None else [state, state]),
        out_specs=[pl.BlockSpec((t_len, width), lambda s, c: (s, c)), state, state],
        out_shape=[jax.ShapeDtypeStruct((n_seq * t_len, LRU_WIDTH), BF16), state_shape, state_shape],
        scratch_shapes=[pltpu.VMEM((t_len, width), F32)] * 4,
        compiler_params=_cparams(2),
        name="rglru",
    )(xb, gb, conv_w, conv_b.reshape(1, LRU_WIDTH), w_bd, b_bd, lam.reshape(1, 2, LRU_WIDTH),
      *(() if h0 is None else h0))


def _lru_gate_weights(w_r, b_r, w_i, b_i):
    n_grp = LRU_WIDTH // LRU_SUB
    bpg = LRU_SUB // LRU_BLOCK
    w_all = jnp.stack([w_r[0], w_i[0], w_r[1], w_i[1]])
    w_all = w_all.reshape(4, n_grp, bpg, LRU_BLOCK, LRU_BLOCK)
    eye = jnp.eye(bpg, dtype=w_all.dtype)
    w_bd = jnp.einsum('tcnij,nm->cnitmj', w_all, eye).reshape(n_grp, LRU_SUB, 4 * LRU_SUB)
    b_all = jnp.stack([b_r[0], b_i[0], b_r[1], b_i[1]]).reshape(4, n_grp, LRU_SUB)
    b_bd = jnp.transpose(b_all, (1, 0, 2)).reshape(n_grp, 1, 4 * LRU_SUB)
    return w_bd.astype(BF16), b_bd


def _fourier_kernel(x_ref, mod_ref, g_ref, cs_ref, ct_ref, w_ref, o_ref, w_bf_ref, *, mod_row0, t_len):
    @pl.when(pl.program_id(0) == 0)
    def _():
        w_bf_ref[...] = w_ref[...].astype(BF16)

    x = x_ref[...]
    row = mod_row0 + pl.program_id(0) if mod_row0 else 0
    h = _norm_mod(x, g_ref[...], _mod_vec(mod_ref, row, 3), _mod_vec(mod_ref, row, 4)).astype(BF16)
    cos_parts, sin_parts = [], []
    for g in range(FOURIER_GROUPS):
        ab = jnp.dot(h[:, g * GROUP_W:(g + 1) * GROUP_W], cs_ref[...], preferred_element_type=F32)
        cos_parts.append(ab[:, :GROUP_W])
        sin_parts.append(ab[:, GROUP_W:])
    cos_all = jnp.concatenate(cos_parts, axis=1).astype(BF16)
    sin_all = jnp.concatenate(sin_parts, axis=1).astype(BF16)
    f_parts = []
    for q in range(x.shape[0] // t_len):
        rows = slice(q * t_len, (q + 1) * t_len)
        stacked = jnp.concatenate([cos_all[rows], sin_all[rows]], axis=0)
        f_parts.append(jnp.dot(ct_ref[...], stacked, preferred_element_type=F32))
    f = jnp.concatenate(f_parts, axis=0) * ((t_len * GROUP_W) ** -0.5)
    y = jnp.dot(f.astype(BF16), w_bf_ref[...], preferred_element_type=F32)
    o_ref[...] = x + _mod_vec(mod_ref, row, 5) * y


def _dft_tables(t_len):
    def cos_sin(n):
        jk = np.outer(np.arange(n), np.arange(n)) % n
        ang = 2.0 * np.pi * jk.astype(np.float64) / n
        return np.cos(ang), np.sin(ang)

    cc, sc = cos_sin(GROUP_W)
    ct, st = cos_sin(t_len)
    chan = jnp.asarray(np.concatenate([cc, sc], axis=1).astype(np.float32)).astype(BF16)
    time = jnp.asarray(np.concatenate([ct, -st], axis=1).astype(np.float32)).astype(BF16)
    return chan, time


def _fourier(x, mod, g, w_out, n_seq, t_len, tok_blk0, mod_row0, seq_per_step):
    assert seq_per_step == 1 or mod_row0 == 0
    chan, time = _dft_tables(t_len)
    rows = seq_per_step * t_len
    seq = lambda s: (tok_blk0 + s, 0)
    const = lambda s: (0, 0)
    return pl.pallas_call(
        functools.partial(_fourier_kernel, mod_row0=mod_row0, t_len=t_len),
        grid=(n_seq // seq_per_step,),
        in_specs=[
            pl.BlockSpec((rows, D_MODEL), seq),
            _mod_spec(),
            pl.BlockSpec((1, D_MODEL), const),
            _resident((GROUP_W, 2 * GROUP_W), const),
            _resident((t_len, 2 * t_len), const),
            _resident((D_MODEL, D_MODEL), const),
        ],
        out_specs=pl.BlockSpec((rows, D_MODEL), seq),
        out_shape=jax.ShapeDtypeStruct((N_TOK, D_MODEL), F32),
        input_output_aliases={0: 0},
        scratch_shapes=[pltpu.VMEM((D_MODEL, D_MODEL), BF16)],
        compiler_params=_cparams(1),
        name="fourier_mixer",
    )(x, mod, g.reshape(1, D_MODEL), chan, time, w_out)


def kernel(x_prompt, x_sample, cache_k, cache_v, state_lru_fwd, state_lru_bwd, c, c_ctx, w_ada, b_ada, norm_g, ffn1_gate, ffn1_up, ffn1_down, ffn2_gate, ffn2_up, ffn2_down, w_in, q_norm_g, k_norm_g, rpb, conv_w, conv_b, lru_w_r, lru_b_r, lru_w_i, lru_b_i, lru_lambda, w_out_ab, w_out_c):
    assert DEPTH == 2, "one neighbourhood/RG-LRU layer followed by one Fourier layer"
    c_ctx2 = c_ctx.reshape(1, D_MODEL)
    b_ada3 = b_ada.reshape(DEPTH, 1, MOD_WIDTH)
    mod0 = _adaln(c_ctx2, c, w_ada, b_ada3, 0)

    ffn1 = (ffn1_gate, ffn1_up, ffn1_down)
    ffn2 = (ffn2_gate, ffn2_up, ffn2_down)

    x, mod1 = _ffn((x_prompt.reshape(N_CTX_TOK, D_MODEL), x_sample.reshape(N_SMP_TOK, D_MODEL)),
                   mod0, norm_g[0, 0], *ffn1, 0, 0, adaln_next=(c_ctx2, c, w_ada, b_ada3, 1))
    q, k, v, xb, gb, new_k, new_v = _proj(x, mod0, norm_g[0, 1], w_in[0], q_norm_g[0], k_norm_g[0])
    o_ctx = _ctx_attn(q, k, v)
    o_smp = _na_attn(q, k, v,
                     cache_k[:, 0].reshape(DEC_BATCH, PAST_LEN, NA_WIDTH),
                     cache_v[:, 0].reshape(DEC_BATCH, PAST_LEN, NA_WIDTH), rpb[0])
    w_bd, b_bd = _lru_gate_weights(lru_w_r[0], lru_b_r[0], lru_w_i[0], lru_b_i[0])
    yb_ctx, new_hf, new_hb = _lru(xb, gb, conv_w[0], conv_b[0], w_bd, b_bd, lru_lambda[0],
                                  None, BATCH, SEQ, 0, LRU_WIDTH)
    yb_smp, _, _ = _lru(xb, gb, conv_w[0], conv_b[0], w_bd, b_bd, lru_lambda[0],
                        (state_lru_fwd, state_lru_bwd), DEC_BATCH, DEC_SEQ, N_CTX_TOK // DEC_SEQ, LRU_SUB)
    (x,) = _ffn((x,), mod0, norm_g[0, 2], *ffn2, 0, 6, mixer_out=(o_ctx, o_smp, yb_ctx, yb_smp, w_out_ab))

    (x,) = _ffn((x,), mod1, norm_g[1, 0], *ffn1, 1, 0)
    x = _fourier(x, mod1, norm_g[1, 1], w_out_c[0], BATCH, SEQ, 0, 0, TOKEN_TILE // SEQ)
    x = _fourier(x, mod1, norm_g[1, 1], w_out_c[0], DEC_BATCH, DEC_SEQ, N_CTX_TOK // DEC_SEQ, 1, 1)
    y_prompt, y_sample = _ffn((x,), mod1, norm_g[1, 2], *ffn2, 1, 6, split_out=True)

    return (y_prompt.reshape(BATCH, SEQ, D_MODEL), y_sample.reshape(DEC_BATCH, DEC_SEQ, D_MODEL),
            new_k.reshape(BATCH, 1, SEQ, NA_HEADS, HEAD_DIM), new_v.reshape(BATCH, 1, SEQ, NA_HEADS, HEAD_DIM),
            new_hf, new_hb)
```

```python
import functools

import numpy as np
import jax
import jax.numpy as jnp
from jax import lax
from jax.experimental import pallas as pl
from jax.experimental.pallas import tpu as pltpu

F32 = jnp.float32
BF16 = jnp.bfloat16

D_MODEL = 1024
BATCH = 16
SEQ = 256
DEPTH = 2
DEC_BATCH = 2
DEC_SEQ = 1024
PAST_LEN = 256
GRID_W = 64
HEAD_DIM = 64
NA_WIDTH = 512
NA_HEADS = 8
WIN_H = 8
WIN_W = 16
LRU_WIDTH = 512
LRU_BLOCKS = 8
LRU_BLOCK = 64
LRU_C = 8.0
LRU_SUB = 256
CONV_W = 4
FOURIER_GROUPS = 4
GROUP_W = D_MODEL // FOURIER_GROUPS
D_FF = 2816
N_MOD = 9
IN_WIDTH = 3 * NA_WIDTH + 2 * LRU_WIDTH
EPS = 1e-6

N_CTX_TOK = BATCH * SEQ
N_SMP_TOK = DEC_BATCH * DEC_SEQ
N_TOK = N_CTX_TOK + N_SMP_TOK
MOD_ROWS = 8
MOD_WIDTH = N_MOD * D_MODEL
ROWS = DEC_SEQ // GRID_W
KH = min(WIN_H, ROWS)

TOKEN_TILE = 512
N_CTX_TILES = N_CTX_TOK // TOKEN_TILE
FFN_TILE = 512
FF_TILE = 256
FF_CHUNKS = D_FF // FF_TILE
FF_STAGE_SLOTS = 2
SUBLANES = 8
LANES = 128
VMEM_LIMIT = 56 * 1024 * 1024

NA_Q_ROWS = 4
NA_GROUPS = ROWS // NA_Q_ROWS
NA_K_ROWS = 12
NA_Q = NA_Q_ROWS * GRID_W
NA_K = NA_K_ROWS * GRID_W
N_DR = 2 * WIN_H - 1
N_DC = 2 * WIN_W - 1
N_DR_PAIRS = N_DR + 1


def _cparams(n_axes):
    return pltpu.CompilerParams(
        dimension_semantics=("arbitrary",) * n_axes, vmem_limit_bytes=VMEM_LIMIT)


def _resident(block_shape, index_map):
    return pl.BlockSpec(block_shape, index_map, pipeline_mode=pl.Buffered(1))


def _mod_spec():
    return _resident((MOD_ROWS, MOD_WIDTH), lambda i: (0, 0))


def _mod_row_of_tile(i, tile=TOKEN_TILE):
    n_ctx_tiles = N_CTX_TOK // tile
    tiles_per_seq = DEC_SEQ // tile
    return jnp.where(i < n_ctx_tiles, 0, 1 + (i - n_ctx_tiles) // tiles_per_seq)


def _mod_vec(mod_ref, row, k):
    return mod_ref[pl.ds(row, 1), k * D_MODEL:(k + 1) * D_MODEL]


def _norm_mod(x, g, shift, scale):
    ms = jnp.mean(x * x, axis=-1, keepdims=True)
    y = x * lax.rsqrt(ms + EPS) * g
    return y * (1.0 + scale) + shift


def _adaln_slab(cctx_ref, c_ref, w_ref, b_ref, cond_ref):
    cond_ref[...] = jnp.zeros_like(cond_ref)
    cond_ref[0:1, :] = cctx_ref[...]
    cond_ref[1:1 + DEC_BATCH, :] = c_ref[...]
    cond = cond_ref[...]
    s = (cond * jax.nn.sigmoid(cond)).astype(BF16)
    return jnp.dot(s, w_ref[...].astype(BF16), preferred_element_type=F32) + b_ref[...]


def _adaln_specs(layer, slab):
    return ([pl.BlockSpec((1, D_MODEL), lambda i: (0, 0)),
             pl.BlockSpec((DEC_BATCH, D_MODEL), lambda i: (0, 0)),
             pl.BlockSpec((None, D_MODEL, slab), lambda i: (layer, 0, i)),
             pl.BlockSpec((None, 1, slab), lambda i: (layer, 0, i))],
            pl.BlockSpec((MOD_ROWS, slab), lambda i: (0, i)))


def _adaln_kernel(cctx_ref, c_ref, w_ref, b_ref, o_ref, cond_ref):
    o_ref[...] = _adaln_slab(cctx_ref, c_ref, w_ref, b_ref, cond_ref)


def _adaln(c_ctx, c, w_ada, b_ada, layer):
    slab = MOD_WIDTH // 4
    in_specs, out_spec = _adaln_specs(layer, slab)
    return pl.pallas_call(
        _adaln_kernel,
        grid=(MOD_WIDTH // slab,),
        in_specs=in_specs,
        out_specs=out_spec,
        out_shape=jax.ShapeDtypeStruct((MOD_ROWS, MOD_WIDTH), F32),
        scratch_shapes=[pltpu.VMEM((MOD_ROWS, D_MODEL), F32)],
        compiler_params=_cparams(1),
        name="adaln",
    )(c_ctx, c, w_ada, b_ada)


def _ffn_weight_copy(w_hbm, stage_ref, sem_ref, layer, j, ff_axis):
    ff = pl.ds(j * FF_TILE, FF_TILE)
    src = w_hbm.at[layer, :, ff] if ff_axis == 1 else w_hbm.at[layer, ff, :]
    slot = j % FF_STAGE_SLOTS
    return pltpu.make_async_copy(src, stage_ref.at[slot], sem_ref.at[slot])


def _mixer_out_copy(w_hbm, stage_ref, sem_ref, j):
    rows = stage_ref.shape[1]
    slot = j % stage_ref.shape[0]
    return pltpu.make_async_copy(w_hbm.at[0, pl.ds(j * rows, rows), :], stage_ref.at[slot], sem_ref.at[slot])


def _ffn_kernel(*refs, layer, mod_base, split_in, split_out, mixer_out, adaln_next):
    refs = list(refs)
    take = lambda n: [refs.pop(0) for _ in range(n)]
    x_refs = take(2 if split_in else 1)
    mix_refs = take(4) if mixer_out else None
    mod_ref, g_ref = take(2)
    wo_hbm = take(1)[0] if mixer_out else None
    ada_refs = take(4) if adaln_next else None
    wg_hbm, wu_hbm, wd_hbm = take(3)
    o_refs = take(2 if split_out else 1)
    modn_ref = take(1)[0] if adaln_next else None
    wg_bf, wu_bf, wd_bf, stg_g, stg_u, stg_d, sem_g, sem_u, sem_d = take(9)
    if adaln_next:
        modn_ref[...] = _adaln_slab(*ada_refs, take(1)[0])
    streams = ((wg_hbm, stg_g, sem_g, wg_bf, 1), (wu_hbm, stg_u, sem_u, wu_bf, 1),
               (wd_hbm, stg_d, sem_d, wd_bf, 0))

    i = pl.program_id(0)
    is_ctx = i < N_CTX_TOK // FFN_TILE
    if split_in:
        x = jnp.where(is_ctx, x_refs[0][...], x_refs[1][...])
    else:
        x = x_refs[0][...]
    row = _mod_row_of_tile(i, FFN_TILE)

    if mixer_out:
        wo_bf, stg_o, sem_o = take(3)
        rows = stg_o.shape[1]
        n_chunks = D_MODEL // rows

        @pl.when(i == 0)
        def _():
            for j in range(stg_o.shape[0]):
                _mixer_out_copy(wo_hbm, stg_o, sem_o, j).start()
            for j in range(n_chunks):
                _mixer_out_copy(wo_hbm, stg_o, sem_o, j).wait()
                wo_bf[j * rows:(j + 1) * rows, :] = stg_o[j % stg_o.shape[0]].astype(BF16)
                if j + stg_o.shape[0] < n_chunks:
                    _mixer_out_copy(wo_hbm, stg_o, sem_o, j + stg_o.shape[0]).start()

        oc_ref, os_ref, yc_ref, ys_ref = mix_refs
        cat = jnp.concatenate([jnp.where(is_ctx, oc_ref[...], os_ref[...]),
                               jnp.where(is_ctx, yc_ref[...], ys_ref[...])], axis=1)
        x = x + _mod_vec(mod_ref, row, mod_base - 1) * jnp.dot(cat, wo_bf[...], preferred_element_type=F32)

    h = _norm_mod(x, g_ref[...], _mod_vec(mod_ref, row, mod_base),
                  _mod_vec(mod_ref, row, mod_base + 1)).astype(BF16)

    def start_chunk(j):
        for w_hbm, stg, sem, _, ff_axis in streams:
            _ffn_weight_copy(w_hbm, stg, sem, layer, j, ff_axis).start()

    def finish_chunk(j):
        for w_hbm, stg, sem, w_bf, ff_axis in streams:
            _ffn_weight_copy(w_hbm, stg, sem, layer, j, ff_axis).wait()
            w_bf[j] = stg[j % FF_STAGE_SLOTS].astype(BF16)

    def run(stream_weights):
        if stream_weights:
            for j in range(FF_STAGE_SLOTS):
                start_chunk(j)
        acc = None
        for j in range(FF_CHUNKS):
            if stream_weights:
                finish_chunk(j)
                if j + FF_STAGE_SLOTS < FF_CHUNKS:
                    start_chunk(j + FF_STAGE_SLOTS)
            a = jnp.dot(h, wg_bf[j], preferred_element_type=F32)
            b = jnp.dot(h, wu_bf[j], preferred_element_type=F32)
            act = (a * jax.nn.sigmoid(a) * b).astype(BF16)
            y = jnp.dot(act, wd_bf[j], preferred_element_type=F32)
            acc = y if acc is None else acc + y
        res = x + 0.5 * _mod_vec(mod_ref, row, mod_base + 2) * acc
        if split_out:
            @pl.when(is_ctx)
            def _():
                o_refs[0][...] = res

            @pl.when(jnp.logical_not(is_ctx))
            def _():
                o_refs[1][...] = res
        else:
            o_refs[0][...] = res

    @pl.when(i == 0)
    def _():
        run(True)

    @pl.when(i > 0)
    def _():
        run(False)


def _ffn(xs, mod, g, wg, wu, wd, layer, mod_base, split_out=False, mixer_out=None, adaln_next=None):
    tm = FFN_TILE
    n_ctx_tiles = N_CTX_TOK // tm
    split_in = len(xs) == 2

    def tiles(width):
        return (pl.BlockSpec((tm, width), lambda i: (i, 0)),
                pl.BlockSpec((tm, width), lambda i: (jnp.minimum(i, n_ctx_tiles - 1), 0)),
                pl.BlockSpec((tm, width), lambda i: (jnp.maximum(i - n_ctx_tiles, 0), 0)))

    tok, ctx_tok, smp_tok = tiles(D_MODEL)
    full = jax.ShapeDtypeStruct((N_TOK, D_MODEL), F32)
    pair = [jax.ShapeDtypeStruct((N_CTX_TOK, D_MODEL), F32), jax.ShapeDtypeStruct((N_SMP_TOK, D_MODEL), F32)]
    hbm = pl.BlockSpec(memory_space=pl.ANY)
    in_specs = [ctx_tok, smp_tok] if split_in else [tok]
    operands = list(xs)
    scratch = [
        pltpu.VMEM((FF_CHUNKS, D_MODEL, FF_TILE), BF16),
        pltpu.VMEM((FF_CHUNKS, D_MODEL, FF_TILE), BF16),
        pltpu.VMEM((FF_CHUNKS, FF_TILE, D_MODEL), BF16),
        pltpu.VMEM((FF_STAGE_SLOTS, D_MODEL, FF_TILE), F32),
        pltpu.VMEM((FF_STAGE_SLOTS, D_MODEL, FF_TILE), F32),
        pltpu.VMEM((FF_STAGE_SLOTS, FF_TILE, D_MODEL), F32),
        pltpu.SemaphoreType.DMA((FF_STAGE_SLOTS,)),
        pltpu.SemaphoreType.DMA((FF_STAGE_SLOTS,)),
        pltpu.SemaphoreType.DMA((FF_STAGE_SLOTS,)),
    ]
    if mixer_out is not None:
        _, ctx_half, smp_half = tiles(NA_WIDTH)
        in_specs += [ctx_half, smp_half, ctx_half, smp_half]
        operands += list(mixer_out[:4])
    in_specs += [_mod_spec(), pl.BlockSpec((1, D_MODEL), lambda i: (0, 0))]
    operands += [mod, g.reshape(1, D_MODEL)]
    out_specs = [ctx_tok, smp_tok] if split_out else [tok]
    out_shape = pair if split_out else [full]
    if adaln_next is not None:
        ada_in, ada_out = _adaln_specs(adaln_next[4], MOD_WIDTH // (N_TOK // tm))
        in_specs += ada_in
        operands += list(adaln_next[:4])
        out_specs.append(ada_out)
        out_shape.append(jax.ShapeDtypeStruct((MOD_ROWS, MOD_WIDTH), F32))
        scratch.append(pltpu.VMEM((MOD_ROWS, D_MODEL), F32))
    if mixer_out is not None:
        in_specs.insert(len(in_specs) - (4 if adaln_next is not None else 0), hbm)
        operands.insert(len(operands) - (4 if adaln_next is not None else 0), mixer_out[4])
        scratch += [
            pltpu.VMEM((D_MODEL, D_MODEL), BF16),
            pltpu.VMEM((FF_STAGE_SLOTS, FF_TILE, D_MODEL), F32),
            pltpu.SemaphoreType.DMA((FF_STAGE_SLOTS,)),
        ]
    return pl.pallas_call(
        functools.partial(_ffn_kernel, layer=layer, mod_base=mod_base, split_in=split_in,
                          split_out=split_out, mixer_out=mixer_out is not None,
                          adaln_next=adaln_next is not None),
        grid=(N_TOK // tm,),
        in_specs=in_specs + [hbm, hbm, hbm],
        out_specs=out_specs,
        out_shape=out_shape,
        scratch_shapes=scratch,
        compiler_params=_cparams(1),
        name="ffn",
    )(*operands, wg, wu, wd)


def _head_rms_norm(z, g, ones_bd):
    z2 = z * z
    hi = z2.astype(BF16)
    lo = (z2 - hi.astype(F32)).astype(BF16)
    n = ones_bd.shape[0]
    parts = []
    for c in range(z.shape[1] // n):
        sl = slice(c * n, (c + 1) * n)
        parts.append(jnp.dot(hi[:, sl], ones_bd, preferred_element_type=F32)
                     + jnp.dot(lo[:, sl], ones_bd, preferred_element_type=F32))
    ss = jnp.concatenate(parts, axis=1)
    return z * lax.rsqrt(ss * (1.0 / HEAD_DIM) + EPS) * g


def _proj_kernel(x_ref, mod_ref, g_ref, w_ref, qkg_ref, ones_ref,
                 q_ref, k_ref, v_ref, xb_ref, gb_ref, kout_ref, vout_ref, w_bf_ref):
    i = pl.program_id(0)

    @pl.when(i == 0)
    def _():
        w_bf_ref[...] = w_ref[...].astype(BF16)

    x = x_ref[...]
    row = _mod_row_of_tile(i)
    h = _norm_mod(x, g_ref[...], _mod_vec(mod_ref, row, 3), _mod_vec(mod_ref, row, 4)).astype(BF16)

    def proj(part):
        return jnp.dot(h, w_bf_ref[:, part * NA_WIDTH:(part + 1) * NA_WIDTH], preferred_element_type=F32)

    ones_bd = ones_ref[...]
    q = _head_rms_norm(proj(0), qkg_ref[0:1, :], ones_bd) * (HEAD_DIM ** -0.5)
    q_ref[...] = q.astype(BF16)
    k = _head_rms_norm(proj(1), qkg_ref[1:2, :], ones_bd)
    k_ref[...] = k.astype(BF16)
    v = proj(2)
    v_ref[...] = v.astype(BF16)
    xb_ref[...] = proj(3)
    gb_ref[...] = proj(4)

    @pl.when(i < N_CTX_TILES)
    def _():
        kout_ref[...] = k.reshape(TOKEN_TILE, NA_HEADS, HEAD_DIM)
        vout_ref[...] = v.reshape(TOKEN_TILE, NA_HEADS, HEAD_DIM)


def _proj(x, mod, g, w_in, q_g, k_g):
    tm = TOKEN_TILE
    head = np.arange(2 * LANES) // HEAD_DIM
    ones_bd = jnp.asarray((head[:, None] == head[None, :]).astype(np.float32), dtype=BF16)
    tok = lambda i: (i, 0)
    const = lambda i: (0, 0)
    act_f32 = jax.ShapeDtypeStruct((N_TOK, NA_WIDTH), F32)
    act_bf16 = jax.ShapeDtypeStruct((N_TOK, NA_WIDTH), BF16)
    cache = jax.ShapeDtypeStruct((N_CTX_TOK, NA_HEADS, HEAD_DIM), F32)
    cache_spec = pl.BlockSpec((tm, NA_HEADS, HEAD_DIM), lambda i: (jnp.minimum(i, N_CTX_TILES - 1), 0, 0))
    return pl.pallas_call(
        _proj_kernel,
        grid=(N_TOK // tm,),
        in_specs=[
            pl.BlockSpec((tm, D_MODEL), tok),
            _mod_spec(),
            pl.BlockSpec((1, D_MODEL), const),
            _resident((D_MODEL, IN_WIDTH), const),
            pl.BlockSpec((2, NA_WIDTH), const),
            _resident((2 * LANES, 2 * LANES), const),
        ],
        out_specs=[pl.BlockSpec((tm, NA_WIDTH), tok)] * 5 + [cache_spec, cache_spec],
        out_shape=[act_bf16, act_bf16, act_bf16, act_f32, act_f32, cache, cache],
        scratch_shapes=[pltpu.VMEM((D_MODEL, IN_WIDTH), BF16)],
        compiler_params=_cparams(1),
        name="mixer_in_proj",
    )(x, mod, g.reshape(1, D_MODEL), w_in, jnp.tile(jnp.stack([q_g, k_g]), (1, NA_HEADS)), ones_bd)


def _head_masks():
    lane = lax.broadcasted_iota(jnp.int32, (1, 2 * HEAD_DIM), 1)
    return [lane < HEAD_DIM, lane >= HEAD_DIM]


def _ctx_attn_kernel(q_ref, k_ref, v_ref, o_ref):
    masks = _head_masks()
    for b in range(q_ref.shape[0] // SEQ):
        rows = slice(b * SEQ, (b + 1) * SEQ)
        for p in range(NA_HEADS // 2):
            sl = slice(2 * HEAD_DIM * p, 2 * HEAD_DIM * (p + 1))
            q2 = q_ref[rows, sl]
            k2t = k_ref[rows, sl].T
            v2 = v_ref[rows, sl]
            out = None
            for e in range(2):
                qm = jnp.where(masks[e], q2, jnp.zeros_like(q2))
                s = jnp.dot(qm, k2t, preferred_element_type=F32)
                pe = jnp.exp(s - jnp.max(s, axis=-1, keepdims=True))
                den = jnp.sum(pe, axis=-1, keepdims=True)
                o = jnp.dot(pe.astype(BF16), v2, preferred_element_type=F32) / den
                out = o if out is None else jnp.where(masks[e], o, out)
            o_ref[rows, sl] = out.astype(BF16)


def _ctx_attn(q, k, v):
    blk = pl.BlockSpec((TOKEN_TILE, NA_WIDTH), lambda b: (b, 0))
    return pl.pallas_call(
        _ctx_attn_kernel,
        grid=(N_CTX_TOK // TOKEN_TILE,),
        in_specs=[blk, blk, blk],
        out_specs=blk,
        out_shape=jax.ShapeDtypeStruct((N_CTX_TOK, NA_WIDTH), BF16),
        compiler_params=_cparams(1),
        name="ctx_attention",
    )(q, k, v)


def _na_build_bias_table(rpb_ref, table_ref):
    qc = lax.broadcasted_iota(jnp.int32, (GRID_W, LANES), 0)
    lane = lax.broadcasted_iota(jnp.int32, (GRID_W, LANES), 1)
    kc = lane % GRID_W
    col_start = jnp.clip(qc - WIN_W // 2, 0, GRID_W - WIN_W)
    col_in = (kc >= col_start) & (kc < col_start + WIN_W)
    neg = jnp.full((GRID_W, LANES), -jnp.inf, F32)

    def toeplitz(h, dr, lane0):
        if dr < 0 or dr >= N_DR:
            return neg
        w = jnp.broadcast_to(rpb_ref[h, dr:dr + 1, :], (GRID_W, LANES))
        return pltpu.roll(w, (lane0 - (WIN_W - 1)) % LANES, 1, stride=1, stride_axis=0)

    for h in range(NA_HEADS):
        for i in range(N_DR_PAIRS):
            t = jnp.where(lane < GRID_W, toeplitz(h, i - 1, 0), toeplitz(h, i, GRID_W))
            table_ref[h, i] = jnp.where(col_in, t, neg)


def _na_kernel(q_ref, k_ref, v_ref, kc_ref, vc_ref, rpb_ref, o_ref, table_ref):
    b = pl.program_id(0)
    g = pl.program_id(1)

    @pl.when((b == 0) & (g == 0))
    def _():
        _na_build_bias_table(rpb_ref, table_ref)

    win_row0 = jnp.where(g < NA_GROUPS // 2, 0, ROWS - NA_K_ROWS)
    start = pl.multiple_of(win_row0 * GRID_W, GRID_W)
    q_row = g * NA_Q_ROWS + lax.broadcasted_iota(jnp.int32, (NA_Q, 1), 0) // GRID_W
    k_row = win_row0 + lax.broadcasted_iota(jnp.int32, (1, NA_K), 1) // GRID_W
    row_start = jnp.clip(q_row - KH // 2, 0, ROWS - KH)
    row_in = (k_row >= row_start) & (k_row < row_start + KH)
    masks = _head_masks()
    for p in range(NA_HEADS // 2):
        sl = slice(2 * HEAD_DIM * p, 2 * HEAD_DIM * (p + 1))
        q2 = q_ref[:, sl]
        klt = k_ref[pl.ds(start, NA_K), sl].T
        vl = v_ref[pl.ds(start, NA_K), sl]
        kct = kc_ref[0, :, sl].astype(BF16).T
        vc = vc_ref[0, :, sl].astype(BF16)
        out = None
        for e in range(2):
            head = 2 * p + e
            bias_rows = []
            for a in range(NA_Q_ROWS):
                tiles = []
                for m in range(NA_K_ROWS // 2):
                    dr = win_row0 + 2 * m - (g * NA_Q_ROWS + a) + (WIN_H - 1)
                    tiles.append(table_ref[head, jnp.clip(dr + 1, 0, N_DR_PAIRS - 1)])
                bias_rows.append(jnp.concatenate(tiles, axis=1))
            bias = jnp.concatenate(bias_rows, axis=0)
            qm = jnp.where(masks[e], q2, jnp.zeros_like(q2))
            s_loc = jnp.where(row_in, jnp.dot(qm, klt, preferred_element_type=F32) + bias, -jnp.inf)
            s_ctx = jnp.dot(qm, kct, preferred_element_type=F32)
            m_max = jnp.maximum(jnp.max(s_loc, axis=-1, keepdims=True),
                                jnp.max(s_ctx, axis=-1, keepdims=True))
            p_loc = jnp.exp(s_loc - m_max)
            p_ctx = jnp.exp(s_ctx - m_max)
            den = jnp.sum(p_loc, axis=-1, keepdims=True) + jnp.sum(p_ctx, axis=-1, keepdims=True)
            o = (jnp.dot(p_loc.astype(BF16), vl, preferred_element_type=F32)
                 + jnp.dot(p_ctx.astype(BF16), vc, preferred_element_type=F32)) / den
            out = o if out is None else jnp.where(masks[e], o, out)
        o_ref[:, sl] = out.astype(BF16)


def _na_attn(q, k, v, k_ctx, v_ctx, rpb_e):
    smp_blk0 = N_CTX_TOK // DEC_SEQ
    q_blk0 = N_CTX_TOK // NA_Q
    kv = pl.BlockSpec((DEC_SEQ, NA_WIDTH), lambda b, g: (smp_blk0 + b, 0))
    ctx = pl.BlockSpec((1, PAST_LEN, NA_WIDTH), lambda b, g: (b, 0, 0))
    rpb_pad = jnp.pad(rpb_e.astype(F32), ((0, 0), (0, 0), (0, LANES - N_DC)))
    return pl.pallas_call(
        _na_kernel,
        grid=(DEC_BATCH, NA_GROUPS),
        in_specs=[
            pl.BlockSpec((NA_Q, NA_WIDTH), lambda b, g: (q_blk0 + b * NA_GROUPS + g, 0)),
            kv, kv, ctx, ctx,
            pl.BlockSpec((NA_HEADS, N_DR, LANES), lambda b, g: (0, 0, 0)),
        ],
        out_specs=pl.BlockSpec((NA_Q, NA_WIDTH), lambda b, g: (b * NA_GROUPS + g, 0)),
        out_shape=jax.ShapeDtypeStruct((N_SMP_TOK, NA_WIDTH), BF16),
        scratch_shapes=[pltpu.VMEM((NA_HEADS, N_DR_PAIRS, GRID_W, LANES), F32)],
        compiler_params=_cparams(2),
        name="neighbourhood_attention",
    )(q, k, v, k_ctx, v_ctx, rpb_pad)


def _sigmoid(x):
    return 0.5 * jnp.tanh(0.5 * x) + 0.5


def _lru_kernel(*refs, zero_state, t_len):
    refs = list(refs)
    xb_ref, gb_ref, cw_ref, cb_ref, w_ref, b_ref, lam_ref = refs[:7]
    h0f_ref, h0b_ref = (None, None) if zero_state else refs[7:9]
    y_ref, hlf_ref, hlb_ref, af_ref, uf_ref, ab_ref, ub_ref = refs[-7:]
    width = xb_ref.shape[1]
    n_seq = xb_ref.shape[0] // t_len
    n_blk = t_len // SUBLANES
    row = lax.broadcasted_iota(jnp.int32, (t_len, 1), 0)
    in_block = lax.broadcasted_iota(jnp.int32, (1, SUBLANES, 1), 1)

    def shifted(z, s):
        rolled = pltpu.roll(z, (-s) % t_len, axis=0)
        ok = (row + s >= 0) & (row + s < t_len)
        return jnp.where(ok, rolled, 0.0)

    left = (CONV_W - 1) // 2
    for q, c in [(q, c) for q in range(n_seq) for c in range(width // LRU_SUB)]:
        rows = slice(q * t_len, (q + 1) * t_len)
        cs = slice(c * LRU_SUB, (c + 1) * LRU_SUB)
        x = xb_ref[rows, cs]
        xc = cb_ref[:, cs]
        for j in range(CONV_W):
            tap = x if j == left else shifted(x, j - left)
            xc = xc + tap * cw_ref[j:j + 1, cs]
        gates = jnp.dot(xc.astype(BF16), w_ref[c], preferred_element_type=F32) + b_ref[c]
        for d, (a_ref, u_ref) in enumerate(((af_ref, uf_ref), (ab_ref, ub_ref))):
            r_gate = _sigmoid(gates[:, (2 * d) * LRU_SUB:(2 * d + 1) * LRU_SUB])
            i_gate = _sigmoid(gates[:, (2 * d + 1) * LRU_SUB:(2 * d + 2) * LRU_SUB])
            lam = lam_ref[0, d:d + 1, cs]
            log_sig = jnp.minimum(lam, 0.0) - jnp.log1p(jnp.exp(-jnp.abs(lam)))
            log_a = LRU_C * r_gate * log_sig
            a = jnp.exp(log_a)
            var = -jnp.tanh(log_a) * (a * a + 1.0)
            u = jnp.where(var > 0.0, var * lax.rsqrt(var), 0.0) * (i_gate * xc)
            a = a.reshape(n_blk, SUBLANES, LRU_SUB)
            u = u.reshape(n_blk, SUBLANES, LRU_SUB)
            step = 1
            while step < SUBLANES:
                if d == 0:
                    ok, shift = in_block >= step, step
                else:
                    ok, shift = in_block < SUBLANES - step, SUBLANES - step
                a_prev = jnp.where(ok, pltpu.roll(a, shift, axis=1), 1.0)
                u_prev = jnp.where(ok, pltpu.roll(u, shift, axis=1), 0.0)
                u = u + a * u_prev
                a = a * a_prev
                step *= 2
            a_ref[rows, cs] = a.reshape(t_len, LRU_SUB)
            u_ref[rows, cs] = u.reshape(t_len, LRU_SUB)

    def body(i, carry):
        new = []
        for q, (cf, cb) in enumerate(carry):
            f0 = pl.multiple_of(q * t_len + i * SUBLANES, SUBLANES)
            b0 = pl.multiple_of(q * t_len + (n_blk - 1 - i) * SUBLANES, SUBLANES)
            hf = uf_ref[pl.ds(f0, SUBLANES), :] + af_ref[pl.ds(f0, SUBLANES), :] * cf
            hb = ub_ref[pl.ds(b0, SUBLANES), :] + ab_ref[pl.ds(b0, SUBLANES), :] * cb
            uf_ref[pl.ds(f0, SUBLANES), :] = hf
            ub_ref[pl.ds(b0, SUBLANES), :] = hb
            new.append((jnp.broadcast_to(hf[SUBLANES - 1:SUBLANES, :], (SUBLANES, width)),
                        jnp.broadcast_to(hb[0:1, :], (SUBLANES, width))))
        return tuple(new)

    if zero_state:
        init = tuple((jnp.zeros((SUBLANES, width), F32),) * 2 for _ in range(n_seq))
    else:
        init = tuple((jnp.broadcast_to(h0f_ref[q], (SUBLANES, width)),
                      jnp.broadcast_to(h0b_ref[q], (SUBLANES, width))) for q in range(n_seq))
    last = lax.fori_loop(0, n_blk, body, init)
    for q, (cf, cb) in enumerate(last):
        hlf_ref[q] = cf[0:1, :]
        hlb_ref[q] = cb[0:1, :]
    y_ref[...] = ((uf_ref[...] + ub_ref[...]) * jax.nn.gelu(gb_ref[...])).astype(BF16)


def _lru(xb, gb, conv_w, conv_b, w_bd, b_bd, lam, h0, n_seq, t_len, tok_blk0, width, seq_per_step):
    n_sub = width // LRU_SUB
    rows = seq_per_step * t_len
    seq = lambda s, c: (tok_blk0 + s, c)
    state = pl.BlockSpec((seq_per_step, 1, width), lambda s, c: (s, 0, c))
    state_shape = jax.ShapeDtypeStruct((n_seq, 1, LRU_WIDTH), F32)
    return pl.pallas_call(
        functools.partial(_lru_kernel, zero_state=h0 is None, t_len=t_len),
        grid=(n_seq // seq_per_step, LRU_WIDTH // width),
        in_specs=[
            pl.BlockSpec((rows, width), seq),
            pl.BlockSpec((rows, width), seq),
            pl.BlockSpec((CONV_W, width), lambda s, c: (0, c)),
            pl.BlockSpec((1, width), lambda s, c: (0, c)),
            pl.BlockSpec((n_sub, LRU_SUB, 4 * LRU_SUB), lambda s, c: (c, 0, 0)),
            pl.BlockSpec((n_sub, 1, 4 * LRU_SUB), lambda s, c: (c, 0, 0)),
            pl.BlockSpec((1, 2, width), lambda s, c: (0, 0, c)),
        ] + ([] if h0 is None else [state, state]),
        out_specs=[pl.BlockSpec((rows, width), lambda s, c: (s, c)), state, state],
        out_shape=[jax.ShapeDtypeStruct((n_seq * t_len, LRU_WIDTH), BF16), state_shape, state_shape],
        scratch_shapes=[pltpu.VMEM((rows, width), F32)] * 4,
        compiler_params=_cparams(2),
        name="rglru",
    )(xb, gb, conv_w, conv_b.reshape(1, LRU_WIDTH), w_bd, b_bd, lam.reshape(1, 2, LRU_WIDTH),
      *(() if h0 is None else h0))


def _lru_gate_weights(w_r, b_r, w_i, b_i):
    n_grp = LRU_WIDTH // LRU_SUB
    bpg = LRU_SUB // LRU_BLOCK
    w_all = jnp.stack([w_r[0], w_i[0], w_r[1], w_i[1]])
    w_all = w_all.reshape(4, n_grp, bpg, LRU_BLOCK, LRU_BLOCK)
    eye = jnp.eye(bpg, dtype=w_all.dtype)
    w_bd = jnp.einsum('tcnij,nm->cnitmj', w_all, eye).reshape(n_grp, LRU_SUB, 4 * LRU_SUB)
    b_all = jnp.stack([b_r[0], b_i[0], b_r[1], b_i[1]]).reshape(4, n_grp, LRU_SUB)
    b_bd = jnp.transpose(b_all, (1, 0, 2)).reshape(n_grp, 1, 4 * LRU_SUB)
    return w_bd.astype(BF16), b_bd


def _fourier_kernel(x_ref, mod_ref, g_ref, cs_ref, ct_ref, w_ref, o_ref, w_bf_ref, *, mod_row0, t_len):
    @pl.when(pl.program_id(0) == 0)
    def _():
        w_bf_ref[...] = w_ref[...].astype(BF16)

    x = x_ref[...]
    row = mod_row0 + pl.program_id(0) if mod_row0 else 0
    h = _norm_mod(x, g_ref[...], _mod_vec(mod_ref, row, 3), _mod_vec(mod_ref, row, 4)).astype(BF16)
    cos_parts, sin_parts = [], []
    for g in range(FOURIER_GROUPS):
        ab = jnp.dot(h[:, g * GROUP_W:(g + 1) * GROUP_W], cs_ref[...], preferred_element_type=F32)
        cos_parts.append(ab[:, :GROUP_W])
        sin_parts.append(ab[:, GROUP_W:])
    cos_all = jnp.concatenate(cos_parts, axis=1).astype(BF16)
    sin_all = jnp.concatenate(sin_parts, axis=1).astype(BF16)
    f_parts = []
    for q in range(x.shape[0] // t_len):
        rows = slice(q * t_len, (q + 1) * t_len)
        stacked = jnp.concatenate([cos_all[rows], sin_all[rows]], axis=0)
        f_parts.append(jnp.dot(ct_ref[...], stacked, preferred_element_type=F32))
    f = jnp.concatenate(f_parts, axis=0) * ((t_len * GROUP_W) ** -0.5)
    y = jnp.dot(f.astype(BF16), w_bf_ref[...], preferred_element_type=F32)
    o_ref[...] = x + _mod_vec(mod_ref, row, 5) * y


def _dft_tables(t_len):
    def cos_sin(n):
        jk = np.outer(np.arange(n), np.arange(n)) % n
        ang = 2.0 * np.pi * jk.astype(np.float64) / n
        return np.cos(ang), np.sin(ang)

    cc, sc = cos_sin(GROUP_W)
    ct, st = cos_sin(t_len)
    chan = jnp.asarray(np.concatenate([cc, sc], axis=1).astype(np.float32)).astype(BF16)
    time = jnp.asarray(np.concatenate([ct, -st], axis=1).astype(np.float32)).astype(BF16)
    return chan, time


def _fourier(x, mod, g, w_out, n_seq, t_len, tok_blk0, mod_row0, seq_per_step):
    assert seq_per_step == 1 or mod_row0 == 0
    chan, time = _dft_tables(t_len)
    rows = seq_per_step * t_len
    seq = lambda s: (tok_blk0 + s, 0)
    const = lambda s: (0, 0)
    return pl.pallas_call(
        functools.partial(_fourier_kernel, mod_row0=mod_row0, t_len=t_len),
        grid=(n_seq // seq_per_step,),
        in_specs=[
            pl.BlockSpec((rows, D_MODEL), seq),
            _mod_spec(),
            pl.BlockSpec((1, D_MODEL), const),
            _resident((GROUP_W, 2 * GROUP_W), const),
            _resident((t_len, 2 * t_len), const),
            _resident((D_MODEL, D_MODEL), const),
        ],
        out_specs=pl.BlockSpec((rows, D_MODEL), seq),
        out_shape=jax.ShapeDtypeStruct((N_TOK, D_MODEL), F32),
        input_output_aliases={0: 0},
        scratch_shapes=[pltpu.VMEM((D_MODEL, D_MODEL), BF16)],
        compiler_params=_cparams(1),
        name="fourier_mixer",
    )(x, mod, g.reshape(1, D_MODEL), chan, time, w_out)


def kernel(x_prompt, x_sample, cache_k, cache_v, state_lru_fwd, state_lru_bwd, c, c_ctx, w_ada, b_ada, norm_g, ffn1_gate, ffn1_up, ffn1_down, ffn2_gate, ffn2_up, ffn2_down, w_in, q_norm_g, k_norm_g, rpb, conv_w, conv_b, lru_w_r, lru_b_r, lru_w_i, lru_b_i, lru_lambda, w_out_ab, w_out_c):
    assert DEPTH == 2, "one neighbourhood/RG-LRU layer followed by one Fourier layer"
    c_ctx2 = c_ctx.reshape(1, D_MODEL)
    b_ada3 = b_ada.reshape(DEPTH, 1, MOD_WIDTH)
    mod0 = _adaln(c_ctx2, c, w_ada, b_ada3, 0)

    ffn1 = (ffn1_gate, ffn1_up, ffn1_down)
    ffn2 = (ffn2_gate, ffn2_up, ffn2_down)

    x, mod1 = _ffn((x_prompt.reshape(N_CTX_TOK, D_MODEL), x_sample.reshape(N_SMP_TOK, D_MODEL)),
                   mod0, norm_g[0, 0], *ffn1, 0, 0, adaln_next=(c_ctx2, c, w_ada, b_ada3, 1))
    q, k, v, xb, gb, new_k, new_v = _proj(x, mod0, norm_g[0, 1], w_in[0], q_norm_g[0], k_norm_g[0])
    o_ctx = _ctx_attn(q, k, v)
    o_smp = _na_attn(q, k, v,
                     cache_k[:, 0].reshape(DEC_BATCH, PAST_LEN, NA_WIDTH),
                     cache_v[:, 0].reshape(DEC_BATCH, PAST_LEN, NA_WIDTH), rpb[0])
    w_bd, b_bd = _lru_gate_weights(lru_w_r[0], lru_b_r[0], lru_w_i[0], lru_b_i[0])
    yb_ctx, new_hf, new_hb = _lru(xb, gb, conv_w[0], conv_b[0], w_bd, b_bd, lru_lambda[0],
                                  None, BATCH, SEQ, 0, LRU_WIDTH, TOKEN_TILE // SEQ)
    yb_smp, _, _ = _lru(xb, gb, conv_w[0], conv_b[0], w_bd, b_bd, lru_lambda[0],
                        (state_lru_fwd, state_lru_bwd), DEC_BATCH, DEC_SEQ, N_CTX_TOK // DEC_SEQ, LRU_WIDTH, 1)
    (x,) = _ffn((x,), mod0, norm_g[0, 2], *ffn2, 0, 6, mixer_out=(o_ctx, o_smp, yb_ctx, yb_smp, w_out_ab))

    (x,) = _ffn((x,), mod1, norm_g[1, 0], *ffn1, 1, 0)
    x = _fourier(x, mod1, norm_g[1, 1], w_out_c[0], BATCH, SEQ, 0, 0, TOKEN_TILE // SEQ)
    x = _fourier(x, mod1, norm_g[1, 1], w_out_c[0], DEC_BATCH, DEC_SEQ, N_CTX_TOK // DEC_SEQ, 1, 1)
    y_prompt, y_sample = _ffn((x,), mod1, norm_g[1, 2], *ffn2, 1, 6, split_out=True)

    return (y_prompt.reshape(BATCH, SEQ, D_MODEL), y_sample.reshape(DEC_BATCH, DEC_SEQ, D_MODEL),
            new_k.reshape(BATCH, 1, SEQ, NA_HEADS, HEAD_DIM), new_v.reshape(BATCH, 1, SEQ, NA_HEADS, HEAD_DIM),
            new_hf, new_hb)
```

```python
import functools

import numpy as np
import jax
import jax.numpy as jnp
from jax import lax
from jax.experimental import pallas as pl
from jax.experimental.pallas import tpu as pltpu

F32 = jnp.float32
BF16 = jnp.bfloat16

D_MODEL = 1024
BATCH = 16
SEQ = 256
DEPTH = 2
DEC_BATCH = 2
DEC_SEQ = 1024
PAST_LEN = 256
GRID_W = 64
HEAD_DIM = 64
NA_WIDTH = 512
NA_HEADS = 8
WIN_H = 8
WIN_W = 16
LRU_WIDTH = 512
LRU_BLOCKS = 8
LRU_BLOCK = 64
LRU_C = 8.0
LRU_SUB = 256
CONV_W = 4
FOURIER_GROUPS = 4
GROUP_W = D_MODEL // FOURIER_GROUPS
D_FF = 2816
N_MOD = 9
IN_WIDTH = 3 * NA_WIDTH + 2 * LRU_WIDTH
EPS = 1e-6

N_CTX_TOK = BATCH * SEQ
N_SMP_TOK = DEC_BATCH * DEC_SEQ
N_TOK = N_CTX_TOK + N_SMP_TOK
MOD_ROWS = 8
MOD_WIDTH = N_MOD * D_MODEL
ROWS = DEC_SEQ // GRID_W
KH = min(WIN_H, ROWS)

TOKEN_TILE = 512
N_CTX_TILES = N_CTX_TOK // TOKEN_TILE
FFN_TILE = 512
FF_TILE = 256
FF_CHUNKS = D_FF // FF_TILE
FF_STAGE_SLOTS = 2
SUBLANES = 8
LANES = 128
VMEM_LIMIT = 56 * 1024 * 1024

NA_Q_ROWS = 4
NA_GROUPS = ROWS // NA_Q_ROWS
NA_K_ROWS = 12
NA_Q = NA_Q_ROWS * GRID_W
NA_K = NA_K_ROWS * GRID_W
N_DR = 2 * WIN_H - 1
N_DC = 2 * WIN_W - 1
N_DR_PAIRS = N_DR + 1


def _cparams(n_axes):
    return pltpu.CompilerParams(
        dimension_semantics=("arbitrary",) * n_axes, vmem_limit_bytes=VMEM_LIMIT)


def _resident(block_shape, index_map):
    return pl.BlockSpec(block_shape, index_map, pipeline_mode=pl.Buffered(1))


def _mod_spec():
    return _resident((MOD_ROWS, MOD_WIDTH), lambda i: (0, 0))


def _mod_row_of_tile(i, tile=TOKEN_TILE):
    n_ctx_tiles = N_CTX_TOK // tile
    tiles_per_seq = DEC_SEQ // tile
    return jnp.where(i < n_ctx_tiles, 0, 1 + (i - n_ctx_tiles) // tiles_per_seq)


def _mod_vec(mod_ref, row, k):
    return mod_ref[pl.ds(row, 1), k * D_MODEL:(k + 1) * D_MODEL]


def _norm_mod(x, g, shift, scale):
    ms = jnp.mean(x * x, axis=-1, keepdims=True)
    y = x * lax.rsqrt(ms + EPS) * g
    return y * (1.0 + scale) + shift


def _adaln_slab(cctx_ref, c_ref, w_ref, b_ref, cond_ref):
    cond_ref[...] = jnp.zeros_like(cond_ref)
    cond_ref[0:1, :] = cctx_ref[...]
    cond_ref[1:1 + DEC_BATCH, :] = c_ref[...]
    cond = cond_ref[...]
    s = (cond * jax.nn.sigmoid(cond)).astype(BF16)
    return jnp.dot(s, w_ref[...].astype(BF16), preferred_element_type=F32) + b_ref[...]


def _adaln_specs(layer, slab):
    return ([pl.BlockSpec((1, D_MODEL), lambda i: (0, 0)),
             pl.BlockSpec((DEC_BATCH, D_MODEL), lambda i: (0, 0)),
             pl.BlockSpec((None, D_MODEL, slab), lambda i: (layer, 0, i)),
             pl.BlockSpec((None, 1, slab), lambda i: (layer, 0, i))],
            pl.BlockSpec((MOD_ROWS, slab), lambda i: (0, i)))


def _adaln_kernel(cctx_ref, c_ref, w_ref, b_ref, o_ref, cond_ref):
    o_ref[...] = _adaln_slab(cctx_ref, c_ref, w_ref, b_ref, cond_ref)


def _adaln(c_ctx, c, w_ada, b_ada, layer):
    slab = MOD_WIDTH // 4
    in_specs, out_spec = _adaln_specs(layer, slab)
    return pl.pallas_call(
        _adaln_kernel,
        grid=(MOD_WIDTH // slab,),
        in_specs=in_specs,
        out_specs=out_spec,
        out_shape=jax.ShapeDtypeStruct((MOD_ROWS, MOD_WIDTH), F32),
        scratch_shapes=[pltpu.VMEM((MOD_ROWS, D_MODEL), F32)],
        compiler_params=_cparams(1),
        name="adaln",
    )(c_ctx, c, w_ada, b_ada)


def _ffn_weight_copy(w_hbm, stage_ref, sem_ref, layer, j, ff_axis):
    ff = pl.ds(j * FF_TILE, FF_TILE)
    src = w_hbm.at[layer, :, ff] if ff_axis == 1 else w_hbm.at[layer, ff, :]
    slot = j % FF_STAGE_SLOTS
    return pltpu.make_async_copy(src, stage_ref.at[slot], sem_ref.at[slot])


def _mixer_out_copy(w_hbm, stage_ref, sem_ref, j):
    rows = stage_ref.shape[1]
    slot = j % stage_ref.shape[0]
    return pltpu.make_async_copy(w_hbm.at[0, pl.ds(j * rows, rows), :], stage_ref.at[slot], sem_ref.at[slot])


def _ffn_kernel(*refs, layer, mod_base, split_in, split_out, mixer_out, adaln_next):
    refs = list(refs)
    take = lambda n: [refs.pop(0) for _ in range(n)]
    x_refs = take(2 if split_in else 1)
    mix_refs = take(4) if mixer_out else None
    mod_ref, g_ref = take(2)
    wo_hbm = take(1)[0] if mixer_out else None
    ada_refs = take(4) if adaln_next else None
    wg_hbm, wu_hbm, wd_hbm = take(3)
    o_refs = take(2 if split_out else 1)
    modn_ref = take(1)[0] if adaln_next else None
    wg_bf, wu_bf, wd_bf, stg_g, stg_u, stg_d, sem_g, sem_u, sem_d = take(9)
    if adaln_next:
        modn_ref[...] = _adaln_slab(*ada_refs, take(1)[0])
    streams = ((wg_hbm, stg_g, sem_g, wg_bf, 1), (wu_hbm, stg_u, sem_u, wu_bf, 1),
               (wd_hbm, stg_d, sem_d, wd_bf, 0))

    i = pl.program_id(0)
    is_ctx = i < N_CTX_TOK // FFN_TILE
    if split_in:
        x = jnp.where(is_ctx, x_refs[0][...], x_refs[1][...])
    else:
        x = x_refs[0][...]
    row = _mod_row_of_tile(i, FFN_TILE)

    if mixer_out:
        wo_bf, stg_o, sem_o = take(3)
        rows = stg_o.shape[1]
        n_chunks = D_MODEL // rows

        @pl.when(i == 0)
        def _():
            for j in range(stg_o.shape[0]):
                _mixer_out_copy(wo_hbm, stg_o, sem_o, j).start()
            for j in range(n_chunks):
                _mixer_out_copy(wo_hbm, stg_o, sem_o, j).wait()
                wo_bf[j * rows:(j + 1) * rows, :] = stg_o[j % stg_o.shape[0]].astype(BF16)
                if j + stg_o.shape[0] < n_chunks:
                    _mixer_out_copy(wo_hbm, stg_o, sem_o, j + stg_o.shape[0]).start()

        oc_ref, os_ref, yc_ref, ys_ref = mix_refs
        cat = jnp.concatenate([jnp.where(is_ctx, oc_ref[...], os_ref[...]),
                               jnp.where(is_ctx, yc_ref[...], ys_ref[...])], axis=1)
        x = x + _mod_vec(mod_ref, row, mod_base - 1) * jnp.dot(cat, wo_bf[...], preferred_element_type=F32)

    h = _norm_mod(x, g_ref[...], _mod_vec(mod_ref, row, mod_base),
                  _mod_vec(mod_ref, row, mod_base + 1)).astype(BF16)

    def start_chunk(j):
        for w_hbm, stg, sem, _, ff_axis in streams:
            _ffn_weight_copy(w_hbm, stg, sem, layer, j, ff_axis).start()

    def finish_chunk(j):
        for w_hbm, stg, sem, w_bf, ff_axis in streams:
            _ffn_weight_copy(w_hbm, stg, sem, layer, j, ff_axis).wait()
            w_bf[j] = stg[j % FF_STAGE_SLOTS].astype(BF16)

    def run(stream_weights):
        if stream_weights:
            for j in range(FF_STAGE_SLOTS):
                start_chunk(j)
        acc = None
        for j in range(FF_CHUNKS):
            if stream_weights:
                finish_chunk(j)
                if j + FF_STAGE_SLOTS < FF_CHUNKS:
                    start_chunk(j + FF_STAGE_SLOTS)
            a = jnp.dot(h, wg_bf[j], preferred_element_type=F32)
            b = jnp.dot(h, wu_bf[j], preferred_element_type=F32)
            act = (a * jax.nn.sigmoid(a) * b).astype(BF16)
            y = jnp.dot(act, wd_bf[j], preferred_element_type=F32)
            acc = y if acc is None else acc + y
        res = x + 0.5 * _mod_vec(mod_ref, row, mod_base + 2) * acc
        if split_out:
            @pl.when(is_ctx)
            def _():
                o_refs[0][...] = res

            @pl.when(jnp.logical_not(is_ctx))
            def _():
                o_refs[1][...] = res
        else:
            o_refs[0][...] = res

    @pl.when(i == 0)
    def _():
        run(True)

    @pl.when(i > 0)
    def _():
        run(False)


def _ffn(xs, mod, g, wg, wu, wd, layer, mod_base, split_out=False, mixer_out=None, adaln_next=None):
    tm = FFN_TILE
    n_ctx_tiles = N_CTX_TOK // tm
    split_in = len(xs) == 2

    def tiles(width):
        return (pl.BlockSpec((tm, width), lambda i: (i, 0)),
                pl.BlockSpec((tm, width), lambda i: (jnp.minimum(i, n_ctx_tiles - 1), 0)),
                pl.BlockSpec((tm, width), lambda i: (jnp.maximum(i - n_ctx_tiles, 0), 0)))

    tok, ctx_tok, smp_tok = tiles(D_MODEL)
    full = jax.ShapeDtypeStruct((N_TOK, D_MODEL), F32)
    pair = [jax.ShapeDtypeStruct((N_CTX_TOK, D_MODEL), F32), jax.ShapeDtypeStruct((N_SMP_TOK, D_MODEL), F32)]
    hbm = pl.BlockSpec(memory_space=pl.ANY)
    in_specs = [ctx_tok, smp_tok] if split_in else [tok]
    operands = list(xs)
    scratch = [
        pltpu.VMEM((FF_CHUNKS, D_MODEL, FF_TILE), BF16),
        pltpu.VMEM((FF_CHUNKS, D_MODEL, FF_TILE), BF16),
        pltpu.VMEM((FF_CHUNKS, FF_TILE, D_MODEL), BF16),
        pltpu.VMEM((FF_STAGE_SLOTS, D_MODEL, FF_TILE), F32),
        pltpu.VMEM((FF_STAGE_SLOTS, D_MODEL, FF_TILE), F32),
        pltpu.VMEM((FF_STAGE_SLOTS, FF_TILE, D_MODEL), F32),
        pltpu.SemaphoreType.DMA((FF_STAGE_SLOTS,)),
        pltpu.SemaphoreType.DMA((FF_STAGE_SLOTS,)),
        pltpu.SemaphoreType.DMA((FF_STAGE_SLOTS,)),
    ]
    if mixer_out is not None:
        _, ctx_half, smp_half = tiles(NA_WIDTH)
        in_specs += [ctx_half, smp_half, ctx_half, smp_half]
        operands += list(mixer_out[:4])
    in_specs += [_mod_spec(), pl.BlockSpec((1, D_MODEL), lambda i: (0, 0))]
    operands += [mod, g.reshape(1, D_MODEL)]
    out_specs = [ctx_tok, smp_tok] if split_out else [tok]
    out_shape = pair if split_out else [full]
    if adaln_next is not None:
        ada_in, ada_out = _adaln_specs(adaln_next[4], MOD_WIDTH // (N_TOK // tm))
        in_specs += ada_in
        operands += list(adaln_next[:4])
        out_specs.append(ada_out)
        out_shape.append(jax.ShapeDtypeStruct((MOD_ROWS, MOD_WIDTH), F32))
        scratch.append(pltpu.VMEM((MOD_ROWS, D_MODEL), F32))
    if mixer_out is not None:
        in_specs.insert(len(in_specs) - (4 if adaln_next is not None else 0), hbm)
        operands.insert(len(operands) - (4 if adaln_next is not None else 0), mixer_out[4])
        scratch += [
            pltpu.VMEM((D_MODEL, D_MODEL), BF16),
            pltpu.VMEM((FF_STAGE_SLOTS, FF_TILE, D_MODEL), F32),
            pltpu.SemaphoreType.DMA((FF_STAGE_SLOTS,)),
        ]
    return pl.pallas_call(
        functools.partial(_ffn_kernel, layer=layer, mod_base=mod_base, split_in=split_in,
                          split_out=split_out, mixer_out=mixer_out is not None,
                          adaln_next=adaln_next is not None),
        grid=(N_TOK // tm,),
        in_specs=in_specs + [hbm, hbm, hbm],
        out_specs=out_specs,
        out_shape=out_shape,
        scratch_shapes=scratch,
        compiler_params=_cparams(1),
        name="ffn",
    )(*operands, wg, wu, wd)


def _head_rms_norm(z, g, ones_bd):
    z2 = z * z
    hi = z2.astype(BF16)
    lo = (z2 - hi.astype(F32)).astype(BF16)
    n = ones_bd.shape[0]
    parts = []
    for c in range(z.shape[1] // n):
        sl = slice(c * n, (c + 1) * n)
        parts.append(jnp.dot(hi[:, sl], ones_bd, preferred_element_type=F32)
                     + jnp.dot(lo[:, sl], ones_bd, preferred_element_type=F32))
    ss = jnp.concatenate(parts, axis=1)
    return z * lax.rsqrt(ss * (1.0 / HEAD_DIM) + EPS) * g


def _proj_kernel(x_ref, mod_ref, g_ref, w_ref, qkg_ref, ones_ref,
                 q_ref, k_ref, v_ref, xb_ref, gb_ref, kout_ref, vout_ref, w_bf_ref):
    i = pl.program_id(0)

    @pl.when(i == 0)
    def _():
        w_bf_ref[...] = w_ref[...].astype(BF16)

    x = x_ref[...]
    row = _mod_row_of_tile(i)
    h = _norm_mod(x, g_ref[...], _mod_vec(mod_ref, row, 3), _mod_vec(mod_ref, row, 4)).astype(BF16)

    def proj(part):
        return jnp.dot(h, w_bf_ref[:, part * NA_WIDTH:(part + 1) * NA_WIDTH], preferred_element_type=F32)

    ones_bd = ones_ref[...]
    q = _head_rms_norm(proj(0), qkg_ref[0:1, :], ones_bd) * (HEAD_DIM ** -0.5)
    q_ref[...] = q.astype(BF16)
    k = _head_rms_norm(proj(1), qkg_ref[1:2, :], ones_bd)
    k_ref[...] = k.astype(BF16)
    v = proj(2)
    v_ref[...] = v.astype(BF16)
    xb_ref[...] = proj(3)
    gb_ref[...] = proj(4)

    @pl.when(i < N_CTX_TILES)
    def _():
        kout_ref[...] = k.reshape(TOKEN_TILE, NA_HEADS, HEAD_DIM)
        vout_ref[...] = v.reshape(TOKEN_TILE, NA_HEADS, HEAD_DIM)


def _proj(x, mod, g, w_in, q_g, k_g):
    tm = TOKEN_TILE
    head = np.arange(2 * LANES) // HEAD_DIM
    ones_bd = jnp.asarray((head[:, None] == head[None, :]).astype(np.float32), dtype=BF16)
    tok = lambda i: (i, 0)
    const = lambda i: (0, 0)
    act_f32 = jax.ShapeDtypeStruct((N_TOK, NA_WIDTH), F32)
    act_bf16 = jax.ShapeDtypeStruct((N_TOK, NA_WIDTH), BF16)
    cache = jax.ShapeDtypeStruct((N_CTX_TOK, NA_HEADS, HEAD_DIM), F32)
    cache_spec = pl.BlockSpec((tm, NA_HEADS, HEAD_DIM), lambda i: (jnp.minimum(i, N_CTX_TILES - 1), 0, 0))
    return pl.pallas_call(
        _proj_kernel,
        grid=(N_TOK // tm,),
        in_specs=[
            pl.BlockSpec((tm, D_MODEL), tok),
            _mod_spec(),
            pl.BlockSpec((1, D_MODEL), const),
            _resident((D_MODEL, IN_WIDTH), const),
            pl.BlockSpec((2, NA_WIDTH), const),
            _resident((2 * LANES, 2 * LANES), const),
        ],
        out_specs=[pl.BlockSpec((tm, NA_WIDTH), tok)] * 5 + [cache_spec, cache_spec],
        out_shape=[act_bf16, act_bf16, act_bf16, act_f32, act_f32, cache, cache],
        scratch_shapes=[pltpu.VMEM((D_MODEL, IN_WIDTH), BF16)],
        compiler_params=_cparams(1),
        name="mixer_in_proj",
    )(x, mod, g.reshape(1, D_MODEL), w_in, jnp.tile(jnp.stack([q_g, k_g]), (1, NA_HEADS)), ones_bd)


def _head_masks():
    lane = lax.broadcasted_iota(jnp.int32, (1, 2 * HEAD_DIM), 1)
    return [lane < HEAD_DIM, lane >= HEAD_DIM]


def _ctx_attn_kernel(q_ref, k_ref, v_ref, o_ref):
    masks = _head_masks()
    for b in range(q_ref.shape[0] // SEQ):
        rows = slice(b * SEQ, (b + 1) * SEQ)
        for p in range(NA_HEADS // 2):
            sl = slice(2 * HEAD_DIM * p, 2 * HEAD_DIM * (p + 1))
            q2 = q_ref[rows, sl]
            k2t = k_ref[rows, sl].T
            v2 = v_ref[rows, sl]
            out = None
            for e in range(2):
                qm = jnp.where(masks[e], q2, jnp.zeros_like(q2))
                s = jnp.dot(qm, k2t, preferred_element_type=F32)
                pe = jnp.exp(s - jnp.max(s, axis=-1, keepdims=True))
                den = jnp.sum(pe, axis=-1, keepdims=True)
                o = jnp.dot(pe.astype(BF16), v2, preferred_element_type=F32) / den
                out = o if out is None else jnp.where(masks[e], o, out)
            o_ref[rows, sl] = out.astype(BF16)


def _ctx_attn(q, k, v):
    blk = pl.BlockSpec((TOKEN_TILE, NA_WIDTH), lambda b: (b, 0))
    return pl.pallas_call(
        _ctx_attn_kernel,
        grid=(N_CTX_TOK // TOKEN_TILE,),
        in_specs=[blk, blk, blk],
        out_specs=blk,
        out_shape=jax.ShapeDtypeStruct((N_CTX_TOK, NA_WIDTH), BF16),
        compiler_params=_cparams(1),
        name="ctx_attention",
    )(q, k, v)


def _na_build_bias_table(rpb_ref, table_ref):
    qc = lax.broadcasted_iota(jnp.int32, (GRID_W, LANES), 0)
    lane = lax.broadcasted_iota(jnp.int32, (GRID_W, LANES), 1)
    kc = lane % GRID_W
    col_start = jnp.clip(qc - WIN_W // 2, 0, GRID_W - WIN_W)
    col_in = (kc >= col_start) & (kc < col_start + WIN_W)
    neg = jnp.full((GRID_W, LANES), -jnp.inf, F32)

    def toeplitz(h, dr, lane0):
        if dr < 0 or dr >= N_DR:
            return neg
        w = jnp.broadcast_to(rpb_ref[h, dr:dr + 1, :], (GRID_W, LANES))
        return pltpu.roll(w, (lane0 - (WIN_W - 1)) % LANES, 1, stride=1, stride_axis=0)

    for h in range(NA_HEADS):
        for i in range(N_DR_PAIRS):
            t = jnp.where(lane < GRID_W, toeplitz(h, i - 1, 0), toeplitz(h, i, GRID_W))
            table_ref[h, i] = jnp.where(col_in, t, neg)


def _na_kernel(q_ref, k_ref, v_ref, kc_ref, vc_ref, rpb_ref, o_ref, table_ref, kc_scr, vc_scr):
    b = pl.program_id(0)
    g = pl.program_id(1)

    @pl.when((b == 0) & (g == 0))
    def _():
        _na_build_bias_table(rpb_ref, table_ref)

    @pl.when(g == 0)
    def _():
        kc_scr[...] = kc_ref[0, 0].reshape(PAST_LEN, NA_WIDTH).astype(BF16)
        vc_scr[...] = vc_ref[0, 0].reshape(PAST_LEN, NA_WIDTH).astype(BF16)

    win_row0 = jnp.where(g < NA_GROUPS // 2, 0, ROWS - NA_K_ROWS)
    start = pl.multiple_of(win_row0 * GRID_W, GRID_W)
    q_row = g * NA_Q_ROWS + lax.broadcasted_iota(jnp.int32, (NA_Q, 1), 0) // GRID_W
    k_row = win_row0 + lax.broadcasted_iota(jnp.int32, (1, NA_K), 1) // GRID_W
    row_start = jnp.clip(q_row - KH // 2, 0, ROWS - KH)
    row_in = (k_row >= row_start) & (k_row < row_start + KH)
    masks = _head_masks()
    for p in range(NA_HEADS // 2):
        sl = slice(2 * HEAD_DIM * p, 2 * HEAD_DIM * (p + 1))
        q2 = q_ref[:, sl]
        klt = k_ref[pl.ds(start, NA_K), sl].T
        vl = v_ref[pl.ds(start, NA_K), sl]
        kct = kc_scr[:, sl].T
        vc = vc_scr[:, sl]
        out = None
        for e in range(2):
            head = 2 * p + e
            bias_rows = []
            for a in range(NA_Q_ROWS):
                tiles = []
                for m in range(NA_K_ROWS // 2):
                    dr = win_row0 + 2 * m - (g * NA_Q_ROWS + a) + (WIN_H - 1)
                    tiles.append(table_ref[head, jnp.clip(dr + 1, 0, N_DR_PAIRS - 1)])
                bias_rows.append(jnp.concatenate(tiles, axis=1))
            bias = jnp.concatenate(bias_rows, axis=0)
            qm = jnp.where(masks[e], q2, jnp.zeros_like(q2))
            s_loc = jnp.where(row_in, jnp.dot(qm, klt, preferred_element_type=F32) + bias, -jnp.inf)
            s_ctx = jnp.dot(qm, kct, preferred_element_type=F32)
            m_max = jnp.maximum(jnp.max(s_loc, axis=-1, keepdims=True),
                                jnp.max(s_ctx, axis=-1, keepdims=True))
            p_loc = jnp.exp(s_loc - m_max)
            p_ctx = jnp.exp(s_ctx - m_max)
            den = jnp.sum(p_loc, axis=-1, keepdims=True) + jnp.sum(p_ctx, axis=-1, keepdims=True)
            o = (jnp.dot(p_loc.astype(BF16), vl, preferred_element_type=F32)
                 + jnp.dot(p_ctx.astype(BF16), vc, preferred_element_type=F32)) / den
            out = o if out is None else jnp.where(masks[e], o, out)
        o_ref[:, sl] = out.astype(BF16)


def _na_attn(q, k, v, cache_k, cache_v, rpb_e):
    smp_blk0 = N_CTX_TOK // DEC_SEQ
    q_blk0 = N_CTX_TOK // NA_Q
    kv = pl.BlockSpec((DEC_SEQ, NA_WIDTH), lambda b, g: (smp_blk0 + b, 0))
    ctx = pl.BlockSpec((1, 1, PAST_LEN, NA_HEADS, HEAD_DIM), lambda b, g: (b, 0, 0, 0, 0))
    rpb_pad = jnp.pad(rpb_e.astype(F32), ((0, 0), (0, 0), (0, LANES - N_DC)))
    return pl.pallas_call(
        _na_kernel,
        grid=(DEC_BATCH, NA_GROUPS),
        in_specs=[
            pl.BlockSpec((NA_Q, NA_WIDTH), lambda b, g: (q_blk0 + b * NA_GROUPS + g, 0)),
            kv, kv, ctx, ctx,
            pl.BlockSpec((NA_HEADS, N_DR, LANES), lambda b, g: (0, 0, 0)),
        ],
        out_specs=pl.BlockSpec((NA_Q, NA_WIDTH), lambda b, g: (b * NA_GROUPS + g, 0)),
        out_shape=jax.ShapeDtypeStruct((N_SMP_TOK, NA_WIDTH), BF16),
        scratch_shapes=[pltpu.VMEM((NA_HEADS, N_DR_PAIRS, GRID_W, LANES), F32),
                        pltpu.VMEM((PAST_LEN, NA_WIDTH), BF16), pltpu.VMEM((PAST_LEN, NA_WIDTH), BF16)],
        compiler_params=_cparams(2),
        name="neighbourhood_attention",
    )(q, k, v, cache_k, cache_v, rpb_pad)


def _gelu_tanh(x):
    c0 = float(np.sqrt(2.0 / np.pi))
    inner = x * (c0 + (c0 * 0.044715) * (x * x))
    return (0.5 * x) * (1.0 + jnp.tanh(inner))


def _lru_build_gate_weights(wr_ref, wi_ref, w_scr):
    blocks_per_group = LRU_SUB // LRU_BLOCK
    w_scr[...] = jnp.zeros_like(w_scr)
    for d in range(2):
        for kind, w_ref in enumerate((wr_ref, wi_ref)):
            col0 = (2 * d + kind) * LRU_SUB
            for blk in range(LRU_BLOCKS):
                c, n = divmod(blk, blocks_per_group)
                r0 = n * LRU_BLOCK
                w_scr[c, r0:r0 + LRU_BLOCK, col0 + r0:col0 + r0 + LRU_BLOCK] = (
                    0.5 * w_ref[d * LRU_BLOCKS + blk]).astype(BF16)


def _lru_kernel(*refs, zero_state, t_len):
    refs = list(refs)
    xb_ref, gb_ref, cw_ref, cb_ref, wr_ref, wi_ref, br_ref, bi_ref, lam_ref = refs[:9]
    h0f_ref, h0b_ref = (None, None) if zero_state else refs[9:11]
    y_ref, hlf_ref, hlb_ref, af_ref, uf_ref, ab_ref, ub_ref, w_scr = refs[-8:]
    width = xb_ref.shape[1]
    n_seq = xb_ref.shape[0] // t_len
    n_blk = t_len // SUBLANES
    row = lax.broadcasted_iota(jnp.int32, (t_len, 1), 0)
    in_block = lax.broadcasted_iota(jnp.int32, (1, SUBLANES, 1), 1)

    @pl.when(pl.program_id(0) == 0)
    def _():
        _lru_build_gate_weights(wr_ref, wi_ref, w_scr)

    def shifted(z, s):
        rolled = pltpu.roll(z, (-s) % t_len, axis=0)
        ok = (row + s >= 0) & (row + s < t_len)
        return jnp.where(ok, rolled, 0.0)

    left = (CONV_W - 1) // 2
    for q, c in [(q, c) for q in range(n_seq) for c in range(width // LRU_SUB)]:
        rows = slice(q * t_len, (q + 1) * t_len)
        cs = slice(c * LRU_SUB, (c + 1) * LRU_SUB)
        x = xb_ref[rows, cs]
        xc = cb_ref[:, cs]
        for j in range(CONV_W):
            tap = x if j == left else shifted(x, j - left)
            xc = xc + tap * cw_ref[j:j + 1, cs]
        half_gates = jnp.dot(xc.astype(BF16), w_scr[c], preferred_element_type=F32)
        half_xc = 0.5 * xc
        for d, (a_ref, u_ref) in enumerate(((af_ref, uf_ref), (ab_ref, ub_ref))):
            t_r = jnp.tanh(half_gates[:, (2 * d) * LRU_SUB:(2 * d + 1) * LRU_SUB] + 0.5 * br_ref[d:d + 1, cs])
            t_i = jnp.tanh(half_gates[:, (2 * d + 1) * LRU_SUB:(2 * d + 2) * LRU_SUB] + 0.5 * bi_ref[d:d + 1, cs])
            lam = lam_ref[d:d + 1, cs]
            log_sig = jnp.minimum(lam, 0.0) - jnp.log1p(jnp.exp(-jnp.abs(lam)))
            half_c_log_sig = (0.5 * LRU_C) * log_sig
            log_a = t_r * half_c_log_sig + half_c_log_sig
            a = jnp.exp(log_a)
            var = -jnp.tanh(log_a) * (a * a + 1.0)
            u = jnp.where(var > 0.0, var * lax.rsqrt(var), 0.0) * ((t_i + 1.0) * half_xc)
            a = a.reshape(n_blk, SUBLANES, LRU_SUB)
            u = u.reshape(n_blk, SUBLANES, LRU_SUB)
            step = 1
            while step < SUBLANES:
                if d == 0:
                    ok, shift = in_block >= step, step
                else:
                    ok, shift = in_block < SUBLANES - step, SUBLANES - step
                a_prev = jnp.where(ok, pltpu.roll(a, shift, axis=1), 1.0)
                u_prev = jnp.where(ok, pltpu.roll(u, shift, axis=1), 0.0)
                u = u + a * u_prev
                a = a * a_prev
                step *= 2
            a_ref[rows, cs] = a.reshape(t_len, LRU_SUB)
            u_ref[rows, cs] = u.reshape(t_len, LRU_SUB)

    def body(i, carry):
        new = []
        for q, (cf, cb) in enumerate(carry):
            f0 = pl.multiple_of(q * t_len + i * SUBLANES, SUBLANES)
            b0 = pl.multiple_of(q * t_len + (n_blk - 1 - i) * SUBLANES, SUBLANES)
            hf = uf_ref[pl.ds(f0, SUBLANES), :] + af_ref[pl.ds(f0, SUBLANES), :] * cf
            hb = ub_ref[pl.ds(b0, SUBLANES), :] + ab_ref[pl.ds(b0, SUBLANES), :] * cb
            uf_ref[pl.ds(f0, SUBLANES), :] = hf
            ub_ref[pl.ds(b0, SUBLANES), :] = hb
            new.append((jnp.broadcast_to(hf[SUBLANES - 1:SUBLANES, :], (SUBLANES, width)),
                        jnp.broadcast_to(hb[0:1, :], (SUBLANES, width))))
        return tuple(new)

    if zero_state:
        init = tuple((jnp.zeros((SUBLANES, width), F32),) * 2 for _ in range(n_seq))
    else:
        init = tuple((jnp.broadcast_to(h0f_ref[q], (SUBLANES, width)),
                      jnp.broadcast_to(h0b_ref[q], (SUBLANES, width))) for q in range(n_seq))
    last = lax.fori_loop(0, n_blk, body, init)
    for q, (cf, cb) in enumerate(last):
        hlf_ref[q] = cf[0:1, :]
        hlb_ref[q] = cb[0:1, :]
    y_ref[...] = ((uf_ref[...] + ub_ref[...]) * _gelu_tanh(gb_ref[...])).astype(BF16)


def _lru(xb, gb, conv_w, conv_b, w_r, b_r, w_i, b_i, lam, h0, n_seq, t_len, tok_blk0, seq_per_step):
    rows = seq_per_step * t_len
    const2 = lambda s: (0, 0)
    const3 = lambda s: (0, 0, 0)
    blocks = pl.BlockSpec((2 * LRU_BLOCKS, LRU_BLOCK, LRU_BLOCK), const3)
    per_dir = pl.BlockSpec((2, LRU_WIDTH), const2)
    state = pl.BlockSpec((seq_per_step, 1, LRU_WIDTH), lambda s: (s, 0, 0))
    state_shape = jax.ShapeDtypeStruct((n_seq, 1, LRU_WIDTH), F32)
    return pl.pallas_call(
        functools.partial(_lru_kernel, zero_state=h0 is None, t_len=t_len),
        grid=(n_seq // seq_per_step,),
        in_specs=[
            pl.BlockSpec((rows, LRU_WIDTH), lambda s: (tok_blk0 + s, 0)),
            pl.BlockSpec((rows, LRU_WIDTH), lambda s: (tok_blk0 + s, 0)),
            pl.BlockSpec((CONV_W, LRU_WIDTH), const2),
            pl.BlockSpec((1, LRU_WIDTH), const2),
            blocks, blocks, per_dir, per_dir, per_dir,
        ] + ([] if h0 is None else [state, state]),
        out_specs=[pl.BlockSpec((rows, LRU_WIDTH), lambda s: (s, 0)), state, state],
        out_shape=[jax.ShapeDtypeStruct((n_seq * t_len, LRU_WIDTH), BF16), state_shape, state_shape],
        scratch_shapes=[pltpu.VMEM((rows, LRU_WIDTH), F32)] * 4
        + [pltpu.VMEM((LRU_WIDTH // LRU_SUB, LRU_SUB, 4 * LRU_SUB), BF16)],
        compiler_params=_cparams(1),
        name="rglru",
    )(xb, gb, conv_w, conv_b.reshape(1, LRU_WIDTH),
      w_r.reshape(2 * LRU_BLOCKS, LRU_BLOCK, LRU_BLOCK), w_i.reshape(2 * LRU_BLOCKS, LRU_BLOCK, LRU_BLOCK),
      b_r, b_i, lam, *(() if h0 is None else h0))


def _fourier_kernel(x_ref, mod_ref, g_ref, cs_ref, ct_ref, w_ref, o_ref, w_bf_ref, *, mod_row0, t_len):
    @pl.when(pl.program_id(0) == 0)
    def _():
        w_bf_ref[...] = w_ref[...].astype(BF16)

    x = x_ref[...]
    row = mod_row0 + pl.program_id(0) if mod_row0 else 0
    h = _norm_mod(x, g_ref[...], _mod_vec(mod_ref, row, 3), _mod_vec(mod_ref, row, 4)).astype(BF16)
    cos_parts, sin_parts = [], []
    for g in range(FOURIER_GROUPS):
        ab = jnp.dot(h[:, g * GROUP_W:(g + 1) * GROUP_W], cs_ref[...], preferred_element_type=F32)
        cos_parts.append(ab[:, :GROUP_W])
        sin_parts.append(ab[:, GROUP_W:])
    cos_all = jnp.concatenate(cos_parts, axis=1).astype(BF16)
    sin_all = jnp.concatenate(sin_parts, axis=1).astype(BF16)
    f_parts = []
    for q in range(x.shape[0] // t_len):
        rows = slice(q * t_len, (q + 1) * t_len)
        stacked = jnp.concatenate([cos_all[rows], sin_all[rows]], axis=0)
        f_parts.append(jnp.dot(ct_ref[...], stacked, preferred_element_type=F32))
    f = jnp.concatenate(f_parts, axis=0) * ((t_len * GROUP_W) ** -0.5)
    y = jnp.dot(f.astype(BF16), w_bf_ref[...], preferred_element_type=F32)
    o_ref[...] = x + _mod_vec(mod_ref, row, 5) * y


def _dft_tables(t_len):
    def cos_sin(n):
        jk = np.outer(np.arange(n), np.arange(n)) % n
        ang = 2.0 * np.pi * jk.astype(np.float64) / n
        return np.cos(ang), np.sin(ang)

    cc, sc = cos_sin(GROUP_W)
    ct, st = cos_sin(t_len)
    chan = jnp.asarray(np.concatenate([cc, sc], axis=1).astype(np.float32)).astype(BF16)
    time = jnp.asarray(np.concatenate([ct, -st], axis=1).astype(np.float32)).astype(BF16)
    return chan, time


def _fourier(x, mod, g, w_out, n_seq, t_len, tok_blk0, mod_row0, seq_per_step):
    assert seq_per_step == 1 or mod_row0 == 0
    chan, time = _dft_tables(t_len)
    rows = seq_per_step * t_len
    seq = lambda s: (tok_blk0 + s, 0)
    const = lambda s: (0, 0)
    return pl.pallas_call(
        functools.partial(_fourier_kernel, mod_row0=mod_row0, t_len=t_len),
        grid=(n_seq // seq_per_step,),
        in_specs=[
            pl.BlockSpec((rows, D_MODEL), seq),
            _mod_spec(),
            pl.BlockSpec((1, D_MODEL), const),
            _resident((GROUP_W, 2 * GROUP_W), const),
            _resident((t_len, 2 * t_len), const),
            _resident((D_MODEL, D_MODEL), const),
        ],
        out_specs=pl.BlockSpec((rows, D_MODEL), seq),
        out_shape=jax.ShapeDtypeStruct((N_TOK, D_MODEL), F32),
        input_output_aliases={0: 0},
        scratch_shapes=[pltpu.VMEM((D_MODEL, D_MODEL), BF16)],
        compiler_params=_cparams(1),
        name="fourier_mixer",
    )(x, mod, g.reshape(1, D_MODEL), chan, time, w_out)


def kernel(x_prompt, x_sample, cache_k, cache_v, state_lru_fwd, state_lru_bwd, c, c_ctx, w_ada, b_ada, norm_g, ffn1_gate, ffn1_up, ffn1_down, ffn2_gate, ffn2_up, ffn2_down, w_in, q_norm_g, k_norm_g, rpb, conv_w, conv_b, lru_w_r, lru_b_r, lru_w_i, lru_b_i, lru_lambda, w_out_ab, w_out_c):
    assert DEPTH == 2, "one neighbourhood/RG-LRU layer followed by one Fourier layer"
    c_ctx2 = c_ctx.reshape(1, D_MODEL)
    b_ada3 = b_ada.reshape(DEPTH, 1, MOD_WIDTH)
    mod0 = _adaln(c_ctx2, c, w_ada, b_ada3, 0)

    ffn1 = (ffn1_gate, ffn1_up, ffn1_down)
    ffn2 = (ffn2_gate, ffn2_up, ffn2_down)

    x, mod1 = _ffn((x_prompt.reshape(N_CTX_TOK, D_MODEL), x_sample.reshape(N_SMP_TOK, D_MODEL)),
                   mod0, norm_g[0, 0], *ffn1, 0, 0, adaln_next=(c_ctx2, c, w_ada, b_ada3, 1))
    q, k, v, xb, gb, new_k, new_v = _proj(x, mod0, norm_g[0, 1], w_in[0], q_norm_g[0], k_norm_g[0])
    o_ctx = _ctx_attn(q, k, v)
    o_smp = _na_attn(q, k, v, cache_k, cache_v, rpb[0])
    lru_prm = (conv_w[0], conv_b[0], lru_w_r[0], lru_b_r[0], lru_w_i[0], lru_b_i[0], lru_lambda[0])
    yb_ctx, new_hf, new_hb = _lru(xb, gb, *lru_prm, None, BATCH, SEQ, 0, TOKEN_TILE // SEQ)
    yb_smp, _, _ = _lru(xb, gb, *lru_prm, (state_lru_fwd, state_lru_bwd),
                        DEC_BATCH, DEC_SEQ, N_CTX_TOK // DEC_SEQ, 1)
    (x,) = _ffn((x,), mod0, norm_g[0, 2], *ffn2, 0, 6, mixer_out=(o_ctx, o_smp, yb_ctx, yb_smp, w_out_ab))

    (x,) = _ffn((x,), mod1, norm_g[1, 0], *ffn1, 1, 0)
    x = _fourier(x, mod1, norm_g[1, 1], w_out_c[0], BATCH, SEQ, 0, 0, TOKEN_TILE // SEQ)
    x = _fourier(x, mod1, norm_g[1, 1], w_out_c[0], DEC_BATCH, DEC_SEQ, N_CTX_TOK // DEC_SEQ, 1, 1)
    y_prompt, y_sample = _ffn((x,), mod1, norm_g[1, 2], *ffn2, 1, 6, split_out=True)

    return (y_prompt.reshape(BATCH, SEQ, D_MODEL), y_sample.reshape(DEC_BATCH, DEC_SEQ, D_MODEL),
            new_k.reshape(BATCH, 1, SEQ, NA_HEADS, HEAD_DIM), new_v.reshape(BATCH, 1, SEQ, NA_HEADS, HEAD_DIM),
            new_hf, new_hb)
```

```python
import functools

import numpy as np
import jax
import jax.numpy as jnp
from jax import lax
from jax.experimental import pallas as pl
from jax.experimental.pallas import tpu as pltpu

F32 = jnp.float32
BF16 = jnp.bfloat16

D_MODEL = 1024
BATCH = 16
SEQ = 256
DEPTH = 2
DEC_BATCH = 2
DEC_SEQ = 1024
PAST_LEN = 256
GRID_W = 64
HEAD_DIM = 64
NA_WIDTH = 512
NA_HEADS = 8
WIN_H = 8
WIN_W = 16
LRU_WIDTH = 512
LRU_BLOCKS = 8
LRU_BLOCK = 64
LRU_C = 8.0
LRU_SUB = 256
CONV_W = 4
FOURIER_GROUPS = 4
GROUP_W = D_MODEL // FOURIER_GROUPS
D_FF = 2816
N_MOD = 9
IN_WIDTH = 3 * NA_WIDTH + 2 * LRU_WIDTH
EPS = 1e-6

N_CTX_TOK = BATCH * SEQ
N_SMP_TOK = DEC_BATCH * DEC_SEQ
N_TOK = N_CTX_TOK + N_SMP_TOK
MOD_ROWS = 8
MOD_WIDTH = N_MOD * D_MODEL
ROWS = DEC_SEQ // GRID_W
KH = min(WIN_H, ROWS)

TOKEN_TILE = 512
N_CTX_TILES = N_CTX_TOK // TOKEN_TILE
FFN_TILE = 512
FF_TILE = 256
FF_CHUNKS = D_FF // FF_TILE
FF_STAGE_SLOTS = 2
SUBLANES = 8
LANES = 128
VMEM_LIMIT = 56 * 1024 * 1024

NA_Q_ROWS = 4
NA_GROUPS = ROWS // NA_Q_ROWS
NA_K_ROWS = 12
NA_Q = NA_Q_ROWS * GRID_W
NA_K = NA_K_ROWS * GRID_W
N_DR = 2 * WIN_H - 1
N_DC = 2 * WIN_W - 1
N_DR_PAIRS = N_DR + 1


def _cparams(n_axes):
    return pltpu.CompilerParams(
        dimension_semantics=("arbitrary",) * n_axes, vmem_limit_bytes=VMEM_LIMIT)


def _resident(block_shape, index_map):
    return pl.BlockSpec(block_shape, index_map, pipeline_mode=pl.Buffered(1))


def _mod_spec():
    return _resident((MOD_ROWS, MOD_WIDTH), lambda i: (0, 0))


def _mod_row_of_tile(i, tile=TOKEN_TILE):
    n_ctx_tiles = N_CTX_TOK // tile
    tiles_per_seq = DEC_SEQ // tile
    return jnp.where(i < n_ctx_tiles, 0, 1 + (i - n_ctx_tiles) // tiles_per_seq)


def _mod_vec(mod_ref, row, k):
    return mod_ref[pl.ds(row, 1), k * D_MODEL:(k + 1) * D_MODEL]


def _norm_mod(x, g, shift, scale):
    ms = jnp.mean(x * x, axis=-1, keepdims=True)
    return (x * lax.rsqrt(ms + EPS)) * (g * (1.0 + scale)) + shift


def _adaln_slab(cctx_ref, c_ref, w_ref, b_ref, cond_ref, layer):
    cond_ref[...] = jnp.zeros_like(cond_ref)
    cond_ref[0:1, :] = cctx_ref[...]
    cond_ref[1:1 + DEC_BATCH, :] = c_ref[...]
    cond = cond_ref[...]
    s = (cond * jax.nn.sigmoid(cond)).astype(BF16)
    return jnp.dot(s, w_ref[...].astype(BF16), preferred_element_type=F32) + b_ref[layer:layer + 1, :]


def _adaln_specs(layer, slab):
    return ([pl.BlockSpec((1, D_MODEL), lambda i: (0, 0)),
             pl.BlockSpec((DEC_BATCH, D_MODEL), lambda i: (0, 0)),
             pl.BlockSpec((None, D_MODEL, slab), lambda i: (layer, 0, i)),
             pl.BlockSpec((DEPTH, slab), lambda i: (0, i))],
            pl.BlockSpec((MOD_ROWS, slab), lambda i: (0, i)))


def _adaln_kernel(cctx_ref, c_ref, w_ref, b_ref, o_ref, cond_ref, *, layer):
    o_ref[...] = _adaln_slab(cctx_ref, c_ref, w_ref, b_ref, cond_ref, layer)


def _adaln(c_ctx, c, w_ada, b_ada, layer):
    slab = MOD_WIDTH // 4
    in_specs, out_spec = _adaln_specs(layer, slab)
    return pl.pallas_call(
        functools.partial(_adaln_kernel, layer=layer),
        grid=(MOD_WIDTH // slab,),
        in_specs=in_specs,
        out_specs=out_spec,
        out_shape=jax.ShapeDtypeStruct((MOD_ROWS, MOD_WIDTH), F32),
        scratch_shapes=[pltpu.VMEM((MOD_ROWS, D_MODEL), F32)],
        compiler_params=_cparams(1),
        name="adaln",
    )(c_ctx, c, w_ada, b_ada)


def _ffn_weight_copy(w_hbm, stage_ref, sem_ref, layer, j, ff_axis):
    ff = pl.ds(j * FF_TILE, FF_TILE)
    src = w_hbm.at[layer, :, ff] if ff_axis == 1 else w_hbm.at[layer, ff, :]
    slot = j % FF_STAGE_SLOTS
    return pltpu.make_async_copy(src, stage_ref.at[slot], sem_ref.at[slot])


def _mixer_out_copy(w_hbm, stage_ref, sem_ref, j):
    rows = stage_ref.shape[1]
    slot = j % stage_ref.shape[0]
    return pltpu.make_async_copy(w_hbm.at[0, pl.ds(j * rows, rows), :], stage_ref.at[slot], sem_ref.at[slot])


def _ffn_kernel(*refs, layer, mod_base, split_in, split_out, mixer_out, adaln_next):
    refs = list(refs)
    take = lambda n: [refs.pop(0) for _ in range(n)]
    x_refs = take(2 if split_in else 1)
    mix_refs = take(4) if mixer_out else None
    mod_ref, g_ref = take(2)
    wo_hbm = take(1)[0] if mixer_out else None
    ada_refs = take(4) if adaln_next is not None else None
    wg_hbm, wu_hbm, wd_hbm = take(3)
    o_refs = take(2 if split_out else 1)
    modn_ref = take(1)[0] if adaln_next is not None else None
    wg_bf, wu_bf, wd_bf, stg_g, stg_u, stg_d, sem_g, sem_u, sem_d = take(9)
    if adaln_next is not None:
        modn_ref[...] = _adaln_slab(*ada_refs, take(1)[0], adaln_next)
    streams = ((wg_hbm, stg_g, sem_g, wg_bf, 1), (wu_hbm, stg_u, sem_u, wu_bf, 1),
               (wd_hbm, stg_d, sem_d, wd_bf, 0))

    i = pl.program_id(0)
    is_ctx = i < N_CTX_TOK // FFN_TILE
    if split_in:
        x = jnp.where(is_ctx, x_refs[0][...], x_refs[1][...])
    else:
        x = x_refs[0][...]
    row = _mod_row_of_tile(i, FFN_TILE)

    if mixer_out:
        wo_bf, stg_o, sem_o = take(3)
        rows = stg_o.shape[1]
        n_chunks = D_MODEL // rows

        @pl.when(i == 0)
        def _():
            for j in range(stg_o.shape[0]):
                _mixer_out_copy(wo_hbm, stg_o, sem_o, j).start()
            for j in range(n_chunks):
                _mixer_out_copy(wo_hbm, stg_o, sem_o, j).wait()
                wo_bf[j * rows:(j + 1) * rows, :] = stg_o[j % stg_o.shape[0]].astype(BF16)
                if j + stg_o.shape[0] < n_chunks:
                    _mixer_out_copy(wo_hbm, stg_o, sem_o, j + stg_o.shape[0]).start()

        oc_ref, os_ref, yc_ref, ys_ref = mix_refs
        cat = jnp.concatenate([jnp.where(is_ctx, oc_ref[...], os_ref[...]),
                               jnp.where(is_ctx, yc_ref[...], ys_ref[...])], axis=1)
        x = x + _mod_vec(mod_ref, row, mod_base - 1) * jnp.dot(cat, wo_bf[...], preferred_element_type=F32)

    h = _norm_mod(x, g_ref[...], _mod_vec(mod_ref, row, mod_base),
                  _mod_vec(mod_ref, row, mod_base + 1)).astype(BF16)

    def start_chunk(j):
        for w_hbm, stg, sem, _, ff_axis in streams:
            _ffn_weight_copy(w_hbm, stg, sem, layer, j, ff_axis).start()

    def finish_chunk(j):
        for w_hbm, stg, sem, w_bf, ff_axis in streams:
            _ffn_weight_copy(w_hbm, stg, sem, layer, j, ff_axis).wait()
            w_bf[j] = stg[j % FF_STAGE_SLOTS].astype(BF16)

    def run(stream_weights):
        if stream_weights:
            for j in range(FF_STAGE_SLOTS):
                start_chunk(j)
        acc = None
        for j in range(FF_CHUNKS):
            if stream_weights:
                finish_chunk(j)
                if j + FF_STAGE_SLOTS < FF_CHUNKS:
                    start_chunk(j + FF_STAGE_SLOTS)
            a = jnp.dot(h, wg_bf[j], preferred_element_type=F32)
            b = jnp.dot(h, wu_bf[j], preferred_element_type=F32)
            act = (a * jax.nn.sigmoid(a) * b).astype(BF16)
            y = jnp.dot(act, wd_bf[j], preferred_element_type=F32)
            acc = y if acc is None else acc + y
        res = x + 0.5 * _mod_vec(mod_ref, row, mod_base + 2) * acc
        if split_out:
            @pl.when(is_ctx)
            def _():
                o_refs[0][...] = res

            @pl.when(jnp.logical_not(is_ctx))
            def _():
                o_refs[1][...] = res
        else:
            o_refs[0][...] = res

    @pl.when(i == 0)
    def _():
        run(True)

    @pl.when(i > 0)
    def _():
        run(False)


def _ffn(xs, mod, g, wg, wu, wd, layer, mod_base, split_out=False, mixer_out=None, adaln_next=None):
    tm = FFN_TILE
    n_ctx_tiles = N_CTX_TOK // tm
    split_in = len(xs) == 2

    def tiles(width):
        return (pl.BlockSpec((tm, width), lambda i: (i, 0)),
                pl.BlockSpec((tm, width), lambda i: (jnp.minimum(i, n_ctx_tiles - 1), 0)),
                pl.BlockSpec((tm, width), lambda i: (jnp.maximum(i - n_ctx_tiles, 0), 0)))

    tok, ctx_tok, smp_tok = tiles(D_MODEL)
    full = jax.ShapeDtypeStruct((N_TOK, D_MODEL), F32)
    pair = [jax.ShapeDtypeStruct((N_CTX_TOK, D_MODEL), F32), jax.ShapeDtypeStruct((N_SMP_TOK, D_MODEL), F32)]
    hbm = pl.BlockSpec(memory_space=pl.ANY)
    in_specs = [ctx_tok, smp_tok] if split_in else [tok]
    operands = list(xs)
    scratch = [
        pltpu.VMEM((FF_CHUNKS, D_MODEL, FF_TILE), BF16),
        pltpu.VMEM((FF_CHUNKS, D_MODEL, FF_TILE), BF16),
        pltpu.VMEM((FF_CHUNKS, FF_TILE, D_MODEL), BF16),
        pltpu.VMEM((FF_STAGE_SLOTS, D_MODEL, FF_TILE), F32),
        pltpu.VMEM((FF_STAGE_SLOTS, D_MODEL, FF_TILE), F32),
        pltpu.VMEM((FF_STAGE_SLOTS, FF_TILE, D_MODEL), F32),
        pltpu.SemaphoreType.DMA((FF_STAGE_SLOTS,)),
        pltpu.SemaphoreType.DMA((FF_STAGE_SLOTS,)),
        pltpu.SemaphoreType.DMA((FF_STAGE_SLOTS,)),
    ]
    if mixer_out is not None:
        _, ctx_half, smp_half = tiles(NA_WIDTH)
        in_specs += [ctx_half, smp_half, ctx_half, smp_half]
        operands += list(mixer_out[:4])
    in_specs += [_mod_spec(), pl.BlockSpec((1, D_MODEL), lambda i: (0, 0))]
    operands += [mod, g.reshape(1, D_MODEL)]
    out_specs = [ctx_tok, smp_tok] if split_out else [tok]
    out_shape = pair if split_out else [full]
    if adaln_next is not None:
        ada_in, ada_out = _adaln_specs(adaln_next[4], MOD_WIDTH // (N_TOK // tm))
        in_specs += ada_in
        operands += list(adaln_next[:4])
        out_specs.append(ada_out)
        out_shape.append(jax.ShapeDtypeStruct((MOD_ROWS, MOD_WIDTH), F32))
        scratch.append(pltpu.VMEM((MOD_ROWS, D_MODEL), F32))
    if mixer_out is not None:
        in_specs.insert(len(in_specs) - (4 if adaln_next is not None else 0), hbm)
        operands.insert(len(operands) - (4 if adaln_next is not None else 0), mixer_out[4])
        scratch += [
            pltpu.VMEM((D_MODEL, D_MODEL), BF16),
            pltpu.VMEM((FF_STAGE_SLOTS, FF_TILE, D_MODEL), F32),
            pltpu.SemaphoreType.DMA((FF_STAGE_SLOTS,)),
        ]
    return pl.pallas_call(
        functools.partial(_ffn_kernel, layer=layer, mod_base=mod_base, split_in=split_in,
                          split_out=split_out, mixer_out=mixer_out is not None,
                          adaln_next=None if adaln_next is None else adaln_next[4]),
        grid=(N_TOK // tm,),
        in_specs=in_specs + [hbm, hbm, hbm],
        out_specs=out_specs,
        out_shape=out_shape,
        scratch_shapes=scratch,
        compiler_params=_cparams(1),
        name="ffn",
    )(*operands, wg, wu, wd)


def _head_rms_norm(z, g, ones_bd):
    z2 = z * z
    hi = z2.astype(BF16)
    lo = (z2 - hi.astype(F32)).astype(BF16)
    n = ones_bd.shape[0]
    parts = []
    for c in range(z.shape[1] // n):
        sl = slice(c * n, (c + 1) * n)
        parts.append(jnp.dot(hi[:, sl], ones_bd, preferred_element_type=F32)
                     + jnp.dot(lo[:, sl], ones_bd, preferred_element_type=F32))
    ss = jnp.concatenate(parts, axis=1)
    return z * lax.rsqrt(ss * (1.0 / HEAD_DIM) + EPS) * g


def _proj_kernel(x_ref, mod_ref, g_ref, w_ref, qkg_ref, ones_ref,
                 q_ref, k_ref, v_ref, xb_ref, gb_ref, kout_ref, vout_ref, w_bf_ref):
    i = pl.program_id(0)

    @pl.when(i == 0)
    def _():
        w_bf_ref[...] = w_ref[...].astype(BF16)

    x = x_ref[...]
    row = _mod_row_of_tile(i)
    h = _norm_mod(x, g_ref[...], _mod_vec(mod_ref, row, 3), _mod_vec(mod_ref, row, 4)).astype(BF16)

    def proj(part):
        return jnp.dot(h, w_bf_ref[:, part * NA_WIDTH:(part + 1) * NA_WIDTH], preferred_element_type=F32)

    ones_bd = ones_ref[...]
    q = _head_rms_norm(proj(0), qkg_ref[0:1, :], ones_bd) * (HEAD_DIM ** -0.5)
    q_ref[...] = q.astype(BF16)
    k = _head_rms_norm(proj(1), qkg_ref[1:2, :], ones_bd)
    k_ref[...] = k.astype(BF16)
    v = proj(2)
    v_ref[...] = v.astype(BF16)
    xb_ref[...] = proj(3)
    gb_ref[...] = proj(4)

    @pl.when(i < N_CTX_TILES)
    def _():
        kout_ref[...] = k.reshape(TOKEN_TILE, NA_HEADS, HEAD_DIM)
        vout_ref[...] = v.reshape(TOKEN_TILE, NA_HEADS, HEAD_DIM)


def _proj(x, mod, g, w_in, q_g, k_g):
    tm = TOKEN_TILE
    head = np.arange(2 * LANES) // HEAD_DIM
    ones_bd = jnp.asarray((head[:, None] == head[None, :]).astype(np.float32), dtype=BF16)
    tok = lambda i: (i, 0)
    const = lambda i: (0, 0)
    act_f32 = jax.ShapeDtypeStruct((N_TOK, NA_WIDTH), F32)
    act_bf16 = jax.ShapeDtypeStruct((N_TOK, NA_WIDTH), BF16)
    cache = jax.ShapeDtypeStruct((N_CTX_TOK, NA_HEADS, HEAD_DIM), F32)
    cache_spec = pl.BlockSpec((tm, NA_HEADS, HEAD_DIM), lambda i: (jnp.minimum(i, N_CTX_TILES - 1), 0, 0))
    return pl.pallas_call(
        _proj_kernel,
        grid=(N_TOK // tm,),
        in_specs=[
            pl.BlockSpec((tm, D_MODEL), tok),
            _mod_spec(),
            pl.BlockSpec((1, D_MODEL), const),
            _resident((D_MODEL, IN_WIDTH), const),
            pl.BlockSpec((2, NA_WIDTH), const),
            _resident((2 * LANES, 2 * LANES), const),
        ],
        out_specs=[pl.BlockSpec((tm, NA_WIDTH), tok)] * 5 + [cache_spec, cache_spec],
        out_shape=[act_bf16, act_bf16, act_bf16, act_f32, act_f32, cache, cache],
        scratch_shapes=[pltpu.VMEM((D_MODEL, IN_WIDTH), BF16)],
        compiler_params=_cparams(1),
        name="mixer_in_proj",
    )(x, mod, g.reshape(1, D_MODEL), w_in, jnp.tile(jnp.stack([q_g, k_g]), (1, NA_HEADS)), ones_bd)


def _head_masks():
    lane = lax.broadcasted_iota(jnp.int32, (1, 2 * HEAD_DIM), 1)
    return [lane < HEAD_DIM, lane >= HEAD_DIM]


def _ctx_attn_kernel(q_ref, k_ref, v_ref, o_ref):
    masks = _head_masks()
    for b in range(q_ref.shape[0] // SEQ):
        rows = slice(b * SEQ, (b + 1) * SEQ)
        for p in range(NA_HEADS // 2):
            sl = slice(2 * HEAD_DIM * p, 2 * HEAD_DIM * (p + 1))
            q2 = q_ref[rows, sl]
            k2t = k_ref[rows, sl].T
            v2 = v_ref[rows, sl]
            out = None
            for e in range(2):
                qm = jnp.where(masks[e], q2, jnp.zeros_like(q2))
                s = jnp.dot(qm, k2t, preferred_element_type=F32)
                pe = jnp.exp(s - jnp.max(s, axis=-1, keepdims=True))
                den = jnp.sum(pe, axis=-1, keepdims=True)
                o = jnp.dot(pe.astype(BF16), v2, preferred_element_type=F32) / den
                out = o if out is None else jnp.where(masks[e], o, out)
            o_ref[rows, sl] = out.astype(BF16)


def _ctx_attn(q, k, v):
    blk = pl.BlockSpec((TOKEN_TILE, NA_WIDTH), lambda b: (b, 0))
    return pl.pallas_call(
        _ctx_attn_kernel,
        grid=(N_CTX_TOK // TOKEN_TILE,),
        in_specs=[blk, blk, blk],
        out_specs=blk,
        out_shape=jax.ShapeDtypeStruct((N_CTX_TOK, NA_WIDTH), BF16),
        compiler_params=_cparams(1),
        name="ctx_attention",
    )(q, k, v)


def _na_build_bias_table(rpb_ref, table_ref):
    qc = lax.broadcasted_iota(jnp.int32, (GRID_W, LANES), 0)
    lane = lax.broadcasted_iota(jnp.int32, (GRID_W, LANES), 1)
    kc = lane % GRID_W
    col_start = jnp.clip(qc - WIN_W // 2, 0, GRID_W - WIN_W)
    col_in = (kc >= col_start) & (kc < col_start + WIN_W)
    neg = jnp.full((GRID_W, LANES), -jnp.inf, F32)

    def toeplitz(h, dr, lane0):
        if dr < 0 or dr >= N_DR:
            return neg
        w = jnp.broadcast_to(rpb_ref[h, dr:dr + 1, :], (GRID_W, LANES))
        return pltpu.roll(w, (lane0 - (WIN_W - 1)) % LANES, 1, stride=1, stride_axis=0)

    for h in range(NA_HEADS):
        for i in range(N_DR_PAIRS):
            t = jnp.where(lane < GRID_W, toeplitz(h, i - 1, 0), toeplitz(h, i, GRID_W))
            table_ref[h, i] = jnp.where(col_in, t, neg)


def _na_kernel(q_ref, k_ref, v_ref, kc_ref, vc_ref, rpb_ref, o_ref, table_ref):
    b = pl.program_id(0)
    g = pl.program_id(1)

    @pl.when((b == 0) & (g == 0))
    def _():
        _na_build_bias_table(rpb_ref, table_ref)

    win_row0 = jnp.where(g < NA_GROUPS // 2, 0, ROWS - NA_K_ROWS)
    start = pl.multiple_of(win_row0 * GRID_W, GRID_W)
    q_row = g * NA_Q_ROWS + lax.broadcasted_iota(jnp.int32, (NA_Q, 1), 0) // GRID_W
    k_row = win_row0 + lax.broadcasted_iota(jnp.int32, (1, NA_K), 1) // GRID_W
    row_start = jnp.clip(q_row - KH // 2, 0, ROWS - KH)
    row_in = (k_row >= row_start) & (k_row < row_start + KH)
    masks = _head_masks()
    for p in range(NA_HEADS // 2):
        sl = slice(2 * HEAD_DIM * p, 2 * HEAD_DIM * (p + 1))
        q2 = q_ref[:, sl]
        klt = k_ref[pl.ds(start, NA_K), sl].T
        vl = v_ref[pl.ds(start, NA_K), sl]
        kct = kc_ref[0, :, sl].astype(BF16).T
        vc = vc_ref[0, :, sl].astype(BF16)
        out = None
        for e in range(2):
            head = 2 * p + e
            bias_rows = []
            for a in range(NA_Q_ROWS):
                tiles = []
                for m in range(NA_K_ROWS // 2):
                    dr = win_row0 + 2 * m - (g * NA_Q_ROWS + a) + (WIN_H - 1)
                    tiles.append(table_ref[head, jnp.clip(dr + 1, 0, N_DR_PAIRS - 1)])
                bias_rows.append(jnp.concatenate(tiles, axis=1))
            bias = jnp.concatenate(bias_rows, axis=0)
            qm = jnp.where(masks[e], q2, jnp.zeros_like(q2))
            s_loc = jnp.where(row_in, jnp.dot(qm, klt, preferred_element_type=F32) + bias, -jnp.inf)
            s_ctx = jnp.dot(qm, kct, preferred_element_type=F32)
            m_max = jnp.maximum(jnp.max(s_loc, axis=-1, keepdims=True),
                                jnp.max(s_ctx, axis=-1, keepdims=True))
            p_loc = jnp.exp(s_loc - m_max)
            p_ctx = jnp.exp(s_ctx - m_max)
            den = jnp.sum(p_loc, axis=-1, keepdims=True) + jnp.sum(p_ctx, axis=-1, keepdims=True)
            o = (jnp.dot(p_loc.astype(BF16), vl, preferred_element_type=F32)
                 + jnp.dot(p_ctx.astype(BF16), vc, preferred_element_type=F32)) / den
            out = o if out is None else jnp.where(masks[e], o, out)
        o_ref[:, sl] = out.astype(BF16)


def _na_attn(q, k, v, k_ctx, v_ctx, rpb_e):
    smp_blk0 = N_CTX_TOK // DEC_SEQ
    q_blk0 = N_CTX_TOK // NA_Q
    kv = pl.BlockSpec((DEC_SEQ, NA_WIDTH), lambda b, g: (smp_blk0 + b, 0))
    ctx = pl.BlockSpec((1, PAST_LEN, NA_WIDTH), lambda b, g: (b, 0, 0))
    rpb_pad = jnp.pad(rpb_e.astype(F32), ((0, 0), (0, 0), (0, LANES - N_DC)))
    return pl.pallas_call(
        _na_kernel,
        grid=(DEC_BATCH, NA_GROUPS),
        in_specs=[
            pl.BlockSpec((NA_Q, NA_WIDTH), lambda b, g: (q_blk0 + b * NA_GROUPS + g, 0)),
            kv, kv, ctx, ctx,
            pl.BlockSpec((NA_HEADS, N_DR, LANES), lambda b, g: (0, 0, 0)),
        ],
        out_specs=pl.BlockSpec((NA_Q, NA_WIDTH), lambda b, g: (b * NA_GROUPS + g, 0)),
        out_shape=jax.ShapeDtypeStruct((N_SMP_TOK, NA_WIDTH), BF16),
        scratch_shapes=[pltpu.VMEM((NA_HEADS, N_DR_PAIRS, GRID_W, LANES), F32)],
        compiler_params=_cparams(2),
        name="neighbourhood_attention",
    )(q, k, v, k_ctx, v_ctx, rpb_pad)


def _gelu_tanh(x):
    c0 = float(np.sqrt(2.0 / np.pi))
    inner = x * (c0 + (c0 * 0.044715) * (x * x))
    return (0.5 * x) * (1.0 + jnp.tanh(inner))


def _lru_build_gate_weights(wr_ref, wi_ref, w_scr):
    blocks_per_group = LRU_SUB // LRU_BLOCK
    w_scr[...] = jnp.zeros_like(w_scr)
    for d in range(2):
        for kind, w_ref in enumerate((wr_ref, wi_ref)):
            col0 = (2 * d + kind) * LRU_SUB
            for blk in range(LRU_BLOCKS):
                c, n = divmod(blk, blocks_per_group)
                r0 = n * LRU_BLOCK
                w_scr[c, r0:r0 + LRU_BLOCK, col0 + r0:col0 + r0 + LRU_BLOCK] = (
                    0.5 * w_ref[d * LRU_BLOCKS + blk]).astype(BF16)


def _lru_kernel(*refs, zero_state, t_len):
    refs = list(refs)
    xb_ref, gb_ref, cw_ref, cb_ref, wr_ref, wi_ref, br_ref, bi_ref, lam_ref = refs[:9]
    h0f_ref, h0b_ref = (None, None) if zero_state else refs[9:11]
    y_ref, hlf_ref, hlb_ref, af_ref, uf_ref, ab_ref, ub_ref, w_scr = refs[-8:]
    width = xb_ref.shape[1]
    n_seq = xb_ref.shape[0] // t_len
    n_blk = t_len // SUBLANES
    row = lax.broadcasted_iota(jnp.int32, (t_len, 1), 0)
    in_block = lax.broadcasted_iota(jnp.int32, (1, SUBLANES, 1), 1)

    @pl.when(pl.program_id(0) == 0)
    def _():
        _lru_build_gate_weights(wr_ref, wi_ref, w_scr)

    def shifted(z, s):
        rolled = pltpu.roll(z, (-s) % t_len, axis=0)
        ok = (row + s >= 0) & (row + s < t_len)
        return jnp.where(ok, rolled, 0.0)

    left = (CONV_W - 1) // 2
    for q, c in [(q, c) for q in range(n_seq) for c in range(width // LRU_SUB)]:
        rows = slice(q * t_len, (q + 1) * t_len)
        cs = slice(c * LRU_SUB, (c + 1) * LRU_SUB)
        x = xb_ref[rows, cs]
        xc = cb_ref[:, cs]
        for j in range(CONV_W):
            tap = x if j == left else shifted(x, j - left)
            xc = xc + tap * cw_ref[j:j + 1, cs]
        half_gates = jnp.dot(xc.astype(BF16), w_scr[c], preferred_element_type=F32)
        half_xc = 0.5 * xc
        for d, (a_ref, u_ref) in enumerate(((af_ref, uf_ref), (ab_ref, ub_ref))):
            t_r = jnp.tanh(half_gates[:, (2 * d) * LRU_SUB:(2 * d + 1) * LRU_SUB] + 0.5 * br_ref[d:d + 1, cs])
            t_i = jnp.tanh(half_gates[:, (2 * d + 1) * LRU_SUB:(2 * d + 2) * LRU_SUB] + 0.5 * bi_ref[d:d + 1, cs])
            lam = lam_ref[d:d + 1, cs]
            log_sig = jnp.minimum(lam, 0.0) - jnp.log1p(jnp.exp(-jnp.abs(lam)))
            half_c_log_sig = (0.5 * LRU_C) * log_sig
            log_a = t_r * half_c_log_sig + half_c_log_sig
            a = jnp.exp(log_a)
            var = -jnp.tanh(log_a) * (a * a + 1.0)
            u = jnp.where(var > 0.0, var * lax.rsqrt(var), 0.0) * ((t_i + 1.0) * half_xc)
            a = a.reshape(n_blk, SUBLANES, LRU_SUB)
            u = u.reshape(n_blk, SUBLANES, LRU_SUB)
            step = 1
            while step < SUBLANES:
                if d == 0:
                    ok, shift = in_block >= step, step
                else:
                    ok, shift = in_block < SUBLANES - step, SUBLANES - step
                a_prev = jnp.where(ok, pltpu.roll(a, shift, axis=1), 1.0)
                u_prev = jnp.where(ok, pltpu.roll(u, shift, axis=1), 0.0)
                u = u + a * u_prev
                a = a * a_prev
                step *= 2
            a_ref[rows, cs] = a.reshape(t_len, LRU_SUB)
            u_ref[rows, cs] = u.reshape(t_len, LRU_SUB)

    def body(i, carry):
        new = []
        for q, (cf, cb) in enumerate(carry):
            f0 = pl.multiple_of(q * t_len + i * SUBLANES, SUBLANES)
            b0 = pl.multiple_of(q * t_len + (n_blk - 1 - i) * SUBLANES, SUBLANES)
            hf = uf_ref[pl.ds(f0, SUBLANES), :] + af_ref[pl.ds(f0, SUBLANES), :] * cf
            hb = ub_ref[pl.ds(b0, SUBLANES), :] + ab_ref[pl.ds(b0, SUBLANES), :] * cb
            uf_ref[pl.ds(f0, SUBLANES), :] = hf
            ub_ref[pl.ds(b0, SUBLANES), :] = hb
            new.append((jnp.broadcast_to(hf[SUBLANES - 1:SUBLANES, :], (SUBLANES, width)),
                        jnp.broadcast_to(hb[0:1, :], (SUBLANES, width))))
        return tuple(new)

    if zero_state:
        init = tuple((jnp.zeros((SUBLANES, width), F32),) * 2 for _ in range(n_seq))
    else:
        init = tuple((jnp.broadcast_to(h0f_ref[q], (SUBLANES, width)),
                      jnp.broadcast_to(h0b_ref[q], (SUBLANES, width))) for q in range(n_seq))
    last = lax.fori_loop(0, n_blk, body, init)
    for q, (cf, cb) in enumerate(last):
        hlf_ref[q] = cf[0:1, :]
        hlb_ref[q] = cb[0:1, :]
    y_ref[...] = ((uf_ref[...] + ub_ref[...]) * _gelu_tanh(gb_ref[...])).astype(BF16)


def _lru(xb, gb, conv_w, conv_b, w_r, b_r, w_i, b_i, lam, h0, n_seq, t_len, tok_blk0, seq_per_step):
    rows = seq_per_step * t_len
    const2 = lambda s: (0, 0)
    const3 = lambda s: (0, 0, 0)
    blocks = pl.BlockSpec((2 * LRU_BLOCKS, LRU_BLOCK, LRU_BLOCK), const3)
    per_dir = pl.BlockSpec((2, LRU_WIDTH), const2)
    state = pl.BlockSpec((seq_per_step, 1, LRU_WIDTH), lambda s: (s, 0, 0))
    state_shape = jax.ShapeDtypeStruct((n_seq, 1, LRU_WIDTH), F32)
    return pl.pallas_call(
        functools.partial(_lru_kernel, zero_state=h0 is None, t_len=t_len),
        grid=(n_seq // seq_per_step,),
        in_specs=[
            pl.BlockSpec((rows, LRU_WIDTH), lambda s: (tok_blk0 + s, 0)),
            pl.BlockSpec((rows, LRU_WIDTH), lambda s: (tok_blk0 + s, 0)),
            pl.BlockSpec((CONV_W, LRU_WIDTH), const2),
            pl.BlockSpec((1, LRU_WIDTH), const2),
            blocks, blocks, per_dir, per_dir, per_dir,
        ] + ([] if h0 is None else [state, state]),
        out_specs=[pl.BlockSpec((rows, LRU_WIDTH), lambda s: (s, 0)), state, state],
        out_shape=[jax.ShapeDtypeStruct((n_seq * t_len, LRU_WIDTH), BF16), state_shape, state_shape],
        scratch_shapes=[pltpu.VMEM((rows, LRU_WIDTH), F32)] * 4
        + [pltpu.VMEM((LRU_WIDTH // LRU_SUB, LRU_SUB, 4 * LRU_SUB), BF16)],
        compiler_params=_cparams(1),
        name="rglru",
    )(xb, gb, conv_w, conv_b.reshape(1, LRU_WIDTH),
      w_r.reshape(2 * LRU_BLOCKS, LRU_BLOCK, LRU_BLOCK), w_i.reshape(2 * LRU_BLOCKS, LRU_BLOCK, LRU_BLOCK),
      b_r, b_i, lam, *(() if h0 is None else h0))


def _fourier_kernel(x_ref, mod_ref, g_ref, cs_ref, ct_ref, w_ref, o_ref, w_bf_ref, *, mod_row0, t_len):
    @pl.when(pl.program_id(0) == 0)
    def _():
        w_bf_ref[...] = w_ref[...].astype(BF16)

    x = x_ref[...]
    row = mod_row0 + pl.program_id(0) if mod_row0 else 0
    h = _norm_mod(x, g_ref[...], _mod_vec(mod_ref, row, 3), _mod_vec(mod_ref, row, 4)).astype(BF16)
    cos_parts, sin_parts = [], []
    for g in range(FOURIER_GROUPS):
        ab = jnp.dot(h[:, g * GROUP_W:(g + 1) * GROUP_W], cs_ref[...], preferred_element_type=F32)
        cos_parts.append(ab[:, :GROUP_W])
        sin_parts.append(ab[:, GROUP_W:])
    cos_all = jnp.concatenate(cos_parts, axis=1).astype(BF16)
    sin_all = jnp.concatenate(sin_parts, axis=1).astype(BF16)
    f_parts = []
    for q in range(x.shape[0] // t_len):
        rows = slice(q * t_len, (q + 1) * t_len)
        stacked = jnp.concatenate([cos_all[rows], sin_all[rows]], axis=0)
        f_parts.append(jnp.dot(ct_ref[...], stacked, preferred_element_type=F32))
    f = jnp.concatenate(f_parts, axis=0) * ((t_len * GROUP_W) ** -0.5)
    y = jnp.dot(f.astype(BF16), w_bf_ref[...], preferred_element_type=F32)
    o_ref[...] = x + _mod_vec(mod_ref, row, 5) * y


def _dft_tables(t_len):
    def cos_sin(n):
        jk = np.outer(np.arange(n), np.arange(n)) % n
        ang = 2.0 * np.pi * jk.astype(np.float64) / n
        return np.cos(ang), np.sin(ang)

    cc, sc = cos_sin(GROUP_W)
    ct, st = cos_sin(t_len)
    chan = jnp.asarray(np.concatenate([cc, sc], axis=1).astype(np.float32)).astype(BF16)
    time = jnp.asarray(np.concatenate([ct, -st], axis=1).astype(np.float32)).astype(BF16)
    return chan, time


def _fourier(x, mod, g, w_out, n_seq, t_len, tok_blk0, mod_row0, seq_per_step):
    assert seq_per_step == 1 or mod_row0 == 0
    chan, time = _dft_tables(t_len)
    rows = seq_per_step * t_len
    seq = lambda s: (tok_blk0 + s, 0)
    const = lambda s: (0, 0)
    return pl.pallas_call(
        functools.partial(_fourier_kernel, mod_row0=mod_row0, t_len=t_len),
        grid=(n_seq // seq_per_step,),
        in_specs=[
            pl.BlockSpec((rows, D_MODEL), seq),
            _mod_spec(),
            pl.BlockSpec((1, D_MODEL), const),
            _resident((GROUP_W, 2 * GROUP_W), const),
            _resident((t_len, 2 * t_len), const),
            _resident((D_MODEL, D_MODEL), const),
        ],
        out_specs=pl.BlockSpec((rows, D_MODEL), seq),
        out_shape=jax.ShapeDtypeStruct((N_TOK, D_MODEL), F32),
        input_output_aliases={0: 0},
        scratch_shapes=[pltpu.VMEM((D_MODEL, D_MODEL), BF16)],
        compiler_params=_cparams(1),
        name="fourier_mixer",
    )(x, mod, g.reshape(1, D_MODEL), chan, time, w_out)


def kernel(x_prompt, x_sample, cache_k, cache_v, state_lru_fwd, state_lru_bwd, c, c_ctx, w_ada, b_ada, norm_g, ffn1_gate, ffn1_up, ffn1_down, ffn2_gate, ffn2_up, ffn2_down, w_in, q_norm_g, k_norm_g, rpb, conv_w, conv_b, lru_w_r, lru_b_r, lru_w_i, lru_b_i, lru_lambda, w_out_ab, w_out_c):
    assert DEPTH == 2, "one neighbourhood/RG-LRU layer followed by one Fourier layer"
    c_ctx2 = c_ctx.reshape(1, D_MODEL)
    mod0 = _adaln(c_ctx2, c, w_ada, b_ada, 0)

    ffn1 = (ffn1_gate, ffn1_up, ffn1_down)
    ffn2 = (ffn2_gate, ffn2_up, ffn2_down)

    x, mod1 = _ffn((x_prompt.reshape(N_CTX_TOK, D_MODEL), x_sample.reshape(N_SMP_TOK, D_MODEL)),
                   mod0, norm_g[0, 0], *ffn1, 0, 0, adaln_next=(c_ctx2, c, w_ada, b_ada, 1))
    q, k, v, xb, gb, new_k, new_v = _proj(x, mod0, norm_g[0, 1], w_in[0], q_norm_g[0], k_norm_g[0])
    o_ctx = _ctx_attn(q, k, v)
    o_smp = _na_attn(q, k, v,
                     cache_k[:, 0].reshape(DEC_BATCH, PAST_LEN, NA_WIDTH),
                     cache_v[:, 0].reshape(DEC_BATCH, PAST_LEN, NA_WIDTH), rpb[0])
    lru_prm = (conv_w[0], conv_b[0], lru_w_r[0], lru_b_r[0], lru_w_i[0], lru_b_i[0], lru_lambda[0])
    yb_ctx, new_hf, new_hb = _lru(xb, gb, *lru_prm, None, BATCH, SEQ, 0, TOKEN_TILE // SEQ)
    yb_smp, _, _ = _lru(xb, gb, *lru_prm, (state_lru_fwd, state_lru_bwd),
                        DEC_BATCH, DEC_SEQ, N_CTX_TOK // DEC_SEQ, 1)
    (x,) = _ffn((x,), mod0, norm_g[0, 2], *ffn2, 0, 6, mixer_out=(o_ctx, o_smp, yb_ctx, yb_smp, w_out_ab))

    (x,) = _ffn((x,), mod1, norm_g[1, 0], *ffn1, 1, 0)
    x = _fourier(x, mod1, norm_g[1, 1], w_out_c[0], BATCH, SEQ, 0, 0, TOKEN_TILE // SEQ)
    x = _fourier(x, mod1, norm_g[1, 1], w_out_c[0], DEC_BATCH, DEC_SEQ, N_CTX_TOK // DEC_SEQ, 1, 1)
    y_prompt, y_sample = _ffn((x,), mod1, norm_g[1, 2], *ffn2, 1, 6, split_out=True)

    return (y_prompt.reshape(BATCH, SEQ, D_MODEL), y_sample.reshape(DEC_BATCH, DEC_SEQ, D_MODEL),
            new_k.reshape(BATCH, 1, SEQ, NA_HEADS, HEAD_DIM), new_v.reshape(BATCH, 1, SEQ, NA_HEADS, HEAD_DIM),
            new_hf, new_hb)
```

```python
import functools

import numpy as np
import jax
import jax.numpy as jnp
from jax import lax
from jax.experimental import pallas as pl
from jax.experimental.pallas import tpu as pltpu

F32 = jnp.float32
BF16 = jnp.bfloat16

D_MODEL = 1024
BATCH = 16
SEQ = 256
DEPTH = 2
DEC_BATCH = 2
DEC_SEQ = 1024
PAST_LEN = 256
GRID_W = 64
HEAD_DIM = 64
NA_WIDTH = 512
NA_HEADS = 8
WIN_H = 8
WIN_W = 16
LRU_WIDTH = 512
LRU_BLOCKS = 8
LRU_BLOCK = 64
LRU_C = 8.0
LRU_SUB = 256
CONV_W = 4
FOURIER_GROUPS = 4
GROUP_W = D_MODEL // FOURIER_GROUPS
D_FF = 2816
N_MOD = 9
IN_WIDTH = 3 * NA_WIDTH + 2 * LRU_WIDTH
EPS = 1e-6

N_CTX_TOK = BATCH * SEQ
N_SMP_TOK = DEC_BATCH * DEC_SEQ
N_TOK = N_CTX_TOK + N_SMP_TOK
MOD_ROWS = 8
MOD_WIDTH = N_MOD * D_MODEL
ROWS = DEC_SEQ // GRID_W
KH = min(WIN_H, ROWS)

TOKEN_TILE = 512
N_CTX_TILES = N_CTX_TOK // TOKEN_TILE
CTX_SEQ_PER_STEP = 4
FFN_TILE = 512
FF_TILE = 256
FF_CHUNKS = D_FF // FF_TILE
FF_STAGE_SLOTS = 2
SUBLANES = 8
LANES = 128
VMEM_LIMIT = 56 * 1024 * 1024

NA_Q_ROWS = 4
NA_GROUPS = ROWS // NA_Q_ROWS
NA_K_ROWS = 12
NA_Q = NA_Q_ROWS * GRID_W
NA_K = NA_K_ROWS * GRID_W
N_DR = 2 * WIN_H - 1
N_DC = 2 * WIN_W - 1
N_DR_PAIRS = N_DR + 1


def _cparams(n_axes):
    return pltpu.CompilerParams(
        dimension_semantics=("arbitrary",) * n_axes, vmem_limit_bytes=VMEM_LIMIT)


def _resident(block_shape, index_map):
    return pl.BlockSpec(block_shape, index_map, pipeline_mode=pl.Buffered(1))


def _mod_spec():
    return _resident((MOD_ROWS, MOD_WIDTH), lambda i: (0, 0))


def _mod_row_of_tile(i, tile=TOKEN_TILE):
    n_ctx_tiles = N_CTX_TOK // tile
    tiles_per_seq = DEC_SEQ // tile
    return jnp.where(i < n_ctx_tiles, 0, 1 + (i - n_ctx_tiles) // tiles_per_seq)


def _mod_vec(mod_ref, row, k):
    return mod_ref[pl.ds(row, 1), k * D_MODEL:(k + 1) * D_MODEL]


def _norm_mod(x, g, shift, scale):
    ms = jnp.mean(x * x, axis=-1, keepdims=True)
    return (x * lax.rsqrt(ms + EPS)) * (g * (1.0 + scale)) + shift


def _adaln_slab(cctx_ref, c_ref, w_ref, b_ref, cond_ref, layer):
    cond_ref[...] = jnp.zeros_like(cond_ref)
    cond_ref[0:1, :] = cctx_ref[...]
    cond_ref[1:1 + DEC_BATCH, :] = c_ref[...]
    cond = cond_ref[...]
    s = (cond * jax.nn.sigmoid(cond)).astype(BF16)
    return jnp.dot(s, w_ref[...].astype(BF16), preferred_element_type=F32) + b_ref[layer:layer + 1, :]


def _adaln_specs(layer, slab):
    return ([pl.BlockSpec((1, D_MODEL), lambda i: (0, 0)),
             pl.BlockSpec((DEC_BATCH, D_MODEL), lambda i: (0, 0)),
             pl.BlockSpec((None, D_MODEL, slab), lambda i: (layer, 0, i)),
             pl.BlockSpec((DEPTH, slab), lambda i: (0, i))],
            pl.BlockSpec((MOD_ROWS, slab), lambda i: (0, i)))


def _adaln_kernel(cctx_ref, c_ref, w_ref, b_ref, o_ref, cond_ref, *, layer):
    o_ref[...] = _adaln_slab(cctx_ref, c_ref, w_ref, b_ref, cond_ref, layer)


def _adaln(c_ctx, c, w_ada, b_ada, layer):
    slab = MOD_WIDTH // 4
    in_specs, out_spec = _adaln_specs(layer, slab)
    return pl.pallas_call(
        functools.partial(_adaln_kernel, layer=layer),
        grid=(MOD_WIDTH // slab,),
        in_specs=in_specs,
        out_specs=out_spec,
        out_shape=jax.ShapeDtypeStruct((MOD_ROWS, MOD_WIDTH), F32),
        scratch_shapes=[pltpu.VMEM((MOD_ROWS, D_MODEL), F32)],
        compiler_params=_cparams(1),
        name="adaln",
    )(c_ctx, c, w_ada, b_ada)


def _ffn_weight_copy(w_hbm, stage_ref, sem_ref, layer, j, ff_axis):
    ff = pl.ds(j * FF_TILE, FF_TILE)
    src = w_hbm.at[layer, :, ff] if ff_axis == 1 else w_hbm.at[layer, ff, :]
    slot = j % FF_STAGE_SLOTS
    return pltpu.make_async_copy(src, stage_ref.at[slot], sem_ref.at[slot])


def _mixer_out_copy(w_hbm, stage_ref, sem_ref, j):
    rows = stage_ref.shape[1]
    slot = j % stage_ref.shape[0]
    return pltpu.make_async_copy(w_hbm.at[0, pl.ds(j * rows, rows), :], stage_ref.at[slot], sem_ref.at[slot])


def _ffn_kernel(*refs, layer, mod_base, split_in, split_out, mixer_out, adaln_next):
    refs = list(refs)
    take = lambda n: [refs.pop(0) for _ in range(n)]
    x_refs = take(2 if split_in else 1)
    mix_refs = take(4) if mixer_out else None
    mod_ref, g_ref = take(2)
    wo_hbm = take(1)[0] if mixer_out else None
    ada_refs = take(4) if adaln_next is not None else None
    wg_hbm, wu_hbm, wd_hbm = take(3)
    o_refs = take(2 if split_out else 1)
    modn_ref = take(1)[0] if adaln_next is not None else None
    wg_bf, wu_bf, wd_bf, stg_g, stg_u, stg_d, sem_g, sem_u, sem_d = take(9)
    if adaln_next is not None:
        modn_ref[...] = _adaln_slab(*ada_refs, take(1)[0], adaln_next)
    streams = ((wg_hbm, stg_g, sem_g, wg_bf, 1), (wu_hbm, stg_u, sem_u, wu_bf, 1),
               (wd_hbm, stg_d, sem_d, wd_bf, 0))

    i = pl.program_id(0)
    is_ctx = i < N_CTX_TOK // FFN_TILE
    if split_in:
        x = jnp.where(is_ctx, x_refs[0][...], x_refs[1][...])
    else:
        x = x_refs[0][...]
    row = _mod_row_of_tile(i, FFN_TILE)

    if mixer_out:
        wo_bf, stg_o, sem_o = take(3)
        rows = stg_o.shape[1]
        n_chunks = D_MODEL // rows

        @pl.when(i == 0)
        def _():
            for j in range(stg_o.shape[0]):
                _mixer_out_copy(wo_hbm, stg_o, sem_o, j).start()
            for j in range(n_chunks):
                _mixer_out_copy(wo_hbm, stg_o, sem_o, j).wait()
                wo_bf[j * rows:(j + 1) * rows, :] = stg_o[j % stg_o.shape[0]].astype(BF16)
                if j + stg_o.shape[0] < n_chunks:
                    _mixer_out_copy(wo_hbm, stg_o, sem_o, j + stg_o.shape[0]).start()

        oc_ref, os_ref, yc_ref, ys_ref = mix_refs
        cat = jnp.concatenate([jnp.where(is_ctx, oc_ref[...], os_ref[...]),
                               jnp.where(is_ctx, yc_ref[...], ys_ref[...])], axis=1)
        x = x + _mod_vec(mod_ref, row, mod_base - 1) * jnp.dot(cat, wo_bf[...], preferred_element_type=F32)

    h = _norm_mod(x, g_ref[...], _mod_vec(mod_ref, row, mod_base),
                  _mod_vec(mod_ref, row, mod_base + 1)).astype(BF16)

    def start_chunk(j):
        for w_hbm, stg, sem, _, ff_axis in streams:
            _ffn_weight_copy(w_hbm, stg, sem, layer, j, ff_axis).start()

    def finish_chunk(j):
        for w_hbm, stg, sem, w_bf, ff_axis in streams:
            _ffn_weight_copy(w_hbm, stg, sem, layer, j, ff_axis).wait()
            w_bf[j] = stg[j % FF_STAGE_SLOTS].astype(BF16)

    def run(stream_weights):
        if stream_weights:
            for j in range(FF_STAGE_SLOTS):
                start_chunk(j)
        acc = None
        for j in range(FF_CHUNKS):
            if stream_weights:
                finish_chunk(j)
                if j + FF_STAGE_SLOTS < FF_CHUNKS:
                    start_chunk(j + FF_STAGE_SLOTS)
            a = jnp.dot(h, wg_bf[j], preferred_element_type=F32)
            b = jnp.dot(h, wu_bf[j], preferred_element_type=F32)
            act = (a * jax.nn.sigmoid(a) * b).astype(BF16)
            y = jnp.dot(act, wd_bf[j], preferred_element_type=F32)
            acc = y if acc is None else acc + y
        res = x + 0.5 * _mod_vec(mod_ref, row, mod_base + 2) * acc
        if split_out:
            @pl.when(is_ctx)
            def _():
                o_refs[0][...] = res

            @pl.when(jnp.logical_not(is_ctx))
            def _():
                o_refs[1][...] = res
        else:
            o_refs[0][...] = res

    @pl.when(i == 0)
    def _():
        run(True)

    @pl.when(i > 0)
    def _():
        run(False)


def _ffn(xs, mod, g, wg, wu, wd, layer, mod_base, split_out=False, mixer_out=None, adaln_next=None):
    tm = FFN_TILE
    n_ctx_tiles = N_CTX_TOK // tm
    split_in = len(xs) == 2

    def tiles(width):
        return (pl.BlockSpec((tm, width), lambda i: (i, 0)),
                pl.BlockSpec((tm, width), lambda i: (jnp.minimum(i, n_ctx_tiles - 1), 0)),
                pl.BlockSpec((tm, width), lambda i: (jnp.maximum(i - n_ctx_tiles, 0), 0)))

    tok, ctx_tok, smp_tok = tiles(D_MODEL)
    full = jax.ShapeDtypeStruct((N_TOK, D_MODEL), F32)
    pair = [jax.ShapeDtypeStruct((N_CTX_TOK, D_MODEL), F32), jax.ShapeDtypeStruct((N_SMP_TOK, D_MODEL), F32)]
    hbm = pl.BlockSpec(memory_space=pl.ANY)
    in_specs = [ctx_tok, smp_tok] if split_in else [tok]
    operands = list(xs)
    scratch = [
        pltpu.VMEM((FF_CHUNKS, D_MODEL, FF_TILE), BF16),
        pltpu.VMEM((FF_CHUNKS, D_MODEL, FF_TILE), BF16),
        pltpu.VMEM((FF_CHUNKS, FF_TILE, D_MODEL), BF16),
        pltpu.VMEM((FF_STAGE_SLOTS, D_MODEL, FF_TILE), F32),
        pltpu.VMEM((FF_STAGE_SLOTS, D_MODEL, FF_TILE), F32),
        pltpu.VMEM((FF_STAGE_SLOTS, FF_TILE, D_MODEL), F32),
        pltpu.SemaphoreType.DMA((FF_STAGE_SLOTS,)),
        pltpu.SemaphoreType.DMA((FF_STAGE_SLOTS,)),
        pltpu.SemaphoreType.DMA((FF_STAGE_SLOTS,)),
    ]
    if mixer_out is not None:
        _, ctx_half, smp_half = tiles(NA_WIDTH)
        in_specs += [ctx_half, smp_half, ctx_half, smp_half]
        operands += list(mixer_out[:4])
    in_specs += [_mod_spec(), pl.BlockSpec((1, D_MODEL), lambda i: (0, 0))]
    operands += [mod, g.reshape(1, D_MODEL)]
    out_specs = [ctx_tok, smp_tok] if split_out else [tok]
    out_shape = pair if split_out else [full]
    if adaln_next is not None:
        ada_in, ada_out = _adaln_specs(adaln_next[4], MOD_WIDTH // (N_TOK // tm))
        in_specs += ada_in
        operands += list(adaln_next[:4])
        out_specs.append(ada_out)
        out_shape.append(jax.ShapeDtypeStruct((MOD_ROWS, MOD_WIDTH), F32))
        scratch.append(pltpu.VMEM((MOD_ROWS, D_MODEL), F32))
    if mixer_out is not None:
        in_specs.insert(len(in_specs) - (4 if adaln_next is not None else 0), hbm)
        operands.insert(len(operands) - (4 if adaln_next is not None else 0), mixer_out[4])
        scratch += [
            pltpu.VMEM((D_MODEL, D_MODEL), BF16),
            pltpu.VMEM((FF_STAGE_SLOTS, FF_TILE, D_MODEL), F32),
            pltpu.SemaphoreType.DMA((FF_STAGE_SLOTS,)),
        ]
    return pl.pallas_call(
        functools.partial(_ffn_kernel, layer=layer, mod_base=mod_base, split_in=split_in,
                          split_out=split_out, mixer_out=mixer_out is not None,
                          adaln_next=None if adaln_next is None else adaln_next[4]),
        grid=(N_TOK // tm,),
        in_specs=in_specs + [hbm, hbm, hbm],
        out_specs=out_specs,
        out_shape=out_shape,
        scratch_shapes=scratch,
        compiler_params=_cparams(1),
        name="ffn",
    )(*operands, wg, wu, wd)


def _head_rms_norm(z, g, ones_bd):
    z2 = z * z
    hi = z2.astype(BF16)
    lo = (z2 - hi.astype(F32)).astype(BF16)
    n = ones_bd.shape[0]
    parts = []
    for c in range(z.shape[1] // n):
        sl = slice(c * n, (c + 1) * n)
        parts.append(jnp.dot(hi[:, sl], ones_bd, preferred_element_type=F32)
                     + jnp.dot(lo[:, sl], ones_bd, preferred_element_type=F32))
    ss = jnp.concatenate(parts, axis=1)
    return z * lax.rsqrt(ss * (1.0 / HEAD_DIM) + EPS) * g


def _proj_kernel(x_ref, mod_ref, g_ref, w_ref, qg_ref, kg_ref, ones_ref,
                 q_ref, k_ref, v_ref, xb_ref, gb_ref, kout_ref, vout_ref, w_bf_ref):
    i = pl.program_id(0)

    @pl.when(i == 0)
    def _():
        w_bf_ref[...] = w_ref[...].astype(BF16)

    x = x_ref[...]
    row = _mod_row_of_tile(i)
    h = _norm_mod(x, g_ref[...], _mod_vec(mod_ref, row, 3), _mod_vec(mod_ref, row, 4)).astype(BF16)

    def proj(part):
        return jnp.dot(h, w_bf_ref[:, part * NA_WIDTH:(part + 1) * NA_WIDTH], preferred_element_type=F32)

    ones_bd = ones_ref[...]
    q = _head_rms_norm(proj(0), jnp.tile(qg_ref[...], (1, NA_HEADS)), ones_bd) * (HEAD_DIM ** -0.5)
    q_ref[...] = q.astype(BF16)
    k = _head_rms_norm(proj(1), jnp.tile(kg_ref[...], (1, NA_HEADS)), ones_bd)
    k_ref[...] = k.astype(BF16)
    v = proj(2)
    v_ref[...] = v.astype(BF16)
    xb_ref[...] = proj(3)
    gb_ref[...] = proj(4)

    @pl.when(i < N_CTX_TILES)
    def _():
        kout_ref[...] = k.reshape(TOKEN_TILE, NA_HEADS, HEAD_DIM)
        vout_ref[...] = v.reshape(TOKEN_TILE, NA_HEADS, HEAD_DIM)


def _proj(x, mod, g, w_in, q_g, k_g):
    tm = TOKEN_TILE
    head = np.arange(2 * LANES) // HEAD_DIM
    ones_bd = jnp.asarray((head[:, None] == head[None, :]).astype(np.float32), dtype=BF16)
    tok = lambda i: (i, 0)
    const = lambda i: (0, 0)
    act_f32 = jax.ShapeDtypeStruct((N_TOK, NA_WIDTH), F32)
    act_bf16 = jax.ShapeDtypeStruct((N_TOK, NA_WIDTH), BF16)
    cache = jax.ShapeDtypeStruct((N_CTX_TOK, NA_HEADS, HEAD_DIM), F32)
    cache_spec = pl.BlockSpec((tm, NA_HEADS, HEAD_DIM), lambda i: (jnp.minimum(i, N_CTX_TILES - 1), 0, 0))
    return pl.pallas_call(
        _proj_kernel,
        grid=(N_TOK // tm,),
        in_specs=[
            pl.BlockSpec((tm, D_MODEL), tok),
            _mod_spec(),
            pl.BlockSpec((1, D_MODEL), const),
            _resident((D_MODEL, IN_WIDTH), const),
            pl.BlockSpec((1, HEAD_DIM), const),
            pl.BlockSpec((1, HEAD_DIM), const),
            _resident((2 * LANES, 2 * LANES), const),
        ],
        out_specs=[pl.BlockSpec((tm, NA_WIDTH), tok)] * 5 + [cache_spec, cache_spec],
        out_shape=[act_bf16, act_bf16, act_bf16, act_f32, act_f32, cache, cache],
        scratch_shapes=[pltpu.VMEM((D_MODEL, IN_WIDTH), BF16)],
        compiler_params=_cparams(1),
        name="mixer_in_proj",
    )(x, mod, g.reshape(1, D_MODEL), w_in, q_g.reshape(1, HEAD_DIM), k_g.reshape(1, HEAD_DIM), ones_bd)


def _head_masks():
    lane = lax.broadcasted_iota(jnp.int32, (1, 2 * HEAD_DIM), 1)
    return [lane < HEAD_DIM, lane >= HEAD_DIM]


def _ctx_attn_kernel(q_ref, k_ref, v_ref, o_ref):
    masks = _head_masks()
    for b in range(q_ref.shape[0] // SEQ):
        rows = slice(b * SEQ, (b + 1) * SEQ)
        for p in range(NA_HEADS // 2):
            sl = slice(2 * HEAD_DIM * p, 2 * HEAD_DIM * (p + 1))
            q2 = q_ref[rows, sl]
            k2t = k_ref[rows, sl].T
            v2 = v_ref[rows, sl]
            out = None
            for e in range(2):
                qm = jnp.where(masks[e], q2, jnp.zeros_like(q2))
                s = jnp.dot(qm, k2t, preferred_element_type=F32)
                pe = jnp.exp(s - jnp.max(s, axis=-1, keepdims=True))
                den = jnp.sum(pe, axis=-1, keepdims=True)
                o = jnp.dot(pe.astype(BF16), v2, preferred_element_type=F32) / den
                out = o if out is None else jnp.where(masks[e], o, out)
            o_ref[rows, sl] = out.astype(BF16)


def _ctx_attn(q, k, v):
    blk = pl.BlockSpec((CTX_SEQ_PER_STEP * SEQ, NA_WIDTH), lambda b: (b, 0))
    return pl.pallas_call(
        _ctx_attn_kernel,
        grid=(BATCH // CTX_SEQ_PER_STEP,),
        in_specs=[blk, blk, blk],
        out_specs=blk,
        out_shape=jax.ShapeDtypeStruct((N_CTX_TOK, NA_WIDTH), BF16),
        compiler_params=_cparams(1),
        name="ctx_attention",
    )(q, k, v)


def _na_build_bias_table(rpb_ref, table_ref):
    qc = lax.broadcasted_iota(jnp.int32, (GRID_W, LANES), 0)
    lane = lax.broadcasted_iota(jnp.int32, (GRID_W, LANES), 1)
    kc = lane % GRID_W
    col_start = jnp.clip(qc - WIN_W // 2, 0, GRID_W - WIN_W)
    col_in = (kc >= col_start) & (kc < col_start + WIN_W)
    neg = jnp.full((GRID_W, LANES), -jnp.inf, F32)

    def toeplitz(h, dr, lane0):
        if dr < 0 or dr >= N_DR:
            return neg
        row = jnp.pad(rpb_ref[h, dr:dr + 1, :], ((0, 0), (0, LANES - N_DC)))
        w = jnp.broadcast_to(row, (GRID_W, LANES))
        return pltpu.roll(w, (lane0 - (WIN_W - 1)) % LANES, 1, stride=1, stride_axis=0)

    for h in range(NA_HEADS):
        for i in range(N_DR_PAIRS):
            t = jnp.where(lane < GRID_W, toeplitz(h, i - 1, 0), toeplitz(h, i, GRID_W))
            table_ref[h, i] = jnp.where(col_in, t, neg)


def _na_kernel(q_ref, k_ref, v_ref, kc_ref, vc_ref, rpb_ref, o_ref, table_ref):
    b = pl.program_id(0)
    g = pl.program_id(1)

    @pl.when((b == 0) & (g == 0))
    def _():
        _na_build_bias_table(rpb_ref, table_ref)

    win_row0 = jnp.where(g < NA_GROUPS // 2, 0, ROWS - NA_K_ROWS)
    start = pl.multiple_of(win_row0 * GRID_W, GRID_W)
    q_row = g * NA_Q_ROWS + lax.broadcasted_iota(jnp.int32, (NA_Q, 1), 0) // GRID_W
    k_row = win_row0 + lax.broadcasted_iota(jnp.int32, (1, NA_K), 1) // GRID_W
    row_start = jnp.clip(q_row - KH // 2, 0, ROWS - KH)
    row_in = (k_row >= row_start) & (k_row < row_start + KH)
    masks = _head_masks()
    for p in range(NA_HEADS // 2):
        sl = slice(2 * HEAD_DIM * p, 2 * HEAD_DIM * (p + 1))
        q2 = q_ref[:, sl]
        klt = k_ref[pl.ds(start, NA_K), sl].T
        vl = v_ref[pl.ds(start, NA_K), sl]
        kct = kc_ref[0, :, sl].astype(BF16).T
        vc = vc_ref[0, :, sl].astype(BF16)
        out = None
        for e in range(2):
            head = 2 * p + e
            bias_rows = []
            for a in range(NA_Q_ROWS):
                tiles = []
                for m in range(NA_K_ROWS // 2):
                    dr = win_row0 + 2 * m - (g * NA_Q_ROWS + a) + (WIN_H - 1)
                    tiles.append(table_ref[head, jnp.clip(dr + 1, 0, N_DR_PAIRS - 1)])
                bias_rows.append(jnp.concatenate(tiles, axis=1))
            bias = jnp.concatenate(bias_rows, axis=0)
            qm = jnp.where(masks[e], q2, jnp.zeros_like(q2))
            s_loc = jnp.where(row_in, jnp.dot(qm, klt, preferred_element_type=F32) + bias, -jnp.inf)
            s_ctx = jnp.dot(qm, kct, preferred_element_type=F32)
            m_max = jnp.maximum(jnp.max(s_loc, axis=-1, keepdims=True),
                                jnp.max(s_ctx, axis=-1, keepdims=True))
            p_loc = jnp.exp(s_loc - m_max)
            p_ctx = jnp.exp(s_ctx - m_max)
            den = jnp.sum(p_loc, axis=-1, keepdims=True) + jnp.sum(p_ctx, axis=-1, keepdims=True)
            o = (jnp.dot(p_loc.astype(BF16), vl, preferred_element_type=F32)
                 + jnp.dot(p_ctx.astype(BF16), vc, preferred_element_type=F32)) / den
            out = o if out is None else jnp.where(masks[e], o, out)
        o_ref[:, sl] = out.astype(BF16)


def _na_attn(q, k, v, k_ctx, v_ctx, rpb_e):
    smp_blk0 = N_CTX_TOK // DEC_SEQ
    q_blk0 = N_CTX_TOK // NA_Q
    kv = pl.BlockSpec((DEC_SEQ, NA_WIDTH), lambda b, g: (smp_blk0 + b, 0))
    ctx = pl.BlockSpec((1, PAST_LEN, NA_WIDTH), lambda b, g: (b, 0, 0))
    return pl.pallas_call(
        _na_kernel,
        grid=(DEC_BATCH, NA_GROUPS),
        in_specs=[
            pl.BlockSpec((NA_Q, NA_WIDTH), lambda b, g: (q_blk0 + b * NA_GROUPS + g, 0)),
            kv, kv, ctx, ctx,
            pl.BlockSpec((NA_HEADS, N_DR, N_DC), lambda b, g: (0, 0, 0)),
        ],
        out_specs=pl.BlockSpec((NA_Q, NA_WIDTH), lambda b, g: (b * NA_GROUPS + g, 0)),
        out_shape=jax.ShapeDtypeStruct((N_SMP_TOK, NA_WIDTH), BF16),
        scratch_shapes=[pltpu.VMEM((NA_HEADS, N_DR_PAIRS, GRID_W, LANES), F32)],
        compiler_params=_cparams(2),
        name="neighbourhood_attention",
    )(q, k, v, k_ctx, v_ctx, rpb_e)


def _gelu_tanh(x):
    c0 = float(np.sqrt(2.0 / np.pi))
    inner = x * (c0 + (c0 * 0.044715) * (x * x))
    return (0.5 * x) * (1.0 + jnp.tanh(inner))


def _lru_build_gate_weights(wr_ref, wi_ref, w_scr):
    blocks_per_group = LRU_SUB // LRU_BLOCK
    w_scr[...] = jnp.zeros_like(w_scr)
    for d in range(2):
        for kind, w_ref in enumerate((wr_ref, wi_ref)):
            col0 = (2 * d + kind) * LRU_SUB
            for blk in range(LRU_BLOCKS):
                c, n = divmod(blk, blocks_per_group)
                r0 = n * LRU_BLOCK
                w_scr[c, r0:r0 + LRU_BLOCK, col0 + r0:col0 + r0 + LRU_BLOCK] = (
                    0.5 * w_ref[d * LRU_BLOCKS + blk]).astype(BF16)


def _lru_kernel(*refs, zero_state, t_len):
    refs = list(refs)
    xb_ref, gb_ref, cw_ref, cb_ref, wr_ref, wi_ref, br_ref, bi_ref, lam_ref = refs[:9]
    h0f_ref, h0b_ref = (None, None) if zero_state else refs[9:11]
    y_ref, hlf_ref, hlb_ref, af_ref, uf_ref, ab_ref, ub_ref, w_scr = refs[-8:]
    width = xb_ref.shape[1]
    n_seq = xb_ref.shape[0] // t_len
    n_blk = t_len // SUBLANES
    row = lax.broadcasted_iota(jnp.int32, (t_len, 1), 0)
    in_block = lax.broadcasted_iota(jnp.int32, (1, SUBLANES, 1), 1)

    @pl.when(pl.program_id(0) == 0)
    def _():
        _lru_build_gate_weights(wr_ref, wi_ref, w_scr)

    def shifted(z, s):
        rolled = pltpu.roll(z, (-s) % t_len, axis=0)
        ok = (row + s >= 0) & (row + s < t_len)
        return jnp.where(ok, rolled, 0.0)

    left = (CONV_W - 1) // 2
    for q, c in [(q, c) for q in range(n_seq) for c in range(width // LRU_SUB)]:
        rows = slice(q * t_len, (q + 1) * t_len)
        cs = slice(c * LRU_SUB, (c + 1) * LRU_SUB)
        x = xb_ref[rows, cs]
        xc = cb_ref[:, cs]
        for j in range(CONV_W):
            tap = x if j == left else shifted(x, j - left)
            xc = xc + tap * cw_ref[j:j + 1, cs]
        half_gates = jnp.dot(xc.astype(BF16), w_scr[c], preferred_element_type=F32)
        half_xc = 0.5 * xc
        for d, (a_ref, u_ref) in enumerate(((af_ref, uf_ref), (ab_ref, ub_ref))):
            t_r = jnp.tanh(half_gates[:, (2 * d) * LRU_SUB:(2 * d + 1) * LRU_SUB] + 0.5 * br_ref[d:d + 1, cs])
            t_i = jnp.tanh(half_gates[:, (2 * d + 1) * LRU_SUB:(2 * d + 2) * LRU_SUB] + 0.5 * bi_ref[d:d + 1, cs])
            lam = lam_ref[d:d + 1, cs]
            log_sig = jnp.minimum(lam, 0.0) - jnp.log1p(jnp.exp(-jnp.abs(lam)))
            half_c_log_sig = (0.5 * LRU_C) * log_sig
            log_a = t_r * half_c_log_sig + half_c_log_sig
            a = jnp.exp(log_a)
            var = -jnp.tanh(log_a) * (a * a + 1.0)
            u = jnp.where(var > 0.0, var * lax.rsqrt(var), 0.0) * ((t_i + 1.0) * half_xc)
            a = a.reshape(n_blk, SUBLANES, LRU_SUB)
            u = u.reshape(n_blk, SUBLANES, LRU_SUB)
            step = 1
            while step < SUBLANES:
                if d == 0:
                    ok, shift = in_block >= step, step
                else:
                    ok, shift = in_block < SUBLANES - step, SUBLANES - step
                a_prev = jnp.where(ok, pltpu.roll(a, shift, axis=1), 1.0)
                u_prev = jnp.where(ok, pltpu.roll(u, shift, axis=1), 0.0)
                u = u + a * u_prev
                a = a * a_prev
                step *= 2
            a_ref[rows, cs] = a.reshape(t_len, LRU_SUB)
            u_ref[rows, cs] = u.reshape(t_len, LRU_SUB)

    def body(i, carry):
        new = []
        for q, (cf, cb) in enumerate(carry):
            f0 = pl.multiple_of(q * t_len + i * SUBLANES, SUBLANES)
            b0 = pl.multiple_of(q * t_len + (n_blk - 1 - i) * SUBLANES, SUBLANES)
            hf = uf_ref[pl.ds(f0, SUBLANES), :] + af_ref[pl.ds(f0, SUBLANES), :] * cf
            hb = ub_ref[pl.ds(b0, SUBLANES), :] + ab_ref[pl.ds(b0, SUBLANES), :] * cb
            uf_ref[pl.ds(f0, SUBLANES), :] = hf
            ub_ref[pl.ds(b0, SUBLANES), :] = hb
            new.append((jnp.broadcast_to(hf[SUBLANES - 1:SUBLANES, :], (SUBLANES, width)),
                        jnp.broadcast_to(hb[0:1, :], (SUBLANES, width))))
        return tuple(new)

    if zero_state:
        init = tuple((jnp.zeros((SUBLANES, width), F32),) * 2 for _ in range(n_seq))
    else:
        init = tuple((jnp.broadcast_to(h0f_ref[q], (SUBLANES, width)),
                      jnp.broadcast_to(h0b_ref[q], (SUBLANES, width))) for q in range(n_seq))
    last = lax.fori_loop(0, n_blk, body, init)
    for q, (cf, cb) in enumerate(last):
        hlf_ref[q] = cf[0:1, :]
        hlb_ref[q] = cb[0:1, :]
    y_ref[...] = ((uf_ref[...] + ub_ref[...]) * _gelu_tanh(gb_ref[...])).astype(BF16)


def _lru(xb, gb, conv_w, conv_b, w_r, b_r, w_i, b_i, lam, h0, n_seq, t_len, tok_blk0, seq_per_step):
    rows = seq_per_step * t_len
    const2 = lambda s: (0, 0)
    const3 = lambda s: (0, 0, 0)
    blocks = pl.BlockSpec((2 * LRU_BLOCKS, LRU_BLOCK, LRU_BLOCK), const3)
    per_dir = pl.BlockSpec((2, LRU_WIDTH), const2)
    state = pl.BlockSpec((seq_per_step, 1, LRU_WIDTH), lambda s: (s, 0, 0))
    state_shape = jax.ShapeDtypeStruct((n_seq, 1, LRU_WIDTH), F32)
    return pl.pallas_call(
        functools.partial(_lru_kernel, zero_state=h0 is None, t_len=t_len),
        grid=(n_seq // seq_per_step,),
        in_specs=[
            pl.BlockSpec((rows, LRU_WIDTH), lambda s: (tok_blk0 + s, 0)),
            pl.BlockSpec((rows, LRU_WIDTH), lambda s: (tok_blk0 + s, 0)),
            pl.BlockSpec((CONV_W, LRU_WIDTH), const2),
            pl.BlockSpec((1, LRU_WIDTH), const2),
            blocks, blocks, per_dir, per_dir, per_dir,
        ] + ([] if h0 is None else [state, state]),
        out_specs=[pl.BlockSpec((rows, LRU_WIDTH), lambda s: (s, 0)), state, state],
        out_shape=[jax.ShapeDtypeStruct((n_seq * t_len, LRU_WIDTH), BF16), state_shape, state_shape],
        scratch_shapes=[pltpu.VMEM((rows, LRU_WIDTH), F32)] * 4
        + [pltpu.VMEM((LRU_WIDTH // LRU_SUB, LRU_SUB, 4 * LRU_SUB), BF16)],
        compiler_params=_cparams(1),
        name="rglru",
    )(xb, gb, conv_w, conv_b.reshape(1, LRU_WIDTH),
      w_r.reshape(2 * LRU_BLOCKS, LRU_BLOCK, LRU_BLOCK), w_i.reshape(2 * LRU_BLOCKS, LRU_BLOCK, LRU_BLOCK),
      b_r, b_i, lam, *(() if h0 is None else h0))


def _fourier_kernel(x_ref, mod_ref, g_ref, cs_ref, ct_ref, w_ref, o_ref, w_bf_ref, *, mod_row0, t_len):
    @pl.when(pl.program_id(0) == 0)
    def _():
        w_bf_ref[...] = w_ref[...].astype(BF16)

    x = x_ref[...]
    row = mod_row0 + pl.program_id(0) if mod_row0 else 0
    h = _norm_mod(x, g_ref[...], _mod_vec(mod_ref, row, 3), _mod_vec(mod_ref, row, 4)).astype(BF16)
    cos_parts, sin_parts = [], []
    for g in range(FOURIER_GROUPS):
        ab = jnp.dot(h[:, g * GROUP_W:(g + 1) * GROUP_W], cs_ref[...], preferred_element_type=F32)
        cos_parts.append(ab[:, :GROUP_W])
        sin_parts.append(ab[:, GROUP_W:])
    cos_all = jnp.concatenate(cos_parts, axis=1).astype(BF16)
    sin_all = jnp.concatenate(sin_parts, axis=1).astype(BF16)
    f_parts = []
    for q in range(x.shape[0] // t_len):
        rows = slice(q * t_len, (q + 1) * t_len)
        stacked = jnp.concatenate([cos_all[rows], sin_all[rows]], axis=0)
        f_parts.append(jnp.dot(ct_ref[...], stacked, preferred_element_type=F32))
    f = jnp.concatenate(f_parts, axis=0) * ((t_len * GROUP_W) ** -0.5)
    y = jnp.dot(f.astype(BF16), w_bf_ref[...], preferred_element_type=F32)
    o_ref[...] = x + _mod_vec(mod_ref, row, 5) * y


def _dft_tables(t_len):
    def cos_sin(n):
        jk = np.outer(np.arange(n), np.arange(n)) % n
        ang = 2.0 * np.pi * jk.astype(np.float64) / n
        return np.cos(ang), np.sin(ang)

    cc, sc = cos_sin(GROUP_W)
    ct, st = cos_sin(t_len)
    chan = jnp.asarray(np.concatenate([cc, sc], axis=1).astype(np.float32)).astype(BF16)
    time = jnp.asarray(np.concatenate([ct, -st], axis=1).astype(np.float32)).astype(BF16)
    return chan, time


def _fourier(x, mod, g, w_out, n_seq, t_len, tok_blk0, mod_row0, seq_per_step):
    assert seq_per_step == 1 or mod_row0 == 0
    chan, time = _dft_tables(t_len)
    rows = seq_per_step * t_len
    seq = lambda s: (tok_blk0 + s, 0)
    const = lambda s: (0, 0)
    return pl.pallas_call(
        functools.partial(_fourier_kernel, mod_row0=mod_row0, t_len=t_len),
        grid=(n_seq // seq_per_step,),
        in_specs=[
            pl.BlockSpec((rows, D_MODEL), seq),
            _mod_spec(),
            pl.BlockSpec((1, D_MODEL), const),
            _resident((GROUP_W, 2 * GROUP_W), const),
            _resident((t_len, 2 * t_len), const),
            _resident((D_MODEL, D_MODEL), const),
        ],
        out_specs=pl.BlockSpec((rows, D_MODEL), seq),
        out_shape=jax.ShapeDtypeStruct((N_TOK, D_MODEL), F32),
        input_output_aliases={0: 0},
        scratch_shapes=[pltpu.VMEM((D_MODEL, D_MODEL), BF16)],
        compiler_params=_cparams(1),
        name="fourier_mixer",
    )(x, mod, g.reshape(1, D_MODEL), chan, time, w_out)


def kernel(x_prompt, x_sample, cache_k, cache_v, state_lru_fwd, state_lru_bwd, c, c_ctx, w_ada, b_ada, norm_g, ffn1_gate, ffn1_up, ffn1_down, ffn2_gate, ffn2_up, ffn2_down, w_in, q_norm_g, k_norm_g, rpb, conv_w, conv_b, lru_w_r, lru_b_r, lru_w_i, lru_b_i, lru_lambda, w_out_ab, w_out_c):
    assert DEPTH == 2, "one neighbourhood/RG-LRU layer followed by one Fourier layer"
    c_ctx2 = c_ctx.reshape(1, D_MODEL)
    mod0 = _adaln(c_ctx2, c, w_ada, b_ada, 0)

    ffn1 = (ffn1_gate, ffn1_up, ffn1_down)
    ffn2 = (ffn2_gate, ffn2_up, ffn2_down)

    x, mod1 = _ffn((x_prompt.reshape(N_CTX_TOK, D_MODEL), x_sample.reshape(N_SMP_TOK, D_MODEL)),
                   mod0, norm_g[0, 0], *ffn1, 0, 0, adaln_next=(c_ctx2, c, w_ada, b_ada, 1))
    q, k, v, xb, gb, new_k, new_v = _proj(x, mod0, norm_g[0, 1], w_in[0], q_norm_g[0], k_norm_g[0])
    o_ctx = _ctx_attn(q, k, v)
    o_smp = _na_attn(q, k, v,
                     cache_k[:, 0].reshape(DEC_BATCH, PAST_LEN, NA_WIDTH),
                     cache_v[:, 0].reshape(DEC_BATCH, PAST_LEN, NA_WIDTH), rpb[0])
    lru_prm = (conv_w[0], conv_b[0], lru_w_r[0], lru_b_r[0], lru_w_i[0], lru_b_i[0], lru_lambda[0])
    yb_ctx, new_hf, new_hb = _lru(xb, gb, *lru_prm, None, BATCH, SEQ, 0, CTX_SEQ_PER_STEP)
    yb_smp, _, _ = _lru(xb, gb, *lru_prm, (state_lru_fwd, state_lru_bwd),
                        DEC_BATCH, DEC_SEQ, N_CTX_TOK // DEC_SEQ, 1)
    (x,) = _ffn((x,), mod0, norm_g[0, 2], *ffn2, 0, 6, mixer_out=(o_ctx, o_smp, yb_ctx, yb_smp, w_out_ab))

    (x,) = _ffn((x,), mod1, norm_g[1, 0], *ffn1, 1, 0)
    x = _fourier(x, mod1, norm_g[1, 1], w_out_c[0], BATCH, SEQ, 0, 0, CTX_SEQ_PER_STEP)
    x = _fourier(x, mod1, norm_g[1, 1], w_out_c[0], DEC_BATCH, DEC_SEQ, N_CTX_TOK // DEC_SEQ, 1, 1)
    y_prompt, y_sample = _ffn((x,), mod1, norm_g[1, 2], *ffn2, 1, 6, split_out=True)

    return (y_prompt.reshape(BATCH, SEQ, D_MODEL), y_sample.reshape(DEC_BATCH, DEC_SEQ, D_MODEL),
            new_k.reshape(BATCH, 1, SEQ, NA_HEADS, HEAD_DIM), new_v.reshape(BATCH, 1, SEQ, NA_HEADS, HEAD_DIM),
            new_hf, new_hb)
```

```python
import functools

import numpy as np
import jax
import jax.numpy as jnp
from jax import lax
from jax.experimental import pallas as pl
from jax.experimental.pallas import tpu as pltpu

F32 = jnp.float32
BF16 = jnp.bfloat16

D_MODEL = 1024
BATCH = 16
SEQ = 256
DEPTH = 2
DEC_BATCH = 2
DEC_SEQ = 1024
PAST_LEN = 256
GRID_W = 64
HEAD_DIM = 64
NA_WIDTH = 512
NA_HEADS = 8
WIN_H = 8
WIN_W = 16
LRU_WIDTH = 512
LRU_BLOCKS = 8
LRU_BLOCK = 64
LRU_C = 8.0
LRU_SUB = 256
CONV_W = 4
FOURIER_GROUPS = 4
GROUP_W = D_MODEL // FOURIER_GROUPS
D_FF = 2816
N_MOD = 9
IN_WIDTH = 3 * NA_WIDTH + 2 * LRU_WIDTH
EPS = 1e-6

N_CTX_TOK = BATCH * SEQ
N_SMP_TOK = DEC_BATCH * DEC_SEQ
N_TOK = N_CTX_TOK + N_SMP_TOK
MOD_ROWS = 8
MOD_WIDTH = N_MOD * D_MODEL
ROWS = DEC_SEQ // GRID_W
KH = min(WIN_H, ROWS)

TOKEN_TILE = 512
N_CTX_TILES = N_CTX_TOK // TOKEN_TILE
CTX_SEQ_PER_STEP = 2
CTX_ATTN_SEQ_PER_STEP = 4
FFN_TILE = 512
FF_TILE = 256
FF_CHUNKS = D_FF // FF_TILE
FF_STAGE_SLOTS = 2
SUBLANES = 8
LANES = 128
VMEM_LIMIT = 56 * 1024 * 1024

NA_Q_ROWS = 4
NA_GROUPS = ROWS // NA_Q_ROWS
NA_K_ROWS = 12
NA_Q = NA_Q_ROWS * GRID_W
NA_K = NA_K_ROWS * GRID_W
N_DR = 2 * WIN_H - 1
N_DC = 2 * WIN_W - 1
N_DR_PAIRS = N_DR + 1


def _cparams(n_axes):
    return pltpu.CompilerParams(
        dimension_semantics=("arbitrary",) * n_axes, vmem_limit_bytes=VMEM_LIMIT)


def _resident(block_shape, index_map):
    return pl.BlockSpec(block_shape, index_map, pipeline_mode=pl.Buffered(1))


def _mod_spec():
    return _resident((MOD_ROWS, MOD_WIDTH), lambda i: (0, 0))


def _mod_row_of_tile(i, tile=TOKEN_TILE):
    n_ctx_tiles = N_CTX_TOK // tile
    tiles_per_seq = DEC_SEQ // tile
    return jnp.where(i < n_ctx_tiles, 0, 1 + (i - n_ctx_tiles) // tiles_per_seq)


def _mod_vec(mod_ref, row, k):
    return mod_ref[pl.ds(row, 1), k * D_MODEL:(k + 1) * D_MODEL]


def _norm_mod(x, g, shift, scale):
    ms = jnp.mean(x * x, axis=-1, keepdims=True)
    return (x * lax.rsqrt(ms + EPS)) * (g * (1.0 + scale)) + shift


def _adaln_slab(cctx_ref, c_ref, w_ref, b_ref, cond_ref, layer):
    cond_ref[...] = jnp.zeros_like(cond_ref)
    cond_ref[0:1, :] = cctx_ref[...]
    cond_ref[1:1 + DEC_BATCH, :] = c_ref[...]
    cond = cond_ref[...]
    s = (cond * jax.nn.sigmoid(cond)).astype(BF16)
    return jnp.dot(s, w_ref[...].astype(BF16), preferred_element_type=F32) + b_ref[layer:layer + 1, :]


def _adaln_specs(layer, slab):
    return ([pl.BlockSpec((1, D_MODEL), lambda i: (0, 0)),
             pl.BlockSpec((DEC_BATCH, D_MODEL), lambda i: (0, 0)),
             pl.BlockSpec((None, D_MODEL, slab), lambda i: (layer, 0, i)),
             pl.BlockSpec((DEPTH, slab), lambda i: (0, i))],
            pl.BlockSpec((MOD_ROWS, slab), lambda i: (0, i)))


def _adaln_kernel(cctx_ref, c_ref, w_ref, b_ref, o_ref, cond_ref, *, layer):
    o_ref[...] = _adaln_slab(cctx_ref, c_ref, w_ref, b_ref, cond_ref, layer)


def _adaln(c_ctx, c, w_ada, b_ada, layer):
    slab = MOD_WIDTH // 4
    in_specs, out_spec = _adaln_specs(layer, slab)
    return pl.pallas_call(
        functools.partial(_adaln_kernel, layer=layer),
        grid=(MOD_WIDTH // slab,),
        in_specs=in_specs,
        out_specs=out_spec,
        out_shape=jax.ShapeDtypeStruct((MOD_ROWS, MOD_WIDTH), F32),
        scratch_shapes=[pltpu.VMEM((MOD_ROWS, D_MODEL), F32)],
        compiler_params=_cparams(1),
        name="adaln",
    )(c_ctx, c, w_ada, b_ada)


def _ffn_weight_copy(w_hbm, stage_ref, sem_ref, layer, j, ff_axis):
    ff = pl.ds(j * FF_TILE, FF_TILE)
    src = w_hbm.at[layer, :, ff] if ff_axis == 1 else w_hbm.at[layer, ff, :]
    slot = j % FF_STAGE_SLOTS
    return pltpu.make_async_copy(src, stage_ref.at[slot], sem_ref.at[slot])


def _mixer_out_copy(w_hbm, stage_ref, sem_ref, j):
    rows = stage_ref.shape[1]
    slot = j % stage_ref.shape[0]
    return pltpu.make_async_copy(w_hbm.at[0, pl.ds(j * rows, rows), :], stage_ref.at[slot], sem_ref.at[slot])


def _ffn_kernel(*refs, layer, mod_base, split_in, split_out, mixer_out, adaln_next):
    refs = list(refs)
    take = lambda n: [refs.pop(0) for _ in range(n)]
    x_refs = take(2 if split_in else 1)
    mix_refs = take(4) if mixer_out else None
    mod_ref, g_ref = take(2)
    wo_hbm = take(1)[0] if mixer_out else None
    ada_refs = take(4) if adaln_next is not None else None
    wg_hbm, wu_hbm, wd_hbm = take(3)
    o_refs = take(2 if split_out else 1)
    modn_ref = take(1)[0] if adaln_next is not None else None
    wg_bf, wu_bf, wd_bf, stg_g, stg_u, stg_d, sem_g, sem_u, sem_d = take(9)
    if adaln_next is not None:
        modn_ref[...] = _adaln_slab(*ada_refs, take(1)[0], adaln_next)
    streams = ((wg_hbm, stg_g, sem_g, wg_bf, 1), (wu_hbm, stg_u, sem_u, wu_bf, 1),
               (wd_hbm, stg_d, sem_d, wd_bf, 0))

    i = pl.program_id(0)
    is_ctx = i < N_CTX_TOK // FFN_TILE
    if split_in:
        x = jnp.where(is_ctx, x_refs[0][...], x_refs[1][...])
    else:
        x = x_refs[0][...]
    row = _mod_row_of_tile(i, FFN_TILE)

    if mixer_out:
        wo_bf, stg_o, sem_o = take(3)
        rows = stg_o.shape[1]
        n_chunks = D_MODEL // rows

        @pl.when(i == 0)
        def _():
            for j in range(stg_o.shape[0]):
                _mixer_out_copy(wo_hbm, stg_o, sem_o, j).start()
            for j in range(n_chunks):
                _mixer_out_copy(wo_hbm, stg_o, sem_o, j).wait()
                wo_bf[j * rows:(j + 1) * rows, :] = stg_o[j % stg_o.shape[0]].astype(BF16)
                if j + stg_o.shape[0] < n_chunks:
                    _mixer_out_copy(wo_hbm, stg_o, sem_o, j + stg_o.shape[0]).start()

        oc_ref, os_ref, yc_ref, ys_ref = mix_refs
        cat = jnp.concatenate([jnp.where(is_ctx, oc_ref[...], os_ref[...]),
                               jnp.where(is_ctx, yc_ref[...], ys_ref[...])], axis=1)
        x = x + _mod_vec(mod_ref, row, mod_base - 1) * jnp.dot(cat, wo_bf[...], preferred_element_type=F32)

    h = _norm_mod(x, g_ref[...], _mod_vec(mod_ref, row, mod_base),
                  _mod_vec(mod_ref, row, mod_base + 1)).astype(BF16)

    def start_chunk(j):
        for w_hbm, stg, sem, _, ff_axis in streams:
            _ffn_weight_copy(w_hbm, stg, sem, layer, j, ff_axis).start()

    def finish_chunk(j):
        for w_hbm, stg, sem, w_bf, ff_axis in streams:
            _ffn_weight_copy(w_hbm, stg, sem, layer, j, ff_axis).wait()
            w_bf[j] = stg[j % FF_STAGE_SLOTS].astype(BF16)

    def run(stream_weights):
        if stream_weights:
            for j in range(FF_STAGE_SLOTS):
                start_chunk(j)
        acc = None
        for j in range(FF_CHUNKS):
            if stream_weights:
                finish_chunk(j)
                if j + FF_STAGE_SLOTS < FF_CHUNKS:
                    start_chunk(j + FF_STAGE_SLOTS)
            a = jnp.dot(h, wg_bf[j], preferred_element_type=F32)
            b = jnp.dot(h, wu_bf[j], preferred_element_type=F32)
            act = (a * jax.nn.sigmoid(a) * b).astype(BF16)
            y = jnp.dot(act, wd_bf[j], preferred_element_type=F32)
            acc = y if acc is None else acc + y
        res = x + 0.5 * _mod_vec(mod_ref, row, mod_base + 2) * acc
        if split_out:
            @pl.when(is_ctx)
            def _():
                o_refs[0][...] = res

            @pl.when(jnp.logical_not(is_ctx))
            def _():
                o_refs[1][...] = res
        else:
            o_refs[0][...] = res

    @pl.when(i == 0)
    def _():
        run(True)

    @pl.when(i > 0)
    def _():
        run(False)


def _ffn(xs, mod, g, wg, wu, wd, layer, mod_base, split_out=False, mixer_out=None, adaln_next=None):
    tm = FFN_TILE
    n_ctx_tiles = N_CTX_TOK // tm
    split_in = len(xs) == 2

    def tiles(width):
        return (pl.BlockSpec((tm, width), lambda i: (i, 0)),
                pl.BlockSpec((tm, width), lambda i: (jnp.minimum(i, n_ctx_tiles - 1), 0)),
                pl.BlockSpec((tm, width), lambda i: (jnp.maximum(i - n_ctx_tiles, 0), 0)))

    tok, ctx_tok, smp_tok = tiles(D_MODEL)
    full = jax.ShapeDtypeStruct((N_TOK, D_MODEL), F32)
    pair = [jax.ShapeDtypeStruct((N_CTX_TOK, D_MODEL), F32), jax.ShapeDtypeStruct((N_SMP_TOK, D_MODEL), F32)]
    hbm = pl.BlockSpec(memory_space=pl.ANY)
    in_specs = [ctx_tok, smp_tok] if split_in else [tok]
    operands = list(xs)
    scratch = [
        pltpu.VMEM((FF_CHUNKS, D_MODEL, FF_TILE), BF16),
        pltpu.VMEM((FF_CHUNKS, D_MODEL, FF_TILE), BF16),
        pltpu.VMEM((FF_CHUNKS, FF_TILE, D_MODEL), BF16),
        pltpu.VMEM((FF_STAGE_SLOTS, D_MODEL, FF_TILE), F32),
        pltpu.VMEM((FF_STAGE_SLOTS, D_MODEL, FF_TILE), F32),
        pltpu.VMEM((FF_STAGE_SLOTS, FF_TILE, D_MODEL), F32),
        pltpu.SemaphoreType.DMA((FF_STAGE_SLOTS,)),
        pltpu.SemaphoreType.DMA((FF_STAGE_SLOTS,)),
        pltpu.SemaphoreType.DMA((FF_STAGE_SLOTS,)),
    ]
    if mixer_out is not None:
        _, ctx_half, smp_half = tiles(NA_WIDTH)
        in_specs += [ctx_half, smp_half, ctx_half, smp_half]
        operands += list(mixer_out[:4])
    in_specs += [_mod_spec(), pl.BlockSpec((1, D_MODEL), lambda i: (0, 0))]
    operands += [mod, g.reshape(1, D_MODEL)]
    out_specs = [ctx_tok, smp_tok] if split_out else [tok]
    out_shape = pair if split_out else [full]
    if adaln_next is not None:
        ada_in, ada_out = _adaln_specs(adaln_next[4], MOD_WIDTH // (N_TOK // tm))
        in_specs += ada_in
        operands += list(adaln_next[:4])
        out_specs.append(ada_out)
        out_shape.append(jax.ShapeDtypeStruct((MOD_ROWS, MOD_WIDTH), F32))
        scratch.append(pltpu.VMEM((MOD_ROWS, D_MODEL), F32))
    if mixer_out is not None:
        in_specs.insert(len(in_specs) - (4 if adaln_next is not None else 0), hbm)
        operands.insert(len(operands) - (4 if adaln_next is not None else 0), mixer_out[4])
        scratch += [
            pltpu.VMEM((D_MODEL, D_MODEL), BF16),
            pltpu.VMEM((FF_STAGE_SLOTS, FF_TILE, D_MODEL), F32),
            pltpu.SemaphoreType.DMA((FF_STAGE_SLOTS,)),
        ]
    return pl.pallas_call(
        functools.partial(_ffn_kernel, layer=layer, mod_base=mod_base, split_in=split_in,
                          split_out=split_out, mixer_out=mixer_out is not None,
                          adaln_next=None if adaln_next is None else adaln_next[4]),
        grid=(N_TOK // tm,),
        in_specs=in_specs + [hbm, hbm, hbm],
        out_specs=out_specs,
        out_shape=out_shape,
        scratch_shapes=scratch,
        compiler_params=_cparams(1),
        name="ffn",
    )(*operands, wg, wu, wd)


def _head_rms_norm(z, g, ones_bd):
    z2 = z * z
    hi = z2.astype(BF16)
    lo = (z2 - hi.astype(F32)).astype(BF16)
    n = ones_bd.shape[0]
    parts = []
    for c in range(z.shape[1] // n):
        sl = slice(c * n, (c + 1) * n)
        parts.append(jnp.dot(hi[:, sl], ones_bd, preferred_element_type=F32)
                     + jnp.dot(lo[:, sl], ones_bd, preferred_element_type=F32))
    ss = jnp.concatenate(parts, axis=1)
    return z * lax.rsqrt(ss * (1.0 / HEAD_DIM) + EPS) * g


def _proj_kernel(x_ref, mod_ref, g_ref, w_ref, qg_ref, kg_ref, ones_ref,
                 q_ref, k_ref, v_ref, xb_ref, gb_ref, kout_ref, vout_ref, w_bf_ref):
    i = pl.program_id(0)

    @pl.when(i == 0)
    def _():
        w_bf_ref[...] = w_ref[...].astype(BF16)

    x = x_ref[...]
    row = _mod_row_of_tile(i)
    h = _norm_mod(x, g_ref[...], _mod_vec(mod_ref, row, 3), _mod_vec(mod_ref, row, 4)).astype(BF16)

    def proj(part):
        return jnp.dot(h, w_bf_ref[:, part * NA_WIDTH:(part + 1) * NA_WIDTH], preferred_element_type=F32)

    ones_bd = ones_ref[...]
    q = _head_rms_norm(proj(0), jnp.tile(qg_ref[...], (1, NA_HEADS)), ones_bd) * (HEAD_DIM ** -0.5)
    q_ref[...] = q.astype(BF16)
    k = _head_rms_norm(proj(1), jnp.tile(kg_ref[...], (1, NA_HEADS)), ones_bd)
    k_ref[...] = k.astype(BF16)
    v = proj(2)
    v_ref[...] = v.astype(BF16)
    xb_ref[...] = proj(3)
    gb_ref[...] = proj(4)

    @pl.when(i < N_CTX_TILES)
    def _():
        kout_ref[...] = k.reshape(TOKEN_TILE, NA_HEADS, HEAD_DIM)
        vout_ref[...] = v.reshape(TOKEN_TILE, NA_HEADS, HEAD_DIM)


def _proj(x, mod, g, w_in, q_g, k_g):
    tm = TOKEN_TILE
    head = np.arange(2 * LANES) // HEAD_DIM
    ones_bd = jnp.asarray((head[:, None] == head[None, :]).astype(np.float32), dtype=BF16)
    tok = lambda i: (i, 0)
    const = lambda i: (0, 0)
    act_f32 = jax.ShapeDtypeStruct((N_TOK, NA_WIDTH), F32)
    act_bf16 = jax.ShapeDtypeStruct((N_TOK, NA_WIDTH), BF16)
    cache = jax.ShapeDtypeStruct((N_CTX_TOK, NA_HEADS, HEAD_DIM), F32)
    cache_spec = pl.BlockSpec((tm, NA_HEADS, HEAD_DIM), lambda i: (jnp.minimum(i, N_CTX_TILES - 1), 0, 0))
    return pl.pallas_call(
        _proj_kernel,
        grid=(N_TOK // tm,),
        in_specs=[
            pl.BlockSpec((tm, D_MODEL), tok),
            _mod_spec(),
            pl.BlockSpec((1, D_MODEL), const),
            _resident((D_MODEL, IN_WIDTH), const),
            pl.BlockSpec((1, HEAD_DIM), const),
            pl.BlockSpec((1, HEAD_DIM), const),
            _resident((2 * LANES, 2 * LANES), const),
        ],
        out_specs=[pl.BlockSpec((tm, NA_WIDTH), tok)] * 5 + [cache_spec, cache_spec],
        out_shape=[act_bf16, act_bf16, act_bf16, act_f32, act_f32, cache, cache],
        scratch_shapes=[pltpu.VMEM((D_MODEL, IN_WIDTH), BF16)],
        compiler_params=_cparams(1),
        name="mixer_in_proj",
    )(x, mod, g.reshape(1, D_MODEL), w_in, q_g.reshape(1, HEAD_DIM), k_g.reshape(1, HEAD_DIM), ones_bd)


def _head_masks():
    lane = lax.broadcasted_iota(jnp.int32, (1, 2 * HEAD_DIM), 1)
    return [lane < HEAD_DIM, lane >= HEAD_DIM]


def _ctx_attn_kernel(q_ref, k_ref, v_ref, o_ref):
    masks = _head_masks()
    for b in range(q_ref.shape[0] // SEQ):
        rows = slice(b * SEQ, (b + 1) * SEQ)
        for p in range(NA_HEADS // 2):
            sl = slice(2 * HEAD_DIM * p, 2 * HEAD_DIM * (p + 1))
            q2 = q_ref[rows, sl]
            k2t = k_ref[rows, sl].T
            v2 = v_ref[rows, sl]
            out = None
            for e in range(2):
                qm = jnp.where(masks[e], q2, jnp.zeros_like(q2))
                s = jnp.dot(qm, k2t, preferred_element_type=F32)
                pe = jnp.exp(s - jnp.max(s, axis=-1, keepdims=True))
                den = jnp.sum(pe, axis=-1, keepdims=True)
                o = jnp.dot(pe.astype(BF16), v2, preferred_element_type=F32) / den
                out = o if out is None else jnp.where(masks[e], o, out)
            o_ref[rows, sl] = out.astype(BF16)


def _ctx_attn(q, k, v):
    blk = pl.BlockSpec((CTX_ATTN_SEQ_PER_STEP * SEQ, NA_WIDTH), lambda b: (b, 0))
    return pl.pallas_call(
        _ctx_attn_kernel,
        grid=(BATCH // CTX_ATTN_SEQ_PER_STEP,),
        in_specs=[blk, blk, blk],
        out_specs=blk,
        out_shape=jax.ShapeDtypeStruct((N_CTX_TOK, NA_WIDTH), BF16),
        compiler_params=_cparams(1),
        name="ctx_attention",
    )(q, k, v)


def _na_build_bias_table(rpb_ref, table_ref):
    qc = lax.broadcasted_iota(jnp.int32, (GRID_W, LANES), 0)
    lane = lax.broadcasted_iota(jnp.int32, (GRID_W, LANES), 1)
    kc = lane % GRID_W
    col_start = jnp.clip(qc - WIN_W // 2, 0, GRID_W - WIN_W)
    col_in = (kc >= col_start) & (kc < col_start + WIN_W)
    neg = jnp.full((GRID_W, LANES), -jnp.inf, F32)

    def toeplitz(h, dr, lane0):
        if dr < 0 or dr >= N_DR:
            return neg
        row = jnp.pad(rpb_ref[h, dr:dr + 1, :], ((0, 0), (0, LANES - N_DC)))
        w = jnp.broadcast_to(row, (GRID_W, LANES))
        return pltpu.roll(w, (lane0 - (WIN_W - 1)) % LANES, 1, stride=1, stride_axis=0)

    for h in range(NA_HEADS):
        for i in range(N_DR_PAIRS):
            t = jnp.where(lane < GRID_W, toeplitz(h, i - 1, 0), toeplitz(h, i, GRID_W))
            table_ref[h, i] = jnp.where(col_in, t, neg)


def _na_kernel(q_ref, k_ref, v_ref, kc_ref, vc_ref, rpb_ref, o_ref, table_ref):
    b = pl.program_id(0)
    g = pl.program_id(1)

    @pl.when((b == 0) & (g == 0))
    def _():
        _na_build_bias_table(rpb_ref, table_ref)

    win_row0 = jnp.where(g < NA_GROUPS // 2, 0, ROWS - NA_K_ROWS)
    start = pl.multiple_of(win_row0 * GRID_W, GRID_W)
    q_row = g * NA_Q_ROWS + lax.broadcasted_iota(jnp.int32, (NA_Q, 1), 0) // GRID_W
    k_row = win_row0 + lax.broadcasted_iota(jnp.int32, (1, NA_K), 1) // GRID_W
    row_start = jnp.clip(q_row - KH // 2, 0, ROWS - KH)
    row_in = (k_row >= row_start) & (k_row < row_start + KH)
    masks = _head_masks()
    for p in range(NA_HEADS // 2):
        sl = slice(2 * HEAD_DIM * p, 2 * HEAD_DIM * (p + 1))
        q2 = q_ref[:, sl]
        klt = k_ref[pl.ds(start, NA_K), sl].T
        vl = v_ref[pl.ds(start, NA_K), sl]
        kct = kc_ref[0, :, sl].astype(BF16).T
        vc = vc_ref[0, :, sl].astype(BF16)
        out = None
        for e in range(2):
            head = 2 * p + e
            bias_rows = []
            for a in range(NA_Q_ROWS):
                tiles = []
                for m in range(NA_K_ROWS // 2):
                    dr = win_row0 + 2 * m - (g * NA_Q_ROWS + a) + (WIN_H - 1)
                    tiles.append(table_ref[head, jnp.clip(dr + 1, 0, N_DR_PAIRS - 1)])
                bias_rows.append(jnp.concatenate(tiles, axis=1))
            bias = jnp.concatenate(bias_rows, axis=0)
            qm = jnp.where(masks[e], q2, jnp.zeros_like(q2))
            s_loc = jnp.where(row_in, jnp.dot(qm, klt, preferred_element_type=F32) + bias, -jnp.inf)
            s_ctx = jnp.dot(qm, kct, preferred_element_type=F32)
            m_max = jnp.maximum(jnp.max(s_loc, axis=-1, keepdims=True),
                                jnp.max(s_ctx, axis=-1, keepdims=True))
            p_loc = jnp.exp(s_loc - m_max)
            p_ctx = jnp.exp(s_ctx - m_max)
            den = jnp.sum(p_loc, axis=-1, keepdims=True) + jnp.sum(p_ctx, axis=-1, keepdims=True)
            o = (jnp.dot(p_loc.astype(BF16), vl, preferred_element_type=F32)
                 + jnp.dot(p_ctx.astype(BF16), vc, preferred_element_type=F32)) / den
            out = o if out is None else jnp.where(masks[e], o, out)
        o_ref[:, sl] = out.astype(BF16)


def _na_attn(q, k, v, k_ctx, v_ctx, rpb_e):
    smp_blk0 = N_CTX_TOK // DEC_SEQ
    q_blk0 = N_CTX_TOK // NA_Q
    kv = pl.BlockSpec((DEC_SEQ, NA_WIDTH), lambda b, g: (smp_blk0 + b, 0))
    ctx = pl.BlockSpec((1, PAST_LEN, NA_WIDTH), lambda b, g: (b, 0, 0))
    return pl.pallas_call(
        _na_kernel,
        grid=(DEC_BATCH, NA_GROUPS),
        in_specs=[
            pl.BlockSpec((NA_Q, NA_WIDTH), lambda b, g: (q_blk0 + b * NA_GROUPS + g, 0)),
            kv, kv, ctx, ctx,
            pl.BlockSpec((NA_HEADS, N_DR, N_DC), lambda b, g: (0, 0, 0)),
        ],
        out_specs=pl.BlockSpec((NA_Q, NA_WIDTH), lambda b, g: (b * NA_GROUPS + g, 0)),
        out_shape=jax.ShapeDtypeStruct((N_SMP_TOK, NA_WIDTH), BF16),
        scratch_shapes=[pltpu.VMEM((NA_HEADS, N_DR_PAIRS, GRID_W, LANES), F32)],
        compiler_params=_cparams(2),
        name="neighbourhood_attention",
    )(q, k, v, k_ctx, v_ctx, rpb_e)


def _gelu_tanh(x):
    c0 = float(np.sqrt(2.0 / np.pi))
    inner = x * (c0 + (c0 * 0.044715) * (x * x))
    return (0.5 * x) * (1.0 + jnp.tanh(inner))


def _lru_build_gate_weights(wr_ref, wi_ref, w_scr):
    blocks_per_group = LRU_SUB // LRU_BLOCK
    w_scr[...] = jnp.zeros_like(w_scr)
    for d in range(2):
        for kind, w_ref in enumerate((wr_ref, wi_ref)):
            col0 = (2 * d + kind) * LRU_SUB
            for blk in range(LRU_BLOCKS):
                c, n = divmod(blk, blocks_per_group)
                r0 = n * LRU_BLOCK
                w_scr[c, r0:r0 + LRU_BLOCK, col0 + r0:col0 + r0 + LRU_BLOCK] = (
                    0.5 * w_ref[d * LRU_BLOCKS + blk]).astype(BF16)


def _lru_kernel(*refs, zero_state, t_len):
    refs = list(refs)
    xb_ref, gb_ref, cw_ref, cb_ref, wr_ref, wi_ref, br_ref, bi_ref, lam_ref = refs[:9]
    h0f_ref, h0b_ref = (None, None) if zero_state else refs[9:11]
    y_ref, hlf_ref, hlb_ref, af_ref, uf_ref, ab_ref, ub_ref, w_scr = refs[-8:]
    width = xb_ref.shape[1]
    n_seq = xb_ref.shape[0] // t_len
    n_blk = t_len // SUBLANES
    row = lax.broadcasted_iota(jnp.int32, (t_len, 1), 0)
    in_block = lax.broadcasted_iota(jnp.int32, (1, SUBLANES, 1), 1)

    @pl.when(pl.program_id(0) == 0)
    def _():
        _lru_build_gate_weights(wr_ref, wi_ref, w_scr)

    def shifted(z, s):
        rolled = pltpu.roll(z, (-s) % t_len, axis=0)
        ok = (row + s >= 0) & (row + s < t_len)
        return jnp.where(ok, rolled, 0.0)

    left = (CONV_W - 1) // 2
    for q, c in [(q, c) for q in range(n_seq) for c in range(width // LRU_SUB)]:
        rows = slice(q * t_len, (q + 1) * t_len)
        cs = slice(c * LRU_SUB, (c + 1) * LRU_SUB)
        x = xb_ref[rows, cs]
        xc = cb_ref[:, cs]
        for j in range(CONV_W):
            tap = x if j == left else shifted(x, j - left)
            xc = xc + tap * cw_ref[j:j + 1, cs]
        half_gates = jnp.dot(xc.astype(BF16), w_scr[c], preferred_element_type=F32)
        half_xc = 0.5 * xc
        for d, (a_ref, u_ref) in enumerate(((af_ref, uf_ref), (ab_ref, ub_ref))):
            t_r = jnp.tanh(half_gates[:, (2 * d) * LRU_SUB:(2 * d + 1) * LRU_SUB] + 0.5 * br_ref[d:d + 1, cs])
            t_i = jnp.tanh(half_gates[:, (2 * d + 1) * LRU_SUB:(2 * d + 2) * LRU_SUB] + 0.5 * bi_ref[d:d + 1, cs])
            lam = lam_ref[d:d + 1, cs]
            log_sig = jnp.minimum(lam, 0.0) - jnp.log1p(jnp.exp(-jnp.abs(lam)))
            half_c_log_sig = (0.5 * LRU_C) * log_sig
            log_a = t_r * half_c_log_sig + half_c_log_sig
            a = jnp.exp(log_a)
            var = -jnp.tanh(log_a) * (a * a + 1.0)
            u = jnp.where(var > 0.0, var * lax.rsqrt(var), 0.0) * ((t_i + 1.0) * half_xc)
            a = a.reshape(n_blk, SUBLANES, LRU_SUB)
            u = u.reshape(n_blk, SUBLANES, LRU_SUB)
            step = 1
            while step < SUBLANES:
                if d == 0:
                    ok, shift = in_block >= step, step
                else:
                    ok, shift = in_block < SUBLANES - step, SUBLANES - step
                a_prev = jnp.where(ok, pltpu.roll(a, shift, axis=1), 1.0)
                u_prev = jnp.where(ok, pltpu.roll(u, shift, axis=1), 0.0)
                u = u + a * u_prev
                a = a * a_prev
                step *= 2
            a_ref[rows, cs] = a.reshape(t_len, LRU_SUB)
            u_ref[rows, cs] = u.reshape(t_len, LRU_SUB)

    def body(i, carry):
        new = []
        for q, (cf, cb) in enumerate(carry):
            f0 = pl.multiple_of(q * t_len + i * SUBLANES, SUBLANES)
            b0 = pl.multiple_of(q * t_len + (n_blk - 1 - i) * SUBLANES, SUBLANES)
            hf = uf_ref[pl.ds(f0, SUBLANES), :] + af_ref[pl.ds(f0, SUBLANES), :] * cf
            hb = ub_ref[pl.ds(b0, SUBLANES), :] + ab_ref[pl.ds(b0, SUBLANES), :] * cb
            uf_ref[pl.ds(f0, SUBLANES), :] = hf
            ub_ref[pl.ds(b0, SUBLANES), :] = hb
            new.append((jnp.broadcast_to(hf[SUBLANES - 1:SUBLANES, :], (SUBLANES, width)),
                        jnp.broadcast_to(hb[0:1, :], (SUBLANES, width))))
        return tuple(new)

    if zero_state:
        init = tuple((jnp.zeros((SUBLANES, width), F32),) * 2 for _ in range(n_seq))
    else:
        init = tuple((jnp.broadcast_to(h0f_ref[q], (SUBLANES, width)),
                      jnp.broadcast_to(h0b_ref[q], (SUBLANES, width))) for q in range(n_seq))
    last = lax.fori_loop(0, n_blk, body, init)
    for q, (cf, cb) in enumerate(last):
        hlf_ref[q] = cf[0:1, :]
        hlb_ref[q] = cb[0:1, :]
    y_ref[...] = ((uf_ref[...] + ub_ref[...]) * _gelu_tanh(gb_ref[...])).astype(BF16)


def _lru(xb, gb, conv_w, conv_b, w_r, b_r, w_i, b_i, lam, h0, n_seq, t_len, tok_blk0, seq_per_step):
    rows = seq_per_step * t_len
    const2 = lambda s: (0, 0)
    const3 = lambda s: (0, 0, 0)
    blocks = pl.BlockSpec((2 * LRU_BLOCKS, LRU_BLOCK, LRU_BLOCK), const3)
    per_dir = pl.BlockSpec((2, LRU_WIDTH), const2)
    state = pl.BlockSpec((seq_per_step, 1, LRU_WIDTH), lambda s: (s, 0, 0))
    state_shape = jax.ShapeDtypeStruct((n_seq, 1, LRU_WIDTH), F32)
    return pl.pallas_call(
        functools.partial(_lru_kernel, zero_state=h0 is None, t_len=t_len),
        grid=(n_seq // seq_per_step,),
        in_specs=[
            pl.BlockSpec((rows, LRU_WIDTH), lambda s: (tok_blk0 + s, 0)),
            pl.BlockSpec((rows, LRU_WIDTH), lambda s: (tok_blk0 + s, 0)),
            pl.BlockSpec((CONV_W, LRU_WIDTH), const2),
            pl.BlockSpec((1, LRU_WIDTH), const2),
            blocks, blocks, per_dir, per_dir, per_dir,
        ] + ([] if h0 is None else [state, state]),
        out_specs=[pl.BlockSpec((rows, LRU_WIDTH), lambda s: (s, 0)), state, state],
        out_shape=[jax.ShapeDtypeStruct((n_seq * t_len, LRU_WIDTH), BF16), state_shape, state_shape],
        scratch_shapes=[pltpu.VMEM((rows, LRU_WIDTH), F32)] * 4
        + [pltpu.VMEM((LRU_WIDTH // LRU_SUB, LRU_SUB, 4 * LRU_SUB), BF16)],
        compiler_params=_cparams(1),
        name="rglru",
    )(xb, gb, conv_w, conv_b.reshape(1, LRU_WIDTH),
      w_r.reshape(2 * LRU_BLOCKS, LRU_BLOCK, LRU_BLOCK), w_i.reshape(2 * LRU_BLOCKS, LRU_BLOCK, LRU_BLOCK),
      b_r, b_i, lam, *(() if h0 is None else h0))


def _fourier_kernel(x_ref, mod_ref, g_ref, cs_ref, ct_ref, w_ref, o_ref, w_bf_ref, *, mod_row0, t_len):
    @pl.when(pl.program_id(0) == 0)
    def _():
        w_bf_ref[...] = w_ref[...].astype(BF16)

    x = x_ref[...]
    row = mod_row0 + pl.program_id(0) if mod_row0 else 0
    h = _norm_mod(x, g_ref[...], _mod_vec(mod_ref, row, 3), _mod_vec(mod_ref, row, 4)).astype(BF16)
    cos_parts, sin_parts = [], []
    for g in range(FOURIER_GROUPS):
        ab = jnp.dot(h[:, g * GROUP_W:(g + 1) * GROUP_W], cs_ref[...], preferred_element_type=F32)
        cos_parts.append(ab[:, :GROUP_W])
        sin_parts.append(ab[:, GROUP_W:])
    cos_all = jnp.concatenate(cos_parts, axis=1).astype(BF16)
    sin_all = jnp.concatenate(sin_parts, axis=1).astype(BF16)
    f_parts = []
    for q in range(x.shape[0] // t_len):
        rows = slice(q * t_len, (q + 1) * t_len)
        stacked = jnp.concatenate([cos_all[rows], sin_all[rows]], axis=0)
        f_parts.append(jnp.dot(ct_ref[...], stacked, preferred_element_type=F32))
    f = jnp.concatenate(f_parts, axis=0) * ((t_len * GROUP_W) ** -0.5)
    y = jnp.dot(f.astype(BF16), w_bf_ref[...], preferred_element_type=F32)
    o_ref[...] = x + _mod_vec(mod_ref, row, 5) * y


def _dft_tables(t_len):
    def cos_sin(n):
        jk = np.outer(np.arange(n), np.arange(n)) % n
        ang = 2.0 * np.pi * jk.astype(np.float64) / n
        return np.cos(ang), np.sin(ang)

    cc, sc = cos_sin(GROUP_W)
    ct, st = cos_sin(t_len)
    chan = jnp.asarray(np.concatenate([cc, sc], axis=1).astype(np.float32)).astype(BF16)
    time = jnp.asarray(np.concatenate([ct, -st], axis=1).astype(np.float32)).astype(BF16)
    return chan, time


def _fourier(x, mod, g, w_out, n_seq, t_len, tok_blk0, mod_row0, seq_per_step):
    assert seq_per_step == 1 or mod_row0 == 0
    chan, time = _dft_tables(t_len)
    rows = seq_per_step * t_len
    seq = lambda s: (tok_blk0 + s, 0)
    const = lambda s: (0, 0)
    return pl.pallas_call(
        functools.partial(_fourier_kernel, mod_row0=mod_row0, t_len=t_len),
        grid=(n_seq // seq_per_step,),
        in_specs=[
            pl.BlockSpec((rows, D_MODEL), seq),
            _mod_spec(),
            pl.BlockSpec((1, D_MODEL), const),
            _resident((GROUP_W, 2 * GROUP_W), const),
            _resident((t_len, 2 * t_len), const),
            _resident((D_MODEL, D_MODEL), const),
        ],
        out_specs=pl.BlockSpec((rows, D_MODEL), seq),
        out_shape=jax.ShapeDtypeStruct((N_TOK, D_MODEL), F32),
        input_output_aliases={0: 0},
        scratch_shapes=[pltpu.VMEM((D_MODEL, D_MODEL), BF16)],
        compiler_params=_cparams(1),
        name="fourier_mixer",
    )(x, mod, g.reshape(1, D_MODEL), chan, time, w_out)


def kernel(x_prompt, x_sample, cache_k, cache_v, state_lru_fwd, state_lru_bwd, c, c_ctx, w_ada, b_ada, norm_g, ffn1_gate, ffn1_up, ffn1_down, ffn2_gate, ffn2_up, ffn2_down, w_in, q_norm_g, k_norm_g, rpb, conv_w, conv_b, lru_w_r, lru_b_r, lru_w_i, lru_b_i, lru_lambda, w_out_ab, w_out_c):
    assert DEPTH == 2, "one neighbourhood/RG-LRU layer followed by one Fourier layer"
    c_ctx2 = c_ctx.reshape(1, D_MODEL)
    mod0 = _adaln(c_ctx2, c, w_ada, b_ada, 0)

    ffn1 = (ffn1_gate, ffn1_up, ffn1_down)
    ffn2 = (ffn2_gate, ffn2_up, ffn2_down)

    x, mod1 = _ffn((x_prompt.reshape(N_CTX_TOK, D_MODEL), x_sample.reshape(N_SMP_TOK, D_MODEL)),
                   mod0, norm_g[0, 0], *ffn1, 0, 0, adaln_next=(c_ctx2, c, w_ada, b_ada, 1))
    q, k, v, xb, gb, new_k, new_v = _proj(x, mod0, norm_g[0, 1], w_in[0], q_norm_g[0], k_norm_g[0])
    o_ctx = _ctx_attn(q, k, v)
    o_smp = _na_attn(q, k, v,
                     cache_k[:, 0].reshape(DEC_BATCH, PAST_LEN, NA_WIDTH),
                     cache_v[:, 0].reshape(DEC_BATCH, PAST_LEN, NA_WIDTH), rpb[0])
    lru_prm = (conv_w[0], conv_b[0], lru_w_r[0], lru_b_r[0], lru_w_i[0], lru_b_i[0], lru_lambda[0])
    yb_ctx, new_hf, new_hb = _lru(xb, gb, *lru_prm, None, BATCH, SEQ, 0, CTX_SEQ_PER_STEP)
    yb_smp, _, _ = _lru(xb, gb, *lru_prm, (state_lru_fwd, state_lru_bwd),
                        DEC_BATCH, DEC_SEQ, N_CTX_TOK // DEC_SEQ, 1)
    (x,) = _ffn((x,), mod0, norm_g[0, 2], *ffn2, 0, 6, mixer_out=(o_ctx, o_smp, yb_ctx, yb_smp, w_out_ab))

    (x,) = _ffn((x,), mod1, norm_g[1, 0], *ffn1, 1, 0)
    x = _fourier(x, mod1, norm_g[1, 1], w_out_c[0], BATCH, SEQ, 0, 0, CTX_SEQ_PER_STEP)
    x = _fourier(x, mod1, norm_g[1, 1], w_out_c[0], DEC_BATCH, DEC_SEQ, N_CTX_TOK // DEC_SEQ, 1, 1)
    y_prompt, y_sample = _ffn((x,), mod1, norm_g[1, 2], *ffn2, 1, 6, split_out=True)

    return (y_prompt.reshape(BATCH, SEQ, D_MODEL), y_sample.reshape(DEC_BATCH, DEC_SEQ, D_MODEL),
            new_k.reshape(BATCH, 1, SEQ, NA_HEADS, HEAD_DIM), new_v.reshape(BATCH, 1, SEQ, NA_HEADS, HEAD_DIM),
            new_hf, new_hb)
```

```python
import functools

import numpy as np
import jax
import jax.numpy as jnp
from jax import lax
from jax.experimental import pallas as pl
from jax.experimental.pallas import tpu as pltpu

F32 = jnp.float32
BF16 = jnp.bfloat16

D_MODEL = 1024
BATCH = 16
SEQ = 256
DEPTH = 2
DEC_BATCH = 2
DEC_SEQ = 1024
PAST_LEN = 256
GRID_W = 64
HEAD_DIM = 64
NA_WIDTH = 512
NA_HEADS = 8
WIN_H = 8
WIN_W = 16
LRU_WIDTH = 512
LRU_BLOCKS = 8
LRU_BLOCK = 64
LRU_C = 8.0
LRU_SUB = 256
CONV_W = 4
FOURIER_GROUPS = 4
GROUP_W = D_MODEL // FOURIER_GROUPS
D_FF = 2816
N_MOD = 9
IN_WIDTH = 3 * NA_WIDTH + 2 * LRU_WIDTH
EPS = 1e-6

N_CTX_TOK = BATCH * SEQ
N_SMP_TOK = DEC_BATCH * DEC_SEQ
N_TOK = N_CTX_TOK + N_SMP_TOK
MOD_ROWS = 8
MOD_WIDTH = N_MOD * D_MODEL
ROWS = DEC_SEQ // GRID_W
KH = min(WIN_H, ROWS)

TOKEN_TILE = 512
N_CTX_TILES = N_CTX_TOK // TOKEN_TILE
CTX_SEQ_PER_STEP = 2
CTX_ATTN_SEQ_PER_STEP = 4
FFN_TILE = 512
FF_TILE = 256
FF_CHUNKS = D_FF // FF_TILE
FF_STAGE_SLOTS = 2
SUBLANES = 8
LANES = 128
VMEM_LIMIT = 56 * 1024 * 1024

NA_Q_ROWS = 4
NA_GROUPS = ROWS // NA_Q_ROWS
NA_K_ROWS = 12
NA_Q = NA_Q_ROWS * GRID_W
NA_K = NA_K_ROWS * GRID_W
N_DR = 2 * WIN_H - 1
N_DC = 2 * WIN_W - 1
N_DR_PAIRS = N_DR + 1


def _cparams(n_axes):
    return pltpu.CompilerParams(
        dimension_semantics=("arbitrary",) * n_axes, vmem_limit_bytes=VMEM_LIMIT)


def _resident(block_shape, index_map):
    return pl.BlockSpec(block_shape, index_map, pipeline_mode=pl.Buffered(1))


def _mod_spec():
    return _resident((MOD_ROWS, MOD_WIDTH), lambda i: (0, 0))


def _mod_row_of_tile(i, tile=TOKEN_TILE):
    n_ctx_tiles = N_CTX_TOK // tile
    tiles_per_seq = DEC_SEQ // tile
    return jnp.where(i < n_ctx_tiles, 0, 1 + (i - n_ctx_tiles) // tiles_per_seq)


def _mod_vec(mod_ref, row, k):
    return mod_ref[pl.ds(row, 1), k * D_MODEL:(k + 1) * D_MODEL]


def _norm_mod(x, g, shift, scale):
    ms = jnp.mean(x * x, axis=-1, keepdims=True)
    return (x * lax.rsqrt(ms + EPS)) * (g * (1.0 + scale)) + shift


def _adaln_slab(cctx_ref, c_ref, w_ref, b_ref, cond_ref, layer):
    cond_ref[...] = jnp.zeros_like(cond_ref)
    cond_ref[0:1, :] = cctx_ref[...]
    cond_ref[1:1 + DEC_BATCH, :] = c_ref[...]
    cond = cond_ref[...]
    s = (cond * jax.nn.sigmoid(cond)).astype(BF16)
    return jnp.dot(s, w_ref[...].astype(BF16), preferred_element_type=F32) + b_ref[layer:layer + 1, :]


def _adaln_specs(layer, slab):
    return ([pl.BlockSpec((1, D_MODEL), lambda i: (0, 0)),
             pl.BlockSpec((DEC_BATCH, D_MODEL), lambda i: (0, 0)),
             pl.BlockSpec((None, D_MODEL, slab), lambda i: (layer, 0, i)),
             pl.BlockSpec((DEPTH, slab), lambda i: (0, i))],
            pl.BlockSpec((MOD_ROWS, slab), lambda i: (0, i)))


def _adaln_kernel(cctx_ref, c_ref, w_ref, b_ref, o_ref, cond_ref, *, layer):
    o_ref[...] = _adaln_slab(cctx_ref, c_ref, w_ref, b_ref, cond_ref, layer)


def _adaln(c_ctx, c, w_ada, b_ada, layer):
    slab = MOD_WIDTH // 4
    in_specs, out_spec = _adaln_specs(layer, slab)
    return pl.pallas_call(
        functools.partial(_adaln_kernel, layer=layer),
        grid=(MOD_WIDTH // slab,),
        in_specs=in_specs,
        out_specs=out_spec,
        out_shape=jax.ShapeDtypeStruct((MOD_ROWS, MOD_WIDTH), F32),
        scratch_shapes=[pltpu.VMEM((MOD_ROWS, D_MODEL), F32)],
        compiler_params=_cparams(1),
        name="adaln",
    )(c_ctx, c, w_ada, b_ada)


def _ffn_weight_copy(w_hbm, stage_ref, sem_ref, layer, j, ff_axis):
    ff = pl.ds(j * FF_TILE, FF_TILE)
    src = w_hbm.at[layer, :, ff] if ff_axis == 1 else w_hbm.at[layer, ff, :]
    slot = j % FF_STAGE_SLOTS
    return pltpu.make_async_copy(src, stage_ref.at[slot], sem_ref.at[slot])


def _mixer_out_copy(w_hbm, stage_ref, sem_ref, j):
    rows = stage_ref.shape[1]
    slot = j % stage_ref.shape[0]
    return pltpu.make_async_copy(w_hbm.at[0, pl.ds(j * rows, rows), :], stage_ref.at[slot], sem_ref.at[slot])


def _ffn_kernel(*refs, layer, mod_base, split_in, split_out, mixer_out, adaln_next):
    refs = list(refs)
    take = lambda n: [refs.pop(0) for _ in range(n)]
    x_refs = take(2 if split_in else 1)
    mix_refs = take(4) if mixer_out else None
    mod_ref, g_ref = take(2)
    wo_hbm = take(1)[0] if mixer_out else None
    ada_refs = take(4) if adaln_next is not None else None
    wg_hbm, wu_hbm, wd_hbm = take(3)
    o_refs = take(2 if split_out else 1)
    modn_ref = take(1)[0] if adaln_next is not None else None
    wg_bf, wu_bf, wd_bf, stg_g, stg_u, stg_d, sem_g, sem_u, sem_d = take(9)
    if adaln_next is not None:
        modn_ref[...] = _adaln_slab(*ada_refs, take(1)[0], adaln_next)
    streams = ((wg_hbm, stg_g, sem_g, wg_bf, 1), (wu_hbm, stg_u, sem_u, wu_bf, 1),
               (wd_hbm, stg_d, sem_d, wd_bf, 0))

    i = pl.program_id(0)
    is_ctx = i < N_CTX_TOK // FFN_TILE
    if split_in:
        x = jnp.where(is_ctx, x_refs[0][...], x_refs[1][...])
    else:
        x = x_refs[0][...]
    row = _mod_row_of_tile(i, FFN_TILE)

    if mixer_out:
        wo_bf, stg_o, sem_o = take(3)
        rows = stg_o.shape[1]
        n_chunks = D_MODEL // rows

        @pl.when(i == 0)
        def _():
            for j in range(stg_o.shape[0]):
                _mixer_out_copy(wo_hbm, stg_o, sem_o, j).start()
            for j in range(n_chunks):
                _mixer_out_copy(wo_hbm, stg_o, sem_o, j).wait()
                wo_bf[j * rows:(j + 1) * rows, :] = stg_o[j % stg_o.shape[0]].astype(BF16)
                if j + stg_o.shape[0] < n_chunks:
                    _mixer_out_copy(wo_hbm, stg_o, sem_o, j + stg_o.shape[0]).start()

        oc_ref, os_ref, yc_ref, ys_ref = mix_refs
        cat = jnp.concatenate([jnp.where(is_ctx, oc_ref[...], os_ref[...]),
                               jnp.where(is_ctx, yc_ref[...], ys_ref[...])], axis=1)
        x = x + _mod_vec(mod_ref, row, mod_base - 1) * jnp.dot(cat, wo_bf[...], preferred_element_type=F32)

    h = _norm_mod(x, g_ref[...], _mod_vec(mod_ref, row, mod_base),
                  _mod_vec(mod_ref, row, mod_base + 1)).astype(BF16)

    def start_chunk(j):
        for w_hbm, stg, sem, _, ff_axis in streams:
            _ffn_weight_copy(w_hbm, stg, sem, layer, j, ff_axis).start()

    def finish_chunk(j):
        for w_hbm, stg, sem, w_bf, ff_axis in streams:
            _ffn_weight_copy(w_hbm, stg, sem, layer, j, ff_axis).wait()
            w_bf[j] = stg[j % FF_STAGE_SLOTS].astype(BF16)

    def run(stream_weights):
        if stream_weights:
            for j in range(FF_STAGE_SLOTS):
                start_chunk(j)
        acc = None
        for j in range(FF_CHUNKS):
            if stream_weights:
                finish_chunk(j)
                if j + FF_STAGE_SLOTS < FF_CHUNKS:
                    start_chunk(j + FF_STAGE_SLOTS)
            a = jnp.dot(h, wg_bf[j], preferred_element_type=F32)
            b = jnp.dot(h, wu_bf[j], preferred_element_type=F32)
            act = (a * jax.nn.sigmoid(a) * b).astype(BF16)
            y = jnp.dot(act, wd_bf[j], preferred_element_type=F32)
            acc = y if acc is None else acc + y
        res = x + 0.5 * _mod_vec(mod_ref, row, mod_base + 2) * acc
        if split_out:
            @pl.when(is_ctx)
            def _():
                o_refs[0][...] = res

            @pl.when(jnp.logical_not(is_ctx))
            def _():
                o_refs[1][...] = res
        else:
            o_refs[0][...] = res

    @pl.when(i == 0)
    def _():
        run(True)

    @pl.when(i > 0)
    def _():
        run(False)


def _ffn(xs, mod, g, wg, wu, wd, layer, mod_base, split_out=False, mixer_out=None, adaln_next=None):
    tm = FFN_TILE
    n_ctx_tiles = N_CTX_TOK // tm
    split_in = len(xs) == 2

    def tiles(width):
        return (pl.BlockSpec((tm, width), lambda i: (i, 0)),
                pl.BlockSpec((tm, width), lambda i: (jnp.minimum(i, n_ctx_tiles - 1), 0)),
                pl.BlockSpec((tm, width), lambda i: (jnp.maximum(i - n_ctx_tiles, 0), 0)))

    tok, ctx_tok, smp_tok = tiles(D_MODEL)
    full = jax.ShapeDtypeStruct((N_TOK, D_MODEL), F32)
    pair = [jax.ShapeDtypeStruct((N_CTX_TOK, D_MODEL), F32), jax.ShapeDtypeStruct((N_SMP_TOK, D_MODEL), F32)]
    hbm = pl.BlockSpec(memory_space=pl.ANY)
    in_specs = [ctx_tok, smp_tok] if split_in else [tok]
    operands = list(xs)
    scratch = [
        pltpu.VMEM((FF_CHUNKS, D_MODEL, FF_TILE), BF16),
        pltpu.VMEM((FF_CHUNKS, D_MODEL, FF_TILE), BF16),
        pltpu.VMEM((FF_CHUNKS, FF_TILE, D_MODEL), BF16),
        pltpu.VMEM((FF_STAGE_SLOTS, D_MODEL, FF_TILE), F32),
        pltpu.VMEM((FF_STAGE_SLOTS, D_MODEL, FF_TILE), F32),
        pltpu.VMEM((FF_STAGE_SLOTS, FF_TILE, D_MODEL), F32),
        pltpu.SemaphoreType.DMA((FF_STAGE_SLOTS,)),
        pltpu.SemaphoreType.DMA((FF_STAGE_SLOTS,)),
        pltpu.SemaphoreType.DMA((FF_STAGE_SLOTS,)),
    ]
    if mixer_out is not None:
        _, ctx_half, smp_half = tiles(NA_WIDTH)
        in_specs += [ctx_half, smp_half, ctx_half, smp_half]
        operands += list(mixer_out[:4])
    in_specs += [_mod_spec(), pl.BlockSpec((1, D_MODEL), lambda i: (0, 0))]
    operands += [mod, g.reshape(1, D_MODEL)]
    out_specs = [ctx_tok, smp_tok] if split_out else [tok]
    out_shape = pair if split_out else [full]
    if adaln_next is not None:
        ada_in, ada_out = _adaln_specs(adaln_next[4], MOD_WIDTH // (N_TOK // tm))
        in_specs += ada_in
        operands += list(adaln_next[:4])
        out_specs.append(ada_out)
        out_shape.append(jax.ShapeDtypeStruct((MOD_ROWS, MOD_WIDTH), F32))
        scratch.append(pltpu.VMEM((MOD_ROWS, D_MODEL), F32))
    if mixer_out is not None:
        in_specs.insert(len(in_specs) - (4 if adaln_next is not None else 0), hbm)
        operands.insert(len(operands) - (4 if adaln_next is not None else 0), mixer_out[4])
        scratch += [
            pltpu.VMEM((D_MODEL, D_MODEL), BF16),
            pltpu.VMEM((FF_STAGE_SLOTS, FF_TILE, D_MODEL), F32),
            pltpu.SemaphoreType.DMA((FF_STAGE_SLOTS,)),
        ]
    return pl.pallas_call(
        functools.partial(_ffn_kernel, layer=layer, mod_base=mod_base, split_in=split_in,
                          split_out=split_out, mixer_out=mixer_out is not None,
                          adaln_next=None if adaln_next is None else adaln_next[4]),
        grid=(N_TOK // tm,),
        in_specs=in_specs + [hbm, hbm, hbm],
        out_specs=out_specs,
        out_shape=out_shape,
        scratch_shapes=scratch,
        compiler_params=_cparams(1),
        name="ffn",
    )(*operands, wg, wu, wd)


def _head_rms_norm(z, g, ones_bd):
    z2 = z * z
    hi = z2.astype(BF16)
    lo = (z2 - hi.astype(F32)).astype(BF16)
    n = ones_bd.shape[0]
    parts = []
    for c in range(z.shape[1] // n):
        sl = slice(c * n, (c + 1) * n)
        parts.append(jnp.dot(hi[:, sl], ones_bd, preferred_element_type=F32)
                     + jnp.dot(lo[:, sl], ones_bd, preferred_element_type=F32))
    ss = jnp.concatenate(parts, axis=1)
    return z * lax.rsqrt(ss * (1.0 / HEAD_DIM) + EPS) * g


def _proj_kernel(x_ref, mod_ref, g_ref, w_ref, qg_ref, kg_ref, ones_ref,
                 q_ref, k_ref, v_ref, xb_ref, gb_ref, kout_ref, vout_ref, w_bf_ref):
    i = pl.program_id(0)

    @pl.when(i == 0)
    def _():
        w_bf_ref[...] = w_ref[...].astype(BF16)

    x = x_ref[...]
    row = _mod_row_of_tile(i)
    h = _norm_mod(x, g_ref[...], _mod_vec(mod_ref, row, 3), _mod_vec(mod_ref, row, 4)).astype(BF16)

    def proj(part):
        return jnp.dot(h, w_bf_ref[:, part * NA_WIDTH:(part + 1) * NA_WIDTH], preferred_element_type=F32)

    ones_bd = ones_ref[...]
    q = _head_rms_norm(proj(0), jnp.tile(qg_ref[...], (1, NA_HEADS)), ones_bd) * (HEAD_DIM ** -0.5)
    q_ref[...] = q.astype(BF16)
    k = _head_rms_norm(proj(1), jnp.tile(kg_ref[...], (1, NA_HEADS)), ones_bd)
    k_ref[...] = k.astype(BF16)
    v = proj(2)
    v_ref[...] = v.astype(BF16)
    xb_ref[...] = proj(3)
    gb_ref[...] = proj(4)

    @pl.when(i < N_CTX_TILES)
    def _():
        for b in range(TOKEN_TILE // SEQ):
            kout_ref[b] = k[b * SEQ:(b + 1) * SEQ, :].T
            vout_ref[b] = v[b * SEQ:(b + 1) * SEQ, :].T


def _proj(x, mod, g, w_in, q_g, k_g):
    tm = TOKEN_TILE
    head = np.arange(2 * LANES) // HEAD_DIM
    ones_bd = jnp.asarray((head[:, None] == head[None, :]).astype(np.float32), dtype=BF16)
    tok = lambda i: (i, 0)
    const = lambda i: (0, 0)
    act_f32 = jax.ShapeDtypeStruct((N_TOK, NA_WIDTH), F32)
    act_bf16 = jax.ShapeDtypeStruct((N_TOK, NA_WIDTH), BF16)
    cache = jax.ShapeDtypeStruct((BATCH, NA_WIDTH, SEQ), F32)
    cache_spec = pl.BlockSpec((tm // SEQ, NA_WIDTH, SEQ), lambda i: (jnp.minimum(i, N_CTX_TILES - 1), 0, 0))
    return pl.pallas_call(
        _proj_kernel,
        grid=(N_TOK // tm,),
        in_specs=[
            pl.BlockSpec((tm, D_MODEL), tok),
            _mod_spec(),
            pl.BlockSpec((1, D_MODEL), const),
            _resident((D_MODEL, IN_WIDTH), const),
            pl.BlockSpec((1, HEAD_DIM), const),
            pl.BlockSpec((1, HEAD_DIM), const),
            _resident((2 * LANES, 2 * LANES), const),
        ],
        out_specs=[pl.BlockSpec((tm, NA_WIDTH), tok)] * 5 + [cache_spec, cache_spec],
        out_shape=[act_bf16, act_bf16, act_bf16, act_f32, act_f32, cache, cache],
        scratch_shapes=[pltpu.VMEM((D_MODEL, IN_WIDTH), BF16)],
        compiler_params=_cparams(1),
        name="mixer_in_proj",
    )(x, mod, g.reshape(1, D_MODEL), w_in, q_g.reshape(1, HEAD_DIM), k_g.reshape(1, HEAD_DIM), ones_bd)


def _head_masks():
    lane = lax.broadcasted_iota(jnp.int32, (1, 2 * HEAD_DIM), 1)
    return [lane < HEAD_DIM, lane >= HEAD_DIM]


def _ctx_attn_kernel(q_ref, k_ref, v_ref, o_ref):
    masks = _head_masks()
    for b in range(q_ref.shape[0] // SEQ):
        rows = slice(b * SEQ, (b + 1) * SEQ)
        for p in range(NA_HEADS // 2):
            sl = slice(2 * HEAD_DIM * p, 2 * HEAD_DIM * (p + 1))
            q2 = q_ref[rows, sl]
            k2t = k_ref[rows, sl].T
            v2 = v_ref[rows, sl]
            out = None
            for e in range(2):
                qm = jnp.where(masks[e], q2, jnp.zeros_like(q2))
                s = jnp.dot(qm, k2t, preferred_element_type=F32)
                pe = jnp.exp(s - jnp.max(s, axis=-1, keepdims=True))
                den = jnp.sum(pe, axis=-1, keepdims=True)
                o = jnp.dot(pe.astype(BF16), v2, preferred_element_type=F32) / den
                out = o if out is None else jnp.where(masks[e], o, out)
            o_ref[rows, sl] = out.astype(BF16)


def _ctx_attn(q, k, v):
    blk = pl.BlockSpec((CTX_ATTN_SEQ_PER_STEP * SEQ, NA_WIDTH), lambda b: (b, 0))
    return pl.pallas_call(
        _ctx_attn_kernel,
        grid=(BATCH // CTX_ATTN_SEQ_PER_STEP,),
        in_specs=[blk, blk, blk],
        out_specs=blk,
        out_shape=jax.ShapeDtypeStruct((N_CTX_TOK, NA_WIDTH), BF16),
        compiler_params=_cparams(1),
        name="ctx_attention",
    )(q, k, v)


def _na_build_bias_table(rpb_ref, table_ref):
    qc = lax.broadcasted_iota(jnp.int32, (GRID_W, LANES), 0)
    lane = lax.broadcasted_iota(jnp.int32, (GRID_W, LANES), 1)
    kc = lane % GRID_W
    col_start = jnp.clip(qc - WIN_W // 2, 0, GRID_W - WIN_W)
    col_in = (kc >= col_start) & (kc < col_start + WIN_W)
    neg = jnp.full((GRID_W, LANES), -jnp.inf, F32)

    def toeplitz(h, dr, lane0):
        if dr < 0 or dr >= N_DR:
            return neg
        row = jnp.pad(rpb_ref[h, dr:dr + 1, :], ((0, 0), (0, LANES - N_DC)))
        w = jnp.broadcast_to(row, (GRID_W, LANES))
        return pltpu.roll(w, (lane0 - (WIN_W - 1)) % LANES, 1, stride=1, stride_axis=0)

    for h in range(NA_HEADS):
        for i in range(N_DR_PAIRS):
            t = jnp.where(lane < GRID_W, toeplitz(h, i - 1, 0), toeplitz(h, i, GRID_W))
            table_ref[h, i] = jnp.where(col_in, t, neg)


def _na_kernel(q_ref, k_ref, v_ref, kc_ref, vc_ref, rpb_ref, o_ref, table_ref):
    b = pl.program_id(0)
    g = pl.program_id(1)

    @pl.when((b == 0) & (g == 0))
    def _():
        _na_build_bias_table(rpb_ref, table_ref)

    win_row0 = jnp.where(g < NA_GROUPS // 2, 0, ROWS - NA_K_ROWS)
    start = pl.multiple_of(win_row0 * GRID_W, GRID_W)
    q_row = g * NA_Q_ROWS + lax.broadcasted_iota(jnp.int32, (NA_Q, 1), 0) // GRID_W
    k_row = win_row0 + lax.broadcasted_iota(jnp.int32, (1, NA_K), 1) // GRID_W
    row_start = jnp.clip(q_row - KH // 2, 0, ROWS - KH)
    row_in = (k_row >= row_start) & (k_row < row_start + KH)
    masks = _head_masks()
    for p in range(NA_HEADS // 2):
        sl = slice(2 * HEAD_DIM * p, 2 * HEAD_DIM * (p + 1))
        q2 = q_ref[:, sl]
        klt = k_ref[pl.ds(start, NA_K), sl].T
        vl = v_ref[pl.ds(start, NA_K), sl]
        kct = kc_ref[0, sl, :].astype(BF16)
        vct = vc_ref[0, sl, :].astype(BF16)
        out = None
        for e in range(2):
            head = 2 * p + e
            bias_rows = []
            for a in range(NA_Q_ROWS):
                tiles = []
                for m in range(NA_K_ROWS // 2):
                    dr = win_row0 + 2 * m - (g * NA_Q_ROWS + a) + (WIN_H - 1)
                    tiles.append(table_ref[head, jnp.clip(dr + 1, 0, N_DR_PAIRS - 1)])
                bias_rows.append(jnp.concatenate(tiles, axis=1))
            bias = jnp.concatenate(bias_rows, axis=0)
            qm = jnp.where(masks[e], q2, jnp.zeros_like(q2))
            s_loc = jnp.where(row_in, jnp.dot(qm, klt, preferred_element_type=F32) + bias, -jnp.inf)
            s_ctx = jnp.dot(qm, kct, preferred_element_type=F32)
            m_max = jnp.maximum(jnp.max(s_loc, axis=-1, keepdims=True),
                                jnp.max(s_ctx, axis=-1, keepdims=True))
            p_loc = jnp.exp(s_loc - m_max)
            p_ctx = jnp.exp(s_ctx - m_max)
            den = jnp.sum(p_loc, axis=-1, keepdims=True) + jnp.sum(p_ctx, axis=-1, keepdims=True)
            o = (jnp.dot(p_loc.astype(BF16), vl, preferred_element_type=F32)
                 + lax.dot_general(p_ctx.astype(BF16), vct, (((1,), (1,)), ((), ())),
                                   preferred_element_type=F32)) / den
            out = o if out is None else jnp.where(masks[e], o, out)
        o_ref[:, sl] = out.astype(BF16)


def _na_attn(q, k, v, k_ctx, v_ctx, rpb_e):
    smp_blk0 = N_CTX_TOK // DEC_SEQ
    q_blk0 = N_CTX_TOK // NA_Q
    kv = pl.BlockSpec((DEC_SEQ, NA_WIDTH), lambda b, g: (smp_blk0 + b, 0))
    ctx = pl.BlockSpec((1, NA_WIDTH, PAST_LEN), lambda b, g: (b, 0, 0))
    return pl.pallas_call(
        _na_kernel,
        grid=(DEC_BATCH, NA_GROUPS),
        in_specs=[
            pl.BlockSpec((NA_Q, NA_WIDTH), lambda b, g: (q_blk0 + b * NA_GROUPS + g, 0)),
            kv, kv, ctx, ctx,
            pl.BlockSpec((NA_HEADS, N_DR, N_DC), lambda b, g: (0, 0, 0)),
        ],
        out_specs=pl.BlockSpec((NA_Q, NA_WIDTH), lambda b, g: (b * NA_GROUPS + g, 0)),
        out_shape=jax.ShapeDtypeStruct((N_SMP_TOK, NA_WIDTH), BF16),
        scratch_shapes=[pltpu.VMEM((NA_HEADS, N_DR_PAIRS, GRID_W, LANES), F32)],
        compiler_params=_cparams(2),
        name="neighbourhood_attention",
    )(q, k, v, k_ctx, v_ctx, rpb_e)


def _gelu_tanh(x):
    c0 = float(np.sqrt(2.0 / np.pi))
    inner = x * (c0 + (c0 * 0.044715) * (x * x))
    return (0.5 * x) * (1.0 + jnp.tanh(inner))


def _lru_build_gate_weights(wr_ref, wi_ref, w_scr):
    blocks_per_group = LRU_SUB // LRU_BLOCK
    w_scr[...] = jnp.zeros_like(w_scr)
    for d in range(2):
        for kind, w_ref in enumerate((wr_ref, wi_ref)):
            col0 = (2 * d + kind) * LRU_SUB
            for blk in range(LRU_BLOCKS):
                c, n = divmod(blk, blocks_per_group)
                r0 = n * LRU_BLOCK
                w_scr[c, r0:r0 + LRU_BLOCK, col0 + r0:col0 + r0 + LRU_BLOCK] = (
                    0.5 * w_ref[d * LRU_BLOCKS + blk]).astype(BF16)


def _lru_kernel(*refs, zero_state, t_len):
    refs = list(refs)
    xb_ref, gb_ref, cw_ref, cb_ref, wr_ref, wi_ref, br_ref, bi_ref, lam_ref = refs[:9]
    h0f_ref, h0b_ref = (None, None) if zero_state else refs[9:11]
    y_ref, hlf_ref, hlb_ref, af_ref, uf_ref, ab_ref, ub_ref, w_scr = refs[-8:]
    width = xb_ref.shape[1]
    n_seq = xb_ref.shape[0] // t_len
    n_blk = t_len // SUBLANES
    row = lax.broadcasted_iota(jnp.int32, (t_len, 1), 0)
    in_block = lax.broadcasted_iota(jnp.int32, (1, SUBLANES, 1), 1)

    @pl.when(pl.program_id(0) == 0)
    def _():
        _lru_build_gate_weights(wr_ref, wi_ref, w_scr)

    def shifted(z, s):
        rolled = pltpu.roll(z, (-s) % t_len, axis=0)
        ok = (row + s >= 0) & (row + s < t_len)
        return jnp.where(ok, rolled, 0.0)

    left = (CONV_W - 1) // 2
    for q, c in [(q, c) for q in range(n_seq) for c in range(width // LRU_SUB)]:
        rows = slice(q * t_len, (q + 1) * t_len)
        cs = slice(c * LRU_SUB, (c + 1) * LRU_SUB)
        x = xb_ref[rows, cs]
        xc = cb_ref[:, cs]
        for j in range(CONV_W):
            tap = x if j == left else shifted(x, j - left)
            xc = xc + tap * cw_ref[j:j + 1, cs]
        half_gates = jnp.dot(xc.astype(BF16), w_scr[c], preferred_element_type=F32)
        half_xc = 0.5 * xc
        for d, (a_ref, u_ref) in enumerate(((af_ref, uf_ref), (ab_ref, ub_ref))):
            t_r = jnp.tanh(half_gates[:, (2 * d) * LRU_SUB:(2 * d + 1) * LRU_SUB] + 0.5 * br_ref[d:d + 1, cs])
            t_i = jnp.tanh(half_gates[:, (2 * d + 1) * LRU_SUB:(2 * d + 2) * LRU_SUB] + 0.5 * bi_ref[d:d + 1, cs])
            lam = lam_ref[d:d + 1, cs]
            log_sig = jnp.minimum(lam, 0.0) - jnp.log1p(jnp.exp(-jnp.abs(lam)))
            half_c_log_sig = (0.5 * LRU_C) * log_sig
            log_a = t_r * half_c_log_sig + half_c_log_sig
            a = jnp.exp(log_a)
            var = -jnp.tanh(log_a) * (a * a + 1.0)
            u = jnp.where(var > 0.0, var * lax.rsqrt(var), 0.0) * ((t_i + 1.0) * half_xc)
            a = a.reshape(n_blk, SUBLANES, LRU_SUB)
            u = u.reshape(n_blk, SUBLANES, LRU_SUB)
            step = 1
            while step < SUBLANES:
                if d == 0:
                    ok, shift = in_block >= step, step
                else:
                    ok, shift = in_block < SUBLANES - step, SUBLANES - step
                a_prev = jnp.where(ok, pltpu.roll(a, shift, axis=1), 1.0)
                u_prev = jnp.where(ok, pltpu.roll(u, shift, axis=1), 0.0)
                u = u + a * u_prev
                a = a * a_prev
                step *= 2
            a_ref[rows, cs] = a.reshape(t_len, LRU_SUB)
            u_ref[rows, cs] = u.reshape(t_len, LRU_SUB)

    def body(i, carry):
        new = []
        for q, (cf, cb) in enumerate(carry):
            f0 = pl.multiple_of(q * t_len + i * SUBLANES, SUBLANES)
            b0 = pl.multiple_of(q * t_len + (n_blk - 1 - i) * SUBLANES, SUBLANES)
            hf = uf_ref[pl.ds(f0, SUBLANES), :] + af_ref[pl.ds(f0, SUBLANES), :] * cf
            hb = ub_ref[pl.ds(b0, SUBLANES), :] + ab_ref[pl.ds(b0, SUBLANES), :] * cb
            uf_ref[pl.ds(f0, SUBLANES), :] = hf
            ub_ref[pl.ds(b0, SUBLANES), :] = hb
            new.append((jnp.broadcast_to(hf[SUBLANES - 1:SUBLANES, :], (SUBLANES, width)),
                        jnp.broadcast_to(hb[0:1, :], (SUBLANES, width))))
        return tuple(new)

    if zero_state:
        init = tuple((jnp.zeros((SUBLANES, width), F32),) * 2 for _ in range(n_seq))
    else:
        init = tuple((jnp.broadcast_to(h0f_ref[q], (SUBLANES, width)),
                      jnp.broadcast_to(h0b_ref[q], (SUBLANES, width))) for q in range(n_seq))
    last = lax.fori_loop(0, n_blk, body, init)
    for q, (cf, cb) in enumerate(last):
        hlf_ref[q] = cf[0:1, :]
        hlb_ref[q] = cb[0:1, :]
    y_ref[...] = ((uf_ref[...] + ub_ref[...]) * _gelu_tanh(gb_ref[...])).astype(BF16)


def _lru(xb, gb, conv_w, conv_b, w_r, b_r, w_i, b_i, lam, h0, n_seq, t_len, tok_blk0, seq_per_step):
    rows = seq_per_step * t_len
    const2 = lambda s: (0, 0)
    const3 = lambda s: (0, 0, 0)
    blocks = pl.BlockSpec((2 * LRU_BLOCKS, LRU_BLOCK, LRU_BLOCK), const3)
    per_dir = pl.BlockSpec((2, LRU_WIDTH), const2)
    state = pl.BlockSpec((seq_per_step, 1, LRU_WIDTH), lambda s: (s, 0, 0))
    state_shape = jax.ShapeDtypeStruct((n_seq, 1, LRU_WIDTH), F32)
    return pl.pallas_call(
        functools.partial(_lru_kernel, zero_state=h0 is None, t_len=t_len),
        grid=(n_seq // seq_per_step,),
        in_specs=[
            pl.BlockSpec((rows, LRU_WIDTH), lambda s: (tok_blk0 + s, 0)),
            pl.BlockSpec((rows, LRU_WIDTH), lambda s: (tok_blk0 + s, 0)),
            pl.BlockSpec((CONV_W, LRU_WIDTH), const2),
            pl.BlockSpec((1, LRU_WIDTH), const2),
            blocks, blocks, per_dir, per_dir, per_dir,
        ] + ([] if h0 is None else [state, state]),
        out_specs=[pl.BlockSpec((rows, LRU_WIDTH), lambda s: (s, 0)), state, state],
        out_shape=[jax.ShapeDtypeStruct((n_seq * t_len, LRU_WIDTH), BF16), state_shape, state_shape],
        scratch_shapes=[pltpu.VMEM((rows, LRU_WIDTH), F32)] * 4
        + [pltpu.VMEM((LRU_WIDTH // LRU_SUB, LRU_SUB, 4 * LRU_SUB), BF16)],
        compiler_params=_cparams(1),
        name="rglru",
    )(xb, gb, conv_w, conv_b.reshape(1, LRU_WIDTH),
      w_r.reshape(2 * LRU_BLOCKS, LRU_BLOCK, LRU_BLOCK), w_i.reshape(2 * LRU_BLOCKS, LRU_BLOCK, LRU_BLOCK),
      b_r, b_i, lam, *(() if h0 is None else h0))


def _fourier_kernel(x_ref, mod_ref, g_ref, cs_ref, ct_ref, w_ref, o_ref, w_bf_ref, *, mod_row0, t_len):
    @pl.when(pl.program_id(0) == 0)
    def _():
        w_bf_ref[...] = w_ref[...].astype(BF16)

    x = x_ref[...]
    row = mod_row0 + pl.program_id(0) if mod_row0 else 0
    h = _norm_mod(x, g_ref[...], _mod_vec(mod_ref, row, 3), _mod_vec(mod_ref, row, 4)).astype(BF16)
    cos_parts, sin_parts = [], []
    for g in range(FOURIER_GROUPS):
        ab = jnp.dot(h[:, g * GROUP_W:(g + 1) * GROUP_W], cs_ref[...], preferred_element_type=F32)
        cos_parts.append(ab[:, :GROUP_W])
        sin_parts.append(ab[:, GROUP_W:])
    cos_all = jnp.concatenate(cos_parts, axis=1).astype(BF16)
    sin_all = jnp.concatenate(sin_parts, axis=1).astype(BF16)
    f_parts = []
    for q in range(x.shape[0] // t_len):
        rows = slice(q * t_len, (q + 1) * t_len)
        stacked = jnp.concatenate([cos_all[rows], sin_all[rows]], axis=0)
        f_parts.append(jnp.dot(ct_ref[...], stacked, preferred_element_type=F32))
    f = jnp.concatenate(f_parts, axis=0) * ((t_len * GROUP_W) ** -0.5)
    y = jnp.dot(f.astype(BF16), w_bf_ref[...], preferred_element_type=F32)
    o_ref[...] = x + _mod_vec(mod_ref, row, 5) * y


def _dft_tables(t_len):
    def cos_sin(n):
        jk = np.outer(np.arange(n), np.arange(n)) % n
        ang = 2.0 * np.pi * jk.astype(np.float64) / n
        return np.cos(ang), np.sin(ang)

    cc, sc = cos_sin(GROUP_W)
    ct, st = cos_sin(t_len)
    chan = jnp.asarray(np.concatenate([cc, sc], axis=1).astype(np.float32)).astype(BF16)
    time = jnp.asarray(np.concatenate([ct, -st], axis=1).astype(np.float32)).astype(BF16)
    return chan, time


def _fourier(x, mod, g, w_out, n_seq, t_len, tok_blk0, mod_row0, seq_per_step):
    assert seq_per_step == 1 or mod_row0 == 0
    chan, time = _dft_tables(t_len)
    rows = seq_per_step * t_len
    seq = lambda s: (tok_blk0 + s, 0)
    const = lambda s: (0, 0)
    return pl.pallas_call(
        functools.partial(_fourier_kernel, mod_row0=mod_row0, t_len=t_len),
        grid=(n_seq // seq_per_step,),
        in_specs=[
            pl.BlockSpec((rows, D_MODEL), seq),
            _mod_spec(),
            pl.BlockSpec((1, D_MODEL), const),
            _resident((GROUP_W, 2 * GROUP_W), const),
            _resident((t_len, 2 * t_len), const),
            _resident((D_MODEL, D_MODEL), const),
        ],
        out_specs=pl.BlockSpec((rows, D_MODEL), seq),
        out_shape=jax.ShapeDtypeStruct((N_TOK, D_MODEL), F32),
        input_output_aliases={0: 0},
        scratch_shapes=[pltpu.VMEM((D_MODEL, D_MODEL), BF16)],
        compiler_params=_cparams(1),
        name="fourier_mixer",
    )(x, mod, g.reshape(1, D_MODEL), chan, time, w_out)


def _cache_layout(t):
    return jnp.transpose(t.reshape(BATCH, 1, NA_HEADS, HEAD_DIM, SEQ), (0, 1, 4, 2, 3))


def kernel(x_prompt, x_sample, cache_k, cache_v, state_lru_fwd, state_lru_bwd, c, c_ctx, w_ada, b_ada, norm_g, ffn1_gate, ffn1_up, ffn1_down, ffn2_gate, ffn2_up, ffn2_down, w_in, q_norm_g, k_norm_g, rpb, conv_w, conv_b, lru_w_r, lru_b_r, lru_w_i, lru_b_i, lru_lambda, w_out_ab, w_out_c):
    assert DEPTH == 2, "one neighbourhood/RG-LRU layer followed by one Fourier layer"
    c_ctx2 = c_ctx.reshape(1, D_MODEL)
    mod0 = _adaln(c_ctx2, c, w_ada, b_ada, 0)

    ffn1 = (ffn1_gate, ffn1_up, ffn1_down)
    ffn2 = (ffn2_gate, ffn2_up, ffn2_down)

    x, mod1 = _ffn((x_prompt.reshape(N_CTX_TOK, D_MODEL), x_sample.reshape(N_SMP_TOK, D_MODEL)),
                   mod0, norm_g[0, 0], *ffn1, 0, 0, adaln_next=(c_ctx2, c, w_ada, b_ada, 1))
    q, k, v, xb, gb, new_k, new_v = _proj(x, mod0, norm_g[0, 1], w_in[0], q_norm_g[0], k_norm_g[0])
    o_ctx = _ctx_attn(q, k, v)
    o_smp = _na_attn(q, k, v,
                     jnp.transpose(cache_k[:, 0], (0, 2, 3, 1)).reshape(DEC_BATCH, NA_WIDTH, PAST_LEN),
                     jnp.transpose(cache_v[:, 0], (0, 2, 3, 1)).reshape(DEC_BATCH, NA_WIDTH, PAST_LEN), rpb[0])
    lru_prm = (conv_w[0], conv_b[0], lru_w_r[0], lru_b_r[0], lru_w_i[0], lru_b_i[0], lru_lambda[0])
    yb_ctx, new_hf, new_hb = _lru(xb, gb, *lru_prm, None, BATCH, SEQ, 0, CTX_SEQ_PER_STEP)
    yb_smp, _, _ = _lru(xb, gb, *lru_prm, (state_lru_fwd, state_lru_bwd),
                        DEC_BATCH, DEC_SEQ, N_CTX_TOK // DEC_SEQ, 1)
    (x,) = _ffn((x,), mod0, norm_g[0, 2], *ffn2, 0, 6, mixer_out=(o_ctx, o_smp, yb_ctx, yb_smp, w_out_ab))

    (x,) = _ffn((x,), mod1, norm_g[1, 0], *ffn1, 1, 0)
    x = _fourier(x, mod1, norm_g[1, 1], w_out_c[0], BATCH, SEQ, 0, 0, CTX_SEQ_PER_STEP)
    x = _fourier(x, mod1, norm_g[1, 1], w_out_c[0], DEC_BATCH, DEC_SEQ, N_CTX_TOK // DEC_SEQ, 1, 1)
    y_prompt, y_sample = _ffn((x,), mod1, norm_g[1, 2], *ffn2, 1, 6, split_out=True)

    return (y_prompt.reshape(BATCH, SEQ, D_MODEL), y_sample.reshape(DEC_BATCH, DEC_SEQ, D_MODEL),
            _cache_layout(new_k), _cache_layout(new_v),
            new_hf, new_hb)
```

```python
import functools

import numpy as np
import jax
import jax.numpy as jnp
from jax import lax
from jax.experimental import pallas as pl
from jax.experimental.pallas import tpu as pltpu

F32 = jnp.float32
BF16 = jnp.bfloat16

D_MODEL = 1024
BATCH = 16
SEQ = 256
DEPTH = 2
DEC_BATCH = 2
DEC_SEQ = 1024
PAST_LEN = 256
GRID_W = 64
HEAD_DIM = 64
NA_WIDTH = 512
NA_HEADS = 8
WIN_H = 8
WIN_W = 16
LRU_WIDTH = 512
LRU_BLOCKS = 8
LRU_BLOCK = 64
LRU_C = 8.0
LRU_SUB = 256
CONV_W = 4
FOURIER_GROUPS = 4
GROUP_W = D_MODEL // FOURIER_GROUPS
D_FF = 2816
N_MOD = 9
IN_WIDTH = 3 * NA_WIDTH + 2 * LRU_WIDTH
EPS = 1e-6

N_CTX_TOK = BATCH * SEQ
N_SMP_TOK = DEC_BATCH * DEC_SEQ
N_TOK = N_CTX_TOK + N_SMP_TOK
MOD_ROWS = 8
MOD_WIDTH = N_MOD * D_MODEL
ROWS = DEC_SEQ // GRID_W
KH = min(WIN_H, ROWS)

TOKEN_TILE = 512
N_CTX_TILES = N_CTX_TOK // TOKEN_TILE
CTX_SEQ_PER_STEP = 2
CTX_ATTN_SEQ_PER_STEP = 4
FFN_TILE = 512
FF_TILE = 256
FF_CHUNKS = D_FF // FF_TILE
FF_STAGE_SLOTS = 2
SUBLANES = 8
LANES = 128
VMEM_LIMIT = 56 * 1024 * 1024

NA_Q_ROWS = 4
NA_GROUPS = ROWS // NA_Q_ROWS
NA_K_ROWS = 12
NA_Q = NA_Q_ROWS * GRID_W
NA_K = NA_K_ROWS * GRID_W
N_DR = 2 * WIN_H - 1
N_DC = 2 * WIN_W - 1
N_DR_PAIRS = N_DR + 1


def _cparams(n_axes):
    return pltpu.CompilerParams(
        dimension_semantics=("arbitrary",) * n_axes, vmem_limit_bytes=VMEM_LIMIT)


def _resident(block_shape, index_map):
    return pl.BlockSpec(block_shape, index_map, pipeline_mode=pl.Buffered(1))


def _mod_spec():
    return _resident((MOD_ROWS, MOD_WIDTH), lambda i: (0, 0))


def _mod_row_of_tile(i, tile=TOKEN_TILE):
    n_ctx_tiles = N_CTX_TOK // tile
    tiles_per_seq = DEC_SEQ // tile
    return jnp.where(i < n_ctx_tiles, 0, 1 + (i - n_ctx_tiles) // tiles_per_seq)


def _mod_vec(mod_ref, row, k):
    return mod_ref[pl.ds(row, 1), k * D_MODEL:(k + 1) * D_MODEL]


def _norm_mod(x, g, shift, scale):
    ms = jnp.mean(x * x, axis=-1, keepdims=True)
    return (x * lax.rsqrt(ms + EPS)) * (g * (1.0 + scale)) + shift


def _adaln_slab(cctx_ref, c_ref, w_ref, b_ref, cond_ref, layer):
    cond_ref[...] = jnp.zeros_like(cond_ref)
    cond_ref[0:1, :] = cctx_ref[...]
    cond_ref[1:1 + DEC_BATCH, :] = c_ref[...]
    cond = cond_ref[...]
    s = (cond * jax.nn.sigmoid(cond)).astype(BF16)
    return jnp.dot(s, w_ref[...].astype(BF16), preferred_element_type=F32) + b_ref[layer:layer + 1, :]


def _adaln_specs(layer, slab):
    return ([pl.BlockSpec((1, D_MODEL), lambda i: (0, 0)),
             pl.BlockSpec((DEC_BATCH, D_MODEL), lambda i: (0, 0)),
             pl.BlockSpec((None, D_MODEL, slab), lambda i: (layer, 0, i)),
             pl.BlockSpec((DEPTH, slab), lambda i: (0, i))],
            pl.BlockSpec((MOD_ROWS, slab), lambda i: (0, i)))


def _adaln_kernel(cctx_ref, c_ref, w_ref, b_ref, o_ref, cond_ref, *, layer):
    o_ref[...] = _adaln_slab(cctx_ref, c_ref, w_ref, b_ref, cond_ref, layer)


def _adaln(c_ctx, c, w_ada, b_ada, layer):
    slab = MOD_WIDTH // 4
    in_specs, out_spec = _adaln_specs(layer, slab)
    return pl.pallas_call(
        functools.partial(_adaln_kernel, layer=layer),
        grid=(MOD_WIDTH // slab,),
        in_specs=in_specs,
        out_specs=out_spec,
        out_shape=jax.ShapeDtypeStruct((MOD_ROWS, MOD_WIDTH), F32),
        scratch_shapes=[pltpu.VMEM((MOD_ROWS, D_MODEL), F32)],
        compiler_params=_cparams(1),
        name="adaln",
    )(c_ctx, c, w_ada, b_ada)


def _ffn_weight_copy(w_hbm, stage_ref, sem_ref, layer, j, ff_axis):
    ff = pl.ds(j * FF_TILE, FF_TILE)
    src = w_hbm.at[layer, :, ff] if ff_axis == 1 else w_hbm.at[layer, ff, :]
    slot = j % FF_STAGE_SLOTS
    return pltpu.make_async_copy(src, stage_ref.at[slot], sem_ref.at[slot])


def _mixer_out_copy(w_hbm, stage_ref, sem_ref, j):
    rows = stage_ref.shape[1]
    slot = j % stage_ref.shape[0]
    return pltpu.make_async_copy(w_hbm.at[0, pl.ds(j * rows, rows), :], stage_ref.at[slot], sem_ref.at[slot])


def _ffn_kernel(*refs, layer, mod_base, split_in, split_out, mixer_out, adaln_next):
    refs = list(refs)
    take = lambda n: [refs.pop(0) for _ in range(n)]
    x_refs = take(2 if split_in else 1)
    mix_refs = take(4) if mixer_out else None
    mod_ref, g_ref = take(2)
    wo_hbm = take(1)[0] if mixer_out else None
    ada_refs = take(4) if adaln_next is not None else None
    wg_hbm, wu_hbm, wd_hbm = take(3)
    o_refs = take(2 if split_out else 1)
    modn_ref = take(1)[0] if adaln_next is not None else None
    wg_bf, wu_bf, wd_bf, stg_g, stg_u, stg_d, sem_g, sem_u, sem_d = take(9)
    if adaln_next is not None:
        modn_ref[...] = _adaln_slab(*ada_refs, take(1)[0], adaln_next)
    streams = ((wg_hbm, stg_g, sem_g, wg_bf, 1), (wu_hbm, stg_u, sem_u, wu_bf, 1),
               (wd_hbm, stg_d, sem_d, wd_bf, 0))

    i = pl.program_id(0)
    is_ctx = i < N_CTX_TOK // FFN_TILE
    if split_in:
        x = jnp.where(is_ctx, x_refs[0][...], x_refs[1][...])
    else:
        x = x_refs[0][...]
    row = _mod_row_of_tile(i, FFN_TILE)

    if mixer_out:
        wo_bf, stg_o, sem_o = take(3)
        rows = stg_o.shape[1]
        n_chunks = D_MODEL // rows

        @pl.when(i == 0)
        def _():
            for j in range(stg_o.shape[0]):
                _mixer_out_copy(wo_hbm, stg_o, sem_o, j).start()
            for j in range(n_chunks):
                _mixer_out_copy(wo_hbm, stg_o, sem_o, j).wait()
                wo_bf[j * rows:(j + 1) * rows, :] = stg_o[j % stg_o.shape[0]].astype(BF16)
                if j + stg_o.shape[0] < n_chunks:
                    _mixer_out_copy(wo_hbm, stg_o, sem_o, j + stg_o.shape[0]).start()

        oc_ref, os_ref, yc_ref, ys_ref = mix_refs
        cat = jnp.concatenate([jnp.where(is_ctx, oc_ref[...], os_ref[...]),
                               jnp.where(is_ctx, yc_ref[...], ys_ref[...])], axis=1)
        x = x + _mod_vec(mod_ref, row, mod_base - 1) * jnp.dot(cat, wo_bf[...], preferred_element_type=F32)

    h = _norm_mod(x, g_ref[...], _mod_vec(mod_ref, row, mod_base),
                  _mod_vec(mod_ref, row, mod_base + 1)).astype(BF16)

    def start_chunk(j):
        for w_hbm, stg, sem, _, ff_axis in streams:
            _ffn_weight_copy(w_hbm, stg, sem, layer, j, ff_axis).start()

    def finish_chunk(j):
        for w_hbm, stg, sem, w_bf, ff_axis in streams:
            _ffn_weight_copy(w_hbm, stg, sem, layer, j, ff_axis).wait()
            w_bf[j] = stg[j % FF_STAGE_SLOTS].astype(BF16)

    def run(stream_weights):
        if stream_weights:
            for j in range(FF_STAGE_SLOTS):
                start_chunk(j)
        acc = None
        for j in range(FF_CHUNKS):
            if stream_weights:
                finish_chunk(j)
                if j + FF_STAGE_SLOTS < FF_CHUNKS:
                    start_chunk(j + FF_STAGE_SLOTS)
            a = jnp.dot(h, wg_bf[j], preferred_element_type=F32)
            b = jnp.dot(h, wu_bf[j], preferred_element_type=F32)
            act = (a * jax.nn.sigmoid(a) * b).astype(BF16)
            y = jnp.dot(act, wd_bf[j], preferred_element_type=F32)
            acc = y if acc is None else acc + y
        res = x + 0.5 * _mod_vec(mod_ref, row, mod_base + 2) * acc
        if split_out:
            @pl.when(is_ctx)
            def _():
                o_refs[0][...] = res

            @pl.when(jnp.logical_not(is_ctx))
            def _():
                o_refs[1][...] = res
        else:
            o_refs[0][...] = res

    @pl.when(i == 0)
    def _():
        run(True)

    @pl.when(i > 0)
    def _():
        run(False)


def _ffn(xs, mod, g, wg, wu, wd, layer, mod_base, split_out=False, mixer_out=None, adaln_next=None):
    tm = FFN_TILE
    n_ctx_tiles = N_CTX_TOK // tm
    split_in = len(xs) == 2

    def tiles(width):
        return (pl.BlockSpec((tm, width), lambda i: (i, 0)),
                pl.BlockSpec((tm, width), lambda i: (jnp.minimum(i, n_ctx_tiles - 1), 0)),
                pl.BlockSpec((tm, width), lambda i: (jnp.maximum(i - n_ctx_tiles, 0), 0)))

    tok, ctx_tok, smp_tok = tiles(D_MODEL)
    full = jax.ShapeDtypeStruct((N_TOK, D_MODEL), F32)
    pair = [jax.ShapeDtypeStruct((N_CTX_TOK, D_MODEL), F32), jax.ShapeDtypeStruct((N_SMP_TOK, D_MODEL), F32)]
    hbm = pl.BlockSpec(memory_space=pl.ANY)
    in_specs = [ctx_tok, smp_tok] if split_in else [tok]
    operands = list(xs)
    scratch = [
        pltpu.VMEM((FF_CHUNKS, D_MODEL, FF_TILE), BF16),
        pltpu.VMEM((FF_CHUNKS, D_MODEL, FF_TILE), BF16),
        pltpu.VMEM((FF_CHUNKS, FF_TILE, D_MODEL), BF16),
        pltpu.VMEM((FF_STAGE_SLOTS, D_MODEL, FF_TILE), F32),
        pltpu.VMEM((FF_STAGE_SLOTS, D_MODEL, FF_TILE), F32),
        pltpu.VMEM((FF_STAGE_SLOTS, FF_TILE, D_MODEL), F32),
        pltpu.SemaphoreType.DMA((FF_STAGE_SLOTS,)),
        pltpu.SemaphoreType.DMA((FF_STAGE_SLOTS,)),
        pltpu.SemaphoreType.DMA((FF_STAGE_SLOTS,)),
    ]
    if mixer_out is not None:
        _, ctx_half, smp_half = tiles(NA_WIDTH)
        in_specs += [ctx_half, smp_half, ctx_half, smp_half]
        operands += list(mixer_out[:4])
    in_specs += [_mod_spec(), pl.BlockSpec((1, D_MODEL), lambda i: (0, 0))]
    operands += [mod, g.reshape(1, D_MODEL)]
    out_specs = [ctx_tok, smp_tok] if split_out else [tok]
    out_shape = pair if split_out else [full]
    if adaln_next is not None:
        ada_in, ada_out = _adaln_specs(adaln_next[4], MOD_WIDTH // (N_TOK // tm))
        in_specs += ada_in
        operands += list(adaln_next[:4])
        out_specs.append(ada_out)
        out_shape.append(jax.ShapeDtypeStruct((MOD_ROWS, MOD_WIDTH), F32))
        scratch.append(pltpu.VMEM((MOD_ROWS, D_MODEL), F32))
    if mixer_out is not None:
        in_specs.insert(len(in_specs) - (4 if adaln_next is not None else 0), hbm)
        operands.insert(len(operands) - (4 if adaln_next is not None else 0), mixer_out[4])
        scratch += [
            pltpu.VMEM((D_MODEL, D_MODEL), BF16),
            pltpu.VMEM((FF_STAGE_SLOTS, FF_TILE, D_MODEL), F32),
            pltpu.SemaphoreType.DMA((FF_STAGE_SLOTS,)),
        ]
    return pl.pallas_call(
        functools.partial(_ffn_kernel, layer=layer, mod_base=mod_base, split_in=split_in,
                          split_out=split_out, mixer_out=mixer_out is not None,
                          adaln_next=None if adaln_next is None else adaln_next[4]),
        grid=(N_TOK // tm,),
        in_specs=in_specs + [hbm, hbm, hbm],
        out_specs=out_specs,
        out_shape=out_shape,
        scratch_shapes=scratch,
        compiler_params=_cparams(1),
        name="ffn",
    )(*operands, wg, wu, wd)


def _head_rms_norm(z, g, ones_bd):
    z2 = z * z
    hi = z2.astype(BF16)
    lo = (z2 - hi.astype(F32)).astype(BF16)
    n = ones_bd.shape[0]
    parts = []
    for c in range(z.shape[1] // n):
        sl = slice(c * n, (c + 1) * n)
        parts.append(jnp.dot(hi[:, sl], ones_bd, preferred_element_type=F32)
                     + jnp.dot(lo[:, sl], ones_bd, preferred_element_type=F32))
    ss = jnp.concatenate(parts, axis=1)
    return z * lax.rsqrt(ss * (1.0 / HEAD_DIM) + EPS) * g


def _proj_kernel(x_ref, mod_ref, g_ref, w_ref, qg_ref, kg_ref, ones_ref,
                 q_ref, k_ref, v_ref, xb_ref, gb_ref, kout_ref, vout_ref, w_bf_ref):
    i = pl.program_id(0)

    @pl.when(i == 0)
    def _():
        w_bf_ref[...] = w_ref[...].astype(BF16)

    x = x_ref[...]
    row = _mod_row_of_tile(i)
    h = _norm_mod(x, g_ref[...], _mod_vec(mod_ref, row, 3), _mod_vec(mod_ref, row, 4)).astype(BF16)

    def proj(part):
        return jnp.dot(h, w_bf_ref[:, part * NA_WIDTH:(part + 1) * NA_WIDTH], preferred_element_type=F32)

    ones_bd = ones_ref[...]
    q = _head_rms_norm(proj(0), jnp.tile(qg_ref[...], (1, NA_HEADS)), ones_bd) * (HEAD_DIM ** -0.5)
    q_ref[...] = q.astype(BF16)
    k = _head_rms_norm(proj(1), jnp.tile(kg_ref[...], (1, NA_HEADS)), ones_bd)
    k_ref[...] = k.astype(BF16)
    v = proj(2)
    v_ref[...] = v.astype(BF16)
    xb_ref[...] = proj(3)
    gb_ref[...] = proj(4)

    kt = [k[b * SEQ:(b + 1) * SEQ, :].T for b in range(TOKEN_TILE // SEQ)]
    vt = [v[b * SEQ:(b + 1) * SEQ, :].T for b in range(TOKEN_TILE // SEQ)]

    @pl.when(i < N_CTX_TILES)
    def _():
        for b in range(TOKEN_TILE // SEQ):
            kout_ref[b] = kt[b]
            vout_ref[b] = vt[b]


def _proj(x, mod, g, w_in, q_g, k_g):
    tm = TOKEN_TILE
    head = np.arange(2 * LANES) // HEAD_DIM
    ones_bd = jnp.asarray((head[:, None] == head[None, :]).astype(np.float32), dtype=BF16)
    tok = lambda i: (i, 0)
    const = lambda i: (0, 0)
    act_f32 = jax.ShapeDtypeStruct((N_TOK, NA_WIDTH), F32)
    act_bf16 = jax.ShapeDtypeStruct((N_TOK, NA_WIDTH), BF16)
    cache = jax.ShapeDtypeStruct((BATCH, NA_WIDTH, SEQ), F32)
    cache_spec = pl.BlockSpec((tm // SEQ, NA_WIDTH, SEQ), lambda i: (jnp.minimum(i, N_CTX_TILES - 1), 0, 0))
    return pl.pallas_call(
        _proj_kernel,
        grid=(N_TOK // tm,),
        in_specs=[
            pl.BlockSpec((tm, D_MODEL), tok),
            _mod_spec(),
            pl.BlockSpec((1, D_MODEL), const),
            _resident((D_MODEL, IN_WIDTH), const),
            pl.BlockSpec((1, HEAD_DIM), const),
            pl.BlockSpec((1, HEAD_DIM), const),
            _resident((2 * LANES, 2 * LANES), const),
        ],
        out_specs=[pl.BlockSpec((tm, NA_WIDTH), tok)] * 5 + [cache_spec, cache_spec],
        out_shape=[act_bf16, act_bf16, act_bf16, act_f32, act_f32, cache, cache],
        scratch_shapes=[pltpu.VMEM((D_MODEL, IN_WIDTH), BF16)],
        compiler_params=_cparams(1),
        name="mixer_in_proj",
    )(x, mod, g.reshape(1, D_MODEL), w_in, q_g.reshape(1, HEAD_DIM), k_g.reshape(1, HEAD_DIM), ones_bd)


def _head_masks():
    lane = lax.broadcasted_iota(jnp.int32, (1, 2 * HEAD_DIM), 1)
    return [lane < HEAD_DIM, lane >= HEAD_DIM]


def _ctx_attn_kernel(q_ref, k_ref, v_ref, o_ref):
    masks = _head_masks()
    units = [(b, h) for b in range(q_ref.shape[0] // SEQ) for h in range(NA_HEADS)]

    def where(b, h):
        return slice(b * SEQ, (b + 1) * SEQ), slice(2 * HEAD_DIM * (h // 2), 2 * HEAD_DIM * (h // 2 + 1))

    def scores(b, h):
        rows, sl = where(b, h)
        q2 = q_ref[rows, sl]
        qm = jnp.where(masks[h % 2], q2, jnp.zeros_like(q2))
        return jnp.dot(qm, k_ref[rows, sl].T, preferred_element_type=F32)

    def attend(b, h, s):
        rows, sl = where(b, h)
        pe = jnp.exp(s - jnp.max(s, axis=-1, keepdims=True))
        den = jnp.sum(pe, axis=-1, keepdims=True)
        return jnp.dot(pe.astype(BF16), v_ref[rows, sl], preferred_element_type=F32) / den

    pending = scores(*units[0])
    out = None
    for n, (b, h) in enumerate(units):
        current = pending
        if n + 1 < len(units):
            pending = scores(*units[n + 1])
        o = attend(b, h, current)
        if h % 2 == 0:
            out = o
        else:
            rows, sl = where(b, h)
            o_ref[rows, sl] = jnp.where(masks[1], o, out).astype(BF16)


def _ctx_attn(q, k, v):
    blk = pl.BlockSpec((CTX_ATTN_SEQ_PER_STEP * SEQ, NA_WIDTH), lambda b: (b, 0))
    return pl.pallas_call(
        _ctx_attn_kernel,
        grid=(BATCH // CTX_ATTN_SEQ_PER_STEP,),
        in_specs=[blk, blk, blk],
        out_specs=blk,
        out_shape=jax.ShapeDtypeStruct((N_CTX_TOK, NA_WIDTH), BF16),
        compiler_params=_cparams(1),
        name="ctx_attention",
    )(q, k, v)


def _na_build_bias_table(rpb_ref, table_ref):
    qc = lax.broadcasted_iota(jnp.int32, (GRID_W, LANES), 0)
    lane = lax.broadcasted_iota(jnp.int32, (GRID_W, LANES), 1)
    kc = lane % GRID_W
    col_start = jnp.clip(qc - WIN_W // 2, 0, GRID_W - WIN_W)
    col_in = (kc >= col_start) & (kc < col_start + WIN_W)
    neg = jnp.full((GRID_W, LANES), -jnp.inf, F32)

    def toeplitz(h, dr, lane0):
        if dr < 0 or dr >= N_DR:
            return neg
        row = jnp.pad(rpb_ref[h, dr:dr + 1, :], ((0, 0), (0, LANES - N_DC)))
        w = jnp.broadcast_to(row, (GRID_W, LANES))
        return pltpu.roll(w, (lane0 - (WIN_W - 1)) % LANES, 1, stride=1, stride_axis=0)

    for h in range(NA_HEADS):
        for i in range(N_DR_PAIRS):
            t = jnp.where(lane < GRID_W, toeplitz(h, i - 1, 0), toeplitz(h, i, GRID_W))
            table_ref[h, i] = jnp.where(col_in, t, neg)


def _na_kernel(q_ref, k_ref, v_ref, kc_ref, vc_ref, rpb_ref, o_ref, table_ref):
    b = pl.program_id(0)
    g = pl.program_id(1)

    @pl.when((b == 0) & (g == 0))
    def _():
        _na_build_bias_table(rpb_ref, table_ref)

    win_row0 = jnp.where(g < NA_GROUPS // 2, 0, ROWS - NA_K_ROWS)
    start = pl.multiple_of(win_row0 * GRID_W, GRID_W)
    q_row = g * NA_Q_ROWS + lax.broadcasted_iota(jnp.int32, (NA_Q, 1), 0) // GRID_W
    k_row = win_row0 + lax.broadcasted_iota(jnp.int32, (1, NA_K), 1) // GRID_W
    row_start = jnp.clip(q_row - KH // 2, 0, ROWS - KH)
    row_in = (k_row >= row_start) & (k_row < row_start + KH)
    masks = _head_masks()

    def pair_slab(p):
        return slice(2 * HEAD_DIM * p, 2 * HEAD_DIM * (p + 1))

    def scores(head):
        p, e = divmod(head, 2)
        sl = pair_slab(p)
        q2 = q_ref[:, sl]
        klt = k_ref[pl.ds(start, NA_K), sl].T
        kct = kc_ref[0, sl, :].astype(BF16)
        bias_rows = []
        for a in range(NA_Q_ROWS):
            tiles = []
            for m in range(NA_K_ROWS // 2):
                dr = win_row0 + 2 * m - (g * NA_Q_ROWS + a) + (WIN_H - 1)
                tiles.append(table_ref[head, jnp.clip(dr + 1, 0, N_DR_PAIRS - 1)])
            bias_rows.append(jnp.concatenate(tiles, axis=1))
        bias = jnp.concatenate(bias_rows, axis=0)
        qm = jnp.where(masks[e], q2, jnp.zeros_like(q2))
        s_loc = jnp.where(row_in, jnp.dot(qm, klt, preferred_element_type=F32) + bias, -jnp.inf)
        s_ctx = jnp.dot(qm, kct, preferred_element_type=F32)
        return s_loc, s_ctx

    def attend(head, s_loc, s_ctx):
        sl = pair_slab(head // 2)
        vl = v_ref[pl.ds(start, NA_K), sl]
        vct = vc_ref[0, sl, :].astype(BF16)
        m_max = jnp.maximum(jnp.max(s_loc, axis=-1, keepdims=True),
                            jnp.max(s_ctx, axis=-1, keepdims=True))
        p_loc = jnp.exp(s_loc - m_max)
        p_ctx = jnp.exp(s_ctx - m_max)
        den = jnp.sum(p_loc, axis=-1, keepdims=True) + jnp.sum(p_ctx, axis=-1, keepdims=True)
        return (jnp.dot(p_loc.astype(BF16), vl, preferred_element_type=F32)
                + lax.dot_general(p_ctx.astype(BF16), vct, (((1,), (1,)), ((), ())),
                                  preferred_element_type=F32)) / den

    pending = scores(0)
    out = None
    for head in range(NA_HEADS):
        current = pending
        if head + 1 < NA_HEADS:
            pending = scores(head + 1)
        o = attend(head, *current)
        if head % 2 == 0:
            out = o
        else:
            o_ref[:, pair_slab(head // 2)] = jnp.where(masks[1], o, out).astype(BF16)


def _na_attn(q, k, v, k_ctx, v_ctx, rpb_e):
    smp_blk0 = N_CTX_TOK // DEC_SEQ
    q_blk0 = N_CTX_TOK // NA_Q
    kv = pl.BlockSpec((DEC_SEQ, NA_WIDTH), lambda b, g: (smp_blk0 + b, 0))
    ctx = pl.BlockSpec((1, NA_WIDTH, PAST_LEN), lambda b, g: (b, 0, 0))
    return pl.pallas_call(
        _na_kernel,
        grid=(DEC_BATCH, NA_GROUPS),
        in_specs=[
            pl.BlockSpec((NA_Q, NA_WIDTH), lambda b, g: (q_blk0 + b * NA_GROUPS + g, 0)),
            kv, kv, ctx, ctx,
            pl.BlockSpec((NA_HEADS, N_DR, N_DC), lambda b, g: (0, 0, 0)),
        ],
        out_specs=pl.BlockSpec((NA_Q, NA_WIDTH), lambda b, g: (b * NA_GROUPS + g, 0)),
        out_shape=jax.ShapeDtypeStruct((N_SMP_TOK, NA_WIDTH), BF16),
        scratch_shapes=[pltpu.VMEM((NA_HEADS, N_DR_PAIRS, GRID_W, LANES), F32)],
        compiler_params=_cparams(2),
        name="neighbourhood_attention",
    )(q, k, v, k_ctx, v_ctx, rpb_e)


def _gelu_tanh(x):
    c0 = float(np.sqrt(2.0 / np.pi))
    inner = x * (c0 + (c0 * 0.044715) * (x * x))
    return (0.5 * x) * (1.0 + jnp.tanh(inner))


def _lru_build_gate_weights(wr_ref, wi_ref, w_scr):
    blocks_per_group = LRU_SUB // LRU_BLOCK
    w_scr[...] = jnp.zeros_like(w_scr)
    for d in range(2):
        for kind, w_ref in enumerate((wr_ref, wi_ref)):
            col0 = (2 * d + kind) * LRU_SUB
            for blk in range(LRU_BLOCKS):
                c, n = divmod(blk, blocks_per_group)
                r0 = n * LRU_BLOCK
                w_scr[c, r0:r0 + LRU_BLOCK, col0 + r0:col0 + r0 + LRU_BLOCK] = (
                    0.5 * w_ref[d * LRU_BLOCKS + blk]).astype(BF16)


def _lru_kernel(*refs, zero_state, t_len):
    refs = list(refs)
    xb_ref, gb_ref, cw_ref, cb_ref, wr_ref, wi_ref, br_ref, bi_ref, lam_ref = refs[:9]
    h0f_ref, h0b_ref = (None, None) if zero_state else refs[9:11]
    y_ref, hlf_ref, hlb_ref, af_ref, uf_ref, ab_ref, ub_ref, w_scr = refs[-8:]
    width = xb_ref.shape[1]
    n_seq = xb_ref.shape[0] // t_len
    n_blk = t_len // SUBLANES
    row = lax.broadcasted_iota(jnp.int32, (t_len, 1), 0)
    in_block = lax.broadcasted_iota(jnp.int32, (1, SUBLANES, 1), 1)

    @pl.when(pl.program_id(0) == 0)
    def _():
        _lru_build_gate_weights(wr_ref, wi_ref, w_scr)

    def shifted(z, s):
        rolled = pltpu.roll(z, (-s) % t_len, axis=0)
        ok = (row + s >= 0) & (row + s < t_len)
        return jnp.where(ok, rolled, 0.0)

    left = (CONV_W - 1) // 2
    for q, c in [(q, c) for q in range(n_seq) for c in range(width // LRU_SUB)]:
        rows = slice(q * t_len, (q + 1) * t_len)
        cs = slice(c * LRU_SUB, (c + 1) * LRU_SUB)
        x = xb_ref[rows, cs]
        xc = cb_ref[:, cs]
        for j in range(CONV_W):
            tap = x if j == left else shifted(x, j - left)
            xc = xc + tap * cw_ref[j:j + 1, cs]
        half_gates = jnp.dot(xc.astype(BF16), w_scr[c], preferred_element_type=F32)
        half_xc = 0.5 * xc
        for d, (a_ref, u_ref) in enumerate(((af_ref, uf_ref), (ab_ref, ub_ref))):
            t_r = jnp.tanh(half_gates[:, (2 * d) * LRU_SUB:(2 * d + 1) * LRU_SUB] + 0.5 * br_ref[d:d + 1, cs])
            t_i = jnp.tanh(half_gates[:, (2 * d + 1) * LRU_SUB:(2 * d + 2) * LRU_SUB] + 0.5 * bi_ref[d:d + 1, cs])
            lam = lam_ref[d:d + 1, cs]
            log_sig = jnp.minimum(lam, 0.0) - jnp.log1p(jnp.exp(-jnp.abs(lam)))
            half_c_log_sig = (0.5 * LRU_C) * log_sig
            log_a = t_r * half_c_log_sig + half_c_log_sig
            a = jnp.exp(log_a)
            var = -jnp.tanh(log_a) * (a * a + 1.0)
            u = jnp.where(var > 0.0, var * lax.rsqrt(var), 0.0) * ((t_i + 1.0) * half_xc)
            a = a.reshape(n_blk, SUBLANES, LRU_SUB)
            u = u.reshape(n_blk, SUBLANES, LRU_SUB)
            step = 1
            while step < SUBLANES:
                if d == 0:
                    ok, shift = in_block >= step, step
                else:
                    ok, shift = in_block < SUBLANES - step, SUBLANES - step
                a_prev = jnp.where(ok, pltpu.roll(a, shift, axis=1), 1.0)
                u_prev = jnp.where(ok, pltpu.roll(u, shift, axis=1), 0.0)
                u = u + a * u_prev
                a = a * a_prev
                step *= 2
            a_ref[rows, cs] = a.reshape(t_len, LRU_SUB)
            u_ref[rows, cs] = u.reshape(t_len, LRU_SUB)

    def body(i, carry):
        new = []
        for q, (cf, cb) in enumerate(carry):
            f0 = pl.multiple_of(q * t_len + i * SUBLANES, SUBLANES)
            b0 = pl.multiple_of(q * t_len + (n_blk - 1 - i) * SUBLANES, SUBLANES)
            hf = uf_ref[pl.ds(f0, SUBLANES), :] + af_ref[pl.ds(f0, SUBLANES), :] * cf
            hb = ub_ref[pl.ds(b0, SUBLANES), :] + ab_ref[pl.ds(b0, SUBLANES), :] * cb
            uf_ref[pl.ds(f0, SUBLANES), :] = hf
            ub_ref[pl.ds(b0, SUBLANES), :] = hb
            new.append((jnp.broadcast_to(hf[SUBLANES - 1:SUBLANES, :], (SUBLANES, width)),
                        jnp.broadcast_to(hb[0:1, :], (SUBLANES, width))))
        return tuple(new)

    if zero_state:
        init = tuple((jnp.zeros((SUBLANES, width), F32),) * 2 for _ in range(n_seq))
    else:
        init = tuple((jnp.broadcast_to(h0f_ref[q], (SUBLANES, width)),
                      jnp.broadcast_to(h0b_ref[q], (SUBLANES, width))) for q in range(n_seq))
    last = lax.fori_loop(0, n_blk, body, init)
    for q, (cf, cb) in enumerate(last):
        hlf_ref[q] = cf[0:1, :]
        hlb_ref[q] = cb[0:1, :]
    y_ref[...] = ((uf_ref[...] + ub_ref[...]) * _gelu_tanh(gb_ref[...])).astype(BF16)


def _lru(xb, gb, conv_w, conv_b, w_r, b_r, w_i, b_i, lam, h0, n_seq, t_len, tok_blk0, seq_per_step):
    rows = seq_per_step * t_len
    const2 = lambda s: (0, 0)
    const3 = lambda s: (0, 0, 0)
    blocks = pl.BlockSpec((2 * LRU_BLOCKS, LRU_BLOCK, LRU_BLOCK), const3)
    per_dir = pl.BlockSpec((2, LRU_WIDTH), const2)
    state = pl.BlockSpec((seq_per_step, 1, LRU_WIDTH), lambda s: (s, 0, 0))
    state_shape = jax.ShapeDtypeStruct((n_seq, 1, LRU_WIDTH), F32)
    return pl.pallas_call(
        functools.partial(_lru_kernel, zero_state=h0 is None, t_len=t_len),
        grid=(n_seq // seq_per_step,),
        in_specs=[
            pl.BlockSpec((rows, LRU_WIDTH), lambda s: (tok_blk0 + s, 0)),
            pl.BlockSpec((rows, LRU_WIDTH), lambda s: (tok_blk0 + s, 0)),
            pl.BlockSpec((CONV_W, LRU_WIDTH), const2),
            pl.BlockSpec((1, LRU_WIDTH), const2),
            blocks, blocks, per_dir, per_dir, per_dir,
        ] + ([] if h0 is None else [state, state]),
        out_specs=[pl.BlockSpec((rows, LRU_WIDTH), lambda s: (s, 0)), state, state],
        out_shape=[jax.ShapeDtypeStruct((n_seq * t_len, LRU_WIDTH), BF16), state_shape, state_shape],
        scratch_shapes=[pltpu.VMEM((rows, LRU_WIDTH), F32)] * 4
        + [pltpu.VMEM((LRU_WIDTH // LRU_SUB, LRU_SUB, 4 * LRU_SUB), BF16)],
        compiler_params=_cparams(1),
        name="rglru",
    )(xb, gb, conv_w, conv_b.reshape(1, LRU_WIDTH),
      w_r.reshape(2 * LRU_BLOCKS, LRU_BLOCK, LRU_BLOCK), w_i.reshape(2 * LRU_BLOCKS, LRU_BLOCK, LRU_BLOCK),
      b_r, b_i, lam, *(() if h0 is None else h0))


def _fourier_kernel(x_ref, mod_ref, g_ref, cs_ref, ct_ref, w_ref, o_ref, w_bf_ref, *, mod_row0, t_len):
    @pl.when(pl.program_id(0) == 0)
    def _():
        w_bf_ref[...] = w_ref[...].astype(BF16)

    x = x_ref[...]
    row = mod_row0 + pl.program_id(0) if mod_row0 else 0
    h = _norm_mod(x, g_ref[...], _mod_vec(mod_ref, row, 3), _mod_vec(mod_ref, row, 4)).astype(BF16)
    cos_parts, sin_parts = [], []
    for g in range(FOURIER_GROUPS):
        ab = jnp.dot(h[:, g * GROUP_W:(g + 1) * GROUP_W], cs_ref[...], preferred_element_type=F32)
        cos_parts.append(ab[:, :GROUP_W])
        sin_parts.append(ab[:, GROUP_W:])
    cos_all = jnp.concatenate(cos_parts, axis=1).astype(BF16)
    sin_all = jnp.concatenate(sin_parts, axis=1).astype(BF16)
    f_parts = []
    for q in range(x.shape[0] // t_len):
        rows = slice(q * t_len, (q + 1) * t_len)
        stacked = jnp.concatenate([cos_all[rows], sin_all[rows]], axis=0)
        f_parts.append(jnp.dot(ct_ref[...], stacked, preferred_element_type=F32))
    f = jnp.concatenate(f_parts, axis=0) * ((t_len * GROUP_W) ** -0.5)
    y = jnp.dot(f.astype(BF16), w_bf_ref[...], preferred_element_type=F32)
    o_ref[...] = x + _mod_vec(mod_ref, row, 5) * y


def _dft_tables(t_len):
    def cos_sin(n):
        jk = np.outer(np.arange(n), np.arange(n)) % n
        ang = 2.0 * np.pi * jk.astype(np.float64) / n
        return np.cos(ang), np.sin(ang)

    cc, sc = cos_sin(GROUP_W)
    ct, st = cos_sin(t_len)
    chan = jnp.asarray(np.concatenate([cc, sc], axis=1).astype(np.float32)).astype(BF16)
    time = jnp.asarray(np.concatenate([ct, -st], axis=1).astype(np.float32)).astype(BF16)
    return chan, time


def _fourier(x, mod, g, w_out, n_seq, t_len, tok_blk0, mod_row0, seq_per_step):
    assert seq_per_step == 1 or mod_row0 == 0
    chan, time = _dft_tables(t_len)
    rows = seq_per_step * t_len
    seq = lambda s: (tok_blk0 + s, 0)
    const = lambda s: (0, 0)
    return pl.pallas_call(
        functools.partial(_fourier_kernel, mod_row0=mod_row0, t_len=t_len),
        grid=(n_seq // seq_per_step,),
        in_specs=[
            pl.BlockSpec((rows, D_MODEL), seq),
            _mod_spec(),
            pl.BlockSpec((1, D_MODEL), const),
            _resident((GROUP_W, 2 * GROUP_W), const),
            _resident((t_len, 2 * t_len), const),
            _resident((D_MODEL, D_MODEL), const),
        ],
        out_specs=pl.BlockSpec((rows, D_MODEL), seq),
        out_shape=jax.ShapeDtypeStruct((N_TOK, D_MODEL), F32),
        input_output_aliases={0: 0},
        scratch_shapes=[pltpu.VMEM((D_MODEL, D_MODEL), BF16)],
        compiler_params=_cparams(1),
        name="fourier_mixer",
    )(x, mod, g.reshape(1, D_MODEL), chan, time, w_out)


def _cache_layout(t):
    return jnp.transpose(t.reshape(BATCH, 1, NA_HEADS, HEAD_DIM, SEQ), (0, 1, 4, 2, 3))


def kernel(x_prompt, x_sample, cache_k, cache_v, state_lru_fwd, state_lru_bwd, c, c_ctx, w_ada, b_ada, norm_g, ffn1_gate, ffn1_up, ffn1_down, ffn2_gate, ffn2_up, ffn2_down, w_in, q_norm_g, k_norm_g, rpb, conv_w, conv_b, lru_w_r, lru_b_r, lru_w_i, lru_b_i, lru_lambda, w_out_ab, w_out_c):
    assert DEPTH == 2, "one neighbourhood/RG-LRU layer followed by one Fourier layer"
    c_ctx2 = c_ctx.reshape(1, D_MODEL)
    mod0 = _adaln(c_ctx2, c, w_ada, b_ada, 0)

    ffn1 = (ffn1_gate, ffn1_up, ffn1_down)
    ffn2 = (ffn2_gate, ffn2_up, ffn2_down)

    x, mod1 = _ffn((x_prompt.reshape(N_CTX_TOK, D_MODEL), x_sample.reshape(N_SMP_TOK, D_MODEL)),
                   mod0, norm_g[0, 0], *ffn1, 0, 0, adaln_next=(c_ctx2, c, w_ada, b_ada, 1))
    q, k, v, xb, gb, new_k, new_v = _proj(x, mod0, norm_g[0, 1], w_in[0], q_norm_g[0], k_norm_g[0])
    o_ctx = _ctx_attn(q, k, v)
    o_smp = _na_attn(q, k, v,
                     jnp.transpose(cache_k[:, 0], (0, 2, 3, 1)).reshape(DEC_BATCH, NA_WIDTH, PAST_LEN),
                     jnp.transpose(cache_v[:, 0], (0, 2, 3, 1)).reshape(DEC_BATCH, NA_WIDTH, PAST_LEN), rpb[0])
    lru_prm = (conv_w[0], conv_b[0], lru_w_r[0], lru_b_r[0], lru_w_i[0], lru_b_i[0], lru_lambda[0])
    yb_ctx, new_hf, new_hb = _lru(xb, gb, *lru_prm, None, BATCH, SEQ, 0, CTX_SEQ_PER_STEP)
    yb_smp, _, _ = _lru(xb, gb, *lru_prm, (state_lru_fwd, state_lru_bwd),
                        DEC_BATCH, DEC_SEQ, N_CTX_TOK // DEC_SEQ, 1)
    (x,) = _ffn((x,), mod0, norm_g[0, 2], *ffn2, 0, 6, mixer_out=(o_ctx, o_smp, yb_ctx, yb_smp, w_out_ab))

    (x,) = _ffn((x,), mod1, norm_g[1, 0], *ffn1, 1, 0)
    x = _fourier(x, mod1, norm_g[1, 1], w_out_c[0], BATCH, SEQ, 0, 0, CTX_SEQ_PER_STEP)
    x = _fourier(x, mod1, norm_g[1, 1], w_out_c[0], DEC_BATCH, DEC_SEQ, N_CTX_TOK // DEC_SEQ, 1, 1)
    y_prompt, y_sample = _ffn((x,), mod1, norm_g[1, 2], *ffn2, 1, 6, split_out=True)

    return (y_prompt.reshape(BATCH, SEQ, D_MODEL), y_sample.reshape(DEC_BATCH, DEC_SEQ, D_MODEL),
            _cache_layout(new_k), _cache_layout(new_v),
            new_hf, new_hb)
```

```python
import functools

import numpy as np
import jax
import jax.numpy as jnp
from jax import lax
from jax.experimental import pallas as pl
from jax.experimental.pallas import tpu as pltpu

F32 = jnp.float32
BF16 = jnp.bfloat16

D_MODEL = 1024
BATCH = 16
SEQ = 256
DEPTH = 2
DEC_BATCH = 2
DEC_SEQ = 1024
PAST_LEN = 256
GRID_W = 64
HEAD_DIM = 64
NA_WIDTH = 512
NA_HEADS = 8
WIN_H = 8
WIN_W = 16
LRU_WIDTH = 512
LRU_BLOCKS = 8
LRU_BLOCK = 64
LRU_C = 8.0
LRU_SUB = 256
CONV_W = 4
FOURIER_GROUPS = 4
GROUP_W = D_MODEL // FOURIER_GROUPS
D_FF = 2816
N_MOD = 9
IN_WIDTH = 3 * NA_WIDTH + 2 * LRU_WIDTH
EPS = 1e-6

N_CTX_TOK = BATCH * SEQ
N_SMP_TOK = DEC_BATCH * DEC_SEQ
N_TOK = N_CTX_TOK + N_SMP_TOK
MOD_ROWS = 8
MOD_WIDTH = N_MOD * D_MODEL
ROWS = DEC_SEQ // GRID_W
KH = min(WIN_H, ROWS)

TOKEN_TILE = 512
N_CTX_TILES = N_CTX_TOK // TOKEN_TILE
CTX_SEQ_PER_STEP = 2
CTX_ATTN_SEQ_PER_STEP = 4
FFN_TILE = 512
FF_TILE = 256
FF_CHUNKS = D_FF // FF_TILE
FF_STAGE_SLOTS = 2
SUBLANES = 8
LANES = 128
VMEM_LIMIT = 56 * 1024 * 1024

NA_Q_ROWS = 4
NA_GROUPS = ROWS // NA_Q_ROWS
NA_K_ROWS = 12
NA_Q = NA_Q_ROWS * GRID_W
NA_K = NA_K_ROWS * GRID_W
N_DR = 2 * WIN_H - 1
N_DC = 2 * WIN_W - 1
N_DR_PAIRS = N_DR + 1


def _cparams(n_axes):
    return pltpu.CompilerParams(
        dimension_semantics=("arbitrary",) * n_axes, vmem_limit_bytes=VMEM_LIMIT)


def _resident(block_shape, index_map):
    return pl.BlockSpec(block_shape, index_map, pipeline_mode=pl.Buffered(1))


def _mod_spec():
    return _resident((MOD_ROWS, MOD_WIDTH), lambda i: (0, 0))


def _mod_row_of_tile(i, tile=TOKEN_TILE):
    n_ctx_tiles = N_CTX_TOK // tile
    tiles_per_seq = DEC_SEQ // tile
    return jnp.where(i < n_ctx_tiles, 0, 1 + (i - n_ctx_tiles) // tiles_per_seq)


def _mod_vec(mod_ref, row, k):
    return mod_ref[pl.ds(row, 1), k * D_MODEL:(k + 1) * D_MODEL]


def _norm_mod(x, g, shift, scale):
    ms = jnp.mean(x * x, axis=-1, keepdims=True)
    return (x * lax.rsqrt(ms + EPS)) * (g * (1.0 + scale)) + shift


def _adaln_slab(cctx_ref, c_ref, w_ref, b_ref, cond_ref, layer):
    cond_ref[...] = jnp.zeros_like(cond_ref)
    cond_ref[0:1, :] = cctx_ref[...]
    cond_ref[1:1 + DEC_BATCH, :] = c_ref[...]
    cond = cond_ref[...]
    s = (cond * jax.nn.sigmoid(cond)).astype(BF16)
    return jnp.dot(s, w_ref[...].astype(BF16), preferred_element_type=F32) + b_ref[layer:layer + 1, :]


def _adaln_specs(layer, slab):
    return ([pl.BlockSpec((1, D_MODEL), lambda i: (0, 0)),
             pl.BlockSpec((DEC_BATCH, D_MODEL), lambda i: (0, 0)),
             pl.BlockSpec((None, D_MODEL, slab), lambda i: (layer, 0, i)),
             pl.BlockSpec((DEPTH, slab), lambda i: (0, i))],
            pl.BlockSpec((MOD_ROWS, slab), lambda i: (0, i)))


def _adaln_kernel(cctx_ref, c_ref, w_ref, b_ref, o_ref, cond_ref, *, layer):
    o_ref[...] = _adaln_slab(cctx_ref, c_ref, w_ref, b_ref, cond_ref, layer)


def _adaln(c_ctx, c, w_ada, b_ada, layer):
    slab = MOD_WIDTH // 4
    in_specs, out_spec = _adaln_specs(layer, slab)
    return pl.pallas_call(
        functools.partial(_adaln_kernel, layer=layer),
        grid=(MOD_WIDTH // slab,),
        in_specs=in_specs,
        out_specs=out_spec,
        out_shape=jax.ShapeDtypeStruct((MOD_ROWS, MOD_WIDTH), F32),
        scratch_shapes=[pltpu.VMEM((MOD_ROWS, D_MODEL), F32)],
        compiler_params=_cparams(1),
        name="adaln",
    )(c_ctx, c, w_ada, b_ada)


def _ffn_weight_copy(w_hbm, stage_ref, sem_ref, layer, j, ff_axis):
    ff = pl.ds(j * FF_TILE, FF_TILE)
    src = w_hbm.at[layer, :, ff] if ff_axis == 1 else w_hbm.at[layer, ff, :]
    slot = j % FF_STAGE_SLOTS
    return pltpu.make_async_copy(src, stage_ref.at[slot], sem_ref.at[slot])


def _mixer_out_copy(w_hbm, stage_ref, sem_ref, j):
    rows = stage_ref.shape[1]
    slot = j % stage_ref.shape[0]
    return pltpu.make_async_copy(w_hbm.at[0, pl.ds(j * rows, rows), :], stage_ref.at[slot], sem_ref.at[slot])


def _ffn_kernel(*refs, layer, mod_base, split_in, split_out, mixer_out, adaln_next):
    refs = list(refs)
    take = lambda n: [refs.pop(0) for _ in range(n)]
    x_refs = take(2 if split_in else 1)
    mix_refs = take(4) if mixer_out else None
    mod_ref, g_ref = take(2)
    wo_hbm = take(1)[0] if mixer_out else None
    ada_refs = take(4) if adaln_next is not None else None
    wg_hbm, wu_hbm, wd_hbm = take(3)
    o_refs = take(2 if split_out else 1)
    modn_ref = take(1)[0] if adaln_next is not None else None
    wg_bf, wu_bf, wd_bf, stg_g, stg_u, stg_d, sem_g, sem_u, sem_d = take(9)
    cond_scr = take(1)[0] if adaln_next is not None else None
    streams = ((wg_hbm, stg_g, sem_g, wg_bf, 1), (wu_hbm, stg_u, sem_u, wu_bf, 1),
               (wd_hbm, stg_d, sem_d, wd_bf, 0))

    i = pl.program_id(0)
    is_ctx = i < N_CTX_TOK // FFN_TILE
    if split_in:
        x = jnp.where(is_ctx, x_refs[0][...], x_refs[1][...])
    else:
        x = x_refs[0][...]
    row = _mod_row_of_tile(i, FFN_TILE)

    if mixer_out:
        wo_bf, stg_o, sem_o = take(3)
        rows = stg_o.shape[1]
        n_chunks = D_MODEL // rows

        @pl.when(i == 0)
        def _():
            for j in range(stg_o.shape[0]):
                _mixer_out_copy(wo_hbm, stg_o, sem_o, j).start()
            for j in range(n_chunks):
                _mixer_out_copy(wo_hbm, stg_o, sem_o, j).wait()
                wo_bf[j * rows:(j + 1) * rows, :] = stg_o[j % stg_o.shape[0]].astype(BF16)
                if j + stg_o.shape[0] < n_chunks:
                    _mixer_out_copy(wo_hbm, stg_o, sem_o, j + stg_o.shape[0]).start()

        oc_ref, os_ref, yc_ref, ys_ref = mix_refs
        cat = jnp.concatenate([jnp.where(is_ctx, oc_ref[...], os_ref[...]),
                               jnp.where(is_ctx, yc_ref[...], ys_ref[...])], axis=1)
        x = x + _mod_vec(mod_ref, row, mod_base - 1) * jnp.dot(cat, wo_bf[...], preferred_element_type=F32)

    h = _norm_mod(x, g_ref[...], _mod_vec(mod_ref, row, mod_base),
                  _mod_vec(mod_ref, row, mod_base + 1)).astype(BF16)

    def start_chunk(j):
        for w_hbm, stg, sem, _, ff_axis in streams:
            _ffn_weight_copy(w_hbm, stg, sem, layer, j, ff_axis).start()

    def finish_chunk(j):
        for w_hbm, stg, sem, w_bf, ff_axis in streams:
            _ffn_weight_copy(w_hbm, stg, sem, layer, j, ff_axis).wait()
            w_bf[j] = stg[j % FF_STAGE_SLOTS].astype(BF16)

    def run(stream_weights):
        if stream_weights:
            for j in range(FF_STAGE_SLOTS):
                start_chunk(j)
        acc = None
        for j in range(FF_CHUNKS):
            if stream_weights:
                finish_chunk(j)
                if j + FF_STAGE_SLOTS < FF_CHUNKS:
                    start_chunk(j + FF_STAGE_SLOTS)
            a = jnp.dot(h, wg_bf[j], preferred_element_type=F32)
            b = jnp.dot(h, wu_bf[j], preferred_element_type=F32)
            if adaln_next is not None and j == 1:
                modn_ref[...] = _adaln_slab(*ada_refs, cond_scr, adaln_next)
            act = (a * jax.nn.sigmoid(a) * b).astype(BF16)
            y = jnp.dot(act, wd_bf[j], preferred_element_type=F32)
            acc = y if acc is None else acc + y
        res = x + 0.5 * _mod_vec(mod_ref, row, mod_base + 2) * acc
        if split_out:
            @pl.when(is_ctx)
            def _():
                o_refs[0][...] = res

            @pl.when(jnp.logical_not(is_ctx))
            def _():
                o_refs[1][...] = res
        else:
            o_refs[0][...] = res

    @pl.when(i == 0)
    def _():
        run(True)

    @pl.when(i > 0)
    def _():
        run(False)


def _ffn(xs, mod, g, wg, wu, wd, layer, mod_base, split_out=False, mixer_out=None, adaln_next=None):
    tm = FFN_TILE
    n_ctx_tiles = N_CTX_TOK // tm
    split_in = len(xs) == 2

    def tiles(width):
        return (pl.BlockSpec((tm, width), lambda i: (i, 0)),
                pl.BlockSpec((tm, width), lambda i: (jnp.minimum(i, n_ctx_tiles - 1), 0)),
                pl.BlockSpec((tm, width), lambda i: (jnp.maximum(i - n_ctx_tiles, 0), 0)))

    tok, ctx_tok, smp_tok = tiles(D_MODEL)
    full = jax.ShapeDtypeStruct((N_TOK, D_MODEL), F32)
    pair = [jax.ShapeDtypeStruct((N_CTX_TOK, D_MODEL), F32), jax.ShapeDtypeStruct((N_SMP_TOK, D_MODEL), F32)]
    hbm = pl.BlockSpec(memory_space=pl.ANY)
    in_specs = [ctx_tok, smp_tok] if split_in else [tok]
    operands = list(xs)
    scratch = [
        pltpu.VMEM((FF_CHUNKS, D_MODEL, FF_TILE), BF16),
        pltpu.VMEM((FF_CHUNKS, D_MODEL, FF_TILE), BF16),
        pltpu.VMEM((FF_CHUNKS, FF_TILE, D_MODEL), BF16),
        pltpu.VMEM((FF_STAGE_SLOTS, D_MODEL, FF_TILE), F32),
        pltpu.VMEM((FF_STAGE_SLOTS, D_MODEL, FF_TILE), F32),
        pltpu.VMEM((FF_STAGE_SLOTS, FF_TILE, D_MODEL), F32),
        pltpu.SemaphoreType.DMA((FF_STAGE_SLOTS,)),
        pltpu.SemaphoreType.DMA((FF_STAGE_SLOTS,)),
        pltpu.SemaphoreType.DMA((FF_STAGE_SLOTS,)),
    ]
    if mixer_out is not None:
        _, ctx_half, smp_half = tiles(NA_WIDTH)
        in_specs += [ctx_half, smp_half, ctx_half, smp_half]
        operands += list(mixer_out[:4])
    in_specs += [_mod_spec(), pl.BlockSpec((1, D_MODEL), lambda i: (0, 0))]
    operands += [mod, g.reshape(1, D_MODEL)]
    out_specs = [ctx_tok, smp_tok] if split_out else [tok]
    out_shape = pair if split_out else [full]
    if adaln_next is not None:
        ada_in, ada_out = _adaln_specs(adaln_next[4], MOD_WIDTH // (N_TOK // tm))
        in_specs += ada_in
        operands += list(adaln_next[:4])
        out_specs.append(ada_out)
        out_shape.append(jax.ShapeDtypeStruct((MOD_ROWS, MOD_WIDTH), F32))
        scratch.append(pltpu.VMEM((MOD_ROWS, D_MODEL), F32))
    if mixer_out is not None:
        in_specs.insert(len(in_specs) - (4 if adaln_next is not None else 0), hbm)
        operands.insert(len(operands) - (4 if adaln_next is not None else 0), mixer_out[4])
        scratch += [
            pltpu.VMEM((D_MODEL, D_MODEL), BF16),
            pltpu.VMEM((FF_STAGE_SLOTS, FF_TILE, D_MODEL), F32),
            pltpu.SemaphoreType.DMA((FF_STAGE_SLOTS,)),
        ]
    return pl.pallas_call(
        functools.partial(_ffn_kernel, layer=layer, mod_base=mod_base, split_in=split_in,
                          split_out=split_out, mixer_out=mixer_out is not None,
                          adaln_next=None if adaln_next is None else adaln_next[4]),
        grid=(N_TOK // tm,),
        in_specs=in_specs + [hbm, hbm, hbm],
        out_specs=out_specs,
        out_shape=out_shape,
        scratch_shapes=scratch,
        compiler_params=_cparams(1),
        name="ffn",
    )(*operands, wg, wu, wd)


def _head_rms_norm(z, g, ones_bd):
    z2 = z * z
    hi = z2.astype(BF16)
    lo = (z2 - hi.astype(F32)).astype(BF16)
    n = ones_bd.shape[0]
    parts = []
    for c in range(z.shape[1] // n):
        sl = slice(c * n, (c + 1) * n)
        parts.append(jnp.dot(hi[:, sl], ones_bd, preferred_element_type=F32)
                     + jnp.dot(lo[:, sl], ones_bd, preferred_element_type=F32))
    ss = jnp.concatenate(parts, axis=1)
    return z * lax.rsqrt(ss * (1.0 / HEAD_DIM) + EPS) * g


def _proj_kernel(x_ref, mod_ref, g_ref, w_ref, qg_ref, kg_ref, ones_ref,
                 q_ref, k_ref, v_ref, xb_ref, gb_ref, kout_ref, vout_ref, w_bf_ref):
    i = pl.program_id(0)

    @pl.when(i == 0)
    def _():
        w_bf_ref[...] = w_ref[...].astype(BF16)

    x = x_ref[...]
    row = _mod_row_of_tile(i)
    h = _norm_mod(x, g_ref[...], _mod_vec(mod_ref, row, 3), _mod_vec(mod_ref, row, 4)).astype(BF16)

    def proj(part):
        return jnp.dot(h, w_bf_ref[:, part * NA_WIDTH:(part + 1) * NA_WIDTH], preferred_element_type=F32)

    ones_bd = ones_ref[...]
    q_raw = proj(0)
    k_raw = proj(1)
    q = _head_rms_norm(q_raw, jnp.tile(qg_ref[...], (1, NA_HEADS)), ones_bd) * (HEAD_DIM ** -0.5)
    q_ref[...] = q.astype(BF16)
    v = proj(2)
    k = _head_rms_norm(k_raw, jnp.tile(kg_ref[...], (1, NA_HEADS)), ones_bd)
    k_ref[...] = k.astype(BF16)
    xb = proj(3)
    v_ref[...] = v.astype(BF16)
    gb = proj(4)
    xb_ref[...] = xb
    gb_ref[...] = gb

    kt = [k[b * SEQ:(b + 1) * SEQ, :].T for b in range(TOKEN_TILE // SEQ)]
    vt = [v[b * SEQ:(b + 1) * SEQ, :].T for b in range(TOKEN_TILE // SEQ)]

    @pl.when(i < N_CTX_TILES)
    def _():
        for b in range(TOKEN_TILE // SEQ):
            kout_ref[b] = kt[b]
            vout_ref[b] = vt[b]


def _proj(x, mod, g, w_in, q_g, k_g):
    tm = TOKEN_TILE
    head = np.arange(2 * LANES) // HEAD_DIM
    ones_bd = jnp.asarray((head[:, None] == head[None, :]).astype(np.float32), dtype=BF16)
    tok = lambda i: (i, 0)
    const = lambda i: (0, 0)
    act_f32 = jax.ShapeDtypeStruct((N_TOK, NA_WIDTH), F32)
    act_bf16 = jax.ShapeDtypeStruct((N_TOK, NA_WIDTH), BF16)
    cache = jax.ShapeDtypeStruct((BATCH, NA_WIDTH, SEQ), F32)
    cache_spec = pl.BlockSpec((tm // SEQ, NA_WIDTH, SEQ), lambda i: (jnp.minimum(i, N_CTX_TILES - 1), 0, 0))
    return pl.pallas_call(
        _proj_kernel,
        grid=(N_TOK // tm,),
        in_specs=[
            pl.BlockSpec((tm, D_MODEL), tok),
            _mod_spec(),
            pl.BlockSpec((1, D_MODEL), const),
            _resident((D_MODEL, IN_WIDTH), const),
            pl.BlockSpec((1, HEAD_DIM), const),
            pl.BlockSpec((1, HEAD_DIM), const),
            _resident((2 * LANES, 2 * LANES), const),
        ],
        out_specs=[pl.BlockSpec((tm, NA_WIDTH), tok)] * 5 + [cache_spec, cache_spec],
        out_shape=[act_bf16, act_bf16, act_bf16, act_f32, act_f32, cache, cache],
        scratch_shapes=[pltpu.VMEM((D_MODEL, IN_WIDTH), BF16)],
        compiler_params=_cparams(1),
        name="mixer_in_proj",
    )(x, mod, g.reshape(1, D_MODEL), w_in, q_g.reshape(1, HEAD_DIM), k_g.reshape(1, HEAD_DIM), ones_bd)


def _head_masks():
    lane = lax.broadcasted_iota(jnp.int32, (1, 2 * HEAD_DIM), 1)
    return [lane < HEAD_DIM, lane >= HEAD_DIM]


def _ctx_attn_kernel(q_ref, k_ref, v_ref, o_ref):
    masks = _head_masks()
    units = [(b, h) for b in range(q_ref.shape[0] // SEQ) for h in range(NA_HEADS)]

    def where(b, h):
        return slice(b * SEQ, (b + 1) * SEQ), slice(2 * HEAD_DIM * (h // 2), 2 * HEAD_DIM * (h // 2 + 1))

    def scores(b, h):
        rows, sl = where(b, h)
        q2 = q_ref[rows, sl]
        qm = jnp.where(masks[h % 2], q2, jnp.zeros_like(q2))
        return jnp.dot(qm, k_ref[rows, sl].T, preferred_element_type=F32)

    def attend(b, h, s):
        rows, sl = where(b, h)
        pe = jnp.exp(s - jnp.max(s, axis=-1, keepdims=True))
        den = jnp.sum(pe, axis=-1, keepdims=True)
        return jnp.dot(pe.astype(BF16), v_ref[rows, sl], preferred_element_type=F32) / den

    pending = scores(*units[0])
    out = None
    for n, (b, h) in enumerate(units):
        current = pending
        if n + 1 < len(units):
            pending = scores(*units[n + 1])
        o = attend(b, h, current)
        if h % 2 == 0:
            out = o
        else:
            rows, sl = where(b, h)
            o_ref[rows, sl] = jnp.where(masks[1], o, out).astype(BF16)


def _ctx_attn(q, k, v):
    blk = pl.BlockSpec((CTX_ATTN_SEQ_PER_STEP * SEQ, NA_WIDTH), lambda b: (b, 0))
    return pl.pallas_call(
        _ctx_attn_kernel,
        grid=(BATCH // CTX_ATTN_SEQ_PER_STEP,),
        in_specs=[blk, blk, blk],
        out_specs=blk,
        out_shape=jax.ShapeDtypeStruct((N_CTX_TOK, NA_WIDTH), BF16),
        compiler_params=_cparams(1),
        name="ctx_attention",
    )(q, k, v)


def _na_build_bias_table(rpb_ref, table_ref):
    qc = lax.broadcasted_iota(jnp.int32, (GRID_W, LANES), 0)
    lane = lax.broadcasted_iota(jnp.int32, (GRID_W, LANES), 1)
    kc = lane % GRID_W
    col_start = jnp.clip(qc - WIN_W // 2, 0, GRID_W - WIN_W)
    col_in = (kc >= col_start) & (kc < col_start + WIN_W)
    neg = jnp.full((GRID_W, LANES), -jnp.inf, F32)

    def toeplitz(h, dr, lane0):
        if dr < 0 or dr >= N_DR:
            return neg
        row = jnp.pad(rpb_ref[dr, h:h + 1, :], ((0, 0), (0, LANES - N_DC)))
        w = jnp.broadcast_to(row, (GRID_W, LANES))
        return pltpu.roll(w, (lane0 - (WIN_W - 1)) % LANES, 1, stride=1, stride_axis=0)

    for h in range(NA_HEADS):
        for i in range(N_DR_PAIRS):
            t = jnp.where(lane < GRID_W, toeplitz(h, i - 1, 0), toeplitz(h, i, GRID_W))
            table_ref[h, i] = jnp.where(col_in, t, neg)


def _na_kernel(q_ref, k_ref, v_ref, kc_ref, vc_ref, rpb_ref, o_ref, table_ref):
    b = pl.program_id(0)
    g = pl.program_id(1)

    @pl.when((b == 0) & (g == 0))
    def _():
        _na_build_bias_table(rpb_ref, table_ref)

    win_row0 = jnp.where(g < NA_GROUPS // 2, 0, ROWS - NA_K_ROWS)
    start = pl.multiple_of(win_row0 * GRID_W, GRID_W)
    q_row = g * NA_Q_ROWS + lax.broadcasted_iota(jnp.int32, (NA_Q, 1), 0) // GRID_W
    k_row = win_row0 + lax.broadcasted_iota(jnp.int32, (1, NA_K), 1) // GRID_W
    row_start = jnp.clip(q_row - KH // 2, 0, ROWS - KH)
    row_in = (k_row >= row_start) & (k_row < row_start + KH)
    masks = _head_masks()

    def pair_slab(p):
        return slice(2 * HEAD_DIM * p, 2 * HEAD_DIM * (p + 1))

    def scores(head):
        p, e = divmod(head, 2)
        sl = pair_slab(p)
        q2 = q_ref[:, sl]
        klt = k_ref[pl.ds(start, NA_K), sl].T
        kct = kc_ref[0, sl, :].astype(BF16)
        bias_rows = []
        for a in range(NA_Q_ROWS):
            tiles = []
            for m in range(NA_K_ROWS // 2):
                dr = win_row0 + 2 * m - (g * NA_Q_ROWS + a) + (WIN_H - 1)
                tiles.append(table_ref[head, jnp.clip(dr + 1, 0, N_DR_PAIRS - 1)])
            bias_rows.append(jnp.concatenate(tiles, axis=1))
        bias = jnp.concatenate(bias_rows, axis=0)
        qm = jnp.where(masks[e], q2, jnp.zeros_like(q2))
        s_loc = jnp.where(row_in, jnp.dot(qm, klt, preferred_element_type=F32) + bias, -jnp.inf)
        s_ctx = jnp.dot(qm, kct, preferred_element_type=F32)
        return s_loc, s_ctx

    def attend(head, s_loc, s_ctx):
        sl = pair_slab(head // 2)
        vl = v_ref[pl.ds(start, NA_K), sl]
        vct = vc_ref[0, sl, :].astype(BF16)
        m_max = jnp.maximum(jnp.max(s_loc, axis=-1, keepdims=True),
                            jnp.max(s_ctx, axis=-1, keepdims=True))
        p_loc = jnp.exp(s_loc - m_max)
        p_ctx = jnp.exp(s_ctx - m_max)
        den = jnp.sum(p_loc, axis=-1, keepdims=True) + jnp.sum(p_ctx, axis=-1, keepdims=True)
        return (jnp.dot(p_loc.astype(BF16), vl, preferred_element_type=F32)
                + lax.dot_general(p_ctx.astype(BF16), vct, (((1,), (1,)), ((), ())),
                                  preferred_element_type=F32)) / den

    pending = scores(0)
    out = None
    for head in range(NA_HEADS):
        current = pending
        if head + 1 < NA_HEADS:
            pending = scores(head + 1)
        o = attend(head, *current)
        if head % 2 == 0:
            out = o
        else:
            o_ref[:, pair_slab(head // 2)] = jnp.where(masks[1], o, out).astype(BF16)


def _na_attn(q, k, v, k_ctx, v_ctx, rpb_e):
    smp_blk0 = N_CTX_TOK // DEC_SEQ
    q_blk0 = N_CTX_TOK // NA_Q
    kv = pl.BlockSpec((DEC_SEQ, NA_WIDTH), lambda b, g: (smp_blk0 + b, 0))
    ctx = pl.BlockSpec((1, NA_WIDTH, PAST_LEN), lambda b, g: (b, 0, 0))
    return pl.pallas_call(
        _na_kernel,
        grid=(DEC_BATCH, NA_GROUPS),
        in_specs=[
            pl.BlockSpec((NA_Q, NA_WIDTH), lambda b, g: (q_blk0 + b * NA_GROUPS + g, 0)),
            kv, kv, ctx, ctx,
            pl.BlockSpec((N_DR, NA_HEADS, N_DC), lambda b, g: (0, 0, 0)),
        ],
        out_specs=pl.BlockSpec((NA_Q, NA_WIDTH), lambda b, g: (b * NA_GROUPS + g, 0)),
        out_shape=jax.ShapeDtypeStruct((N_SMP_TOK, NA_WIDTH), BF16),
        scratch_shapes=[pltpu.VMEM((NA_HEADS, N_DR_PAIRS, GRID_W, LANES), F32)],
        compiler_params=_cparams(2),
        name="neighbourhood_attention",
    )(q, k, v, k_ctx, v_ctx, jnp.transpose(rpb_e, (1, 0, 2)))


def _gelu_tanh(x):
    c0 = float(np.sqrt(2.0 / np.pi))
    inner = x * (c0 + (c0 * 0.044715) * (x * x))
    return (0.5 * x) * (1.0 + jnp.tanh(inner))


def _lru_build_gate_weights(wr_ref, wi_ref, w_scr):
    blocks_per_group = LRU_SUB // LRU_BLOCK
    w_scr[...] = jnp.zeros_like(w_scr)
    for d in range(2):
        for kind, w_ref in enumerate((wr_ref, wi_ref)):
            col0 = (2 * d + kind) * LRU_SUB
            for blk in range(LRU_BLOCKS):
                c, n = divmod(blk, blocks_per_group)
                r0 = n * LRU_BLOCK
                w_scr[c, r0:r0 + LRU_BLOCK, col0 + r0:col0 + r0 + LRU_BLOCK] = (
                    0.5 * w_ref[d * LRU_BLOCKS + blk]).astype(BF16)


def _lru_kernel(*refs, zero_state, t_len):
    refs = list(refs)
    xb_ref, gb_ref, cw_ref, cb_ref, wr_ref, wi_ref, br_ref, bi_ref, lam_ref = refs[:9]
    h0f_ref, h0b_ref = (None, None) if zero_state else refs[9:11]
    y_ref, hlf_ref, hlb_ref, af_ref, uf_ref, ab_ref, ub_ref, w_scr = refs[-8:]
    width = xb_ref.shape[1]
    n_seq = xb_ref.shape[0] // t_len
    n_blk = t_len // SUBLANES
    row = lax.broadcasted_iota(jnp.int32, (t_len, 1), 0)
    in_block = lax.broadcasted_iota(jnp.int32, (1, SUBLANES, 1), 1)

    @pl.when(pl.program_id(0) == 0)
    def _():
        _lru_build_gate_weights(wr_ref, wi_ref, w_scr)

    def shifted(z, s):
        rolled = pltpu.roll(z, (-s) % t_len, axis=0)
        ok = (row + s >= 0) & (row + s < t_len)
        return jnp.where(ok, rolled, 0.0)

    left = (CONV_W - 1) // 2
    for q, c in [(q, c) for q in range(n_seq) for c in range(width // LRU_SUB)]:
        rows = slice(q * t_len, (q + 1) * t_len)
        cs = slice(c * LRU_SUB, (c + 1) * LRU_SUB)
        x = xb_ref[rows, cs]
        xc = cb_ref[:, cs]
        for j in range(CONV_W):
            tap = x if j == left else shifted(x, j - left)
            xc = xc + tap * cw_ref[j:j + 1, cs]
        half_gates = jnp.dot(xc.astype(BF16), w_scr[c], preferred_element_type=F32)
        half_xc = 0.5 * xc
        for d, (a_ref, u_ref) in enumerate(((af_ref, uf_ref), (ab_ref, ub_ref))):
            t_r = jnp.tanh(half_gates[:, (2 * d) * LRU_SUB:(2 * d + 1) * LRU_SUB] + 0.5 * br_ref[d:d + 1, cs])
            t_i = jnp.tanh(half_gates[:, (2 * d + 1) * LRU_SUB:(2 * d + 2) * LRU_SUB] + 0.5 * bi_ref[d:d + 1, cs])
            lam = lam_ref[d:d + 1, cs]
            log_sig = jnp.minimum(lam, 0.0) - jnp.log1p(jnp.exp(-jnp.abs(lam)))
            half_c_log_sig = (0.5 * LRU_C) * log_sig
            log_a = t_r * half_c_log_sig + half_c_log_sig
            a = jnp.exp(log_a)
            var = -jnp.tanh(log_a) * (a * a + 1.0)
            u = jnp.where(var > 0.0, var * lax.rsqrt(var), 0.0) * ((t_i + 1.0) * half_xc)
            a = a.reshape(n_blk, SUBLANES, LRU_SUB)
            u = u.reshape(n_blk, SUBLANES, LRU_SUB)
            step = 1
            while step < SUBLANES:
                if d == 0:
                    ok, shift = in_block >= step, step
                else:
                    ok, shift = in_block < SUBLANES - step, SUBLANES - step
                a_prev = jnp.where(ok, pltpu.roll(a, shift, axis=1), 1.0)
                u_prev = jnp.where(ok, pltpu.roll(u, shift, axis=1), 0.0)
                u = u + a * u_prev
                a = a * a_prev
                step *= 2
            a_ref[rows, cs] = a.reshape(t_len, LRU_SUB)
            u_ref[rows, cs] = u.reshape(t_len, LRU_SUB)

    def body(i, carry):
        new = []
        for q, (cf, cb) in enumerate(carry):
            f0 = pl.multiple_of(q * t_len + i * SUBLANES, SUBLANES)
            b0 = pl.multiple_of(q * t_len + (n_blk - 1 - i) * SUBLANES, SUBLANES)
            hf = uf_ref[pl.ds(f0, SUBLANES), :] + af_ref[pl.ds(f0, SUBLANES), :] * cf
            hb = ub_ref[pl.ds(b0, SUBLANES), :] + ab_ref[pl.ds(b0, SUBLANES), :] * cb
            uf_ref[pl.ds(f0, SUBLANES), :] = hf
            ub_ref[pl.ds(b0, SUBLANES), :] = hb
            new.append((jnp.broadcast_to(hf[SUBLANES - 1:SUBLANES, :], (SUBLANES, width)),
                        jnp.broadcast_to(hb[0:1, :], (SUBLANES, width))))
        return tuple(new)

    if zero_state:
        init = tuple((jnp.zeros((SUBLANES, width), F32),) * 2 for _ in range(n_seq))
    else:
        init = tuple((jnp.broadcast_to(h0f_ref[q], (SUBLANES, width)),
                      jnp.broadcast_to(h0b_ref[q], (SUBLANES, width))) for q in range(n_seq))
    last = lax.fori_loop(0, n_blk, body, init)
    for q, (cf, cb) in enumerate(last):
        hlf_ref[q] = cf[0:1, :]
        hlb_ref[q] = cb[0:1, :]
    y_ref[...] = ((uf_ref[...] + ub_ref[...]) * _gelu_tanh(gb_ref[...])).astype(BF16)


def _lru(xb, gb, conv_w, conv_b, w_r, b_r, w_i, b_i, lam, h0, n_seq, t_len, tok_blk0, seq_per_step):
    rows = seq_per_step * t_len
    const2 = lambda s: (0, 0)
    const3 = lambda s: (0, 0, 0)
    blocks = pl.BlockSpec((2 * LRU_BLOCKS, LRU_BLOCK, LRU_BLOCK), const3)
    per_dir = pl.BlockSpec((2, LRU_WIDTH), const2)
    state = pl.BlockSpec((seq_per_step, 1, LRU_WIDTH), lambda s: (s, 0, 0))
    state_shape = jax.ShapeDtypeStruct((n_seq, 1, LRU_WIDTH), F32)
    return pl.pallas_call(
        functools.partial(_lru_kernel, zero_state=h0 is None, t_len=t_len),
        grid=(n_seq // seq_per_step,),
        in_specs=[
            pl.BlockSpec((rows, LRU_WIDTH), lambda s: (tok_blk0 + s, 0)),
            pl.BlockSpec((rows, LRU_WIDTH), lambda s: (tok_blk0 + s, 0)),
            pl.BlockSpec((CONV_W, LRU_WIDTH), const2),
            pl.BlockSpec((1, LRU_WIDTH), const2),
            blocks, blocks, per_dir, per_dir, per_dir,
        ] + ([] if h0 is None else [state, state]),
        out_specs=[pl.BlockSpec((rows, LRU_WIDTH), lambda s: (s, 0)), state, state],
        out_shape=[jax.ShapeDtypeStruct((n_seq * t_len, LRU_WIDTH), BF16), state_shape, state_shape],
        scratch_shapes=[pltpu.VMEM((rows, LRU_WIDTH), F32)] * 4
        + [pltpu.VMEM((LRU_WIDTH // LRU_SUB, LRU_SUB, 4 * LRU_SUB), BF16)],
        compiler_params=_cparams(1),
        name="rglru",
    )(xb, gb, conv_w, conv_b.reshape(1, LRU_WIDTH),
      w_r.reshape(2 * LRU_BLOCKS, LRU_BLOCK, LRU_BLOCK), w_i.reshape(2 * LRU_BLOCKS, LRU_BLOCK, LRU_BLOCK),
      b_r, b_i, lam, *(() if h0 is None else h0))


def _fourier_kernel(x_ref, mod_ref, g_ref, cs_ref, ct_ref, w_ref, o_ref, w_bf_ref, *, mod_row0, t_len):
    @pl.when(pl.program_id(0) == 0)
    def _():
        w_bf_ref[...] = w_ref[...].astype(BF16)

    x = x_ref[...]
    row = mod_row0 + pl.program_id(0) if mod_row0 else 0
    h = _norm_mod(x, g_ref[...], _mod_vec(mod_ref, row, 3), _mod_vec(mod_ref, row, 4)).astype(BF16)
    cos_parts, sin_parts = [], []
    for g in range(FOURIER_GROUPS):
        ab = jnp.dot(h[:, g * GROUP_W:(g + 1) * GROUP_W], cs_ref[...], preferred_element_type=F32)
        cos_parts.append(ab[:, :GROUP_W])
        sin_parts.append(ab[:, GROUP_W:])
    cos_all = jnp.concatenate(cos_parts, axis=1).astype(BF16)
    sin_all = jnp.concatenate(sin_parts, axis=1).astype(BF16)
    f_parts = []
    for q in range(x.shape[0] // t_len):
        rows = slice(q * t_len, (q + 1) * t_len)
        stacked = jnp.concatenate([cos_all[rows], sin_all[rows]], axis=0)
        f_parts.append(jnp.dot(ct_ref[...], stacked, preferred_element_type=F32))
    f = jnp.concatenate(f_parts, axis=0) * ((t_len * GROUP_W) ** -0.5)
    y = jnp.dot(f.astype(BF16), w_bf_ref[...], preferred_element_type=F32)
    o_ref[...] = x + _mod_vec(mod_ref, row, 5) * y


def _dft_tables(t_len):
    def cos_sin(n):
        jk = np.outer(np.arange(n), np.arange(n)) % n
        ang = 2.0 * np.pi * jk.astype(np.float64) / n
        return np.cos(ang), np.sin(ang)

    cc, sc = cos_sin(GROUP_W)
    ct, st = cos_sin(t_len)
    chan = jnp.asarray(np.concatenate([cc, sc], axis=1).astype(np.float32)).astype(BF16)
    time = jnp.asarray(np.concatenate([ct, -st], axis=1).astype(np.float32)).astype(BF16)
    return chan, time


def _fourier(x, mod, g, w_out, n_seq, t_len, tok_blk0, mod_row0, seq_per_step):
    assert seq_per_step == 1 or mod_row0 == 0
    chan, time = _dft_tables(t_len)
    rows = seq_per_step * t_len
    seq = lambda s: (tok_blk0 + s, 0)
    const = lambda s: (0, 0)
    return pl.pallas_call(
        functools.partial(_fourier_kernel, mod_row0=mod_row0, t_len=t_len),
        grid=(n_seq // seq_per_step,),
        in_specs=[
            pl.BlockSpec((rows, D_MODEL), seq),
            _mod_spec(),
            pl.BlockSpec((1, D_MODEL), const),
            _resident((GROUP_W, 2 * GROUP_W), const),
            _resident((t_len, 2 * t_len), const),
            _resident((D_MODEL, D_MODEL), const),
        ],
        out_specs=pl.BlockSpec((rows, D_MODEL), seq),
        out_shape=jax.ShapeDtypeStruct((N_TOK, D_MODEL), F32),
        input_output_aliases={0: 0},
        scratch_shapes=[pltpu.VMEM((D_MODEL, D_MODEL), BF16)],
        compiler_params=_cparams(1),
        name="fourier_mixer",
    )(x, mod, g.reshape(1, D_MODEL), chan, time, w_out)


def _cache_layout(t):
    return jnp.transpose(t.reshape(BATCH, 1, NA_HEADS, HEAD_DIM, SEQ), (0, 1, 4, 2, 3))


def kernel(x_prompt, x_sample, cache_k, cache_v, state_lru_fwd, state_lru_bwd, c, c_ctx, w_ada, b_ada, norm_g, ffn1_gate, ffn1_up, ffn1_down, ffn2_gate, ffn2_up, ffn2_down, w_in, q_norm_g, k_norm_g, rpb, conv_w, conv_b, lru_w_r, lru_b_r, lru_w_i, lru_b_i, lru_lambda, w_out_ab, w_out_c):
    assert DEPTH == 2, "one neighbourhood/RG-LRU layer followed by one Fourier layer"
    c_ctx2 = c_ctx.reshape(1, D_MODEL)
    mod0 = _adaln(c_ctx2, c, w_ada, b_ada, 0)

    ffn1 = (ffn1_gate, ffn1_up, ffn1_down)
    ffn2 = (ffn2_gate, ffn2_up, ffn2_down)

    x, mod1 = _ffn((x_prompt.reshape(N_CTX_TOK, D_MODEL), x_sample.reshape(N_SMP_TOK, D_MODEL)),
                   mod0, norm_g[0, 0], *ffn1, 0, 0, adaln_next=(c_ctx2, c, w_ada, b_ada, 1))
    q, k, v, xb, gb, new_k, new_v = _proj(x, mod0, norm_g[0, 1], w_in[0], q_norm_g[0], k_norm_g[0])
    o_ctx = _ctx_attn(q, k, v)
    o_smp = _na_attn(q, k, v,
                     jnp.transpose(cache_k[:, 0], (0, 2, 3, 1)).reshape(DEC_BATCH, NA_WIDTH, PAST_LEN),
                     jnp.transpose(cache_v[:, 0], (0, 2, 3, 1)).reshape(DEC_BATCH, NA_WIDTH, PAST_LEN), rpb[0])
    lru_prm = (conv_w[0], conv_b[0], lru_w_r[0], lru_b_r[0], lru_w_i[0], lru_b_i[0], lru_lambda[0])
    yb_ctx, new_hf, new_hb = _lru(xb, gb, *lru_prm, None, BATCH, SEQ, 0, CTX_SEQ_PER_STEP)
    yb_smp, _, _ = _lru(xb, gb, *lru_prm, (state_lru_fwd, state_lru_bwd),
                        DEC_BATCH, DEC_SEQ, N_CTX_TOK // DEC_SEQ, 1)
    (x,) = _ffn((x,), mod0, norm_g[0, 2], *ffn2, 0, 6, mixer_out=(o_ctx, o_smp, yb_ctx, yb_smp, w_out_ab))

    (x,) = _ffn((x,), mod1, norm_g[1, 0], *ffn1, 1, 0)
    x = _fourier(x, mod1, norm_g[1, 1], w_out_c[0], BATCH, SEQ, 0, 0, CTX_SEQ_PER_STEP)
    x = _fourier(x, mod1, norm_g[1, 1], w_out_c[0], DEC_BATCH, DEC_SEQ, N_CTX_TOK // DEC_SEQ, 1, 1)
    y_prompt, y_sample = _ffn((x,), mod1, norm_g[1, 2], *ffn2, 1, 6, split_out=True)

    return (y_prompt.reshape(BATCH, SEQ, D_MODEL), y_sample.reshape(DEC_BATCH, DEC_SEQ, D_MODEL),
            _cache_layout(new_k), _cache_layout(new_v),
            new_hf, new_hb)
```

```python
import functools

import numpy as np
import jax
import jax.numpy as jnp
from jax import lax
from jax.experimental import pallas as pl
from jax.experimental.pallas import tpu as pltpu

F32 = jnp.float32
BF16 = jnp.bfloat16

D_MODEL = 1024
BATCH = 16
SEQ = 256
DEPTH = 2
DEC_BATCH = 2
DEC_SEQ = 1024
PAST_LEN = 256
GRID_W = 64
HEAD_DIM = 64
NA_WIDTH = 512
NA_HEADS = 8
WIN_H = 8
WIN_W = 16
LRU_WIDTH = 512
LRU_BLOCKS = 8
LRU_BLOCK = 64
LRU_C = 8.0
LRU_SUB = 256
CONV_W = 4
FOURIER_GROUPS = 4
GROUP_W = D_MODEL // FOURIER_GROUPS
D_FF = 2816
N_MOD = 9
IN_WIDTH = 3 * NA_WIDTH + 2 * LRU_WIDTH
EPS = 1e-6

N_CTX_TOK = BATCH * SEQ
N_SMP_TOK = DEC_BATCH * DEC_SEQ
N_TOK = N_CTX_TOK + N_SMP_TOK
MOD_ROWS = 8
MOD_WIDTH = N_MOD * D_MODEL
ROWS = DEC_SEQ // GRID_W
KH = min(WIN_H, ROWS)

TOKEN_TILE = 512
N_CTX_TILES = N_CTX_TOK // TOKEN_TILE
CTX_SEQ_PER_STEP = 2
CTX_ATTN_SEQ_PER_STEP = 4
FFN_TILE = 512
FF_TILE = 256
FF_CHUNKS = D_FF // FF_TILE
FF_STAGE_SLOTS = 2
SUBLANES = 8
LANES = 128
VMEM_LIMIT = 56 * 1024 * 1024

NA_Q_ROWS = 4
NA_GROUPS = ROWS // NA_Q_ROWS
NA_K_ROWS = 12
NA_Q = NA_Q_ROWS * GRID_W
NA_K = NA_K_ROWS * GRID_W
N_DR = 2 * WIN_H - 1
N_DC = 2 * WIN_W - 1
N_DR_PAIRS = N_DR + 1


def _cparams(n_axes):
    return pltpu.CompilerParams(
        dimension_semantics=("arbitrary",) * n_axes, vmem_limit_bytes=VMEM_LIMIT)


def _resident(block_shape, index_map):
    return pl.BlockSpec(block_shape, index_map, pipeline_mode=pl.Buffered(1))


def _mod_spec(mod):
    return _resident(mod.shape, lambda i: (0, 0))


def _mod_row_of_tile(i, tile=TOKEN_TILE):
    n_ctx_tiles = N_CTX_TOK // tile
    tiles_per_seq = DEC_SEQ // tile
    return jnp.where(i < n_ctx_tiles, 0, 1 + (i - n_ctx_tiles) // tiles_per_seq)


def _mod_vec(mod_ref, row, k):
    return mod_ref[pl.ds(row, 1), k * D_MODEL:(k + 1) * D_MODEL]


def _norm_mod(x, g, shift, scale):
    ms = jnp.mean(x * x, axis=-1, keepdims=True)
    return (x * lax.rsqrt(ms + EPS)) * (g * (1.0 + scale)) + shift


def _adaln_slab(cctx_ref, c_ref, w_ref, b_ref, cond_ref, layer):
    cond_ref[...] = jnp.zeros_like(cond_ref)
    cond_ref[0:1, :] = cctx_ref[...]
    cond_ref[1:1 + DEC_BATCH, :] = c_ref[...]
    cond = cond_ref[...]
    s = (cond * jax.nn.sigmoid(cond)).astype(BF16)
    return jnp.dot(s, w_ref[...].astype(BF16), preferred_element_type=F32) + b_ref[layer:layer + 1, :]


def _adaln_specs(job, n_steps):
    layer, col0, n_cols = job[4:]
    slab = n_cols // n_steps
    assert n_cols % n_steps == 0 and slab % LANES == 0 and col0 % slab == 0
    blk0 = col0 // slab
    return ([pl.BlockSpec((1, D_MODEL), lambda i: (0, 0)),
             pl.BlockSpec((DEC_BATCH, D_MODEL), lambda i: (0, 0)),
             pl.BlockSpec((None, D_MODEL, slab), lambda i: (layer, 0, blk0 + i)),
             pl.BlockSpec((DEPTH, slab), lambda i: (0, blk0 + i))],
            pl.BlockSpec((MOD_ROWS, slab), lambda i: (0, i)),
            jax.ShapeDtypeStruct((MOD_ROWS, n_cols), F32))


def _adaln_kernel(cctx_ref, c_ref, w_ref, b_ref, o_ref, cond_ref, *, layer):
    o_ref[...] = _adaln_slab(cctx_ref, c_ref, w_ref, b_ref, cond_ref, layer)


def _adaln(job, n_steps):
    in_specs, out_spec, out_shape = _adaln_specs(job, n_steps)
    return pl.pallas_call(
        functools.partial(_adaln_kernel, layer=job[4]),
        grid=(n_steps,),
        in_specs=in_specs,
        out_specs=out_spec,
        out_shape=out_shape,
        scratch_shapes=[pltpu.VMEM((MOD_ROWS, D_MODEL), F32)],
        compiler_params=_cparams(1),
        name="adaln",
    )(*job[:4])


def _ffn_weight_copy(w_hbm, stage_ref, sem_ref, layer, j, ff_axis):
    ff = pl.ds(j * FF_TILE, FF_TILE)
    src = w_hbm.at[layer, :, ff] if ff_axis == 1 else w_hbm.at[layer, ff, :]
    slot = j % FF_STAGE_SLOTS
    return pltpu.make_async_copy(src, stage_ref.at[slot], sem_ref.at[slot])


def _mixer_out_copy(w_hbm, stage_ref, sem_ref, j):
    rows = stage_ref.shape[1]
    slot = j % stage_ref.shape[0]
    return pltpu.make_async_copy(w_hbm.at[0, pl.ds(j * rows, rows), :], stage_ref.at[slot], sem_ref.at[slot])


def _ffn_kernel(*refs, layer, mod_base, split_in, split_out, mixer_out, adaln_next):
    refs = list(refs)
    take = lambda n: [refs.pop(0) for _ in range(n)]
    x_refs = take(2 if split_in else 1)
    mix_refs = take(4) if mixer_out else None
    mod_ref, g_ref = take(2)
    wo_hbm = take(1)[0] if mixer_out else None
    ada_refs = take(4) if adaln_next is not None else None
    wg_hbm, wu_hbm, wd_hbm = take(3)
    o_refs = take(2 if split_out else 1)
    modn_ref = take(1)[0] if adaln_next is not None else None
    wg_bf, wu_bf, wd_bf, stg_g, stg_u, stg_d, sem_g, sem_u, sem_d = take(9)
    cond_scr = take(1)[0] if adaln_next is not None else None
    streams = ((wg_hbm, stg_g, sem_g, wg_bf, 1), (wu_hbm, stg_u, sem_u, wu_bf, 1),
               (wd_hbm, stg_d, sem_d, wd_bf, 0))

    i = pl.program_id(0)
    is_ctx = i < N_CTX_TOK // FFN_TILE
    if split_in:
        x = jnp.where(is_ctx, x_refs[0][...], x_refs[1][...])
    else:
        x = x_refs[0][...]
    row = _mod_row_of_tile(i, FFN_TILE)

    if mixer_out:
        wo_bf, stg_o, sem_o = take(3)
        rows = stg_o.shape[1]
        n_chunks = D_MODEL // rows

        @pl.when(i == 0)
        def _():
            for j in range(stg_o.shape[0]):
                _mixer_out_copy(wo_hbm, stg_o, sem_o, j).start()
            for j in range(n_chunks):
                _mixer_out_copy(wo_hbm, stg_o, sem_o, j).wait()
                wo_bf[j * rows:(j + 1) * rows, :] = stg_o[j % stg_o.shape[0]].astype(BF16)
                if j + stg_o.shape[0] < n_chunks:
                    _mixer_out_copy(wo_hbm, stg_o, sem_o, j + stg_o.shape[0]).start()

        oc_ref, os_ref, yc_ref, ys_ref = mix_refs
        cat = jnp.concatenate([jnp.where(is_ctx, oc_ref[...], os_ref[...]),
                               jnp.where(is_ctx, yc_ref[...], ys_ref[...])], axis=1)
        x = x + _mod_vec(mod_ref, row, mod_base - 1) * jnp.dot(cat, wo_bf[...], preferred_element_type=F32)

    h = _norm_mod(x, g_ref[...], _mod_vec(mod_ref, row, mod_base),
                  _mod_vec(mod_ref, row, mod_base + 1)).astype(BF16)

    def start_chunk(j):
        for w_hbm, stg, sem, _, ff_axis in streams:
            _ffn_weight_copy(w_hbm, stg, sem, layer, j, ff_axis).start()

    def finish_chunk(j):
        for w_hbm, stg, sem, w_bf, ff_axis in streams:
            _ffn_weight_copy(w_hbm, stg, sem, layer, j, ff_axis).wait()
            w_bf[j] = stg[j % FF_STAGE_SLOTS].astype(BF16)

    def run(stream_weights):
        if stream_weights:
            for j in range(FF_STAGE_SLOTS):
                start_chunk(j)
        acc = None
        for j in range(FF_CHUNKS):
            if stream_weights:
                finish_chunk(j)
                if j + FF_STAGE_SLOTS < FF_CHUNKS:
                    start_chunk(j + FF_STAGE_SLOTS)
            a = jnp.dot(h, wg_bf[j], preferred_element_type=F32)
            b = jnp.dot(h, wu_bf[j], preferred_element_type=F32)
            if adaln_next is not None and j == 1:
                modn_ref[...] = _adaln_slab(*ada_refs, cond_scr, adaln_next)
            act = (a * jax.nn.sigmoid(a) * b).astype(BF16)
            y = jnp.dot(act, wd_bf[j], preferred_element_type=F32)
            acc = y if acc is None else acc + y
        res = x + 0.5 * _mod_vec(mod_ref, row, mod_base + 2) * acc
        if split_out:
            @pl.when(is_ctx)
            def _():
                o_refs[0][...] = res

            @pl.when(jnp.logical_not(is_ctx))
            def _():
                o_refs[1][...] = res
        else:
            o_refs[0][...] = res

    @pl.when(i == 0)
    def _():
        run(True)

    @pl.when(i > 0)
    def _():
        run(False)


def _ffn(xs, mod, g, wg, wu, wd, layer, mod_base, split_out=False, mixer_out=None, adaln_next=None):
    tm = FFN_TILE
    n_ctx_tiles = N_CTX_TOK // tm
    split_in = len(xs) == 2

    def tiles(width):
        return (pl.BlockSpec((tm, width), lambda i: (i, 0)),
                pl.BlockSpec((tm, width), lambda i: (jnp.minimum(i, n_ctx_tiles - 1), 0)),
                pl.BlockSpec((tm, width), lambda i: (jnp.maximum(i - n_ctx_tiles, 0), 0)))

    tok, ctx_tok, smp_tok = tiles(D_MODEL)
    full = jax.ShapeDtypeStruct((N_TOK, D_MODEL), F32)
    pair = [jax.ShapeDtypeStruct((N_CTX_TOK, D_MODEL), F32), jax.ShapeDtypeStruct((N_SMP_TOK, D_MODEL), F32)]
    hbm = pl.BlockSpec(memory_space=pl.ANY)
    in_specs = [ctx_tok, smp_tok] if split_in else [tok]
    operands = list(xs)
    scratch = [
        pltpu.VMEM((FF_CHUNKS, D_MODEL, FF_TILE), BF16),
        pltpu.VMEM((FF_CHUNKS, D_MODEL, FF_TILE), BF16),
        pltpu.VMEM((FF_CHUNKS, FF_TILE, D_MODEL), BF16),
        pltpu.VMEM((FF_STAGE_SLOTS, D_MODEL, FF_TILE), F32),
        pltpu.VMEM((FF_STAGE_SLOTS, D_MODEL, FF_TILE), F32),
        pltpu.VMEM((FF_STAGE_SLOTS, FF_TILE, D_MODEL), F32),
        pltpu.SemaphoreType.DMA((FF_STAGE_SLOTS,)),
        pltpu.SemaphoreType.DMA((FF_STAGE_SLOTS,)),
        pltpu.SemaphoreType.DMA((FF_STAGE_SLOTS,)),
    ]
    if mixer_out is not None:
        _, ctx_half, smp_half = tiles(NA_WIDTH)
        in_specs += [ctx_half, smp_half, ctx_half, smp_half]
        operands += list(mixer_out[:4])
    in_specs += [_mod_spec(mod), pl.BlockSpec((1, D_MODEL), lambda i: (0, 0))]
    operands += [mod, g.reshape(1, D_MODEL)]
    out_specs = [ctx_tok, smp_tok] if split_out else [tok]
    out_shape = pair if split_out else [full]
    if adaln_next is not None:
        ada_in, ada_out, ada_shape = _adaln_specs(adaln_next, N_TOK // tm)
        in_specs += ada_in
        operands += list(adaln_next[:4])
        out_specs.append(ada_out)
        out_shape.append(ada_shape)
        scratch.append(pltpu.VMEM((MOD_ROWS, D_MODEL), F32))
    if mixer_out is not None:
        in_specs.insert(len(in_specs) - (4 if adaln_next is not None else 0), hbm)
        operands.insert(len(operands) - (4 if adaln_next is not None else 0), mixer_out[4])
        scratch += [
            pltpu.VMEM((D_MODEL, D_MODEL), BF16),
            pltpu.VMEM((FF_STAGE_SLOTS, FF_TILE, D_MODEL), F32),
            pltpu.SemaphoreType.DMA((FF_STAGE_SLOTS,)),
        ]
    return pl.pallas_call(
        functools.partial(_ffn_kernel, layer=layer, mod_base=mod_base, split_in=split_in,
                          split_out=split_out, mixer_out=mixer_out is not None,
                          adaln_next=None if adaln_next is None else adaln_next[4]),
        grid=(N_TOK // tm,),
        in_specs=in_specs + [hbm, hbm, hbm],
        out_specs=out_specs,
        out_shape=out_shape,
        scratch_shapes=scratch,
        compiler_params=_cparams(1),
        name="ffn",
    )(*operands, wg, wu, wd)


def _head_rms_norm(z, g, ones_bd):
    z2 = z * z
    hi = z2.astype(BF16)
    lo = (z2 - hi.astype(F32)).astype(BF16)
    n = ones_bd.shape[0]
    parts = []
    for c in range(z.shape[1] // n):
        sl = slice(c * n, (c + 1) * n)
        parts.append(jnp.dot(hi[:, sl], ones_bd, preferred_element_type=F32)
                     + jnp.dot(lo[:, sl], ones_bd, preferred_element_type=F32))
    ss = jnp.concatenate(parts, axis=1)
    return z * lax.rsqrt(ss * (1.0 / HEAD_DIM) + EPS) * g


def _proj_kernel(*refs, mod_k0, adaln_layer):
    refs = list(refs)
    x_ref, mod_ref, g_ref, w_ref, qg_ref, kg_ref, ones_ref = refs[:7]
    ada_refs = refs[7:11] if adaln_layer is not None else None
    n_in = 7 if adaln_layer is None else 11
    q_ref, k_ref, v_ref, xb_ref, gb_ref, kout_ref, vout_ref = refs[n_in:n_in + 7]
    rest = refs[n_in + 7:]
    modn_ref = rest.pop(0) if adaln_layer is not None else None
    w_bf_ref = rest.pop(0)
    i = pl.program_id(0)

    @pl.when(i == 0)
    def _():
        w_bf_ref[...] = w_ref[...].astype(BF16)

    x = x_ref[...]
    row = _mod_row_of_tile(i)
    h = _norm_mod(x, g_ref[...], _mod_vec(mod_ref, row, mod_k0),
                  _mod_vec(mod_ref, row, mod_k0 + 1)).astype(BF16)

    def proj(part):
        return jnp.dot(h, w_bf_ref[:, part * NA_WIDTH:(part + 1) * NA_WIDTH], preferred_element_type=F32)

    ones_bd = ones_ref[...]
    q_raw = proj(0)
    k_raw = proj(1)
    q = _head_rms_norm(q_raw, jnp.tile(qg_ref[...], (1, NA_HEADS)), ones_bd) * (HEAD_DIM ** -0.5)
    q_ref[...] = q.astype(BF16)
    v = proj(2)
    if adaln_layer is not None:
        modn_ref[...] = _adaln_slab(*ada_refs, rest.pop(0), adaln_layer)
    k = _head_rms_norm(k_raw, jnp.tile(kg_ref[...], (1, NA_HEADS)), ones_bd)
    k_ref[...] = k.astype(BF16)
    xb = proj(3)
    v_ref[...] = v.astype(BF16)
    gb = proj(4)
    xb_ref[...] = xb
    gb_ref[...] = gb

    kt = [k[b * SEQ:(b + 1) * SEQ, :].T for b in range(TOKEN_TILE // SEQ)]
    vt = [v[b * SEQ:(b + 1) * SEQ, :].T for b in range(TOKEN_TILE // SEQ)]

    @pl.when(i < N_CTX_TILES)
    def _():
        for b in range(TOKEN_TILE // SEQ):
            kout_ref[b] = kt[b]
            vout_ref[b] = vt[b]


def _proj(x, mod, mod_k0, g, w_in, q_g, k_g, adaln_next=None):
    tm = TOKEN_TILE
    head = np.arange(2 * LANES) // HEAD_DIM
    ones_bd = jnp.asarray((head[:, None] == head[None, :]).astype(np.float32), dtype=BF16)
    tok = lambda i: (i, 0)
    const = lambda i: (0, 0)
    act_f32 = jax.ShapeDtypeStruct((N_TOK, NA_WIDTH), F32)
    act_bf16 = jax.ShapeDtypeStruct((N_TOK, NA_WIDTH), BF16)
    cache = jax.ShapeDtypeStruct((BATCH, NA_WIDTH, SEQ), F32)
    cache_spec = pl.BlockSpec((tm // SEQ, NA_WIDTH, SEQ), lambda i: (jnp.minimum(i, N_CTX_TILES - 1), 0, 0))
    in_specs = [
        pl.BlockSpec((tm, D_MODEL), tok),
        _mod_spec(mod),
        pl.BlockSpec((1, D_MODEL), const),
        _resident((D_MODEL, IN_WIDTH), const),
        pl.BlockSpec((1, HEAD_DIM), const),
        pl.BlockSpec((1, HEAD_DIM), const),
        _resident((2 * LANES, 2 * LANES), const),
    ]
    operands = [x, mod, g.reshape(1, D_MODEL), w_in, q_g.reshape(1, HEAD_DIM), k_g.reshape(1, HEAD_DIM), ones_bd]
    out_specs = [pl.BlockSpec((tm, NA_WIDTH), tok)] * 5 + [cache_spec, cache_spec]
    out_shape = [act_bf16, act_bf16, act_bf16, act_f32, act_f32, cache, cache]
    scratch = [pltpu.VMEM((D_MODEL, IN_WIDTH), BF16)]
    if adaln_next is not None:
        ada_in, ada_out, ada_shape = _adaln_specs(adaln_next, N_TOK // tm)
        in_specs += ada_in
        operands += list(adaln_next[:4])
        out_specs.append(ada_out)
        out_shape.append(ada_shape)
        scratch.append(pltpu.VMEM((MOD_ROWS, D_MODEL), F32))
    return pl.pallas_call(
        functools.partial(_proj_kernel, mod_k0=mod_k0,
                          adaln_layer=None if adaln_next is None else adaln_next[4]),
        grid=(N_TOK // tm,),
        in_specs=in_specs,
        out_specs=out_specs,
        out_shape=out_shape,
        scratch_shapes=scratch,
        compiler_params=_cparams(1),
        name="mixer_in_proj",
    )(*operands)


def _head_masks():
    lane = lax.broadcasted_iota(jnp.int32, (1, 2 * HEAD_DIM), 1)
    return [lane < HEAD_DIM, lane >= HEAD_DIM]


def _ctx_attn_kernel(q_ref, k_ref, v_ref, o_ref):
    masks = _head_masks()
    units = [(b, h) for b in range(q_ref.shape[0] // SEQ) for h in range(NA_HEADS)]

    def where(b, h):
        return slice(b * SEQ, (b + 1) * SEQ), slice(2 * HEAD_DIM * (h // 2), 2 * HEAD_DIM * (h // 2 + 1))

    def scores(b, h):
        rows, sl = where(b, h)
        q2 = q_ref[rows, sl]
        qm = jnp.where(masks[h % 2], q2, jnp.zeros_like(q2))
        return jnp.dot(qm, k_ref[rows, sl].T, preferred_element_type=F32)

    def attend(b, h, s):
        rows, sl = where(b, h)
        pe = jnp.exp(s - jnp.max(s, axis=-1, keepdims=True))
        den = jnp.sum(pe, axis=-1, keepdims=True)
        return jnp.dot(pe.astype(BF16), v_ref[rows, sl], preferred_element_type=F32) / den

    pending = scores(*units[0])
    out = None
    for n, (b, h) in enumerate(units):
        current = pending
        if n + 1 < len(units):
            pending = scores(*units[n + 1])
        o = attend(b, h, current)
        if h % 2 == 0:
            out = o
        else:
            rows, sl = where(b, h)
            o_ref[rows, sl] = jnp.where(masks[1], o, out).astype(BF16)


def _ctx_attn(q, k, v):
    blk = pl.BlockSpec((CTX_ATTN_SEQ_PER_STEP * SEQ, NA_WIDTH), lambda b: (b, 0))
    return pl.pallas_call(
        _ctx_attn_kernel,
        grid=(BATCH // CTX_ATTN_SEQ_PER_STEP,),
        in_specs=[blk, blk, blk],
        out_specs=blk,
        out_shape=jax.ShapeDtypeStruct((N_CTX_TOK, NA_WIDTH), BF16),
        compiler_params=_cparams(1),
        name="ctx_attention",
    )(q, k, v)


def _na_build_bias_table(rpb_ref, table_ref):
    qc = lax.broadcasted_iota(jnp.int32, (GRID_W, LANES), 0)
    lane = lax.broadcasted_iota(jnp.int32, (GRID_W, LANES), 1)
    kc = lane % GRID_W
    col_start = jnp.clip(qc - WIN_W // 2, 0, GRID_W - WIN_W)
    col_in = (kc >= col_start) & (kc < col_start + WIN_W)
    neg = jnp.full((GRID_W, LANES), -jnp.inf, F32)

    def toeplitz(h, dr, lane0):
        if dr < 0 or dr >= N_DR:
            return neg
        row = jnp.pad(rpb_ref[dr, h:h + 1, :], ((0, 0), (0, LANES - N_DC)))
        w = jnp.broadcast_to(row, (GRID_W, LANES))
        return pltpu.roll(w, (lane0 - (WIN_W - 1)) % LANES, 1, stride=1, stride_axis=0)

    for h in range(NA_HEADS):
        for i in range(N_DR_PAIRS):
            t = jnp.where(lane < GRID_W, toeplitz(h, i - 1, 0), toeplitz(h, i, GRID_W))
            table_ref[h, i] = jnp.where(col_in, t, neg)


def _na_kernel(q_ref, k_ref, v_ref, kc_ref, vc_ref, rpb_ref, o_ref, table_ref):
    b = pl.program_id(0)
    g = pl.program_id(1)

    @pl.when((b == 0) & (g == 0))
    def _():
        _na_build_bias_table(rpb_ref, table_ref)

    win_row0 = jnp.where(g < NA_GROUPS // 2, 0, ROWS - NA_K_ROWS)
    start = pl.multiple_of(win_row0 * GRID_W, GRID_W)
    q_row = g * NA_Q_ROWS + lax.broadcasted_iota(jnp.int32, (NA_Q, 1), 0) // GRID_W
    k_row = win_row0 + lax.broadcasted_iota(jnp.int32, (1, NA_K), 1) // GRID_W
    row_start = jnp.clip(q_row - KH // 2, 0, ROWS - KH)
    row_in = (k_row >= row_start) & (k_row < row_start + KH)
    masks = _head_masks()

    def pair_slab(p):
        return slice(2 * HEAD_DIM * p, 2 * HEAD_DIM * (p + 1))

    def scores(head):
        p, e = divmod(head, 2)
        sl = pair_slab(p)
        q2 = q_ref[:, sl]
        klt = k_ref[pl.ds(start, NA_K), sl].T
        kct = kc_ref[0, sl, :].astype(BF16)
        bias_rows = []
        for a in range(NA_Q_ROWS):
            tiles = []
            for m in range(NA_K_ROWS // 2):
                dr = win_row0 + 2 * m - (g * NA_Q_ROWS + a) + (WIN_H - 1)
                tiles.append(table_ref[head, jnp.clip(dr + 1, 0, N_DR_PAIRS - 1)])
            bias_rows.append(jnp.concatenate(tiles, axis=1))
        bias = jnp.concatenate(bias_rows, axis=0)
        qm = jnp.where(masks[e], q2, jnp.zeros_like(q2))
        s_loc = jnp.where(row_in, jnp.dot(qm, klt, preferred_element_type=F32) + bias, -jnp.inf)
        s_ctx = jnp.dot(qm, kct, preferred_element_type=F32)
        return s_loc, s_ctx

    def attend(head, s_loc, s_ctx):
        sl = pair_slab(head // 2)
        vl = v_ref[pl.ds(start, NA_K), sl]
        vct = vc_ref[0, sl, :].astype(BF16)
        m_max = jnp.maximum(jnp.max(s_loc, axis=-1, keepdims=True),
                            jnp.max(s_ctx, axis=-1, keepdims=True))
        p_loc = jnp.exp(s_loc - m_max)
        p_ctx = jnp.exp(s_ctx - m_max)
        den = jnp.sum(p_loc, axis=-1, keepdims=True) + jnp.sum(p_ctx, axis=-1, keepdims=True)
        return (jnp.dot(p_loc.astype(BF16), vl, preferred_element_type=F32)
                + lax.dot_general(p_ctx.astype(BF16), vct, (((1,), (1,)), ((), ())),
                                  preferred_element_type=F32)) / den

    pending = scores(0)
    out = None
    for head in range(NA_HEADS):
        current = pending
        if head + 1 < NA_HEADS:
            pending = scores(head + 1)
        o = attend(head, *current)
        if head % 2 == 0:
            out = o
        else:
            o_ref[:, pair_slab(head // 2)] = jnp.where(masks[1], o, out).astype(BF16)


def _na_attn(q, k, v, k_ctx, v_ctx, rpb_e):
    smp_blk0 = N_CTX_TOK // DEC_SEQ
    q_blk0 = N_CTX_TOK // NA_Q
    kv = pl.BlockSpec((DEC_SEQ, NA_WIDTH), lambda b, g: (smp_blk0 + b, 0))
    ctx = pl.BlockSpec((1, NA_WIDTH, PAST_LEN), lambda b, g: (b, 0, 0))
    return pl.pallas_call(
        _na_kernel,
        grid=(DEC_BATCH, NA_GROUPS),
        in_specs=[
            pl.BlockSpec((NA_Q, NA_WIDTH), lambda b, g: (q_blk0 + b * NA_GROUPS + g, 0)),
            kv, kv, ctx, ctx,
            pl.BlockSpec((N_DR, NA_HEADS, N_DC), lambda b, g: (0, 0, 0)),
        ],
        out_specs=pl.BlockSpec((NA_Q, NA_WIDTH), lambda b, g: (b * NA_GROUPS + g, 0)),
        out_shape=jax.ShapeDtypeStruct((N_SMP_TOK, NA_WIDTH), BF16),
        scratch_shapes=[pltpu.VMEM((NA_HEADS, N_DR_PAIRS, GRID_W, LANES), F32)],
        compiler_params=_cparams(2),
        name="neighbourhood_attention",
    )(q, k, v, k_ctx, v_ctx, jnp.transpose(rpb_e, (1, 0, 2)))


def _gelu_tanh(x):
    c0 = float(np.sqrt(2.0 / np.pi))
    inner = x * (c0 + (c0 * 0.044715) * (x * x))
    return (0.5 * x) * (1.0 + jnp.tanh(inner))


def _lru_build_gate_weights(wr_ref, wi_ref, w_scr):
    blocks_per_group = LRU_SUB // LRU_BLOCK
    w_scr[...] = jnp.zeros_like(w_scr)
    for d in range(2):
        for kind, w_ref in enumerate((wr_ref, wi_ref)):
            col0 = (2 * d + kind) * LRU_SUB
            for blk in range(LRU_BLOCKS):
                c, n = divmod(blk, blocks_per_group)
                r0 = n * LRU_BLOCK
                w_scr[c, r0:r0 + LRU_BLOCK, col0 + r0:col0 + r0 + LRU_BLOCK] = (
                    0.5 * w_ref[d * LRU_BLOCKS + blk]).astype(BF16)


def _lru_kernel(*refs, zero_state, t_len):
    refs = list(refs)
    xb_ref, gb_ref, cw_ref, cb_ref, wr_ref, wi_ref, br_ref, bi_ref, lam_ref = refs[:9]
    h0f_ref, h0b_ref = (None, None) if zero_state else refs[9:11]
    y_ref, hlf_ref, hlb_ref, af_ref, uf_ref, ab_ref, ub_ref, w_scr = refs[-8:]
    width = xb_ref.shape[1]
    n_seq = xb_ref.shape[0] // t_len
    n_blk = t_len // SUBLANES
    row = lax.broadcasted_iota(jnp.int32, (t_len, 1), 0)
    in_block = lax.broadcasted_iota(jnp.int32, (1, SUBLANES, 1), 1)

    @pl.when(pl.program_id(0) == 0)
    def _():
        _lru_build_gate_weights(wr_ref, wi_ref, w_scr)

    def shifted(z, s):
        rolled = pltpu.roll(z, (-s) % t_len, axis=0)
        ok = (row + s >= 0) & (row + s < t_len)
        return jnp.where(ok, rolled, 0.0)

    left = (CONV_W - 1) // 2
    for q, c in [(q, c) for q in range(n_seq) for c in range(width // LRU_SUB)]:
        rows = slice(q * t_len, (q + 1) * t_len)
        cs = slice(c * LRU_SUB, (c + 1) * LRU_SUB)
        x = xb_ref[rows, cs]
        xc = cb_ref[:, cs]
        for j in range(CONV_W):
            tap = x if j == left else shifted(x, j - left)
            xc = xc + tap * cw_ref[j:j + 1, cs]
        half_gates = jnp.dot(xc.astype(BF16), w_scr[c], preferred_element_type=F32)
        half_xc = 0.5 * xc
        for d, (a_ref, u_ref) in enumerate(((af_ref, uf_ref), (ab_ref, ub_ref))):
            t_r = jnp.tanh(half_gates[:, (2 * d) * LRU_SUB:(2 * d + 1) * LRU_SUB] + 0.5 * br_ref[d:d + 1, cs])
            t_i = jnp.tanh(half_gates[:, (2 * d + 1) * LRU_SUB:(2 * d + 2) * LRU_SUB] + 0.5 * bi_ref[d:d + 1, cs])
            lam = lam_ref[d:d + 1, cs]
            log_sig = jnp.minimum(lam, 0.0) - jnp.log1p(jnp.exp(-jnp.abs(lam)))
            half_c_log_sig = (0.5 * LRU_C) * log_sig
            log_a = t_r * half_c_log_sig + half_c_log_sig
            a = jnp.exp(log_a)
            var = -jnp.tanh(log_a) * (a * a + 1.0)
            u = jnp.where(var > 0.0, var * lax.rsqrt(var), 0.0) * ((t_i + 1.0) * half_xc)
            a = a.reshape(n_blk, SUBLANES, LRU_SUB)
            u = u.reshape(n_blk, SUBLANES, LRU_SUB)
            step = 1
            while step < SUBLANES:
                if d == 0:
                    ok, shift = in_block >= step, step
                else:
                    ok, shift = in_block < SUBLANES - step, SUBLANES - step
                a_prev = jnp.where(ok, pltpu.roll(a, shift, axis=1), 1.0)
                u_prev = jnp.where(ok, pltpu.roll(u, shift, axis=1), 0.0)
                u = u + a * u_prev
                a = a * a_prev
                step *= 2
            a_ref[rows, cs] = a.reshape(t_len, LRU_SUB)
            u_ref[rows, cs] = u.reshape(t_len, LRU_SUB)

    def body(i, carry):
        new = []
        for q, (cf, cb) in enumerate(carry):
            f0 = pl.multiple_of(q * t_len + i * SUBLANES, SUBLANES)
            b0 = pl.multiple_of(q * t_len + (n_blk - 1 - i) * SUBLANES, SUBLANES)
            hf = uf_ref[pl.ds(f0, SUBLANES), :] + af_ref[pl.ds(f0, SUBLANES), :] * cf
            hb = ub_ref[pl.ds(b0, SUBLANES), :] + ab_ref[pl.ds(b0, SUBLANES), :] * cb
            uf_ref[pl.ds(f0, SUBLANES), :] = hf
            ub_ref[pl.ds(b0, SUBLANES), :] = hb
            new.append((jnp.broadcast_to(hf[SUBLANES - 1:SUBLANES, :], (SUBLANES, width)),
                        jnp.broadcast_to(hb[0:1, :], (SUBLANES, width))))
        return tuple(new)

    if zero_state:
        init = tuple((jnp.zeros((SUBLANES, width), F32),) * 2 for _ in range(n_seq))
    else:
        init = tuple((jnp.broadcast_to(h0f_ref[q], (SUBLANES, width)),
                      jnp.broadcast_to(h0b_ref[q], (SUBLANES, width))) for q in range(n_seq))
    last = lax.fori_loop(0, n_blk, body, init)
    for q, (cf, cb) in enumerate(last):
        hlf_ref[q] = cf[0:1, :]
        hlb_ref[q] = cb[0:1, :]
    y_ref[...] = ((uf_ref[...] + ub_ref[...]) * _gelu_tanh(gb_ref[...])).astype(BF16)


def _lru(xb, gb, conv_w, conv_b, w_r, b_r, w_i, b_i, lam, h0, n_seq, t_len, tok_blk0, seq_per_step):
    rows = seq_per_step * t_len
    const2 = lambda s: (0, 0)
    const3 = lambda s: (0, 0, 0)
    blocks = pl.BlockSpec((2 * LRU_BLOCKS, LRU_BLOCK, LRU_BLOCK), const3)
    per_dir = pl.BlockSpec((2, LRU_WIDTH), const2)
    state = pl.BlockSpec((seq_per_step, 1, LRU_WIDTH), lambda s: (s, 0, 0))
    state_shape = jax.ShapeDtypeStruct((n_seq, 1, LRU_WIDTH), F32)
    return pl.pallas_call(
        functools.partial(_lru_kernel, zero_state=h0 is None, t_len=t_len),
        grid=(n_seq // seq_per_step,),
        in_specs=[
            pl.BlockSpec((rows, LRU_WIDTH), lambda s: (tok_blk0 + s, 0)),
            pl.BlockSpec((rows, LRU_WIDTH), lambda s: (tok_blk0 + s, 0)),
            pl.BlockSpec((CONV_W, LRU_WIDTH), const2),
            pl.BlockSpec((1, LRU_WIDTH), const2),
            blocks, blocks, per_dir, per_dir, per_dir,
        ] + ([] if h0 is None else [state, state]),
        out_specs=[pl.BlockSpec((rows, LRU_WIDTH), lambda s: (s, 0)), state, state],
        out_shape=[jax.ShapeDtypeStruct((n_seq * t_len, LRU_WIDTH), BF16), state_shape, state_shape],
        scratch_shapes=[pltpu.VMEM((rows, LRU_WIDTH), F32)] * 4
        + [pltpu.VMEM((LRU_WIDTH // LRU_SUB, LRU_SUB, 4 * LRU_SUB), BF16)],
        compiler_params=_cparams(1),
        name="rglru",
    )(xb, gb, conv_w, conv_b.reshape(1, LRU_WIDTH),
      w_r.reshape(2 * LRU_BLOCKS, LRU_BLOCK, LRU_BLOCK), w_i.reshape(2 * LRU_BLOCKS, LRU_BLOCK, LRU_BLOCK),
      b_r, b_i, lam, *(() if h0 is None else h0))


def _fourier_kernel(x_ref, mod_ref, g_ref, cs_ref, ct_ref, w_ref, o_ref, w_bf_ref, *, mod_row0, t_len):
    @pl.when(pl.program_id(0) == 0)
    def _():
        w_bf_ref[...] = w_ref[...].astype(BF16)

    x = x_ref[...]
    row = mod_row0 + pl.program_id(0) if mod_row0 else 0
    h = _norm_mod(x, g_ref[...], _mod_vec(mod_ref, row, 3), _mod_vec(mod_ref, row, 4)).astype(BF16)
    cos_parts, sin_parts = [], []
    for g in range(FOURIER_GROUPS):
        ab = jnp.dot(h[:, g * GROUP_W:(g + 1) * GROUP_W], cs_ref[...], preferred_element_type=F32)
        cos_parts.append(ab[:, :GROUP_W])
        sin_parts.append(ab[:, GROUP_W:])
    cos_all = jnp.concatenate(cos_parts, axis=1).astype(BF16)
    sin_all = jnp.concatenate(sin_parts, axis=1).astype(BF16)
    f_parts = []
    for q in range(x.shape[0] // t_len):
        rows = slice(q * t_len, (q + 1) * t_len)
        stacked = jnp.concatenate([cos_all[rows], sin_all[rows]], axis=0)
        f_parts.append(jnp.dot(ct_ref[...], stacked, preferred_element_type=F32))
    f = jnp.concatenate(f_parts, axis=0) * ((t_len * GROUP_W) ** -0.5)
    y = jnp.dot(f.astype(BF16), w_bf_ref[...], preferred_element_type=F32)
    o_ref[...] = x + _mod_vec(mod_ref, row, 5) * y


def _dft_tables(t_len):
    def cos_sin(n):
        jk = np.outer(np.arange(n), np.arange(n)) % n
        ang = 2.0 * np.pi * jk.astype(np.float64) / n
        return np.cos(ang), np.sin(ang)

    cc, sc = cos_sin(GROUP_W)
    ct, st = cos_sin(t_len)
    chan = jnp.asarray(np.concatenate([cc, sc], axis=1).astype(np.float32)).astype(BF16)
    time = jnp.asarray(np.concatenate([ct, -st], axis=1).astype(np.float32)).astype(BF16)
    return chan, time


def _fourier(x, mod, g, w_out, n_seq, t_len, tok_blk0, mod_row0, seq_per_step):
    assert seq_per_step == 1 or mod_row0 == 0
    chan, time = _dft_tables(t_len)
    rows = seq_per_step * t_len
    seq = lambda s: (tok_blk0 + s, 0)
    const = lambda s: (0, 0)
    return pl.pallas_call(
        functools.partial(_fourier_kernel, mod_row0=mod_row0, t_len=t_len),
        grid=(n_seq // seq_per_step,),
        in_specs=[
            pl.BlockSpec((rows, D_MODEL), seq),
            _mod_spec(mod),
            pl.BlockSpec((1, D_MODEL), const),
            _resident((GROUP_W, 2 * GROUP_W), const),
            _resident((t_len, 2 * t_len), const),
            _resident((D_MODEL, D_MODEL), const),
        ],
        out_specs=pl.BlockSpec((rows, D_MODEL), seq),
        out_shape=jax.ShapeDtypeStruct((N_TOK, D_MODEL), F32),
        input_output_aliases={0: 0},
        scratch_shapes=[pltpu.VMEM((D_MODEL, D_MODEL), BF16)],
        compiler_params=_cparams(1),
        name="fourier_mixer",
    )(x, mod, g.reshape(1, D_MODEL), chan, time, w_out)


def _cache_layout(t):
    return jnp.transpose(t.reshape(BATCH, 1, NA_HEADS, HEAD_DIM, SEQ), (0, 1, 4, 2, 3))


def kernel(x_prompt, x_sample, cache_k, cache_v, state_lru_fwd, state_lru_bwd, c, c_ctx, w_ada, b_ada, norm_g, ffn1_gate, ffn1_up, ffn1_down, ffn2_gate, ffn2_up, ffn2_down, w_in, q_norm_g, k_norm_g, rpb, conv_w, conv_b, lru_w_r, lru_b_r, lru_w_i, lru_b_i, lru_lambda, w_out_ab, w_out_c):
    assert DEPTH == 2, "one neighbourhood/RG-LRU layer followed by one Fourier layer"
    c_ctx2 = c_ctx.reshape(1, D_MODEL)
    ada = (c_ctx2, c, w_ada, b_ada)
    mod0_ffn1 = _adaln(ada + (0, 0, 3 * D_MODEL), 3)

    ffn1 = (ffn1_gate, ffn1_up, ffn1_down)
    ffn2 = (ffn2_gate, ffn2_up, ffn2_down)

    x, mod0_rest = _ffn((x_prompt.reshape(N_CTX_TOK, D_MODEL), x_sample.reshape(N_SMP_TOK, D_MODEL)),
                        mod0_ffn1, norm_g[0, 0], *ffn1, 0, 0,
                        adaln_next=ada + (0, 3 * D_MODEL, (N_MOD - 3) * D_MODEL))
    q, k, v, xb, gb, new_k, new_v, mod1 = _proj(x, mod0_rest, 0, norm_g[0, 1], w_in[0], q_norm_g[0],
                                                k_norm_g[0], adaln_next=ada + (1, 0, MOD_WIDTH))
    o_ctx = _ctx_attn(q, k, v)
    o_smp = _na_attn(q, k, v,
                     jnp.transpose(cache_k[:, 0], (0, 2, 3, 1)).reshape(DEC_BATCH, NA_WIDTH, PAST_LEN),
                     jnp.transpose(cache_v[:, 0], (0, 2, 3, 1)).reshape(DEC_BATCH, NA_WIDTH, PAST_LEN), rpb[0])
    lru_prm = (conv_w[0], conv_b[0], lru_w_r[0], lru_b_r[0], lru_w_i[0], lru_b_i[0], lru_lambda[0])
    yb_ctx, new_hf, new_hb = _lru(xb, gb, *lru_prm, None, BATCH, SEQ, 0, CTX_SEQ_PER_STEP)
    yb_smp, _, _ = _lru(xb, gb, *lru_prm, (state_lru_fwd, state_lru_bwd),
                        DEC_BATCH, DEC_SEQ, N_CTX_TOK // DEC_SEQ, 1)
    (x,) = _ffn((x,), mod0_rest, norm_g[0, 2], *ffn2, 0, 3, mixer_out=(o_ctx, o_smp, yb_ctx, yb_smp, w_out_ab))

    (x,) = _ffn((x,), mod1, norm_g[1, 0], *ffn1, 1, 0)
    x = _fourier(x, mod1, norm_g[1, 1], w_out_c[0], BATCH, SEQ, 0, 0, CTX_SEQ_PER_STEP)
    x = _fourier(x, mod1, norm_g[1, 1], w_out_c[0], DEC_BATCH, DEC_SEQ, N_CTX_TOK // DEC_SEQ, 1, 1)
    y_prompt, y_sample = _ffn((x,), mod1, norm_g[1, 2], *ffn2, 1, 6, split_out=True)

    return (y_prompt.reshape(BATCH, SEQ, D_MODEL), y_sample.reshape(DEC_BATCH, DEC_SEQ, D_MODEL),
            _cache_layout(new_k), _cache_layout(new_v),
            new_hf, new_hb)
```

```python
import functools

import numpy as np
import jax
import jax.numpy as jnp
from jax import lax
from jax.experimental import pallas as pl
from jax.experimental.pallas import tpu as pltpu

F32 = jnp.float32
BF16 = jnp.bfloat16

D_MODEL = 1024
BATCH = 16
SEQ = 256
DEPTH = 2
DEC_BATCH = 2
DEC_SEQ = 1024
PAST_LEN = 256
GRID_W = 64
HEAD_DIM = 64
NA_WIDTH = 512
NA_HEADS = 8
WIN_H = 8
WIN_W = 16
LRU_WIDTH = 512
LRU_BLOCKS = 8
LRU_BLOCK = 64
LRU_C = 8.0
LRU_SUB = 256
CONV_W = 4
FOURIER_GROUPS = 4
GROUP_W = D_MODEL // FOURIER_GROUPS
D_FF = 2816
N_MOD = 9
IN_WIDTH = 3 * NA_WIDTH + 2 * LRU_WIDTH
EPS = 1e-6

N_CTX_TOK = BATCH * SEQ
N_SMP_TOK = DEC_BATCH * DEC_SEQ
N_TOK = N_CTX_TOK + N_SMP_TOK
MOD_ROWS = 8
MOD_WIDTH = N_MOD * D_MODEL
ROWS = DEC_SEQ // GRID_W
KH = min(WIN_H, ROWS)

TOKEN_TILE = 512
N_CTX_TILES = N_CTX_TOK // TOKEN_TILE
CTX_SEQ_PER_STEP = 2
CTX_ATTN_SEQ_PER_STEP = 4
FFN_TILE = 512
FF_TILE = 256
FF_CHUNKS = D_FF // FF_TILE
FF_STAGE_SLOTS = 2
SUBLANES = 8
LANES = 128
VMEM_LIMIT = 56 * 1024 * 1024

NA_Q_ROWS = 4
NA_GROUPS = ROWS // NA_Q_ROWS
NA_K_ROWS = 12
NA_Q = NA_Q_ROWS * GRID_W
NA_K = NA_K_ROWS * GRID_W
N_DR = 2 * WIN_H - 1
N_DC = 2 * WIN_W - 1
N_DR_PAIRS = N_DR + 1


def _cparams(n_axes):
    return pltpu.CompilerParams(
        dimension_semantics=("arbitrary",) * n_axes, vmem_limit_bytes=VMEM_LIMIT)


def _resident(block_shape, index_map):
    return pl.BlockSpec(block_shape, index_map, pipeline_mode=pl.Buffered(1))


def _mod_spec(mod):
    return _resident(mod.shape, lambda i: (0, 0))


def _mod_row_of_tile(i, tile=TOKEN_TILE):
    n_ctx_tiles = N_CTX_TOK // tile
    tiles_per_seq = DEC_SEQ // tile
    return jnp.where(i < n_ctx_tiles, 0, 1 + (i - n_ctx_tiles) // tiles_per_seq)


def _mod_vec(mod_ref, row, k):
    return mod_ref[pl.ds(row, 1), k * D_MODEL:(k + 1) * D_MODEL]


def _norm_mod(x, g, shift, scale):
    ms = jnp.mean(x * x, axis=-1, keepdims=True)
    return (x * lax.rsqrt(ms + EPS)) * (g * (1.0 + scale)) + shift


def _adaln_slab(cctx_ref, c_ref, w_ref, b_ref, cond_ref, layer):
    cond_ref[...] = jnp.zeros_like(cond_ref)
    cond_ref[0:1, :] = cctx_ref[...]
    cond_ref[1:1 + DEC_BATCH, :] = c_ref[...]
    cond = cond_ref[...]
    s = (cond * jax.nn.sigmoid(cond)).astype(BF16)
    return jnp.dot(s, w_ref[...].astype(BF16), preferred_element_type=F32) + b_ref[layer:layer + 1, :]


def _adaln_specs(job, n_steps, step=lambda i: i):
    layer, col0, n_cols = job[4:]
    slab = n_cols // n_steps
    assert n_cols % n_steps == 0 and slab % LANES == 0 and col0 % slab == 0
    blk0 = col0 // slab
    return ([pl.BlockSpec((1, D_MODEL), lambda *ids: (0, 0)),
             pl.BlockSpec((DEC_BATCH, D_MODEL), lambda *ids: (0, 0)),
             pl.BlockSpec((None, D_MODEL, slab), lambda *ids: (layer, 0, blk0 + step(*ids))),
             pl.BlockSpec((DEPTH, slab), lambda *ids: (0, blk0 + step(*ids)))],
            pl.BlockSpec((MOD_ROWS, slab), lambda *ids: (0, step(*ids))),
            jax.ShapeDtypeStruct((MOD_ROWS, n_cols), F32))


def _adaln_kernel(cctx_ref, c_ref, w_ref, b_ref, o_ref, cond_ref, *, layer):
    o_ref[...] = _adaln_slab(cctx_ref, c_ref, w_ref, b_ref, cond_ref, layer)


def _adaln(job, n_steps):
    in_specs, out_spec, out_shape = _adaln_specs(job, n_steps)
    return pl.pallas_call(
        functools.partial(_adaln_kernel, layer=job[4]),
        grid=(n_steps,),
        in_specs=in_specs,
        out_specs=out_spec,
        out_shape=out_shape,
        scratch_shapes=[pltpu.VMEM((MOD_ROWS, D_MODEL), F32)],
        compiler_params=_cparams(1),
        name="adaln",
    )(*job[:4])


def _ffn_weight_copy(w_hbm, stage_ref, sem_ref, layer, j, ff_axis):
    ff = pl.ds(j * FF_TILE, FF_TILE)
    src = w_hbm.at[layer, :, ff] if ff_axis == 1 else w_hbm.at[layer, ff, :]
    slot = j % FF_STAGE_SLOTS
    return pltpu.make_async_copy(src, stage_ref.at[slot], sem_ref.at[slot])


def _mixer_out_copy(w_hbm, stage_ref, sem_ref, j):
    rows = stage_ref.shape[1]
    slot = j % stage_ref.shape[0]
    return pltpu.make_async_copy(w_hbm.at[0, pl.ds(j * rows, rows), :], stage_ref.at[slot], sem_ref.at[slot])


def _ffn_kernel(*refs, layer, mod_base, split_in, split_out, mixer_out, adaln_next):
    refs = list(refs)
    take = lambda n: [refs.pop(0) for _ in range(n)]
    x_refs = take(2 if split_in else 1)
    mix_refs = take(4) if mixer_out else None
    mod_ref, g_ref = take(2)
    wo_hbm = take(1)[0] if mixer_out else None
    ada_refs = take(4) if adaln_next is not None else None
    wg_hbm, wu_hbm, wd_hbm = take(3)
    o_refs = take(2 if split_out else 1)
    modn_ref = take(1)[0] if adaln_next is not None else None
    wg_bf, wu_bf, wd_bf, stg_g, stg_u, stg_d, sem_g, sem_u, sem_d = take(9)
    cond_scr = take(1)[0] if adaln_next is not None else None
    streams = ((wg_hbm, stg_g, sem_g, wg_bf, 1), (wu_hbm, stg_u, sem_u, wu_bf, 1),
               (wd_hbm, stg_d, sem_d, wd_bf, 0))

    i = pl.program_id(0)
    is_ctx = i < N_CTX_TOK // FFN_TILE
    if split_in:
        x = jnp.where(is_ctx, x_refs[0][...], x_refs[1][...])
    else:
        x = x_refs[0][...]
    row = _mod_row_of_tile(i, FFN_TILE)

    if mixer_out:
        wo_bf, stg_o, sem_o = take(3)
        rows = stg_o.shape[1]
        n_chunks = D_MODEL // rows

        @pl.when(i == 0)
        def _():
            for j in range(stg_o.shape[0]):
                _mixer_out_copy(wo_hbm, stg_o, sem_o, j).start()
            for j in range(n_chunks):
                _mixer_out_copy(wo_hbm, stg_o, sem_o, j).wait()
                wo_bf[j * rows:(j + 1) * rows, :] = stg_o[j % stg_o.shape[0]].astype(BF16)
                if j + stg_o.shape[0] < n_chunks:
                    _mixer_out_copy(wo_hbm, stg_o, sem_o, j + stg_o.shape[0]).start()

        oc_ref, os_ref, yc_ref, ys_ref = mix_refs
        cat = jnp.concatenate([jnp.where(is_ctx, oc_ref[...], os_ref[...]),
                               jnp.where(is_ctx, yc_ref[...], ys_ref[...])], axis=1)
        x = x + _mod_vec(mod_ref, row, mod_base - 1) * jnp.dot(cat, wo_bf[...], preferred_element_type=F32)

    h = _norm_mod(x, g_ref[...], _mod_vec(mod_ref, row, mod_base),
                  _mod_vec(mod_ref, row, mod_base + 1)).astype(BF16)

    def start_chunk(j):
        for w_hbm, stg, sem, _, ff_axis in streams:
            _ffn_weight_copy(w_hbm, stg, sem, layer, j, ff_axis).start()

    def finish_chunk(j):
        for w_hbm, stg, sem, w_bf, ff_axis in streams:
            _ffn_weight_copy(w_hbm, stg, sem, layer, j, ff_axis).wait()
            w_bf[j] = stg[j % FF_STAGE_SLOTS].astype(BF16)

    def run(stream_weights):
        if stream_weights:
            for j in range(FF_STAGE_SLOTS):
                start_chunk(j)
        acc = None
        for j in range(FF_CHUNKS):
            if stream_weights:
                finish_chunk(j)
                if j + FF_STAGE_SLOTS < FF_CHUNKS:
                    start_chunk(j + FF_STAGE_SLOTS)
            a = jnp.dot(h, wg_bf[j], preferred_element_type=F32)
            b = jnp.dot(h, wu_bf[j], preferred_element_type=F32)
            if adaln_next is not None and j == 1:
                modn_ref[...] = _adaln_slab(*ada_refs, cond_scr, adaln_next)
            act = (a * jax.nn.sigmoid(a) * b).astype(BF16)
            y = jnp.dot(act, wd_bf[j], preferred_element_type=F32)
            acc = y if acc is None else acc + y
        res = x + 0.5 * _mod_vec(mod_ref, row, mod_base + 2) * acc
        if split_out:
            @pl.when(is_ctx)
            def _():
                o_refs[0][...] = res

            @pl.when(jnp.logical_not(is_ctx))
            def _():
                o_refs[1][...] = res
        else:
            o_refs[0][...] = res

    @pl.when(i == 0)
    def _():
        run(True)

    @pl.when(i > 0)
    def _():
        run(False)


def _ffn(xs, mod, g, wg, wu, wd, layer, mod_base, split_out=False, mixer_out=None, adaln_next=None):
    tm = FFN_TILE
    n_ctx_tiles = N_CTX_TOK // tm
    split_in = len(xs) == 2

    def tiles(width):
        return (pl.BlockSpec((tm, width), lambda i: (i, 0)),
                pl.BlockSpec((tm, width), lambda i: (jnp.minimum(i, n_ctx_tiles - 1), 0)),
                pl.BlockSpec((tm, width), lambda i: (jnp.maximum(i - n_ctx_tiles, 0), 0)))

    tok, ctx_tok, smp_tok = tiles(D_MODEL)
    full = jax.ShapeDtypeStruct((N_TOK, D_MODEL), F32)
    pair = [jax.ShapeDtypeStruct((N_CTX_TOK, D_MODEL), F32), jax.ShapeDtypeStruct((N_SMP_TOK, D_MODEL), F32)]
    hbm = pl.BlockSpec(memory_space=pl.ANY)
    in_specs = [ctx_tok, smp_tok] if split_in else [tok]
    operands = list(xs)
    scratch = [
        pltpu.VMEM((FF_CHUNKS, D_MODEL, FF_TILE), BF16),
        pltpu.VMEM((FF_CHUNKS, D_MODEL, FF_TILE), BF16),
        pltpu.VMEM((FF_CHUNKS, FF_TILE, D_MODEL), BF16),
        pltpu.VMEM((FF_STAGE_SLOTS, D_MODEL, FF_TILE), F32),
        pltpu.VMEM((FF_STAGE_SLOTS, D_MODEL, FF_TILE), F32),
        pltpu.VMEM((FF_STAGE_SLOTS, FF_TILE, D_MODEL), F32),
        pltpu.SemaphoreType.DMA((FF_STAGE_SLOTS,)),
        pltpu.SemaphoreType.DMA((FF_STAGE_SLOTS,)),
        pltpu.SemaphoreType.DMA((FF_STAGE_SLOTS,)),
    ]
    if mixer_out is not None:
        _, ctx_half, smp_half = tiles(NA_WIDTH)
        in_specs += [ctx_half, smp_half, ctx_half, smp_half]
        operands += list(mixer_out[:4])
    in_specs += [_mod_spec(mod), pl.BlockSpec((1, D_MODEL), lambda i: (0, 0))]
    operands += [mod, g.reshape(1, D_MODEL)]
    out_specs = [ctx_tok, smp_tok] if split_out else [tok]
    out_shape = pair if split_out else [full]
    if adaln_next is not None:
        ada_in, ada_out, ada_shape = _adaln_specs(adaln_next, N_TOK // tm)
        in_specs += ada_in
        operands += list(adaln_next[:4])
        out_specs.append(ada_out)
        out_shape.append(ada_shape)
        scratch.append(pltpu.VMEM((MOD_ROWS, D_MODEL), F32))
    if mixer_out is not None:
        in_specs.insert(len(in_specs) - (4 if adaln_next is not None else 0), hbm)
        operands.insert(len(operands) - (4 if adaln_next is not None else 0), mixer_out[4])
        scratch += [
            pltpu.VMEM((D_MODEL, D_MODEL), BF16),
            pltpu.VMEM((FF_STAGE_SLOTS, FF_TILE, D_MODEL), F32),
            pltpu.SemaphoreType.DMA((FF_STAGE_SLOTS,)),
        ]
    return pl.pallas_call(
        functools.partial(_ffn_kernel, layer=layer, mod_base=mod_base, split_in=split_in,
                          split_out=split_out, mixer_out=mixer_out is not None,
                          adaln_next=None if adaln_next is None else adaln_next[4]),
        grid=(N_TOK // tm,),
        in_specs=in_specs + [hbm, hbm, hbm],
        out_specs=out_specs,
        out_shape=out_shape,
        scratch_shapes=scratch,
        compiler_params=_cparams(1),
        name="ffn",
    )(*operands, wg, wu, wd)


def _head_rms_norm(z, g, ones_bd):
    z2 = z * z
    hi = z2.astype(BF16)
    lo = (z2 - hi.astype(F32)).astype(BF16)
    n = ones_bd.shape[0]
    parts = []
    for c in range(z.shape[1] // n):
        sl = slice(c * n, (c + 1) * n)
        parts.append(jnp.dot(hi[:, sl], ones_bd, preferred_element_type=F32)
                     + jnp.dot(lo[:, sl], ones_bd, preferred_element_type=F32))
    ss = jnp.concatenate(parts, axis=1)
    return z * lax.rsqrt(ss * (1.0 / HEAD_DIM) + EPS) * g


def _proj_kernel(*refs, mod_k0, adaln_layer):
    refs = list(refs)
    x_ref, mod_ref, g_ref, w_ref, qg_ref, kg_ref, ones_ref = refs[:7]
    ada_refs = refs[7:11] if adaln_layer is not None else None
    n_in = 7 if adaln_layer is None else 11
    q_ref, k_ref, v_ref, xb_ref, gb_ref, kout_ref, vout_ref = refs[n_in:n_in + 7]
    rest = refs[n_in + 7:]
    modn_ref = rest.pop(0) if adaln_layer is not None else None
    w_bf_ref = rest.pop(0)
    i = pl.program_id(0)

    @pl.when(i == 0)
    def _():
        w_bf_ref[...] = w_ref[...].astype(BF16)

    x = x_ref[...]
    row = _mod_row_of_tile(i)
    h = _norm_mod(x, g_ref[...], _mod_vec(mod_ref, row, mod_k0),
                  _mod_vec(mod_ref, row, mod_k0 + 1)).astype(BF16)

    def proj(part):
        return jnp.dot(h, w_bf_ref[:, part * NA_WIDTH:(part + 1) * NA_WIDTH], preferred_element_type=F32)

    ones_bd = ones_ref[...]
    q_raw = proj(0)
    k_raw = proj(1)
    q = _head_rms_norm(q_raw, jnp.tile(qg_ref[...], (1, NA_HEADS)), ones_bd) * (HEAD_DIM ** -0.5)
    q_ref[...] = q.astype(BF16)
    v = proj(2)
    if adaln_layer is not None:
        modn_ref[...] = _adaln_slab(*ada_refs, rest.pop(0), adaln_layer)
    k = _head_rms_norm(k_raw, jnp.tile(kg_ref[...], (1, NA_HEADS)), ones_bd)
    k_ref[...] = k.astype(BF16)
    xb = proj(3)
    v_ref[...] = v.astype(BF16)
    gb = proj(4)
    xb_ref[...] = xb
    gb_ref[...] = gb

    kt = [k[b * SEQ:(b + 1) * SEQ, :].T for b in range(TOKEN_TILE // SEQ)]
    vt = [v[b * SEQ:(b + 1) * SEQ, :].T for b in range(TOKEN_TILE // SEQ)]

    @pl.when(i < N_CTX_TILES)
    def _():
        for b in range(TOKEN_TILE // SEQ):
            kout_ref[b] = kt[b]
            vout_ref[b] = vt[b]


def _proj(x, mod, mod_k0, g, w_in, q_g, k_g, adaln_next=None):
    tm = TOKEN_TILE
    head = np.arange(2 * LANES) // HEAD_DIM
    ones_bd = jnp.asarray((head[:, None] == head[None, :]).astype(np.float32), dtype=BF16)
    tok = lambda i: (i, 0)
    const = lambda i: (0, 0)
    act_f32 = jax.ShapeDtypeStruct((N_TOK, NA_WIDTH), F32)
    act_bf16 = jax.ShapeDtypeStruct((N_TOK, NA_WIDTH), BF16)
    cache = jax.ShapeDtypeStruct((BATCH, NA_WIDTH, SEQ), F32)
    cache_spec = pl.BlockSpec((tm // SEQ, NA_WIDTH, SEQ), lambda i: (jnp.minimum(i, N_CTX_TILES - 1), 0, 0))
    in_specs = [
        pl.BlockSpec((tm, D_MODEL), tok),
        _mod_spec(mod),
        pl.BlockSpec((1, D_MODEL), const),
        _resident((D_MODEL, IN_WIDTH), const),
        pl.BlockSpec((1, HEAD_DIM), const),
        pl.BlockSpec((1, HEAD_DIM), const),
        _resident((2 * LANES, 2 * LANES), const),
    ]
    operands = [x, mod, g.reshape(1, D_MODEL), w_in, q_g.reshape(1, HEAD_DIM), k_g.reshape(1, HEAD_DIM), ones_bd]
    out_specs = [pl.BlockSpec((tm, NA_WIDTH), tok)] * 5 + [cache_spec, cache_spec]
    out_shape = [act_bf16, act_bf16, act_bf16, act_f32, act_f32, cache, cache]
    scratch = [pltpu.VMEM((D_MODEL, IN_WIDTH), BF16)]
    if adaln_next is not None:
        ada_in, ada_out, ada_shape = _adaln_specs(adaln_next, N_TOK // tm)
        in_specs += ada_in
        operands += list(adaln_next[:4])
        out_specs.append(ada_out)
        out_shape.append(ada_shape)
        scratch.append(pltpu.VMEM((MOD_ROWS, D_MODEL), F32))
    return pl.pallas_call(
        functools.partial(_proj_kernel, mod_k0=mod_k0,
                          adaln_layer=None if adaln_next is None else adaln_next[4]),
        grid=(N_TOK // tm,),
        in_specs=in_specs,
        out_specs=out_specs,
        out_shape=out_shape,
        scratch_shapes=scratch,
        compiler_params=_cparams(1),
        name="mixer_in_proj",
    )(*operands)


def _head_masks():
    lane = lax.broadcasted_iota(jnp.int32, (1, 2 * HEAD_DIM), 1)
    return [lane < HEAD_DIM, lane >= HEAD_DIM]


def _ctx_attn_kernel(q_ref, k_ref, v_ref, o_ref):
    masks = _head_masks()
    units = [(b, h) for b in range(q_ref.shape[0] // SEQ) for h in range(NA_HEADS)]

    def where(b, h):
        return slice(b * SEQ, (b + 1) * SEQ), slice(2 * HEAD_DIM * (h // 2), 2 * HEAD_DIM * (h // 2 + 1))

    def scores(b, h):
        rows, sl = where(b, h)
        q2 = q_ref[rows, sl]
        qm = jnp.where(masks[h % 2], q2, jnp.zeros_like(q2))
        return jnp.dot(qm, k_ref[rows, sl].T, preferred_element_type=F32)

    def attend(b, h, s):
        rows, sl = where(b, h)
        pe = jnp.exp(s - jnp.max(s, axis=-1, keepdims=True))
        den = jnp.sum(pe, axis=-1, keepdims=True)
        return jnp.dot(pe.astype(BF16), v_ref[rows, sl], preferred_element_type=F32) / den

    pending = scores(*units[0])
    out = None
    for n, (b, h) in enumerate(units):
        current = pending
        if n + 1 < len(units):
            pending = scores(*units[n + 1])
        o = attend(b, h, current)
        if h % 2 == 0:
            out = o
        else:
            rows, sl = where(b, h)
            o_ref[rows, sl] = jnp.where(masks[1], o, out).astype(BF16)


def _ctx_attn(q, k, v):
    blk = pl.BlockSpec((CTX_ATTN_SEQ_PER_STEP * SEQ, NA_WIDTH), lambda b: (b, 0))
    return pl.pallas_call(
        _ctx_attn_kernel,
        grid=(BATCH // CTX_ATTN_SEQ_PER_STEP,),
        in_specs=[blk, blk, blk],
        out_specs=blk,
        out_shape=jax.ShapeDtypeStruct((N_CTX_TOK, NA_WIDTH), BF16),
        compiler_params=_cparams(1),
        name="ctx_attention",
    )(q, k, v)


def _na_build_bias_table(rpb_ref, table_ref):
    qc = lax.broadcasted_iota(jnp.int32, (GRID_W, LANES), 0)
    lane = lax.broadcasted_iota(jnp.int32, (GRID_W, LANES), 1)
    kc = lane % GRID_W
    col_start = jnp.clip(qc - WIN_W // 2, 0, GRID_W - WIN_W)
    col_in = (kc >= col_start) & (kc < col_start + WIN_W)
    neg = jnp.full((GRID_W, LANES), -jnp.inf, F32)

    def toeplitz(h, dr, lane0):
        if dr < 0 or dr >= N_DR:
            return neg
        row = jnp.pad(rpb_ref[dr, h:h + 1, :], ((0, 0), (0, LANES - N_DC)))
        w = jnp.broadcast_to(row, (GRID_W, LANES))
        return pltpu.roll(w, (lane0 - (WIN_W - 1)) % LANES, 1, stride=1, stride_axis=0)

    for h in range(NA_HEADS):
        for i in range(N_DR_PAIRS):
            t = jnp.where(lane < GRID_W, toeplitz(h, i - 1, 0), toeplitz(h, i, GRID_W))
            table_ref[h, i] = jnp.where(col_in, t, neg)


def _na_kernel(*refs, adaln_layer):
    q_ref, k_ref, v_ref, kc_ref, vc_ref, rpb_ref = refs[:6]
    if adaln_layer is None:
        o_ref, table_ref = refs[6:]
    else:
        ada_refs = refs[6:10]
        o_ref, modn_ref, table_ref, cond_scr = refs[10:]
    b = pl.program_id(0)
    g = pl.program_id(1)

    @pl.when((b == 0) & (g == 0))
    def _():
        _na_build_bias_table(rpb_ref, table_ref)

    win_row0 = jnp.where(g < NA_GROUPS // 2, 0, ROWS - NA_K_ROWS)
    start = pl.multiple_of(win_row0 * GRID_W, GRID_W)
    q_row = g * NA_Q_ROWS + lax.broadcasted_iota(jnp.int32, (NA_Q, 1), 0) // GRID_W
    k_row = win_row0 + lax.broadcasted_iota(jnp.int32, (1, NA_K), 1) // GRID_W
    row_start = jnp.clip(q_row - KH // 2, 0, ROWS - KH)
    row_in = (k_row >= row_start) & (k_row < row_start + KH)
    masks = _head_masks()

    def pair_slab(p):
        return slice(2 * HEAD_DIM * p, 2 * HEAD_DIM * (p + 1))

    def scores(head):
        p, e = divmod(head, 2)
        sl = pair_slab(p)
        q2 = q_ref[:, sl]
        klt = k_ref[pl.ds(start, NA_K), sl].T
        kct = kc_ref[0, sl, :].astype(BF16)
        bias_rows = []
        for a in range(NA_Q_ROWS):
            tiles = []
            for m in range(NA_K_ROWS // 2):
                dr = win_row0 + 2 * m - (g * NA_Q_ROWS + a) + (WIN_H - 1)
                tiles.append(table_ref[head, jnp.clip(dr + 1, 0, N_DR_PAIRS - 1)])
            bias_rows.append(jnp.concatenate(tiles, axis=1))
        bias = jnp.concatenate(bias_rows, axis=0)
        qm = jnp.where(masks[e], q2, jnp.zeros_like(q2))
        s_loc = jnp.where(row_in, jnp.dot(qm, klt, preferred_element_type=F32) + bias, -jnp.inf)
        s_ctx = jnp.dot(qm, kct, preferred_element_type=F32)
        return s_loc, s_ctx

    def attend(head, s_loc, s_ctx):
        sl = pair_slab(head // 2)
        vl = v_ref[pl.ds(start, NA_K), sl]
        vct = vc_ref[0, sl, :].astype(BF16)
        m_max = jnp.maximum(jnp.max(s_loc, axis=-1, keepdims=True),
                            jnp.max(s_ctx, axis=-1, keepdims=True))
        p_loc = jnp.exp(s_loc - m_max)
        p_ctx = jnp.exp(s_ctx - m_max)
        den = jnp.sum(p_loc, axis=-1, keepdims=True) + jnp.sum(p_ctx, axis=-1, keepdims=True)
        return (jnp.dot(p_loc.astype(BF16), vl, preferred_element_type=F32)
                + lax.dot_general(p_ctx.astype(BF16), vct, (((1,), (1,)), ((), ())),
                                  preferred_element_type=F32)) / den

    pending = scores(0)
    out = None
    for head in range(NA_HEADS):
        current = pending
        if head + 1 < NA_HEADS:
            pending = scores(head + 1)
        o = attend(head, *current)
        if adaln_layer is not None and head == 0:
            modn_ref[...] = _adaln_slab(*ada_refs, cond_scr, adaln_layer)
        if head % 2 == 0:
            out = o
        else:
            o_ref[:, pair_slab(head // 2)] = jnp.where(masks[1], o, out).astype(BF16)


def _na_attn(q, k, v, k_ctx, v_ctx, rpb_e, adaln_next=None):
    smp_blk0 = N_CTX_TOK // DEC_SEQ
    q_blk0 = N_CTX_TOK // NA_Q
    kv = pl.BlockSpec((DEC_SEQ, NA_WIDTH), lambda b, g: (smp_blk0 + b, 0))
    ctx = pl.BlockSpec((1, NA_WIDTH, PAST_LEN), lambda b, g: (b, 0, 0))
    in_specs = [
        pl.BlockSpec((NA_Q, NA_WIDTH), lambda b, g: (q_blk0 + b * NA_GROUPS + g, 0)),
        kv, kv, ctx, ctx,
        pl.BlockSpec((N_DR, NA_HEADS, N_DC), lambda b, g: (0, 0, 0)),
    ]
    operands = [q, k, v, k_ctx, v_ctx, jnp.transpose(rpb_e, (1, 0, 2))]
    out_specs = [pl.BlockSpec((NA_Q, NA_WIDTH), lambda b, g: (b * NA_GROUPS + g, 0))]
    out_shape = [jax.ShapeDtypeStruct((N_SMP_TOK, NA_WIDTH), BF16)]
    scratch = [pltpu.VMEM((NA_HEADS, N_DR_PAIRS, GRID_W, LANES), F32)]
    if adaln_next is not None:
        ada_in, ada_out, ada_shape = _adaln_specs(adaln_next, DEC_BATCH * NA_GROUPS,
                                                  lambda b, g: b * NA_GROUPS + g)
        in_specs += ada_in
        operands += list(adaln_next[:4])
        out_specs.append(ada_out)
        out_shape.append(ada_shape)
        scratch.append(pltpu.VMEM((MOD_ROWS, D_MODEL), F32))
    return pl.pallas_call(
        functools.partial(_na_kernel, adaln_layer=None if adaln_next is None else adaln_next[4]),
        grid=(DEC_BATCH, NA_GROUPS),
        in_specs=in_specs,
        out_specs=out_specs,
        out_shape=out_shape,
        scratch_shapes=scratch,
        compiler_params=_cparams(2),
        name="neighbourhood_attention",
    )(*operands)


def _gelu_tanh(x):
    c0 = float(np.sqrt(2.0 / np.pi))
    inner = x * (c0 + (c0 * 0.044715) * (x * x))
    return (0.5 * x) * (1.0 + jnp.tanh(inner))


def _lru_build_gate_weights(wr_ref, wi_ref, w_scr):
    blocks_per_group = LRU_SUB // LRU_BLOCK
    w_scr[...] = jnp.zeros_like(w_scr)
    for d in range(2):
        for kind, w_ref in enumerate((wr_ref, wi_ref)):
            col0 = (2 * d + kind) * LRU_SUB
            for blk in range(LRU_BLOCKS):
                c, n = divmod(blk, blocks_per_group)
                r0 = n * LRU_BLOCK
                w_scr[c, r0:r0 + LRU_BLOCK, col0 + r0:col0 + r0 + LRU_BLOCK] = (
                    0.5 * w_ref[d * LRU_BLOCKS + blk]).astype(BF16)


def _lru_kernel(*refs, zero_state, t_len):
    refs = list(refs)
    xb_ref, gb_ref, cw_ref, cb_ref, wr_ref, wi_ref, br_ref, bi_ref, lam_ref = refs[:9]
    h0f_ref, h0b_ref = (None, None) if zero_state else refs[9:11]
    y_ref, hlf_ref, hlb_ref, af_ref, uf_ref, ab_ref, ub_ref, w_scr = refs[-8:]
    width = xb_ref.shape[1]
    n_seq = xb_ref.shape[0] // t_len
    n_blk = t_len // SUBLANES
    row = lax.broadcasted_iota(jnp.int32, (t_len, 1), 0)
    in_block = lax.broadcasted_iota(jnp.int32, (1, SUBLANES, 1), 1)

    @pl.when(pl.program_id(0) == 0)
    def _():
        _lru_build_gate_weights(wr_ref, wi_ref, w_scr)

    def shifted(z, s):
        rolled = pltpu.roll(z, (-s) % t_len, axis=0)
        ok = (row + s >= 0) & (row + s < t_len)
        return jnp.where(ok, rolled, 0.0)

    left = (CONV_W - 1) // 2
    for q, c in [(q, c) for q in range(n_seq) for c in range(width // LRU_SUB)]:
        rows = slice(q * t_len, (q + 1) * t_len)
        cs = slice(c * LRU_SUB, (c + 1) * LRU_SUB)
        x = xb_ref[rows, cs]
        xc = cb_ref[:, cs]
        for j in range(CONV_W):
            tap = x if j == left else shifted(x, j - left)
            xc = xc + tap * cw_ref[j:j + 1, cs]
        half_gates = jnp.dot(xc.astype(BF16), w_scr[c], preferred_element_type=F32)
        half_xc = 0.5 * xc
        for d, (a_ref, u_ref) in enumerate(((af_ref, uf_ref), (ab_ref, ub_ref))):
            t_r = jnp.tanh(half_gates[:, (2 * d) * LRU_SUB:(2 * d + 1) * LRU_SUB] + 0.5 * br_ref[d:d + 1, cs])
            t_i = jnp.tanh(half_gates[:, (2 * d + 1) * LRU_SUB:(2 * d + 2) * LRU_SUB] + 0.5 * bi_ref[d:d + 1, cs])
            lam = lam_ref[d:d + 1, cs]
            log_sig = jnp.minimum(lam, 0.0) - jnp.log1p(jnp.exp(-jnp.abs(lam)))
            half_c_log_sig = (0.5 * LRU_C) * log_sig
            log_a = t_r * half_c_log_sig + half_c_log_sig
            a = jnp.exp(log_a)
            var = -jnp.tanh(log_a) * (a * a + 1.0)
            u = jnp.where(var > 0.0, var * lax.rsqrt(var), 0.0) * ((t_i + 1.0) * half_xc)
            a = a.reshape(n_blk, SUBLANES, LRU_SUB)
            u = u.reshape(n_blk, SUBLANES, LRU_SUB)
            step = 1
            while step < SUBLANES:
                if d == 0:
                    ok, shift = in_block >= step, step
                else:
                    ok, shift = in_block < SUBLANES - step, SUBLANES - step
                a_prev = jnp.where(ok, pltpu.roll(a, shift, axis=1), 1.0)
                u_prev = jnp.where(ok, pltpu.roll(u, shift, axis=1), 0.0)
                u = u + a * u_prev
                a = a * a_prev
                step *= 2
            a_ref[rows, cs] = a.reshape(t_len, LRU_SUB)
            u_ref[rows, cs] = u.reshape(t_len, LRU_SUB)

    def body(i, carry):
        new = []
        for q, (cf, cb) in enumerate(carry):
            f0 = pl.multiple_of(q * t_len + i * SUBLANES, SUBLANES)
            b0 = pl.multiple_of(q * t_len + (n_blk - 1 - i) * SUBLANES, SUBLANES)
            hf = uf_ref[pl.ds(f0, SUBLANES), :] + af_ref[pl.ds(f0, SUBLANES), :] * cf
            hb = ub_ref[pl.ds(b0, SUBLANES), :] + ab_ref[pl.ds(b0, SUBLANES), :] * cb
            uf_ref[pl.ds(f0, SUBLANES), :] = hf
            ub_ref[pl.ds(b0, SUBLANES), :] = hb
            new.append((jnp.broadcast_to(hf[SUBLANES - 1:SUBLANES, :], (SUBLANES, width)),
                        jnp.broadcast_to(hb[0:1, :], (SUBLANES, width))))
        return tuple(new)

    if zero_state:
        init = tuple((jnp.zeros((SUBLANES, width), F32),) * 2 for _ in range(n_seq))
    else:
        init = tuple((jnp.broadcast_to(h0f_ref[q], (SUBLANES, width)),
                      jnp.broadcast_to(h0b_ref[q], (SUBLANES, width))) for q in range(n_seq))
    last = lax.fori_loop(0, n_blk, body, init)
    for q, (cf, cb) in enumerate(last):
        hlf_ref[q] = cf[0:1, :]
        hlb_ref[q] = cb[0:1, :]
    y_ref[...] = ((uf_ref[...] + ub_ref[...]) * _gelu_tanh(gb_ref[...])).astype(BF16)


def _lru(xb, gb, conv_w, conv_b, w_r, b_r, w_i, b_i, lam, h0, n_seq, t_len, tok_blk0, seq_per_step):
    rows = seq_per_step * t_len
    const2 = lambda s: (0, 0)
    const3 = lambda s: (0, 0, 0)
    blocks = pl.BlockSpec((2 * LRU_BLOCKS, LRU_BLOCK, LRU_BLOCK), const3)
    per_dir = pl.BlockSpec((2, LRU_WIDTH), const2)
    state = pl.BlockSpec((seq_per_step, 1, LRU_WIDTH), lambda s: (s, 0, 0))
    state_shape = jax.ShapeDtypeStruct((n_seq, 1, LRU_WIDTH), F32)
    return pl.pallas_call(
        functools.partial(_lru_kernel, zero_state=h0 is None, t_len=t_len),
        grid=(n_seq // seq_per_step,),
        in_specs=[
            pl.BlockSpec((rows, LRU_WIDTH), lambda s: (tok_blk0 + s, 0)),
            pl.BlockSpec((rows, LRU_WIDTH), lambda s: (tok_blk0 + s, 0)),
            pl.BlockSpec((CONV_W, LRU_WIDTH), const2),
            pl.BlockSpec((1, LRU_WIDTH), const2),
            blocks, blocks, per_dir, per_dir, per_dir,
        ] + ([] if h0 is None else [state, state]),
        out_specs=[pl.BlockSpec((rows, LRU_WIDTH), lambda s: (s, 0)), state, state],
        out_shape=[jax.ShapeDtypeStruct((n_seq * t_len, LRU_WIDTH), BF16), state_shape, state_shape],
        scratch_shapes=[pltpu.VMEM((rows, LRU_WIDTH), F32)] * 4
        + [pltpu.VMEM((LRU_WIDTH // LRU_SUB, LRU_SUB, 4 * LRU_SUB), BF16)],
        compiler_params=_cparams(1),
        name="rglru",
    )(xb, gb, conv_w, conv_b.reshape(1, LRU_WIDTH),
      w_r.reshape(2 * LRU_BLOCKS, LRU_BLOCK, LRU_BLOCK), w_i.reshape(2 * LRU_BLOCKS, LRU_BLOCK, LRU_BLOCK),
      b_r, b_i, lam, *(() if h0 is None else h0))


def _fourier_kernel(x_ref, mod_ref, g_ref, cs_ref, ct_ref, w_ref, o_ref, w_bf_ref, *, mod_row0, t_len):
    @pl.when(pl.program_id(0) == 0)
    def _():
        w_bf_ref[...] = w_ref[...].astype(BF16)

    x = x_ref[...]
    row = mod_row0 + pl.program_id(0) if mod_row0 else 0
    h = _norm_mod(x, g_ref[...], _mod_vec(mod_ref, row, 3), _mod_vec(mod_ref, row, 4)).astype(BF16)
    cos_parts, sin_parts = [], []
    for g in range(FOURIER_GROUPS):
        ab = jnp.dot(h[:, g * GROUP_W:(g + 1) * GROUP_W], cs_ref[...], preferred_element_type=F32)
        cos_parts.append(ab[:, :GROUP_W])
        sin_parts.append(ab[:, GROUP_W:])
    cos_all = jnp.concatenate(cos_parts, axis=1).astype(BF16)
    sin_all = jnp.concatenate(sin_parts, axis=1).astype(BF16)
    f_parts = []
    for q in range(x.shape[0] // t_len):
        rows = slice(q * t_len, (q + 1) * t_len)
        stacked = jnp.concatenate([cos_all[rows], sin_all[rows]], axis=0)
        f_parts.append(jnp.dot(ct_ref[...], stacked, preferred_element_type=F32))
    f = jnp.concatenate(f_parts, axis=0) * ((t_len * GROUP_W) ** -0.5)
    y = jnp.dot(f.astype(BF16), w_bf_ref[...], preferred_element_type=F32)
    o_ref[...] = x + _mod_vec(mod_ref, row, 5) * y


def _dft_tables(t_len):
    def cos_sin(n):
        jk = np.outer(np.arange(n), np.arange(n)) % n
        ang = 2.0 * np.pi * jk.astype(np.float64) / n
        return np.cos(ang), np.sin(ang)

    cc, sc = cos_sin(GROUP_W)
    ct, st = cos_sin(t_len)
    chan = jnp.asarray(np.concatenate([cc, sc], axis=1).astype(np.float32)).astype(BF16)
    time = jnp.asarray(np.concatenate([ct, -st], axis=1).astype(np.float32)).astype(BF16)
    return chan, time


def _fourier(x, mod, g, w_out, n_seq, t_len, tok_blk0, mod_row0, seq_per_step):
    assert seq_per_step == 1 or mod_row0 == 0
    chan, time = _dft_tables(t_len)
    rows = seq_per_step * t_len
    seq = lambda s: (tok_blk0 + s, 0)
    const = lambda s: (0, 0)
    return pl.pallas_call(
        functools.partial(_fourier_kernel, mod_row0=mod_row0, t_len=t_len),
        grid=(n_seq // seq_per_step,),
        in_specs=[
            pl.BlockSpec((rows, D_MODEL), seq),
            _mod_spec(mod),
            pl.BlockSpec((1, D_MODEL), const),
            _resident((GROUP_W, 2 * GROUP_W), const),
            _resident((t_len, 2 * t_len), const),
            _resident((D_MODEL, D_MODEL), const),
        ],
        out_specs=pl.BlockSpec((rows, D_MODEL), seq),
        out_shape=jax.ShapeDtypeStruct((N_TOK, D_MODEL), F32),
        input_output_aliases={0: 0},
        scratch_shapes=[pltpu.VMEM((D_MODEL, D_MODEL), BF16)],
        compiler_params=_cparams(1),
        name="fourier_mixer",
    )(x, mod, g.reshape(1, D_MODEL), chan, time, w_out)


def _cache_layout(t):
    return jnp.transpose(t.reshape(BATCH, 1, NA_HEADS, HEAD_DIM, SEQ), (0, 1, 4, 2, 3))


def kernel(x_prompt, x_sample, cache_k, cache_v, state_lru_fwd, state_lru_bwd, c, c_ctx, w_ada, b_ada, norm_g, ffn1_gate, ffn1_up, ffn1_down, ffn2_gate, ffn2_up, ffn2_down, w_in, q_norm_g, k_norm_g, rpb, conv_w, conv_b, lru_w_r, lru_b_r, lru_w_i, lru_b_i, lru_lambda, w_out_ab, w_out_c):
    assert DEPTH == 2, "one neighbourhood/RG-LRU layer followed by one Fourier layer"
    c_ctx2 = c_ctx.reshape(1, D_MODEL)
    ada = (c_ctx2, c, w_ada, b_ada)
    mod0_ffn1 = _adaln(ada + (0, 0, 3 * D_MODEL), 3)

    ffn1 = (ffn1_gate, ffn1_up, ffn1_down)
    ffn2 = (ffn2_gate, ffn2_up, ffn2_down)

    x, mod0_rest = _ffn((x_prompt.reshape(N_CTX_TOK, D_MODEL), x_sample.reshape(N_SMP_TOK, D_MODEL)),
                        mod0_ffn1, norm_g[0, 0], *ffn1, 0, 0,
                        adaln_next=ada + (0, 3 * D_MODEL, (N_MOD - 3) * D_MODEL))
    q, k, v, xb, gb, new_k, new_v = _proj(x, mod0_rest, 0, norm_g[0, 1], w_in[0], q_norm_g[0], k_norm_g[0])
    o_ctx = _ctx_attn(q, k, v)
    o_smp, mod1 = _na_attn(q, k, v,
                           jnp.transpose(cache_k[:, 0], (0, 2, 3, 1)).reshape(DEC_BATCH, NA_WIDTH, PAST_LEN),
                           jnp.transpose(cache_v[:, 0], (0, 2, 3, 1)).reshape(DEC_BATCH, NA_WIDTH, PAST_LEN),
                           rpb[0], adaln_next=ada + (1, 0, MOD_WIDTH))
    lru_prm = (conv_w[0], conv_b[0], lru_w_r[0], lru_b_r[0], lru_w_i[0], lru_b_i[0], lru_lambda[0])
    yb_ctx, new_hf, new_hb = _lru(xb, gb, *lru_prm, None, BATCH, SEQ, 0, CTX_SEQ_PER_STEP)
    yb_smp, _, _ = _lru(xb, gb, *lru_prm, (state_lru_fwd, state_lru_bwd),
                        DEC_BATCH, DEC_SEQ, N_CTX_TOK // DEC_SEQ, 1)
    (x,) = _ffn((x,), mod0_rest, norm_g[0, 2], *ffn2, 0, 3, mixer_out=(o_ctx, o_smp, yb_ctx, yb_smp, w_out_ab))

    (x,) = _ffn((x,), mod1, norm_g[1, 0], *ffn1, 1, 0)
    x = _fourier(x, mod1, norm_g[1, 1], w_out_c[0], BATCH, SEQ, 0, 0, CTX_SEQ_PER_STEP)
    x = _fourier(x, mod1, norm_g[1, 1], w_out_c[0], DEC_BATCH, DEC_SEQ, N_CTX_TOK // DEC_SEQ, 1, 1)
    y_prompt, y_sample = _ffn((x,), mod1, norm_g[1, 2], *ffn2, 1, 6, split_out=True)

    return (y_prompt.reshape(BATCH, SEQ, D_MODEL), y_sample.reshape(DEC_BATCH, DEC_SEQ, D_MODEL),
            _cache_layout(new_k), _cache_layout(new_v),
            new_hf, new_hb)
```

```python
import functools

import numpy as np
import jax
import jax.numpy as jnp
from jax import lax
from jax.experimental import pallas as pl
from jax.experimental.pallas import tpu as pltpu

F32 = jnp.float32
BF16 = jnp.bfloat16

D_MODEL = 1024
BATCH = 16
SEQ = 256
DEPTH = 2
DEC_BATCH = 2
DEC_SEQ = 1024
PAST_LEN = 256
GRID_W = 64
HEAD_DIM = 64
NA_WIDTH = 512
NA_HEADS = 8
WIN_H = 8
WIN_W = 16
LRU_WIDTH = 512
LRU_BLOCKS = 8
LRU_BLOCK = 64
LRU_C = 8.0
LRU_SUB = 256
CONV_W = 4
FOURIER_GROUPS = 4
GROUP_W = D_MODEL // FOURIER_GROUPS
D_FF = 2816
N_MOD = 9
IN_WIDTH = 3 * NA_WIDTH + 2 * LRU_WIDTH
EPS = 1e-6

N_CTX_TOK = BATCH * SEQ
N_SMP_TOK = DEC_BATCH * DEC_SEQ
N_TOK = N_CTX_TOK + N_SMP_TOK
MOD_ROWS = 8
MOD_WIDTH = N_MOD * D_MODEL
ROWS = DEC_SEQ // GRID_W
KH = min(WIN_H, ROWS)

TOKEN_TILE = 512
N_CTX_TILES = N_CTX_TOK // TOKEN_TILE
CTX_SEQ_PER_STEP = 2
CTX_ATTN_SEQ_PER_STEP = 4
FFN_TILE = 512
FF_TILE = 256
FF_CHUNKS = D_FF // FF_TILE
FF_STAGE_SLOTS = 2
SUBLANES = 8
LANES = 128
VMEM_LIMIT = 56 * 1024 * 1024

NA_Q_ROWS = 4
NA_GROUPS = ROWS // NA_Q_ROWS
NA_K_ROWS = 12
NA_Q = NA_Q_ROWS * GRID_W
NA_K = NA_K_ROWS * GRID_W
N_DR = 2 * WIN_H - 1
N_DC = 2 * WIN_W - 1
N_DR_PAIRS = N_DR + 1


def _cparams(n_axes):
    return pltpu.CompilerParams(
        dimension_semantics=("arbitrary",) * n_axes, vmem_limit_bytes=VMEM_LIMIT)


def _resident(block_shape, index_map):
    return pl.BlockSpec(block_shape, index_map, pipeline_mode=pl.Buffered(1))


def _mod_spec(mod):
    return _resident(mod.shape, lambda i: (0, 0))


def _norm_g_spec():
    return pl.BlockSpec((3, DEPTH, D_MODEL), lambda *ids: (0, 0, 0))


def _norm_gain(g_ref, sub, layer):
    return g_ref[sub, layer:layer + 1, :]


def _mod_row_of_tile(i, tile=TOKEN_TILE):
    n_ctx_tiles = N_CTX_TOK // tile
    tiles_per_seq = DEC_SEQ // tile
    return jnp.where(i < n_ctx_tiles, 0, 1 + (i - n_ctx_tiles) // tiles_per_seq)


def _mod_vec(mod_ref, row, k):
    return mod_ref[pl.ds(row, 1), k * D_MODEL:(k + 1) * D_MODEL]


def _norm_mod(x, g, shift, scale):
    ms = jnp.mean(x * x, axis=-1, keepdims=True)
    return (x * lax.rsqrt(ms + EPS)) * (g * (1.0 + scale)) + shift


def _adaln_slab(cctx_ref, c_ref, w_ref, b_ref, cond_ref, layer):
    cond_ref[...] = jnp.zeros_like(cond_ref)
    cond_ref[0:1, :] = cctx_ref[...]
    cond_ref[1:1 + DEC_BATCH, :] = c_ref[...]
    cond = cond_ref[...]
    s = (cond * jax.nn.sigmoid(cond)).astype(BF16)
    return jnp.dot(s, w_ref[...].astype(BF16), preferred_element_type=F32) + b_ref[layer:layer + 1, :]


def _adaln_specs(job, n_steps, step=lambda i: i):
    layer, col0, n_cols = job[4:]
    slab = n_cols // n_steps
    assert n_cols % n_steps == 0 and slab % LANES == 0 and col0 % slab == 0
    blk0 = col0 // slab
    return ([pl.BlockSpec((1, D_MODEL), lambda *ids: (0, 0)),
             pl.BlockSpec((DEC_BATCH, D_MODEL), lambda *ids: (0, 0)),
             pl.BlockSpec((None, D_MODEL, slab), lambda *ids: (layer, 0, blk0 + step(*ids))),
             pl.BlockSpec((DEPTH, slab), lambda *ids: (0, blk0 + step(*ids)))],
            pl.BlockSpec((MOD_ROWS, slab), lambda *ids: (0, step(*ids))),
            jax.ShapeDtypeStruct((MOD_ROWS, n_cols), F32))


def _adaln_kernel(cctx_ref, c_ref, w_ref, b_ref, o_ref, cond_ref, *, layer):
    o_ref[...] = _adaln_slab(cctx_ref, c_ref, w_ref, b_ref, cond_ref, layer)


def _adaln(job, n_steps):
    in_specs, out_spec, out_shape = _adaln_specs(job, n_steps)
    return pl.pallas_call(
        functools.partial(_adaln_kernel, layer=job[4]),
        grid=(n_steps,),
        in_specs=in_specs,
        out_specs=out_spec,
        out_shape=out_shape,
        scratch_shapes=[pltpu.VMEM((MOD_ROWS, D_MODEL), F32)],
        compiler_params=_cparams(1),
        name="adaln",
    )(*job[:4])


def _ffn_weight_copy(w_hbm, stage_ref, sem_ref, layer, j, ff_axis):
    ff = pl.ds(j * FF_TILE, FF_TILE)
    src = w_hbm.at[layer, :, ff] if ff_axis == 1 else w_hbm.at[layer, ff, :]
    slot = j % FF_STAGE_SLOTS
    return pltpu.make_async_copy(src, stage_ref.at[slot], sem_ref.at[slot])


def _mixer_out_copy(w_hbm, stage_ref, sem_ref, j):
    rows = stage_ref.shape[1]
    slot = j % stage_ref.shape[0]
    return pltpu.make_async_copy(w_hbm.at[0, pl.ds(j * rows, rows), :], stage_ref.at[slot], sem_ref.at[slot])


def _ffn_kernel(*refs, layer, norm_sub, mod_base, split_in, split_out, mixer_out, adaln_next):
    refs = list(refs)
    take = lambda n: [refs.pop(0) for _ in range(n)]
    x_refs = take(2 if split_in else 1)
    mix_refs = take(4) if mixer_out else None
    mod_ref, g_ref = take(2)
    wo_hbm = take(1)[0] if mixer_out else None
    ada_refs = take(4) if adaln_next is not None else None
    wg_hbm, wu_hbm, wd_hbm = take(3)
    o_refs = take(2 if split_out else 1)
    modn_ref = take(1)[0] if adaln_next is not None else None
    wg_bf, wu_bf, wd_bf, stg_g, stg_u, stg_d, sem_g, sem_u, sem_d = take(9)
    cond_scr = take(1)[0] if adaln_next is not None else None
    streams = ((wg_hbm, stg_g, sem_g, wg_bf, 1), (wu_hbm, stg_u, sem_u, wu_bf, 1),
               (wd_hbm, stg_d, sem_d, wd_bf, 0))

    i = pl.program_id(0)
    is_ctx = i < N_CTX_TOK // FFN_TILE
    if split_in:
        x = jnp.where(is_ctx, x_refs[0][...], x_refs[1][...])
    else:
        x = x_refs[0][...]
    row = _mod_row_of_tile(i, FFN_TILE)

    if mixer_out:
        wo_bf, stg_o, sem_o = take(3)
        rows = stg_o.shape[1]
        n_chunks = D_MODEL // rows

        @pl.when(i == 0)
        def _():
            for j in range(stg_o.shape[0]):
                _mixer_out_copy(wo_hbm, stg_o, sem_o, j).start()
            for j in range(n_chunks):
                _mixer_out_copy(wo_hbm, stg_o, sem_o, j).wait()
                wo_bf[j * rows:(j + 1) * rows, :] = stg_o[j % stg_o.shape[0]].astype(BF16)
                if j + stg_o.shape[0] < n_chunks:
                    _mixer_out_copy(wo_hbm, stg_o, sem_o, j + stg_o.shape[0]).start()

        oc_ref, os_ref, yc_ref, ys_ref = mix_refs
        cat = jnp.concatenate([jnp.where(is_ctx, oc_ref[...], os_ref[...]),
                               jnp.where(is_ctx, yc_ref[...], ys_ref[...])], axis=1)
        x = x + _mod_vec(mod_ref, row, mod_base - 1) * jnp.dot(cat, wo_bf[...], preferred_element_type=F32)

    h = _norm_mod(x, _norm_gain(g_ref, norm_sub, layer), _mod_vec(mod_ref, row, mod_base),
                  _mod_vec(mod_ref, row, mod_base + 1)).astype(BF16)

    def start_chunk(j):
        for w_hbm, stg, sem, _, ff_axis in streams:
            _ffn_weight_copy(w_hbm, stg, sem, layer, j, ff_axis).start()

    def finish_chunk(j):
        for w_hbm, stg, sem, w_bf, ff_axis in streams:
            _ffn_weight_copy(w_hbm, stg, sem, layer, j, ff_axis).wait()
            w_bf[j] = stg[j % FF_STAGE_SLOTS].astype(BF16)

    def run(stream_weights):
        if stream_weights:
            for j in range(FF_STAGE_SLOTS):
                start_chunk(j)
        acc = None
        for j in range(FF_CHUNKS):
            if stream_weights:
                finish_chunk(j)
                if j + FF_STAGE_SLOTS < FF_CHUNKS:
                    start_chunk(j + FF_STAGE_SLOTS)
            a = jnp.dot(h, wg_bf[j], preferred_element_type=F32)
            b = jnp.dot(h, wu_bf[j], preferred_element_type=F32)
            if adaln_next is not None and j == 1:
                modn_ref[...] = _adaln_slab(*ada_refs, cond_scr, adaln_next)
            act = (a * jax.nn.sigmoid(a) * b).astype(BF16)
            y = jnp.dot(act, wd_bf[j], preferred_element_type=F32)
            acc = y if acc is None else acc + y
        res = x + 0.5 * _mod_vec(mod_ref, row, mod_base + 2) * acc
        if split_out:
            @pl.when(is_ctx)
            def _():
                o_refs[0][...] = res

            @pl.when(jnp.logical_not(is_ctx))
            def _():
                o_refs[1][...] = res
        else:
            o_refs[0][...] = res

    @pl.when(i == 0)
    def _():
        run(True)

    @pl.when(i > 0)
    def _():
        run(False)


def _ffn(xs, mod, norm_g, norm_sub, wg, wu, wd, layer, mod_base, split_out=False, mixer_out=None,
         adaln_next=None):
    tm = FFN_TILE
    n_ctx_tiles = N_CTX_TOK // tm
    split_in = len(xs) == 2

    def tiles(width):
        return (pl.BlockSpec((tm, width), lambda i: (i, 0)),
                pl.BlockSpec((tm, width), lambda i: (jnp.minimum(i, n_ctx_tiles - 1), 0)),
                pl.BlockSpec((tm, width), lambda i: (jnp.maximum(i - n_ctx_tiles, 0), 0)))

    tok, ctx_tok, smp_tok = tiles(D_MODEL)
    full = jax.ShapeDtypeStruct((N_TOK, D_MODEL), F32)
    pair = [jax.ShapeDtypeStruct((N_CTX_TOK, D_MODEL), F32), jax.ShapeDtypeStruct((N_SMP_TOK, D_MODEL), F32)]
    hbm = pl.BlockSpec(memory_space=pl.ANY)
    in_specs = [ctx_tok, smp_tok] if split_in else [tok]
    operands = list(xs)
    scratch = [
        pltpu.VMEM((FF_CHUNKS, D_MODEL, FF_TILE), BF16),
        pltpu.VMEM((FF_CHUNKS, D_MODEL, FF_TILE), BF16),
        pltpu.VMEM((FF_CHUNKS, FF_TILE, D_MODEL), BF16),
        pltpu.VMEM((FF_STAGE_SLOTS, D_MODEL, FF_TILE), F32),
        pltpu.VMEM((FF_STAGE_SLOTS, D_MODEL, FF_TILE), F32),
        pltpu.VMEM((FF_STAGE_SLOTS, FF_TILE, D_MODEL), F32),
        pltpu.SemaphoreType.DMA((FF_STAGE_SLOTS,)),
        pltpu.SemaphoreType.DMA((FF_STAGE_SLOTS,)),
        pltpu.SemaphoreType.DMA((FF_STAGE_SLOTS,)),
    ]
    if mixer_out is not None:
        _, ctx_half, smp_half = tiles(NA_WIDTH)
        in_specs += [ctx_half, smp_half, ctx_half, smp_half]
        operands += list(mixer_out[:4])
    in_specs += [_mod_spec(mod), _norm_g_spec()]
    operands += [mod, norm_g]
    out_specs = [ctx_tok, smp_tok] if split_out else [tok]
    out_shape = pair if split_out else [full]
    if adaln_next is not None:
        ada_in, ada_out, ada_shape = _adaln_specs(adaln_next, N_TOK // tm)
        in_specs += ada_in
        operands += list(adaln_next[:4])
        out_specs.append(ada_out)
        out_shape.append(ada_shape)
        scratch.append(pltpu.VMEM((MOD_ROWS, D_MODEL), F32))
    if mixer_out is not None:
        in_specs.insert(len(in_specs) - (4 if adaln_next is not None else 0), hbm)
        operands.insert(len(operands) - (4 if adaln_next is not None else 0), mixer_out[4])
        scratch += [
            pltpu.VMEM((D_MODEL, D_MODEL), BF16),
            pltpu.VMEM((FF_STAGE_SLOTS, FF_TILE, D_MODEL), F32),
            pltpu.SemaphoreType.DMA((FF_STAGE_SLOTS,)),
        ]
    return pl.pallas_call(
        functools.partial(_ffn_kernel, layer=layer, norm_sub=norm_sub, mod_base=mod_base, split_in=split_in,
                          split_out=split_out, mixer_out=mixer_out is not None,
                          adaln_next=None if adaln_next is None else adaln_next[4]),
        grid=(N_TOK // tm,),
        in_specs=in_specs + [hbm, hbm, hbm],
        out_specs=out_specs,
        out_shape=out_shape,
        scratch_shapes=scratch,
        compiler_params=_cparams(1),
        name="ffn",
    )(*operands, wg, wu, wd)


def _head_rms_norm(z, g, ones_bd):
    z2 = z * z
    hi = z2.astype(BF16)
    lo = (z2 - hi.astype(F32)).astype(BF16)
    n = ones_bd.shape[0]
    parts = []
    for c in range(z.shape[1] // n):
        sl = slice(c * n, (c + 1) * n)
        parts.append(jnp.dot(hi[:, sl], ones_bd, preferred_element_type=F32)
                     + jnp.dot(lo[:, sl], ones_bd, preferred_element_type=F32))
    ss = jnp.concatenate(parts, axis=1)
    return z * lax.rsqrt(ss * (1.0 / HEAD_DIM) + EPS) * g


def _proj_kernel(*refs, mod_k0, adaln_layer):
    refs = list(refs)
    x_ref, mod_ref, g_ref, w_ref, qg_ref, kg_ref, ones_ref = refs[:7]
    ada_refs = refs[7:11] if adaln_layer is not None else None
    n_in = 7 if adaln_layer is None else 11
    q_ref, k_ref, v_ref, xb_ref, gb_ref, kout_ref, vout_ref = refs[n_in:n_in + 7]
    rest = refs[n_in + 7:]
    modn_ref = rest.pop(0) if adaln_layer is not None else None
    w_bf_ref = rest.pop(0)
    i = pl.program_id(0)

    @pl.when(i == 0)
    def _():
        w_bf_ref[...] = w_ref[...].astype(BF16)

    x = x_ref[...]
    row = _mod_row_of_tile(i)
    h = _norm_mod(x, _norm_gain(g_ref, 1, 0), _mod_vec(mod_ref, row, mod_k0),
                  _mod_vec(mod_ref, row, mod_k0 + 1)).astype(BF16)

    def proj(part):
        return jnp.dot(h, w_bf_ref[:, part * NA_WIDTH:(part + 1) * NA_WIDTH], preferred_element_type=F32)

    ones_bd = ones_ref[...]
    q_raw = proj(0)
    k_raw = proj(1)
    q = _head_rms_norm(q_raw, jnp.tile(qg_ref[...], (1, NA_HEADS)), ones_bd) * (HEAD_DIM ** -0.5)
    q_ref[...] = q.astype(BF16)
    v = proj(2)
    if adaln_layer is not None:
        modn_ref[...] = _adaln_slab(*ada_refs, rest.pop(0), adaln_layer)
    k = _head_rms_norm(k_raw, jnp.tile(kg_ref[...], (1, NA_HEADS)), ones_bd)
    k_ref[...] = k.astype(BF16)
    xb = proj(3)
    v_ref[...] = v.astype(BF16)
    gb = proj(4)
    xb_ref[...] = xb
    gb_ref[...] = gb

    kt = [k[b * SEQ:(b + 1) * SEQ, :].T for b in range(TOKEN_TILE // SEQ)]
    vt = [v[b * SEQ:(b + 1) * SEQ, :].T for b in range(TOKEN_TILE // SEQ)]

    @pl.when(i < N_CTX_TILES)
    def _():
        for b in range(TOKEN_TILE // SEQ):
            kout_ref[b] = kt[b]
            vout_ref[b] = vt[b]


def _proj(x, mod, mod_k0, norm_g, w_in, q_g, k_g, adaln_next=None):
    tm = TOKEN_TILE
    head = np.arange(2 * LANES) // HEAD_DIM
    ones_bd = jnp.asarray((head[:, None] == head[None, :]).astype(np.float32), dtype=BF16)
    tok = lambda i: (i, 0)
    const = lambda i: (0, 0)
    act_f32 = jax.ShapeDtypeStruct((N_TOK, NA_WIDTH), F32)
    act_bf16 = jax.ShapeDtypeStruct((N_TOK, NA_WIDTH), BF16)
    cache = jax.ShapeDtypeStruct((BATCH, NA_WIDTH, SEQ), F32)
    cache_spec = pl.BlockSpec((tm // SEQ, NA_WIDTH, SEQ), lambda i: (jnp.minimum(i, N_CTX_TILES - 1), 0, 0))
    in_specs = [
        pl.BlockSpec((tm, D_MODEL), tok),
        _mod_spec(mod),
        _norm_g_spec(),
        _resident((D_MODEL, IN_WIDTH), const),
        pl.BlockSpec((1, HEAD_DIM), const),
        pl.BlockSpec((1, HEAD_DIM), const),
        _resident((2 * LANES, 2 * LANES), const),
    ]
    operands = [x, mod, norm_g, w_in, q_g.reshape(1, HEAD_DIM), k_g.reshape(1, HEAD_DIM), ones_bd]
    out_specs = [pl.BlockSpec((tm, NA_WIDTH), tok)] * 5 + [cache_spec, cache_spec]
    out_shape = [act_bf16, act_bf16, act_bf16, act_f32, act_f32, cache, cache]
    scratch = [pltpu.VMEM((D_MODEL, IN_WIDTH), BF16)]
    if adaln_next is not None:
        ada_in, ada_out, ada_shape = _adaln_specs(adaln_next, N_TOK // tm)
        in_specs += ada_in
        operands += list(adaln_next[:4])
        out_specs.append(ada_out)
        out_shape.append(ada_shape)
        scratch.append(pltpu.VMEM((MOD_ROWS, D_MODEL), F32))
    return pl.pallas_call(
        functools.partial(_proj_kernel, mod_k0=mod_k0,
                          adaln_layer=None if adaln_next is None else adaln_next[4]),
        grid=(N_TOK // tm,),
        in_specs=in_specs,
        out_specs=out_specs,
        out_shape=out_shape,
        scratch_shapes=scratch,
        compiler_params=_cparams(1),
        name="mixer_in_proj",
    )(*operands)


def _head_masks():
    lane = lax.broadcasted_iota(jnp.int32, (1, 2 * HEAD_DIM), 1)
    return [lane < HEAD_DIM, lane >= HEAD_DIM]


def _ctx_attn_kernel(q_ref, k_ref, v_ref, o_ref):
    masks = _head_masks()
    units = [(b, h) for b in range(q_ref.shape[0] // SEQ) for h in range(NA_HEADS)]

    def where(b, h):
        return slice(b * SEQ, (b + 1) * SEQ), slice(2 * HEAD_DIM * (h // 2), 2 * HEAD_DIM * (h // 2 + 1))

    def scores(b, h):
        rows, sl = where(b, h)
        q2 = q_ref[rows, sl]
        qm = jnp.where(masks[h % 2], q2, jnp.zeros_like(q2))
        return jnp.dot(qm, k_ref[rows, sl].T, preferred_element_type=F32)

    def attend(b, h, s):
        rows, sl = where(b, h)
        pe = jnp.exp(s - jnp.max(s, axis=-1, keepdims=True))
        den = jnp.sum(pe, axis=-1, keepdims=True)
        return jnp.dot(pe.astype(BF16), v_ref[rows, sl], preferred_element_type=F32) / den

    pending = scores(*units[0])
    out = None
    for n, (b, h) in enumerate(units):
        current = pending
        if n + 1 < len(units):
            pending = scores(*units[n + 1])
        o = attend(b, h, current)
        if h % 2 == 0:
            out = o
        else:
            rows, sl = where(b, h)
            o_ref[rows, sl] = jnp.where(masks[1], o, out).astype(BF16)


def _ctx_attn(q, k, v):
    blk = pl.BlockSpec((CTX_ATTN_SEQ_PER_STEP * SEQ, NA_WIDTH), lambda b: (b, 0))
    return pl.pallas_call(
        _ctx_attn_kernel,
        grid=(BATCH // CTX_ATTN_SEQ_PER_STEP,),
        in_specs=[blk, blk, blk],
        out_specs=blk,
        out_shape=jax.ShapeDtypeStruct((N_CTX_TOK, NA_WIDTH), BF16),
        compiler_params=_cparams(1),
        name="ctx_attention",
    )(q, k, v)


def _na_build_bias_table(rpb_ref, table_ref):
    qc = lax.broadcasted_iota(jnp.int32, (GRID_W, LANES), 0)
    lane = lax.broadcasted_iota(jnp.int32, (GRID_W, LANES), 1)
    kc = lane % GRID_W
    col_start = jnp.clip(qc - WIN_W // 2, 0, GRID_W - WIN_W)
    col_in = (kc >= col_start) & (kc < col_start + WIN_W)
    neg = jnp.full((GRID_W, LANES), -jnp.inf, F32)

    def toeplitz(h, dr, lane0):
        if dr < 0 or dr >= N_DR:
            return neg
        row = jnp.pad(rpb_ref[dr, h:h + 1, :], ((0, 0), (0, LANES - N_DC)))
        w = jnp.broadcast_to(row, (GRID_W, LANES))
        return pltpu.roll(w, (lane0 - (WIN_W - 1)) % LANES, 1, stride=1, stride_axis=0)

    for h in range(NA_HEADS):
        for i in range(N_DR_PAIRS):
            t = jnp.where(lane < GRID_W, toeplitz(h, i - 1, 0), toeplitz(h, i, GRID_W))
            table_ref[h, i] = jnp.where(col_in, t, neg)


def _na_kernel(*refs, adaln_layer):
    q_ref, k_ref, v_ref, kc_ref, vc_ref, rpb_ref = refs[:6]
    if adaln_layer is None:
        o_ref, table_ref = refs[6:]
    else:
        ada_refs = refs[6:10]
        o_ref, modn_ref, table_ref, cond_scr = refs[10:]
    b = pl.program_id(0)
    g = pl.program_id(1)

    @pl.when((b == 0) & (g == 0))
    def _():
        _na_build_bias_table(rpb_ref, table_ref)

    win_row0 = jnp.where(g < NA_GROUPS // 2, 0, ROWS - NA_K_ROWS)
    start = pl.multiple_of(win_row0 * GRID_W, GRID_W)
    q_row = g * NA_Q_ROWS + lax.broadcasted_iota(jnp.int32, (NA_Q, 1), 0) // GRID_W
    k_row = win_row0 + lax.broadcasted_iota(jnp.int32, (1, NA_K), 1) // GRID_W
    row_start = jnp.clip(q_row - KH // 2, 0, ROWS - KH)
    row_in = (k_row >= row_start) & (k_row < row_start + KH)
    masks = _head_masks()

    def pair_slab(p):
        return slice(2 * HEAD_DIM * p, 2 * HEAD_DIM * (p + 1))

    def scores(head):
        p, e = divmod(head, 2)
        sl = pair_slab(p)
        q2 = q_ref[:, sl]
        klt = k_ref[pl.ds(start, NA_K), sl].T
        kct = kc_ref[0, sl, :].astype(BF16)
        bias_rows = []
        for a in range(NA_Q_ROWS):
            tiles = []
            for m in range(NA_K_ROWS // 2):
                dr = win_row0 + 2 * m - (g * NA_Q_ROWS + a) + (WIN_H - 1)
                tiles.append(table_ref[head, jnp.clip(dr + 1, 0, N_DR_PAIRS - 1)])
            bias_rows.append(jnp.concatenate(tiles, axis=1))
        bias = jnp.concatenate(bias_rows, axis=0)
        qm = jnp.where(masks[e], q2, jnp.zeros_like(q2))
        s_loc = jnp.where(row_in, jnp.dot(qm, klt, preferred_element_type=F32) + bias, -jnp.inf)
        s_ctx = jnp.dot(qm, kct, preferred_element_type=F32)
        return s_loc, s_ctx

    def attend(head, s_loc, s_ctx):
        sl = pair_slab(head // 2)
        vl = v_ref[pl.ds(start, NA_K), sl]
        vct = vc_ref[0, sl, :].astype(BF16)
        m_max = jnp.maximum(jnp.max(s_loc, axis=-1, keepdims=True),
                            jnp.max(s_ctx, axis=-1, keepdims=True))
        p_loc = jnp.exp(s_loc - m_max)
        p_ctx = jnp.exp(s_ctx - m_max)
        den = jnp.sum(p_loc, axis=-1, keepdims=True) + jnp.sum(p_ctx, axis=-1, keepdims=True)
        return (jnp.dot(p_loc.astype(BF16), vl, preferred_element_type=F32)
                + lax.dot_general(p_ctx.astype(BF16), vct, (((1,), (1,)), ((), ())),
                                  preferred_element_type=F32)) / den

    pending = scores(0)
    out = None
    for head in range(NA_HEADS):
        current = pending
        if head + 1 < NA_HEADS:
            pending = scores(head + 1)
        o = attend(head, *current)
        if adaln_layer is not None and head == 0:
            modn_ref[...] = _adaln_slab(*ada_refs, cond_scr, adaln_layer)
        if head % 2 == 0:
            out = o
        else:
            o_ref[:, pair_slab(head // 2)] = jnp.where(masks[1], o, out).astype(BF16)


def _na_attn(q, k, v, k_ctx, v_ctx, rpb_e, adaln_next=None):
    smp_blk0 = N_CTX_TOK // DEC_SEQ
    q_blk0 = N_CTX_TOK // NA_Q
    kv = pl.BlockSpec((DEC_SEQ, NA_WIDTH), lambda b, g: (smp_blk0 + b, 0))
    ctx = pl.BlockSpec((1, NA_WIDTH, PAST_LEN), lambda b, g: (b, 0, 0))
    in_specs = [
        pl.BlockSpec((NA_Q, NA_WIDTH), lambda b, g: (q_blk0 + b * NA_GROUPS + g, 0)),
        kv, kv, ctx, ctx,
        pl.BlockSpec((N_DR, NA_HEADS, N_DC), lambda b, g: (0, 0, 0)),
    ]
    operands = [q, k, v, k_ctx, v_ctx, jnp.transpose(rpb_e, (1, 0, 2))]
    out_specs = [pl.BlockSpec((NA_Q, NA_WIDTH), lambda b, g: (b * NA_GROUPS + g, 0))]
    out_shape = [jax.ShapeDtypeStruct((N_SMP_TOK, NA_WIDTH), BF16)]
    scratch = [pltpu.VMEM((NA_HEADS, N_DR_PAIRS, GRID_W, LANES), F32)]
    if adaln_next is not None:
        ada_in, ada_out, ada_shape = _adaln_specs(adaln_next, DEC_BATCH * NA_GROUPS,
                                                  lambda b, g: b * NA_GROUPS + g)
        in_specs += ada_in
        operands += list(adaln_next[:4])
        out_specs.append(ada_out)
        out_shape.append(ada_shape)
        scratch.append(pltpu.VMEM((MOD_ROWS, D_MODEL), F32))
    return pl.pallas_call(
        functools.partial(_na_kernel, adaln_layer=None if adaln_next is None else adaln_next[4]),
        grid=(DEC_BATCH, NA_GROUPS),
        in_specs=in_specs,
        out_specs=out_specs,
        out_shape=out_shape,
        scratch_shapes=scratch,
        compiler_params=_cparams(2),
        name="neighbourhood_attention",
    )(*operands)


def _gelu_tanh(x):
    c0 = float(np.sqrt(2.0 / np.pi))
    inner = x * (c0 + (c0 * 0.044715) * (x * x))
    return (0.5 * x) * (1.0 + jnp.tanh(inner))


def _lru_build_gate_weights(wr_ref, wi_ref, w_scr):
    blocks_per_group = LRU_SUB // LRU_BLOCK
    w_scr[...] = jnp.zeros_like(w_scr)
    for d in range(2):
        for kind, w_ref in enumerate((wr_ref, wi_ref)):
            col0 = (2 * d + kind) * LRU_SUB
            for blk in range(LRU_BLOCKS):
                c, n = divmod(blk, blocks_per_group)
                r0 = n * LRU_BLOCK
                w_scr[c, r0:r0 + LRU_BLOCK, col0 + r0:col0 + r0 + LRU_BLOCK] = (
                    0.5 * w_ref[d * LRU_BLOCKS + blk]).astype(BF16)


def _lru_kernel(*refs, zero_state, t_len):
    refs = list(refs)
    xb_ref, gb_ref, cw_ref, cb_ref, wr_ref, wi_ref, br_ref, bi_ref, lam_ref = refs[:9]
    h0f_ref, h0b_ref = (None, None) if zero_state else refs[9:11]
    y_ref, hlf_ref, hlb_ref, af_ref, uf_ref, ab_ref, ub_ref, w_scr = refs[-8:]
    width = xb_ref.shape[1]
    n_seq = xb_ref.shape[0] // t_len
    n_blk = t_len // SUBLANES
    row = lax.broadcasted_iota(jnp.int32, (t_len, 1), 0)
    in_block = lax.broadcasted_iota(jnp.int32, (1, SUBLANES, 1), 1)

    @pl.when(pl.program_id(0) == 0)
    def _():
        _lru_build_gate_weights(wr_ref, wi_ref, w_scr)

    def shifted(z, s):
        rolled = pltpu.roll(z, (-s) % t_len, axis=0)
        ok = (row + s >= 0) & (row + s < t_len)
        return jnp.where(ok, rolled, 0.0)

    left = (CONV_W - 1) // 2
    for q, c in [(q, c) for q in range(n_seq) for c in range(width // LRU_SUB)]:
        rows = slice(q * t_len, (q + 1) * t_len)
        cs = slice(c * LRU_SUB, (c + 1) * LRU_SUB)
        x = xb_ref[rows, cs]
        xc = cb_ref[:, cs]
        for j in range(CONV_W):
            tap = x if j == left else shifted(x, j - left)
            xc = xc + tap * cw_ref[j:j + 1, cs]
        half_gates = jnp.dot(xc.astype(BF16), w_scr[c], preferred_element_type=F32)
        half_xc = 0.5 * xc
        for d, (a_ref, u_ref) in enumerate(((af_ref, uf_ref), (ab_ref, ub_ref))):
            t_r = jnp.tanh(half_gates[:, (2 * d) * LRU_SUB:(2 * d + 1) * LRU_SUB] + 0.5 * br_ref[d:d + 1, cs])
            t_i = jnp.tanh(half_gates[:, (2 * d + 1) * LRU_SUB:(2 * d + 2) * LRU_SUB] + 0.5 * bi_ref[d:d + 1, cs])
            lam = lam_ref[d:d + 1, cs]
            log_sig = jnp.minimum(lam, 0.0) - jnp.log1p(jnp.exp(-jnp.abs(lam)))
            half_c_log_sig = (0.5 * LRU_C) * log_sig
            log_a = t_r * half_c_log_sig + half_c_log_sig
            a = jnp.exp(log_a)
            var = -jnp.tanh(log_a) * (a * a + 1.0)
            u = jnp.where(var > 0.0, var * lax.rsqrt(var), 0.0) * ((t_i + 1.0) * half_xc)
            a = a.reshape(n_blk, SUBLANES, LRU_SUB)
            u = u.reshape(n_blk, SUBLANES, LRU_SUB)
            step = 1
            while step < SUBLANES:
                if d == 0:
                    ok, shift = in_block >= step, step
                else:
                    ok, shift = in_block < SUBLANES - step, SUBLANES - step
                a_prev = jnp.where(ok, pltpu.roll(a, shift, axis=1), 1.0)
                u_prev = jnp.where(ok, pltpu.roll(u, shift, axis=1), 0.0)
                u = u + a * u_prev
                a = a * a_prev
                step *= 2
            a_ref[rows, cs] = a.reshape(t_len, LRU_SUB)
            u_ref[rows, cs] = u.reshape(t_len, LRU_SUB)

    def body(i, carry):
        new = []
        for q, (cf, cb) in enumerate(carry):
            f0 = pl.multiple_of(q * t_len + i * SUBLANES, SUBLANES)
            b0 = pl.multiple_of(q * t_len + (n_blk - 1 - i) * SUBLANES, SUBLANES)
            hf = uf_ref[pl.ds(f0, SUBLANES), :] + af_ref[pl.ds(f0, SUBLANES), :] * cf
            hb = ub_ref[pl.ds(b0, SUBLANES), :] + ab_ref[pl.ds(b0, SUBLANES), :] * cb
            uf_ref[pl.ds(f0, SUBLANES), :] = hf
            ub_ref[pl.ds(b0, SUBLANES), :] = hb
            new.append((jnp.broadcast_to(hf[SUBLANES - 1:SUBLANES, :], (SUBLANES, width)),
                        jnp.broadcast_to(hb[0:1, :], (SUBLANES, width))))
        return tuple(new)

    if zero_state:
        init = tuple((jnp.zeros((SUBLANES, width), F32),) * 2 for _ in range(n_seq))
    else:
        init = tuple((jnp.broadcast_to(h0f_ref[q], (SUBLANES, width)),
                      jnp.broadcast_to(h0b_ref[q], (SUBLANES, width))) for q in range(n_seq))
    last = lax.fori_loop(0, n_blk, body, init)
    for q, (cf, cb) in enumerate(last):
        hlf_ref[q] = cf[0:1, :]
        hlb_ref[q] = cb[0:1, :]
    y_ref[...] = ((uf_ref[...] + ub_ref[...]) * _gelu_tanh(gb_ref[...])).astype(BF16)


def _lru(xb, gb, conv_w, conv_b, w_r, b_r, w_i, b_i, lam, h0, n_seq, t_len, tok_blk0, seq_per_step):
    rows = seq_per_step * t_len
    const2 = lambda s: (0, 0)
    const3 = lambda s: (0, 0, 0)
    blocks = pl.BlockSpec((2 * LRU_BLOCKS, LRU_BLOCK, LRU_BLOCK), const3)
    per_dir = pl.BlockSpec((2, LRU_WIDTH), const2)
    state = pl.BlockSpec((seq_per_step, 1, LRU_WIDTH), lambda s: (s, 0, 0))
    state_shape = jax.ShapeDtypeStruct((n_seq, 1, LRU_WIDTH), F32)
    return pl.pallas_call(
        functools.partial(_lru_kernel, zero_state=h0 is None, t_len=t_len),
        grid=(n_seq // seq_per_step,),
        in_specs=[
            pl.BlockSpec((rows, LRU_WIDTH), lambda s: (tok_blk0 + s, 0)),
            pl.BlockSpec((rows, LRU_WIDTH), lambda s: (tok_blk0 + s, 0)),
            pl.BlockSpec((CONV_W, LRU_WIDTH), const2),
            pl.BlockSpec((1, LRU_WIDTH), const2),
            blocks, blocks, per_dir, per_dir, per_dir,
        ] + ([] if h0 is None else [state, state]),
        out_specs=[pl.BlockSpec((rows, LRU_WIDTH), lambda s: (s, 0)), state, state],
        out_shape=[jax.ShapeDtypeStruct((n_seq * t_len, LRU_WIDTH), BF16), state_shape, state_shape],
        scratch_shapes=[pltpu.VMEM((rows, LRU_WIDTH), F32)] * 4
        + [pltpu.VMEM((LRU_WIDTH // LRU_SUB, LRU_SUB, 4 * LRU_SUB), BF16)],
        compiler_params=_cparams(1),
        name="rglru",
    )(xb, gb, conv_w, conv_b.reshape(1, LRU_WIDTH),
      w_r.reshape(2 * LRU_BLOCKS, LRU_BLOCK, LRU_BLOCK), w_i.reshape(2 * LRU_BLOCKS, LRU_BLOCK, LRU_BLOCK),
      b_r, b_i, lam, *(() if h0 is None else h0))


def _fourier_kernel(x_ref, mod_ref, g_ref, cs_ref, ct_ref, w_ref, o_ref, w_bf_ref, *, mod_row0, t_len):
    @pl.when(pl.program_id(0) == 0)
    def _():
        w_bf_ref[...] = w_ref[...].astype(BF16)

    x = x_ref[...]
    row = mod_row0 + pl.program_id(0) if mod_row0 else 0
    h = _norm_mod(x, _norm_gain(g_ref, 1, 1), _mod_vec(mod_ref, row, 3), _mod_vec(mod_ref, row, 4)).astype(BF16)
    cos_parts, sin_parts = [], []
    for g in range(FOURIER_GROUPS):
        ab = jnp.dot(h[:, g * GROUP_W:(g + 1) * GROUP_W], cs_ref[...], preferred_element_type=F32)
        cos_parts.append(ab[:, :GROUP_W])
        sin_parts.append(ab[:, GROUP_W:])
    cos_all = jnp.concatenate(cos_parts, axis=1).astype(BF16)
    sin_all = jnp.concatenate(sin_parts, axis=1).astype(BF16)
    f_parts = []
    for q in range(x.shape[0] // t_len):
        rows = slice(q * t_len, (q + 1) * t_len)
        stacked = jnp.concatenate([cos_all[rows], sin_all[rows]], axis=0)
        f_parts.append(jnp.dot(ct_ref[...], stacked, preferred_element_type=F32))
    f = jnp.concatenate(f_parts, axis=0) * ((t_len * GROUP_W) ** -0.5)
    y = jnp.dot(f.astype(BF16), w_bf_ref[...], preferred_element_type=F32)
    o_ref[...] = x + _mod_vec(mod_ref, row, 5) * y


def _dft_tables(t_len):
    def cos_sin(n):
        jk = np.outer(np.arange(n), np.arange(n)) % n
        ang = 2.0 * np.pi * jk.astype(np.float64) / n
        return np.cos(ang), np.sin(ang)

    cc, sc = cos_sin(GROUP_W)
    ct, st = cos_sin(t_len)
    chan = jnp.asarray(np.concatenate([cc, sc], axis=1).astype(np.float32)).astype(BF16)
    time = jnp.asarray(np.concatenate([ct, -st], axis=1).astype(np.float32)).astype(BF16)
    return chan, time


def _fourier(x, mod, norm_g, w_out, n_seq, t_len, tok_blk0, mod_row0, seq_per_step):
    assert seq_per_step == 1 or mod_row0 == 0
    chan, time = _dft_tables(t_len)
    rows = seq_per_step * t_len
    seq = lambda s: (tok_blk0 + s, 0)
    const = lambda s: (0, 0)
    return pl.pallas_call(
        functools.partial(_fourier_kernel, mod_row0=mod_row0, t_len=t_len),
        grid=(n_seq // seq_per_step,),
        in_specs=[
            pl.BlockSpec((rows, D_MODEL), seq),
            _mod_spec(mod),
            _norm_g_spec(),
            _resident((GROUP_W, 2 * GROUP_W), const),
            _resident((t_len, 2 * t_len), const),
            _resident((D_MODEL, D_MODEL), const),
        ],
        out_specs=pl.BlockSpec((rows, D_MODEL), seq),
        out_shape=jax.ShapeDtypeStruct((N_TOK, D_MODEL), F32),
        input_output_aliases={0: 0},
        scratch_shapes=[pltpu.VMEM((D_MODEL, D_MODEL), BF16)],
        compiler_params=_cparams(1),
        name="fourier_mixer",
    )(x, mod, norm_g, chan, time, w_out)


def _cache_layout(t):
    return jnp.transpose(t.reshape(BATCH, 1, NA_HEADS, HEAD_DIM, SEQ), (0, 1, 4, 2, 3))


def kernel(x_prompt, x_sample, cache_k, cache_v, state_lru_fwd, state_lru_bwd, c, c_ctx, w_ada, b_ada, norm_g, ffn1_gate, ffn1_up, ffn1_down, ffn2_gate, ffn2_up, ffn2_down, w_in, q_norm_g, k_norm_g, rpb, conv_w, conv_b, lru_w_r, lru_b_r, lru_w_i, lru_b_i, lru_lambda, w_out_ab, w_out_c):
    assert DEPTH == 2, "one neighbourhood/RG-LRU layer followed by one Fourier layer"
    c_ctx2 = c_ctx.reshape(1, D_MODEL)
    ada = (c_ctx2, c, w_ada, b_ada)
    mod0_ffn1 = _adaln(ada + (0, 0, 3 * D_MODEL), 3)

    gains = jnp.transpose(norm_g, (1, 0, 2))
    ffn1 = (ffn1_gate, ffn1_up, ffn1_down)
    ffn2 = (ffn2_gate, ffn2_up, ffn2_down)

    x, mod0_rest = _ffn((x_prompt.reshape(N_CTX_TOK, D_MODEL), x_sample.reshape(N_SMP_TOK, D_MODEL)),
                        mod0_ffn1, gains, 0, *ffn1, 0, 0,
                        adaln_next=ada + (0, 3 * D_MODEL, (N_MOD - 3) * D_MODEL))
    q, k, v, xb, gb, new_k, new_v = _proj(x, mod0_rest, 0, gains, w_in[0], q_norm_g[0], k_norm_g[0])
    o_ctx = _ctx_attn(q, k, v)
    o_smp, mod1 = _na_attn(q, k, v,
                           jnp.transpose(cache_k[:, 0], (0, 2, 3, 1)).reshape(DEC_BATCH, NA_WIDTH, PAST_LEN),
                           jnp.transpose(cache_v[:, 0], (0, 2, 3, 1)).reshape(DEC_BATCH, NA_WIDTH, PAST_LEN),
                           rpb[0], adaln_next=ada + (1, 0, MOD_WIDTH))
    lru_prm = (conv_w[0], conv_b[0], lru_w_r[0], lru_b_r[0], lru_w_i[0], lru_b_i[0], lru_lambda[0])
    yb_ctx, new_hf, new_hb = _lru(xb, gb, *lru_prm, None, BATCH, SEQ, 0, CTX_SEQ_PER_STEP)
    yb_smp, _, _ = _lru(xb, gb, *lru_prm, (state_lru_fwd, state_lru_bwd),
                        DEC_BATCH, DEC_SEQ, N_CTX_TOK // DEC_SEQ, 1)
    (x,) = _ffn((x,), mod0_rest, gains, 2, *ffn2, 0, 3, mixer_out=(o_ctx, o_smp, yb_ctx, yb_smp, w_out_ab))

    (x,) = _ffn((x,), mod1, gains, 0, *ffn1, 1, 0)
    x = _fourier(x, mod1, gains, w_out_c[0], BATCH, SEQ, 0, 0, CTX_SEQ_PER_STEP)
    x = _fourier(x, mod1, gains, w_out_c[0], DEC_BATCH, DEC_SEQ, N_CTX_TOK // DEC_SEQ, 1, 1)
    y_prompt, y_sample = _ffn((x,), mod1, gains, 2, *ffn2, 1, 6, split_out=True)

    return (y_prompt.reshape(BATCH, SEQ, D_MODEL), y_sample.reshape(DEC_BATCH, DEC_SEQ, D_MODEL),
            _cache_layout(new_k), _cache_layout(new_v),
            new_hf, new_hb)
```

```python
import functools

import numpy as np
import jax
import jax.numpy as jnp
from jax import lax
from jax.experimental import pallas as pl
from jax.experimental.pallas import tpu as pltpu

F32 = jnp.float32
BF16 = jnp.bfloat16

D_MODEL = 1024
BATCH = 16
SEQ = 256
DEPTH = 2
DEC_BATCH = 2
DEC_SEQ = 1024
PAST_LEN = 256
GRID_W = 64
HEAD_DIM = 64
NA_WIDTH = 512
NA_HEADS = 8
WIN_H = 8
WIN_W = 16
LRU_WIDTH = 512
LRU_BLOCKS = 8
LRU_BLOCK = 64
LRU_C = 8.0
LRU_SUB = 256
CONV_W = 4
FOURIER_GROUPS = 4
GROUP_W = D_MODEL // FOURIER_GROUPS
D_FF = 2816
N_MOD = 9
IN_WIDTH = 3 * NA_WIDTH + 2 * LRU_WIDTH
EPS = 1e-6

N_CTX_TOK = BATCH * SEQ
N_SMP_TOK = DEC_BATCH * DEC_SEQ
N_TOK = N_CTX_TOK + N_SMP_TOK
MOD_ROWS = 8
MOD_WIDTH = N_MOD * D_MODEL
ROWS = DEC_SEQ // GRID_W
KH = min(WIN_H, ROWS)

TOKEN_TILE = 512
N_CTX_TILES = N_CTX_TOK // TOKEN_TILE
CTX_SEQ_PER_STEP = 2
CTX_ATTN_SEQ_PER_STEP = 8
FFN_TILE = 512
FF_TILE = 256
FF_CHUNKS = D_FF // FF_TILE
FF_STAGE_SLOTS = 3
SUBLANES = 8
LANES = 128
VMEM_LIMIT = 56 * 1024 * 1024

NA_Q_ROWS = 4
NA_GROUPS = ROWS // NA_Q_ROWS
NA_K_ROWS = 12
NA_Q = NA_Q_ROWS * GRID_W
NA_K = NA_K_ROWS * GRID_W
N_DR = 2 * WIN_H - 1
N_DC = 2 * WIN_W - 1
N_DR_PAIRS = N_DR + 1


def _cparams(n_axes):
    return pltpu.CompilerParams(
        dimension_semantics=("arbitrary",) * n_axes, vmem_limit_bytes=VMEM_LIMIT)


def _resident(block_shape, index_map):
    return pl.BlockSpec(block_shape, index_map, pipeline_mode=pl.Buffered(1))


def _mod_spec(mod):
    return _resident(mod.shape, lambda i: (0, 0))


def _norm_g_spec():
    return pl.BlockSpec((3, DEPTH, D_MODEL), lambda *ids: (0, 0, 0))


def _norm_gain(g_ref, sub, layer):
    return g_ref[sub, layer:layer + 1, :]


def _mod_row_of_tile(i, tile=TOKEN_TILE):
    n_ctx_tiles = N_CTX_TOK // tile
    tiles_per_seq = DEC_SEQ // tile
    return jnp.where(i < n_ctx_tiles, 0, 1 + (i - n_ctx_tiles) // tiles_per_seq)


def _mod_vec(mod_ref, row, k):
    return mod_ref[pl.ds(row, 1), k * D_MODEL:(k + 1) * D_MODEL]


def _norm_mod(x, g, shift, scale):
    ms = jnp.mean(x * x, axis=-1, keepdims=True)
    return (x * lax.rsqrt(ms + EPS)) * (g * (1.0 + scale)) + shift


def _adaln_slab(cctx_ref, c_ref, w_ref, b_ref, cond_ref, layer):
    cond_ref[...] = jnp.zeros_like(cond_ref)
    cond_ref[0:1, :] = cctx_ref[...]
    cond_ref[1:1 + DEC_BATCH, :] = c_ref[...]
    cond = cond_ref[...]
    s = (cond * jax.nn.sigmoid(cond)).astype(BF16)
    return jnp.dot(s, w_ref[...].astype(BF16), preferred_element_type=F32) + b_ref[layer:layer + 1, :]


def _adaln_specs(job, n_steps, step=lambda i: i):
    layer, col0, n_cols = job[4:]
    slab = n_cols // n_steps
    assert n_cols % n_steps == 0 and slab % LANES == 0 and col0 % slab == 0
    blk0 = col0 // slab
    return ([pl.BlockSpec((1, D_MODEL), lambda *ids: (0, 0)),
             pl.BlockSpec((DEC_BATCH, D_MODEL), lambda *ids: (0, 0)),
             pl.BlockSpec((None, D_MODEL, slab), lambda *ids: (layer, 0, blk0 + step(*ids))),
             pl.BlockSpec((DEPTH, slab), lambda *ids: (0, blk0 + step(*ids)))],
            pl.BlockSpec((MOD_ROWS, slab), lambda *ids: (0, step(*ids))),
            jax.ShapeDtypeStruct((MOD_ROWS, n_cols), F32))


def _adaln_kernel(cctx_ref, c_ref, w_ref, b_ref, o_ref, cond_ref, *, layer):
    o_ref[...] = _adaln_slab(cctx_ref, c_ref, w_ref, b_ref, cond_ref, layer)


def _adaln(job, n_steps):
    in_specs, out_spec, out_shape = _adaln_specs(job, n_steps)
    return pl.pallas_call(
        functools.partial(_adaln_kernel, layer=job[4]),
        grid=(n_steps,),
        in_specs=in_specs,
        out_specs=out_spec,
        out_shape=out_shape,
        scratch_shapes=[pltpu.VMEM((MOD_ROWS, D_MODEL), F32)],
        compiler_params=_cparams(1),
        name="adaln",
    )(*job[:4])


def _ffn_weight_copy(w_hbm, stage_ref, sem_ref, layer, j, ff_axis):
    ff = pl.ds(j * FF_TILE, FF_TILE)
    src = w_hbm.at[layer, :, ff] if ff_axis == 1 else w_hbm.at[layer, ff, :]
    slot = j % FF_STAGE_SLOTS
    return pltpu.make_async_copy(src, stage_ref.at[slot], sem_ref.at[slot])


def _mixer_out_copy(w_hbm, stage_ref, sem_ref, j):
    rows = stage_ref.shape[1]
    slot = j % stage_ref.shape[0]
    return pltpu.make_async_copy(w_hbm.at[0, pl.ds(j * rows, rows), :], stage_ref.at[slot], sem_ref.at[slot])


def _ffn_kernel(*refs, layer, norm_sub, mod_base, split_in, split_out, mixer_out, adaln_next):
    refs = list(refs)
    take = lambda n: [refs.pop(0) for _ in range(n)]
    x_refs = take(2 if split_in else 1)
    mix_refs = take(4) if mixer_out else None
    mod_ref, g_ref = take(2)
    wo_hbm = take(1)[0] if mixer_out else None
    ada_refs = take(4) if adaln_next is not None else None
    wg_hbm, wu_hbm, wd_hbm = take(3)
    o_refs = take(2 if split_out else 1)
    modn_ref = take(1)[0] if adaln_next is not None else None
    wg_bf, wu_bf, wd_bf, stg_g, stg_u, stg_d, sem_g, sem_u, sem_d = take(9)
    cond_scr = take(1)[0] if adaln_next is not None else None
    streams = ((wg_hbm, stg_g, sem_g, wg_bf, 1), (wu_hbm, stg_u, sem_u, wu_bf, 1),
               (wd_hbm, stg_d, sem_d, wd_bf, 0))

    i = pl.program_id(0)
    is_ctx = i < N_CTX_TOK // FFN_TILE
    if split_in:
        x = jnp.where(is_ctx, x_refs[0][...], x_refs[1][...])
    else:
        x = x_refs[0][...]
    row = _mod_row_of_tile(i, FFN_TILE)

    if mixer_out:
        wo_bf, stg_o, sem_o = take(3)
        rows = stg_o.shape[1]
        n_chunks = D_MODEL // rows

        @pl.when(i == 0)
        def _():
            for j in range(stg_o.shape[0]):
                _mixer_out_copy(wo_hbm, stg_o, sem_o, j).start()
            for j in range(n_chunks):
                _mixer_out_copy(wo_hbm, stg_o, sem_o, j).wait()
                wo_bf[j * rows:(j + 1) * rows, :] = stg_o[j % stg_o.shape[0]].astype(BF16)
                if j + stg_o.shape[0] < n_chunks:
                    _mixer_out_copy(wo_hbm, stg_o, sem_o, j + stg_o.shape[0]).start()

        oc_ref, os_ref, yc_ref, ys_ref = mix_refs
        cat = jnp.concatenate([jnp.where(is_ctx, oc_ref[...], os_ref[...]),
                               jnp.where(is_ctx, yc_ref[...], ys_ref[...])], axis=1)
        x = x + _mod_vec(mod_ref, row, mod_base - 1) * jnp.dot(cat, wo_bf[...], preferred_element_type=F32)

    h = _norm_mod(x, _norm_gain(g_ref, norm_sub, layer), _mod_vec(mod_ref, row, mod_base),
                  _mod_vec(mod_ref, row, mod_base + 1)).astype(BF16)

    def start_chunk(j):
        for w_hbm, stg, sem, _, ff_axis in streams:
            _ffn_weight_copy(w_hbm, stg, sem, layer, j, ff_axis).start()

    def finish_chunk(j):
        for w_hbm, stg, sem, w_bf, ff_axis in streams:
            _ffn_weight_copy(w_hbm, stg, sem, layer, j, ff_axis).wait()
            w_bf[j] = stg[j % FF_STAGE_SLOTS].astype(BF16)

    def run(stream_weights):
        if stream_weights:
            for j in range(FF_STAGE_SLOTS):
                start_chunk(j)
        acc = None
        for j in range(FF_CHUNKS):
            if stream_weights:
                finish_chunk(j)
                if j + FF_STAGE_SLOTS < FF_CHUNKS:
                    start_chunk(j + FF_STAGE_SLOTS)
            a = jnp.dot(h, wg_bf[j], preferred_element_type=F32)
            b = jnp.dot(h, wu_bf[j], preferred_element_type=F32)
            if adaln_next is not None and j == 1:
                modn_ref[...] = _adaln_slab(*ada_refs, cond_scr, adaln_next)
            act = (a * jax.nn.sigmoid(a) * b).astype(BF16)
            y = jnp.dot(act, wd_bf[j], preferred_element_type=F32)
            acc = y if acc is None else acc + y
        res = x + 0.5 * _mod_vec(mod_ref, row, mod_base + 2) * acc
        if split_out:
            @pl.when(is_ctx)
            def _():
                o_refs[0][...] = res

            @pl.when(jnp.logical_not(is_ctx))
            def _():
                o_refs[1][...] = res
        else:
            o_refs[0][...] = res

    @pl.when(i == 0)
    def _():
        run(True)

    @pl.when(i > 0)
    def _():
        run(False)


def _ffn(xs, mod, norm_g, norm_sub, wg, wu, wd, layer, mod_base, split_out=False, mixer_out=None,
         adaln_next=None):
    tm = FFN_TILE
    n_ctx_tiles = N_CTX_TOK // tm
    split_in = len(xs) == 2

    def tiles(width):
        return (pl.BlockSpec((tm, width), lambda i: (i, 0)),
                pl.BlockSpec((tm, width), lambda i: (jnp.minimum(i, n_ctx_tiles - 1), 0)),
                pl.BlockSpec((tm, width), lambda i: (jnp.maximum(i - n_ctx_tiles, 0), 0)))

    tok, ctx_tok, smp_tok = tiles(D_MODEL)
    full = jax.ShapeDtypeStruct((N_TOK, D_MODEL), F32)
    pair = [jax.ShapeDtypeStruct((N_CTX_TOK, D_MODEL), F32), jax.ShapeDtypeStruct((N_SMP_TOK, D_MODEL), F32)]
    hbm = pl.BlockSpec(memory_space=pl.ANY)
    in_specs = [ctx_tok, smp_tok] if split_in else [tok]
    operands = list(xs)
    scratch = [
        pltpu.VMEM((FF_CHUNKS, D_MODEL, FF_TILE), BF16),
        pltpu.VMEM((FF_CHUNKS, D_MODEL, FF_TILE), BF16),
        pltpu.VMEM((FF_CHUNKS, FF_TILE, D_MODEL), BF16),
        pltpu.VMEM((FF_STAGE_SLOTS, D_MODEL, FF_TILE), F32),
        pltpu.VMEM((FF_STAGE_SLOTS, D_MODEL, FF_TILE), F32),
        pltpu.VMEM((FF_STAGE_SLOTS, FF_TILE, D_MODEL), F32),
        pltpu.SemaphoreType.DMA((FF_STAGE_SLOTS,)),
        pltpu.SemaphoreType.DMA((FF_STAGE_SLOTS,)),
        pltpu.SemaphoreType.DMA((FF_STAGE_SLOTS,)),
    ]
    if mixer_out is not None:
        _, ctx_half, smp_half = tiles(NA_WIDTH)
        in_specs += [ctx_half, smp_half, ctx_half, smp_half]
        operands += list(mixer_out[:4])
    in_specs += [_mod_spec(mod), _norm_g_spec()]
    operands += [mod, norm_g]
    out_specs = [ctx_tok, smp_tok] if split_out else [tok]
    out_shape = pair if split_out else [full]
    if adaln_next is not None:
        ada_in, ada_out, ada_shape = _adaln_specs(adaln_next, N_TOK // tm)
        in_specs += ada_in
        operands += list(adaln_next[:4])
        out_specs.append(ada_out)
        out_shape.append(ada_shape)
        scratch.append(pltpu.VMEM((MOD_ROWS, D_MODEL), F32))
    if mixer_out is not None:
        in_specs.insert(len(in_specs) - (4 if adaln_next is not None else 0), hbm)
        operands.insert(len(operands) - (4 if adaln_next is not None else 0), mixer_out[4])
        scratch += [
            pltpu.VMEM((D_MODEL, D_MODEL), BF16),
            pltpu.VMEM((FF_STAGE_SLOTS, FF_TILE, D_MODEL), F32),
            pltpu.SemaphoreType.DMA((FF_STAGE_SLOTS,)),
        ]
    return pl.pallas_call(
        functools.partial(_ffn_kernel, layer=layer, norm_sub=norm_sub, mod_base=mod_base, split_in=split_in,
                          split_out=split_out, mixer_out=mixer_out is not None,
                          adaln_next=None if adaln_next is None else adaln_next[4]),
        grid=(N_TOK // tm,),
        in_specs=in_specs + [hbm, hbm, hbm],
        out_specs=out_specs,
        out_shape=out_shape,
        scratch_shapes=scratch,
        compiler_params=_cparams(1),
        name="ffn",
    )(*operands, wg, wu, wd)


def _head_rms_norm(z, g, ones_bd):
    z2 = z * z
    hi = z2.astype(BF16)
    lo = (z2 - hi.astype(F32)).astype(BF16)
    n = ones_bd.shape[0]
    parts = []
    for c in range(z.shape[1] // n):
        sl = slice(c * n, (c + 1) * n)
        parts.append(jnp.dot(hi[:, sl], ones_bd, preferred_element_type=F32)
                     + jnp.dot(lo[:, sl], ones_bd, preferred_element_type=F32))
    ss = jnp.concatenate(parts, axis=1)
    return z * lax.rsqrt(ss * (1.0 / HEAD_DIM) + EPS) * g


def _proj_kernel(*refs, mod_k0, adaln_layer):
    refs = list(refs)
    x_ref, mod_ref, g_ref, w_ref, qg_ref, kg_ref, ones_ref = refs[:7]
    ada_refs = refs[7:11] if adaln_layer is not None else None
    n_in = 7 if adaln_layer is None else 11
    q_ref, k_ref, v_ref, xb_ref, gb_ref, kout_ref, vout_ref = refs[n_in:n_in + 7]
    rest = refs[n_in + 7:]
    modn_ref = rest.pop(0) if adaln_layer is not None else None
    w_bf_ref = rest.pop(0)
    i = pl.program_id(0)

    @pl.when(i == 0)
    def _():
        w_bf_ref[...] = w_ref[...].astype(BF16)

    x = x_ref[...]
    row = _mod_row_of_tile(i)
    h = _norm_mod(x, _norm_gain(g_ref, 1, 0), _mod_vec(mod_ref, row, mod_k0),
                  _mod_vec(mod_ref, row, mod_k0 + 1)).astype(BF16)

    def proj(part):
        return jnp.dot(h, w_bf_ref[:, part * NA_WIDTH:(part + 1) * NA_WIDTH], preferred_element_type=F32)

    ones_bd = ones_ref[...]
    q_raw = proj(0)
    k_raw = proj(1)
    q = _head_rms_norm(q_raw, jnp.tile(qg_ref[...], (1, NA_HEADS)), ones_bd) * (HEAD_DIM ** -0.5)
    q_ref[...] = q.astype(BF16)
    v = proj(2)
    if adaln_layer is not None:
        modn_ref[...] = _adaln_slab(*ada_refs, rest.pop(0), adaln_layer)
    k = _head_rms_norm(k_raw, jnp.tile(kg_ref[...], (1, NA_HEADS)), ones_bd)
    k_ref[...] = k.astype(BF16)
    xb = proj(3)
    v_ref[...] = v.astype(BF16)
    gb = proj(4)
    xb_ref[...] = xb
    gb_ref[...] = gb

    kt = [k[b * SEQ:(b + 1) * SEQ, :].T for b in range(TOKEN_TILE // SEQ)]
    vt = [v[b * SEQ:(b + 1) * SEQ, :].T for b in range(TOKEN_TILE // SEQ)]

    @pl.when(i < N_CTX_TILES)
    def _():
        for b in range(TOKEN_TILE // SEQ):
            kout_ref[b] = kt[b]
            vout_ref[b] = vt[b]


def _proj(x, mod, mod_k0, norm_g, w_in, q_g, k_g, adaln_next=None):
    tm = TOKEN_TILE
    head = np.arange(2 * LANES) // HEAD_DIM
    ones_bd = jnp.asarray((head[:, None] == head[None, :]).astype(np.float32), dtype=BF16)
    tok = lambda i: (i, 0)
    const = lambda i: (0, 0)
    act_f32 = jax.ShapeDtypeStruct((N_TOK, NA_WIDTH), F32)
    act_bf16 = jax.ShapeDtypeStruct((N_TOK, NA_WIDTH), BF16)
    cache = jax.ShapeDtypeStruct((BATCH, NA_WIDTH, SEQ), F32)
    cache_spec = pl.BlockSpec((tm // SEQ, NA_WIDTH, SEQ), lambda i: (jnp.minimum(i, N_CTX_TILES - 1), 0, 0))
    in_specs = [
        pl.BlockSpec((tm, D_MODEL), tok),
        _mod_spec(mod),
        _norm_g_spec(),
        _resident((D_MODEL, IN_WIDTH), const),
        pl.BlockSpec((1, HEAD_DIM), const),
        pl.BlockSpec((1, HEAD_DIM), const),
        _resident((2 * LANES, 2 * LANES), const),
    ]
    operands = [x, mod, norm_g, w_in, q_g.reshape(1, HEAD_DIM), k_g.reshape(1, HEAD_DIM), ones_bd]
    out_specs = [pl.BlockSpec((tm, NA_WIDTH), tok)] * 5 + [cache_spec, cache_spec]
    out_shape = [act_bf16, act_bf16, act_bf16, act_f32, act_f32, cache, cache]
    scratch = [pltpu.VMEM((D_MODEL, IN_WIDTH), BF16)]
    if adaln_next is not None:
        ada_in, ada_out, ada_shape = _adaln_specs(adaln_next, N_TOK // tm)
        in_specs += ada_in
        operands += list(adaln_next[:4])
        out_specs.append(ada_out)
        out_shape.append(ada_shape)
        scratch.append(pltpu.VMEM((MOD_ROWS, D_MODEL), F32))
    return pl.pallas_call(
        functools.partial(_proj_kernel, mod_k0=mod_k0,
                          adaln_layer=None if adaln_next is None else adaln_next[4]),
        grid=(N_TOK // tm,),
        in_specs=in_specs,
        out_specs=out_specs,
        out_shape=out_shape,
        scratch_shapes=scratch,
        compiler_params=_cparams(1),
        name="mixer_in_proj",
    )(*operands)


def _head_masks():
    lane = lax.broadcasted_iota(jnp.int32, (1, 2 * HEAD_DIM), 1)
    return [lane < HEAD_DIM, lane >= HEAD_DIM]


def _ctx_attn_kernel(q_ref, k_ref, v_ref, o_ref):
    masks = _head_masks()
    units = [(b, h) for b in range(q_ref.shape[0] // SEQ) for h in range(NA_HEADS)]

    def where(b, h):
        return slice(b * SEQ, (b + 1) * SEQ), slice(2 * HEAD_DIM * (h // 2), 2 * HEAD_DIM * (h // 2 + 1))

    def scores(b, h):
        rows, sl = where(b, h)
        q2 = q_ref[rows, sl]
        qm = jnp.where(masks[h % 2], q2, jnp.zeros_like(q2))
        return jnp.dot(qm, k_ref[rows, sl].T, preferred_element_type=F32)

    def attend(b, h, s):
        rows, sl = where(b, h)
        pe = jnp.exp(s - jnp.max(s, axis=-1, keepdims=True))
        den = jnp.sum(pe, axis=-1, keepdims=True)
        return jnp.dot(pe.astype(BF16), v_ref[rows, sl], preferred_element_type=F32) / den

    pending = scores(*units[0])
    out = None
    for n, (b, h) in enumerate(units):
        current = pending
        if n + 1 < len(units):
            pending = scores(*units[n + 1])
        o = attend(b, h, current)
        if h % 2 == 0:
            out = o
        else:
            rows, sl = where(b, h)
            o_ref[rows, sl] = jnp.where(masks[1], o, out).astype(BF16)


def _ctx_attn(q, k, v):
    blk = pl.BlockSpec((CTX_ATTN_SEQ_PER_STEP * SEQ, NA_WIDTH), lambda b: (b, 0))
    return pl.pallas_call(
        _ctx_attn_kernel,
        grid=(BATCH // CTX_ATTN_SEQ_PER_STEP,),
        in_specs=[blk, blk, blk],
        out_specs=blk,
        out_shape=jax.ShapeDtypeStruct((N_CTX_TOK, NA_WIDTH), BF16),
        compiler_params=_cparams(1),
        name="ctx_attention",
    )(q, k, v)


def _na_build_bias_table(rpb_ref, table_ref):
    qc = lax.broadcasted_iota(jnp.int32, (GRID_W, LANES), 0)
    lane = lax.broadcasted_iota(jnp.int32, (GRID_W, LANES), 1)
    kc = lane % GRID_W
    col_start = jnp.clip(qc - WIN_W // 2, 0, GRID_W - WIN_W)
    col_in = (kc >= col_start) & (kc < col_start + WIN_W)
    neg = jnp.full((GRID_W, LANES), -jnp.inf, F32)

    def toeplitz(h, dr, lane0):
        if dr < 0 or dr >= N_DR:
            return neg
        row = jnp.pad(rpb_ref[dr, h:h + 1, :], ((0, 0), (0, LANES - N_DC)))
        w = jnp.broadcast_to(row, (GRID_W, LANES))
        return pltpu.roll(w, (lane0 - (WIN_W - 1)) % LANES, 1, stride=1, stride_axis=0)

    for h in range(NA_HEADS):
        for i in range(N_DR_PAIRS):
            t = jnp.where(lane < GRID_W, toeplitz(h, i - 1, 0), toeplitz(h, i, GRID_W))
            table_ref[h, i] = jnp.where(col_in, t, neg)


def _na_kernel(*refs, adaln_layer):
    q_ref, k_ref, v_ref, kc_ref, vc_ref, rpb_ref = refs[:6]
    if adaln_layer is None:
        o_ref, table_ref = refs[6:]
    else:
        ada_refs = refs[6:10]
        o_ref, modn_ref, table_ref, cond_scr = refs[10:]
    b = pl.program_id(0)
    g = pl.program_id(1)

    @pl.when((b == 0) & (g == 0))
    def _():
        _na_build_bias_table(rpb_ref, table_ref)

    win_row0 = jnp.where(g < NA_GROUPS // 2, 0, ROWS - NA_K_ROWS)
    start = pl.multiple_of(win_row0 * GRID_W, GRID_W)
    q_row = g * NA_Q_ROWS + lax.broadcasted_iota(jnp.int32, (NA_Q, 1), 0) // GRID_W
    k_row = win_row0 + lax.broadcasted_iota(jnp.int32, (1, NA_K), 1) // GRID_W
    row_start = jnp.clip(q_row - KH // 2, 0, ROWS - KH)
    row_in = (k_row >= row_start) & (k_row < row_start + KH)
    masks = _head_masks()

    def pair_slab(p):
        return slice(2 * HEAD_DIM * p, 2 * HEAD_DIM * (p + 1))

    def scores(head):
        p, e = divmod(head, 2)
        sl = pair_slab(p)
        q2 = q_ref[:, sl]
        klt = k_ref[pl.ds(start, NA_K), sl].T
        kct = kc_ref[0, sl, :].astype(BF16)
        bias_rows = []
        for a in range(NA_Q_ROWS):
            tiles = []
            for m in range(NA_K_ROWS // 2):
                dr = win_row0 + 2 * m - (g * NA_Q_ROWS + a) + (WIN_H - 1)
                tiles.append(table_ref[head, jnp.clip(dr + 1, 0, N_DR_PAIRS - 1)])
            bias_rows.append(jnp.concatenate(tiles, axis=1))
        bias = jnp.concatenate(bias_rows, axis=0)
        qm = jnp.where(masks[e], q2, jnp.zeros_like(q2))
        s_loc = jnp.where(row_in, jnp.dot(qm, klt, preferred_element_type=F32) + bias, -jnp.inf)
        s_ctx = jnp.dot(qm, kct, preferred_element_type=F32)
        return s_loc, s_ctx

    def attend(head, s_loc, s_ctx):
        sl = pair_slab(head // 2)
        vl = v_ref[pl.ds(start, NA_K), sl]
        vct = vc_ref[0, sl, :].astype(BF16)
        m_max = jnp.maximum(jnp.max(s_loc, axis=-1, keepdims=True),
                            jnp.max(s_ctx, axis=-1, keepdims=True))
        p_loc = jnp.exp(s_loc - m_max)
        p_ctx = jnp.exp(s_ctx - m_max)
        den = jnp.sum(p_loc, axis=-1, keepdims=True) + jnp.sum(p_ctx, axis=-1, keepdims=True)
        return (jnp.dot(p_loc.astype(BF16), vl, preferred_element_type=F32)
                + lax.dot_general(p_ctx.astype(BF16), vct, (((1,), (1,)), ((), ())),
                                  preferred_element_type=F32)) / den

    pending = scores(0)
    out = None
    for head in range(NA_HEADS):
        current = pending
        if head + 1 < NA_HEADS:
            pending = scores(head + 1)
        o = attend(head, *current)
        if adaln_layer is not None and head == 0:
            modn_ref[...] = _adaln_slab(*ada_refs, cond_scr, adaln_layer)
        if head % 2 == 0:
            out = o
        else:
            o_ref[:, pair_slab(head // 2)] = jnp.where(masks[1], o, out).astype(BF16)


def _na_attn(q, k, v, k_ctx, v_ctx, rpb_e, adaln_next=None):
    smp_blk0 = N_CTX_TOK // DEC_SEQ
    q_blk0 = N_CTX_TOK // NA_Q
    kv = pl.BlockSpec((DEC_SEQ, NA_WIDTH), lambda b, g: (smp_blk0 + b, 0))
    ctx = pl.BlockSpec((1, NA_WIDTH, PAST_LEN), lambda b, g: (b, 0, 0))
    in_specs = [
        pl.BlockSpec((NA_Q, NA_WIDTH), lambda b, g: (q_blk0 + b * NA_GROUPS + g, 0)),
        kv, kv, ctx, ctx,
        pl.BlockSpec((N_DR, NA_HEADS, N_DC), lambda b, g: (0, 0, 0)),
    ]
    operands = [q, k, v, k_ctx, v_ctx, jnp.transpose(rpb_e, (1, 0, 2))]
    out_specs = [pl.BlockSpec((NA_Q, NA_WIDTH), lambda b, g: (b * NA_GROUPS + g, 0))]
    out_shape = [jax.ShapeDtypeStruct((N_SMP_TOK, NA_WIDTH), BF16)]
    scratch = [pltpu.VMEM((NA_HEADS, N_DR_PAIRS, GRID_W, LANES), F32)]
    if adaln_next is not None:
        ada_in, ada_out, ada_shape = _adaln_specs(adaln_next, DEC_BATCH * NA_GROUPS,
                                                  lambda b, g: b * NA_GROUPS + g)
        in_specs += ada_in
        operands += list(adaln_next[:4])
        out_specs.append(ada_out)
        out_shape.append(ada_shape)
        scratch.append(pltpu.VMEM((MOD_ROWS, D_MODEL), F32))
    return pl.pallas_call(
        functools.partial(_na_kernel, adaln_layer=None if adaln_next is None else adaln_next[4]),
        grid=(DEC_BATCH, NA_GROUPS),
        in_specs=in_specs,
        out_specs=out_specs,
        out_shape=out_shape,
        scratch_shapes=scratch,
        compiler_params=_cparams(2),
        name="neighbourhood_attention",
    )(*operands)


def _gelu_tanh(x):
    c0 = float(np.sqrt(2.0 / np.pi))
    inner = x * (c0 + (c0 * 0.044715) * (x * x))
    return (0.5 * x) * (1.0 + jnp.tanh(inner))


def _lru_build_gate_weights(wr_ref, wi_ref, w_scr):
    blocks_per_group = LRU_SUB // LRU_BLOCK
    w_scr[...] = jnp.zeros_like(w_scr)
    for d in range(2):
        for kind, w_ref in enumerate((wr_ref, wi_ref)):
            col0 = (2 * d + kind) * LRU_SUB
            for blk in range(LRU_BLOCKS):
                c, n = divmod(blk, blocks_per_group)
                r0 = n * LRU_BLOCK
                w_scr[c, r0:r0 + LRU_BLOCK, col0 + r0:col0 + r0 + LRU_BLOCK] = (
                    0.5 * w_ref[d * LRU_BLOCKS + blk]).astype(BF16)


def _lru_kernel(*refs, zero_state, t_len):
    refs = list(refs)
    xb_ref, gb_ref, cw_ref, cb_ref, wr_ref, wi_ref, br_ref, bi_ref, lam_ref = refs[:9]
    h0f_ref, h0b_ref = (None, None) if zero_state else refs[9:11]
    y_ref, hlf_ref, hlb_ref, af_ref, uf_ref, ab_ref, ub_ref, w_scr = refs[-8:]
    width = xb_ref.shape[1]
    n_seq = xb_ref.shape[0] // t_len
    n_blk = t_len // SUBLANES
    row = lax.broadcasted_iota(jnp.int32, (t_len, 1), 0)
    in_block = lax.broadcasted_iota(jnp.int32, (1, SUBLANES, 1), 1)

    @pl.when(pl.program_id(0) == 0)
    def _():
        _lru_build_gate_weights(wr_ref, wi_ref, w_scr)

    def shifted(z, s):
        rolled = pltpu.roll(z, (-s) % t_len, axis=0)
        ok = (row + s >= 0) & (row + s < t_len)
        return jnp.where(ok, rolled, 0.0)

    left = (CONV_W - 1) // 2
    for q, c in [(q, c) for q in range(n_seq) for c in range(width // LRU_SUB)]:
        rows = slice(q * t_len, (q + 1) * t_len)
        cs = slice(c * LRU_SUB, (c + 1) * LRU_SUB)
        x = xb_ref[rows, cs]
        xc = cb_ref[:, cs]
        for j in range(CONV_W):
            tap = x if j == left else shifted(x, j - left)
            xc = xc + tap * cw_ref[j:j + 1, cs]
        half_gates = jnp.dot(xc.astype(BF16), w_scr[c], preferred_element_type=F32)
        half_xc = 0.5 * xc
        for d, (a_ref, u_ref) in enumerate(((af_ref, uf_ref), (ab_ref, ub_ref))):
            t_r = jnp.tanh(half_gates[:, (2 * d) * LRU_SUB:(2 * d + 1) * LRU_SUB] + 0.5 * br_ref[d:d + 1, cs])
            t_i = jnp.tanh(half_gates[:, (2 * d + 1) * LRU_SUB:(2 * d + 2) * LRU_SUB] + 0.5 * bi_ref[d:d + 1, cs])
            lam = lam_ref[d:d + 1, cs]
            log_sig = jnp.minimum(lam, 0.0) - jnp.log1p(jnp.exp(-jnp.abs(lam)))
            half_c_log_sig = (0.5 * LRU_C) * log_sig
            log_a = t_r * half_c_log_sig + half_c_log_sig
            a = jnp.exp(log_a)
            var = -jnp.tanh(log_a) * (a * a + 1.0)
            u = jnp.where(var > 0.0, var * lax.rsqrt(var), 0.0) * ((t_i + 1.0) * half_xc)
            a = a.reshape(n_blk, SUBLANES, LRU_SUB)
            u = u.reshape(n_blk, SUBLANES, LRU_SUB)
            step = 1
            while step < SUBLANES:
                if d == 0:
                    ok, shift = in_block >= step, step
                else:
                    ok, shift = in_block < SUBLANES - step, SUBLANES - step
                a_prev = jnp.where(ok, pltpu.roll(a, shift, axis=1), 1.0)
                u_prev = jnp.where(ok, pltpu.roll(u, shift, axis=1), 0.0)
                u = u + a * u_prev
                a = a * a_prev
                step *= 2
            a_ref[rows, cs] = a.reshape(t_len, LRU_SUB)
            u_ref[rows, cs] = u.reshape(t_len, LRU_SUB)

    def body(i, carry):
        new = []
        for q, (cf, cb) in enumerate(carry):
            f0 = pl.multiple_of(q * t_len + i * SUBLANES, SUBLANES)
            b0 = pl.multiple_of(q * t_len + (n_blk - 1 - i) * SUBLANES, SUBLANES)
            hf = uf_ref[pl.ds(f0, SUBLANES), :] + af_ref[pl.ds(f0, SUBLANES), :] * cf
            hb = ub_ref[pl.ds(b0, SUBLANES), :] + ab_ref[pl.ds(b0, SUBLANES), :] * cb
            uf_ref[pl.ds(f0, SUBLANES), :] = hf
            ub_ref[pl.ds(b0, SUBLANES), :] = hb
            new.append((jnp.broadcast_to(hf[SUBLANES - 1:SUBLANES, :], (SUBLANES, width)),
                        jnp.broadcast_to(hb[0:1, :], (SUBLANES, width))))
        return tuple(new)

    if zero_state:
        init = tuple((jnp.zeros((SUBLANES, width), F32),) * 2 for _ in range(n_seq))
    else:
        init = tuple((jnp.broadcast_to(h0f_ref[q], (SUBLANES, width)),
                      jnp.broadcast_to(h0b_ref[q], (SUBLANES, width))) for q in range(n_seq))
    last = lax.fori_loop(0, n_blk, body, init)
    for q, (cf, cb) in enumerate(last):
        hlf_ref[q] = cf[0:1, :]
        hlb_ref[q] = cb[0:1, :]
    y_ref[...] = ((uf_ref[...] + ub_ref[...]) * _gelu_tanh(gb_ref[...])).astype(BF16)


def _lru(xb, gb, conv_w, conv_b, w_r, b_r, w_i, b_i, lam, h0, n_seq, t_len, tok_blk0, seq_per_step):
    rows = seq_per_step * t_len
    const2 = lambda s: (0, 0)
    const3 = lambda s: (0, 0, 0)
    blocks = pl.BlockSpec((2 * LRU_BLOCKS, LRU_BLOCK, LRU_BLOCK), const3)
    per_dir = pl.BlockSpec((2, LRU_WIDTH), const2)
    state = pl.BlockSpec((seq_per_step, 1, LRU_WIDTH), lambda s: (s, 0, 0))
    state_shape = jax.ShapeDtypeStruct((n_seq, 1, LRU_WIDTH), F32)
    return pl.pallas_call(
        functools.partial(_lru_kernel, zero_state=h0 is None, t_len=t_len),
        grid=(n_seq // seq_per_step,),
        in_specs=[
            pl.BlockSpec((rows, LRU_WIDTH), lambda s: (tok_blk0 + s, 0)),
            pl.BlockSpec((rows, LRU_WIDTH), lambda s: (tok_blk0 + s, 0)),
            pl.BlockSpec((CONV_W, LRU_WIDTH), const2),
            pl.BlockSpec((1, LRU_WIDTH), const2),
            blocks, blocks, per_dir, per_dir, per_dir,
        ] + ([] if h0 is None else [state, state]),
        out_specs=[pl.BlockSpec((rows, LRU_WIDTH), lambda s: (s, 0)), state, state],
        out_shape=[jax.ShapeDtypeStruct((n_seq * t_len, LRU_WIDTH), BF16), state_shape, state_shape],
        scratch_shapes=[pltpu.VMEM((rows, LRU_WIDTH), F32)] * 4
        + [pltpu.VMEM((LRU_WIDTH // LRU_SUB, LRU_SUB, 4 * LRU_SUB), BF16)],
        compiler_params=_cparams(1),
        name="rglru",
    )(xb, gb, conv_w, conv_b.reshape(1, LRU_WIDTH),
      w_r.reshape(2 * LRU_BLOCKS, LRU_BLOCK, LRU_BLOCK), w_i.reshape(2 * LRU_BLOCKS, LRU_BLOCK, LRU_BLOCK),
      b_r, b_i, lam, *(() if h0 is None else h0))


def _fourier_kernel(x_ref, mod_ref, g_ref, cs_ref, ct_ref, w_ref, o_ref, w_bf_ref, *, mod_row0, t_len):
    @pl.when(pl.program_id(0) == 0)
    def _():
        w_bf_ref[...] = w_ref[...].astype(BF16)

    x = x_ref[...]
    row = mod_row0 + pl.program_id(0) if mod_row0 else 0
    h = _norm_mod(x, _norm_gain(g_ref, 1, 1), _mod_vec(mod_ref, row, 3), _mod_vec(mod_ref, row, 4)).astype(BF16)
    cos_parts, sin_parts = [], []
    for g in range(FOURIER_GROUPS):
        ab = jnp.dot(h[:, g * GROUP_W:(g + 1) * GROUP_W], cs_ref[...], preferred_element_type=F32)
        cos_parts.append(ab[:, :GROUP_W])
        sin_parts.append(ab[:, GROUP_W:])
    cos_all = jnp.concatenate(cos_parts, axis=1).astype(BF16)
    sin_all = jnp.concatenate(sin_parts, axis=1).astype(BF16)
    f_parts = []
    for q in range(x.shape[0] // t_len):
        rows = slice(q * t_len, (q + 1) * t_len)
        stacked = jnp.concatenate([cos_all[rows], sin_all[rows]], axis=0)
        f_parts.append(jnp.dot(ct_ref[...], stacked, preferred_element_type=F32))
    f = jnp.concatenate(f_parts, axis=0) * ((t_len * GROUP_W) ** -0.5)
    y = jnp.dot(f.astype(BF16), w_bf_ref[...], preferred_element_type=F32)
    o_ref[...] = x + _mod_vec(mod_ref, row, 5) * y


def _dft_tables(t_len):
    def cos_sin(n):
        jk = np.outer(np.arange(n), np.arange(n)) % n
        ang = 2.0 * np.pi * jk.astype(np.float64) / n
        return np.cos(ang), np.sin(ang)

    cc, sc = cos_sin(GROUP_W)
    ct, st = cos_sin(t_len)
    chan = jnp.asarray(np.concatenate([cc, sc], axis=1).astype(np.float32)).astype(BF16)
    time = jnp.asarray(np.concatenate([ct, -st], axis=1).astype(np.float32)).astype(BF16)
    return chan, time


def _fourier(x, mod, norm_g, w_out, n_seq, t_len, tok_blk0, mod_row0, seq_per_step):
    assert seq_per_step == 1 or mod_row0 == 0
    chan, time = _dft_tables(t_len)
    rows = seq_per_step * t_len
    seq = lambda s: (tok_blk0 + s, 0)
    const = lambda s: (0, 0)
    return pl.pallas_call(
        functools.partial(_fourier_kernel, mod_row0=mod_row0, t_len=t_len),
        grid=(n_seq // seq_per_step,),
        in_specs=[
            pl.BlockSpec((rows, D_MODEL), seq),
            _mod_spec(mod),
            _norm_g_spec(),
            _resident((GROUP_W, 2 * GROUP_W), const),
            _resident((t_len, 2 * t_len), const),
            _resident((D_MODEL, D_MODEL), const),
        ],
        out_specs=pl.BlockSpec((rows, D_MODEL), seq),
        out_shape=jax.ShapeDtypeStruct((N_TOK, D_MODEL), F32),
        input_output_aliases={0: 0},
        scratch_shapes=[pltpu.VMEM((D_MODEL, D_MODEL), BF16)],
        compiler_params=_cparams(1),
        name="fourier_mixer",
    )(x, mod, norm_g, chan, time, w_out)


def _cache_layout(t):
    return jnp.transpose(t.reshape(BATCH, 1, NA_HEADS, HEAD_DIM, SEQ), (0, 1, 4, 2, 3))


def kernel(x_prompt, x_sample, cache_k, cache_v, state_lru_fwd, state_lru_bwd, c, c_ctx, w_ada, b_ada, norm_g, ffn1_gate, ffn1_up, ffn1_down, ffn2_gate, ffn2_up, ffn2_down, w_in, q_norm_g, k_norm_g, rpb, conv_w, conv_b, lru_w_r, lru_b_r, lru_w_i, lru_b_i, lru_lambda, w_out_ab, w_out_c):
    assert DEPTH == 2, "one neighbourhood/RG-LRU layer followed by one Fourier layer"
    c_ctx2 = c_ctx.reshape(1, D_MODEL)
    ada = (c_ctx2, c, w_ada, b_ada)
    mod0_ffn1 = _adaln(ada + (0, 0, 3 * D_MODEL), 3)

    gains = jnp.transpose(norm_g, (1, 0, 2))
    ffn1 = (ffn1_gate, ffn1_up, ffn1_down)
    ffn2 = (ffn2_gate, ffn2_up, ffn2_down)

    x, mod0_rest = _ffn((x_prompt.reshape(N_CTX_TOK, D_MODEL), x_sample.reshape(N_SMP_TOK, D_MODEL)),
                        mod0_ffn1, gains, 0, *ffn1, 0, 0,
                        adaln_next=ada + (0, 3 * D_MODEL, (N_MOD - 3) * D_MODEL))
    q, k, v, xb, gb, new_k, new_v = _proj(x, mod0_rest, 0, gains, w_in[0], q_norm_g[0], k_norm_g[0])
    o_ctx = _ctx_attn(q, k, v)
    o_smp, mod1 = _na_attn(q, k, v,
                           jnp.transpose(cache_k[:, 0], (0, 2, 3, 1)).reshape(DEC_BATCH, NA_WIDTH, PAST_LEN),
                           jnp.transpose(cache_v[:, 0], (0, 2, 3, 1)).reshape(DEC_BATCH, NA_WIDTH, PAST_LEN),
                           rpb[0], adaln_next=ada + (1, 0, MOD_WIDTH))
    lru_prm = (conv_w[0], conv_b[0], lru_w_r[0], lru_b_r[0], lru_w_i[0], lru_b_i[0], lru_lambda[0])
    yb_ctx, new_hf, new_hb = _lru(xb, gb, *lru_prm, None, BATCH, SEQ, 0, CTX_SEQ_PER_STEP)
    yb_smp, _, _ = _lru(xb, gb, *lru_prm, (state_lru_fwd, state_lru_bwd),
                        DEC_BATCH, DEC_SEQ, N_CTX_TOK // DEC_SEQ, 1)
    (x,) = _ffn((x,), mod0_rest, gains, 2, *ffn2, 0, 3, mixer_out=(o_ctx, o_smp, yb_ctx, yb_smp, w_out_ab))

    (x,) = _ffn((x,), mod1, gains, 0, *ffn1, 1, 0)
    x = _fourier(x, mod1, gains, w_out_c[0], BATCH, SEQ, 0, 0, CTX_SEQ_PER_STEP)
    x = _fourier(x, mod1, gains, w_out_c[0], DEC_BATCH, DEC_SEQ, N_CTX_TOK // DEC_SEQ, 1, 1)
    y_prompt, y_sample = _ffn((x,), mod1, gains, 2, *ffn2, 1, 6, split_out=True)

    return (y_prompt.reshape(BATCH, SEQ, D_MODEL), y_sample.reshape(DEC_BATCH, DEC_SEQ, D_MODEL),
            _cache_layout(new_k), _cache_layout(new_v),
            new_hf, new_hb)
```

```python
import functools

import numpy as np
import jax
import jax.numpy as jnp
from jax import lax
from jax.experimental import pallas as pl
from jax.experimental.pallas import tpu as pltpu

F32 = jnp.float32
BF16 = jnp.bfloat16

D_MODEL = 1024
BATCH = 16
SEQ = 256
DEPTH = 2
DEC_BATCH = 2
DEC_SEQ = 1024
PAST_LEN = 256
GRID_W = 64
HEAD_DIM = 64
NA_WIDTH = 512
NA_HEADS = 8
WIN_H = 8
WIN_W = 16
LRU_WIDTH = 512
LRU_BLOCKS = 8
LRU_BLOCK = 64
LRU_C = 8.0
LRU_SUB = 256
CONV_W = 4
FOURIER_GROUPS = 4
GROUP_W = D_MODEL // FOURIER_GROUPS
D_FF = 2816
N_MOD = 9
IN_WIDTH = 3 * NA_WIDTH + 2 * LRU_WIDTH
EPS = 1e-6

N_CTX_TOK = BATCH * SEQ
N_SMP_TOK = DEC_BATCH * DEC_SEQ
N_TOK = N_CTX_TOK + N_SMP_TOK
MOD_ROWS = 8
MOD_WIDTH = N_MOD * D_MODEL
ROWS = DEC_SEQ // GRID_W
KH = min(WIN_H, ROWS)

TOKEN_TILE = 512
N_CTX_TILES = N_CTX_TOK // TOKEN_TILE
CTX_SEQ_PER_STEP = 2
CTX_ATTN_SEQ_PER_STEP = 4
FFN_TILE = 512
FF_TILE = 256
FF_CHUNKS = D_FF // FF_TILE
FF_STAGE_SLOTS = 2
SUBLANES = 8
LANES = 128
VMEM_LIMIT = 56 * 1024 * 1024

NA_Q_ROWS = 4
NA_GROUPS = ROWS // NA_Q_ROWS
NA_K_ROWS = 12
NA_Q = NA_Q_ROWS * GRID_W
NA_K = NA_K_ROWS * GRID_W
N_DR = 2 * WIN_H - 1
N_DC = 2 * WIN_W - 1
N_DR_PAIRS = N_DR + 1


def _cparams(n_axes):
    return pltpu.CompilerParams(
        dimension_semantics=("arbitrary",) * n_axes, vmem_limit_bytes=VMEM_LIMIT)


def _resident(block_shape, index_map):
    return pl.BlockSpec(block_shape, index_map, pipeline_mode=pl.Buffered(1))


def _mod_spec(mod):
    return _resident(mod.shape, lambda i: (0, 0))


def _mod_row_of_tile(i, tile=TOKEN_TILE):
    n_ctx_tiles = N_CTX_TOK // tile
    tiles_per_seq = DEC_SEQ // tile
    return jnp.where(i < n_ctx_tiles, 0, 1 + (i - n_ctx_tiles) // tiles_per_seq)


def _mod_vec(mod_ref, row, k):
    return mod_ref[pl.ds(row, 1), k * D_MODEL:(k + 1) * D_MODEL]


def _norm_mod(x, g, shift, scale):
    ms = jnp.mean(x * x, axis=-1, keepdims=True)
    return (x * lax.rsqrt(ms + EPS)) * (g * (1.0 + scale)) + shift


def _adaln_slab(cctx_ref, c_ref, w_ref, b_ref, cond_ref, layer):
    cond_ref[...] = jnp.zeros_like(cond_ref)
    cond_ref[0:1, :] = cctx_ref[...]
    cond_ref[1:1 + DEC_BATCH, :] = c_ref[...]
    cond = cond_ref[...]
    s = (cond * jax.nn.sigmoid(cond)).astype(BF16)
    return jnp.dot(s, w_ref[...].astype(BF16), preferred_element_type=F32) + b_ref[layer:layer + 1, :]


def _adaln_specs(job, n_steps, step=lambda i: i):
    layer, col0, n_cols = job[4:]
    slab = n_cols // n_steps
    assert n_cols % n_steps == 0 and slab % LANES == 0 and col0 % slab == 0
    blk0 = col0 // slab
    return ([pl.BlockSpec((1, D_MODEL), lambda *ids: (0, 0)),
             pl.BlockSpec((DEC_BATCH, D_MODEL), lambda *ids: (0, 0)),
             pl.BlockSpec((None, D_MODEL, slab), lambda *ids: (layer, 0, blk0 + step(*ids))),
             pl.BlockSpec((DEPTH, slab), lambda *ids: (0, blk0 + step(*ids)))],
            pl.BlockSpec((MOD_ROWS, slab), lambda *ids: (0, step(*ids))),
            jax.ShapeDtypeStruct((MOD_ROWS, n_cols), F32))


def _adaln_kernel(cctx_ref, c_ref, w_ref, b_ref, o_ref, cond_ref, *, layer):
    o_ref[...] = _adaln_slab(cctx_ref, c_ref, w_ref, b_ref, cond_ref, layer)


def _adaln(job, n_steps):
    in_specs, out_spec, out_shape = _adaln_specs(job, n_steps)
    return pl.pallas_call(
        functools.partial(_adaln_kernel, layer=job[4]),
        grid=(n_steps,),
        in_specs=in_specs,
        out_specs=out_spec,
        out_shape=out_shape,
        scratch_shapes=[pltpu.VMEM((MOD_ROWS, D_MODEL), F32)],
        compiler_params=_cparams(1),
        name="adaln",
    )(*job[:4])


def _ffn_weight_copy(w_hbm, stage_ref, sem_ref, layer, j, ff_axis):
    ff = pl.ds(j * FF_TILE, FF_TILE)
    src = w_hbm.at[layer, :, ff] if ff_axis == 1 else w_hbm.at[layer, ff, :]
    slot = j % FF_STAGE_SLOTS
    return pltpu.make_async_copy(src, stage_ref.at[slot], sem_ref.at[slot])


def _mixer_out_copy(w_hbm, stage_ref, sem_ref, j):
    rows = stage_ref.shape[1]
    slot = j % stage_ref.shape[0]
    return pltpu.make_async_copy(w_hbm.at[0, pl.ds(j * rows, rows), :], stage_ref.at[slot], sem_ref.at[slot])


def _ffn_kernel(*refs, layer, mod_base, split_in, split_out, mixer_out, adaln_next):
    refs = list(refs)
    take = lambda n: [refs.pop(0) for _ in range(n)]
    x_refs = take(2 if split_in else 1)
    mix_refs = take(4) if mixer_out else None
    mod_ref, g_ref = take(2)
    wo_hbm = take(1)[0] if mixer_out else None
    ada_refs = take(4) if adaln_next is not None else None
    wg_hbm, wu_hbm, wd_hbm = take(3)
    o_refs = take(2 if split_out else 1)
    modn_ref = take(1)[0] if adaln_next is not None else None
    wg_bf, wu_bf, wd_bf, stg_g, stg_u, stg_d, sem_g, sem_u, sem_d = take(9)
    cond_scr = take(1)[0] if adaln_next is not None else None
    streams = ((wg_hbm, stg_g, sem_g, wg_bf, 1), (wu_hbm, stg_u, sem_u, wu_bf, 1),
               (wd_hbm, stg_d, sem_d, wd_bf, 0))

    i = pl.program_id(0)
    is_ctx = i < N_CTX_TOK // FFN_TILE
    if split_in:
        x = jnp.where(is_ctx, x_refs[0][...], x_refs[1][...])
    else:
        x = x_refs[0][...]
    row = _mod_row_of_tile(i, FFN_TILE)

    if mixer_out:
        wo_bf, stg_o, sem_o = take(3)
        rows = stg_o.shape[1]
        n_chunks = D_MODEL // rows

        @pl.when(i == 0)
        def _():
            for j in range(stg_o.shape[0]):
                _mixer_out_copy(wo_hbm, stg_o, sem_o, j).start()
            for j in range(n_chunks):
                _mixer_out_copy(wo_hbm, stg_o, sem_o, j).wait()
                wo_bf[j * rows:(j + 1) * rows, :] = stg_o[j % stg_o.shape[0]].astype(BF16)
                if j + stg_o.shape[0] < n_chunks:
                    _mixer_out_copy(wo_hbm, stg_o, sem_o, j + stg_o.shape[0]).start()

        oc_ref, os_ref, yc_ref, ys_ref = mix_refs
        cat = jnp.concatenate([jnp.where(is_ctx, oc_ref[...], os_ref[...]),
                               jnp.where(is_ctx, yc_ref[...], ys_ref[...])], axis=1)
        x = x + _mod_vec(mod_ref, row, mod_base - 1) * jnp.dot(cat, wo_bf[...], preferred_element_type=F32)

    h = _norm_mod(x, g_ref[...], _mod_vec(mod_ref, row, mod_base),
                  _mod_vec(mod_ref, row, mod_base + 1)).astype(BF16)

    def start_chunk(j):
        for w_hbm, stg, sem, _, ff_axis in streams:
            _ffn_weight_copy(w_hbm, stg, sem, layer, j, ff_axis).start()

    def finish_chunk(j):
        for w_hbm, stg, sem, w_bf, ff_axis in streams:
            _ffn_weight_copy(w_hbm, stg, sem, layer, j, ff_axis).wait()
            w_bf[j] = stg[j % FF_STAGE_SLOTS].astype(BF16)

    def run(stream_weights):
        if stream_weights:
            for j in range(FF_STAGE_SLOTS):
                start_chunk(j)
        acc = None
        for j in range(FF_CHUNKS):
            if stream_weights:
                finish_chunk(j)
                if j + FF_STAGE_SLOTS < FF_CHUNKS:
                    start_chunk(j + FF_STAGE_SLOTS)
            a = jnp.dot(h, wg_bf[j], preferred_element_type=F32)
            b = jnp.dot(h, wu_bf[j], preferred_element_type=F32)
            if adaln_next is not None and j == 1:
                modn_ref[...] = _adaln_slab(*ada_refs, cond_scr, adaln_next)
            act = (a * jax.nn.sigmoid(a) * b).astype(BF16)
            y = jnp.dot(act, wd_bf[j], preferred_element_type=F32)
            acc = y if acc is None else acc + y
        res = x + 0.5 * _mod_vec(mod_ref, row, mod_base + 2) * acc
        if split_out:
            @pl.when(is_ctx)
            def _():
                o_refs[0][...] = res

            @pl.when(jnp.logical_not(is_ctx))
            def _():
                o_refs[1][...] = res
        else:
            o_refs[0][...] = res

    @pl.when(i == 0)
    def _():
        run(True)

    @pl.when(i > 0)
    def _():
        run(False)


def _ffn(xs, mod, g, wg, wu, wd, layer, mod_base, split_out=False, mixer_out=None, adaln_next=None):
    tm = FFN_TILE
    n_ctx_tiles = N_CTX_TOK // tm
    split_in = len(xs) == 2

    def tiles(width):
        return (pl.BlockSpec((tm, width), lambda i: (i, 0)),
                pl.BlockSpec((tm, width), lambda i: (jnp.minimum(i, n_ctx_tiles - 1), 0)),
                pl.BlockSpec((tm, width), lambda i: (jnp.maximum(i - n_ctx_tiles, 0), 0)))

    tok, ctx_tok, smp_tok = tiles(D_MODEL)
    full = jax.ShapeDtypeStruct((N_TOK, D_MODEL), F32)
    pair = [jax.ShapeDtypeStruct((N_CTX_TOK, D_MODEL), F32), jax.ShapeDtypeStruct((N_SMP_TOK, D_MODEL), F32)]
    hbm = pl.BlockSpec(memory_space=pl.ANY)
    in_specs = [ctx_tok, smp_tok] if split_in else [tok]
    operands = list(xs)
    scratch = [
        pltpu.VMEM((FF_CHUNKS, D_MODEL, FF_TILE), BF16),
        pltpu.VMEM((FF_CHUNKS, D_MODEL, FF_TILE), BF16),
        pltpu.VMEM((FF_CHUNKS, FF_TILE, D_MODEL), BF16),
        pltpu.VMEM((FF_STAGE_SLOTS, D_MODEL, FF_TILE), F32),
        pltpu.VMEM((FF_STAGE_SLOTS, D_MODEL, FF_TILE), F32),
        pltpu.VMEM((FF_STAGE_SLOTS, FF_TILE, D_MODEL), F32),
        pltpu.SemaphoreType.DMA((FF_STAGE_SLOTS,)),
        pltpu.SemaphoreType.DMA((FF_STAGE_SLOTS,)),
        pltpu.SemaphoreType.DMA((FF_STAGE_SLOTS,)),
    ]
    if mixer_out is not None:
        _, ctx_half, smp_half = tiles(NA_WIDTH)
        in_specs += [ctx_half, smp_half, ctx_half, smp_half]
        operands += list(mixer_out[:4])
    in_specs += [_mod_spec(mod), pl.BlockSpec((1, D_MODEL), lambda i: (0, 0))]
    operands += [mod, g.reshape(1, D_MODEL)]
    out_specs = [ctx_tok, smp_tok] if split_out else [tok]
    out_shape = pair if split_out else [full]
    if adaln_next is not None:
        ada_in, ada_out, ada_shape = _adaln_specs(adaln_next, N_TOK // tm)
        in_specs += ada_in
        operands += list(adaln_next[:4])
        out_specs.append(ada_out)
        out_shape.append(ada_shape)
        scratch.append(pltpu.VMEM((MOD_ROWS, D_MODEL), F32))
    if mixer_out is not None:
        in_specs.insert(len(in_specs) - (4 if adaln_next is not None else 0), hbm)
        operands.insert(len(operands) - (4 if adaln_next is not None else 0), mixer_out[4])
        scratch += [
            pltpu.VMEM((D_MODEL, D_MODEL), BF16),
            pltpu.VMEM((FF_STAGE_SLOTS, FF_TILE, D_MODEL), F32),
            pltpu.SemaphoreType.DMA((FF_STAGE_SLOTS,)),
        ]
    return pl.pallas_call(
        functools.partial(_ffn_kernel, layer=layer, mod_base=mod_base, split_in=split_in,
                          split_out=split_out, mixer_out=mixer_out is not None,
                          adaln_next=None if adaln_next is None else adaln_next[4]),
        grid=(N_TOK // tm,),
        in_specs=in_specs + [hbm, hbm, hbm],
        out_specs=out_specs,
        out_shape=out_shape,
        scratch_shapes=scratch,
        compiler_params=_cparams(1),
        name="ffn",
    )(*operands, wg, wu, wd)


def _head_rms_norm(z, g, ones_bd):
    z2 = z * z
    hi = z2.astype(BF16)
    lo = (z2 - hi.astype(F32)).astype(BF16)
    n = ones_bd.shape[0]
    parts = []
    for c in range(z.shape[1] // n):
        sl = slice(c * n, (c + 1) * n)
        parts.append(jnp.dot(hi[:, sl], ones_bd, preferred_element_type=F32)
                     + jnp.dot(lo[:, sl], ones_bd, preferred_element_type=F32))
    ss = jnp.concatenate(parts, axis=1)
    return z * lax.rsqrt(ss * (1.0 / HEAD_DIM) + EPS) * g


def _proj_weight_copy(w_hbm, stage_ref, sem_ref, part):
    return pltpu.make_async_copy(w_hbm.at[:, pl.ds(part * NA_WIDTH, NA_WIDTH)], stage_ref.at[part],
                                 sem_ref.at[part])


def _proj_kernel(*refs, mod_k0, adaln_layer):
    refs = list(refs)
    x_ref, mod_ref, g_ref, w_hbm, qg_ref, kg_ref, ones_ref = refs[:7]
    ada_refs = refs[7:11] if adaln_layer is not None else None
    n_in = 7 if adaln_layer is None else 11
    q_ref, k_ref, v_ref, xb_ref, gb_ref, kout_ref, vout_ref = refs[n_in:n_in + 7]
    rest = refs[n_in + 7:]
    modn_ref = rest.pop(0) if adaln_layer is not None else None
    w_bf_ref, stage_ref, sem_ref = rest[:3]
    i = pl.program_id(0)
    n_groups = IN_WIDTH // NA_WIDTH

    def run(stream_weights):
        if stream_weights:
            for part in range(n_groups):
                _proj_weight_copy(w_hbm, stage_ref, sem_ref, part).start()
        x = x_ref[...]
        row = _mod_row_of_tile(i)
        h = _norm_mod(x, g_ref[...], _mod_vec(mod_ref, row, mod_k0),
                      _mod_vec(mod_ref, row, mod_k0 + 1)).astype(BF16)

        def proj(part):
            cols = slice(part * NA_WIDTH, (part + 1) * NA_WIDTH)
            if stream_weights:
                _proj_weight_copy(w_hbm, stage_ref, sem_ref, part).wait()
                w_bf_ref[:, cols] = stage_ref[part].astype(BF16)
            return jnp.dot(h, w_bf_ref[:, cols], preferred_element_type=F32)

        ones_bd = ones_ref[...]
        q_raw = proj(0)
        k_raw = proj(1)
        q = _head_rms_norm(q_raw, jnp.tile(qg_ref[...], (1, NA_HEADS)), ones_bd) * (HEAD_DIM ** -0.5)
        q_ref[...] = q.astype(BF16)
        v = proj(2)
        if adaln_layer is not None:
            modn_ref[...] = _adaln_slab(*ada_refs, rest[3], adaln_layer)
        k = _head_rms_norm(k_raw, jnp.tile(kg_ref[...], (1, NA_HEADS)), ones_bd)
        k_ref[...] = k.astype(BF16)
        xb = proj(3)
        v_ref[...] = v.astype(BF16)
        gb = proj(4)
        xb_ref[...] = xb
        gb_ref[...] = gb

        kt = [k[b * SEQ:(b + 1) * SEQ, :].T for b in range(TOKEN_TILE // SEQ)]
        vt = [v[b * SEQ:(b + 1) * SEQ, :].T for b in range(TOKEN_TILE // SEQ)]

        @pl.when(i < N_CTX_TILES)
        def _():
            for b in range(TOKEN_TILE // SEQ):
                kout_ref[b] = kt[b]
                vout_ref[b] = vt[b]

    @pl.when(i == 0)
    def _():
        run(True)

    @pl.when(i > 0)
    def _():
        run(False)


def _proj(x, mod, mod_k0, g, w_in, q_g, k_g, adaln_next=None):
    tm = TOKEN_TILE
    head = np.arange(2 * LANES) // HEAD_DIM
    ones_bd = jnp.asarray((head[:, None] == head[None, :]).astype(np.float32), dtype=BF16)
    tok = lambda i: (i, 0)
    const = lambda i: (0, 0)
    act_f32 = jax.ShapeDtypeStruct((N_TOK, NA_WIDTH), F32)
    act_bf16 = jax.ShapeDtypeStruct((N_TOK, NA_WIDTH), BF16)
    cache = jax.ShapeDtypeStruct((BATCH, NA_WIDTH, SEQ), F32)
    cache_spec = pl.BlockSpec((tm // SEQ, NA_WIDTH, SEQ), lambda i: (jnp.minimum(i, N_CTX_TILES - 1), 0, 0))
    in_specs = [
        pl.BlockSpec((tm, D_MODEL), tok),
        _mod_spec(mod),
        pl.BlockSpec((1, D_MODEL), const),
        pl.BlockSpec(memory_space=pl.ANY),
        pl.BlockSpec((1, HEAD_DIM), const),
        pl.BlockSpec((1, HEAD_DIM), const),
        _resident((2 * LANES, 2 * LANES), const),
    ]
    operands = [x, mod, g.reshape(1, D_MODEL), w_in, q_g.reshape(1, HEAD_DIM), k_g.reshape(1, HEAD_DIM), ones_bd]
    out_specs = [pl.BlockSpec((tm, NA_WIDTH), tok)] * 5 + [cache_spec, cache_spec]
    out_shape = [act_bf16, act_bf16, act_bf16, act_f32, act_f32, cache, cache]
    scratch = [pltpu.VMEM((D_MODEL, IN_WIDTH), BF16),
               pltpu.VMEM((IN_WIDTH // NA_WIDTH, D_MODEL, NA_WIDTH), F32),
               pltpu.SemaphoreType.DMA((IN_WIDTH // NA_WIDTH,))]
    if adaln_next is not None:
        ada_in, ada_out, ada_shape = _adaln_specs(adaln_next, N_TOK // tm)
        in_specs += ada_in
        operands += list(adaln_next[:4])
        out_specs.append(ada_out)
        out_shape.append(ada_shape)
        scratch.append(pltpu.VMEM((MOD_ROWS, D_MODEL), F32))
    return pl.pallas_call(
        functools.partial(_proj_kernel, mod_k0=mod_k0,
                          adaln_layer=None if adaln_next is None else adaln_next[4]),
        grid=(N_TOK // tm,),
        in_specs=in_specs,
        out_specs=out_specs,
        out_shape=out_shape,
        scratch_shapes=scratch,
        compiler_params=_cparams(1),
        name="mixer_in_proj",
    )(*operands)


def _head_masks():
    lane = lax.broadcasted_iota(jnp.int32, (1, 2 * HEAD_DIM), 1)
    return [lane < HEAD_DIM, lane >= HEAD_DIM]


def _ctx_attn_kernel(q_ref, k_ref, v_ref, o_ref):
    masks = _head_masks()
    units = [(b, h) for b in range(q_ref.shape[0] // SEQ) for h in range(NA_HEADS)]

    def where(b, h):
        return slice(b * SEQ, (b + 1) * SEQ), slice(2 * HEAD_DIM * (h // 2), 2 * HEAD_DIM * (h // 2 + 1))

    def scores(b, h):
        rows, sl = where(b, h)
        q2 = q_ref[rows, sl]
        qm = jnp.where(masks[h % 2], q2, jnp.zeros_like(q2))
        return jnp.dot(qm, k_ref[rows, sl].T, preferred_element_type=F32)

    def attend(b, h, s):
        rows, sl = where(b, h)
        pe = jnp.exp(s - jnp.max(s, axis=-1, keepdims=True))
        den = jnp.sum(pe, axis=-1, keepdims=True)
        return jnp.dot(pe.astype(BF16), v_ref[rows, sl], preferred_element_type=F32) / den

    pending = scores(*units[0])
    out = None
    for n, (b, h) in enumerate(units):
        current = pending
        if n + 1 < len(units):
            pending = scores(*units[n + 1])
        o = attend(b, h, current)
        if h % 2 == 0:
            out = o
        else:
            rows, sl = where(b, h)
            o_ref[rows, sl] = jnp.where(masks[1], o, out).astype(BF16)


def _ctx_attn(q, k, v):
    blk = pl.BlockSpec((CTX_ATTN_SEQ_PER_STEP * SEQ, NA_WIDTH), lambda b: (b, 0))
    return pl.pallas_call(
        _ctx_attn_kernel,
        grid=(BATCH // CTX_ATTN_SEQ_PER_STEP,),
        in_specs=[blk, blk, blk],
        out_specs=blk,
        out_shape=jax.ShapeDtypeStruct((N_CTX_TOK, NA_WIDTH), BF16),
        compiler_params=_cparams(1),
        name="ctx_attention",
    )(q, k, v)


def _na_build_bias_table(rpb_ref, table_ref):
    qc = lax.broadcasted_iota(jnp.int32, (GRID_W, LANES), 0)
    lane = lax.broadcasted_iota(jnp.int32, (GRID_W, LANES), 1)
    kc = lane % GRID_W
    col_start = jnp.clip(qc - WIN_W // 2, 0, GRID_W - WIN_W)
    col_in = (kc >= col_start) & (kc < col_start + WIN_W)
    neg = jnp.full((GRID_W, LANES), -jnp.inf, F32)

    def toeplitz(h, dr, lane0):
        if dr < 0 or dr >= N_DR:
            return neg
        row = jnp.pad(rpb_ref[dr, h:h + 1, :], ((0, 0), (0, LANES - N_DC)))
        w = jnp.broadcast_to(row, (GRID_W, LANES))
        return pltpu.roll(w, (lane0 - (WIN_W - 1)) % LANES, 1, stride=1, stride_axis=0)

    for h in range(NA_HEADS):
        for i in range(N_DR_PAIRS):
            t = jnp.where(lane < GRID_W, toeplitz(h, i - 1, 0), toeplitz(h, i, GRID_W))
            table_ref[h, i] = jnp.where(col_in, t, neg)


def _na_kernel(*refs, adaln_layer):
    q_ref, k_ref, v_ref, kc_ref, vc_ref, rpb_ref = refs[:6]
    if adaln_layer is None:
        o_ref, table_ref = refs[6:]
    else:
        ada_refs = refs[6:10]
        o_ref, modn_ref, table_ref, cond_scr = refs[10:]
    b = pl.program_id(0)
    g = pl.program_id(1)

    @pl.when((b == 0) & (g == 0))
    def _():
        _na_build_bias_table(rpb_ref, table_ref)

    win_row0 = jnp.where(g < NA_GROUPS // 2, 0, ROWS - NA_K_ROWS)
    start = pl.multiple_of(win_row0 * GRID_W, GRID_W)
    q_row = g * NA_Q_ROWS + lax.broadcasted_iota(jnp.int32, (NA_Q, 1), 0) // GRID_W
    k_row = win_row0 + lax.broadcasted_iota(jnp.int32, (1, NA_K), 1) // GRID_W
    row_start = jnp.clip(q_row - KH // 2, 0, ROWS - KH)
    row_in = (k_row >= row_start) & (k_row < row_start + KH)
    masks = _head_masks()

    def pair_slab(p):
        return slice(2 * HEAD_DIM * p, 2 * HEAD_DIM * (p + 1))

    def scores(head):
        p, e = divmod(head, 2)
        sl = pair_slab(p)
        q2 = q_ref[:, sl]
        klt = k_ref[pl.ds(start, NA_K), sl].T
        kct = kc_ref[0, sl, :].astype(BF16)
        bias_rows = []
        for a in range(NA_Q_ROWS):
            tiles = []
            for m in range(NA_K_ROWS // 2):
                dr = win_row0 + 2 * m - (g * NA_Q_ROWS + a) + (WIN_H - 1)
                tiles.append(table_ref[head, jnp.clip(dr + 1, 0, N_DR_PAIRS - 1)])
            bias_rows.append(jnp.concatenate(tiles, axis=1))
        bias = jnp.concatenate(bias_rows, axis=0)
        qm = jnp.where(masks[e], q2, jnp.zeros_like(q2))
        s_loc = jnp.where(row_in, jnp.dot(qm, klt, preferred_element_type=F32) + bias, -jnp.inf)
        s_ctx = jnp.dot(qm, kct, preferred_element_type=F32)
        return s_loc, s_ctx

    def attend(head, s_loc, s_ctx):
        sl = pair_slab(head // 2)
        vl = v_ref[pl.ds(start, NA_K), sl]
        vct = vc_ref[0, sl, :].astype(BF16)
        m_max = jnp.maximum(jnp.max(s_loc, axis=-1, keepdims=True),
                            jnp.max(s_ctx, axis=-1, keepdims=True))
        p_loc = jnp.exp(s_loc - m_max)
        p_ctx = jnp.exp(s_ctx - m_max)
        den = jnp.sum(p_loc, axis=-1, keepdims=True) + jnp.sum(p_ctx, axis=-1, keepdims=True)
        return (jnp.dot(p_loc.astype(BF16), vl, preferred_element_type=F32)
                + lax.dot_general(p_ctx.astype(BF16), vct, (((1,), (1,)), ((), ())),
                                  preferred_element_type=F32)) / den

    pending = scores(0)
    out = None
    for head in range(NA_HEADS):
        current = pending
        if head + 1 < NA_HEADS:
            pending = scores(head + 1)
        o = attend(head, *current)
        if adaln_layer is not None and head == 0:
            modn_ref[...] = _adaln_slab(*ada_refs, cond_scr, adaln_layer)
        if head % 2 == 0:
            out = o
        else:
            o_ref[:, pair_slab(head // 2)] = jnp.where(masks[1], o, out).astype(BF16)


def _na_attn(q, k, v, k_ctx, v_ctx, rpb_e, adaln_next=None):
    smp_blk0 = N_CTX_TOK // DEC_SEQ
    q_blk0 = N_CTX_TOK // NA_Q
    kv = pl.BlockSpec((DEC_SEQ, NA_WIDTH), lambda b, g: (smp_blk0 + b, 0))
    ctx = pl.BlockSpec((1, NA_WIDTH, PAST_LEN), lambda b, g: (b, 0, 0))
    in_specs = [
        pl.BlockSpec((NA_Q, NA_WIDTH), lambda b, g: (q_blk0 + b * NA_GROUPS + g, 0)),
        kv, kv, ctx, ctx,
        pl.BlockSpec((N_DR, NA_HEADS, N_DC), lambda b, g: (0, 0, 0)),
    ]
    operands = [q, k, v, k_ctx, v_ctx, jnp.transpose(rpb_e, (1, 0, 2))]
    out_specs = [pl.BlockSpec((NA_Q, NA_WIDTH), lambda b, g: (b * NA_GROUPS + g, 0))]
    out_shape = [jax.ShapeDtypeStruct((N_SMP_TOK, NA_WIDTH), BF16)]
    scratch = [pltpu.VMEM((NA_HEADS, N_DR_PAIRS, GRID_W, LANES), F32)]
    if adaln_next is not None:
        ada_in, ada_out, ada_shape = _adaln_specs(adaln_next, DEC_BATCH * NA_GROUPS,
                                                  lambda b, g: b * NA_GROUPS + g)
        in_specs += ada_in
        operands += list(adaln_next[:4])
        out_specs.append(ada_out)
        out_shape.append(ada_shape)
        scratch.append(pltpu.VMEM((MOD_ROWS, D_MODEL), F32))
    return pl.pallas_call(
        functools.partial(_na_kernel, adaln_layer=None if adaln_next is None else adaln_next[4]),
        grid=(DEC_BATCH, NA_GROUPS),
        in_specs=in_specs,
        out_specs=out_specs,
        out_shape=out_shape,
        scratch_shapes=scratch,
        compiler_params=_cparams(2),
        name="neighbourhood_attention",
    )(*operands)


def _gelu_tanh(x):
    c0 = float(np.sqrt(2.0 / np.pi))
    inner = x * (c0 + (c0 * 0.044715) * (x * x))
    return (0.5 * x) * (1.0 + jnp.tanh(inner))


def _lru_build_gate_weights(wr_ref, wi_ref, w_scr):
    blocks_per_group = LRU_SUB // LRU_BLOCK
    w_scr[...] = jnp.zeros_like(w_scr)
    for d in range(2):
        for kind, w_ref in enumerate((wr_ref, wi_ref)):
            col0 = (2 * d + kind) * LRU_SUB
            for blk in range(LRU_BLOCKS):
                c, n = divmod(blk, blocks_per_group)
                r0 = n * LRU_BLOCK
                w_scr[c, r0:r0 + LRU_BLOCK, col0 + r0:col0 + r0 + LRU_BLOCK] = (
                    0.5 * w_ref[d * LRU_BLOCKS + blk]).astype(BF16)


def _lru_kernel(*refs, zero_state, t_len):
    refs = list(refs)
    xb_ref, gb_ref, cw_ref, cb_ref, wr_ref, wi_ref, br_ref, bi_ref, lam_ref = refs[:9]
    h0f_ref, h0b_ref = (None, None) if zero_state else refs[9:11]
    y_ref, hlf_ref, hlb_ref, af_ref, uf_ref, ab_ref, ub_ref, w_scr = refs[-8:]
    width = xb_ref.shape[1]
    n_seq = xb_ref.shape[0] // t_len
    n_blk = t_len // SUBLANES
    row = lax.broadcasted_iota(jnp.int32, (t_len, 1), 0)
    in_block = lax.broadcasted_iota(jnp.int32, (1, SUBLANES, 1), 1)

    @pl.when(pl.program_id(0) == 0)
    def _():
        _lru_build_gate_weights(wr_ref, wi_ref, w_scr)

    def shifted(z, s):
        rolled = pltpu.roll(z, (-s) % t_len, axis=0)
        ok = (row + s >= 0) & (row + s < t_len)
        return jnp.where(ok, rolled, 0.0)

    left = (CONV_W - 1) // 2
    for q, c in [(q, c) for q in range(n_seq) for c in range(width // LRU_SUB)]:
        rows = slice(q * t_len, (q + 1) * t_len)
        cs = slice(c * LRU_SUB, (c + 1) * LRU_SUB)
        x = xb_ref[rows, cs]
        xc = cb_ref[:, cs]
        for j in range(CONV_W):
            tap = x if j == left else shifted(x, j - left)
            xc = xc + tap * cw_ref[j:j + 1, cs]
        half_gates = jnp.dot(xc.astype(BF16), w_scr[c], preferred_element_type=F32)
        half_xc = 0.5 * xc
        for d, (a_ref, u_ref) in enumerate(((af_ref, uf_ref), (ab_ref, ub_ref))):
            t_r = jnp.tanh(half_gates[:, (2 * d) * LRU_SUB:(2 * d + 1) * LRU_SUB] + 0.5 * br_ref[d:d + 1, cs])
            t_i = jnp.tanh(half_gates[:, (2 * d + 1) * LRU_SUB:(2 * d + 2) * LRU_SUB] + 0.5 * bi_ref[d:d + 1, cs])
            lam = lam_ref[d:d + 1, cs]
            log_sig = jnp.minimum(lam, 0.0) - jnp.log1p(jnp.exp(-jnp.abs(lam)))
            half_c_log_sig = (0.5 * LRU_C) * log_sig
            log_a = t_r * half_c_log_sig + half_c_log_sig
            a = jnp.exp(log_a)
            var = -jnp.tanh(log_a) * (a * a + 1.0)
            u = jnp.where(var > 0.0, var * lax.rsqrt(var), 0.0) * ((t_i + 1.0) * half_xc)
            a = a.reshape(n_blk, SUBLANES, LRU_SUB)
            u = u.reshape(n_blk, SUBLANES, LRU_SUB)
            step = 1
            while step < SUBLANES:
                if d == 0:
                    ok, shift = in_block >= step, step
                else:
                    ok, shift = in_block < SUBLANES - step, SUBLANES - step
                a_prev = jnp.where(ok, pltpu.roll(a, shift, axis=1), 1.0)
                u_prev = jnp.where(ok, pltpu.roll(u, shift, axis=1), 0.0)
                u = u + a * u_prev
                a = a * a_prev
                step *= 2
            a_ref[rows, cs] = a.reshape(t_len, LRU_SUB)
            u_ref[rows, cs] = u.reshape(t_len, LRU_SUB)

    def body(i, carry):
        new = []
        for q, (cf, cb) in enumerate(carry):
            f0 = pl.multiple_of(q * t_len + i * SUBLANES, SUBLANES)
            b0 = pl.multiple_of(q * t_len + (n_blk - 1 - i) * SUBLANES, SUBLANES)
            hf = uf_ref[pl.ds(f0, SUBLANES), :] + af_ref[pl.ds(f0, SUBLANES), :] * cf
            hb = ub_ref[pl.ds(b0, SUBLANES), :] + ab_ref[pl.ds(b0, SUBLANES), :] * cb
            uf_ref[pl.ds(f0, SUBLANES), :] = hf
            ub_ref[pl.ds(b0, SUBLANES), :] = hb
            new.append((jnp.broadcast_to(hf[SUBLANES - 1:SUBLANES, :], (SUBLANES, width)),
                        jnp.broadcast_to(hb[0:1, :], (SUBLANES, width))))
        return tuple(new)

    if zero_state:
        init = tuple((jnp.zeros((SUBLANES, width), F32),) * 2 for _ in range(n_seq))
    else:
        init = tuple((jnp.broadcast_to(h0f_ref[q], (SUBLANES, width)),
                      jnp.broadcast_to(h0b_ref[q], (SUBLANES, width))) for q in range(n_seq))
    last = lax.fori_loop(0, n_blk, body, init)
    for q, (cf, cb) in enumerate(last):
        hlf_ref[q] = cf[0:1, :]
        hlb_ref[q] = cb[0:1, :]
    y_ref[...] = ((uf_ref[...] + ub_ref[...]) * _gelu_tanh(gb_ref[...])).astype(BF16)


def _lru(xb, gb, conv_w, conv_b, w_r, b_r, w_i, b_i, lam, h0, n_seq, t_len, tok_blk0, seq_per_step):
    rows = seq_per_step * t_len
    const2 = lambda s: (0, 0)
    const3 = lambda s: (0, 0, 0)
    blocks = pl.BlockSpec((2 * LRU_BLOCKS, LRU_BLOCK, LRU_BLOCK), const3)
    per_dir = pl.BlockSpec((2, LRU_WIDTH), const2)
    state = pl.BlockSpec((seq_per_step, 1, LRU_WIDTH), lambda s: (s, 0, 0))
    state_shape = jax.ShapeDtypeStruct((n_seq, 1, LRU_WIDTH), F32)
    return pl.pallas_call(
        functools.partial(_lru_kernel, zero_state=h0 is None, t_len=t_len),
        grid=(n_seq // seq_per_step,),
        in_specs=[
            pl.BlockSpec((rows, LRU_WIDTH), lambda s: (tok_blk0 + s, 0)),
            pl.BlockSpec((rows, LRU_WIDTH), lambda s: (tok_blk0 + s, 0)),
            pl.BlockSpec((CONV_W, LRU_WIDTH), const2),
            pl.BlockSpec((1, LRU_WIDTH), const2),
            blocks, blocks, per_dir, per_dir, per_dir,
        ] + ([] if h0 is None else [state, state]),
        out_specs=[pl.BlockSpec((rows, LRU_WIDTH), lambda s: (s, 0)), state, state],
        out_shape=[jax.ShapeDtypeStruct((n_seq * t_len, LRU_WIDTH), BF16), state_shape, state_shape],
        scratch_shapes=[pltpu.VMEM((rows, LRU_WIDTH), F32)] * 4
        + [pltpu.VMEM((LRU_WIDTH // LRU_SUB, LRU_SUB, 4 * LRU_SUB), BF16)],
        compiler_params=_cparams(1),
        name="rglru",
    )(xb, gb, conv_w, conv_b.reshape(1, LRU_WIDTH),
      w_r.reshape(2 * LRU_BLOCKS, LRU_BLOCK, LRU_BLOCK), w_i.reshape(2 * LRU_BLOCKS, LRU_BLOCK, LRU_BLOCK),
      b_r, b_i, lam, *(() if h0 is None else h0))


def _fourier_kernel(x_ref, mod_ref, g_ref, cs_ref, ct_ref, w_ref, o_ref, w_bf_ref, *, mod_row0, t_len):
    @pl.when(pl.program_id(0) == 0)
    def _():
        w_bf_ref[...] = w_ref[...].astype(BF16)

    x = x_ref[...]
    row = mod_row0 + pl.program_id(0) if mod_row0 else 0
    h = _norm_mod(x, g_ref[...], _mod_vec(mod_ref, row, 3), _mod_vec(mod_ref, row, 4)).astype(BF16)
    cos_parts, sin_parts = [], []
    for g in range(FOURIER_GROUPS):
        ab = jnp.dot(h[:, g * GROUP_W:(g + 1) * GROUP_W], cs_ref[...], preferred_element_type=F32)
        cos_parts.append(ab[:, :GROUP_W])
        sin_parts.append(ab[:, GROUP_W:])
    cos_all = jnp.concatenate(cos_parts, axis=1).astype(BF16)
    sin_all = jnp.concatenate(sin_parts, axis=1).astype(BF16)
    f_parts = []
    for q in range(x.shape[0] // t_len):
        rows = slice(q * t_len, (q + 1) * t_len)
        stacked = jnp.concatenate([cos_all[rows], sin_all[rows]], axis=0)
        f_parts.append(jnp.dot(ct_ref[...], stacked, preferred_element_type=F32))
    f = jnp.concatenate(f_parts, axis=0) * ((t_len * GROUP_W) ** -0.5)
    y = jnp.dot(f.astype(BF16), w_bf_ref[...], preferred_element_type=F32)
    o_ref[...] = x + _mod_vec(mod_ref, row, 5) * y


def _dft_tables(t_len):
    def cos_sin(n):
        jk = np.outer(np.arange(n), np.arange(n)) % n
        ang = 2.0 * np.pi * jk.astype(np.float64) / n
        return np.cos(ang), np.sin(ang)

    cc, sc = cos_sin(GROUP_W)
    ct, st = cos_sin(t_len)
    chan = jnp.asarray(np.concatenate([cc, sc], axis=1).astype(np.float32)).astype(BF16)
    time = jnp.asarray(np.concatenate([ct, -st], axis=1).astype(np.float32)).astype(BF16)
    return chan, time


def _fourier(x, mod, g, w_out, n_seq, t_len, tok_blk0, mod_row0, seq_per_step):
    assert seq_per_step == 1 or mod_row0 == 0
    chan, time = _dft_tables(t_len)
    rows = seq_per_step * t_len
    seq = lambda s: (tok_blk0 + s, 0)
    const = lambda s: (0, 0)
    return pl.pallas_call(
        functools.partial(_fourier_kernel, mod_row0=mod_row0, t_len=t_len),
        grid=(n_seq // seq_per_step,),
        in_specs=[
            pl.BlockSpec((rows, D_MODEL), seq),
            _mod_spec(mod),
            pl.BlockSpec((1, D_MODEL), const),
            _resident((GROUP_W, 2 * GROUP_W), const),
            _resident((t_len, 2 * t_len), const),
            _resident((D_MODEL, D_MODEL), const),
        ],
        out_specs=pl.BlockSpec((rows, D_MODEL), seq),
        out_shape=jax.ShapeDtypeStruct((N_TOK, D_MODEL), F32),
        input_output_aliases={0: 0},
        scratch_shapes=[pltpu.VMEM((D_MODEL, D_MODEL), BF16)],
        compiler_params=_cparams(1),
        name="fourier_mixer",
    )(x, mod, g.reshape(1, D_MODEL), chan, time, w_out)


def _cache_layout(t):
    return jnp.transpose(t.reshape(BATCH, 1, NA_HEADS, HEAD_DIM, SEQ), (0, 1, 4, 2, 3))


def kernel(x_prompt, x_sample, cache_k, cache_v, state_lru_fwd, state_lru_bwd, c, c_ctx, w_ada, b_ada, norm_g, ffn1_gate, ffn1_up, ffn1_down, ffn2_gate, ffn2_up, ffn2_down, w_in, q_norm_g, k_norm_g, rpb, conv_w, conv_b, lru_w_r, lru_b_r, lru_w_i, lru_b_i, lru_lambda, w_out_ab, w_out_c):
    assert DEPTH == 2, "one neighbourhood/RG-LRU layer followed by one Fourier layer"
    c_ctx2 = c_ctx.reshape(1, D_MODEL)
    ada = (c_ctx2, c, w_ada, b_ada)
    mod0_ffn1 = _adaln(ada + (0, 0, 3 * D_MODEL), 3)

    ffn1 = (ffn1_gate, ffn1_up, ffn1_down)
    ffn2 = (ffn2_gate, ffn2_up, ffn2_down)

    x, mod0_rest = _ffn((x_prompt.reshape(N_CTX_TOK, D_MODEL), x_sample.reshape(N_SMP_TOK, D_MODEL)),
                        mod0_ffn1, norm_g[0, 0], *ffn1, 0, 0,
                        adaln_next=ada + (0, 3 * D_MODEL, (N_MOD - 3) * D_MODEL))
    q, k, v, xb, gb, new_k, new_v = _proj(x, mod0_rest, 0, norm_g[0, 1], w_in[0], q_norm_g[0], k_norm_g[0])
    o_ctx = _ctx_attn(q, k, v)
    o_smp, mod1 = _na_attn(q, k, v,
                           jnp.transpose(cache_k[:, 0], (0, 2, 3, 1)).reshape(DEC_BATCH, NA_WIDTH, PAST_LEN),
                           jnp.transpose(cache_v[:, 0], (0, 2, 3, 1)).reshape(DEC_BATCH, NA_WIDTH, PAST_LEN),
                           rpb[0], adaln_next=ada + (1, 0, MOD_WIDTH))
    lru_prm = (conv_w[0], conv_b[0], lru_w_r[0], lru_b_r[0], lru_w_i[0], lru_b_i[0], lru_lambda[0])
    yb_ctx, new_hf, new_hb = _lru(xb, gb, *lru_prm, None, BATCH, SEQ, 0, CTX_SEQ_PER_STEP)
    yb_smp, _, _ = _lru(xb, gb, *lru_prm, (state_lru_fwd, state_lru_bwd),
                        DEC_BATCH, DEC_SEQ, N_CTX_TOK // DEC_SEQ, 1)
    (x,) = _ffn((x,), mod0_rest, norm_g[0, 2], *ffn2, 0, 3, mixer_out=(o_ctx, o_smp, yb_ctx, yb_smp, w_out_ab))

    (x,) = _ffn((x,), mod1, norm_g[1, 0], *ffn1, 1, 0)
    x = _fourier(x, mod1, norm_g[1, 1], w_out_c[0], BATCH, SEQ, 0, 0, CTX_SEQ_PER_STEP)
    x = _fourier(x, mod1, norm_g[1, 1], w_out_c[0], DEC_BATCH, DEC_SEQ, N_CTX_TOK // DEC_SEQ, 1, 1)
    y_prompt, y_sample = _ffn((x,), mod1, norm_g[1, 2], *ffn2, 1, 6, split_out=True)

    return (y_prompt.reshape(BATCH, SEQ, D_MODEL), y_sample.reshape(DEC_BATCH, DEC_SEQ, D_MODEL),
            _cache_layout(new_k), _cache_layout(new_v),
            new_hf, new_hb)
```

```python
import functools

import numpy as np
import jax
import jax.numpy as jnp
from jax import lax
from jax.experimental import pallas as pl
from jax.experimental.pallas import tpu as pltpu

F32 = jnp.float32
BF16 = jnp.bfloat16

D_MODEL = 1024
BATCH = 16
SEQ = 256
DEPTH = 2
DEC_BATCH = 2
DEC_SEQ = 1024
PAST_LEN = 256
GRID_W = 64
HEAD_DIM = 64
NA_WIDTH = 512
NA_HEADS = 8
WIN_H = 8
WIN_W = 16
LRU_WIDTH = 512
LRU_BLOCKS = 8
LRU_BLOCK = 64
LRU_C = 8.0
LRU_SUB = 256
CONV_W = 4
FOURIER_GROUPS = 4
GROUP_W = D_MODEL // FOURIER_GROUPS
D_FF = 2816
N_MOD = 9
IN_WIDTH = 3 * NA_WIDTH + 2 * LRU_WIDTH
EPS = 1e-6

N_CTX_TOK = BATCH * SEQ
N_SMP_TOK = DEC_BATCH * DEC_SEQ
N_TOK = N_CTX_TOK + N_SMP_TOK
MOD_ROWS = 8
MOD_WIDTH = N_MOD * D_MODEL
ROWS = DEC_SEQ // GRID_W
KH = min(WIN_H, ROWS)

TOKEN_TILE = 512
N_CTX_TILES = N_CTX_TOK // TOKEN_TILE
CTX_SEQ_PER_STEP = 2
CTX_ATTN_SEQ_PER_STEP = 4
FFN_TILE = 512
FF_TILE = 256
FF_CHUNKS = D_FF // FF_TILE
FF_STAGE_SLOTS = 2
SUBLANES = 8
LANES = 128
VMEM_LIMIT = 60 * 1024 * 1024

NA_Q_ROWS = 4
NA_GROUPS = ROWS // NA_Q_ROWS
NA_K_ROWS = 12
NA_Q = NA_Q_ROWS * GRID_W
NA_K = NA_K_ROWS * GRID_W
N_DR = 2 * WIN_H - 1
N_DC = 2 * WIN_W - 1
N_DR_PAIRS = N_DR + 1


def _cparams(n_axes):
    return pltpu.CompilerParams(
        dimension_semantics=("arbitrary",) * n_axes, vmem_limit_bytes=VMEM_LIMIT)


def _resident(block_shape, index_map):
    return pl.BlockSpec(block_shape, index_map, pipeline_mode=pl.Buffered(1))


def _mod_spec(mod):
    return _resident(mod.shape, lambda i: (0, 0))


def _mod_row_of_tile(i, tile=TOKEN_TILE):
    n_ctx_tiles = N_CTX_TOK // tile
    tiles_per_seq = DEC_SEQ // tile
    return jnp.where(i < n_ctx_tiles, 0, 1 + (i - n_ctx_tiles) // tiles_per_seq)


def _mod_vec(mod_ref, row, k):
    return mod_ref[pl.ds(row, 1), k * D_MODEL:(k + 1) * D_MODEL]


def _norm_mod(x, g, shift, scale):
    ms = jnp.mean(x * x, axis=-1, keepdims=True)
    return (x * lax.rsqrt(ms + EPS)) * (g * (1.0 + scale)) + shift


def _adaln_slab(cctx_ref, c_ref, w_ref, b_ref, cond_ref, layer):
    cond_ref[...] = jnp.zeros_like(cond_ref)
    cond_ref[0:1, :] = cctx_ref[...]
    cond_ref[1:1 + DEC_BATCH, :] = c_ref[...]
    cond = cond_ref[...]
    s = (cond * jax.nn.sigmoid(cond)).astype(BF16)
    return jnp.dot(s, w_ref[...].astype(BF16), preferred_element_type=F32) + b_ref[layer:layer + 1, :]


def _adaln_specs(job, n_steps, step=lambda i: i):
    layer, col0, n_cols = job[4:]
    slab = n_cols // n_steps
    assert n_cols % n_steps == 0 and slab % LANES == 0 and col0 % slab == 0
    blk0 = col0 // slab
    return ([pl.BlockSpec((1, D_MODEL), lambda *ids: (0, 0)),
             pl.BlockSpec((DEC_BATCH, D_MODEL), lambda *ids: (0, 0)),
             pl.BlockSpec((None, D_MODEL, slab), lambda *ids: (layer, 0, blk0 + step(*ids))),
             pl.BlockSpec((DEPTH, slab), lambda *ids: (0, blk0 + step(*ids)))],
            pl.BlockSpec((MOD_ROWS, slab), lambda *ids: (0, step(*ids))),
            jax.ShapeDtypeStruct((MOD_ROWS, n_cols), F32))


def _adaln_kernel(cctx_ref, c_ref, w_ref, b_ref, o_ref, cond_ref, *, layer):
    o_ref[...] = _adaln_slab(cctx_ref, c_ref, w_ref, b_ref, cond_ref, layer)


def _adaln(job, n_steps):
    in_specs, out_spec, out_shape = _adaln_specs(job, n_steps)
    return pl.pallas_call(
        functools.partial(_adaln_kernel, layer=job[4]),
        grid=(n_steps,),
        in_specs=in_specs,
        out_specs=out_spec,
        out_shape=out_shape,
        scratch_shapes=[pltpu.VMEM((MOD_ROWS, D_MODEL), F32)],
        compiler_params=_cparams(1),
        name="adaln",
    )(*job[:4])


def _ffn_weight_copy(w_hbm, stage_ref, sem_ref, layer, j, ff_axis):
    ff = pl.ds(j * FF_TILE, FF_TILE)
    src = w_hbm.at[layer, :, ff] if ff_axis == 1 else w_hbm.at[layer, ff, :]
    slot = j % FF_STAGE_SLOTS
    return pltpu.make_async_copy(src, stage_ref.at[slot], sem_ref.at[slot])


def _mixer_out_copy(w_hbm, stage_ref, sem_ref, j):
    rows = stage_ref.shape[1]
    slot = j % stage_ref.shape[0]
    return pltpu.make_async_copy(w_hbm.at[0, pl.ds(j * rows, rows), :], stage_ref.at[slot], sem_ref.at[slot])


def _ffn_kernel(*refs, layer, mod_base, split_in, split_out, mixer_out, adaln_next):
    refs = list(refs)
    take = lambda n: [refs.pop(0) for _ in range(n)]
    x_refs = take(2 if split_in else 1)
    mix_refs = take(4) if mixer_out else None
    mod_ref, g_ref = take(2)
    wo_hbm = take(1)[0] if mixer_out else None
    ada_refs = take(4) if adaln_next is not None else None
    wg_hbm, wu_hbm, wd_hbm = take(3)
    o_refs = take(2 if split_out else 1)
    modn_ref = take(1)[0] if adaln_next is not None else None
    wg_bf, wu_bf, wd_bf, stg_g, stg_u, stg_d, sem_g, sem_u, sem_d = take(9)
    cond_scr = take(1)[0] if adaln_next is not None else None
    streams = ((wg_hbm, stg_g, sem_g, wg_bf, 1), (wu_hbm, stg_u, sem_u, wu_bf, 1),
               (wd_hbm, stg_d, sem_d, wd_bf, 0))

    i = pl.program_id(0)
    is_ctx = i < N_CTX_TOK // FFN_TILE
    if split_in:
        x = jnp.where(is_ctx, x_refs[0][...], x_refs[1][...])
    else:
        x = x_refs[0][...]
    row = _mod_row_of_tile(i, FFN_TILE)

    if mixer_out:
        wo_bf, stg_o, sem_o = take(3)
        rows = stg_o.shape[1]
        n_chunks = D_MODEL // rows

        @pl.when(i == 0)
        def _():
            for j in range(stg_o.shape[0]):
                _mixer_out_copy(wo_hbm, stg_o, sem_o, j).start()
            for j in range(n_chunks):
                _mixer_out_copy(wo_hbm, stg_o, sem_o, j).wait()
                wo_bf[j * rows:(j + 1) * rows, :] = stg_o[j % stg_o.shape[0]].astype(BF16)
                if j + stg_o.shape[0] < n_chunks:
                    _mixer_out_copy(wo_hbm, stg_o, sem_o, j + stg_o.shape[0]).start()

        oc_ref, os_ref, yc_ref, ys_ref = mix_refs
        cat = jnp.concatenate([jnp.where(is_ctx, oc_ref[...], os_ref[...]),
                               jnp.where(is_ctx, yc_ref[...], ys_ref[...])], axis=1)
        x = x + _mod_vec(mod_ref, row, mod_base - 1) * jnp.dot(cat, wo_bf[...], preferred_element_type=F32)

    h = _norm_mod(x, g_ref[...], _mod_vec(mod_ref, row, mod_base),
                  _mod_vec(mod_ref, row, mod_base + 1)).astype(BF16)

    def start_chunk(j):
        for w_hbm, stg, sem, _, ff_axis in streams:
            _ffn_weight_copy(w_hbm, stg, sem, layer, j, ff_axis).start()

    def finish_chunk(j):
        for w_hbm, stg, sem, w_bf, ff_axis in streams:
            _ffn_weight_copy(w_hbm, stg, sem, layer, j, ff_axis).wait()
            w_bf[j] = stg[j % FF_STAGE_SLOTS].astype(BF16)

    def run(stream_weights):
        if stream_weights:
            for j in range(FF_STAGE_SLOTS):
                start_chunk(j)
        acc = None
        for j in range(FF_CHUNKS):
            if stream_weights:
                finish_chunk(j)
                if j + FF_STAGE_SLOTS < FF_CHUNKS:
                    start_chunk(j + FF_STAGE_SLOTS)
            a = jnp.dot(h, wg_bf[j], preferred_element_type=F32)
            b = jnp.dot(h, wu_bf[j], preferred_element_type=F32)
            if adaln_next is not None and j == 1:
                modn_ref[...] = _adaln_slab(*ada_refs, cond_scr, adaln_next)
            act = (a * jax.nn.sigmoid(a) * b).astype(BF16)
            y = jnp.dot(act, wd_bf[j], preferred_element_type=F32)
            acc = y if acc is None else acc + y
        res = x + 0.5 * _mod_vec(mod_ref, row, mod_base + 2) * acc
        if split_out:
            @pl.when(is_ctx)
            def _():
                o_refs[0][...] = res

            @pl.when(jnp.logical_not(is_ctx))
            def _():
                o_refs[1][...] = res
        else:
            o_refs[0][...] = res

    @pl.when(i == 0)
    def _():
        run(True)

    @pl.when(i > 0)
    def _():
        run(False)


def _ffn(xs, mod, g, wg, wu, wd, layer, mod_base, split_out=False, mixer_out=None, adaln_next=None):
    tm = FFN_TILE
    n_ctx_tiles = N_CTX_TOK // tm
    split_in = len(xs) == 2

    def tiles(width):
        return (pl.BlockSpec((tm, width), lambda i: (i, 0)),
                pl.BlockSpec((tm, width), lambda i: (jnp.minimum(i, n_ctx_tiles - 1), 0)),
                pl.BlockSpec((tm, width), lambda i: (jnp.maximum(i - n_ctx_tiles, 0), 0)))

    tok, ctx_tok, smp_tok = tiles(D_MODEL)
    full = jax.ShapeDtypeStruct((N_TOK, D_MODEL), F32)
    pair = [jax.ShapeDtypeStruct((N_CTX_TOK, D_MODEL), F32), jax.ShapeDtypeStruct((N_SMP_TOK, D_MODEL), F32)]
    hbm = pl.BlockSpec(memory_space=pl.ANY)
    in_specs = [ctx_tok, smp_tok] if split_in else [tok]
    operands = list(xs)
    scratch = [
        pltpu.VMEM((FF_CHUNKS, D_MODEL, FF_TILE), BF16),
        pltpu.VMEM((FF_CHUNKS, D_MODEL, FF_TILE), BF16),
        pltpu.VMEM((FF_CHUNKS, FF_TILE, D_MODEL), BF16),
        pltpu.VMEM((FF_STAGE_SLOTS, D_MODEL, FF_TILE), F32),
        pltpu.VMEM((FF_STAGE_SLOTS, D_MODEL, FF_TILE), F32),
        pltpu.VMEM((FF_STAGE_SLOTS, FF_TILE, D_MODEL), F32),
        pltpu.SemaphoreType.DMA((FF_STAGE_SLOTS,)),
        pltpu.SemaphoreType.DMA((FF_STAGE_SLOTS,)),
        pltpu.SemaphoreType.DMA((FF_STAGE_SLOTS,)),
    ]
    if mixer_out is not None:
        _, ctx_half, smp_half = tiles(NA_WIDTH)
        in_specs += [ctx_half, smp_half, ctx_half, smp_half]
        operands += list(mixer_out[:4])
    in_specs += [_mod_spec(mod), pl.BlockSpec((1, D_MODEL), lambda i: (0, 0))]
    operands += [mod, g.reshape(1, D_MODEL)]
    out_specs = [ctx_tok, smp_tok] if split_out else [tok]
    out_shape = pair if split_out else [full]
    if adaln_next is not None:
        ada_in, ada_out, ada_shape = _adaln_specs(adaln_next, N_TOK // tm)
        in_specs += ada_in
        operands += list(adaln_next[:4])
        out_specs.append(ada_out)
        out_shape.append(ada_shape)
        scratch.append(pltpu.VMEM((MOD_ROWS, D_MODEL), F32))
    if mixer_out is not None:
        in_specs.insert(len(in_specs) - (4 if adaln_next is not None else 0), hbm)
        operands.insert(len(operands) - (4 if adaln_next is not None else 0), mixer_out[4])
        scratch += [
            pltpu.VMEM((D_MODEL, D_MODEL), BF16),
            pltpu.VMEM((FF_STAGE_SLOTS, FF_TILE, D_MODEL), F32),
            pltpu.SemaphoreType.DMA((FF_STAGE_SLOTS,)),
        ]
    return pl.pallas_call(
        functools.partial(_ffn_kernel, layer=layer, mod_base=mod_base, split_in=split_in,
                          split_out=split_out, mixer_out=mixer_out is not None,
                          adaln_next=None if adaln_next is None else adaln_next[4]),
        grid=(N_TOK // tm,),
        in_specs=in_specs + [hbm, hbm, hbm],
        out_specs=out_specs,
        out_shape=out_shape,
        scratch_shapes=scratch,
        compiler_params=_cparams(1),
        name="ffn",
    )(*operands, wg, wu, wd)


def _head_rms_norm(z, g, ones_bd):
    z2 = z * z
    hi = z2.astype(BF16)
    lo = (z2 - hi.astype(F32)).astype(BF16)
    n = ones_bd.shape[0]
    parts = []
    for c in range(z.shape[1] // n):
        sl = slice(c * n, (c + 1) * n)
        parts.append(jnp.dot(hi[:, sl], ones_bd, preferred_element_type=F32)
                     + jnp.dot(lo[:, sl], ones_bd, preferred_element_type=F32))
    ss = jnp.concatenate(parts, axis=1)
    return z * lax.rsqrt(ss * (1.0 / HEAD_DIM) + EPS) * g


def _proj_kernel(*refs, mod_k0, adaln_layer):
    refs = list(refs)
    x_ref, mod_ref, g_ref, w_ref, qg_ref, kg_ref, ones_ref = refs[:7]
    ada_refs = refs[7:11] if adaln_layer is not None else None
    n_in = 7 if adaln_layer is None else 11
    q_ref, k_ref, v_ref, xb_ref, gb_ref, kout_ref, vout_ref = refs[n_in:n_in + 7]
    rest = refs[n_in + 7:]
    modn_ref = rest.pop(0) if adaln_layer is not None else None
    w_bf_ref = rest.pop(0)
    i = pl.program_id(0)

    @pl.when(i == 0)
    def _():
        w_bf_ref[...] = w_ref[...].astype(BF16)

    x = x_ref[...]
    row = _mod_row_of_tile(i)
    h = _norm_mod(x, g_ref[...], _mod_vec(mod_ref, row, mod_k0),
                  _mod_vec(mod_ref, row, mod_k0 + 1)).astype(BF16)

    def proj(part):
        return jnp.dot(h, w_bf_ref[:, part * NA_WIDTH:(part + 1) * NA_WIDTH], preferred_element_type=F32)

    ones_bd = ones_ref[...]
    q_raw = proj(0)
    k_raw = proj(1)
    q = _head_rms_norm(q_raw, jnp.tile(qg_ref[...], (1, NA_HEADS)), ones_bd) * (HEAD_DIM ** -0.5)
    q_ref[...] = q.astype(BF16)
    v = proj(2)
    if adaln_layer is not None:
        modn_ref[...] = _adaln_slab(*ada_refs, rest.pop(0), adaln_layer)
    k = _head_rms_norm(k_raw, jnp.tile(kg_ref[...], (1, NA_HEADS)), ones_bd)
    k_ref[...] = k.astype(BF16)
    xb = proj(3)
    v_ref[...] = v.astype(BF16)
    gb = proj(4)
    xb_ref[...] = xb
    gb_ref[...] = gb

    kt = [k[b * SEQ:(b + 1) * SEQ, :].T for b in range(TOKEN_TILE // SEQ)]
    vt = [v[b * SEQ:(b + 1) * SEQ, :].T for b in range(TOKEN_TILE // SEQ)]

    @pl.when(i < N_CTX_TILES)
    def _():
        for b in range(TOKEN_TILE // SEQ):
            kout_ref[b] = kt[b]
            vout_ref[b] = vt[b]


def _proj(x, mod, mod_k0, g, w_in, q_g, k_g, adaln_next=None):
    tm = TOKEN_TILE
    head = np.arange(2 * LANES) // HEAD_DIM
    ones_bd = jnp.asarray((head[:, None] == head[None, :]).astype(np.float32), dtype=BF16)
    tok = lambda i: (i, 0)
    const = lambda i: (0, 0)
    act_f32 = jax.ShapeDtypeStruct((N_TOK, NA_WIDTH), F32)
    act_bf16 = jax.ShapeDtypeStruct((N_TOK, NA_WIDTH), BF16)
    cache = jax.ShapeDtypeStruct((BATCH, NA_WIDTH, SEQ), F32)
    cache_spec = pl.BlockSpec((tm // SEQ, NA_WIDTH, SEQ), lambda i: (jnp.minimum(i, N_CTX_TILES - 1), 0, 0))
    in_specs = [
        pl.BlockSpec((tm, D_MODEL), tok),
        _mod_spec(mod),
        pl.BlockSpec((1, D_MODEL), const),
        _resident((D_MODEL, IN_WIDTH), const),
        pl.BlockSpec((1, HEAD_DIM), const),
        pl.BlockSpec((1, HEAD_DIM), const),
        _resident((2 * LANES, 2 * LANES), const),
    ]
    operands = [x, mod, g.reshape(1, D_MODEL), w_in, q_g.reshape(1, HEAD_DIM), k_g.reshape(1, HEAD_DIM), ones_bd]
    out_specs = [pl.BlockSpec((tm, NA_WIDTH), tok)] * 5 + [cache_spec, cache_spec]
    out_shape = [act_bf16, act_bf16, act_bf16, act_f32, act_f32, cache, cache]
    scratch = [pltpu.VMEM((D_MODEL, IN_WIDTH), BF16)]
    if adaln_next is not None:
        ada_in, ada_out, ada_shape = _adaln_specs(adaln_next, N_TOK // tm)
        in_specs += ada_in
        operands += list(adaln_next[:4])
        out_specs.append(ada_out)
        out_shape.append(ada_shape)
        scratch.append(pltpu.VMEM((MOD_ROWS, D_MODEL), F32))
    return pl.pallas_call(
        functools.partial(_proj_kernel, mod_k0=mod_k0,
                          adaln_layer=None if adaln_next is None else adaln_next[4]),
        grid=(N_TOK // tm,),
        in_specs=in_specs,
        out_specs=out_specs,
        out_shape=out_shape,
        scratch_shapes=scratch,
        compiler_params=_cparams(1),
        name="mixer_in_proj",
    )(*operands)


def _head_masks():
    lane = lax.broadcasted_iota(jnp.int32, (1, 2 * HEAD_DIM), 1)
    return [lane < HEAD_DIM, lane >= HEAD_DIM]


def _ctx_attn_kernel(q_ref, k_ref, v_ref, o_ref):
    masks = _head_masks()
    units = [(b, h) for b in range(q_ref.shape[0] // SEQ) for h in range(NA_HEADS)]

    def where(b, h):
        return slice(b * SEQ, (b + 1) * SEQ), slice(2 * HEAD_DIM * (h // 2), 2 * HEAD_DIM * (h // 2 + 1))

    def scores(b, h):
        rows, sl = where(b, h)
        q2 = q_ref[rows, sl]
        qm = jnp.where(masks[h % 2], q2, jnp.zeros_like(q2))
        return jnp.dot(qm, k_ref[rows, sl].T, preferred_element_type=F32)

    def attend(b, h, s):
        rows, sl = where(b, h)
        pe = jnp.exp(s - jnp.max(s, axis=-1, keepdims=True))
        den = jnp.sum(pe, axis=-1, keepdims=True)
        return jnp.dot(pe.astype(BF16), v_ref[rows, sl], preferred_element_type=F32) / den

    pending = scores(*units[0])
    out = None
    for n, (b, h) in enumerate(units):
        current = pending
        if n + 1 < len(units):
            pending = scores(*units[n + 1])
        o = attend(b, h, current)
        if h % 2 == 0:
            out = o
        else:
            rows, sl = where(b, h)
            o_ref[rows, sl] = jnp.where(masks[1], o, out).astype(BF16)


def _ctx_attn(q, k, v):
    blk = pl.BlockSpec((CTX_ATTN_SEQ_PER_STEP * SEQ, NA_WIDTH), lambda b: (b, 0))
    return pl.pallas_call(
        _ctx_attn_kernel,
        grid=(BATCH // CTX_ATTN_SEQ_PER_STEP,),
        in_specs=[blk, blk, blk],
        out_specs=blk,
        out_shape=jax.ShapeDtypeStruct((N_CTX_TOK, NA_WIDTH), BF16),
        compiler_params=_cparams(1),
        name="ctx_attention",
    )(q, k, v)


def _na_build_bias_table(rpb_ref, table_ref):
    qc = lax.broadcasted_iota(jnp.int32, (GRID_W, LANES), 0)
    lane = lax.broadcasted_iota(jnp.int32, (GRID_W, LANES), 1)
    kc = lane % GRID_W
    col_start = jnp.clip(qc - WIN_W // 2, 0, GRID_W - WIN_W)
    col_in = (kc >= col_start) & (kc < col_start + WIN_W)
    neg = jnp.full((GRID_W, LANES), -jnp.inf, F32)

    def toeplitz(h, dr, lane0):
        if dr < 0 or dr >= N_DR:
            return neg
        row = jnp.pad(rpb_ref[dr, h:h + 1, :], ((0, 0), (0, LANES - N_DC)))
        w = jnp.broadcast_to(row, (GRID_W, LANES))
        return pltpu.roll(w, (lane0 - (WIN_W - 1)) % LANES, 1, stride=1, stride_axis=0)

    for h in range(NA_HEADS):
        for i in range(N_DR_PAIRS):
            t = jnp.where(lane < GRID_W, toeplitz(h, i - 1, 0), toeplitz(h, i, GRID_W))
            table_ref[h, i] = jnp.where(col_in, t, neg)


def _na_kernel(*refs, adaln_layer):
    q_ref, k_ref, v_ref, kc_ref, vc_ref, rpb_ref = refs[:6]
    if adaln_layer is None:
        o_ref, table_ref = refs[6:]
    else:
        ada_refs = refs[6:10]
        o_ref, modn_ref, table_ref, cond_scr = refs[10:]
    b = pl.program_id(0)
    g = pl.program_id(1)

    @pl.when((b == 0) & (g == 0))
    def _():
        _na_build_bias_table(rpb_ref, table_ref)

    win_row0 = jnp.where(g < NA_GROUPS // 2, 0, ROWS - NA_K_ROWS)
    start = pl.multiple_of(win_row0 * GRID_W, GRID_W)
    q_row = g * NA_Q_ROWS + lax.broadcasted_iota(jnp.int32, (NA_Q, 1), 0) // GRID_W
    k_row = win_row0 + lax.broadcasted_iota(jnp.int32, (1, NA_K), 1) // GRID_W
    row_start = jnp.clip(q_row - KH // 2, 0, ROWS - KH)
    row_in = (k_row >= row_start) & (k_row < row_start + KH)
    masks = _head_masks()

    def pair_slab(p):
        return slice(2 * HEAD_DIM * p, 2 * HEAD_DIM * (p + 1))

    def scores(head):
        p, e = divmod(head, 2)
        sl = pair_slab(p)
        q2 = q_ref[:, sl]
        klt = k_ref[pl.ds(start, NA_K), sl].T
        kct = kc_ref[0, sl, :].astype(BF16)
        bias_rows = []
        for a in range(NA_Q_ROWS):
            tiles = []
            for m in range(NA_K_ROWS // 2):
                dr = win_row0 + 2 * m - (g * NA_Q_ROWS + a) + (WIN_H - 1)
                tiles.append(table_ref[head, jnp.clip(dr + 1, 0, N_DR_PAIRS - 1)])
            bias_rows.append(jnp.concatenate(tiles, axis=1))
        bias = jnp.concatenate(bias_rows, axis=0)
        qm = jnp.where(masks[e], q2, jnp.zeros_like(q2))
        s_loc = jnp.where(row_in, jnp.dot(qm, klt, preferred_element_type=F32) + bias, -jnp.inf)
        s_ctx = jnp.dot(qm, kct, preferred_element_type=F32)
        return s_loc, s_ctx

    def attend(head, s_loc, s_ctx):
        sl = pair_slab(head // 2)
        vl = v_ref[pl.ds(start, NA_K), sl]
        vct = vc_ref[0, sl, :].astype(BF16)
        m_max = jnp.maximum(jnp.max(s_loc, axis=-1, keepdims=True),
                            jnp.max(s_ctx, axis=-1, keepdims=True))
        p_loc = jnp.exp(s_loc - m_max)
        p_ctx = jnp.exp(s_ctx - m_max)
        den = jnp.sum(p_loc, axis=-1, keepdims=True) + jnp.sum(p_ctx, axis=-1, keepdims=True)
        return (jnp.dot(p_loc.astype(BF16), vl, preferred_element_type=F32)
                + lax.dot_general(p_ctx.astype(BF16), vct, (((1,), (1,)), ((), ())),
                                  preferred_element_type=F32)) / den

    pending = scores(0)
    out = None
    for head in range(NA_HEADS):
        current = pending
        if head + 1 < NA_HEADS:
            pending = scores(head + 1)
        o = attend(head, *current)
        if adaln_layer is not None and head == 0:
            modn_ref[...] = _adaln_slab(*ada_refs, cond_scr, adaln_layer)
        if head % 2 == 0:
            out = o
        else:
            o_ref[:, pair_slab(head // 2)] = jnp.where(masks[1], o, out).astype(BF16)


def _na_attn(q, k, v, k_ctx, v_ctx, rpb_e, adaln_next=None):
    smp_blk0 = N_CTX_TOK // DEC_SEQ
    q_blk0 = N_CTX_TOK // NA_Q
    kv = pl.BlockSpec((DEC_SEQ, NA_WIDTH), lambda b, g: (smp_blk0 + b, 0))
    ctx = pl.BlockSpec((1, NA_WIDTH, PAST_LEN), lambda b, g: (b, 0, 0))
    in_specs = [
        pl.BlockSpec((NA_Q, NA_WIDTH), lambda b, g: (q_blk0 + b * NA_GROUPS + g, 0)),
        kv, kv, ctx, ctx,
        pl.BlockSpec((N_DR, NA_HEADS, N_DC), lambda b, g: (0, 0, 0)),
    ]
    operands = [q, k, v, k_ctx, v_ctx, jnp.transpose(rpb_e, (1, 0, 2))]
    out_specs = [pl.BlockSpec((NA_Q, NA_WIDTH), lambda b, g: (b * NA_GROUPS + g, 0))]
    out_shape = [jax.ShapeDtypeStruct((N_SMP_TOK, NA_WIDTH), BF16)]
    scratch = [pltpu.VMEM((NA_HEADS, N_DR_PAIRS, GRID_W, LANES), F32)]
    if adaln_next is not None:
        ada_in, ada_out, ada_shape = _adaln_specs(adaln_next, DEC_BATCH * NA_GROUPS,
                                                  lambda b, g: b * NA_GROUPS + g)
        in_specs += ada_in
        operands += list(adaln_next[:4])
        out_specs.append(ada_out)
        out_shape.append(ada_shape)
        scratch.append(pltpu.VMEM((MOD_ROWS, D_MODEL), F32))
    return pl.pallas_call(
        functools.partial(_na_kernel, adaln_layer=None if adaln_next is None else adaln_next[4]),
        grid=(DEC_BATCH, NA_GROUPS),
        in_specs=in_specs,
        out_specs=out_specs,
        out_shape=out_shape,
        scratch_shapes=scratch,
        compiler_params=_cparams(2),
        name="neighbourhood_attention",
    )(*operands)


def _gelu_tanh(x):
    c0 = float(np.sqrt(2.0 / np.pi))
    inner = x * (c0 + (c0 * 0.044715) * (x * x))
    return (0.5 * x) * (1.0 + jnp.tanh(inner))


def _lru_build_gate_weights(wr_ref, wi_ref, w_scr):
    blocks_per_group = LRU_SUB // LRU_BLOCK
    w_scr[...] = jnp.zeros_like(w_scr)
    for d in range(2):
        for kind, w_ref in enumerate((wr_ref, wi_ref)):
            col0 = (2 * d + kind) * LRU_SUB
            for blk in range(LRU_BLOCKS):
                c, n = divmod(blk, blocks_per_group)
                r0 = n * LRU_BLOCK
                w_scr[c, r0:r0 + LRU_BLOCK, col0 + r0:col0 + r0 + LRU_BLOCK] = (
                    0.5 * w_ref[d * LRU_BLOCKS + blk]).astype(BF16)


def _lru_kernel(*refs, zero_state, t_len):
    refs = list(refs)
    xb_ref, gb_ref, cw_ref, cb_ref, wr_ref, wi_ref, br_ref, bi_ref, lam_ref = refs[:9]
    h0f_ref, h0b_ref = (None, None) if zero_state else refs[9:11]
    y_ref, hlf_ref, hlb_ref, af_ref, uf_ref, ab_ref, ub_ref, w_scr = refs[-8:]
    width = xb_ref.shape[1]
    n_seq = xb_ref.shape[0] // t_len
    n_blk = t_len // SUBLANES
    row = lax.broadcasted_iota(jnp.int32, (t_len, 1), 0)
    in_block = lax.broadcasted_iota(jnp.int32, (1, SUBLANES, 1), 1)

    @pl.when(pl.program_id(0) == 0)
    def _():
        _lru_build_gate_weights(wr_ref, wi_ref, w_scr)

    def shifted(z, s):
        rolled = pltpu.roll(z, (-s) % t_len, axis=0)
        ok = (row + s >= 0) & (row + s < t_len)
        return jnp.where(ok, rolled, 0.0)

    left = (CONV_W - 1) // 2
    for q, c in [(q, c) for q in range(n_seq) for c in range(width // LRU_SUB)]:
        rows = slice(q * t_len, (q + 1) * t_len)
        cs = slice(c * LRU_SUB, (c + 1) * LRU_SUB)
        x = xb_ref[rows, cs]
        xc = cb_ref[:, cs]
        for j in range(CONV_W):
            tap = x if j == left else shifted(x, j - left)
            xc = xc + tap * cw_ref[j:j + 1, cs]
        half_gates = jnp.dot(xc.astype(BF16), w_scr[c], preferred_element_type=F32)
        half_xc = 0.5 * xc
        for d, (a_ref, u_ref) in enumerate(((af_ref, uf_ref), (ab_ref, ub_ref))):
            t_r = jnp.tanh(half_gates[:, (2 * d) * LRU_SUB:(2 * d + 1) * LRU_SUB] + 0.5 * br_ref[d:d + 1, cs])
            t_i = jnp.tanh(half_gates[:, (2 * d + 1) * LRU_SUB:(2 * d + 2) * LRU_SUB] + 0.5 * bi_ref[d:d + 1, cs])
            lam = lam_ref[d:d + 1, cs]
            log_sig = jnp.minimum(lam, 0.0) - jnp.log1p(jnp.exp(-jnp.abs(lam)))
            half_c_log_sig = (0.5 * LRU_C) * log_sig
            log_a = t_r * half_c_log_sig + half_c_log_sig
            a = jnp.exp(log_a)
            var = -jnp.tanh(log_a) * (a * a + 1.0)
            u = jnp.where(var > 0.0, var * lax.rsqrt(var), 0.0) * ((t_i + 1.0) * half_xc)
            a = a.reshape(n_blk, SUBLANES, LRU_SUB)
            u = u.reshape(n_blk, SUBLANES, LRU_SUB)
            step = 1
            while step < SUBLANES:
                if d == 0:
                    ok, shift = in_block >= step, step
                else:
                    ok, shift = in_block < SUBLANES - step, SUBLANES - step
                a_prev = jnp.where(ok, pltpu.roll(a, shift, axis=1), 1.0)
                u_prev = jnp.where(ok, pltpu.roll(u, shift, axis=1), 0.0)
                u = u + a * u_prev
                a = a * a_prev
                step *= 2
            a_ref[rows, cs] = a.reshape(t_len, LRU_SUB)
            u_ref[rows, cs] = u.reshape(t_len, LRU_SUB)

    def body(i, carry):
        new = []
        for q, (cf, cb) in enumerate(carry):
            f0 = pl.multiple_of(q * t_len + i * SUBLANES, SUBLANES)
            b0 = pl.multiple_of(q * t_len + (n_blk - 1 - i) * SUBLANES, SUBLANES)
            hf = uf_ref[pl.ds(f0, SUBLANES), :] + af_ref[pl.ds(f0, SUBLANES), :] * cf
            hb = ub_ref[pl.ds(b0, SUBLANES), :] + ab_ref[pl.ds(b0, SUBLANES), :] * cb
            uf_ref[pl.ds(f0, SUBLANES), :] = hf
            ub_ref[pl.ds(b0, SUBLANES), :] = hb
            new.append((jnp.broadcast_to(hf[SUBLANES - 1:SUBLANES, :], (SUBLANES, width)),
                        jnp.broadcast_to(hb[0:1, :], (SUBLANES, width))))
        return tuple(new)

    if zero_state:
        init = tuple((jnp.zeros((SUBLANES, width), F32),) * 2 for _ in range(n_seq))
    else:
        init = tuple((jnp.broadcast_to(h0f_ref[q], (SUBLANES, width)),
                      jnp.broadcast_to(h0b_ref[q], (SUBLANES, width))) for q in range(n_seq))
    last = lax.fori_loop(0, n_blk, body, init)
    for q, (cf, cb) in enumerate(last):
        hlf_ref[q] = cf[0:1, :]
        hlb_ref[q] = cb[0:1, :]
    y_ref[...] = ((uf_ref[...] + ub_ref[...]) * _gelu_tanh(gb_ref[...])).astype(BF16)


def _lru(xb, gb, conv_w, conv_b, w_r, b_r, w_i, b_i, lam, h0, n_seq, t_len, tok_blk0, seq_per_step):
    rows = seq_per_step * t_len
    const2 = lambda s: (0, 0)
    const3 = lambda s: (0, 0, 0)
    blocks = pl.BlockSpec((2 * LRU_BLOCKS, LRU_BLOCK, LRU_BLOCK), const3)
    per_dir = pl.BlockSpec((2, LRU_WIDTH), const2)
    state = pl.BlockSpec((seq_per_step, 1, LRU_WIDTH), lambda s: (s, 0, 0))
    state_shape = jax.ShapeDtypeStruct((n_seq, 1, LRU_WIDTH), F32)
    return pl.pallas_call(
        functools.partial(_lru_kernel, zero_state=h0 is None, t_len=t_len),
        grid=(n_seq // seq_per_step,),
        in_specs=[
            pl.BlockSpec((rows, LRU_WIDTH), lambda s: (tok_blk0 + s, 0)),
            pl.BlockSpec((rows, LRU_WIDTH), lambda s: (tok_blk0 + s, 0)),
            pl.BlockSpec((CONV_W, LRU_WIDTH), const2),
            pl.BlockSpec((1, LRU_WIDTH), const2),
            blocks, blocks, per_dir, per_dir, per_dir,
        ] + ([] if h0 is None else [state, state]),
        out_specs=[pl.BlockSpec((rows, LRU_WIDTH), lambda s: (s, 0)), state, state],
        out_shape=[jax.ShapeDtypeStruct((n_seq * t_len, LRU_WIDTH), BF16), state_shape, state_shape],
        scratch_shapes=[pltpu.VMEM((rows, LRU_WIDTH), F32)] * 4
        + [pltpu.VMEM((LRU_WIDTH // LRU_SUB, LRU_SUB, 4 * LRU_SUB), BF16)],
        compiler_params=_cparams(1),
        name="rglru",
    )(xb, gb, conv_w, conv_b.reshape(1, LRU_WIDTH),
      w_r.reshape(2 * LRU_BLOCKS, LRU_BLOCK, LRU_BLOCK), w_i.reshape(2 * LRU_BLOCKS, LRU_BLOCK, LRU_BLOCK),
      b_r, b_i, lam, *(() if h0 is None else h0))


def _fourier_kernel(x_ref, mod_ref, g_ref, cs_ref, ct_ref, w_ref, o_ref, w_bf_ref, *, mod_row0, t_len):
    @pl.when(pl.program_id(0) == 0)
    def _():
        w_bf_ref[...] = w_ref[...].astype(BF16)

    x = x_ref[...]
    row = mod_row0 + pl.program_id(0) if mod_row0 else 0
    h = _norm_mod(x, g_ref[...], _mod_vec(mod_ref, row, 3), _mod_vec(mod_ref, row, 4)).astype(BF16)
    cos_parts, sin_parts = [], []
    for g in range(FOURIER_GROUPS):
        ab = jnp.dot(h[:, g * GROUP_W:(g + 1) * GROUP_W], cs_ref[...], preferred_element_type=F32)
        cos_parts.append(ab[:, :GROUP_W])
        sin_parts.append(ab[:, GROUP_W:])
    cos_all = jnp.concatenate(cos_parts, axis=1).astype(BF16)
    sin_all = jnp.concatenate(sin_parts, axis=1).astype(BF16)
    f_parts = []
    for q in range(x.shape[0] // t_len):
        rows = slice(q * t_len, (q + 1) * t_len)
        stacked = jnp.concatenate([cos_all[rows], sin_all[rows]], axis=0)
        f_parts.append(jnp.dot(ct_ref[...], stacked, preferred_element_type=F32))
    f = jnp.concatenate(f_parts, axis=0) * ((t_len * GROUP_W) ** -0.5)
    y = jnp.dot(f.astype(BF16), w_bf_ref[...], preferred_element_type=F32)
    o_ref[...] = x + _mod_vec(mod_ref, row, 5) * y


def _dft_tables(t_len):
    def cos_sin(n):
        jk = np.outer(np.arange(n), np.arange(n)) % n
        ang = 2.0 * np.pi * jk.astype(np.float64) / n
        return np.cos(ang), np.sin(ang)

    cc, sc = cos_sin(GROUP_W)
    ct, st = cos_sin(t_len)
    chan = jnp.asarray(np.concatenate([cc, sc], axis=1).astype(np.float32)).astype(BF16)
    time = jnp.asarray(np.concatenate([ct, -st], axis=1).astype(np.float32)).astype(BF16)
    return chan, time


def _fourier(x, mod, g, w_out, n_seq, t_len, tok_blk0, mod_row0, seq_per_step):
    assert seq_per_step == 1 or mod_row0 == 0
    chan, time = _dft_tables(t_len)
    rows = seq_per_step * t_len
    seq = lambda s: (tok_blk0 + s, 0)
    const = lambda s: (0, 0)
    return pl.pallas_call(
        functools.partial(_fourier_kernel, mod_row0=mod_row0, t_len=t_len),
        grid=(n_seq // seq_per_step,),
        in_specs=[
            pl.BlockSpec((rows, D_MODEL), seq),
            _mod_spec(mod),
            pl.BlockSpec((1, D_MODEL), const),
            _resident((GROUP_W, 2 * GROUP_W), const),
            _resident((t_len, 2 * t_len), const),
            _resident((D_MODEL, D_MODEL), const),
        ],
        out_specs=pl.BlockSpec((rows, D_MODEL), seq),
        out_shape=jax.ShapeDtypeStruct((N_TOK, D_MODEL), F32),
        input_output_aliases={0: 0},
        scratch_shapes=[pltpu.VMEM((D_MODEL, D_MODEL), BF16)],
        compiler_params=_cparams(1),
        name="fourier_mixer",
    )(x, mod, g.reshape(1, D_MODEL), chan, time, w_out)


def _cache_layout(t):
    return jnp.transpose(t.reshape(BATCH, 1, NA_HEADS, HEAD_DIM, SEQ), (0, 1, 4, 2, 3))


def kernel(x_prompt, x_sample, cache_k, cache_v, state_lru_fwd, state_lru_bwd, c, c_ctx, w_ada, b_ada, norm_g, ffn1_gate, ffn1_up, ffn1_down, ffn2_gate, ffn2_up, ffn2_down, w_in, q_norm_g, k_norm_g, rpb, conv_w, conv_b, lru_w_r, lru_b_r, lru_w_i, lru_b_i, lru_lambda, w_out_ab, w_out_c):
    assert DEPTH == 2, "one neighbourhood/RG-LRU layer followed by one Fourier layer"
    c_ctx2 = c_ctx.reshape(1, D_MODEL)
    ada = (c_ctx2, c, w_ada, b_ada)
    mod0_ffn1 = _adaln(ada + (0, 0, 3 * D_MODEL), 3)

    ffn1 = (ffn1_gate, ffn1_up, ffn1_down)
    ffn2 = (ffn2_gate, ffn2_up, ffn2_down)

    x, mod0_rest = _ffn((x_prompt.reshape(N_CTX_TOK, D_MODEL), x_sample.reshape(N_SMP_TOK, D_MODEL)),
                        mod0_ffn1, norm_g[0, 0], *ffn1, 0, 0,
                        adaln_next=ada + (0, 3 * D_MODEL, (N_MOD - 3) * D_MODEL))
    q, k, v, xb, gb, new_k, new_v = _proj(x, mod0_rest, 0, norm_g[0, 1], w_in[0], q_norm_g[0], k_norm_g[0])
    o_ctx = _ctx_attn(q, k, v)
    (o_smp,) = _na_attn(q, k, v,
                        jnp.transpose(cache_k[:, 0], (0, 2, 3, 1)).reshape(DEC_BATCH, NA_WIDTH, PAST_LEN),
                        jnp.transpose(cache_v[:, 0], (0, 2, 3, 1)).reshape(DEC_BATCH, NA_WIDTH, PAST_LEN),
                        rpb[0])
    lru_prm = (conv_w[0], conv_b[0], lru_w_r[0], lru_b_r[0], lru_w_i[0], lru_b_i[0], lru_lambda[0])
    yb_ctx, new_hf, new_hb = _lru(xb, gb, *lru_prm, None, BATCH, SEQ, 0, CTX_SEQ_PER_STEP)
    yb_smp, _, _ = _lru(xb, gb, *lru_prm, (state_lru_fwd, state_lru_bwd),
                        DEC_BATCH, DEC_SEQ, N_CTX_TOK // DEC_SEQ, 1)
    x, mod1 = _ffn((x,), mod0_rest, norm_g[0, 2], *ffn2, 0, 3,
                   mixer_out=(o_ctx, o_smp, yb_ctx, yb_smp, w_out_ab), adaln_next=ada + (1, 0, MOD_WIDTH))

    (x,) = _ffn((x,), mod1, norm_g[1, 0], *ffn1, 1, 0)
    x = _fourier(x, mod1, norm_g[1, 1], w_out_c[0], BATCH, SEQ, 0, 0, CTX_SEQ_PER_STEP)
    x = _fourier(x, mod1, norm_g[1, 1], w_out_c[0], DEC_BATCH, DEC_SEQ, N_CTX_TOK // DEC_SEQ, 1, 1)
    y_prompt, y_sample = _ffn((x,), mod1, norm_g[1, 2], *ffn2, 1, 6, split_out=True)

    return (y_prompt.reshape(BATCH, SEQ, D_MODEL), y_sample.reshape(DEC_BATCH, DEC_SEQ, D_MODEL),
            _cache_layout(new_k), _cache_layout(new_v),
            new_hf, new_hb)
```

```python
import functools

import numpy as np
import jax
import jax.numpy as jnp
from jax import lax
from jax.experimental import pallas as pl
from jax.experimental.pallas import tpu as pltpu

F32 = jnp.float32
BF16 = jnp.bfloat16

D_MODEL = 1024
BATCH = 16
SEQ = 256
DEPTH = 2
DEC_BATCH = 2
DEC_SEQ = 1024
PAST_LEN = 256
GRID_W = 64
HEAD_DIM = 64
NA_WIDTH = 512
NA_HEADS = 8
WIN_H = 8
WIN_W = 16
LRU_WIDTH = 512
LRU_BLOCKS = 8
LRU_BLOCK = 64
LRU_C = 8.0
LRU_SUB = 256
CONV_W = 4
FOURIER_GROUPS = 4
GROUP_W = D_MODEL // FOURIER_GROUPS
D_FF = 2816
N_MOD = 9
IN_WIDTH = 3 * NA_WIDTH + 2 * LRU_WIDTH
EPS = 1e-6

N_CTX_TOK = BATCH * SEQ
N_SMP_TOK = DEC_BATCH * DEC_SEQ
N_TOK = N_CTX_TOK + N_SMP_TOK
MOD_ROWS = 8
MOD_WIDTH = N_MOD * D_MODEL
ROWS = DEC_SEQ // GRID_W
KH = min(WIN_H, ROWS)

TOKEN_TILE = 512
N_CTX_TILES = N_CTX_TOK // TOKEN_TILE
CTX_SEQ_PER_STEP = 2
FFN_TILE = 512
FF_TILE = 256
FF_CHUNKS = D_FF // FF_TILE
FF_STAGE_SLOTS = 2
SUBLANES = 8
LANES = 128
VMEM_LIMIT = 56 * 1024 * 1024

NA_Q_ROWS = 4
NA_GROUPS = ROWS // NA_Q_ROWS
NA_K_ROWS = 12
NA_Q = NA_Q_ROWS * GRID_W
NA_K = NA_K_ROWS * GRID_W
N_DR = 2 * WIN_H - 1
N_DC = 2 * WIN_W - 1
N_DR_PAIRS = N_DR + 1


def _cparams(n_axes):
    return pltpu.CompilerParams(
        dimension_semantics=("arbitrary",) * n_axes, vmem_limit_bytes=VMEM_LIMIT)


def _resident(block_shape, index_map):
    return pl.BlockSpec(block_shape, index_map, pipeline_mode=pl.Buffered(1))


def _mod_spec(mod):
    return _resident(mod.shape, lambda i: (0, 0))


def _mod_row_of_tile(i, tile=TOKEN_TILE):
    n_ctx_tiles = N_CTX_TOK // tile
    tiles_per_seq = DEC_SEQ // tile
    return jnp.where(i < n_ctx_tiles, 0, 1 + (i - n_ctx_tiles) // tiles_per_seq)


def _mod_vec(mod_ref, row, k):
    return mod_ref[pl.ds(row, 1), k * D_MODEL:(k + 1) * D_MODEL]


def _norm_mod(x, g, shift, scale):
    ms = jnp.mean(x * x, axis=-1, keepdims=True)
    return (x * lax.rsqrt(ms + EPS)) * (g * (1.0 + scale)) + shift


def _adaln_slab(cctx_ref, c_ref, w_ref, b_ref, cond_ref, layer):
    cond_ref[...] = jnp.zeros_like(cond_ref)
    cond_ref[0:1, :] = cctx_ref[...]
    cond_ref[1:1 + DEC_BATCH, :] = c_ref[...]
    cond = cond_ref[...]
    s = (cond * jax.nn.sigmoid(cond)).astype(BF16)
    return jnp.dot(s, w_ref[...].astype(BF16), preferred_element_type=F32) + b_ref[layer:layer + 1, :]


def _adaln_specs(job, n_steps, step=lambda i: i):
    layer, col0, n_cols = job[4:]
    slab = n_cols // n_steps
    assert n_cols % n_steps == 0 and slab % LANES == 0 and col0 % slab == 0
    blk0 = col0 // slab
    return ([pl.BlockSpec((1, D_MODEL), lambda *ids: (0, 0)),
             pl.BlockSpec((DEC_BATCH, D_MODEL), lambda *ids: (0, 0)),
             pl.BlockSpec((None, D_MODEL, slab), lambda *ids: (layer, 0, blk0 + step(*ids))),
             pl.BlockSpec((DEPTH, slab), lambda *ids: (0, blk0 + step(*ids)))],
            pl.BlockSpec((MOD_ROWS, slab), lambda *ids: (0, step(*ids))),
            jax.ShapeDtypeStruct((MOD_ROWS, n_cols), F32))


def _adaln_kernel(cctx_ref, c_ref, w_ref, b_ref, o_ref, cond_ref, *, layer):
    o_ref[...] = _adaln_slab(cctx_ref, c_ref, w_ref, b_ref, cond_ref, layer)


def _adaln(job, n_steps):
    in_specs, out_spec, out_shape = _adaln_specs(job, n_steps)
    return pl.pallas_call(
        functools.partial(_adaln_kernel, layer=job[4]),
        grid=(n_steps,),
        in_specs=in_specs,
        out_specs=out_spec,
        out_shape=out_shape,
        scratch_shapes=[pltpu.VMEM((MOD_ROWS, D_MODEL), F32)],
        compiler_params=_cparams(1),
        name="adaln",
    )(*job[:4])


def _ffn_weight_copy(w_hbm, stage_ref, sem_ref, layer, j, ff_axis):
    ff = pl.ds(j * FF_TILE, FF_TILE)
    src = w_hbm.at[layer, :, ff] if ff_axis == 1 else w_hbm.at[layer, ff, :]
    slot = j % FF_STAGE_SLOTS
    return pltpu.make_async_copy(src, stage_ref.at[slot], sem_ref.at[slot])


def _mixer_out_copy(w_hbm, stage_ref, sem_ref, j):
    rows = stage_ref.shape[1]
    slot = j % stage_ref.shape[0]
    return pltpu.make_async_copy(w_hbm.at[0, pl.ds(j * rows, rows), :], stage_ref.at[slot], sem_ref.at[slot])


def _ffn_kernel(*refs, layer, mod_base, split_in, split_out, mixer_out, adaln_next):
    refs = list(refs)
    take = lambda n: [refs.pop(0) for _ in range(n)]
    x_refs = take(2 if split_in else 1)
    mix_refs = take(4) if mixer_out else None
    mod_ref, g_ref = take(2)
    wo_hbm = take(1)[0] if mixer_out else None
    ada_refs = take(4) if adaln_next is not None else None
    wg_hbm, wu_hbm, wd_hbm = take(3)
    o_refs = take(2 if split_out else 1)
    modn_ref = take(1)[0] if adaln_next is not None else None
    wg_bf, wu_bf, wd_bf, stg_g, stg_u, stg_d, sem_g, sem_u, sem_d = take(9)
    cond_scr = take(1)[0] if adaln_next is not None else None
    streams = ((wg_hbm, stg_g, sem_g, wg_bf, 1), (wu_hbm, stg_u, sem_u, wu_bf, 1),
               (wd_hbm, stg_d, sem_d, wd_bf, 0))

    i = pl.program_id(0)
    is_ctx = i < N_CTX_TOK // FFN_TILE
    if split_in:
        x = jnp.where(is_ctx, x_refs[0][...], x_refs[1][...])
    else:
        x = x_refs[0][...]
    row = _mod_row_of_tile(i, FFN_TILE)

    if mixer_out:
        wo_bf, stg_o, sem_o = take(3)
        rows = stg_o.shape[1]
        n_chunks = D_MODEL // rows

        @pl.when(i == 0)
        def _():
            for j in range(stg_o.shape[0]):
                _mixer_out_copy(wo_hbm, stg_o, sem_o, j).start()
            for j in range(n_chunks):
                _mixer_out_copy(wo_hbm, stg_o, sem_o, j).wait()
                wo_bf[j * rows:(j + 1) * rows, :] = stg_o[j % stg_o.shape[0]].astype(BF16)
                if j + stg_o.shape[0] < n_chunks:
                    _mixer_out_copy(wo_hbm, stg_o, sem_o, j + stg_o.shape[0]).start()

        oc_ref, os_ref, yc_ref, ys_ref = mix_refs
        cat = jnp.concatenate([jnp.where(is_ctx, oc_ref[...], os_ref[...]),
                               jnp.where(is_ctx, yc_ref[...], ys_ref[...])], axis=1)
        x = x + _mod_vec(mod_ref, row, mod_base - 1) * jnp.dot(cat, wo_bf[...], preferred_element_type=F32)

    h = _norm_mod(x, g_ref[...], _mod_vec(mod_ref, row, mod_base),
                  _mod_vec(mod_ref, row, mod_base + 1)).astype(BF16)

    def start_chunk(j):
        for w_hbm, stg, sem, _, ff_axis in streams:
            _ffn_weight_copy(w_hbm, stg, sem, layer, j, ff_axis).start()

    def finish_chunk(j):
        for w_hbm, stg, sem, w_bf, ff_axis in streams:
            _ffn_weight_copy(w_hbm, stg, sem, layer, j, ff_axis).wait()
            w_bf[j] = stg[j % FF_STAGE_SLOTS].astype(BF16)

    def run(stream_weights):
        if stream_weights:
            for j in range(FF_STAGE_SLOTS):
                start_chunk(j)
        acc = None
        for j in range(FF_CHUNKS):
            if stream_weights:
                finish_chunk(j)
                if j + FF_STAGE_SLOTS < FF_CHUNKS:
                    start_chunk(j + FF_STAGE_SLOTS)
            a = jnp.dot(h, wg_bf[j], preferred_element_type=F32)
            b = jnp.dot(h, wu_bf[j], preferred_element_type=F32)
            if adaln_next is not None and j == 1:
                modn_ref[...] = _adaln_slab(*ada_refs, cond_scr, adaln_next)
            act = (a * jax.nn.sigmoid(a) * b).astype(BF16)
            y = jnp.dot(act, wd_bf[j], preferred_element_type=F32)
            acc = y if acc is None else acc + y
        res = x + 0.5 * _mod_vec(mod_ref, row, mod_base + 2) * acc
        if split_out:
            @pl.when(is_ctx)
            def _():
                o_refs[0][...] = res

            @pl.when(jnp.logical_not(is_ctx))
            def _():
                o_refs[1][...] = res
        else:
            o_refs[0][...] = res

    @pl.when(i == 0)
    def _():
        run(True)

    @pl.when(i > 0)
    def _():
        run(False)


def _ffn(xs, mod, g, wg, wu, wd, layer, mod_base, split_out=False, mixer_out=None, adaln_next=None):
    tm = FFN_TILE
    n_ctx_tiles = N_CTX_TOK // tm
    split_in = len(xs) == 2

    def tiles(width):
        return (pl.BlockSpec((tm, width), lambda i: (i, 0)),
                pl.BlockSpec((tm, width), lambda i: (jnp.minimum(i, n_ctx_tiles - 1), 0)),
                pl.BlockSpec((tm, width), lambda i: (jnp.maximum(i - n_ctx_tiles, 0), 0)))

    tok, ctx_tok, smp_tok = tiles(D_MODEL)
    full = jax.ShapeDtypeStruct((N_TOK, D_MODEL), F32)
    pair = [jax.ShapeDtypeStruct((N_CTX_TOK, D_MODEL), F32), jax.ShapeDtypeStruct((N_SMP_TOK, D_MODEL), F32)]
    hbm = pl.BlockSpec(memory_space=pl.ANY)
    in_specs = [ctx_tok, smp_tok] if split_in else [tok]
    operands = list(xs)
    scratch = [
        pltpu.VMEM((FF_CHUNKS, D_MODEL, FF_TILE), BF16),
        pltpu.VMEM((FF_CHUNKS, D_MODEL, FF_TILE), BF16),
        pltpu.VMEM((FF_CHUNKS, FF_TILE, D_MODEL), BF16),
        pltpu.VMEM((FF_STAGE_SLOTS, D_MODEL, FF_TILE), F32),
        pltpu.VMEM((FF_STAGE_SLOTS, D_MODEL, FF_TILE), F32),
        pltpu.VMEM((FF_STAGE_SLOTS, FF_TILE, D_MODEL), F32),
        pltpu.SemaphoreType.DMA((FF_STAGE_SLOTS,)),
        pltpu.SemaphoreType.DMA((FF_STAGE_SLOTS,)),
        pltpu.SemaphoreType.DMA((FF_STAGE_SLOTS,)),
    ]
    if mixer_out is not None:
        _, ctx_half, smp_half = tiles(NA_WIDTH)
        in_specs += [ctx_half, smp_half, ctx_half, smp_half]
        operands += list(mixer_out[:4])
    in_specs += [_mod_spec(mod), pl.BlockSpec((1, D_MODEL), lambda i: (0, 0))]
    operands += [mod, g.reshape(1, D_MODEL)]
    out_specs = [ctx_tok, smp_tok] if split_out else [tok]
    out_shape = pair if split_out else [full]
    if adaln_next is not None:
        ada_in, ada_out, ada_shape = _adaln_specs(adaln_next, N_TOK // tm)
        in_specs += ada_in
        operands += list(adaln_next[:4])
        out_specs.append(ada_out)
        out_shape.append(ada_shape)
        scratch.append(pltpu.VMEM((MOD_ROWS, D_MODEL), F32))
    if mixer_out is not None:
        in_specs.insert(len(in_specs) - (4 if adaln_next is not None else 0), hbm)
        operands.insert(len(operands) - (4 if adaln_next is not None else 0), mixer_out[4])
        scratch += [
            pltpu.VMEM((D_MODEL, D_MODEL), BF16),
            pltpu.VMEM((FF_STAGE_SLOTS, FF_TILE, D_MODEL), F32),
            pltpu.SemaphoreType.DMA((FF_STAGE_SLOTS,)),
        ]
    return pl.pallas_call(
        functools.partial(_ffn_kernel, layer=layer, mod_base=mod_base, split_in=split_in,
                          split_out=split_out, mixer_out=mixer_out is not None,
                          adaln_next=None if adaln_next is None else adaln_next[4]),
        grid=(N_TOK // tm,),
        in_specs=in_specs + [hbm, hbm, hbm],
        out_specs=out_specs,
        out_shape=out_shape,
        scratch_shapes=scratch,
        compiler_params=_cparams(1),
        name="ffn",
    )(*operands, wg, wu, wd)


def _head_rms_norm(z, g, ones_bd):
    z2 = z * z
    hi = z2.astype(BF16)
    lo = (z2 - hi.astype(F32)).astype(BF16)
    n = ones_bd.shape[0]
    parts = []
    for c in range(z.shape[1] // n):
        sl = slice(c * n, (c + 1) * n)
        parts.append(jnp.dot(hi[:, sl], ones_bd, preferred_element_type=F32)
                     + jnp.dot(lo[:, sl], ones_bd, preferred_element_type=F32))
    ss = jnp.concatenate(parts, axis=1)
    return z * lax.rsqrt(ss * (1.0 / HEAD_DIM) + EPS) * g


def _proj_kernel(*refs, mod_k0, adaln_layer):
    refs = list(refs)
    x_ref, mod_ref, g_ref, w_ref, qg_ref, kg_ref, ones_ref = refs[:7]
    ada_refs = refs[7:11] if adaln_layer is not None else None
    n_in = 7 if adaln_layer is None else 11
    q_ref, k_ref, v_ref, xb_ref, gb_ref, kout_ref, vout_ref = refs[n_in:n_in + 7]
    rest = refs[n_in + 7:]
    modn_ref = rest.pop(0) if adaln_layer is not None else None
    w_bf_ref = rest.pop(0)
    i = pl.program_id(0)

    @pl.when(i == 0)
    def _():
        w_bf_ref[...] = w_ref[...].astype(BF16)

    x = x_ref[...]
    row = _mod_row_of_tile(i)
    h = _norm_mod(x, g_ref[...], _mod_vec(mod_ref, row, mod_k0),
                  _mod_vec(mod_ref, row, mod_k0 + 1)).astype(BF16)

    def proj(part):
        return jnp.dot(h, w_bf_ref[:, part * NA_WIDTH:(part + 1) * NA_WIDTH], preferred_element_type=F32)

    ones_bd = ones_ref[...]
    q_raw = proj(0)
    k_raw = proj(1)
    q = _head_rms_norm(q_raw, jnp.tile(qg_ref[...], (1, NA_HEADS)), ones_bd) * (HEAD_DIM ** -0.5)
    q_ref[...] = q.astype(BF16)
    v = proj(2)
    if adaln_layer is not None:
        modn_ref[...] = _adaln_slab(*ada_refs, rest.pop(0), adaln_layer)
    k = _head_rms_norm(k_raw, jnp.tile(kg_ref[...], (1, NA_HEADS)), ones_bd)
    k_ref[...] = k.astype(BF16)
    xb = proj(3)
    v_ref[...] = v.astype(BF16)
    gb = proj(4)
    xb_ref[...] = xb
    gb_ref[...] = gb

    kt = [k[b * SEQ:(b + 1) * SEQ, :].T for b in range(TOKEN_TILE // SEQ)]
    vt = [v[b * SEQ:(b + 1) * SEQ, :].T for b in range(TOKEN_TILE // SEQ)]

    @pl.when(i < N_CTX_TILES)
    def _():
        for b in range(TOKEN_TILE // SEQ):
            kout_ref[b] = kt[b]
            vout_ref[b] = vt[b]


def _proj(x, mod, mod_k0, g, w_in, q_g, k_g, adaln_next=None):
    tm = TOKEN_TILE
    head = np.arange(2 * LANES) // HEAD_DIM
    ones_bd = jnp.asarray((head[:, None] == head[None, :]).astype(np.float32), dtype=BF16)
    tok = lambda i: (i, 0)
    const = lambda i: (0, 0)
    act_f32 = jax.ShapeDtypeStruct((N_TOK, NA_WIDTH), F32)
    act_bf16 = jax.ShapeDtypeStruct((N_TOK, NA_WIDTH), BF16)
    cache = jax.ShapeDtypeStruct((BATCH, NA_WIDTH, SEQ), F32)
    cache_spec = pl.BlockSpec((tm // SEQ, NA_WIDTH, SEQ), lambda i: (jnp.minimum(i, N_CTX_TILES - 1), 0, 0))
    in_specs = [
        pl.BlockSpec((tm, D_MODEL), tok),
        _mod_spec(mod),
        pl.BlockSpec((1, D_MODEL), const),
        _resident((D_MODEL, IN_WIDTH), const),
        pl.BlockSpec((1, HEAD_DIM), const),
        pl.BlockSpec((1, HEAD_DIM), const),
        _resident((2 * LANES, 2 * LANES), const),
    ]
    operands = [x, mod, g.reshape(1, D_MODEL), w_in, q_g.reshape(1, HEAD_DIM), k_g.reshape(1, HEAD_DIM), ones_bd]
    out_specs = [pl.BlockSpec((tm, NA_WIDTH), tok)] * 5 + [cache_spec, cache_spec]
    out_shape = [act_bf16, act_bf16, act_bf16, act_f32, act_f32, cache, cache]
    scratch = [pltpu.VMEM((D_MODEL, IN_WIDTH), BF16)]
    if adaln_next is not None:
        ada_in, ada_out, ada_shape = _adaln_specs(adaln_next, N_TOK // tm)
        in_specs += ada_in
        operands += list(adaln_next[:4])
        out_specs.append(ada_out)
        out_shape.append(ada_shape)
        scratch.append(pltpu.VMEM((MOD_ROWS, D_MODEL), F32))
    return pl.pallas_call(
        functools.partial(_proj_kernel, mod_k0=mod_k0,
                          adaln_layer=None if adaln_next is None else adaln_next[4]),
        grid=(N_TOK // tm,),
        in_specs=in_specs,
        out_specs=out_specs,
        out_shape=out_shape,
        scratch_shapes=scratch,
        compiler_params=_cparams(1),
        name="mixer_in_proj",
    )(*operands)


def _head_masks():
    lane = lax.broadcasted_iota(jnp.int32, (1, 2 * HEAD_DIM), 1)
    return [lane < HEAD_DIM, lane >= HEAD_DIM]


def _ctx_attn_kernel(q_ref, k_ref, v_ref, o_ref):
    masks = _head_masks()
    units = [(b, h) for b in range(q_ref.shape[0] // SEQ) for h in range(NA_HEADS)]

    def where(b, h):
        return slice(b * SEQ, (b + 1) * SEQ), slice(2 * HEAD_DIM * (h // 2), 2 * HEAD_DIM * (h // 2 + 1))

    def scores(b, h):
        rows, sl = where(b, h)
        q2 = q_ref[rows, sl]
        qm = jnp.where(masks[h % 2], q2, jnp.zeros_like(q2))
        return jnp.dot(qm, k_ref[rows, sl].T, preferred_element_type=F32)

    def attend(b, h, s):
        rows, sl = where(b, h)
        pe = jnp.exp(s - jnp.max(s, axis=-1, keepdims=True))
        den = jnp.sum(pe, axis=-1, keepdims=True)
        return jnp.dot(pe.astype(BF16), v_ref[rows, sl], preferred_element_type=F32) / den

    pending = scores(*units[0])
    out = None
    for n, (b, h) in enumerate(units):
        current = pending
        if n + 1 < len(units):
            pending = scores(*units[n + 1])
        o = attend(b, h, current)
        if h % 2 == 0:
            out = o
        else:
            rows, sl = where(b, h)
            o_ref[rows, sl] = jnp.where(masks[1], o, out).astype(BF16)


def _na_build_bias_table(rpb_ref, table_ref):
    qc = lax.broadcasted_iota(jnp.int32, (GRID_W, LANES), 0)
    lane = lax.broadcasted_iota(jnp.int32, (GRID_W, LANES), 1)
    kc = lane % GRID_W
    col_start = jnp.clip(qc - WIN_W // 2, 0, GRID_W - WIN_W)
    col_in = (kc >= col_start) & (kc < col_start + WIN_W)
    neg = jnp.full((GRID_W, LANES), -jnp.inf, F32)

    def toeplitz(h, dr, lane0):
        if dr < 0 or dr >= N_DR:
            return neg
        row = jnp.pad(rpb_ref[dr, h:h + 1, :], ((0, 0), (0, LANES - N_DC)))
        w = jnp.broadcast_to(row, (GRID_W, LANES))
        return pltpu.roll(w, (lane0 - (WIN_W - 1)) % LANES, 1, stride=1, stride_axis=0)

    for h in range(NA_HEADS):
        for i in range(N_DR_PAIRS):
            t = jnp.where(lane < GRID_W, toeplitz(h, i - 1, 0), toeplitz(h, i, GRID_W))
            table_ref[h, i] = jnp.where(col_in, t, neg)


def _na_kernel(*refs, adaln_layer):
    q_ref, k_ref, v_ref, kc_ref, vc_ref, rpb_ref = refs[:6]
    if adaln_layer is None:
        o_ref, table_ref = refs[6:]
    else:
        ada_refs = refs[6:10]
        o_ref, modn_ref, table_ref, cond_scr = refs[10:]
    b = pl.program_id(0)
    g = pl.program_id(1)

    @pl.when((b == 0) & (g == 0))
    def _():
        _na_build_bias_table(rpb_ref, table_ref)

    win_row0 = jnp.where(g < NA_GROUPS // 2, 0, ROWS - NA_K_ROWS)
    start = pl.multiple_of(win_row0 * GRID_W, GRID_W)
    q_row = g * NA_Q_ROWS + lax.broadcasted_iota(jnp.int32, (NA_Q, 1), 0) // GRID_W
    k_row = win_row0 + lax.broadcasted_iota(jnp.int32, (1, NA_K), 1) // GRID_W
    row_start = jnp.clip(q_row - KH // 2, 0, ROWS - KH)
    row_in = (k_row >= row_start) & (k_row < row_start + KH)
    masks = _head_masks()

    def pair_slab(p):
        return slice(2 * HEAD_DIM * p, 2 * HEAD_DIM * (p + 1))

    def scores(head):
        p, e = divmod(head, 2)
        sl = pair_slab(p)
        q2 = q_ref[:, sl]
        klt = k_ref[pl.ds(start, NA_K), sl].T
        kct = kc_ref[0, sl, :].astype(BF16)
        bias_rows = []
        for a in range(NA_Q_ROWS):
            tiles = []
            for m in range(NA_K_ROWS // 2):
                dr = win_row0 + 2 * m - (g * NA_Q_ROWS + a) + (WIN_H - 1)
                tiles.append(table_ref[head, jnp.clip(dr + 1, 0, N_DR_PAIRS - 1)])
            bias_rows.append(jnp.concatenate(tiles, axis=1))
        bias = jnp.concatenate(bias_rows, axis=0)
        qm = jnp.where(masks[e], q2, jnp.zeros_like(q2))
        s_loc = jnp.where(row_in, jnp.dot(qm, klt, preferred_element_type=F32) + bias, -jnp.inf)
        s_ctx = jnp.dot(qm, kct, preferred_element_type=F32)
        return s_loc, s_ctx

    def attend(head, s_loc, s_ctx):
        sl = pair_slab(head // 2)
        vl = v_ref[pl.ds(start, NA_K), sl]
        vct = vc_ref[0, sl, :].astype(BF16)
        m_max = jnp.maximum(jnp.max(s_loc, axis=-1, keepdims=True),
                            jnp.max(s_ctx, axis=-1, keepdims=True))
        p_loc = jnp.exp(s_loc - m_max)
        p_ctx = jnp.exp(s_ctx - m_max)
        den = jnp.sum(p_loc, axis=-1, keepdims=True) + jnp.sum(p_ctx, axis=-1, keepdims=True)
        return (jnp.dot(p_loc.astype(BF16), vl, preferred_element_type=F32)
                + lax.dot_general(p_ctx.astype(BF16), vct, (((1,), (1,)), ((), ())),
                                  preferred_element_type=F32)) / den

    pending = scores(0)
    out = None
    for head in range(NA_HEADS):
        current = pending
        if head + 1 < NA_HEADS:
            pending = scores(head + 1)
        o = attend(head, *current)
        if adaln_layer is not None and head == 0:
            modn_ref[...] = _adaln_slab(*ada_refs, cond_scr, adaln_layer)
        if head % 2 == 0:
            out = o
        else:
            o_ref[:, pair_slab(head // 2)] = jnp.where(masks[1], o, out).astype(BF16)


def _na_attn(q, k, v, k_ctx, v_ctx, rpb_e, adaln_next=None):
    smp_blk0 = N_CTX_TOK // DEC_SEQ
    q_blk0 = N_CTX_TOK // NA_Q
    kv = pl.BlockSpec((DEC_SEQ, NA_WIDTH), lambda b, g: (smp_blk0 + b, 0))
    ctx = pl.BlockSpec((1, NA_WIDTH, PAST_LEN), lambda b, g: (b, 0, 0))
    in_specs = [
        pl.BlockSpec((NA_Q, NA_WIDTH), lambda b, g: (q_blk0 + b * NA_GROUPS + g, 0)),
        kv, kv, ctx, ctx,
        pl.BlockSpec((N_DR, NA_HEADS, N_DC), lambda b, g: (0, 0, 0)),
    ]
    operands = [q, k, v, k_ctx, v_ctx, jnp.transpose(rpb_e, (1, 0, 2))]
    out_specs = [pl.BlockSpec((NA_Q, NA_WIDTH), lambda b, g: (b * NA_GROUPS + g, 0))]
    out_shape = [jax.ShapeDtypeStruct((N_SMP_TOK, NA_WIDTH), BF16)]
    scratch = [pltpu.VMEM((NA_HEADS, N_DR_PAIRS, GRID_W, LANES), F32)]
    if adaln_next is not None:
        ada_in, ada_out, ada_shape = _adaln_specs(adaln_next, DEC_BATCH * NA_GROUPS,
                                                  lambda b, g: b * NA_GROUPS + g)
        in_specs += ada_in
        operands += list(adaln_next[:4])
        out_specs.append(ada_out)
        out_shape.append(ada_shape)
        scratch.append(pltpu.VMEM((MOD_ROWS, D_MODEL), F32))
    return pl.pallas_call(
        functools.partial(_na_kernel, adaln_layer=None if adaln_next is None else adaln_next[4]),
        grid=(DEC_BATCH, NA_GROUPS),
        in_specs=in_specs,
        out_specs=out_specs,
        out_shape=out_shape,
        scratch_shapes=scratch,
        compiler_params=_cparams(2),
        name="neighbourhood_attention",
    )(*operands)


def _gelu_tanh(x):
    c0 = float(np.sqrt(2.0 / np.pi))
    inner = x * (c0 + (c0 * 0.044715) * (x * x))
    return (0.5 * x) * (1.0 + jnp.tanh(inner))


def _lru_build_gate_weights(wr_ref, wi_ref, w_scr):
    blocks_per_group = LRU_SUB // LRU_BLOCK
    w_scr[...] = jnp.zeros_like(w_scr)
    for d in range(2):
        for kind, w_ref in enumerate((wr_ref, wi_ref)):
            col0 = (2 * d + kind) * LRU_SUB
            for blk in range(LRU_BLOCKS):
                c, n = divmod(blk, blocks_per_group)
                r0 = n * LRU_BLOCK
                w_scr[c, r0:r0 + LRU_BLOCK, col0 + r0:col0 + r0 + LRU_BLOCK] = (
                    0.5 * w_ref[d * LRU_BLOCKS + blk]).astype(BF16)


def _lru_kernel(*refs, zero_state, t_len, with_attention):
    refs = list(refs)
    if with_attention:
        q_ref, k_ref, v_ref = refs[:3]
        refs = refs[3:]
    xb_ref, gb_ref, cw_ref, cb_ref, wr_ref, wi_ref, br_ref, bi_ref, lam_ref = refs[:9]
    h0f_ref, h0b_ref = (None, None) if zero_state else refs[9:11]
    y_ref, hlf_ref, hlb_ref, af_ref, uf_ref, ab_ref, ub_ref, w_scr = refs[-8:]
    width = xb_ref.shape[1]
    n_seq = xb_ref.shape[0] // t_len
    n_blk = t_len // SUBLANES
    row = lax.broadcasted_iota(jnp.int32, (t_len, 1), 0)
    in_block = lax.broadcasted_iota(jnp.int32, (1, SUBLANES, 1), 1)

    @pl.when(pl.program_id(0) == 0)
    def _():
        _lru_build_gate_weights(wr_ref, wi_ref, w_scr)

    if with_attention:
        _ctx_attn_kernel(q_ref, k_ref, v_ref, refs[-9])

    def shifted(z, s):
        rolled = pltpu.roll(z, (-s) % t_len, axis=0)
        ok = (row + s >= 0) & (row + s < t_len)
        return jnp.where(ok, rolled, 0.0)

    left = (CONV_W - 1) // 2
    for q, c in [(q, c) for q in range(n_seq) for c in range(width // LRU_SUB)]:
        rows = slice(q * t_len, (q + 1) * t_len)
        cs = slice(c * LRU_SUB, (c + 1) * LRU_SUB)
        x = xb_ref[rows, cs]
        xc = cb_ref[:, cs]
        for j in range(CONV_W):
            tap = x if j == left else shifted(x, j - left)
            xc = xc + tap * cw_ref[j:j + 1, cs]
        half_gates = jnp.dot(xc.astype(BF16), w_scr[c], preferred_element_type=F32)
        half_xc = 0.5 * xc
        for d, (a_ref, u_ref) in enumerate(((af_ref, uf_ref), (ab_ref, ub_ref))):
            t_r = jnp.tanh(half_gates[:, (2 * d) * LRU_SUB:(2 * d + 1) * LRU_SUB] + 0.5 * br_ref[d:d + 1, cs])
            t_i = jnp.tanh(half_gates[:, (2 * d + 1) * LRU_SUB:(2 * d + 2) * LRU_SUB] + 0.5 * bi_ref[d:d + 1, cs])
            lam = lam_ref[d:d + 1, cs]
            log_sig = jnp.minimum(lam, 0.0) - jnp.log1p(jnp.exp(-jnp.abs(lam)))
            half_c_log_sig = (0.5 * LRU_C) * log_sig
            log_a = t_r * half_c_log_sig + half_c_log_sig
            a = jnp.exp(log_a)
            var = -jnp.tanh(log_a) * (a * a + 1.0)
            u = jnp.where(var > 0.0, var * lax.rsqrt(var), 0.0) * ((t_i + 1.0) * half_xc)
            a = a.reshape(n_blk, SUBLANES, LRU_SUB)
            u = u.reshape(n_blk, SUBLANES, LRU_SUB)
            step = 1
            while step < SUBLANES:
                if d == 0:
                    ok, shift = in_block >= step, step
                else:
                    ok, shift = in_block < SUBLANES - step, SUBLANES - step
                a_prev = jnp.where(ok, pltpu.roll(a, shift, axis=1), 1.0)
                u_prev = jnp.where(ok, pltpu.roll(u, shift, axis=1), 0.0)
                u = u + a * u_prev
                a = a * a_prev
                step *= 2
            a_ref[rows, cs] = a.reshape(t_len, LRU_SUB)
            u_ref[rows, cs] = u.reshape(t_len, LRU_SUB)

    def body(i, carry):
        new = []
        for q, (cf, cb) in enumerate(carry):
            f0 = pl.multiple_of(q * t_len + i * SUBLANES, SUBLANES)
            b0 = pl.multiple_of(q * t_len + (n_blk - 1 - i) * SUBLANES, SUBLANES)
            hf = uf_ref[pl.ds(f0, SUBLANES), :] + af_ref[pl.ds(f0, SUBLANES), :] * cf
            hb = ub_ref[pl.ds(b0, SUBLANES), :] + ab_ref[pl.ds(b0, SUBLANES), :] * cb
            uf_ref[pl.ds(f0, SUBLANES), :] = hf
            ub_ref[pl.ds(b0, SUBLANES), :] = hb
            new.append((jnp.broadcast_to(hf[SUBLANES - 1:SUBLANES, :], (SUBLANES, width)),
                        jnp.broadcast_to(hb[0:1, :], (SUBLANES, width))))
        return tuple(new)

    if zero_state:
        init = tuple((jnp.zeros((SUBLANES, width), F32),) * 2 for _ in range(n_seq))
    else:
        init = tuple((jnp.broadcast_to(h0f_ref[q], (SUBLANES, width)),
                      jnp.broadcast_to(h0b_ref[q], (SUBLANES, width))) for q in range(n_seq))
    last = lax.fori_loop(0, n_blk, body, init)
    for q, (cf, cb) in enumerate(last):
        hlf_ref[q] = cf[0:1, :]
        hlb_ref[q] = cb[0:1, :]
    y_ref[...] = ((uf_ref[...] + ub_ref[...]) * _gelu_tanh(gb_ref[...])).astype(BF16)


def _lru(xb, gb, conv_w, conv_b, w_r, b_r, w_i, b_i, lam, h0, n_seq, t_len, tok_blk0, seq_per_step,
         attention=None):
    rows = seq_per_step * t_len
    tok_in = pl.BlockSpec((rows, LRU_WIDTH), lambda s: (tok_blk0 + s, 0))
    tok_out = pl.BlockSpec((rows, LRU_WIDTH), lambda s: (s, 0))
    tok_shape = jax.ShapeDtypeStruct((n_seq * t_len, LRU_WIDTH), BF16)
    n_att = 0 if attention is None else 1
    const2 = lambda s: (0, 0)
    const3 = lambda s: (0, 0, 0)
    blocks = pl.BlockSpec((2 * LRU_BLOCKS, LRU_BLOCK, LRU_BLOCK), const3)
    per_dir = pl.BlockSpec((2, LRU_WIDTH), const2)
    state = pl.BlockSpec((seq_per_step, 1, LRU_WIDTH), lambda s: (s, 0, 0))
    state_shape = jax.ShapeDtypeStruct((n_seq, 1, LRU_WIDTH), F32)
    return pl.pallas_call(
        functools.partial(_lru_kernel, zero_state=h0 is None, t_len=t_len,
                          with_attention=attention is not None),
        grid=(n_seq // seq_per_step,),
        in_specs=[tok_in] * (3 * n_att) + [
            tok_in,
            tok_in,
            pl.BlockSpec((CONV_W, LRU_WIDTH), const2),
            pl.BlockSpec((1, LRU_WIDTH), const2),
            blocks, blocks, per_dir, per_dir, per_dir,
        ] + ([] if h0 is None else [state, state]),
        out_specs=[tok_out] * n_att + [tok_out, state, state],
        out_shape=[tok_shape] * n_att + [tok_shape, state_shape, state_shape],
        scratch_shapes=[pltpu.VMEM((rows, LRU_WIDTH), F32)] * 4
        + [pltpu.VMEM((LRU_WIDTH // LRU_SUB, LRU_SUB, 4 * LRU_SUB), BF16)],
        compiler_params=_cparams(1),
        name="rglru",
    )(*(() if attention is None else attention), xb, gb, conv_w, conv_b.reshape(1, LRU_WIDTH),
      w_r.reshape(2 * LRU_BLOCKS, LRU_BLOCK, LRU_BLOCK), w_i.reshape(2 * LRU_BLOCKS, LRU_BLOCK, LRU_BLOCK),
      b_r, b_i, lam, *(() if h0 is None else h0))


def _fourier_kernel(x_ref, mod_ref, g_ref, cs_ref, ct_ref, w_ref, o_ref, w_bf_ref, *, mod_row0, t_len):
    @pl.when(pl.program_id(0) == 0)
    def _():
        w_bf_ref[...] = w_ref[...].astype(BF16)

    x = x_ref[...]
    row = mod_row0 + pl.program_id(0) if mod_row0 else 0
    h = _norm_mod(x, g_ref[...], _mod_vec(mod_ref, row, 3), _mod_vec(mod_ref, row, 4)).astype(BF16)
    cos_parts, sin_parts = [], []
    for g in range(FOURIER_GROUPS):
        ab = jnp.dot(h[:, g * GROUP_W:(g + 1) * GROUP_W], cs_ref[...], preferred_element_type=F32)
        cos_parts.append(ab[:, :GROUP_W])
        sin_parts.append(ab[:, GROUP_W:])
    cos_all = jnp.concatenate(cos_parts, axis=1).astype(BF16)
    sin_all = jnp.concatenate(sin_parts, axis=1).astype(BF16)
    f_parts = []
    for q in range(x.shape[0] // t_len):
        rows = slice(q * t_len, (q + 1) * t_len)
        stacked = jnp.concatenate([cos_all[rows], sin_all[rows]], axis=0)
        f_parts.append(jnp.dot(ct_ref[...], stacked, preferred_element_type=F32))
    f = jnp.concatenate(f_parts, axis=0) * ((t_len * GROUP_W) ** -0.5)
    y = jnp.dot(f.astype(BF16), w_bf_ref[...], preferred_element_type=F32)
    o_ref[...] = x + _mod_vec(mod_ref, row, 5) * y


def _dft_tables(t_len):
    def cos_sin(n):
        jk = np.outer(np.arange(n), np.arange(n)) % n
        ang = 2.0 * np.pi * jk.astype(np.float64) / n
        return np.cos(ang), np.sin(ang)

    cc, sc = cos_sin(GROUP_W)
    ct, st = cos_sin(t_len)
    chan = jnp.asarray(np.concatenate([cc, sc], axis=1).astype(np.float32)).astype(BF16)
    time = jnp.asarray(np.concatenate([ct, -st], axis=1).astype(np.float32)).astype(BF16)
    return chan, time


def _fourier(x, mod, g, w_out, n_seq, t_len, tok_blk0, mod_row0, seq_per_step):
    assert seq_per_step == 1 or mod_row0 == 0
    chan, time = _dft_tables(t_len)
    rows = seq_per_step * t_len
    seq = lambda s: (tok_blk0 + s, 0)
    const = lambda s: (0, 0)
    return pl.pallas_call(
        functools.partial(_fourier_kernel, mod_row0=mod_row0, t_len=t_len),
        grid=(n_seq // seq_per_step,),
        in_specs=[
            pl.BlockSpec((rows, D_MODEL), seq),
            _mod_spec(mod),
            pl.BlockSpec((1, D_MODEL), const),
            _resident((GROUP_W, 2 * GROUP_W), const),
            _resident((t_len, 2 * t_len), const),
            _resident((D_MODEL, D_MODEL), const),
        ],
        out_specs=pl.BlockSpec((rows, D_MODEL), seq),
        out_shape=jax.ShapeDtypeStruct((N_TOK, D_MODEL), F32),
        input_output_aliases={0: 0},
        scratch_shapes=[pltpu.VMEM((D_MODEL, D_MODEL), BF16)],
        compiler_params=_cparams(1),
        name="fourier_mixer",
    )(x, mod, g.reshape(1, D_MODEL), chan, time, w_out)


def _cache_layout(t):
    return jnp.transpose(t.reshape(BATCH, 1, NA_HEADS, HEAD_DIM, SEQ), (0, 1, 4, 2, 3))


def kernel(x_prompt, x_sample, cache_k, cache_v, state_lru_fwd, state_lru_bwd, c, c_ctx, w_ada, b_ada, norm_g, ffn1_gate, ffn1_up, ffn1_down, ffn2_gate, ffn2_up, ffn2_down, w_in, q_norm_g, k_norm_g, rpb, conv_w, conv_b, lru_w_r, lru_b_r, lru_w_i, lru_b_i, lru_lambda, w_out_ab, w_out_c):
    assert DEPTH == 2, "one neighbourhood/RG-LRU layer followed by one Fourier layer"
    c_ctx2 = c_ctx.reshape(1, D_MODEL)
    ada = (c_ctx2, c, w_ada, b_ada)
    mod0_ffn1 = _adaln(ada + (0, 0, 3 * D_MODEL), 3)

    ffn1 = (ffn1_gate, ffn1_up, ffn1_down)
    ffn2 = (ffn2_gate, ffn2_up, ffn2_down)

    x, mod0_rest = _ffn((x_prompt.reshape(N_CTX_TOK, D_MODEL), x_sample.reshape(N_SMP_TOK, D_MODEL)),
                        mod0_ffn1, norm_g[0, 0], *ffn1, 0, 0,
                        adaln_next=ada + (0, 3 * D_MODEL, (N_MOD - 3) * D_MODEL))
    q, k, v, xb, gb, new_k, new_v = _proj(x, mod0_rest, 0, norm_g[0, 1], w_in[0], q_norm_g[0], k_norm_g[0])
    o_smp, mod1 = _na_attn(q, k, v,
                           jnp.transpose(cache_k[:, 0], (0, 2, 3, 1)).reshape(DEC_BATCH, NA_WIDTH, PAST_LEN),
                           jnp.transpose(cache_v[:, 0], (0, 2, 3, 1)).reshape(DEC_BATCH, NA_WIDTH, PAST_LEN),
                           rpb[0], adaln_next=ada + (1, 0, MOD_WIDTH))
    lru_prm = (conv_w[0], conv_b[0], lru_w_r[0], lru_b_r[0], lru_w_i[0], lru_b_i[0], lru_lambda[0])
    o_ctx, yb_ctx, new_hf, new_hb = _lru(xb, gb, *lru_prm, None, BATCH, SEQ, 0, CTX_SEQ_PER_STEP,
                                         attention=(q, k, v))
    yb_smp, _, _ = _lru(xb, gb, *lru_prm, (state_lru_fwd, state_lru_bwd),
                        DEC_BATCH, DEC_SEQ, N_CTX_TOK // DEC_SEQ, 1)
    (x,) = _ffn((x,), mod0_rest, norm_g[0, 2], *ffn2, 0, 3, mixer_out=(o_ctx, o_smp, yb_ctx, yb_smp, w_out_ab))

    (x,) = _ffn((x,), mod1, norm_g[1, 0], *ffn1, 1, 0)
    x = _fourier(x, mod1, norm_g[1, 1], w_out_c[0], BATCH, SEQ, 0, 0, CTX_SEQ_PER_STEP)
    x = _fourier(x, mod1, norm_g[1, 1], w_out_c[0], DEC_BATCH, DEC_SEQ, N_CTX_TOK // DEC_SEQ, 1, 1)
    y_prompt, y_sample = _ffn((x,), mod1, norm_g[1, 2], *ffn2, 1, 6, split_out=True)

    return (y_prompt.reshape(BATCH, SEQ, D_MODEL), y_sample.reshape(DEC_BATCH, DEC_SEQ, D_MODEL),
            _cache_layout(new_k), _cache_layout(new_v),
            new_hf, new_hb)
```

```python
import functools

import numpy as np
import jax
import jax.numpy as jnp
from jax import lax
from jax.experimental import pallas as pl
from jax.experimental.pallas import tpu as pltpu

F32 = jnp.float32
BF16 = jnp.bfloat16

D_MODEL = 1024
BATCH = 16
SEQ = 256
DEPTH = 2
DEC_BATCH = 2
DEC_SEQ = 1024
PAST_LEN = 256
GRID_W = 64
HEAD_DIM = 64
NA_WIDTH = 512
NA_HEADS = 8
WIN_H = 8
WIN_W = 16
LRU_WIDTH = 512
LRU_BLOCKS = 8
LRU_BLOCK = 64
LRU_C = 8.0
LRU_SUB = 256
CONV_W = 4
FOURIER_GROUPS = 4
GROUP_W = D_MODEL // FOURIER_GROUPS
D_FF = 2816
N_MOD = 9
IN_WIDTH = 3 * NA_WIDTH + 2 * LRU_WIDTH
EPS = 1e-6

N_CTX_TOK = BATCH * SEQ
N_SMP_TOK = DEC_BATCH * DEC_SEQ
N_TOK = N_CTX_TOK + N_SMP_TOK
MOD_ROWS = 8
MOD_WIDTH = N_MOD * D_MODEL
ROWS = DEC_SEQ // GRID_W
KH = min(WIN_H, ROWS)

TOKEN_TILE = 512
N_CTX_TILES = N_CTX_TOK // TOKEN_TILE
CTX_SEQ_PER_STEP = 2
CTX_ATTN_SEQ_PER_STEP = 4
FFN_TILE = 512
FF_TILE = 256
FF_CHUNKS = D_FF // FF_TILE
FF_STAGE_SLOTS = 2
SUBLANES = 8
LANES = 128
VMEM_LIMIT = 56 * 1024 * 1024

NA_Q_ROWS = 4
NA_GROUPS = ROWS // NA_Q_ROWS
NA_K_ROWS = 12
NA_Q = NA_Q_ROWS * GRID_W
NA_K = NA_K_ROWS * GRID_W
N_DR = 2 * WIN_H - 1
N_DC = 2 * WIN_W - 1
N_DR_PAIRS = N_DR + 1


def _cparams(n_axes):
    return pltpu.CompilerParams(
        dimension_semantics=("arbitrary",) * n_axes, vmem_limit_bytes=VMEM_LIMIT)


def _resident(block_shape, index_map):
    return pl.BlockSpec(block_shape, index_map, pipeline_mode=pl.Buffered(1))


def _mod_spec(mod):
    return _resident(mod.shape, lambda i: (0, 0))


def _gain_spec(sub):
    return pl.BlockSpec((None, DEPTH, D_MODEL), lambda *ids: (sub, 0, 0))


def _mod_row_of_tile(i, tile=TOKEN_TILE):
    n_ctx_tiles = N_CTX_TOK // tile
    tiles_per_seq = DEC_SEQ // tile
    return jnp.where(i < n_ctx_tiles, 0, 1 + (i - n_ctx_tiles) // tiles_per_seq)


def _mod_vec(mod_ref, row, k):
    return mod_ref[pl.ds(row, 1), k * D_MODEL:(k + 1) * D_MODEL]


def _norm_mod(x, g, shift, scale):
    ms = jnp.mean(x * x, axis=-1, keepdims=True)
    return (x * lax.rsqrt(ms + EPS)) * (g * (1.0 + scale)) + shift


def _adaln_slab(cctx_ref, c_ref, w_ref, b_ref, cond_ref, layer):
    cond_ref[...] = jnp.zeros_like(cond_ref)
    cond_ref[0:1, :] = cctx_ref[...]
    cond_ref[1:1 + DEC_BATCH, :] = c_ref[...]
    cond = cond_ref[...]
    s = (cond * jax.nn.sigmoid(cond)).astype(BF16)
    return jnp.dot(s, w_ref[...].astype(BF16), preferred_element_type=F32) + b_ref[layer:layer + 1, :]


def _adaln_specs(job, n_steps, step=lambda i: i):
    layer, col0, n_cols = job[4:]
    slab = n_cols // n_steps
    assert n_cols % n_steps == 0 and slab % LANES == 0 and col0 % slab == 0
    blk0 = col0 // slab
    return ([pl.BlockSpec((1, D_MODEL), lambda *ids: (0, 0)),
             pl.BlockSpec((DEC_BATCH, D_MODEL), lambda *ids: (0, 0)),
             pl.BlockSpec((None, D_MODEL, slab), lambda *ids: (layer, 0, blk0 + step(*ids))),
             pl.BlockSpec((DEPTH, slab), lambda *ids: (0, blk0 + step(*ids)))],
            pl.BlockSpec((MOD_ROWS, slab), lambda *ids: (0, step(*ids))),
            jax.ShapeDtypeStruct((MOD_ROWS, n_cols), F32))


def _adaln_kernel(cctx_ref, c_ref, w_ref, b_ref, o_ref, cond_ref, *, layer):
    o_ref[...] = _adaln_slab(cctx_ref, c_ref, w_ref, b_ref, cond_ref, layer)


def _adaln(job, n_steps):
    in_specs, out_spec, out_shape = _adaln_specs(job, n_steps)
    return pl.pallas_call(
        functools.partial(_adaln_kernel, layer=job[4]),
        grid=(n_steps,),
        in_specs=in_specs,
        out_specs=out_spec,
        out_shape=out_shape,
        scratch_shapes=[pltpu.VMEM((MOD_ROWS, D_MODEL), F32)],
        compiler_params=_cparams(1),
        name="adaln",
    )(*job[:4])


def _ffn_weight_copy(w_hbm, stage_ref, sem_ref, layer, j, ff_axis):
    ff = pl.ds(j * FF_TILE, FF_TILE)
    src = w_hbm.at[layer, :, ff] if ff_axis == 1 else w_hbm.at[layer, ff, :]
    slot = j % FF_STAGE_SLOTS
    return pltpu.make_async_copy(src, stage_ref.at[slot], sem_ref.at[slot])


def _mixer_out_copy(w_hbm, stage_ref, sem_ref, j):
    rows = stage_ref.shape[1]
    slot = j % stage_ref.shape[0]
    return pltpu.make_async_copy(w_hbm.at[0, pl.ds(j * rows, rows), :], stage_ref.at[slot], sem_ref.at[slot])


def _ffn_kernel(*refs, layer, mod_base, split_in, split_out, mixer_out, adaln_next):
    refs = list(refs)
    take = lambda n: [refs.pop(0) for _ in range(n)]
    x_refs = take(2 if split_in else 1)
    mix_refs = take(4) if mixer_out else None
    mod_ref, g_ref = take(2)
    wo_hbm = take(1)[0] if mixer_out else None
    ada_refs = take(4) if adaln_next is not None else None
    wg_hbm, wu_hbm, wd_hbm = take(3)
    o_refs = take(2 if split_out else 1)
    modn_ref = take(1)[0] if adaln_next is not None else None
    wg_bf, wu_bf, wd_bf, stg_g, stg_u, stg_d, sem_g, sem_u, sem_d = take(9)
    cond_scr = take(1)[0] if adaln_next is not None else None
    streams = ((wg_hbm, stg_g, sem_g, wg_bf, 1), (wu_hbm, stg_u, sem_u, wu_bf, 1),
               (wd_hbm, stg_d, sem_d, wd_bf, 0))

    i = pl.program_id(0)
    is_ctx = i < N_CTX_TOK // FFN_TILE
    if split_in:
        x = jnp.where(is_ctx, x_refs[0][...], x_refs[1][...])
    else:
        x = x_refs[0][...]
    row = _mod_row_of_tile(i, FFN_TILE)

    if mixer_out:
        wo_bf, stg_o, sem_o = take(3)
        rows = stg_o.shape[1]
        n_chunks = D_MODEL // rows

        @pl.when(i == 0)
        def _():
            for j in range(stg_o.shape[0]):
                _mixer_out_copy(wo_hbm, stg_o, sem_o, j).start()
            for j in range(n_chunks):
                _mixer_out_copy(wo_hbm, stg_o, sem_o, j).wait()
                wo_bf[j * rows:(j + 1) * rows, :] = stg_o[j % stg_o.shape[0]].astype(BF16)
                if j + stg_o.shape[0] < n_chunks:
                    _mixer_out_copy(wo_hbm, stg_o, sem_o, j + stg_o.shape[0]).start()

        oc_ref, os_ref, yc_ref, ys_ref = mix_refs
        cat = jnp.concatenate([jnp.where(is_ctx, oc_ref[...], os_ref[...]),
                               jnp.where(is_ctx, yc_ref[...], ys_ref[...])], axis=1)
        x = x + _mod_vec(mod_ref, row, mod_base - 1) * jnp.dot(cat, wo_bf[...], preferred_element_type=F32)

    h = _norm_mod(x, g_ref[layer:layer + 1, :], _mod_vec(mod_ref, row, mod_base),
                  _mod_vec(mod_ref, row, mod_base + 1)).astype(BF16)

    def start_chunk(j):
        for w_hbm, stg, sem, _, ff_axis in streams:
            _ffn_weight_copy(w_hbm, stg, sem, layer, j, ff_axis).start()

    def finish_chunk(j):
        for w_hbm, stg, sem, w_bf, ff_axis in streams:
            _ffn_weight_copy(w_hbm, stg, sem, layer, j, ff_axis).wait()
            w_bf[j] = stg[j % FF_STAGE_SLOTS].astype(BF16)

    def run(stream_weights):
        if stream_weights:
            for j in range(FF_STAGE_SLOTS):
                start_chunk(j)
        acc = None
        for j in range(FF_CHUNKS):
            if stream_weights:
                finish_chunk(j)
                if j + FF_STAGE_SLOTS < FF_CHUNKS:
                    start_chunk(j + FF_STAGE_SLOTS)
            a = jnp.dot(h, wg_bf[j], preferred_element_type=F32)
            b = jnp.dot(h, wu_bf[j], preferred_element_type=F32)
            if adaln_next is not None and j == 1:
                modn_ref[...] = _adaln_slab(*ada_refs, cond_scr, adaln_next)
            act = (a * jax.nn.sigmoid(a) * b).astype(BF16)
            y = jnp.dot(act, wd_bf[j], preferred_element_type=F32)
            acc = y if acc is None else acc + y
        res = x + 0.5 * _mod_vec(mod_ref, row, mod_base + 2) * acc
        if split_out:
            @pl.when(is_ctx)
            def _():
                o_refs[0][...] = res

            @pl.when(jnp.logical_not(is_ctx))
            def _():
                o_refs[1][...] = res
        else:
            o_refs[0][...] = res

    @pl.when(i == 0)
    def _():
        run(True)

    @pl.when(i > 0)
    def _():
        run(False)


def _ffn(xs, mod, g, wg, wu, wd, layer, mod_base, split_out=False, mixer_out=None, adaln_next=None):
    tm = FFN_TILE
    n_ctx_tiles = N_CTX_TOK // tm
    split_in = len(xs) == 2

    def tiles(width):
        return (pl.BlockSpec((tm, width), lambda i: (i, 0)),
                pl.BlockSpec((tm, width), lambda i: (jnp.minimum(i, n_ctx_tiles - 1), 0)),
                pl.BlockSpec((tm, width), lambda i: (jnp.maximum(i - n_ctx_tiles, 0), 0)))

    tok, ctx_tok, smp_tok = tiles(D_MODEL)
    full = jax.ShapeDtypeStruct((N_TOK, D_MODEL), F32)
    pair = [jax.ShapeDtypeStruct((N_CTX_TOK, D_MODEL), F32), jax.ShapeDtypeStruct((N_SMP_TOK, D_MODEL), F32)]
    hbm = pl.BlockSpec(memory_space=pl.ANY)
    in_specs = [ctx_tok, smp_tok] if split_in else [tok]
    operands = list(xs)
    scratch = [
        pltpu.VMEM((FF_CHUNKS, D_MODEL, FF_TILE), BF16),
        pltpu.VMEM((FF_CHUNKS, D_MODEL, FF_TILE), BF16),
        pltpu.VMEM((FF_CHUNKS, FF_TILE, D_MODEL), BF16),
        pltpu.VMEM((FF_STAGE_SLOTS, D_MODEL, FF_TILE), F32),
        pltpu.VMEM((FF_STAGE_SLOTS, D_MODEL, FF_TILE), F32),
        pltpu.VMEM((FF_STAGE_SLOTS, FF_TILE, D_MODEL), F32),
        pltpu.SemaphoreType.DMA((FF_STAGE_SLOTS,)),
        pltpu.SemaphoreType.DMA((FF_STAGE_SLOTS,)),
        pltpu.SemaphoreType.DMA((FF_STAGE_SLOTS,)),
    ]
    if mixer_out is not None:
        _, ctx_half, smp_half = tiles(NA_WIDTH)
        in_specs += [ctx_half, smp_half, ctx_half, smp_half]
        operands += list(mixer_out[:4])
    in_specs += [_mod_spec(mod), _gain_spec(g[1])]
    operands += [mod, g[0]]
    out_specs = [ctx_tok, smp_tok] if split_out else [tok]
    out_shape = pair if split_out else [full]
    if adaln_next is not None:
        ada_in, ada_out, ada_shape = _adaln_specs(adaln_next, N_TOK // tm)
        in_specs += ada_in
        operands += list(adaln_next[:4])
        out_specs.append(ada_out)
        out_shape.append(ada_shape)
        scratch.append(pltpu.VMEM((MOD_ROWS, D_MODEL), F32))
    if mixer_out is not None:
        in_specs.insert(len(in_specs) - (4 if adaln_next is not None else 0), hbm)
        operands.insert(len(operands) - (4 if adaln_next is not None else 0), mixer_out[4])
        scratch += [
            pltpu.VMEM((D_MODEL, D_MODEL), BF16),
            pltpu.VMEM((FF_STAGE_SLOTS, FF_TILE, D_MODEL), F32),
            pltpu.SemaphoreType.DMA((FF_STAGE_SLOTS,)),
        ]
    return pl.pallas_call(
        functools.partial(_ffn_kernel, layer=layer, mod_base=mod_base, split_in=split_in,
                          split_out=split_out, mixer_out=mixer_out is not None,
                          adaln_next=None if adaln_next is None else adaln_next[4]),
        grid=(N_TOK // tm,),
        in_specs=in_specs + [hbm, hbm, hbm],
        out_specs=out_specs,
        out_shape=out_shape,
        scratch_shapes=scratch,
        compiler_params=_cparams(1),
        name="ffn",
    )(*operands, wg, wu, wd)


def _head_rms_norm(z, g, ones_bd):
    z2 = z * z
    hi = z2.astype(BF16)
    lo = (z2 - hi.astype(F32)).astype(BF16)
    n = ones_bd.shape[0]
    parts = []
    for c in range(z.shape[1] // n):
        sl = slice(c * n, (c + 1) * n)
        parts.append(jnp.dot(hi[:, sl], ones_bd, preferred_element_type=F32)
                     + jnp.dot(lo[:, sl], ones_bd, preferred_element_type=F32))
    ss = jnp.concatenate(parts, axis=1)
    return z * lax.rsqrt(ss * (1.0 / HEAD_DIM) + EPS) * g


def _proj_kernel(*refs, layer, mod_k0, adaln_layer):
    refs = list(refs)
    x_ref, mod_ref, g_ref, w_ref, qg_ref, kg_ref, ones_ref = refs[:7]
    ada_refs = refs[7:11] if adaln_layer is not None else None
    n_in = 7 if adaln_layer is None else 11
    q_ref, k_ref, v_ref, xb_ref, gb_ref, kout_ref, vout_ref = refs[n_in:n_in + 7]
    rest = refs[n_in + 7:]
    modn_ref = rest.pop(0) if adaln_layer is not None else None
    w_bf_ref = rest.pop(0)
    i = pl.program_id(0)

    @pl.when(i == 0)
    def _():
        w_bf_ref[...] = w_ref[...].astype(BF16)

    x = x_ref[...]
    row = _mod_row_of_tile(i)
    h = _norm_mod(x, g_ref[layer:layer + 1, :], _mod_vec(mod_ref, row, mod_k0),
                  _mod_vec(mod_ref, row, mod_k0 + 1)).astype(BF16)

    def proj(part):
        return jnp.dot(h, w_bf_ref[:, part * NA_WIDTH:(part + 1) * NA_WIDTH], preferred_element_type=F32)

    ones_bd = ones_ref[...]
    q_raw = proj(0)
    k_raw = proj(1)
    q = _head_rms_norm(q_raw, jnp.tile(qg_ref[...], (1, NA_HEADS)), ones_bd) * (HEAD_DIM ** -0.5)
    q_ref[...] = q.astype(BF16)
    v = proj(2)
    if adaln_layer is not None:
        modn_ref[...] = _adaln_slab(*ada_refs, rest.pop(0), adaln_layer)
    k = _head_rms_norm(k_raw, jnp.tile(kg_ref[...], (1, NA_HEADS)), ones_bd)
    k_ref[...] = k.astype(BF16)
    xb = proj(3)
    v_ref[...] = v.astype(BF16)
    gb = proj(4)
    xb_ref[...] = xb
    gb_ref[...] = gb

    kt = [k[b * SEQ:(b + 1) * SEQ, :].T for b in range(TOKEN_TILE // SEQ)]
    vt = [v[b * SEQ:(b + 1) * SEQ, :].T for b in range(TOKEN_TILE // SEQ)]

    @pl.when(i < N_CTX_TILES)
    def _():
        for b in range(TOKEN_TILE // SEQ):
            kout_ref[b] = kt[b]
            vout_ref[b] = vt[b]


def _proj(x, mod, mod_k0, g, w_in, q_g, k_g, adaln_next=None):
    tm = TOKEN_TILE
    head = np.arange(2 * LANES) // HEAD_DIM
    ones_bd = jnp.asarray((head[:, None] == head[None, :]).astype(np.float32), dtype=BF16)
    tok = lambda i: (i, 0)
    const = lambda i: (0, 0)
    act_f32 = jax.ShapeDtypeStruct((N_TOK, NA_WIDTH), F32)
    act_bf16 = jax.ShapeDtypeStruct((N_TOK, NA_WIDTH), BF16)
    cache = jax.ShapeDtypeStruct((BATCH, NA_WIDTH, SEQ), F32)
    cache_spec = pl.BlockSpec((tm // SEQ, NA_WIDTH, SEQ), lambda i: (jnp.minimum(i, N_CTX_TILES - 1), 0, 0))
    in_specs = [
        pl.BlockSpec((tm, D_MODEL), tok),
        _mod_spec(mod),
        _gain_spec(g[1]),
        _resident((D_MODEL, IN_WIDTH), const),
        pl.BlockSpec((1, HEAD_DIM), const),
        pl.BlockSpec((1, HEAD_DIM), const),
        _resident((2 * LANES, 2 * LANES), const),
    ]
    operands = [x, mod, g[0], w_in, q_g.reshape(1, HEAD_DIM), k_g.reshape(1, HEAD_DIM), ones_bd]
    out_specs = [pl.BlockSpec((tm, NA_WIDTH), tok)] * 5 + [cache_spec, cache_spec]
    out_shape = [act_bf16, act_bf16, act_bf16, act_f32, act_f32, cache, cache]
    scratch = [pltpu.VMEM((D_MODEL, IN_WIDTH), BF16)]
    if adaln_next is not None:
        ada_in, ada_out, ada_shape = _adaln_specs(adaln_next, N_TOK // tm)
        in_specs += ada_in
        operands += list(adaln_next[:4])
        out_specs.append(ada_out)
        out_shape.append(ada_shape)
        scratch.append(pltpu.VMEM((MOD_ROWS, D_MODEL), F32))
    return pl.pallas_call(
        functools.partial(_proj_kernel, layer=g[2], mod_k0=mod_k0,
                          adaln_layer=None if adaln_next is None else adaln_next[4]),
        grid=(N_TOK // tm,),
        in_specs=in_specs,
        out_specs=out_specs,
        out_shape=out_shape,
        scratch_shapes=scratch,
        compiler_params=_cparams(1),
        name="mixer_in_proj",
    )(*operands)


def _head_masks():
    lane = lax.broadcasted_iota(jnp.int32, (1, 2 * HEAD_DIM), 1)
    return [lane < HEAD_DIM, lane >= HEAD_DIM]


def _ctx_attn_kernel(q_ref, k_ref, v_ref, o_ref):
    masks = _head_masks()
    units = [(b, h) for b in range(q_ref.shape[0] // SEQ) for h in range(NA_HEADS)]

    def where(b, h):
        return slice(b * SEQ, (b + 1) * SEQ), slice(2 * HEAD_DIM * (h // 2), 2 * HEAD_DIM * (h // 2 + 1))

    def scores(b, h):
        rows, sl = where(b, h)
        q2 = q_ref[rows, sl]
        qm = jnp.where(masks[h % 2], q2, jnp.zeros_like(q2))
        return jnp.dot(qm, k_ref[rows, sl].T, preferred_element_type=F32)

    def attend(b, h, s):
        rows, sl = where(b, h)
        pe = jnp.exp(s - jnp.max(s, axis=-1, keepdims=True))
        den = jnp.sum(pe, axis=-1, keepdims=True)
        return jnp.dot(pe.astype(BF16), v_ref[rows, sl], preferred_element_type=F32) / den

    pending = scores(*units[0])
    out = None
    for n, (b, h) in enumerate(units):
        current = pending
        if n + 1 < len(units):
            pending = scores(*units[n + 1])
        o = attend(b, h, current)
        if h % 2 == 0:
            out = o
        else:
            rows, sl = where(b, h)
            o_ref[rows, sl] = jnp.where(masks[1], o, out).astype(BF16)


def _ctx_attn(q, k, v):
    blk = pl.BlockSpec((CTX_ATTN_SEQ_PER_STEP * SEQ, NA_WIDTH), lambda b: (b, 0))
    return pl.pallas_call(
        _ctx_attn_kernel,
        grid=(BATCH // CTX_ATTN_SEQ_PER_STEP,),
        in_specs=[blk, blk, blk],
        out_specs=blk,
        out_shape=jax.ShapeDtypeStruct((N_CTX_TOK, NA_WIDTH), BF16),
        compiler_params=_cparams(1),
        name="ctx_attention",
    )(q, k, v)


def _na_build_bias_table(rpb_ref, table_ref):
    qc = lax.broadcasted_iota(jnp.int32, (GRID_W, LANES), 0)
    lane = lax.broadcasted_iota(jnp.int32, (GRID_W, LANES), 1)
    kc = lane % GRID_W
    col_start = jnp.clip(qc - WIN_W // 2, 0, GRID_W - WIN_W)
    col_in = (kc >= col_start) & (kc < col_start + WIN_W)
    neg = jnp.full((GRID_W, LANES), -jnp.inf, F32)

    def toeplitz(h, dr, lane0):
        if dr < 0 or dr >= N_DR:
            return neg
        row = jnp.pad(rpb_ref[dr, h:h + 1, :], ((0, 0), (0, LANES - N_DC)))
        w = jnp.broadcast_to(row, (GRID_W, LANES))
        return pltpu.roll(w, (lane0 - (WIN_W - 1)) % LANES, 1, stride=1, stride_axis=0)

    for h in range(NA_HEADS):
        for i in range(N_DR_PAIRS):
            t = jnp.where(lane < GRID_W, toeplitz(h, i - 1, 0), toeplitz(h, i, GRID_W))
            table_ref[h, i] = jnp.where(col_in, t, neg)


def _na_kernel(*refs, adaln_layer):
    q_ref, k_ref, v_ref, kc_ref, vc_ref, rpb_ref = refs[:6]
    if adaln_layer is None:
        o_ref, table_ref = refs[6:]
    else:
        ada_refs = refs[6:10]
        o_ref, modn_ref, table_ref, cond_scr = refs[10:]
    b = pl.program_id(0)
    g = pl.program_id(1)

    @pl.when((b == 0) & (g == 0))
    def _():
        _na_build_bias_table(rpb_ref, table_ref)

    win_row0 = jnp.where(g < NA_GROUPS // 2, 0, ROWS - NA_K_ROWS)
    start = pl.multiple_of(win_row0 * GRID_W, GRID_W)
    q_row = g * NA_Q_ROWS + lax.broadcasted_iota(jnp.int32, (NA_Q, 1), 0) // GRID_W
    k_row = win_row0 + lax.broadcasted_iota(jnp.int32, (1, NA_K), 1) // GRID_W
    row_start = jnp.clip(q_row - KH // 2, 0, ROWS - KH)
    row_in = (k_row >= row_start) & (k_row < row_start + KH)
    masks = _head_masks()

    def pair_slab(p):
        return slice(2 * HEAD_DIM * p, 2 * HEAD_DIM * (p + 1))

    def scores(head):
        p, e = divmod(head, 2)
        sl = pair_slab(p)
        q2 = q_ref[:, sl]
        klt = k_ref[pl.ds(start, NA_K), sl].T
        kct = kc_ref[0, sl, :].astype(BF16)
        bias_rows = []
        for a in range(NA_Q_ROWS):
            tiles = []
            for m in range(NA_K_ROWS // 2):
                dr = win_row0 + 2 * m - (g * NA_Q_ROWS + a) + (WIN_H - 1)
                tiles.append(table_ref[head, jnp.clip(dr + 1, 0, N_DR_PAIRS - 1)])
            bias_rows.append(jnp.concatenate(tiles, axis=1))
        bias = jnp.concatenate(bias_rows, axis=0)
        qm = jnp.where(masks[e], q2, jnp.zeros_like(q2))
        s_loc = jnp.where(row_in, jnp.dot(qm, klt, preferred_element_type=F32) + bias, -jnp.inf)
        s_ctx = jnp.dot(qm, kct, preferred_element_type=F32)
        return s_loc, s_ctx

    def attend(head, s_loc, s_ctx):
        sl = pair_slab(head // 2)
        vl = v_ref[pl.ds(start, NA_K), sl]
        vct = vc_ref[0, sl, :].astype(BF16)
        m_max = jnp.maximum(jnp.max(s_loc, axis=-1, keepdims=True),
                            jnp.max(s_ctx, axis=-1, keepdims=True))
        p_loc = jnp.exp(s_loc - m_max)
        p_ctx = jnp.exp(s_ctx - m_max)
        den = jnp.sum(p_loc, axis=-1, keepdims=True) + jnp.sum(p_ctx, axis=-1, keepdims=True)
        return (jnp.dot(p_loc.astype(BF16), vl, preferred_element_type=F32)
                + lax.dot_general(p_ctx.astype(BF16), vct, (((1,), (1,)), ((), ())),
                                  preferred_element_type=F32)) / den

    pending = scores(0)
    out = None
    for head in range(NA_HEADS):
        current = pending
        if head + 1 < NA_HEADS:
            pending = scores(head + 1)
        o = attend(head, *current)
        if adaln_layer is not None and head == 0:
            modn_ref[...] = _adaln_slab(*ada_refs, cond_scr, adaln_layer)
        if head % 2 == 0:
            out = o
        else:
            o_ref[:, pair_slab(head // 2)] = jnp.where(masks[1], o, out).astype(BF16)


def _na_attn(q, k, v, k_ctx, v_ctx, rpb_e, adaln_next=None):
    smp_blk0 = N_CTX_TOK // DEC_SEQ
    q_blk0 = N_CTX_TOK // NA_Q
    kv = pl.BlockSpec((DEC_SEQ, NA_WIDTH), lambda b, g: (smp_blk0 + b, 0))
    ctx = pl.BlockSpec((1, NA_WIDTH, PAST_LEN), lambda b, g: (b, 0, 0))
    in_specs = [
        pl.BlockSpec((NA_Q, NA_WIDTH), lambda b, g: (q_blk0 + b * NA_GROUPS + g, 0)),
        kv, kv, ctx, ctx,
        pl.BlockSpec((N_DR, NA_HEADS, N_DC), lambda b, g: (0, 0, 0)),
    ]
    operands = [q, k, v, k_ctx, v_ctx, jnp.transpose(rpb_e, (1, 0, 2))]
    out_specs = [pl.BlockSpec((NA_Q, NA_WIDTH), lambda b, g: (b * NA_GROUPS + g, 0))]
    out_shape = [jax.ShapeDtypeStruct((N_SMP_TOK, NA_WIDTH), BF16)]
    scratch = [pltpu.VMEM((NA_HEADS, N_DR_PAIRS, GRID_W, LANES), F32)]
    if adaln_next is not None:
        ada_in, ada_out, ada_shape = _adaln_specs(adaln_next, DEC_BATCH * NA_GROUPS,
                                                  lambda b, g: b * NA_GROUPS + g)
        in_specs += ada_in
        operands += list(adaln_next[:4])
        out_specs.append(ada_out)
        out_shape.append(ada_shape)
        scratch.append(pltpu.VMEM((MOD_ROWS, D_MODEL), F32))
    return pl.pallas_call(
        functools.partial(_na_kernel, adaln_layer=None if adaln_next is None else adaln_next[4]),
        grid=(DEC_BATCH, NA_GROUPS),
        in_specs=in_specs,
        out_specs=out_specs,
        out_shape=out_shape,
        scratch_shapes=scratch,
        compiler_params=_cparams(2),
        name="neighbourhood_attention",
    )(*operands)


def _gelu_tanh(x):
    c0 = float(np.sqrt(2.0 / np.pi))
    inner = x * (c0 + (c0 * 0.044715) * (x * x))
    return (0.5 * x) * (1.0 + jnp.tanh(inner))


def _lru_build_gate_weights(wr_ref, wi_ref, w_scr):
    blocks_per_group = LRU_SUB // LRU_BLOCK
    w_scr[...] = jnp.zeros_like(w_scr)
    for d in range(2):
        for kind, w_ref in enumerate((wr_ref, wi_ref)):
            col0 = (2 * d + kind) * LRU_SUB
            for blk in range(LRU_BLOCKS):
                c, n = divmod(blk, blocks_per_group)
                r0 = n * LRU_BLOCK
                w_scr[c, r0:r0 + LRU_BLOCK, col0 + r0:col0 + r0 + LRU_BLOCK] = (
                    0.5 * w_ref[d * LRU_BLOCKS + blk]).astype(BF16)


def _lru_kernel(*refs, zero_state, t_len):
    refs = list(refs)
    xb_ref, gb_ref, cw_ref, cb_ref, wr_ref, wi_ref, br_ref, bi_ref, lam_ref = refs[:9]
    h0f_ref, h0b_ref = (None, None) if zero_state else refs[9:11]
    y_ref, hlf_ref, hlb_ref, af_ref, uf_ref, ab_ref, ub_ref, w_scr = refs[-8:]
    width = xb_ref.shape[1]
    n_seq = xb_ref.shape[0] // t_len
    n_blk = t_len // SUBLANES
    row = lax.broadcasted_iota(jnp.int32, (t_len, 1), 0)
    in_block = lax.broadcasted_iota(jnp.int32, (1, SUBLANES, 1), 1)

    @pl.when(pl.program_id(0) == 0)
    def _():
        _lru_build_gate_weights(wr_ref, wi_ref, w_scr)

    def shifted(z, s):
        rolled = pltpu.roll(z, (-s) % t_len, axis=0)
        ok = (row + s >= 0) & (row + s < t_len)
        return jnp.where(ok, rolled, 0.0)

    left = (CONV_W - 1) // 2
    for q, c in [(q, c) for q in range(n_seq) for c in range(width // LRU_SUB)]:
        rows = slice(q * t_len, (q + 1) * t_len)
        cs = slice(c * LRU_SUB, (c + 1) * LRU_SUB)
        x = xb_ref[rows, cs]
        xc = cb_ref[:, cs]
        for j in range(CONV_W):
            tap = x if j == left else shifted(x, j - left)
            xc = xc + tap * cw_ref[j:j + 1, cs]
        half_gates = jnp.dot(xc.astype(BF16), w_scr[c], preferred_element_type=F32)
        half_xc = 0.5 * xc
        for d, (a_ref, u_ref) in enumerate(((af_ref, uf_ref), (ab_ref, ub_ref))):
            t_r = jnp.tanh(half_gates[:, (2 * d) * LRU_SUB:(2 * d + 1) * LRU_SUB] + 0.5 * br_ref[d:d + 1, cs])
            t_i = jnp.tanh(half_gates[:, (2 * d + 1) * LRU_SUB:(2 * d + 2) * LRU_SUB] + 0.5 * bi_ref[d:d + 1, cs])
            lam = lam_ref[d:d + 1, cs]
            log_sig = jnp.minimum(lam, 0.0) - jnp.log1p(jnp.exp(-jnp.abs(lam)))
            half_c_log_sig = (0.5 * LRU_C) * log_sig
            log_a = t_r * half_c_log_sig + half_c_log_sig
            a = jnp.exp(log_a)
            var = -jnp.tanh(log_a) * (a * a + 1.0)
            u = jnp.where(var > 0.0, var * lax.rsqrt(var), 0.0) * ((t_i + 1.0) * half_xc)
            a = a.reshape(n_blk, SUBLANES, LRU_SUB)
            u = u.reshape(n_blk, SUBLANES, LRU_SUB)
            step = 1
            while step < SUBLANES:
                if d == 0:
                    ok, shift = in_block >= step, step
                else:
                    ok, shift = in_block < SUBLANES - step, SUBLANES - step
                a_prev = jnp.where(ok, pltpu.roll(a, shift, axis=1), 1.0)
                u_prev = jnp.where(ok, pltpu.roll(u, shift, axis=1), 0.0)
                u = u + a * u_prev
                a = a * a_prev
                step *= 2
            a_ref[rows, cs] = a.reshape(t_len, LRU_SUB)
            u_ref[rows, cs] = u.reshape(t_len, LRU_SUB)

    def body(i, carry):
        new = []
        for q, (cf, cb) in enumerate(carry):
            f0 = pl.multiple_of(q * t_len + i * SUBLANES, SUBLANES)
            b0 = pl.multiple_of(q * t_len + (n_blk - 1 - i) * SUBLANES, SUBLANES)
            hf = uf_ref[pl.ds(f0, SUBLANES), :] + af_ref[pl.ds(f0, SUBLANES), :] * cf
            hb = ub_ref[pl.ds(b0, SUBLANES), :] + ab_ref[pl.ds(b0, SUBLANES), :] * cb
            uf_ref[pl.ds(f0, SUBLANES), :] = hf
            ub_ref[pl.ds(b0, SUBLANES), :] = hb
            new.append((jnp.broadcast_to(hf[SUBLANES - 1:SUBLANES, :], (SUBLANES, width)),
                        jnp.broadcast_to(hb[0:1, :], (SUBLANES, width))))
        return tuple(new)

    if zero_state:
        init = tuple((jnp.zeros((SUBLANES, width), F32),) * 2 for _ in range(n_seq))
    else:
        init = tuple((jnp.broadcast_to(h0f_ref[q], (SUBLANES, width)),
                      jnp.broadcast_to(h0b_ref[q], (SUBLANES, width))) for q in range(n_seq))
    last = lax.fori_loop(0, n_blk, body, init)
    for q, (cf, cb) in enumerate(last):
        hlf_ref[q] = cf[0:1, :]
        hlb_ref[q] = cb[0:1, :]
    y_ref[...] = ((uf_ref[...] + ub_ref[...]) * _gelu_tanh(gb_ref[...])).astype(BF16)


def _lru(xb, gb, conv_w, conv_b, w_r, b_r, w_i, b_i, lam, h0, n_seq, t_len, tok_blk0, seq_per_step):
    rows = seq_per_step * t_len
    const2 = lambda s: (0, 0)
    const3 = lambda s: (0, 0, 0)
    blocks = pl.BlockSpec((2 * LRU_BLOCKS, LRU_BLOCK, LRU_BLOCK), const3)
    per_dir = pl.BlockSpec((2, LRU_WIDTH), const2)
    state = pl.BlockSpec((seq_per_step, 1, LRU_WIDTH), lambda s: (s, 0, 0))
    state_shape = jax.ShapeDtypeStruct((n_seq, 1, LRU_WIDTH), F32)
    return pl.pallas_call(
        functools.partial(_lru_kernel, zero_state=h0 is None, t_len=t_len),
        grid=(n_seq // seq_per_step,),
        in_specs=[
            pl.BlockSpec((rows, LRU_WIDTH), lambda s: (tok_blk0 + s, 0)),
            pl.BlockSpec((rows, LRU_WIDTH), lambda s: (tok_blk0 + s, 0)),
            pl.BlockSpec((CONV_W, LRU_WIDTH), const2),
            pl.BlockSpec((1, LRU_WIDTH), const2),
            blocks, blocks, per_dir, per_dir, per_dir,
        ] + ([] if h0 is None else [state, state]),
        out_specs=[pl.BlockSpec((rows, LRU_WIDTH), lambda s: (s, 0)), state, state],
        out_shape=[jax.ShapeDtypeStruct((n_seq * t_len, LRU_WIDTH), BF16), state_shape, state_shape],
        scratch_shapes=[pltpu.VMEM((rows, LRU_WIDTH), F32)] * 4
        + [pltpu.VMEM((LRU_WIDTH // LRU_SUB, LRU_SUB, 4 * LRU_SUB), BF16)],
        compiler_params=_cparams(1),
        name="rglru",
    )(xb, gb, conv_w, conv_b.reshape(1, LRU_WIDTH),
      w_r.reshape(2 * LRU_BLOCKS, LRU_BLOCK, LRU_BLOCK), w_i.reshape(2 * LRU_BLOCKS, LRU_BLOCK, LRU_BLOCK),
      b_r, b_i, lam, *(() if h0 is None else h0))


def _fourier_kernel(x_ref, mod_ref, g_ref, cs_ref, ct_ref, w_ref, o_ref, w_bf_ref, *, layer, mod_row0, t_len):
    @pl.when(pl.program_id(0) == 0)
    def _():
        w_bf_ref[...] = w_ref[...].astype(BF16)

    x = x_ref[...]
    row = mod_row0 + pl.program_id(0) if mod_row0 else 0
    h = _norm_mod(x, g_ref[layer:layer + 1, :], _mod_vec(mod_ref, row, 3), _mod_vec(mod_ref, row, 4)).astype(BF16)
    cos_parts, sin_parts = [], []
    for g in range(FOURIER_GROUPS):
        ab = jnp.dot(h[:, g * GROUP_W:(g + 1) * GROUP_W], cs_ref[...], preferred_element_type=F32)
        cos_parts.append(ab[:, :GROUP_W])
        sin_parts.append(ab[:, GROUP_W:])
    cos_all = jnp.concatenate(cos_parts, axis=1).astype(BF16)
    sin_all = jnp.concatenate(sin_parts, axis=1).astype(BF16)
    f_parts = []
    for q in range(x.shape[0] // t_len):
        rows = slice(q * t_len, (q + 1) * t_len)
        stacked = jnp.concatenate([cos_all[rows], sin_all[rows]], axis=0)
        f_parts.append(jnp.dot(ct_ref[...], stacked, preferred_element_type=F32))
    f = jnp.concatenate(f_parts, axis=0) * ((t_len * GROUP_W) ** -0.5)
    y = jnp.dot(f.astype(BF16), w_bf_ref[...], preferred_element_type=F32)
    o_ref[...] = x + _mod_vec(mod_ref, row, 5) * y


def _dft_tables(t_len):
    def cos_sin(n):
        jk = np.outer(np.arange(n), np.arange(n)) % n
        ang = 2.0 * np.pi * jk.astype(np.float64) / n
        return np.cos(ang), np.sin(ang)

    cc, sc = cos_sin(GROUP_W)
    ct, st = cos_sin(t_len)
    chan = jnp.asarray(np.concatenate([cc, sc], axis=1).astype(np.float32)).astype(BF16)
    time = jnp.asarray(np.concatenate([ct, -st], axis=1).astype(np.float32)).astype(BF16)
    return chan, time


def _fourier(x, mod, g, w_out, n_seq, t_len, tok_blk0, mod_row0, seq_per_step):
    assert seq_per_step == 1 or mod_row0 == 0
    chan, time = _dft_tables(t_len)
    rows = seq_per_step * t_len
    seq = lambda s: (tok_blk0 + s, 0)
    const = lambda s: (0, 0)
    return pl.pallas_call(
        functools.partial(_fourier_kernel, layer=g[2], mod_row0=mod_row0, t_len=t_len),
        grid=(n_seq // seq_per_step,),
        in_specs=[
            pl.BlockSpec((rows, D_MODEL), seq),
            _mod_spec(mod),
            _gain_spec(g[1]),
            _resident((GROUP_W, 2 * GROUP_W), const),
            _resident((t_len, 2 * t_len), const),
            _resident((D_MODEL, D_MODEL), const),
        ],
        out_specs=pl.BlockSpec((rows, D_MODEL), seq),
        out_shape=jax.ShapeDtypeStruct((N_TOK, D_MODEL), F32),
        input_output_aliases={0: 0},
        scratch_shapes=[pltpu.VMEM((D_MODEL, D_MODEL), BF16)],
        compiler_params=_cparams(1),
        name="fourier_mixer",
    )(x, mod, g[0], chan, time, w_out)


def _cache_layout(t):
    return jnp.transpose(t.reshape(BATCH, 1, NA_HEADS, HEAD_DIM, SEQ), (0, 1, 4, 2, 3))


def kernel(x_prompt, x_sample, cache_k, cache_v, state_lru_fwd, state_lru_bwd, c, c_ctx, w_ada, b_ada, norm_g, ffn1_gate, ffn1_up, ffn1_down, ffn2_gate, ffn2_up, ffn2_down, w_in, q_norm_g, k_norm_g, rpb, conv_w, conv_b, lru_w_r, lru_b_r, lru_w_i, lru_b_i, lru_lambda, w_out_ab, w_out_c):
    assert DEPTH == 2, "one neighbourhood/RG-LRU layer followed by one Fourier layer"
    c_ctx2 = c_ctx.reshape(1, D_MODEL)
    ada = (c_ctx2, c, w_ada, b_ada)
    mod0_ffn1 = _adaln(ada + (0, 0, 3 * D_MODEL), 3)

    ffn1 = (ffn1_gate, ffn1_up, ffn1_down)
    ffn2 = (ffn2_gate, ffn2_up, ffn2_down)
    norm_gt = jnp.transpose(norm_g, (1, 0, 2))
    gain = lambda layer, sub: (norm_gt, sub, layer)

    x, mod0_rest = _ffn((x_prompt.reshape(N_CTX_TOK, D_MODEL), x_sample.reshape(N_SMP_TOK, D_MODEL)),
                        mod0_ffn1, gain(0, 0), *ffn1, 0, 0,
                        adaln_next=ada + (0, 3 * D_MODEL, (N_MOD - 3) * D_MODEL))
    q, k, v, xb, gb, new_k, new_v = _proj(x, mod0_rest, 0, gain(0, 1), w_in[0], q_norm_g[0], k_norm_g[0])
    o_ctx = _ctx_attn(q, k, v)
    o_smp, mod1 = _na_attn(q, k, v,
                           jnp.transpose(cache_k[:, 0], (0, 2, 3, 1)).reshape(DEC_BATCH, NA_WIDTH, PAST_LEN),
                           jnp.transpose(cache_v[:, 0], (0, 2, 3, 1)).reshape(DEC_BATCH, NA_WIDTH, PAST_LEN),
                           rpb[0], adaln_next=ada + (1, 0, MOD_WIDTH))
    lru_prm = (conv_w[0], conv_b[0], lru_w_r[0], lru_b_r[0], lru_w_i[0], lru_b_i[0], lru_lambda[0])
    yb_ctx, new_hf, new_hb = _lru(xb, gb, *lru_prm, None, BATCH, SEQ, 0, CTX_SEQ_PER_STEP)
    yb_smp, _, _ = _lru(xb, gb, *lru_prm, (state_lru_fwd, state_lru_bwd),
                        DEC_BATCH, DEC_SEQ, N_CTX_TOK // DEC_SEQ, 1)
    (x,) = _ffn((x,), mod0_rest, gain(0, 2), *ffn2, 0, 3, mixer_out=(o_ctx, o_smp, yb_ctx, yb_smp, w_out_ab))

    (x,) = _ffn((x,), mod1, gain(1, 0), *ffn1, 1, 0)
    x = _fourier(x, mod1, gain(1, 1), w_out_c[0], BATCH, SEQ, 0, 0, CTX_SEQ_PER_STEP)
    x = _fourier(x, mod1, gain(1, 1), w_out_c[0], DEC_BATCH, DEC_SEQ, N_CTX_TOK // DEC_SEQ, 1, 1)
    y_prompt, y_sample = _ffn((x,), mod1, gain(1, 2), *ffn2, 1, 6, split_out=True)

    return (y_prompt.reshape(BATCH, SEQ, D_MODEL), y_sample.reshape(DEC_BATCH, DEC_SEQ, D_MODEL),
            _cache_layout(new_k), _cache_layout(new_v),
            new_hf, new_hb)
```
